```python
import jax, jax.numpy as jnp
from jax import lax
import numpy as np

D_MODEL = 1024
BATCH = 16
SEQ = 2048
DEPTH = 2
DEC_BATCH = 32
DEC_SEQ = 32
PAST_LEN = 1024

CHUNK = 64
N_MIXERS = 2
N_A_LAYERS = (DEPTH + 1) // 2
N_B_LAYERS = DEPTH // 2
A_HEADS = 16
A_KV_HEADS = 4
A_GROUP = A_HEADS // A_KV_HEADS
A_HEAD_DIM = 64
WINDOW = 128
WIN_CHUNKS = WINDOW // CHUNK
B_HEADS = 8
B_HEAD_DIM = 64
B_Q_BLOCK = 128
N_GROUPS = 4
EXPERTS_PER_GROUP = 8
TOP_K_IN_GROUP = 2
D_EXPERT = 256
NORM_EPS = 1e-6
SUBLN_EPS = 1e-5

kernel_name = "hybrid_swa_sink_diffattn_hmoe_stream_step"


def rms_norm(x, g, eps=NORM_EPS):
    xf = x.astype(jnp.float32)
    y = xf * lax.rsqrt(jnp.mean(xf * xf, axis=-1, keepdims=True) + eps)
    return (y * g.astype(jnp.float32)).astype(x.dtype)


def alibi_slopes(n):
    return jnp.asarray(2.0 ** (-8.0 * np.arange(1, n + 1) / n), dtype=jnp.float32)


def _a_project(h, w_qkv, b_qkv):
    b_, t = h.shape[:2]
    qkv = h @ w_qkv + b_qkv
    nq = A_HEADS * A_HEAD_DIM
    nk = A_KV_HEADS * A_HEAD_DIM
    q = qkv[..., :nq].reshape(b_, t, A_KV_HEADS, A_GROUP, A_HEAD_DIM)
    k = qkv[..., nq:nq + nk].reshape(b_, t, A_KV_HEADS, A_HEAD_DIM)
    v = qkv[..., nq + nk:].reshape(b_, t, A_KV_HEADS, A_HEAD_DIM)
    return q, k, v


def sink_attend(q, k, v, bias, valid, sinks):
    s = jnp.einsum('...qhgd,...shd->...hgqs', q, k).astype(jnp.float32) * (A_HEAD_DIM ** -0.5) + bias
    if valid is not None:
        s = jnp.where(valid, s, -jnp.inf)
    sk = sinks.astype(jnp.float32)[:, :, None, None]
    m = jnp.maximum(jnp.max(s, axis=-1, keepdims=True), sk)
    e = jnp.exp(s - m)
    p = e / (jnp.sum(e, axis=-1, keepdims=True) + jnp.exp(sk - m))
    return jnp.einsum('...hgqs,...shd->...qhgd', p.astype(v.dtype), v)


def _a_bias(qpos, kpos):
    dist = jnp.abs(qpos[:, None] - kpos[None, :]).astype(jnp.float32)
    slopes = alibi_slopes(A_HEADS).reshape(A_KV_HEADS, A_GROUP)
    return -slopes[:, :, None, None] * dist


def mixer_a_prompt(h, w_qkv, b_qkv, sinks, w_o):
    b_, t, _ = h.shape
    nc = t // CHUNK
    band = CHUNK * (WIN_CHUNKS + 1)
    q, k, v = _a_project(h, w_qkv, b_qkv)
    qc = q.reshape(b_, nc, CHUNK, A_KV_HEADS, A_GROUP, A_HEAD_DIM)
    pad = ((0, 0), (WIN_CHUNKS, 0), (0, 0), (0, 0), (0, 0))
    kp = jnp.pad(k.reshape(b_, nc, CHUNK, A_KV_HEADS, A_HEAD_DIM), pad)
    vp = jnp.pad(v.reshape(b_, nc, CHUNK, A_KV_HEADS, A_HEAD_DIM), pad)
    kb = jnp.concatenate([kp[:, j:j + nc] for j in range(WIN_CHUNKS + 1)], axis=2)
    vb = jnp.concatenate([vp[:, j:j + nc] for j in range(WIN_CHUNKS + 1)], axis=2)
    qi = jnp.arange(CHUNK)
    kj = jnp.arange(band) - WIN_CHUNKS * CHUNK
    bias = _a_bias(qi, kj)
    kpos = jnp.arange(nc)[:, None] * CHUNK + kj[None, :]
    valid = (kpos >= 0)[None, :, None, None, None, :]
    o = sink_attend(qc, kb, vb, bias, valid, sinks.reshape(A_KV_HEADS, A_GROUP))
    y = o.reshape(b_, t, A_HEADS * A_HEAD_DIM) @ w_o
    keep = min(WINDOW, t)
    return y, k[:, t - keep:], v[:, t - keep:]


def mixer_a_sample(h, ck, cv, w_qkv, b_qkv, sinks, w_o):
    b_, t, _ = h.shape
    l = ck.shape[1]
    q, k, v = _a_project(h, w_qkv, b_qkv)
    kk = jnp.concatenate([ck.astype(k.dtype), k], axis=1)
    vv = jnp.concatenate([cv.astype(v.dtype), v], axis=1)
    kpos = jnp.concatenate([jnp.arange(l) - l, jnp.arange(t)])
    bias = _a_bias(jnp.arange(t), kpos)
    o = sink_attend(q, kk, vv, bias, None, sinks.reshape(A_KV_HEADS, A_GROUP))
    y = o.reshape(b_, t, A_HEADS * A_HEAD_DIM) @ w_o
    return y, kk[:, t:], vv[:, t:]


def _b_project(h, w_qkv):
    b_, t = h.shape[:2]
    qkv = h @ w_qkv
    w = B_HEADS * 2 * B_HEAD_DIM
    q = qkv[..., :w].reshape(b_, t, B_HEADS, 2, B_HEAD_DIM)
    k = qkv[..., w:2 * w].reshape(b_, t, B_HEADS, 2, B_HEAD_DIM)
    v = qkv[..., 2 * w:].reshape(b_, t, B_HEADS, 2 * B_HEAD_DIM)
    return q, k, v


def _b_lambda(lam_p, lam_init):
    lp = lam_p.astype(jnp.float32)
    return jnp.exp(jnp.sum(lp[0] * lp[1])) - jnp.exp(jnp.sum(lp[2] * lp[3])) + lam_init


def diff_attend(q, k, v, bias, valid, lam):
    s = jnp.einsum('bqhcd,bshcd->bhcqs', q, k).astype(jnp.float32) * (B_HEAD_DIM ** -0.5) + bias[:, None]
    if valid is not None:
        s = jnp.where(valid, s, -jnp.inf)
    p = jax.nn.softmax(s, axis=-1)
    a = p[:, :, 0] - lam * p[:, :, 1]
    return jnp.einsum('bhqs,bshe->bqhe', a.astype(v.dtype), v)


def _b_out(o, subln, lam_init, w_o):
    b_, t = o.shape[:2]
    o = rms_norm(o, subln, SUBLN_EPS) * (1.0 - lam_init)
    return o.reshape(b_, t, B_HEADS * 2 * B_HEAD_DIM) @ w_o


def mixer_b_prompt(h, w_qkv, lam_p, subln, w_o, lam_init):
    b_, t, _ = h.shape
    q, k, v = _b_project(h, w_qkv)
    lam = _b_lambda(lam_p, lam_init)
    slopes = alibi_slopes(B_HEADS)
    nb = t // B_Q_BLOCK
    qb = jnp.moveaxis(q.reshape(b_, nb, B_Q_BLOCK, B_HEADS, 2, B_HEAD_DIM), 1, 0)
    kpos = jnp.arange(t)

    def block(args):
        qblk, bi = args
        qpos = bi * B_Q_BLOCK + jnp.arange(B_Q_BLOCK)
        dist = jnp.abs(qpos[:, None] - kpos[None, :]).astype(jnp.float32)
        bias = -slopes[:, None, None] * dist
        valid = (kpos[None, :] // CHUNK) <= (qpos[:, None] // CHUNK)
        return diff_attend(qblk, k, v, bias, valid, lam)

    o = lax.map(block, (qb, jnp.arange(nb)))
    o = jnp.moveaxis(o, 0, 1).reshape(b_, t, B_HEADS, 2 * B_HEAD_DIM)
    y = _b_out(o, subln, lam_init, w_o)
    return y, k.reshape(b_, t, B_HEADS, 2 * B_HEAD_DIM), v


def mixer_b_sample(h, ck, cv, w_qkv, lam_p, subln, w_o, lam_init):
    b_, t, _ = h.shape
    p = ck.shape[1]
    q, k, v = _b_project(h, w_qkv)
    lam = _b_lambda(lam_p, lam_init)
    kk = jnp.concatenate([ck.astype(k.dtype).reshape(b_, p, B_HEADS, 2, B_HEAD_DIM), k], axis=1)
    vv = jnp.concatenate([cv.astype(v.dtype), v], axis=1)
    kpos = jnp.arange(p + t)
    qpos = p + jnp.arange(t)
    dist = jnp.abs(qpos[:, None] - kpos[None, :]).astype(jnp.float32)
    bias = -alibi_slopes(B_HEADS)[:, None, None] * dist
    o = diff_attend(q, kk, vv, bias, None, lam)
    y = _b_out(o, subln, lam_init, w_o)
    return y, k.reshape(b_, t, B_HEADS, 2 * B_HEAD_DIM), v


def hier_moe(h, w_rg, b_rg, w_re, b_re, w_gate, w_up, w_down):
    b_, t, d = h.shape
    x = h.reshape(-1, d)
    pg = jax.nn.softmax((x @ w_rg).astype(jnp.float32) + b_rg.astype(jnp.float32), axis=-1)
    g_idx = jnp.argmax(pg, axis=-1)
    g_prob = jnp.max(pg, axis=-1)
    g_hot = jax.nn.one_hot(g_idx, N_GROUPS, dtype=jnp.float32)
    el = ((x @ w_re).astype(jnp.float32) + b_re.astype(jnp.float32)).reshape(-1, N_GROUPS, EXPERTS_PER_GROUP)
    el_sel = jnp.einsum('nge,ng->ne', el, g_hot)
    pe = jax.nn.softmax(el_sel, axis=-1)
    top_v, top_i = lax.top_k(pe, TOP_K_IN_GROUP)
    top_v = top_v / jnp.sum(top_v, axis=-1, keepdims=True)
    gate_e = jnp.sum(jax.nn.one_hot(top_i, EXPERTS_PER_GROUP, dtype=jnp.float32) * top_v[..., None], axis=1)
    gate = g_hot[:, :, None] * (g_prob[:, None] * gate_e)[:, None, :]
    y = jnp.zeros_like(x)
    for g in range(N_GROUPS):
        a = jnp.einsum('nd,edf->nef', x, w_gate[g])
        u = jnp.einsum('nd,edf->nef', x, w_up[g])
        hh = jax.nn.silu(a) * u * gate[:, g, :, None].astype(x.dtype)
        y = y + jnp.einsum('nef,efd->nd', hh, w_down[g])
    return y.reshape(b_, t, d)


def setup_inputs(seed: int = 0) -> dict:
    key = jax.random.key(seed)
    ks = iter(jax.random.split(key, 32))
    f32 = jnp.float32

    def nrm(shape, scale):
        return jax.random.normal(next(ks), shape, f32) * scale

    a_w = (A_HEADS + 2 * A_KV_HEADS) * A_HEAD_DIM
    b_w = B_HEADS * 2 * B_HEAD_DIM
    a_len = min(WINDOW, PAST_LEN)
    G, E, F, D = N_GROUPS, EXPERTS_PER_GROUP, D_EXPERT, D_MODEL
    return {
        "x_prompt": nrm((BATCH, SEQ, D), 1.0),
        "x_sample": nrm((DEC_BATCH, DEC_SEQ, D), 1.0),
        "cache_a_k": nrm((N_A_LAYERS, DEC_BATCH, a_len, A_KV_HEADS, A_HEAD_DIM), 1.0),
        "cache_a_v": nrm((N_A_LAYERS, DEC_BATCH, a_len, A_KV_HEADS, A_HEAD_DIM), 1.0),
        "cache_b_k": nrm((N_B_LAYERS, DEC_BATCH, PAST_LEN, B_HEADS, 2 * B_HEAD_DIM), 1.0),
        "cache_b_v": nrm((N_B_LAYERS, DEC_BATCH, PAST_LEN, B_HEADS, 2 * B_HEAD_DIM), 1.0),
        "norm_mix": 1.0 + nrm((DEPTH, D), 0.01),
        "norm_ffn": 1.0 + nrm((DEPTH, D), 0.01),
        "norm_final": 1.0 + nrm((D,), 0.01),
        "w_a_qkv": nrm((N_A_LAYERS, D, a_w), D ** -0.5),
        "b_a_qkv": nrm((N_A_LAYERS, a_w), 0.02),
        "a_sinks": nrm((N_A_LAYERS, A_HEADS), 0.5),
        "w_a_o": nrm((N_A_LAYERS, A_HEADS * A_HEAD_DIM, D), (A_HEADS * A_HEAD_DIM) ** -0.5),
        "w_b_qkv": nrm((N_B_LAYERS, D, 3 * b_w), D ** -0.5),
        "b_lambda": nrm((N_B_LAYERS, 4, B_HEAD_DIM), 0.1),
        "b_subln": 1.0 + nrm((N_B_LAYERS, 2 * B_HEAD_DIM), 0.01),
        "w_b_o": nrm((N_B_LAYERS, b_w, D), b_w ** -0.5),
        "w_route_group": nrm((DEPTH, D, G), D ** -0.5),
        "b_route_group": nrm((DEPTH, G), 0.01),
        "w_route_expert": nrm((DEPTH, D, G * E), D ** -0.5),
        "b_route_expert": nrm((DEPTH, G * E), 0.01),
        "w_gate": nrm((DEPTH, G, E, D, F), D ** -0.5),
        "w_up": nrm((DEPTH, G, E, D, F), D ** -0.5),
        "w_down": nrm((DEPTH, G, E, F, D), F ** -0.5),
    }


def reference(x_prompt, x_sample, cache_a_k, cache_a_v, cache_b_k, cache_b_v,
              norm_mix, norm_ffn, norm_final,
              w_a_qkv, b_a_qkv, a_sinks, w_a_o,
              w_b_qkv, b_lambda, b_subln, w_b_o,
              w_route_group, b_route_group, w_route_expert, b_route_expert,
              w_gate, w_up, w_down):
    hp, hs = x_prompt, x_sample
    akp, avp, aks, avs = [], [], [], []
    bkp, bvp, bks, bvs = [], [], [], []
    for i in range(DEPTH):
        hp_n = rms_norm(hp, norm_mix[i])
        hs_n = rms_norm(hs, norm_mix[i])
        j = i // N_MIXERS
        if i % N_MIXERS == 0:
            yp, kp_, vp_ = mixer_a_prompt(hp_n, w_a_qkv[j], b_a_qkv[j], a_sinks[j], w_a_o[j])
            ys, ks_, vs_ = mixer_a_sample(hs_n, cache_a_k[j], cache_a_v[j], w_a_qkv[j], b_a_qkv[j], a_sinks[j], w_a_o[j])
            akp.append(kp_); avp.append(vp_); aks.append(ks_); avs.append(vs_)
        else:
            lam_init = 0.8 - 0.6 * float(np.exp(-0.3 * i))
            yp, kp_, vp_ = mixer_b_prompt(hp_n, w_b_qkv[j], b_lambda[j], b_subln[j], w_b_o[j], lam_init)
            ys, ks_, vs_ = mixer_b_sample(hs_n, cache_b_k[j], cache_b_v[j], w_b_qkv[j], b_lambda[j], b_subln[j], w_b_o[j], lam_init)
            bkp.append(kp_); bvp.append(vp_); bks.append(ks_); bvs.append(vs_)
        hp = hp + yp
        hs = hs + ys
        hp = hp + hier_moe(rms_norm(hp, norm_ffn[i]), w_route_group[i], b_route_group[i], w_route_expert[i], b_route_expert[i], w_gate[i], w_up[i], w_down[i])
        hs = hs + hier_moe(rms_norm(hs, norm_ffn[i]), w_route_group[i], b_route_group[i], w_route_expert[i], b_route_expert[i], w_gate[i], w_up[i], w_down[i])
    y_prompt = rms_norm(hp, norm_final)
    y_sample = rms_norm(hs, norm_final)
    a_k_prompt = jnp.stack(akp)
    a_v_prompt = jnp.stack(avp)
    a_k_sample = jnp.stack(aks)
    a_v_sample = jnp.stack(avs)
    b_k_prompt = jnp.stack(bkp)
    b_v_prompt = jnp.stack(bvp)
    b_k_sample = jnp.stack(bks)
    b_v_sample = jnp.stack(bvs)
    return (y_prompt, y_sample, a_k_prompt, a_v_prompt, a_k_sample, a_v_sample, b_k_prompt, b_v_prompt, b_k_sample, b_v_sample)
```

```python
import functools

import jax
import jax.numpy as jnp
import numpy as np
from jax import lax
from jax.experimental import pallas as pl
from jax.experimental.pallas import tpu as pltpu

F32 = jnp.float32
BF16 = jnp.bfloat16
NEG_INF = float("-inf")

CHUNK = 64
WIN_CHUNKS = 2
A_HEADS, A_KV_HEADS, A_GROUP, A_HD = 16, 4, 4, 64
B_HEADS, B_HD = 8, 64
N_GROUPS, EXPERTS_PER_GROUP = 4, 8
N_EXPERTS = N_GROUPS * EXPERTS_PER_GROUP
NORM_EPS = 1e-6
SUBLN_EPS = 1e-5

LANES = 128
COL_CHUNK = 512
VMEM_LIMIT = 48 * 1024 * 1024


def _cparams(sem):
    return pltpu.CompilerParams(dimension_semantics=sem, vmem_limit_bytes=VMEM_LIMIT)


def _alibi(n):
    return 2.0 ** (-8.0 * np.arange(1, n + 1) / n)


def _norm_proj_kernel(n_prompt_blocks, two_src, has_bias, outs, *refs):
    refs = list(refs)
    xa_ref = refs.pop(0)
    xb_ref = refs.pop(0) if two_src else None
    g_ref = refs.pop(0)
    w_ref = refs.pop(0)
    b_ref = refs.pop(0) if has_bias else None
    out_refs = refs
    x = xa_ref[...]
    if two_src:
        x = jnp.where(pl.program_id(0) < n_prompt_blocks, x, xb_ref[...])
    ms = jnp.mean(x * x, axis=-1, keepdims=True)
    xn = (x * lax.rsqrt(ms + NORM_EPS) * g_ref[...]).astype(BF16)
    is_prompt = pl.program_id(0) < n_prompt_blocks
    out_refs = list(out_refs)
    targets = []
    for c0, c1, layout in outs:
        targets.append((c0, c1, layout, out_refs.pop(0), out_refs.pop(0) if layout == "split" else None))
    for s in range(0, w_ref.shape[1], COL_CHUNK):
        e = s + COL_CHUNK
        r = jnp.dot(xn, w_ref[:, s:e], preferred_element_type=F32)
        if has_bias:
            r = r + b_ref[:, s:e]
        for c0, c1, layout, o_ref, o2_ref in targets:
            if not (c0 <= s and e <= c1):
                continue
            rr = r.astype(o_ref.dtype)
            if layout == "flat":
                o_ref[:, s - c0:e - c0] = rr
            elif layout == "split":

                @pl.when(is_prompt)
                def _(rr=rr, o_ref=o_ref, s=s, e=e, c0=c0):
                    o_ref[:, s - c0:e - c0] = rr

                @pl.when(jnp.logical_not(is_prompt))
                def _(rr=rr, o2_ref=o2_ref, s=s, e=e, c0=c0):
                    o2_ref[:, s - c0:e - c0] = rr
            else:
                for hh in range((e - s) // LANES):
                    o_ref[(s - c0) // LANES + hh] = rr[:, hh * LANES:(hh + 1) * LANES]


def norm_proj(xa, xb, g, w, b, outs, tr, n_prompt):
    na, d = xa.shape
    nb = 0 if xb is None else xb.shape[0]
    n = na + nb
    npb = n_prompt // tr
    two = xb is not None
    in_specs = [pl.BlockSpec((tr, d), (lambda i: (jnp.minimum(i, npb - 1), 0)) if two else (lambda i: (i, 0)))]
    args = [xa]
    if two:
        in_specs.append(pl.BlockSpec((tr, d), lambda i: (jnp.maximum(i - npb, 0), 0)))
        args.append(xb)
    in_specs.append(pl.BlockSpec((1, d), lambda i: (0, 0)))
    args.append(g.reshape(1, d).astype(F32))
    in_specs.append(pl.BlockSpec(w.shape, lambda i: (0, 0)))
    args.append(w)
    if b is not None:
        in_specs.append(pl.BlockSpec((1, w.shape[1]), lambda i: (0, 0)))
        args.append(b.reshape(1, -1).astype(F32))
    out_shapes, out_specs = [], []
    for c0, c1, layout, dt in outs:
        assert c0 % COL_CHUNK == 0 and c1 % COL_CHUNK == 0
        if layout == "flat":
            out_shapes.append(jax.ShapeDtypeStruct((n, c1 - c0), dt))
            out_specs.append(pl.BlockSpec((tr, c1 - c0), lambda i: (i, 0)))
        elif layout == "split":
            out_shapes.append(jax.ShapeDtypeStruct((n_prompt, c1 - c0), dt))
            out_specs.append(pl.BlockSpec((tr, c1 - c0), lambda i: (jnp.minimum(i, npb - 1), 0)))
            out_shapes.append(jax.ShapeDtypeStruct((n - n_prompt, c1 - c0), dt))
            out_specs.append(pl.BlockSpec((tr, c1 - c0), lambda i: (jnp.maximum(i - npb, 0), 0)))
        else:
            nh = (c1 - c0) // LANES
            out_shapes.append(jax.ShapeDtypeStruct((nh, n, LANES), dt))
            out_specs.append(pl.BlockSpec((nh, tr, LANES), lambda i: (0, i, 0)))
    kern = functools.partial(_norm_proj_kernel, npb, two, b is not None, [(c0, c1, lay) for c0, c1, lay, _ in outs])
    return pl.pallas_call(
        kern, grid=(n // tr,), in_specs=in_specs, out_specs=out_specs, out_shape=out_shapes,
        compiler_params=_cparams(("arbitrary",)), name="norm_proj")(*args)


def _attn_a_chunk(q, k, v, bias_ref, sink_ref, valid):
    nq = q.shape[0]
    outs = []
    for kh in range(A_KV_HEADS):
        qs = jnp.concatenate(
            [q[:, (kh * A_GROUP + g) * A_HD:(kh * A_GROUP + g + 1) * A_HD] for g in range(A_GROUP)], axis=0)
        qs = qs * jnp.asarray(A_HD ** -0.5, BF16)
        kk = k[:, kh * A_HD:(kh + 1) * A_HD]
        s = lax.dot_general(qs, kk, (((1,), (1,)), ((), ())), preferred_element_type=F32) + bias_ref[kh]
        if valid is not None:
            s = jnp.where(valid, s, NEG_INF)
        sk = sink_ref[kh]
        m = jnp.maximum(jnp.max(s, axis=-1, keepdims=True), sk)
        e = jnp.exp(s - m)
        den = jnp.sum(e, axis=-1, keepdims=True) + jnp.exp(sk - m)
        p = (e * (1.0 / den)).astype(BF16)
        o = jnp.dot(p, v[:, kh * A_HD:(kh + 1) * A_HD], preferred_element_type=F32)
        for g in range(A_GROUP):
            outs.append(o[g * nq:(g + 1) * nq])
    return jnp.concatenate(outs, axis=1)


def _attn_a_prompt_kernel(qb, q_ref, kvc_ref, kvp_ref, bias_ref, sink_ref, o_ref):
    cb = pl.program_id(1)
    band = CHUNK * (WIN_CHUNKS + 1)
    back = CHUNK * WIN_CHUNKS
    kvfull = jnp.concatenate([kvp_ref[...], kvc_ref[...]], axis=0).astype(BF16)
    nkv = A_KV_HEADS * A_HD
    for c in range(qb // CHUNK):
        q = q_ref[c * CHUNK:(c + 1) * CHUNK, :]
        kv = kvfull[c * CHUNK:c * CHUNK + band]
        first_key = cb * qb + c * CHUNK - back
        valid = (lax.broadcasted_iota(jnp.int32, (1, band), 1) + first_key) >= 0
        o = _attn_a_chunk(q, kv[:, :nkv], kv[:, nkv:], bias_ref, sink_ref, valid)
        o_ref[c * CHUNK:(c + 1) * CHUNK, :] = o.astype(o_ref.dtype)


def _a_tables(qpos, kpos, sinks):
    slopes = _alibi(A_HEADS).reshape(A_KV_HEADS, A_GROUP)
    dist = np.abs(qpos[:, None] - kpos[None, :]).astype(np.float64)
    bias = -slopes[:, :, None, None] * dist
    nq = len(qpos)
    bias = jnp.asarray(bias.reshape(A_KV_HEADS, A_GROUP * nq, len(kpos)), F32)
    sk = jnp.repeat(sinks.astype(F32).reshape(A_KV_HEADS, A_GROUP), nq, axis=1)[..., None]
    return bias, sk


def attn_a_prompt(q_all, kv_all, sinks, batch, t, qb):
    nq = A_HEADS * A_HD
    nkv2 = 2 * A_KV_HEADS * A_HD
    band = CHUNK * (WIN_CHUNKS + 1)
    back = CHUNK * WIN_CHUNKS
    bias, sk = _a_tables(np.arange(CHUNK), np.arange(band) - back, sinks)
    nblk = t // qb
    r = qb // back
    return pl.pallas_call(
        functools.partial(_attn_a_prompt_kernel, qb),
        grid=(batch, nblk),
        in_specs=[
            pl.BlockSpec((qb, nq), lambda b, i: (b * nblk + i, 0)),
            pl.BlockSpec((qb, nkv2), lambda b, i: (b * nblk + i, 0)),
            pl.BlockSpec((back, nkv2), lambda b, i: (jnp.maximum((b * nblk + i) * r - 1, 0), 0)),
            pl.BlockSpec(bias.shape, lambda b, i: (0, 0, 0)),
            pl.BlockSpec(sk.shape, lambda b, i: (0, 0, 0)),
        ],
        out_specs=pl.BlockSpec((qb, nq), lambda b, i: (b * nblk + i, 0)),
        out_shape=jax.ShapeDtypeStruct((batch * t, nq), BF16),
        compiler_params=_cparams(("arbitrary", "arbitrary")), name="attn_a_prompt",
    )(q_all, kv_all, kv_all, bias, sk)


def _attn_a_sample_kernel(q_ref, k_ref, v_ref, bias_ref, sink_ref, o_ref):
    o = _attn_a_chunk(q_ref[...], k_ref[...].astype(BF16), v_ref[...].astype(BF16), bias_ref, sink_ref, None)
    o_ref[...] = o.astype(o_ref.dtype)


def attn_a_sample(q_all, kband, vband, sinks, row0, dec_batch, dec_t):
    nq = A_HEADS * A_HD
    s = kband.shape[1]
    past = s - dec_t
    kpos = np.concatenate([np.arange(past) - past, np.arange(dec_t)])
    bias, sk = _a_tables(np.arange(dec_t), kpos, sinks)
    blk0 = row0 // dec_t
    return pl.pallas_call(
        _attn_a_sample_kernel,
        grid=(dec_batch,),
        in_specs=[
            pl.BlockSpec((dec_t, nq), lambda b: (blk0 + b, 0)),
            pl.BlockSpec((None, s, kband.shape[2]), lambda b: (b, 0, 0)),
            pl.BlockSpec((None, s, vband.shape[2]), lambda b: (b, 0, 0)),
            pl.BlockSpec(bias.shape, lambda b: (0, 0, 0)),
            pl.BlockSpec(sk.shape, lambda b: (0, 0, 0)),
        ],
        out_specs=pl.BlockSpec((dec_t, nq), lambda b: (b, 0)),
        out_shape=jax.ShapeDtypeStruct((dec_batch * dec_t, nq), BF16),
        compiler_params=_cparams(("arbitrary",)), name="attn_a_sample",
    )(q_all, kband, vband, bias, sk)


def _stack_maps(qh):
    qh = qh * jnp.asarray(B_HD ** -0.5, BF16)
    lane = lax.broadcasted_iota(jnp.int32, qh.shape, 1)
    zero = jnp.zeros_like(qh)
    return jnp.concatenate([jnp.where(lane < B_HD, qh, zero), jnp.where(lane >= B_HD, qh, zero)], axis=0)


def _diff_finish(acc, l, nq, lam, lam_init, subln):
    inv = 1.0 / l
    o = acc[:nq] * inv[:nq] - lam * (acc[nq:] * inv[nq:])
    ms = jnp.mean(o * o, axis=-1, keepdims=True)
    return o * lax.rsqrt(ms + SUBLN_EPS) * subln * (1.0 - lam_init)


def _attn_b_prompt_kernel(qb, lam_init, slope_ref, lam_ref, q_ref, k_ref, v_ref, tri_ref, msk_ref, subln_ref, o_ref):
    i = pl.program_id(1)
    lam = lam_ref[0]
    subln = subln_ref[...]
    col = lax.broadcasted_iota(jnp.int32, (1, qb), 1)

    def head_body(h, carry):
        slope = slope_ref[h]
        qq = _stack_maps(q_ref[h])

        def kv_step(j, c, diag):
            m, l, acc = c
            start = pl.multiple_of(j * qb, qb)
            kb = k_ref[h, pl.ds(start, qb), :]
            vb = v_ref[h, pl.ds(start, qb), :]
            s = lax.dot_general(qq, kb, (((1,), (1,)), ((), ())), preferred_element_type=F32)
            s = s + slope * (col + (j - i) * qb).astype(F32)
            if diag:
                s = s + slope * tri_ref[...] + msk_ref[...]
            m_new = jnp.maximum(m, jnp.max(s, axis=-1, keepdims=True))
            alpha = jnp.exp(m - m_new)
            p = jnp.exp(s - m_new)
            l = alpha * l + jnp.sum(p, axis=-1, keepdims=True)
            acc = alpha * acc + jnp.dot(p.astype(BF16), vb, preferred_element_type=F32)
            return m_new, l, acc

        init = (jnp.full((2 * qb, 1), NEG_INF, F32), jnp.zeros((2 * qb, 1), F32), jnp.zeros((2 * qb, LANES), F32))
        c = lax.fori_loop(0, i, lambda j, c: kv_step(j, c, False), init)
        _, l, acc = kv_step(i, c, True)
        o_ref[h] = _diff_finish(acc, l, qb, lam, lam_init, subln).astype(o_ref.dtype)
        return carry

    lax.fori_loop(0, B_HEADS, head_body, 0)


def attn_b_prompt(qkvh, lam, subln, lam_init, batch, t, qb):
    n = qkvh.shape[1]
    nblk = t // qb
    qq = np.arange(qb)
    tri = -2.0 * np.maximum(qq[None, :] - qq[:, None], 0).astype(np.float32)
    msk = np.where((qq[None, :] // CHUNK) <= (qq[:, None] // CHUNK), 0.0, -np.inf).astype(np.float32)
    tri = jnp.asarray(np.concatenate([tri, tri], axis=0))
    msk = jnp.asarray(np.concatenate([msk, msk], axis=0))
    slopes = jnp.asarray(_alibi(B_HEADS), F32)
    grid_spec = pltpu.PrefetchScalarGridSpec(
        num_scalar_prefetch=2, grid=(batch, nblk),
        in_specs=[
            pl.BlockSpec((B_HEADS, qb, LANES), lambda b, i, *_: (0, b * nblk + i, 0)),
            pl.BlockSpec((B_HEADS, t, LANES), lambda b, i, *_: (1, b, 0)),
            pl.BlockSpec((B_HEADS, t, LANES), lambda b, i, *_: (2, b, 0)),
            pl.BlockSpec(tri.shape, lambda b, i, *_: (0, 0)),
            pl.BlockSpec(msk.shape, lambda b, i, *_: (0, 0)),
            pl.BlockSpec((1, LANES), lambda b, i, *_: (0, 0)),
        ],
        out_specs=pl.BlockSpec((B_HEADS, qb, LANES), lambda b, i, *_: (0, b * nblk + i, 0)),
    )
    return pl.pallas_call(
        functools.partial(_attn_b_prompt_kernel, qb, lam_init), grid_spec=grid_spec,
        out_shape=jax.ShapeDtypeStruct((B_HEADS, batch * t, LANES), BF16),
        compiler_params=_cparams(("arbitrary", "arbitrary")), name="attn_b_prompt",
    )(slopes, lam.reshape(1).astype(F32), qkvh, qkvh, qkvh, tri, msk, subln.reshape(1, LANES).astype(F32))


def _attn_b_sample_kernel(dec_t, past, lam_init, slopes, lam_ref, q_ref, kn_ref, vn_ref, ck_ref, cv_ref,
                          tnew_ref, subln_ref, o_ref):
    lam = lam_ref[0]
    subln = subln_ref[...]
    colc = lax.broadcasted_iota(jnp.int32, (1, past), 1).astype(F32) - float(past)
    for h in range(B_HEADS):
        slope = float(slopes[h])
        qq = _stack_maps(q_ref[h])
        kc = ck_ref[:, h * LANES:(h + 1) * LANES].astype(BF16)
        vc = cv_ref[:, h * LANES:(h + 1) * LANES].astype(BF16)
        dn = (((1,), (1,)), ((), ()))
        s_c = lax.dot_general(qq, kc, dn, preferred_element_type=F32) + slope * colc
        s_n = lax.dot_general(qq, kn_ref[h], dn, preferred_element_type=F32) + slope * tnew_ref[...]
        m = jnp.maximum(jnp.max(s_c, axis=-1, keepdims=True), jnp.max(s_n, axis=-1, keepdims=True))
        p_c = jnp.exp(s_c - m)
        p_n = jnp.exp(s_n - m)
        l = jnp.sum(p_c, axis=-1, keepdims=True) + jnp.sum(p_n, axis=-1, keepdims=True)
        acc = (jnp.dot(p_c.astype(BF16), vc, preferred_element_type=F32)
               + jnp.dot(p_n.astype(BF16), vn_ref[h], preferred_element_type=F32))
        o_ref[h] = _diff_finish(acc, l, dec_t, lam, lam_init, subln).astype(o_ref.dtype)


def attn_b_sample(qkvh, cache_k, cache_v, lam, subln, lam_init, row0, dec_batch, dec_t):
    past = cache_k.shape[1]
    qi = np.arange(dec_t)
    tnew = (qi[:, None] - np.abs(qi[:, None] - qi[None, :])).astype(np.float32)
    tnew = jnp.asarray(np.concatenate([tnew, tnew], axis=0))
    blk0 = row0 // dec_t
    width = B_HEADS * LANES
    grid_spec = pltpu.PrefetchScalarGridSpec(
        num_scalar_prefetch=1, grid=(dec_batch,),
        in_specs=[
            pl.BlockSpec((B_HEADS, dec_t, LANES), lambda b, *_: (0, blk0 + b, 0)),
            pl.BlockSpec((B_HEADS, dec_t, LANES), lambda b, *_: (1, blk0 + b, 0)),
            pl.BlockSpec((B_HEADS, dec_t, LANES), lambda b, *_: (2, blk0 + b, 0)),
            pl.BlockSpec((None, past, width), lambda b, *_: (b, 0, 0)),
            pl.BlockSpec((None, past, width), lambda b, *_: (b, 0, 0)),
            pl.BlockSpec(tnew.shape, lambda b, *_: (0, 0)),
            pl.BlockSpec((1, LANES), lambda b, *_: (0, 0)),
        ],
        out_specs=pl.BlockSpec((B_HEADS, dec_t, LANES), lambda b, *_: (0, b, 0)),
    )
    return pl.pallas_call(
        functools.partial(_attn_b_sample_kernel, dec_t, past, lam_init, tuple(_alibi(B_HEADS))),
        grid_spec=grid_spec,
        out_shape=jax.ShapeDtypeStruct((B_HEADS, dec_batch * dec_t, LANES), BF16),
        compiler_params=_cparams(("arbitrary",)), name="attn_b_sample",
    )(lam.reshape(1).astype(F32), qkvh, qkvh, qkvh, cache_k, cache_v, tnew, subln.reshape(1, LANES).astype(F32))


def _route(logits):
    lane = lax.broadcasted_iota(jnp.int32, logits.shape, 1).astype(F32)
    big = float(LANES)
    lg = jnp.where(lane < N_GROUPS, logits, NEG_INF)
    gmax = jnp.max(lg, axis=-1, keepdims=True)
    g_idx = jnp.min(jnp.where(lg == gmax, lane, big), axis=-1, keepdims=True)
    g_prob = 1.0 / jnp.sum(jnp.exp(lg - gmax), axis=-1, keepdims=True)
    lo = N_GROUPS + EXPERTS_PER_GROUP * g_idx
    le = jnp.where((lane >= lo) & (lane < lo + EXPERTS_PER_GROUP), logits, NEG_INF)
    m1 = jnp.max(le, axis=-1, keepdims=True)
    i1 = jnp.min(jnp.where(le == m1, lane, big), axis=-1, keepdims=True)
    le2 = jnp.where(lane == i1, NEG_INF, le)
    m2 = jnp.max(le2, axis=-1, keepdims=True)
    i2 = jnp.min(jnp.where(le2 == m2, lane, big), axis=-1, keepdims=True)
    t = jnp.exp(m2 - m1)
    ga = 1.0 / (1.0 + t)
    gb = t * ga
    meta = jnp.where(lane == 0, i1 - N_GROUPS,
                     jnp.where(lane == 1, i2 - N_GROUPS,
                               jnp.where(lane == 2, g_prob * ga, jnp.where(lane == 3, g_prob * gb, 0.0))))
    return meta


def _out_router_kernel(n_prompt_blocks, heads_in, two_o, two_res, *refs):
    refs = list(refs)
    oa_ref = refs.pop(0)
    ob_ref = refs.pop(0) if two_o else None
    ra_ref = refs.pop(0)
    rb_ref = refs.pop(0) if two_res else None
    wo_ref, g_ref, wr_ref, wr2_ref, br_ref, h_ref, xn_ref, meta_ref = refs
    is_prompt = pl.program_id(0) < n_prompt_blocks

    def load_o(ref):
        if heads_in:
            return jnp.concatenate([ref[hh] for hh in range(ref.shape[0])], axis=1)
        return ref[...]

    o = load_o(oa_ref)
    if two_o:
        o = jnp.where(is_prompt, o, load_o(ob_ref))
    res = ra_ref[...]
    if two_res:
        res = jnp.where(is_prompt, res, rb_ref[...])
    h = res + jnp.dot(o, wo_ref[...], preferred_element_type=F32)
    h_ref[...] = h
    ms = jnp.mean(h * h, axis=-1, keepdims=True)
    xn = h * lax.rsqrt(ms + NORM_EPS) * g_ref[...]
    x_hi = xn.astype(BF16)
    xn_ref[...] = x_hi
    x_lo = (xn - x_hi.astype(F32)).astype(BF16)
    r = jnp.dot(x_hi, wr_ref[...], preferred_element_type=F32)
    logits = (r[:, :LANES] + r[:, LANES:]) + jnp.dot(x_lo, wr2_ref[...], preferred_element_type=F32) + br_ref[...]
    meta_ref[...] = _route(logits)


def out_router(oa, ob, ra, rb, wo, g, w_rg, b_rg, w_re, b_re, heads_in, tr):
    d = ra.shape[1]
    if heads_in:
        na = oa.shape[1]
        nb = 0 if ob is None else ob.shape[1]
    else:
        na = oa.shape[0]
        nb = 0 if ob is None else ob.shape[0]
    n = na + nb
    npb = na // tr
    two_o = ob is not None
    two_res = rb is not None
    first = lambda i: (jnp.minimum(i, npb - 1), 0)
    second = lambda i: (jnp.maximum(i - npb, 0), 0)
    plain = lambda i: (i, 0)
    in_specs, args = [], []

    def add_o(x, imap):
        if heads_in:
            in_specs.append(pl.BlockSpec((x.shape[0], tr, LANES), lambda i: (0, imap(i)[0], 0)))
        else:
            in_specs.append(pl.BlockSpec((tr, x.shape[1]), imap))
        args.append(x)

    add_o(oa, first if two_o else plain)
    if two_o:
        add_o(ob, second)
    in_specs.append(pl.BlockSpec((tr, d), first if two_res else plain))
    args.append(ra)
    if two_res:
        in_specs.append(pl.BlockSpec((tr, d), second))
        args.append(rb)
    wr = jnp.zeros((d, LANES), F32).at[:, :N_GROUPS].set(w_rg.astype(F32))
    wr = wr.at[:, N_GROUPS:N_GROUPS + N_EXPERTS].set(w_re.astype(F32))
    wr_hi = wr.astype(BF16)
    wr_lo = (wr - wr_hi.astype(F32)).astype(BF16)
    br = jnp.zeros((1, LANES), F32).at[0, :N_GROUPS].set(b_rg.astype(F32))
    br = br.at[0, N_GROUPS:N_GROUPS + N_EXPERTS].set(b_re.astype(F32))
    consts = [wo, g.reshape(1, d).astype(F32), jnp.concatenate([wr_hi, wr_lo], axis=1), wr_hi, br]
    for cst in consts:
        in_specs.append(pl.BlockSpec(cst.shape, lambda i: (0, 0)))
        args.append(cst)
    out_shape = [jax.ShapeDtypeStruct((n, d), F32), jax.ShapeDtypeStruct((n, d), BF16),
                 jax.ShapeDtypeStruct((n, LANES), F32)]
    out_specs = [pl.BlockSpec((tr, d), plain), pl.BlockSpec((tr, d), plain), pl.BlockSpec((tr, LANES), plain)]
    return pl.pallas_call(
        functools.partial(_out_router_kernel, npb, heads_in, two_o, two_res),
        grid=(n // tr,), in_specs=in_specs, out_specs=out_specs, out_shape=out_shape,
        compiler_params=_cparams(("arbitrary",)), name="out_router")(*args)


def _expert_kernel(te_ref, nt_ref, x_ref, wg_ref, wu_ref, wd_ref, o_ref):
    @pl.when(pl.program_id(0) < nt_ref[0])
    def _():
        x = x_ref[...]
        a = jnp.dot(x, wg_ref[...], preferred_element_type=F32)
        u = jnp.dot(x, wu_ref[...], preferred_element_type=F32)
        hh = (a * (1.0 / (1.0 + jnp.exp(-a))) * u).astype(BF16)
        o_ref[...] = jnp.dot(hh, wd_ref[...], preferred_element_type=F32).astype(o_ref.dtype)

    @pl.when(pl.program_id(0) >= nt_ref[0])
    def _():
        o_ref[...] = jnp.zeros_like(o_ref)


def expert_mlp(xs, tile_expert, n_tiles, wg, wu, wd, tm):
    p, d = xs.shape
    f = wg.shape[2]
    grid_spec = pltpu.PrefetchScalarGridSpec(
        num_scalar_prefetch=2, grid=(p // tm,),
        in_specs=[
            pl.BlockSpec((tm, d), lambda t, te, nt: (t, 0)),
            pl.BlockSpec((None, d, f), lambda t, te, nt: (te[t], 0, 0)),
            pl.BlockSpec((None, d, f), lambda t, te, nt: (te[t], 0, 0)),
            pl.BlockSpec((None, f, d), lambda t, te, nt: (te[t], 0, 0)),
        ],
        out_specs=pl.BlockSpec((tm, d), lambda t, te, nt: (t, 0)),
    )
    return pl.pallas_call(
        _expert_kernel, grid_spec=grid_spec, out_shape=jax.ShapeDtypeStruct((p, d), BF16),
        compiler_params=_cparams(("arbitrary",)), name="expert_mlp")(tile_expert, n_tiles, xs, wg, wu, wd)


def moe(h, xn, meta, wg, wu, wd, tm):
    n, d = xn.shape
    eid = meta[:, :2].astype(jnp.int32)
    gates = meta[:, 2:4]
    flat_e = eid.reshape(-1)
    onehot = (flat_e[:, None] == jnp.arange(N_EXPERTS, dtype=jnp.int32)[None, :]).astype(jnp.int32)
    csum = jnp.cumsum(onehot, axis=0)
    counts = csum[-1]
    rank = jnp.take_along_axis(csum, flat_e[:, None], axis=1)[:, 0] - 1
    padded = ((counts + tm - 1) // tm) * tm
    pend = jnp.cumsum(padded)
    pstart = pend - padded
    pos = pstart[flat_e] + rank
    p = 2 * n + N_EXPERTS * tm
    n_tiles_total = p // tm
    src = jnp.zeros((p,), jnp.int32).at[pos].set(jnp.arange(2 * n, dtype=jnp.int32) // 2)
    xs = jnp.take(xn, src, axis=0)
    tile_start = jnp.arange(n_tiles_total, dtype=jnp.int32) * tm
    tile_expert = jnp.minimum(jnp.searchsorted(pend, tile_start, side="right"), N_EXPERTS - 1).astype(jnp.int32)
    n_tiles = (pend[-1] // tm).astype(jnp.int32).reshape(1)
    o = expert_mlp(xs, tile_expert, n_tiles, wg, wu, wd, tm)
    og = jnp.take(o, pos, axis=0).reshape(n, 2, d).astype(F32)
    return h + og[:, 0] * gates[:, 0:1] + og[:, 1] * gates[:, 1:2]


def _final_norm_kernel(n_prompt_blocks, h_ref, g_ref, yp_ref, ys_ref):
    x = h_ref[...]
    ms = jnp.mean(x * x, axis=-1, keepdims=True)
    y = x * lax.rsqrt(ms + NORM_EPS) * g_ref[...]
    i = pl.program_id(0)

    @pl.when(i < n_prompt_blocks)
    def _():
        yp_ref[...] = y

    @pl.when(i >= n_prompt_blocks)
    def _():
        ys_ref[...] = y


def final_norm(h, g, n_prompt, tr):
    n, d = h.shape
    npb = n_prompt // tr
    return pl.pallas_call(
        functools.partial(_final_norm_kernel, npb),
        grid=(n // tr,),
        in_specs=[pl.BlockSpec((tr, d), lambda i: (i, 0)), pl.BlockSpec((1, d), lambda i: (0, 0))],
        out_specs=[pl.BlockSpec((tr, d), lambda i: (jnp.minimum(i, npb - 1), 0)),
                   pl.BlockSpec((tr, d), lambda i: (jnp.maximum(i - npb, 0), 0))],
        out_shape=[jax.ShapeDtypeStruct((n_prompt, d), F32), jax.ShapeDtypeStruct((n - n_prompt, d), F32)],
        compiler_params=_cparams(("arbitrary",)), name="final_norm")(h, g.reshape(1, d).astype(F32))


def _pick_rows_tile(n_prompt, n_sample):
    for tr in (512, 256, 128, 64, 32, 16, 8):
        if n_prompt % tr == 0 and n_sample % tr == 0:
            return tr
    raise ValueError("row counts must be multiples of 8")


def kernel(x_prompt, x_sample, cache_a_k, cache_a_v, cache_b_k, cache_b_v, norm_mix, norm_ffn, norm_final,
           w_a_qkv, b_a_qkv, a_sinks, w_a_o, w_b_qkv, b_lambda, b_subln, w_b_o,
           w_route_group, b_route_group, w_route_expert, b_route_expert, w_gate, w_up, w_down):
    batch, t, d = x_prompt.shape
    dec_batch, dec_t, _ = x_sample.shape
    n_p, n_s = batch * t, dec_batch * dec_t
    tr = _pick_rows_tile(n_p, n_s)
    tm = min(512, tr * 4)
    xp = x_prompt.reshape(n_p, d)
    xs = x_sample.reshape(n_s, d)
    nq_a = A_HEADS * A_HD
    nkv_a = A_KV_HEADS * A_HD
    e_all = N_EXPERTS

    def expert_weights(i):
        f = w_gate.shape[-1]
        return (w_gate[i].reshape(e_all, d, f).astype(BF16), w_up[i].reshape(e_all, d, f).astype(BF16),
                w_down[i].reshape(e_all, f, d).astype(BF16))

    q_a, kv_a = norm_proj(xp, xs, norm_mix[0], w_a_qkv[0].astype(BF16), b_a_qkv[0],
                          [(0, nq_a, "flat", BF16), (nq_a, nq_a + 2 * nkv_a, "flat", F32)], tr, n_p)
    qb_a = min(256, t)
    o_ap = attn_a_prompt(q_a, kv_a, a_sinks[0], batch, t, qb_a)
    kv_s = kv_a[n_p:].reshape(dec_batch, dec_t, 2 * nkv_a)
    past_a = cache_a_k.shape[2]
    kband = jnp.concatenate([cache_a_k[0].reshape(dec_batch, past_a, nkv_a), kv_s[..., :nkv_a]], axis=1)
    vband = jnp.concatenate([cache_a_v[0].reshape(dec_batch, past_a, nkv_a), kv_s[..., nkv_a:]], axis=1)
    o_as = attn_a_sample(q_a, kband, vband, a_sinks[0], n_p, dec_batch, dec_t)
    h, xn, meta = out_router(o_ap, o_as, xp, xs, w_a_o[0].astype(BF16), norm_ffn[0],
                             w_route_group[0], b_route_group[0], w_route_expert[0], b_route_expert[0], False, tr)
    h = moe(h, xn, meta, *expert_weights(0), tm)

    keep = min(CHUNK * WIN_CHUNKS, t)
    kv_p = kv_a[:n_p].reshape(batch, t, 2 * nkv_a)[:, t - keep:]
    a_k_prompt = kv_p[..., :nkv_a].reshape(1, batch, keep, A_KV_HEADS, A_HD)
    a_v_prompt = kv_p[..., nkv_a:].reshape(1, batch, keep, A_KV_HEADS, A_HD)
    a_k_sample = kband[:, dec_t:].reshape(1, dec_batch, past_a, A_KV_HEADS, A_HD)
    a_v_sample = vband[:, dec_t:].reshape(1, dec_batch, past_a, A_KV_HEADS, A_HD)

    lam_init = 0.8 - 0.6 * float(np.exp(-0.3 * 1))
    lp = b_lambda[0].astype(F32)
    lam = jnp.exp(jnp.sum(lp[0] * lp[1])) - jnp.exp(jnp.sum(lp[2] * lp[3])) + lam_init
    wb = B_HEADS * 2 * B_HD
    qkvh, k_bp, k_bs, v_bp, v_bs = norm_proj(
        h, None, norm_mix[1], w_b_qkv[0].astype(BF16), None,
        [(0, 3 * wb, "heads", BF16), (wb, 2 * wb, "split", F32), (2 * wb, 3 * wb, "split", F32)], tr, n_p)
    qb_b = min(256, t)
    o_bp = attn_b_prompt(qkvh, lam, b_subln[0], lam_init, batch, t, qb_b)
    past_b = cache_b_k.shape[2]
    o_bs = attn_b_sample(qkvh, cache_b_k[0].reshape(dec_batch, past_b, wb), cache_b_v[0].reshape(dec_batch, past_b, wb),
                         lam, b_subln[0], lam_init, n_p, dec_batch, dec_t)
    h, xn, meta = out_router(o_bp, o_bs, h, None, w_b_o[0].astype(BF16), norm_ffn[1],
                             w_route_group[1], b_route_group[1], w_route_expert[1], b_route_expert[1], True, tr)
    h = moe(h, xn, meta, *expert_weights(1), tm)

    y_p, y_s = final_norm(h, norm_final, n_p, tr)
    hd2 = 2 * B_HD
    return (y_p.reshape(batch, t, d), y_s.reshape(dec_batch, dec_t, d),
            a_k_prompt, a_v_prompt, a_k_sample, a_v_sample,
            k_bp.reshape(1, batch, t, B_HEADS, hd2), v_bp.reshape(1, batch, t, B_HEADS, hd2),
            k_bs.reshape(1, dec_batch, dec_t, B_HEADS, hd2), v_bs.reshape(1, dec_batch, dec_t, B_HEADS, hd2))
```

```python
import functools

import jax
import jax.numpy as jnp
import numpy as np
from jax import lax
from jax.experimental import pallas as pl
from jax.experimental.pallas import tpu as pltpu

F32 = jnp.float32
BF16 = jnp.bfloat16
NEG_INF = float("-inf")

CHUNK = 64
WIN_CHUNKS = 2
A_HEADS, A_KV_HEADS, A_GROUP, A_HD = 16, 4, 4, 64
B_HEADS, B_HD = 8, 64
N_GROUPS, EXPERTS_PER_GROUP = 4, 8
N_EXPERTS = N_GROUPS * EXPERTS_PER_GROUP
NORM_EPS = 1e-6
SUBLN_EPS = 1e-5

LANES = 128
COL_CHUNK = 512
HEAD_GROUP = 2
VMEM_LIMIT = 48 * 1024 * 1024


def _cparams(sem):
    return pltpu.CompilerParams(dimension_semantics=sem, vmem_limit_bytes=VMEM_LIMIT)


def _alibi(n):
    return 2.0 ** (-8.0 * np.arange(1, n + 1) / n)


def _norm_proj_kernel(n_prompt_blocks, two_src, has_bias, outs, *refs):
    refs = list(refs)
    xa_ref = refs.pop(0)
    xb_ref = refs.pop(0) if two_src else None
    g_ref = refs.pop(0)
    w_ref = refs.pop(0)
    b_ref = refs.pop(0) if has_bias else None
    out_refs = refs
    x = xa_ref[...]
    if two_src:
        x = jnp.where(pl.program_id(0) < n_prompt_blocks, x, xb_ref[...])
    ms = jnp.mean(x * x, axis=-1, keepdims=True)
    xn = (x * lax.rsqrt(ms + NORM_EPS) * g_ref[...]).astype(BF16)
    is_prompt = pl.program_id(0) < n_prompt_blocks
    out_refs = list(out_refs)
    targets = []
    for c0, c1, layout in outs:
        targets.append((c0, c1, layout, out_refs.pop(0), out_refs.pop(0) if layout == "split" else None))
    for s in range(0, w_ref.shape[1], COL_CHUNK):
        e = s + COL_CHUNK
        r = jnp.dot(xn, w_ref[:, s:e], preferred_element_type=F32)
        if has_bias:
            r = r + b_ref[:, s:e]
        for c0, c1, layout, o_ref, o2_ref in targets:
            if not (c0 <= s and e <= c1):
                continue
            rr = r.astype(o_ref.dtype)
            if layout == "flat":
                o_ref[:, s - c0:e - c0] = rr
            elif layout == "split":

                @pl.when(is_prompt)
                def _(rr=rr, o_ref=o_ref, s=s, e=e, c0=c0):
                    o_ref[:, s - c0:e - c0] = rr

                @pl.when(jnp.logical_not(is_prompt))
                def _(rr=rr, o2_ref=o2_ref, s=s, e=e, c0=c0):
                    o2_ref[:, s - c0:e - c0] = rr
            elif layout == "heads":
                for hh in range((e - s) // LANES):
                    o_ref[(s - c0) // LANES + hh] = rr[:, hh * LANES:(hh + 1) * LANES]
            else:
                for hh in range((e - s) // LANES):
                    o_ref[(s - c0) // LANES + hh] = r[:, hh * LANES:(hh + 1) * LANES].T.astype(o_ref.dtype)


def norm_proj(xa, xb, g, w, b, outs, tr, n_prompt):
    na, d = xa.shape
    nb = 0 if xb is None else xb.shape[0]
    n = na + nb
    npb = n_prompt // tr
    two = xb is not None
    in_specs = [pl.BlockSpec((tr, d), (lambda i: (jnp.minimum(i, npb - 1), 0)) if two else (lambda i: (i, 0)))]
    args = [xa]
    if two:
        in_specs.append(pl.BlockSpec((tr, d), lambda i: (jnp.maximum(i - npb, 0), 0)))
        args.append(xb)
    in_specs.append(pl.BlockSpec((1, d), lambda i: (0, 0)))
    args.append(g.reshape(1, d).astype(F32))
    in_specs.append(pl.BlockSpec(w.shape, lambda i: (0, 0)))
    args.append(w)
    if b is not None:
        in_specs.append(pl.BlockSpec((1, w.shape[1]), lambda i: (0, 0)))
        args.append(b.reshape(1, -1).astype(F32))
    out_shapes, out_specs = [], []
    for c0, c1, layout, dt in outs:
        assert c0 % COL_CHUNK == 0 and c1 % COL_CHUNK == 0
        if layout == "flat":
            out_shapes.append(jax.ShapeDtypeStruct((n, c1 - c0), dt))
            out_specs.append(pl.BlockSpec((tr, c1 - c0), lambda i: (i, 0)))
        elif layout == "split":
            out_shapes.append(jax.ShapeDtypeStruct((n_prompt, c1 - c0), dt))
            out_specs.append(pl.BlockSpec((tr, c1 - c0), lambda i: (jnp.minimum(i, npb - 1), 0)))
            out_shapes.append(jax.ShapeDtypeStruct((n - n_prompt, c1 - c0), dt))
            out_specs.append(pl.BlockSpec((tr, c1 - c0), lambda i: (jnp.maximum(i - npb, 0), 0)))
        elif layout == "heads":
            nh = (c1 - c0) // LANES
            out_shapes.append(jax.ShapeDtypeStruct((nh, n, LANES), dt))
            out_specs.append(pl.BlockSpec((nh, tr, LANES), lambda i: (0, i, 0)))
        else:
            assert layout == "heads_t"
            nh = (c1 - c0) // LANES
            out_shapes.append(jax.ShapeDtypeStruct((nh, LANES, n), dt))
            out_specs.append(pl.BlockSpec((nh, LANES, tr), lambda i: (0, 0, i)))
    kern = functools.partial(_norm_proj_kernel, npb, two, b is not None, [(c0, c1, lay) for c0, c1, lay, _ in outs])
    return pl.pallas_call(
        kern, grid=(n // tr,), in_specs=in_specs, out_specs=out_specs, out_shape=out_shapes,
        compiler_params=_cparams(("arbitrary",)), name="norm_proj")(*args)


def _attn_a_chunk(q, k, v, bias_ref, sink_ref, valid):
    nq = q.shape[0]
    outs = []
    for kh in range(A_KV_HEADS):
        qs = jnp.concatenate(
            [q[:, (kh * A_GROUP + g) * A_HD:(kh * A_GROUP + g + 1) * A_HD] for g in range(A_GROUP)], axis=0)
        qs = qs * jnp.asarray(A_HD ** -0.5, BF16)
        kk = k[:, kh * A_HD:(kh + 1) * A_HD]
        s = lax.dot_general(qs, kk, (((1,), (1,)), ((), ())), preferred_element_type=F32) + bias_ref[kh]
        if valid is not None:
            s = jnp.where(valid, s, NEG_INF)
        sk = sink_ref[kh]
        m = jnp.maximum(jnp.max(s, axis=-1, keepdims=True), sk)
        e = jnp.exp(s - m)
        den = jnp.sum(e, axis=-1, keepdims=True) + jnp.exp(sk - m)
        p = (e * (1.0 / den)).astype(BF16)
        o = jnp.dot(p, v[:, kh * A_HD:(kh + 1) * A_HD], preferred_element_type=F32)
        for g in range(A_GROUP):
            outs.append(o[g * nq:(g + 1) * nq])
    return jnp.concatenate(outs, axis=1)


def _attn_a_prompt_kernel(qb, q_ref, kvc_ref, kvp_ref, bias_ref, sink_ref, o_ref):
    cb = pl.program_id(1)
    back = CHUNK * WIN_CHUNKS
    nk = back + qb
    nkv = A_KV_HEADS * A_HD
    kvfull = jnp.concatenate([kvp_ref[...], kvc_ref[...]], axis=0)
    k_all = kvfull[:, :nkv].astype(BF16)
    vt_all = kvfull[:, nkv:].T.astype(BF16)
    key_pos = lax.broadcasted_iota(jnp.int32, (nk, 1), 0) + (cb * qb - back)
    before_start = jnp.where(key_pos >= 0, 0.0, NEG_INF)
    dn = (((1,), (1,)), ((), ()))
    outs = []
    for kh in range(A_KV_HEADS):
        qs = jnp.concatenate(
            [q_ref[:, (kh * A_GROUP + g) * A_HD:(kh * A_GROUP + g + 1) * A_HD] for g in range(A_GROUP)], axis=0)
        qs = qs * jnp.asarray(A_HD ** -0.5, BF16)
        st = lax.dot_general(k_all[:, kh * A_HD:(kh + 1) * A_HD], qs, dn, preferred_element_type=F32)
        st = st + bias_ref[kh] + before_start
        sk = sink_ref[kh]
        m = jnp.maximum(jnp.max(st, axis=0, keepdims=True), sk)
        e = jnp.exp(st - m)
        den = jnp.sum(e, axis=0, keepdims=True) + jnp.exp(sk - m)
        p = (e * (1.0 / den)).astype(BF16)
        ot = jnp.dot(vt_all[kh * A_HD:(kh + 1) * A_HD, :], p, preferred_element_type=F32)
        o = ot.T
        for g in range(A_GROUP):
            outs.append(o[g * qb:(g + 1) * qb])
    o_ref[...] = jnp.concatenate(outs, axis=1).astype(o_ref.dtype)


def _a_prompt_tables(qb, sinks):
    back = CHUNK * WIN_CHUNKS
    slopes = _alibi(A_HEADS).reshape(A_KV_HEADS, A_GROUP)
    qpos = np.arange(qb)
    kpos = np.arange(back + qb) - back
    dist = np.abs(kpos[:, None] - qpos[None, :]).astype(np.float64)
    band_lo = (qpos // CHUNK) * CHUNK - back
    in_band = (kpos[:, None] >= band_lo[None, :]) & (kpos[:, None] < band_lo[None, :] + back + CHUNK)
    bias = np.where(in_band[None, None], -slopes[:, :, None, None] * dist[None, None], -np.inf)
    bias = np.transpose(bias, (0, 2, 1, 3)).reshape(A_KV_HEADS, back + qb, A_GROUP * qb)
    sk = jnp.repeat(sinks.astype(F32).reshape(A_KV_HEADS, A_GROUP), qb, axis=1)[:, None, :]
    return jnp.asarray(bias, F32), sk


def _a_tables(qpos, kpos, sinks):
    slopes = _alibi(A_HEADS).reshape(A_KV_HEADS, A_GROUP)
    dist = np.abs(qpos[:, None] - kpos[None, :]).astype(np.float64)
    bias = -slopes[:, :, None, None] * dist
    nq = len(qpos)
    bias = jnp.asarray(bias.reshape(A_KV_HEADS, A_GROUP * nq, len(kpos)), F32)
    sk = jnp.repeat(sinks.astype(F32).reshape(A_KV_HEADS, A_GROUP), nq, axis=1)[..., None]
    return bias, sk


def attn_a_prompt(q_all, kv_all, sinks, batch, t, qb):
    nq = A_HEADS * A_HD
    nkv2 = 2 * A_KV_HEADS * A_HD
    back = CHUNK * WIN_CHUNKS
    bias, sk = _a_prompt_tables(qb, sinks)
    nblk = t // qb
    r = qb // back
    return pl.pallas_call(
        functools.partial(_attn_a_prompt_kernel, qb),
        grid=(batch, nblk),
        in_specs=[
            pl.BlockSpec((qb, nq), lambda b, i: (b * nblk + i, 0)),
            pl.BlockSpec((qb, nkv2), lambda b, i: (b * nblk + i, 0)),
            pl.BlockSpec((back, nkv2), lambda b, i: (jnp.maximum((b * nblk + i) * r - 1, 0), 0)),
            pl.BlockSpec(bias.shape, lambda b, i: (0, 0, 0)),
            pl.BlockSpec(sk.shape, lambda b, i: (0, 0, 0)),
        ],
        out_specs=pl.BlockSpec((qb, nq), lambda b, i: (b * nblk + i, 0)),
        out_shape=jax.ShapeDtypeStruct((batch * t, nq), BF16),
        compiler_params=_cparams(("arbitrary", "arbitrary")), name="attn_a_prompt",
    )(q_all, kv_all, kv_all, bias, sk)


def _attn_a_sample_kernel(q_ref, k_ref, v_ref, bias_ref, sink_ref, o_ref):
    o = _attn_a_chunk(q_ref[...], k_ref[...].astype(BF16), v_ref[...].astype(BF16), bias_ref, sink_ref, None)
    o_ref[...] = o.astype(o_ref.dtype)


def attn_a_sample(q_all, kband, vband, sinks, row0, dec_batch, dec_t):
    nq = A_HEADS * A_HD
    s = kband.shape[1]
    past = s - dec_t
    kpos = np.concatenate([np.arange(past) - past, np.arange(dec_t)])
    bias, sk = _a_tables(np.arange(dec_t), kpos, sinks)
    blk0 = row0 // dec_t
    return pl.pallas_call(
        _attn_a_sample_kernel,
        grid=(dec_batch,),
        in_specs=[
            pl.BlockSpec((dec_t, nq), lambda b: (blk0 + b, 0)),
            pl.BlockSpec((None, s, kband.shape[2]), lambda b: (b, 0, 0)),
            pl.BlockSpec((None, s, vband.shape[2]), lambda b: (b, 0, 0)),
            pl.BlockSpec(bias.shape, lambda b: (0, 0, 0)),
            pl.BlockSpec(sk.shape, lambda b: (0, 0, 0)),
        ],
        out_specs=pl.BlockSpec((dec_t, nq), lambda b: (b, 0)),
        out_shape=jax.ShapeDtypeStruct((dec_batch * dec_t, nq), BF16),
        compiler_params=_cparams(("arbitrary",)), name="attn_a_sample",
    )(q_all, kband, vband, bias, sk)


def _stack_maps(qh):
    qh = qh * jnp.asarray(B_HD ** -0.5, BF16)
    lane = lax.broadcasted_iota(jnp.int32, qh.shape, 1)
    zero = jnp.zeros_like(qh)
    return jnp.concatenate([jnp.where(lane < B_HD, qh, zero), jnp.where(lane >= B_HD, qh, zero)], axis=0)


def _diff_finish(acc, l, nq, lam, lam_init, subln):
    inv = 1.0 / l
    o = acc[:nq] * inv[:nq] - lam * (acc[nq:] * inv[nq:])
    ms = jnp.mean(o * o, axis=-1, keepdims=True)
    return o * lax.rsqrt(ms + SUBLN_EPS) * subln * (1.0 - lam_init)


def _attn_b_prompt_kernel(qb, lam_init, slope_ref, lam_ref, q_ref, k_ref, vt_ref, kcol_ref, tri_ref, msk_ref,
                          subln_ref, o_ref, acc_ref):
    i = pl.program_id(1)
    lam = lam_ref[0]
    dn = (((1,), (1,)), ((), ()))

    def group_body(hg, carry):
        heads = [hg * HEAD_GROUP + u for u in range(HEAD_GROUP)]
        slopes = [slope_ref[h] for h in heads]
        qqs = [_stack_maps(q_ref[h]) for h in heads]

        def kv_step(j, c, diag):
            out = []
            for u, h in enumerate(heads):
                m, l = c[u]
                slope = slopes[u]
                start = pl.multiple_of(j * qb, qb)
                kb = k_ref[h, pl.ds(start, qb), :]
                vtb = vt_ref[h, :, pl.ds(start, qb)]
                t = lax.dot_general(kb, qqs[u], dn, preferred_element_type=F32) + slope * kcol_ref[...]
                if diag:
                    t = t + slope * tri_ref[...] + msk_ref[...]
                off = slope * jnp.full((1, 2 * qb), (j - i) * qb, jnp.int32).astype(F32)
                m_new = jnp.maximum(m, jnp.max(t, axis=0, keepdims=True) + off)
                alpha = jnp.exp(m - m_new)
                p = jnp.exp(t + (off - m_new))
                l = alpha * l + jnp.sum(p, axis=0, keepdims=True)
                acc_ref[u] = alpha * acc_ref[u] + jnp.dot(vtb, p.astype(BF16), preferred_element_type=F32)
                out.append((m_new, l))
            return tuple(out)

        acc_ref[...] = jnp.zeros_like(acc_ref)
        init = tuple((jnp.full((1, 2 * qb), NEG_INF, F32), jnp.zeros((1, 2 * qb), F32)) for _ in heads)
        c = lax.fori_loop(0, i, lambda j, c: kv_step(j, c, False), init)
        c = kv_step(i, c, True)
        for u, h in enumerate(heads):
            inv = 1.0 / c[u][1]
            acc = acc_ref[u]
            ot = acc[:, :qb] * inv[:, :qb] - lam * (acc[:, qb:] * inv[:, qb:])
            ms = jnp.mean(ot * ot, axis=0, keepdims=True)
            ot = ot * lax.rsqrt(ms + SUBLN_EPS) * subln_ref[...] * (1.0 - lam_init)
            o_ref[h] = ot.T.astype(o_ref.dtype)
        return carry

    lax.fori_loop(0, B_HEADS // HEAD_GROUP, group_body, 0)


def attn_b_prompt(qkvh, vt, lam, subln, lam_init, batch, t, qb):
    nblk = t // qb
    qq = np.arange(qb)
    kcol = np.broadcast_to(qq[:, None], (qb, 2 * qb)).astype(np.float32)
    tri = -2.0 * np.maximum(qq[:, None] - qq[None, :], 0).astype(np.float32)
    msk = np.where((qq[:, None] // CHUNK) <= (qq[None, :] // CHUNK), 0.0, -np.inf).astype(np.float32)
    tri = jnp.asarray(np.concatenate([tri, tri], axis=1))
    msk = jnp.asarray(np.concatenate([msk, msk], axis=1))
    kcol = jnp.asarray(kcol)
    slopes = jnp.asarray(_alibi(B_HEADS), F32)
    subln_t = jnp.broadcast_to(subln.astype(F32).reshape(LANES, 1), (LANES, qb))
    const = lambda b, i, *_: (0, 0)
    grid_spec = pltpu.PrefetchScalarGridSpec(
        num_scalar_prefetch=2, grid=(batch, nblk),
        in_specs=[
            pl.BlockSpec((B_HEADS, qb, LANES), lambda b, i, *_: (0, b * nblk + i, 0)),
            pl.BlockSpec((B_HEADS, t, LANES), lambda b, i, *_: (1, b, 0)),
            pl.BlockSpec((B_HEADS, LANES, t), lambda b, i, *_: (0, 0, b)),
            pl.BlockSpec(kcol.shape, const),
            pl.BlockSpec(tri.shape, const),
            pl.BlockSpec(msk.shape, const),
            pl.BlockSpec(subln_t.shape, const),
        ],
        out_specs=pl.BlockSpec((B_HEADS, qb, LANES), lambda b, i, *_: (0, b * nblk + i, 0)),
        scratch_shapes=[pltpu.VMEM((HEAD_GROUP, LANES, 2 * qb), F32)],
    )
    return pl.pallas_call(
        functools.partial(_attn_b_prompt_kernel, qb, lam_init), grid_spec=grid_spec,
        out_shape=jax.ShapeDtypeStruct((B_HEADS, batch * t, LANES), BF16),
        compiler_params=_cparams(("arbitrary", "arbitrary")), name="attn_b_prompt",
    )(slopes, lam.reshape(1).astype(F32), qkvh, qkvh, vt, kcol, tri, msk, subln_t)


def _attn_b_sample_kernel(dec_t, past, lam_init, slopes, lam_ref, q_ref, kn_ref, vn_ref, ck_ref, cv_ref,
                          tnew_ref, subln_ref, o_ref):
    lam = lam_ref[0]
    subln = subln_ref[...]
    colc = lax.broadcasted_iota(jnp.int32, (1, past), 1).astype(F32) - float(past)
    for h in range(B_HEADS):
        slope = float(slopes[h])
        qq = _stack_maps(q_ref[h])
        kc = ck_ref[:, h, :].astype(BF16)
        vc = cv_ref[:, h, :].astype(BF16)
        dn = (((1,), (1,)), ((), ()))
        s_c = lax.dot_general(qq, kc, dn, preferred_element_type=F32) + slope * colc
        s_n = lax.dot_general(qq, kn_ref[h], dn, preferred_element_type=F32) + slope * tnew_ref[...]
        m = jnp.maximum(jnp.max(s_c, axis=-1, keepdims=True), jnp.max(s_n, axis=-1, keepdims=True))
        p_c = jnp.exp(s_c - m)
        p_n = jnp.exp(s_n - m)
        l = jnp.sum(p_c, axis=-1, keepdims=True) + jnp.sum(p_n, axis=-1, keepdims=True)
        acc = (jnp.dot(p_c.astype(BF16), vc, preferred_element_type=F32)
               + jnp.dot(p_n.astype(BF16), vn_ref[h], preferred_element_type=F32))
        o_ref[h] = _diff_finish(acc, l, dec_t, lam, lam_init, subln).astype(o_ref.dtype)


def attn_b_sample(qkvh, cache_k, cache_v, lam, subln, lam_init, row0, dec_batch, dec_t):
    past = cache_k.shape[1]
    qi = np.arange(dec_t)
    tnew = (qi[:, None] - np.abs(qi[:, None] - qi[None, :])).astype(np.float32)
    tnew = jnp.asarray(np.concatenate([tnew, tnew], axis=0))
    blk0 = row0 // dec_t
    grid_spec = pltpu.PrefetchScalarGridSpec(
        num_scalar_prefetch=1, grid=(dec_batch,),
        in_specs=[
            pl.BlockSpec((B_HEADS, dec_t, LANES), lambda b, *_: (0, blk0 + b, 0)),
            pl.BlockSpec((B_HEADS, dec_t, LANES), lambda b, *_: (1, blk0 + b, 0)),
            pl.BlockSpec((B_HEADS, dec_t, LANES), lambda b, *_: (2, blk0 + b, 0)),
            pl.BlockSpec((None, past, B_HEADS, LANES), lambda b, *_: (b, 0, 0, 0)),
            pl.BlockSpec((None, past, B_HEADS, LANES), lambda b, *_: (b, 0, 0, 0)),
            pl.BlockSpec(tnew.shape, lambda b, *_: (0, 0)),
            pl.BlockSpec((1, LANES), lambda b, *_: (0, 0)),
        ],
        out_specs=pl.BlockSpec((B_HEADS, dec_t, LANES), lambda b, *_: (0, b, 0)),
    )
    return pl.pallas_call(
        functools.partial(_attn_b_sample_kernel, dec_t, past, lam_init, tuple(_alibi(B_HEADS))),
        grid_spec=grid_spec,
        out_shape=jax.ShapeDtypeStruct((B_HEADS, dec_batch * dec_t, LANES), BF16),
        compiler_params=_cparams(("arbitrary",)), name="attn_b_sample",
    )(lam.reshape(1).astype(F32), qkvh, qkvh, qkvh, cache_k, cache_v, tnew, subln.reshape(1, LANES).astype(F32))


def _route(logits):
    lane = lax.broadcasted_iota(jnp.int32, logits.shape, 1).astype(F32)
    big = float(LANES)
    lg = jnp.where(lane < N_GROUPS, logits, NEG_INF)
    gmax = jnp.max(lg, axis=-1, keepdims=True)
    g_idx = jnp.min(jnp.where(lg == gmax, lane, big), axis=-1, keepdims=True)
    g_prob = 1.0 / jnp.sum(jnp.exp(lg - gmax), axis=-1, keepdims=True)
    lo = N_GROUPS + EXPERTS_PER_GROUP * g_idx
    le = jnp.where((lane >= lo) & (lane < lo + EXPERTS_PER_GROUP), logits, NEG_INF)
    m1 = jnp.max(le, axis=-1, keepdims=True)
    i1 = jnp.min(jnp.where(le == m1, lane, big), axis=-1, keepdims=True)
    le2 = jnp.where(lane == i1, NEG_INF, le)
    m2 = jnp.max(le2, axis=-1, keepdims=True)
    i2 = jnp.min(jnp.where(le2 == m2, lane, big), axis=-1, keepdims=True)
    t = jnp.exp(m2 - m1)
    ga = 1.0 / (1.0 + t)
    gb = t * ga
    meta = jnp.where(lane == 0, i1 - N_GROUPS,
                     jnp.where(lane == 1, i2 - N_GROUPS,
                               jnp.where(lane == 2, g_prob * ga, jnp.where(lane == 3, g_prob * gb, 0.0))))
    return meta


def _out_router_kernel(n_prompt_blocks, heads_in, two_o, two_res, *refs):
    refs = list(refs)
    oa_ref = refs.pop(0)
    ob_ref = refs.pop(0) if two_o else None
    ra_ref = refs.pop(0)
    rb_ref = refs.pop(0) if two_res else None
    wo_ref, g_ref, wr_ref, wr2_ref, br_ref, h_ref, xn_ref, meta_ref = refs
    is_prompt = pl.program_id(0) < n_prompt_blocks

    def load_o(ref):
        if heads_in:
            return jnp.concatenate([ref[hh] for hh in range(ref.shape[0])], axis=1)
        return ref[...]

    o = load_o(oa_ref)
    if two_o:
        o = jnp.where(is_prompt, o, load_o(ob_ref))
    res = ra_ref[...]
    if two_res:
        res = jnp.where(is_prompt, res, rb_ref[...])
    h = res + jnp.dot(o, wo_ref[...], preferred_element_type=F32)
    h_ref[...] = h
    ms = jnp.mean(h * h, axis=-1, keepdims=True)
    xn = h * lax.rsqrt(ms + NORM_EPS) * g_ref[...]
    x_hi = xn.astype(BF16)
    xn_ref[...] = x_hi
    x_lo = (xn - x_hi.astype(F32)).astype(BF16)
    r = jnp.dot(x_hi, wr_ref[...], preferred_element_type=F32)
    logits = (r[:, :LANES] + r[:, LANES:]) + jnp.dot(x_lo, wr2_ref[...], preferred_element_type=F32) + br_ref[...]
    meta_ref[...] = _route(logits)


def out_router(oa, ob, ra, rb, wo, g, w_rg, b_rg, w_re, b_re, heads_in, tr):
    d = ra.shape[1]
    if heads_in:
        na = oa.shape[1]
        nb = 0 if ob is None else ob.shape[1]
    else:
        na = oa.shape[0]
        nb = 0 if ob is None else ob.shape[0]
    n = na + nb
    npb = na // tr
    two_o = ob is not None
    two_res = rb is not None
    first = lambda i: (jnp.minimum(i, npb - 1), 0)
    second = lambda i: (jnp.maximum(i - npb, 0), 0)
    plain = lambda i: (i, 0)
    in_specs, args = [], []

    def add_o(x, imap):
        if heads_in:
            in_specs.append(pl.BlockSpec((x.shape[0], tr, LANES), lambda i: (0, imap(i)[0], 0)))
        else:
            in_specs.append(pl.BlockSpec((tr, x.shape[1]), imap))
        args.append(x)

    add_o(oa, first if two_o else plain)
    if two_o:
        add_o(ob, second)
    in_specs.append(pl.BlockSpec((tr, d), first if two_res else plain))
    args.append(ra)
    if two_res:
        in_specs.append(pl.BlockSpec((tr, d), second))
        args.append(rb)
    wr = jnp.zeros((d, LANES), F32).at[:, :N_GROUPS].set(w_rg.astype(F32))
    wr = wr.at[:, N_GROUPS:N_GROUPS + N_EXPERTS].set(w_re.astype(F32))
    wr_hi = wr.astype(BF16)
    wr_lo = (wr - wr_hi.astype(F32)).astype(BF16)
    br = jnp.zeros((1, LANES), F32).at[0, :N_GROUPS].set(b_rg.astype(F32))
    br = br.at[0, N_GROUPS:N_GROUPS + N_EXPERTS].set(b_re.astype(F32))
    consts = [wo, g.reshape(1, d).astype(F32), jnp.concatenate([wr_hi, wr_lo], axis=1), wr_hi, br]
    for cst in consts:
        in_specs.append(pl.BlockSpec(cst.shape, lambda i: (0, 0)))
        args.append(cst)
    out_shape = [jax.ShapeDtypeStruct((n, d), F32), jax.ShapeDtypeStruct((n, d), BF16),
                 jax.ShapeDtypeStruct((n, LANES), F32)]
    out_specs = [pl.BlockSpec((tr, d), plain), pl.BlockSpec((tr, d), plain), pl.BlockSpec((tr, LANES), plain)]
    return pl.pallas_call(
        functools.partial(_out_router_kernel, npb, heads_in, two_o, two_res),
        grid=(n // tr,), in_specs=in_specs, out_specs=out_specs, out_shape=out_shape,
        compiler_params=_cparams(("arbitrary",)), name="out_router")(*args)


def _expert_kernel(te_ref, nt_ref, x_ref, wg_ref, wu_ref, wd_ref, o_ref):
    @pl.when(pl.program_id(0) < nt_ref[0])
    def _():
        x = x_ref[...]
        a = jnp.dot(x, wg_ref[...], preferred_element_type=F32)
        u = jnp.dot(x, wu_ref[...], preferred_element_type=F32)
        hh = (a * (1.0 / (1.0 + jnp.exp(-a))) * u).astype(BF16)
        o_ref[...] = jnp.dot(hh, wd_ref[...], preferred_element_type=F32).astype(o_ref.dtype)

    @pl.when(pl.program_id(0) >= nt_ref[0])
    def _():
        o_ref[...] = jnp.zeros_like(o_ref)


def expert_mlp(xs, tile_expert, n_tiles, wg, wu, wd, tm):
    p, d = xs.shape
    f = wg.shape[2]
    grid_spec = pltpu.PrefetchScalarGridSpec(
        num_scalar_prefetch=2, grid=(p // tm,),
        in_specs=[
            pl.BlockSpec((tm, d), lambda t, te, nt: (t, 0)),
            pl.BlockSpec((None, d, f), lambda t, te, nt: (te[t], 0, 0)),
            pl.BlockSpec((None, d, f), lambda t, te, nt: (te[t], 0, 0)),
            pl.BlockSpec((None, f, d), lambda t, te, nt: (te[t], 0, 0)),
        ],
        out_specs=pl.BlockSpec((tm, d), lambda t, te, nt: (t, 0)),
    )
    return pl.pallas_call(
        _expert_kernel, grid_spec=grid_spec, out_shape=jax.ShapeDtypeStruct((p, d), BF16),
        compiler_params=_cparams(("arbitrary",)), name="expert_mlp")(tile_expert, n_tiles, xs, wg, wu, wd)


def moe(h, xn, meta, wg, wu, wd, tm):
    n, d = xn.shape
    eid = meta[:, :2].astype(jnp.int32)
    gates = meta[:, 2:4]
    flat_e = eid.reshape(-1)
    onehot = (flat_e[:, None] == jnp.arange(N_EXPERTS, dtype=jnp.int32)[None, :]).astype(jnp.int32)
    csum = jnp.cumsum(onehot, axis=0)
    counts = csum[-1]
    rank = jnp.take_along_axis(csum, flat_e[:, None], axis=1)[:, 0] - 1
    padded = ((counts + tm - 1) // tm) * tm
    pend = jnp.cumsum(padded)
    pstart = pend - padded
    pos = pstart[flat_e] + rank
    p = 2 * n + N_EXPERTS * tm
    n_tiles_total = p // tm
    src = jnp.zeros((p,), jnp.int32).at[pos].set(jnp.arange(2 * n, dtype=jnp.int32) // 2)
    xs = jnp.take(xn, src, axis=0)
    tile_start = jnp.arange(n_tiles_total, dtype=jnp.int32) * tm
    tile_expert = jnp.sum((tile_start[:, None] >= pend[None, :]).astype(jnp.int32), axis=1)
    tile_expert = jnp.minimum(tile_expert, N_EXPERTS - 1)
    n_tiles = (pend[-1] // tm).astype(jnp.int32).reshape(1)
    o = expert_mlp(xs, tile_expert, n_tiles, wg, wu, wd, tm)
    pos2 = pos.reshape(n, 2)
    o1 = jnp.take(o, pos2[:, 0], axis=0).astype(F32)
    o2 = jnp.take(o, pos2[:, 1], axis=0).astype(F32)
    return h + o1 * gates[:, 0:1] + o2 * gates[:, 1:2]


def _final_norm_kernel(n_prompt_blocks, h_ref, g_ref, yp_ref, ys_ref):
    x = h_ref[...]
    ms = jnp.mean(x * x, axis=-1, keepdims=True)
    y = x * lax.rsqrt(ms + NORM_EPS) * g_ref[...]
    i = pl.program_id(0)

    @pl.when(i < n_prompt_blocks)
    def _():
        yp_ref[...] = y

    @pl.when(i >= n_prompt_blocks)
    def _():
        ys_ref[...] = y


def final_norm(h, g, n_prompt, tr):
    n, d = h.shape
    npb = n_prompt // tr
    return pl.pallas_call(
        functools.partial(_final_norm_kernel, npb),
        grid=(n // tr,),
        in_specs=[pl.BlockSpec((tr, d), lambda i: (i, 0)), pl.BlockSpec((1, d), lambda i: (0, 0))],
        out_specs=[pl.BlockSpec((tr, d), lambda i: (jnp.minimum(i, npb - 1), 0)),
                   pl.BlockSpec((tr, d), lambda i: (jnp.maximum(i - npb, 0), 0))],
        out_shape=[jax.ShapeDtypeStruct((n_prompt, d), F32), jax.ShapeDtypeStruct((n - n_prompt, d), F32)],
        compiler_params=_cparams(("arbitrary",)), name="final_norm")(h, g.reshape(1, d).astype(F32))


def _pick_rows_tile(n_prompt, n_sample):
    for tr in (512, 256, 128, 64, 32, 16, 8):
        if n_prompt % tr == 0 and n_sample % tr == 0:
            return tr
    raise ValueError("row counts must be multiples of 8")


def kernel(x_prompt, x_sample, cache_a_k, cache_a_v, cache_b_k, cache_b_v, norm_mix, norm_ffn, norm_final,
           w_a_qkv, b_a_qkv, a_sinks, w_a_o, w_b_qkv, b_lambda, b_subln, w_b_o,
           w_route_group, b_route_group, w_route_expert, b_route_expert, w_gate, w_up, w_down):
    batch, t, d = x_prompt.shape
    dec_batch, dec_t, _ = x_sample.shape
    n_p, n_s = batch * t, dec_batch * dec_t
    tr = _pick_rows_tile(n_p, n_s)
    tm = min(512, tr * 4)
    xp = x_prompt.reshape(n_p, d)
    xs = x_sample.reshape(n_s, d)
    nq_a = A_HEADS * A_HD
    nkv_a = A_KV_HEADS * A_HD
    e_all = N_EXPERTS

    def expert_weights(i):
        f = w_gate.shape[-1]
        return (w_gate[i].reshape(e_all, d, f).astype(BF16), w_up[i].reshape(e_all, d, f).astype(BF16),
                w_down[i].reshape(e_all, f, d).astype(BF16))

    q_a, kv_ap, kv_as = norm_proj(xp, xs, norm_mix[0], w_a_qkv[0].astype(BF16), b_a_qkv[0],
                                  [(0, nq_a, "flat", BF16), (nq_a, nq_a + 2 * nkv_a, "split", F32)], tr, n_p)
    qb_a = min(256, t)
    o_ap = attn_a_prompt(q_a, kv_ap, a_sinks[0], batch, t, qb_a)
    kv_s = kv_as.reshape(dec_batch, dec_t, 2 * nkv_a)
    past_a = cache_a_k.shape[2]
    kband = jnp.concatenate([cache_a_k[0].reshape(dec_batch, past_a, nkv_a), kv_s[..., :nkv_a]], axis=1)
    vband = jnp.concatenate([cache_a_v[0].reshape(dec_batch, past_a, nkv_a), kv_s[..., nkv_a:]], axis=1)
    o_as = attn_a_sample(q_a, kband, vband, a_sinks[0], n_p, dec_batch, dec_t)
    h, xn, meta = out_router(o_ap, o_as, xp, xs, w_a_o[0].astype(BF16), norm_ffn[0],
                             w_route_group[0], b_route_group[0], w_route_expert[0], b_route_expert[0], False, tr)
    h = moe(h, xn, meta, *expert_weights(0), tm)

    keep = min(CHUNK * WIN_CHUNKS, t)
    kv_p = kv_ap.reshape(batch, t, 2 * nkv_a)[:, t - keep:]
    a_k_prompt = kv_p[..., :nkv_a].reshape(1, batch, keep, A_KV_HEADS, A_HD)
    a_v_prompt = kv_p[..., nkv_a:].reshape(1, batch, keep, A_KV_HEADS, A_HD)
    a_k_sample = kband[:, dec_t:].reshape(1, dec_batch, past_a, A_KV_HEADS, A_HD)
    a_v_sample = vband[:, dec_t:].reshape(1, dec_batch, past_a, A_KV_HEADS, A_HD)

    lam_init = 0.8 - 0.6 * float(np.exp(-0.3 * 1))
    lp = b_lambda[0].astype(F32)
    lam = jnp.exp(jnp.sum(lp[0] * lp[1])) - jnp.exp(jnp.sum(lp[2] * lp[3])) + lam_init
    wb = B_HEADS * 2 * B_HD
    qkvh, k_bp, k_bs, v_bp, v_bs, vt = norm_proj(
        h, None, norm_mix[1], w_b_qkv[0].astype(BF16), None,
        [(0, 3 * wb, "heads", BF16), (wb, 2 * wb, "split", F32), (2 * wb, 3 * wb, "split", F32),
         (2 * wb, 3 * wb, "heads_t", BF16)], tr, n_p)
    qb_b = min(256, t)
    o_bp = attn_b_prompt(qkvh, vt, lam, b_subln[0], lam_init, batch, t, qb_b)
    o_bs = attn_b_sample(qkvh, cache_b_k[0], cache_b_v[0], lam, b_subln[0], lam_init, n_p, dec_batch, dec_t)
    h, xn, meta = out_router(o_bp, o_bs, h, None, w_b_o[0].astype(BF16), norm_ffn[1],
                             w_route_group[1], b_route_group[1], w_route_expert[1], b_route_expert[1], True, tr)
    h = moe(h, xn, meta, *expert_weights(1), tm)

    y_p, y_s = final_norm(h, norm_final, n_p, tr)
    hd2 = 2 * B_HD
    return (y_p.reshape(batch, t, d), y_s.reshape(dec_batch, dec_t, d),
            a_k_prompt, a_v_prompt, a_k_sample, a_v_sample,
            k_bp.reshape(1, batch, t, B_HEADS, hd2), v_bp.reshape(1, batch, t, B_HEADS, hd2),
            k_bs.reshape(1, dec_batch, dec_t, B_HEADS, hd2), v_bs.reshape(1, dec_batch, dec_t, B_HEADS, hd2))
```

```python
import functools

import jax
import jax.numpy as jnp
import numpy as np
from jax import lax
from jax.experimental import pallas as pl
from jax.experimental.pallas import tpu as pltpu
from jax.experimental.pallas import tpu_sc as plsc

F32 = jnp.float32
BF16 = jnp.bfloat16
NEG_INF = float("-inf")

CHUNK = 64
WIN_CHUNKS = 2
A_HEADS, A_KV_HEADS, A_GROUP, A_HD = 16, 4, 4, 64
B_HEADS, B_HD = 8, 64
N_GROUPS, EXPERTS_PER_GROUP = 4, 8
N_EXPERTS = N_GROUPS * EXPERTS_PER_GROUP
NORM_EPS = 1e-6
SUBLN_EPS = 1e-5

LANES = 128
COL_CHUNK = 512
HEAD_GROUP = 2
SC_WINDOW = 128
SC_ROW_WORDS = 256
VMEM_LIMIT = 48 * 1024 * 1024


def _cparams(sem):
    return pltpu.CompilerParams(dimension_semantics=sem, vmem_limit_bytes=VMEM_LIMIT)


def _alibi(n):
    return 2.0 ** (-8.0 * np.arange(1, n + 1) / n)


def _add_expert_outputs(x, og_ref, meta_ref):
    half = og_ref.shape[1] // 2
    meta = meta_ref[...]
    return (x + meta[:, 2:3] * _unpack_pairs(og_ref[:, :half])) + meta[:, 3:4] * _unpack_pairs(og_ref[:, half:])


def _norm_proj_kernel(n_prompt_blocks, two_src, has_bias, combine, outs, *refs):
    refs = list(refs)
    xa_ref = refs.pop(0)
    xb_ref = refs.pop(0) if two_src else None
    og_ref, meta_ref = (refs.pop(0), refs.pop(0)) if combine else (None, None)
    g_ref = refs.pop(0)
    w_ref = refs.pop(0)
    b_ref = refs.pop(0) if has_bias else None
    out_refs = refs
    x = xa_ref[...]
    if two_src:
        x = jnp.where(pl.program_id(0) < n_prompt_blocks, x, xb_ref[...])
    if combine:
        x = _add_expert_outputs(x, og_ref, meta_ref)
        out_refs.pop(0)[...] = x
    ms = jnp.mean(x * x, axis=-1, keepdims=True)
    xn = (x * lax.rsqrt(ms + NORM_EPS) * g_ref[...]).astype(BF16)
    is_prompt = pl.program_id(0) < n_prompt_blocks
    out_refs = list(out_refs)
    targets = []
    for c0, c1, layout in outs:
        targets.append((c0, c1, layout, out_refs.pop(0), out_refs.pop(0) if layout == "split" else None))
    for s in range(0, w_ref.shape[1], COL_CHUNK):
        e = s + COL_CHUNK
        r = jnp.dot(xn, w_ref[:, s:e], preferred_element_type=F32)
        if has_bias:
            r = r + b_ref[:, s:e]
        for c0, c1, layout, o_ref, o2_ref in targets:
            if not (c0 <= s and e <= c1):
                continue
            rr = r.astype(o_ref.dtype)
            if layout == "flat":
                o_ref[:, s - c0:e - c0] = rr
            elif layout == "split":

                @pl.when(is_prompt)
                def _(rr=rr, o_ref=o_ref, s=s, e=e, c0=c0):
                    o_ref[:, s - c0:e - c0] = rr

                @pl.when(jnp.logical_not(is_prompt))
                def _(rr=rr, o2_ref=o2_ref, s=s, e=e, c0=c0):
                    o2_ref[:, s - c0:e - c0] = rr
            elif layout == "heads":
                for hh in range((e - s) // LANES):
                    o_ref[(s - c0) // LANES + hh] = rr[:, hh * LANES:(hh + 1) * LANES]
            else:
                for hh in range((e - s) // LANES):
                    o_ref[(s - c0) // LANES + hh] = r[:, hh * LANES:(hh + 1) * LANES].T.astype(o_ref.dtype)


def norm_proj(xa, xb, g, w, b, outs, tr, n_prompt, og=None, meta=None):
    na, d = xa.shape
    nb = 0 if xb is None else xb.shape[0]
    n = na + nb
    npb = n_prompt // tr
    two = xb is not None
    combine = og is not None
    in_specs = [pl.BlockSpec((tr, d), (lambda i: (jnp.minimum(i, npb - 1), 0)) if two else (lambda i: (i, 0)))]
    args = [xa]
    if two:
        in_specs.append(pl.BlockSpec((tr, d), lambda i: (jnp.maximum(i - npb, 0), 0)))
        args.append(xb)
    if combine:
        in_specs += [pl.BlockSpec((tr, og.shape[1]), lambda i: (i, 0)), pl.BlockSpec((tr, LANES), lambda i: (i, 0))]
        args += [og, meta]
    in_specs.append(pl.BlockSpec((1, d), lambda i: (0, 0)))
    args.append(g.reshape(1, d).astype(F32))
    in_specs.append(pl.BlockSpec(w.shape, lambda i: (0, 0)))
    args.append(w)
    if b is not None:
        in_specs.append(pl.BlockSpec((1, w.shape[1]), lambda i: (0, 0)))
        args.append(b.reshape(1, -1).astype(F32))
    out_shapes, out_specs = [], []
    if combine:
        out_shapes.append(jax.ShapeDtypeStruct((n, d), F32))
        out_specs.append(pl.BlockSpec((tr, d), lambda i: (i, 0)))
    for c0, c1, layout, dt in outs:
        assert c0 % COL_CHUNK == 0 and c1 % COL_CHUNK == 0
        if layout == "flat":
            out_shapes.append(jax.ShapeDtypeStruct((n, c1 - c0), dt))
            out_specs.append(pl.BlockSpec((tr, c1 - c0), lambda i: (i, 0)))
        elif layout == "split":
            out_shapes.append(jax.ShapeDtypeStruct((n_prompt, c1 - c0), dt))
            out_specs.append(pl.BlockSpec((tr, c1 - c0), lambda i: (jnp.minimum(i, npb - 1), 0)))
            out_shapes.append(jax.ShapeDtypeStruct((n - n_prompt, c1 - c0), dt))
            out_specs.append(pl.BlockSpec((tr, c1 - c0), lambda i: (jnp.maximum(i - npb, 0), 0)))
        elif layout == "heads":
            nh = (c1 - c0) // LANES
            out_shapes.append(jax.ShapeDtypeStruct((nh, n, LANES), dt))
            out_specs.append(pl.BlockSpec((nh, tr, LANES), lambda i: (0, i, 0)))
        else:
            assert layout == "heads_t"
            nh = (c1 - c0) // LANES
            out_shapes.append(jax.ShapeDtypeStruct((nh, LANES, n), dt))
            out_specs.append(pl.BlockSpec((nh, LANES, tr), lambda i: (0, 0, i)))
    kern = functools.partial(_norm_proj_kernel, npb, two, b is not None, combine,
                             [(c0, c1, lay) for c0, c1, lay, _ in outs])
    return pl.pallas_call(
        kern, grid=(n // tr,), in_specs=in_specs, out_specs=out_specs, out_shape=out_shapes,
        compiler_params=_cparams(("arbitrary",)), name="norm_proj")(*args)


def _attn_a_chunk(q, k, v, bias_ref, sink_ref, valid):
    nq = q.shape[0]
    outs = []
    for kh in range(A_KV_HEADS):
        qs = jnp.concatenate(
            [q[:, (kh * A_GROUP + g) * A_HD:(kh * A_GROUP + g + 1) * A_HD] for g in range(A_GROUP)], axis=0)
        qs = qs * jnp.asarray(A_HD ** -0.5, BF16)
        kk = k[:, kh * A_HD:(kh + 1) * A_HD]
        s = lax.dot_general(qs, kk, (((1,), (1,)), ((), ())), preferred_element_type=F32) + bias_ref[kh]
        if valid is not None:
            s = jnp.where(valid, s, NEG_INF)
        sk = sink_ref[kh]
        m = jnp.maximum(jnp.max(s, axis=-1, keepdims=True), sk)
        e = jnp.exp(s - m)
        den = jnp.sum(e, axis=-1, keepdims=True) + jnp.exp(sk - m)
        p = (e * (1.0 / den)).astype(BF16)
        o = jnp.dot(p, v[:, kh * A_HD:(kh + 1) * A_HD], preferred_element_type=F32)
        for g in range(A_GROUP):
            outs.append(o[g * nq:(g + 1) * nq])
    return jnp.concatenate(outs, axis=1)


def _attn_a_prompt_kernel(qb, q_ref, kvc_ref, kvp_ref, bias_ref, sink_ref, o_ref):
    cb = pl.program_id(1)
    back = CHUNK * WIN_CHUNKS
    nk = back + qb
    nkv = A_KV_HEADS * A_HD
    kvfull = jnp.concatenate([kvp_ref[...], kvc_ref[...]], axis=0)
    k_all = kvfull[:, :nkv].astype(BF16)
    vt_all = kvfull[:, nkv:].T.astype(BF16)
    key_pos = lax.broadcasted_iota(jnp.int32, (nk, 1), 0) + (cb * qb - back)
    before_start = jnp.where(key_pos >= 0, 0.0, NEG_INF)
    dn = (((1,), (1,)), ((), ()))
    outs = []
    for kh in range(A_KV_HEADS):
        qs = jnp.concatenate(
            [q_ref[:, (kh * A_GROUP + g) * A_HD:(kh * A_GROUP + g + 1) * A_HD] for g in range(A_GROUP)], axis=0)
        qs = qs * jnp.asarray(A_HD ** -0.5, BF16)
        st = lax.dot_general(k_all[:, kh * A_HD:(kh + 1) * A_HD], qs, dn, preferred_element_type=F32)
        st = st + bias_ref[kh] + before_start
        sk = sink_ref[kh]
        m = jnp.maximum(jnp.max(st, axis=0, keepdims=True), sk)
        e = jnp.exp(st - m)
        den = jnp.sum(e, axis=0, keepdims=True) + jnp.exp(sk - m)
        p = (e * (1.0 / den)).astype(BF16)
        ot = jnp.dot(vt_all[kh * A_HD:(kh + 1) * A_HD, :], p, preferred_element_type=F32)
        o = ot.T
        for g in range(A_GROUP):
            outs.append(o[g * qb:(g + 1) * qb])
    o_ref[...] = jnp.concatenate(outs, axis=1).astype(o_ref.dtype)


def _a_prompt_tables(qb, sinks):
    back = CHUNK * WIN_CHUNKS
    slopes = _alibi(A_HEADS).reshape(A_KV_HEADS, A_GROUP)
    qpos = np.arange(qb)
    kpos = np.arange(back + qb) - back
    dist = np.abs(kpos[:, None] - qpos[None, :]).astype(np.float64)
    band_lo = (qpos // CHUNK) * CHUNK - back
    in_band = (kpos[:, None] >= band_lo[None, :]) & (kpos[:, None] < band_lo[None, :] + back + CHUNK)
    bias = np.where(in_band[None, None], -slopes[:, :, None, None] * dist[None, None], -np.inf)
    bias = np.transpose(bias, (0, 2, 1, 3)).reshape(A_KV_HEADS, back + qb, A_GROUP * qb)
    sk = jnp.repeat(sinks.astype(F32).reshape(A_KV_HEADS, A_GROUP), qb, axis=1)[:, None, :]
    return jnp.asarray(bias, F32), sk


def _a_tables(qpos, kpos, sinks):
    slopes = _alibi(A_HEADS).reshape(A_KV_HEADS, A_GROUP)
    dist = np.abs(qpos[:, None] - kpos[None, :]).astype(np.float64)
    bias = -slopes[:, :, None, None] * dist
    nq = len(qpos)
    bias = jnp.asarray(bias.reshape(A_KV_HEADS, A_GROUP * nq, len(kpos)), F32)
    sk = jnp.repeat(sinks.astype(F32).reshape(A_KV_HEADS, A_GROUP), nq, axis=1)[..., None]
    return bias, sk


def attn_a_prompt(q_all, kv_all, sinks, batch, t, qb):
    nq = A_HEADS * A_HD
    nkv2 = 2 * A_KV_HEADS * A_HD
    back = CHUNK * WIN_CHUNKS
    bias, sk = _a_prompt_tables(qb, sinks)
    nblk = t // qb
    r = qb // back
    return pl.pallas_call(
        functools.partial(_attn_a_prompt_kernel, qb),
        grid=(batch, nblk),
        in_specs=[
            pl.BlockSpec((qb, nq), lambda b, i: (b * nblk + i, 0)),
            pl.BlockSpec((qb, nkv2), lambda b, i: (b * nblk + i, 0)),
            pl.BlockSpec((back, nkv2), lambda b, i: (jnp.maximum((b * nblk + i) * r - 1, 0), 0)),
            pl.BlockSpec(bias.shape, lambda b, i: (0, 0, 0)),
            pl.BlockSpec(sk.shape, lambda b, i: (0, 0, 0)),
        ],
        out_specs=pl.BlockSpec((qb, nq), lambda b, i: (b * nblk + i, 0)),
        out_shape=jax.ShapeDtypeStruct((batch * t, nq), BF16),
        compiler_params=_cparams(("arbitrary", "arbitrary")), name="attn_a_prompt",
    )(q_all, kv_all, kv_all, bias, sk)


def _attn_a_sample_kernel(q_ref, k_ref, v_ref, bias_ref, sink_ref, o_ref):
    o = _attn_a_chunk(q_ref[...], k_ref[...].astype(BF16), v_ref[...].astype(BF16), bias_ref, sink_ref, None)
    o_ref[...] = o.astype(o_ref.dtype)


def attn_a_sample(q_all, kband, vband, sinks, row0, dec_batch, dec_t):
    nq = A_HEADS * A_HD
    s = kband.shape[1]
    past = s - dec_t
    kpos = np.concatenate([np.arange(past) - past, np.arange(dec_t)])
    bias, sk = _a_tables(np.arange(dec_t), kpos, sinks)
    blk0 = row0 // dec_t
    return pl.pallas_call(
        _attn_a_sample_kernel,
        grid=(dec_batch,),
        in_specs=[
            pl.BlockSpec((dec_t, nq), lambda b: (blk0 + b, 0)),
            pl.BlockSpec((None, s, kband.shape[2]), lambda b: (b, 0, 0)),
            pl.BlockSpec((None, s, vband.shape[2]), lambda b: (b, 0, 0)),
            pl.BlockSpec(bias.shape, lambda b: (0, 0, 0)),
            pl.BlockSpec(sk.shape, lambda b: (0, 0, 0)),
        ],
        out_specs=pl.BlockSpec((dec_t, nq), lambda b: (b, 0)),
        out_shape=jax.ShapeDtypeStruct((dec_batch * dec_t, nq), BF16),
        compiler_params=_cparams(("arbitrary",)), name="attn_a_sample",
    )(q_all, kband, vband, bias, sk)


def _stack_maps(qh):
    qh = qh * jnp.asarray(B_HD ** -0.5, BF16)
    lane = lax.broadcasted_iota(jnp.int32, qh.shape, 1)
    zero = jnp.zeros_like(qh)
    return jnp.concatenate([jnp.where(lane < B_HD, qh, zero), jnp.where(lane >= B_HD, qh, zero)], axis=0)


def _diff_finish(acc, l, nq, lam, lam_init, subln):
    inv = 1.0 / l
    o = acc[:nq] * inv[:nq] - lam * (acc[nq:] * inv[nq:])
    ms = jnp.mean(o * o, axis=-1, keepdims=True)
    return o * lax.rsqrt(ms + SUBLN_EPS) * subln * (1.0 - lam_init)


def _attn_b_prompt_kernel(qb, lam_init, slope_ref, lam_ref, q_ref, k_ref, vt_ref, kcol_ref, tri_ref, msk_ref,
                          subln_ref, o_ref, acc_ref):
    i = pl.program_id(1)
    lam = lam_ref[0]
    dn = (((1,), (1,)), ((), ()))

    def group_body(hg, carry):
        heads = [hg * HEAD_GROUP + u for u in range(HEAD_GROUP)]
        slopes = [slope_ref[h] for h in heads]
        qqs = [_stack_maps(q_ref[h]) for h in heads]

        def kv_step(j, c, diag):
            out = []
            for u, h in enumerate(heads):
                m, l = c[u]
                slope = slopes[u]
                start = pl.multiple_of(j * qb, qb)
                kb = k_ref[h, pl.ds(start, qb), :]
                vtb = vt_ref[h, :, pl.ds(start, qb)]
                t = lax.dot_general(kb, qqs[u], dn, preferred_element_type=F32) + slope * kcol_ref[...]
                if diag:
                    t = t + slope * tri_ref[...] + msk_ref[...]
                off = slope * jnp.full((1, 2 * qb), (j - i) * qb, jnp.int32).astype(F32)
                m_new = jnp.maximum(m, jnp.max(t, axis=0, keepdims=True) + off)
                alpha = jnp.exp(m - m_new)
                p = jnp.exp(t + (off - m_new))
                l = alpha * l + jnp.sum(p, axis=0, keepdims=True)
                acc_ref[u] = alpha * acc_ref[u] + jnp.dot(vtb, p.astype(BF16), preferred_element_type=F32)
                out.append((m_new, l))
            return tuple(out)

        acc_ref[...] = jnp.zeros_like(acc_ref)
        init = tuple((jnp.full((1, 2 * qb), NEG_INF, F32), jnp.zeros((1, 2 * qb), F32)) for _ in heads)
        c = lax.fori_loop(0, i, lambda j, c: kv_step(j, c, False), init)
        c = kv_step(i, c, True)
        for u, h in enumerate(heads):
            inv = 1.0 / c[u][1]
            acc = acc_ref[u]
            ot = acc[:, :qb] * inv[:, :qb] - lam * (acc[:, qb:] * inv[:, qb:])
            ms = jnp.mean(ot * ot, axis=0, keepdims=True)
            ot = ot * lax.rsqrt(ms + SUBLN_EPS) * subln_ref[...] * (1.0 - lam_init)
            o_ref[h] = ot.T.astype(o_ref.dtype)
        return carry

    lax.fori_loop(0, B_HEADS // HEAD_GROUP, group_body, 0)


def attn_b_prompt(qkvh, vt, lam, subln, lam_init, batch, t, qb):
    nblk = t // qb
    qq = np.arange(qb)
    kcol = np.broadcast_to(qq[:, None], (qb, 2 * qb)).astype(np.float32)
    tri = -2.0 * np.maximum(qq[:, None] - qq[None, :], 0).astype(np.float32)
    msk = np.where((qq[:, None] // CHUNK) <= (qq[None, :] // CHUNK), 0.0, -np.inf).astype(np.float32)
    tri = jnp.asarray(np.concatenate([tri, tri], axis=1))
    msk = jnp.asarray(np.concatenate([msk, msk], axis=1))
    kcol = jnp.asarray(kcol)
    slopes = jnp.asarray(_alibi(B_HEADS), F32)
    subln_t = jnp.broadcast_to(subln.astype(F32).reshape(LANES, 1), (LANES, qb))
    const = lambda b, i, *_: (0, 0)
    grid_spec = pltpu.PrefetchScalarGridSpec(
        num_scalar_prefetch=2, grid=(batch, nblk),
        in_specs=[
            pl.BlockSpec((B_HEADS, qb, LANES), lambda b, i, *_: (0, b * nblk + i, 0)),
            pl.BlockSpec((B_HEADS, t, LANES), lambda b, i, *_: (1, b, 0)),
            pl.BlockSpec((B_HEADS, LANES, t), lambda b, i, *_: (0, 0, b)),
            pl.BlockSpec(kcol.shape, const),
            pl.BlockSpec(tri.shape, const),
            pl.BlockSpec(msk.shape, const),
            pl.BlockSpec(subln_t.shape, const),
        ],
        out_specs=pl.BlockSpec((B_HEADS, qb, LANES), lambda b, i, *_: (0, b * nblk + i, 0)),
        scratch_shapes=[pltpu.VMEM((HEAD_GROUP, LANES, 2 * qb), F32)],
    )
    return pl.pallas_call(
        functools.partial(_attn_b_prompt_kernel, qb, lam_init), grid_spec=grid_spec,
        out_shape=jax.ShapeDtypeStruct((B_HEADS, batch * t, LANES), BF16),
        compiler_params=_cparams(("arbitrary", "arbitrary")), name="attn_b_prompt",
    )(slopes, lam.reshape(1).astype(F32), qkvh, qkvh, vt, kcol, tri, msk, subln_t)


def _attn_b_sample_kernel(dec_t, past, lam_init, slopes, lam_ref, q_ref, kn_ref, vn_ref, ck_ref, cv_ref,
                          tnew_ref, subln_ref, o_ref):
    lam = lam_ref[0]
    subln = subln_ref[...]
    colc = lax.broadcasted_iota(jnp.int32, (1, past), 1).astype(F32) - float(past)
    for h in range(B_HEADS):
        slope = float(slopes[h])
        qq = _stack_maps(q_ref[h])
        kc = ck_ref[:, h, :].astype(BF16)
        vc = cv_ref[:, h, :].astype(BF16)
        dn = (((1,), (1,)), ((), ()))
        s_c = lax.dot_general(qq, kc, dn, preferred_element_type=F32) + slope * colc
        s_n = lax.dot_general(qq, kn_ref[h], dn, preferred_element_type=F32) + slope * tnew_ref[...]
        m = jnp.maximum(jnp.max(s_c, axis=-1, keepdims=True), jnp.max(s_n, axis=-1, keepdims=True))
        p_c = jnp.exp(s_c - m)
        p_n = jnp.exp(s_n - m)
        l = jnp.sum(p_c, axis=-1, keepdims=True) + jnp.sum(p_n, axis=-1, keepdims=True)
        acc = (jnp.dot(p_c.astype(BF16), vc, preferred_element_type=F32)
               + jnp.dot(p_n.astype(BF16), vn_ref[h], preferred_element_type=F32))
        o_ref[h] = _diff_finish(acc, l, dec_t, lam, lam_init, subln).astype(o_ref.dtype)


def attn_b_sample(qkvh, cache_k, cache_v, lam, subln, lam_init, row0, dec_batch, dec_t):
    past = cache_k.shape[1]
    qi = np.arange(dec_t)
    tnew = (qi[:, None] - np.abs(qi[:, None] - qi[None, :])).astype(np.float32)
    tnew = jnp.asarray(np.concatenate([tnew, tnew], axis=0))
    blk0 = row0 // dec_t
    grid_spec = pltpu.PrefetchScalarGridSpec(
        num_scalar_prefetch=1, grid=(dec_batch,),
        in_specs=[
            pl.BlockSpec((B_HEADS, dec_t, LANES), lambda b, *_: (0, blk0 + b, 0)),
            pl.BlockSpec((B_HEADS, dec_t, LANES), lambda b, *_: (1, blk0 + b, 0)),
            pl.BlockSpec((B_HEADS, dec_t, LANES), lambda b, *_: (2, blk0 + b, 0)),
            pl.BlockSpec((None, past, B_HEADS, LANES), lambda b, *_: (b, 0, 0, 0)),
            pl.BlockSpec((None, past, B_HEADS, LANES), lambda b, *_: (b, 0, 0, 0)),
            pl.BlockSpec(tnew.shape, lambda b, *_: (0, 0)),
            pl.BlockSpec((1, LANES), lambda b, *_: (0, 0)),
        ],
        out_specs=pl.BlockSpec((B_HEADS, dec_t, LANES), lambda b, *_: (0, b, 0)),
    )
    return pl.pallas_call(
        functools.partial(_attn_b_sample_kernel, dec_t, past, lam_init, tuple(_alibi(B_HEADS))),
        grid_spec=grid_spec,
        out_shape=jax.ShapeDtypeStruct((B_HEADS, dec_batch * dec_t, LANES), BF16),
        compiler_params=_cparams(("arbitrary",)), name="attn_b_sample",
    )(lam.reshape(1).astype(F32), qkvh, qkvh, qkvh, cache_k, cache_v, tnew, subln.reshape(1, LANES).astype(F32))


def _route(logits):
    lane = lax.broadcasted_iota(jnp.int32, logits.shape, 1).astype(F32)
    big = float(LANES)
    lg = jnp.where(lane < N_GROUPS, logits, NEG_INF)
    gmax = jnp.max(lg, axis=-1, keepdims=True)
    g_idx = jnp.min(jnp.where(lg == gmax, lane, big), axis=-1, keepdims=True)
    g_prob = 1.0 / jnp.sum(jnp.exp(lg - gmax), axis=-1, keepdims=True)
    lo = N_GROUPS + EXPERTS_PER_GROUP * g_idx
    le = jnp.where((lane >= lo) & (lane < lo + EXPERTS_PER_GROUP), logits, NEG_INF)
    m1 = jnp.max(le, axis=-1, keepdims=True)
    i1 = jnp.min(jnp.where(le == m1, lane, big), axis=-1, keepdims=True)
    le2 = jnp.where(lane == i1, NEG_INF, le)
    m2 = jnp.max(le2, axis=-1, keepdims=True)
    i2 = jnp.min(jnp.where(le2 == m2, lane, big), axis=-1, keepdims=True)
    t = jnp.exp(m2 - m1)
    ga = 1.0 / (1.0 + t)
    gb = t * ga
    return lane, i1, i2, g_prob * ga, g_prob * gb


def _unpack_pairs(u):
    lo = pltpu.bitcast(u << 16, F32)
    hi = pltpu.bitcast(u & jnp.uint32(0xFFFF0000), F32)
    return jnp.concatenate([lo, hi], axis=1)


def _pack_pairs(x):
    bits = pltpu.bitcast(x.astype(BF16).astype(F32), jnp.uint32)
    w = x.shape[1] // 2
    return (bits[:, :w] >> 16) | bits[:, w:]


def _out_router_kernel(n_prompt_blocks, heads_in, two_o, two_res, *refs):
    refs = list(refs)
    oa_ref = refs.pop(0)
    ob_ref = refs.pop(0) if two_o else None
    ra_ref = refs.pop(0)
    rb_ref = refs.pop(0) if two_res else None
    wo_ref, g_ref, wr_ref, wr2_ref, br_ref, ltri_ref, h_ref, xn_ref, meta_ref, cnt_ref, carry_ref = refs
    is_prompt = pl.program_id(0) < n_prompt_blocks

    def load_o(ref):
        if heads_in:
            return jnp.concatenate([ref[hh] for hh in range(ref.shape[0])], axis=1)
        return ref[...]

    o = load_o(oa_ref)
    if two_o:
        o = jnp.where(is_prompt, o, load_o(ob_ref))
    res = ra_ref[...]
    if two_res:
        res = jnp.where(is_prompt, res, rb_ref[...])
    h = res + jnp.dot(o, wo_ref[...], preferred_element_type=F32)
    h_ref[...] = h
    ms = jnp.mean(h * h, axis=-1, keepdims=True)
    xn = h * lax.rsqrt(ms + NORM_EPS) * g_ref[...]
    x_hi = xn.astype(BF16)
    xn_ref[...] = _pack_pairs(xn)
    x_lo = (xn - x_hi.astype(F32)).astype(BF16)
    r = jnp.dot(x_hi, wr_ref[...], preferred_element_type=F32)
    logits = (r[:, :LANES] + r[:, LANES:]) + jnp.dot(x_lo, wr2_ref[...], preferred_element_type=F32) + br_ref[...]
    lane, i1, i2, g1, g2 = _route(logits)

    @pl.when(pl.program_id(0) == 0)
    def _():
        carry_ref[...] = jnp.zeros_like(carry_ref)

    sel1 = lane == i1
    sel2 = lane == i2
    onehot = jnp.where(sel1 | sel2, 1.0, 0.0)
    before = jnp.dot(ltri_ref[...], onehot.astype(BF16), preferred_element_type=F32) + carry_ref[...]
    rank1 = jnp.sum(jnp.where(sel1, before, 0.0), axis=-1, keepdims=True)
    rank2 = jnp.sum(jnp.where(sel2, before, 0.0), axis=-1, keepdims=True)
    carry_ref[...] = carry_ref[...] + jnp.sum(onehot, axis=0, keepdims=True)
    cnt_ref[...] = carry_ref[...]
    cols = [i1 - N_GROUPS, i2 - N_GROUPS, g1, g2, rank1, rank2]
    meta = jnp.zeros_like(logits)
    for c, v in enumerate(cols):
        meta = jnp.where(lane == c, v, meta)
    meta_ref[...] = meta


def out_router(oa, ob, ra, rb, wo, g, w_rg, b_rg, w_re, b_re, heads_in, tr):
    d = ra.shape[1]
    if heads_in:
        na = oa.shape[1]
        nb = 0 if ob is None else ob.shape[1]
    else:
        na = oa.shape[0]
        nb = 0 if ob is None else ob.shape[0]
    n = na + nb
    npb = na // tr
    two_o = ob is not None
    two_res = rb is not None
    first = lambda i: (jnp.minimum(i, npb - 1), 0)
    second = lambda i: (jnp.maximum(i - npb, 0), 0)
    plain = lambda i: (i, 0)
    in_specs, args = [], []

    def add_o(x, imap):
        if heads_in:
            in_specs.append(pl.BlockSpec((x.shape[0], tr, LANES), lambda i: (0, imap(i)[0], 0)))
        else:
            in_specs.append(pl.BlockSpec((tr, x.shape[1]), imap))
        args.append(x)

    add_o(oa, first if two_o else plain)
    if two_o:
        add_o(ob, second)
    in_specs.append(pl.BlockSpec((tr, d), first if two_res else plain))
    args.append(ra)
    if two_res:
        in_specs.append(pl.BlockSpec((tr, d), second))
        args.append(rb)
    wr = jnp.zeros((d, LANES), F32).at[:, :N_GROUPS].set(w_rg.astype(F32))
    wr = wr.at[:, N_GROUPS:N_GROUPS + N_EXPERTS].set(w_re.astype(F32))
    wr_hi = wr.astype(BF16)
    wr_lo = (wr - wr_hi.astype(F32)).astype(BF16)
    br = jnp.zeros((1, LANES), F32).at[0, :N_GROUPS].set(b_rg.astype(F32))
    br = br.at[0, N_GROUPS:N_GROUPS + N_EXPERTS].set(b_re.astype(F32))
    ltri = jnp.asarray(np.tril(np.ones((tr, tr), np.float32), -1), BF16)
    consts = [wo, g.reshape(1, d).astype(F32), jnp.concatenate([wr_hi, wr_lo], axis=1), wr_hi, br, ltri]
    for cst in consts:
        in_specs.append(pl.BlockSpec(cst.shape, lambda i: (0, 0)))
        args.append(cst)
    out_shape = [jax.ShapeDtypeStruct((n, d), F32), jax.ShapeDtypeStruct((n, d // 2), jnp.uint32),
                 jax.ShapeDtypeStruct((n, LANES), F32), jax.ShapeDtypeStruct((1, LANES), F32)]
    out_specs = [pl.BlockSpec((tr, d), plain), pl.BlockSpec((tr, d // 2), plain), pl.BlockSpec((tr, LANES), plain),
                 pl.BlockSpec((1, LANES), lambda i: (0, 0))]
    return pl.pallas_call(
        functools.partial(_out_router_kernel, npb, heads_in, two_o, two_res),
        grid=(n // tr,), in_specs=in_specs, out_specs=out_specs, out_shape=out_shape,
        scratch_shapes=[pltpu.VMEM((1, LANES), F32)],
        compiler_params=_cparams(("arbitrary",)), name="out_router")(*args)


def _expert_kernel(te_ref, nt_ref, x_ref, wg_ref, wu_ref, wd_ref, o_ref):
    @pl.when(pl.program_id(0) < nt_ref[0])
    def _():
        x = _unpack_pairs(x_ref[...]).astype(BF16)
        a = jnp.dot(x, wg_ref[...].astype(BF16), preferred_element_type=F32)
        u = jnp.dot(x, wu_ref[...].astype(BF16), preferred_element_type=F32)
        hh = (a * (1.0 / (1.0 + jnp.exp(-a))) * u).astype(BF16)
        o_ref[...] = _pack_pairs(jnp.dot(hh, wd_ref[...].astype(BF16), preferred_element_type=F32))

    @pl.when(pl.program_id(0) >= nt_ref[0])
    def _():
        o_ref[...] = jnp.zeros_like(o_ref)


def expert_mlp(xs, tile_expert, n_tiles, w_gate, w_up, w_down, layer, tm):
    p, dh = xs.shape
    d, f = w_gate.shape[-2:]
    epg = w_gate.shape[2]
    wmap = lambda t, te, nt: (layer, te[t] // epg, te[t] % epg, 0, 0)
    grid_spec = pltpu.PrefetchScalarGridSpec(
        num_scalar_prefetch=2, grid=(p // tm,),
        in_specs=[
            pl.BlockSpec((tm, dh), lambda t, te, nt: (t, 0)),
            pl.BlockSpec((None, None, None, d, f), wmap),
            pl.BlockSpec((None, None, None, d, f), wmap),
            pl.BlockSpec((None, None, None, f, d), wmap),
        ],
        out_specs=pl.BlockSpec((tm, dh), lambda t, te, nt: (t, 0)),
    )
    return pl.pallas_call(
        _expert_kernel, grid_spec=grid_spec, out_shape=jax.ShapeDtypeStruct((p, dh), jnp.uint32),
        compiler_params=_cparams(("arbitrary",)), name="expert_mlp")(tile_expert, n_tiles, xs, w_gate, w_up, w_down)


def sc_scatter_rows(x2, idx, n_out):
    mesh = plsc.VectorSubcoreMesh(core_axis_name="c", subcore_axis_name="s")
    n_src_win = x2.shape[0] // SC_WINDOW

    @functools.partial(pl.kernel, out_type=jax.ShapeDtypeStruct((n_out, SC_ROW_WORDS), x2.dtype), mesh=mesh)
    def scatter(x_hbm, i_hbm, o_hbm):
        def body(x_vmem, i_vmem):
            pltpu.sync_copy(x_vmem, o_hbm.at[i_vmem.at[0]])

        pltpu.emit_pipeline(
            body, grid=(n_out // SC_WINDOW,),
            in_specs=[pl.BlockSpec((SC_WINDOW, SC_ROW_WORDS), lambda i: (lax.rem(i, n_src_win), 0)),
                      pl.BlockSpec((1, SC_WINDOW), lambda i: (0, i))],
            out_specs=[], core_axis_name=("c", "s"), dimension_semantics=(pltpu.PARALLEL,),
        )(x_hbm, i_hbm)

    return scatter(x2, idx)


def sc_gather_rows(x2, idx):
    mesh = plsc.VectorSubcoreMesh(core_axis_name="c", subcore_axis_name="s")
    n_out = idx.shape[1]

    @functools.partial(pl.kernel, out_type=jax.ShapeDtypeStruct((n_out, SC_ROW_WORDS), x2.dtype), mesh=mesh)
    def gather(x_hbm, i_hbm, o_hbm):
        def body(i_vmem, o_vmem):
            pltpu.sync_copy(x_hbm.at[i_vmem.at[0]], o_vmem)

        pltpu.emit_pipeline(
            body, grid=(n_out // SC_WINDOW,),
            in_specs=[pl.BlockSpec((1, SC_WINDOW), lambda i: (0, i))],
            out_specs=[pl.BlockSpec((SC_WINDOW, SC_ROW_WORDS), lambda i: (i, 0))],
            core_axis_name=("c", "s"), dimension_semantics=(pltpu.PARALLEL,),
        )(i_hbm, o_hbm)

    return gather(x2, idx)


def moe_dispatch(xn_packed, meta, counts, tm):
    n, dh = xn_packed.shape
    parts = dh // SC_ROW_WORDS
    e = meta[:, 0:2].astype(jnp.int32)
    rank = meta[:, 4:6].astype(jnp.int32)
    counts = counts[0, N_GROUPS:N_GROUPS + N_EXPERTS].astype(jnp.int32)
    padded = ((counts + tm - 1) // tm) * tm
    pend = jnp.cumsum(padded)
    pstart = pend - padded
    ids = jnp.arange(N_EXPERTS, dtype=jnp.int32)
    pos = jnp.sum(jnp.where(e[..., None] == ids, pstart, 0), axis=-1) + rank
    p = 2 * n + N_EXPERTS * tm
    n_pad = p - 2 * n
    gap_len = jnp.concatenate([padded - counts, (p - pend[-1]).reshape(1)])
    gap_first = jnp.concatenate([pstart + counts, pend[-1:]])
    gap_end = jnp.cumsum(gap_len)
    k = jnp.arange(n_pad, dtype=jnp.int32)
    gap = jnp.sum((k[:, None] >= gap_end[None, :]).astype(jnp.int32), axis=1)
    shift = gap_first - (gap_end - gap_len)
    pad_pos = k + jnp.sum(jnp.where(gap[:, None] == jnp.arange(N_EXPERTS + 1), shift, 0), axis=-1)
    sub = jnp.arange(parts, dtype=jnp.int32)
    expand = lambda rows: (parts * rows[:, None] + sub).reshape(-1)
    scat_idx = jnp.concatenate([expand(pos[:, 0]), expand(pos[:, 1]), expand(pad_pos)]).reshape(1, parts * p)
    gath_idx = expand(pos.reshape(-1)).reshape(1, 2 * parts * n)
    xs = sc_scatter_rows(xn_packed.reshape(n * parts, SC_ROW_WORDS), scat_idx, parts * p).reshape(p, dh)
    tile_start = jnp.arange(p // tm, dtype=jnp.int32) * tm
    tile_expert = jnp.sum((tile_start[:, None] >= pend[None, :]).astype(jnp.int32), axis=1)
    tile_expert = jnp.minimum(tile_expert, N_EXPERTS - 1)
    n_tiles = (pend[-1] // tm).astype(jnp.int32).reshape(1)
    return xs, tile_expert, n_tiles, gath_idx


def moe_experts(xn_packed, meta, counts, w_gate, w_up, w_down, layer, tm):
    n, dh = xn_packed.shape
    xs, tile_expert, n_tiles, gath_idx = moe_dispatch(xn_packed, meta, counts, tm)
    o = expert_mlp(xs, tile_expert, n_tiles, w_gate, w_up, w_down, layer, tm)
    parts = dh // SC_ROW_WORDS
    return sc_gather_rows(o.reshape(o.shape[0] * parts, SC_ROW_WORDS), gath_idx).reshape(n, 2 * dh)


def _final_norm_kernel(n_prompt_blocks, h_ref, og_ref, meta_ref, g_ref, yp_ref, ys_ref):
    x = _add_expert_outputs(h_ref[...], og_ref, meta_ref)
    ms = jnp.mean(x * x, axis=-1, keepdims=True)
    y = x * lax.rsqrt(ms + NORM_EPS) * g_ref[...]
    i = pl.program_id(0)

    @pl.when(i < n_prompt_blocks)
    def _():
        yp_ref[...] = y

    @pl.when(i >= n_prompt_blocks)
    def _():
        ys_ref[...] = y


def final_norm(h, og, meta, g, n_prompt, tr):
    n, d = h.shape
    npb = n_prompt // tr
    return pl.pallas_call(
        functools.partial(_final_norm_kernel, npb),
        grid=(n // tr,),
        in_specs=[pl.BlockSpec((tr, d), lambda i: (i, 0)), pl.BlockSpec((tr, og.shape[1]), lambda i: (i, 0)),
                  pl.BlockSpec((tr, LANES), lambda i: (i, 0)), pl.BlockSpec((1, d), lambda i: (0, 0))],
        out_specs=[pl.BlockSpec((tr, d), lambda i: (jnp.minimum(i, npb - 1), 0)),
                   pl.BlockSpec((tr, d), lambda i: (jnp.maximum(i - npb, 0), 0))],
        out_shape=[jax.ShapeDtypeStruct((n_prompt, d), F32), jax.ShapeDtypeStruct((n - n_prompt, d), F32)],
        compiler_params=_cparams(("arbitrary",)), name="final_norm")(h, og, meta, g.reshape(1, d).astype(F32))


def _pick_rows_tile(n_prompt, n_sample):
    for tr in (512, 256, 128, 64, 32, 16, 8):
        if n_prompt % tr == 0 and n_sample % tr == 0:
            return tr
    raise ValueError("row counts must be multiples of 8")


def kernel(x_prompt, x_sample, cache_a_k, cache_a_v, cache_b_k, cache_b_v, norm_mix, norm_ffn, norm_final,
           w_a_qkv, b_a_qkv, a_sinks, w_a_o, w_b_qkv, b_lambda, b_subln, w_b_o,
           w_route_group, b_route_group, w_route_expert, b_route_expert, w_gate, w_up, w_down):
    batch, t, d = x_prompt.shape
    dec_batch, dec_t, _ = x_sample.shape
    n_p, n_s = batch * t, dec_batch * dec_t
    tr = _pick_rows_tile(n_p, n_s)
    tm = min(512, tr * 4)
    xp = x_prompt.reshape(n_p, d)
    xs = x_sample.reshape(n_s, d)
    nq_a = A_HEADS * A_HD
    nkv_a = A_KV_HEADS * A_HD

    q_a, kv_ap, kv_as = norm_proj(xp, xs, norm_mix[0], w_a_qkv[0].astype(BF16), b_a_qkv[0],
                                  [(0, nq_a, "flat", BF16), (nq_a, nq_a + 2 * nkv_a, "split", F32)], tr, n_p)
    qb_a = min(256, t)
    o_ap = attn_a_prompt(q_a, kv_ap, a_sinks[0], batch, t, qb_a)
    kv_s = kv_as.reshape(dec_batch, dec_t, 2 * nkv_a)
    past_a = cache_a_k.shape[2]
    kband = jnp.concatenate([cache_a_k[0].reshape(dec_batch, past_a, nkv_a), kv_s[..., :nkv_a]], axis=1)
    vband = jnp.concatenate([cache_a_v[0].reshape(dec_batch, past_a, nkv_a), kv_s[..., nkv_a:]], axis=1)
    o_as = attn_a_sample(q_a, kband, vband, a_sinks[0], n_p, dec_batch, dec_t)
    h, xn, meta, counts = out_router(o_ap, o_as, xp, xs, w_a_o[0].astype(BF16), norm_ffn[0], w_route_group[0],
                                     b_route_group[0], w_route_expert[0], b_route_expert[0], False, tr)
    og = moe_experts(xn, meta, counts, w_gate, w_up, w_down, 0, tm)

    keep = min(CHUNK * WIN_CHUNKS, t)
    kv_p = kv_ap.reshape(batch, t, 2 * nkv_a)[:, t - keep:]
    a_k_prompt = kv_p[..., :nkv_a].reshape(1, batch, keep, A_KV_HEADS, A_HD)
    a_v_prompt = kv_p[..., nkv_a:].reshape(1, batch, keep, A_KV_HEADS, A_HD)
    a_k_sample = kband[:, dec_t:].reshape(1, dec_batch, past_a, A_KV_HEADS, A_HD)
    a_v_sample = vband[:, dec_t:].reshape(1, dec_batch, past_a, A_KV_HEADS, A_HD)

    lam_init = 0.8 - 0.6 * float(np.exp(-0.3 * 1))
    lp = b_lambda[0].astype(F32)
    lam = jnp.exp(jnp.sum(lp[0] * lp[1])) - jnp.exp(jnp.sum(lp[2] * lp[3])) + lam_init
    wb = B_HEADS * 2 * B_HD
    h, qkvh, k_bp, k_bs, v_bp, v_bs, vt = norm_proj(
        h, None, norm_mix[1], w_b_qkv[0].astype(BF16), None,
        [(0, 3 * wb, "heads", BF16), (wb, 2 * wb, "split", F32), (2 * wb, 3 * wb, "split", F32),
         (2 * wb, 3 * wb, "heads_t", BF16)], tr, n_p, og=og, meta=meta)
    qb_b = min(256, t)
    o_bp = attn_b_prompt(qkvh, vt, lam, b_subln[0], lam_init, batch, t, qb_b)
    o_bs = attn_b_sample(qkvh, cache_b_k[0], cache_b_v[0], lam, b_subln[0], lam_init, n_p, dec_batch, dec_t)
    h, xn, meta, counts = out_router(o_bp, o_bs, h, None, w_b_o[0].astype(BF16), norm_ffn[1], w_route_group[1],
                                     b_route_group[1], w_route_expert[1], b_route_expert[1], True, tr)
    og = moe_experts(xn, meta, counts, w_gate, w_up, w_down, 1, tm)

    y_p, y_s = final_norm(h, og, meta, norm_final, n_p, tr)
    hd2 = 2 * B_HD
    return (y_p.reshape(batch, t, d), y_s.reshape(dec_batch, dec_t, d),
            a_k_prompt, a_v_prompt, a_k_sample, a_v_sample,
            k_bp.reshape(1, batch, t, B_HEADS, hd2), v_bp.reshape(1, batch, t, B_HEADS, hd2),
            k_bs.reshape(1, dec_batch, dec_t, B_HEADS, hd2), v_bs.reshape(1, dec_batch, dec_t, B_HEADS, hd2))
```

```python
import functools

import jax
import jax.numpy as jnp
import numpy as np
from jax import lax
from jax.experimental import pallas as pl
from jax.experimental.pallas import tpu as pltpu
from jax.experimental.pallas import tpu_sc as plsc

F32 = jnp.float32
BF16 = jnp.bfloat16
NEG_INF = float("-inf")

CHUNK = 64
WIN_CHUNKS = 2
A_HEADS, A_KV_HEADS, A_GROUP, A_HD = 16, 4, 4, 64
B_HEADS, B_HD = 8, 64
N_GROUPS, EXPERTS_PER_GROUP = 4, 8
N_EXPERTS = N_GROUPS * EXPERTS_PER_GROUP
NORM_EPS = 1e-6
SUBLN_EPS = 1e-5

LANES = 128
COL_CHUNK = 512
HEAD_GROUP = 4
SC_WINDOW = 128
SC_ROW_WORDS = 256
VMEM_LIMIT = 48 * 1024 * 1024


def _cparams(sem):
    return pltpu.CompilerParams(dimension_semantics=sem, vmem_limit_bytes=VMEM_LIMIT)


def _alibi(n):
    return 2.0 ** (-8.0 * np.arange(1, n + 1) / n)


def _store_planes(ref, x):
    w = ref.shape[2]
    for p in range(ref.shape[0]):
        ref[p] = x[:, p * w:(p + 1) * w]


def _join_planes(ref, first, count):
    return jnp.concatenate([ref[first + p] for p in range(count)], axis=1)


def _add_expert_outputs(x, og_ref, meta_ref):
    parts = og_ref.shape[0] // 2
    meta = meta_ref[...]
    o0 = _unpack_pairs(_join_planes(og_ref, 0, parts))
    o1 = _unpack_pairs(_join_planes(og_ref, parts, parts))
    return (x + meta[:, 2:3] * o0) + meta[:, 3:4] * o1


def _norm_proj_kernel(n_prompt_blocks, two_src, has_bias, combine, outs, *refs):
    refs = list(refs)
    xa_ref = refs.pop(0)
    xb_ref = refs.pop(0) if two_src else None
    og_ref, meta_ref = (refs.pop(0), refs.pop(0)) if combine else (None, None)
    g_ref = refs.pop(0)
    w_ref = refs.pop(0)
    b_ref = refs.pop(0) if has_bias else None
    out_refs = refs
    x = xa_ref[...]
    if two_src:
        x = jnp.where(pl.program_id(0) < n_prompt_blocks, x, xb_ref[...])
    if combine:
        x = _add_expert_outputs(x, og_ref, meta_ref)
        out_refs.pop(0)[...] = x
    ms = jnp.mean(x * x, axis=-1, keepdims=True)
    xn = (x * lax.rsqrt(ms + NORM_EPS) * g_ref[...]).astype(BF16)
    is_prompt = pl.program_id(0) < n_prompt_blocks
    out_refs = list(out_refs)
    targets = []
    for c0, c1, layout in outs:
        targets.append((c0, c1, layout, out_refs.pop(0), out_refs.pop(0) if layout == "split" else None))
    for s in range(0, w_ref.shape[1], COL_CHUNK):
        e = s + COL_CHUNK
        r = jnp.dot(xn, w_ref[:, s:e], preferred_element_type=F32)
        if has_bias:
            r = r + b_ref[:, s:e]
        for c0, c1, layout, o_ref, o2_ref in targets:
            if not (c0 <= s and e <= c1):
                continue
            rr = r.astype(o_ref.dtype)
            if layout == "flat":
                o_ref[:, s - c0:e - c0] = rr
            elif layout == "split":

                @pl.when(is_prompt)
                def _(rr=rr, o_ref=o_ref, s=s, e=e, c0=c0):
                    o_ref[:, s - c0:e - c0] = rr

                @pl.when(jnp.logical_not(is_prompt))
                def _(rr=rr, o2_ref=o2_ref, s=s, e=e, c0=c0):
                    o2_ref[:, s - c0:e - c0] = rr
            elif layout == "heads":
                for hh in range((e - s) // LANES):
                    o_ref[(s - c0) // LANES + hh] = rr[:, hh * LANES:(hh + 1) * LANES]
            else:
                for hh in range((e - s) // LANES):
                    o_ref[(s - c0) // LANES + hh] = r[:, hh * LANES:(hh + 1) * LANES].T.astype(o_ref.dtype)


def norm_proj(xa, xb, g, w, b, outs, tr, n_prompt, og=None, meta=None):
    na, d = xa.shape
    nb = 0 if xb is None else xb.shape[0]
    n = na + nb
    npb = n_prompt // tr
    two = xb is not None
    combine = og is not None
    in_specs = [pl.BlockSpec((tr, d), (lambda i: (jnp.minimum(i, npb - 1), 0)) if two else (lambda i: (i, 0)))]
    args = [xa]
    if two:
        in_specs.append(pl.BlockSpec((tr, d), lambda i: (jnp.maximum(i - npb, 0), 0)))
        args.append(xb)
    if combine:
        in_specs += [pl.BlockSpec((og.shape[0], tr, og.shape[2]), lambda i: (0, i, 0)),
                     pl.BlockSpec((tr, LANES), lambda i: (i, 0))]
        args += [og, meta]
    in_specs.append(pl.BlockSpec((1, d), lambda i: (0, 0)))
    args.append(g.reshape(1, d).astype(F32))
    in_specs.append(pl.BlockSpec(w.shape, lambda i: (0, 0)))
    args.append(w)
    if b is not None:
        in_specs.append(pl.BlockSpec((1, w.shape[1]), lambda i: (0, 0)))
        args.append(b.reshape(1, -1).astype(F32))
    out_shapes, out_specs = [], []
    if combine:
        out_shapes.append(jax.ShapeDtypeStruct((n, d), F32))
        out_specs.append(pl.BlockSpec((tr, d), lambda i: (i, 0)))
    for c0, c1, layout, dt in outs:
        assert c0 % COL_CHUNK == 0 and c1 % COL_CHUNK == 0
        if layout == "flat":
            out_shapes.append(jax.ShapeDtypeStruct((n, c1 - c0), dt))
            out_specs.append(pl.BlockSpec((tr, c1 - c0), lambda i: (i, 0)))
        elif layout == "split":
            out_shapes.append(jax.ShapeDtypeStruct((n_prompt, c1 - c0), dt))
            out_specs.append(pl.BlockSpec((tr, c1 - c0), lambda i: (jnp.minimum(i, npb - 1), 0)))
            out_shapes.append(jax.ShapeDtypeStruct((n - n_prompt, c1 - c0), dt))
            out_specs.append(pl.BlockSpec((tr, c1 - c0), lambda i: (jnp.maximum(i - npb, 0), 0)))
        elif layout == "heads":
            nh = (c1 - c0) // LANES
            out_shapes.append(jax.ShapeDtypeStruct((nh, n, LANES), dt))
            out_specs.append(pl.BlockSpec((nh, tr, LANES), lambda i: (0, i, 0)))
        else:
            assert layout == "heads_t"
            nh = (c1 - c0) // LANES
            out_shapes.append(jax.ShapeDtypeStruct((nh, LANES, n), dt))
            out_specs.append(pl.BlockSpec((nh, LANES, tr), lambda i: (0, 0, i)))
    kern = functools.partial(_norm_proj_kernel, npb, two, b is not None, combine,
                             [(c0, c1, lay) for c0, c1, lay, _ in outs])
    return pl.pallas_call(
        kern, grid=(n // tr,), in_specs=in_specs, out_specs=out_specs, out_shape=out_shapes,
        compiler_params=_cparams(("arbitrary",)), name="norm_proj")(*args)


def _attn_a_chunk(q, k, v, bias_ref, sink_ref, valid):
    nq = q.shape[0]
    outs = []
    for kh in range(A_KV_HEADS):
        qs = jnp.concatenate(
            [q[:, (kh * A_GROUP + g) * A_HD:(kh * A_GROUP + g + 1) * A_HD] for g in range(A_GROUP)], axis=0)
        qs = qs * jnp.asarray(A_HD ** -0.5, BF16)
        kk = k[:, kh * A_HD:(kh + 1) * A_HD]
        s = lax.dot_general(qs, kk, (((1,), (1,)), ((), ())), preferred_element_type=F32) + bias_ref[kh]
        if valid is not None:
            s = jnp.where(valid, s, NEG_INF)
        sk = sink_ref[kh]
        m = jnp.maximum(jnp.max(s, axis=-1, keepdims=True), sk)
        e = jnp.exp(s - m)
        den = jnp.sum(e, axis=-1, keepdims=True) + jnp.exp(sk - m)
        p = (e * (1.0 / den)).astype(BF16)
        o = jnp.dot(p, v[:, kh * A_HD:(kh + 1) * A_HD], preferred_element_type=F32)
        for g in range(A_GROUP):
            outs.append(o[g * nq:(g + 1) * nq])
    return jnp.concatenate(outs, axis=1)


def _attn_a_prompt_kernel(qb, q_ref, kvc_ref, kvp_ref, bias_ref, sink_ref, o_ref):
    cb = pl.program_id(1)
    back = CHUNK * WIN_CHUNKS
    nk = back + qb
    nkv = A_KV_HEADS * A_HD
    kvfull = jnp.concatenate([kvp_ref[...], kvc_ref[...]], axis=0)
    k_all = kvfull[:, :nkv].astype(BF16)
    vt_all = kvfull[:, nkv:].T.astype(BF16)
    key_pos = lax.broadcasted_iota(jnp.int32, (nk, 1), 0) + (cb * qb - back)
    before_start = jnp.where(key_pos >= 0, 0.0, NEG_INF)
    dn = (((1,), (1,)), ((), ()))
    outs = []
    for kh in range(A_KV_HEADS):
        qs = jnp.concatenate(
            [q_ref[:, (kh * A_GROUP + g) * A_HD:(kh * A_GROUP + g + 1) * A_HD] for g in range(A_GROUP)], axis=0)
        qs = qs * jnp.asarray(A_HD ** -0.5, BF16)
        st = lax.dot_general(k_all[:, kh * A_HD:(kh + 1) * A_HD], qs, dn, preferred_element_type=F32)
        st = st + bias_ref[kh] + before_start
        sk = sink_ref[kh]
        m = jnp.maximum(jnp.max(st, axis=0, keepdims=True), sk)
        e = jnp.exp(st - m)
        den = jnp.sum(e, axis=0, keepdims=True) + jnp.exp(sk - m)
        p = (e * (1.0 / den)).astype(BF16)
        ot = jnp.dot(vt_all[kh * A_HD:(kh + 1) * A_HD, :], p, preferred_element_type=F32)
        o = ot.T
        for g in range(A_GROUP):
            outs.append(o[g * qb:(g + 1) * qb])
    o_ref[...] = jnp.concatenate(outs, axis=1).astype(o_ref.dtype)


def _a_prompt_tables(qb, sinks):
    back = CHUNK * WIN_CHUNKS
    slopes = _alibi(A_HEADS).reshape(A_KV_HEADS, A_GROUP)
    qpos = np.arange(qb)
    kpos = np.arange(back + qb) - back
    dist = np.abs(kpos[:, None] - qpos[None, :]).astype(np.float64)
    band_lo = (qpos // CHUNK) * CHUNK - back
    in_band = (kpos[:, None] >= band_lo[None, :]) & (kpos[:, None] < band_lo[None, :] + back + CHUNK)
    bias = np.where(in_band[None, None], -slopes[:, :, None, None] * dist[None, None], -np.inf)
    bias = np.transpose(bias, (0, 2, 1, 3)).reshape(A_KV_HEADS, back + qb, A_GROUP * qb)
    sk = jnp.repeat(sinks.astype(F32).reshape(A_KV_HEADS, A_GROUP), qb, axis=1)[:, None, :]
    return jnp.asarray(bias, F32), sk


def _a_tables(qpos, kpos, sinks):
    slopes = _alibi(A_HEADS).reshape(A_KV_HEADS, A_GROUP)
    dist = np.abs(qpos[:, None] - kpos[None, :]).astype(np.float64)
    bias = -slopes[:, :, None, None] * dist
    nq = len(qpos)
    bias = jnp.asarray(bias.reshape(A_KV_HEADS, A_GROUP * nq, len(kpos)), F32)
    sk = jnp.repeat(sinks.astype(F32).reshape(A_KV_HEADS, A_GROUP), nq, axis=1)[..., None]
    return bias, sk


def attn_a_prompt(q_all, kv_all, sinks, batch, t, qb):
    nq = A_HEADS * A_HD
    nkv2 = 2 * A_KV_HEADS * A_HD
    back = CHUNK * WIN_CHUNKS
    bias, sk = _a_prompt_tables(qb, sinks)
    nblk = t // qb
    r = qb // back
    return pl.pallas_call(
        functools.partial(_attn_a_prompt_kernel, qb),
        grid=(batch, nblk),
        in_specs=[
            pl.BlockSpec((qb, nq), lambda b, i: (b * nblk + i, 0)),
            pl.BlockSpec((qb, nkv2), lambda b, i: (b * nblk + i, 0)),
            pl.BlockSpec((back, nkv2), lambda b, i: (jnp.maximum((b * nblk + i) * r - 1, 0), 0)),
            pl.BlockSpec(bias.shape, lambda b, i: (0, 0, 0)),
            pl.BlockSpec(sk.shape, lambda b, i: (0, 0, 0)),
        ],
        out_specs=pl.BlockSpec((qb, nq), lambda b, i: (b * nblk + i, 0)),
        out_shape=jax.ShapeDtypeStruct((batch * t, nq), BF16),
        compiler_params=_cparams(("arbitrary", "arbitrary")), name="attn_a_prompt",
    )(q_all, kv_all, kv_all, bias, sk)


def _attn_a_sample_kernel(q_ref, k_ref, v_ref, bias_ref, sink_ref, o_ref):
    o = _attn_a_chunk(q_ref[...], k_ref[...].astype(BF16), v_ref[...].astype(BF16), bias_ref, sink_ref, None)
    o_ref[...] = o.astype(o_ref.dtype)


def attn_a_sample(q_all, kband, vband, sinks, row0, dec_batch, dec_t):
    nq = A_HEADS * A_HD
    s = kband.shape[1]
    past = s - dec_t
    kpos = np.concatenate([np.arange(past) - past, np.arange(dec_t)])
    bias, sk = _a_tables(np.arange(dec_t), kpos, sinks)
    blk0 = row0 // dec_t
    return pl.pallas_call(
        _attn_a_sample_kernel,
        grid=(dec_batch,),
        in_specs=[
            pl.BlockSpec((dec_t, nq), lambda b: (blk0 + b, 0)),
            pl.BlockSpec((None, s, kband.shape[2]), lambda b: (b, 0, 0)),
            pl.BlockSpec((None, s, vband.shape[2]), lambda b: (b, 0, 0)),
            pl.BlockSpec(bias.shape, lambda b: (0, 0, 0)),
            pl.BlockSpec(sk.shape, lambda b: (0, 0, 0)),
        ],
        out_specs=pl.BlockSpec((dec_t, nq), lambda b: (b, 0)),
        out_shape=jax.ShapeDtypeStruct((dec_batch * dec_t, nq), BF16),
        compiler_params=_cparams(("arbitrary",)), name="attn_a_sample",
    )(q_all, kband, vband, bias, sk)


B_Q_SCALE = B_HD ** -0.5 * float(np.log2(np.e))
LOG2E = float(np.log2(np.e))


def _stack_maps(qh):
    lane = lax.broadcasted_iota(jnp.int32, qh.shape, 1)
    zero = jnp.zeros_like(qh)
    return jnp.concatenate([jnp.where(lane < B_HD, qh, zero), jnp.where(lane >= B_HD, qh, zero)], axis=0)


def _diff_finish(acc, l, nq, lam, lam_init, subln):
    inv = 1.0 / l
    o = acc[:nq] * inv[:nq] - lam * (acc[nq:] * inv[nq:])
    ms = jnp.mean(o * o, axis=-1, keepdims=True)
    return o * lax.rsqrt(ms + SUBLN_EPS) * subln * (1.0 - lam_init)


def _attn_b_prompt_kernel(qb, lam_init, slope_ref, lam_ref, q_ref, k_ref, vt_ref, kcol_ref, tri_ref, msk_ref,
                          subln_ref, o_ref, acc_ref):
    i = pl.program_id(1)
    lam = lam_ref[0]
    dn = (((1,), (1,)), ((), ()))

    def group_body(hg, carry):
        heads = [hg * HEAD_GROUP + u for u in range(HEAD_GROUP)]
        slopes = [slope_ref[h] for h in heads]
        qqs = [_stack_maps(q_ref[h]) for h in heads]

        def kv_step(j, c, diag):
            out = []
            for u, h in enumerate(heads):
                m, l = c[u]
                slope = slopes[u]
                start = pl.multiple_of(j * qb, qb)
                kb = k_ref[h, pl.ds(start, qb), :]
                vtb = vt_ref[h, :, pl.ds(start, qb)]
                t = lax.dot_general(kb, qqs[u], dn, preferred_element_type=F32) + kcol_ref[h]
                if diag:
                    t = t + slope * tri_ref[...] + msk_ref[...]
                off = slope * jnp.full((1, 2 * qb), (j - i) * qb, jnp.int32).astype(F32)
                m_new = jnp.maximum(m, jnp.max(t, axis=0, keepdims=True) + off)
                alpha = jnp.exp2(m - m_new)
                p = jnp.exp2(t + (off - m_new))
                l = alpha * l + jnp.sum(p, axis=0, keepdims=True)
                acc_ref[u] = alpha * acc_ref[u] + jnp.dot(vtb, p.astype(BF16), preferred_element_type=F32)
                out.append((m_new, l))
            return tuple(out)

        acc_ref[...] = jnp.zeros_like(acc_ref)
        init = tuple((jnp.full((1, 2 * qb), NEG_INF, F32), jnp.zeros((1, 2 * qb), F32)) for _ in heads)
        c = lax.fori_loop(0, i, lambda j, c: kv_step(j, c, False), init)
        c = kv_step(i, c, True)
        for u, h in enumerate(heads):
            inv = 1.0 / c[u][1]
            acc = acc_ref[u]
            ot = acc[:, :qb] * inv[:, :qb] - lam * (acc[:, qb:] * inv[:, qb:])
            ms = jnp.mean(ot * ot, axis=0, keepdims=True)
            ot = ot * lax.rsqrt(ms + SUBLN_EPS) * subln_ref[...] * (1.0 - lam_init)
            o_ref[h] = ot.T.astype(o_ref.dtype)
        return carry

    lax.fori_loop(0, B_HEADS // HEAD_GROUP, group_body, 0)


def attn_b_prompt(qkvh, vt, lam, subln, lam_init, batch, t, qb):
    nblk = t // qb
    qq = np.arange(qb)
    kcol = (_alibi(B_HEADS) * LOG2E)[:, None, None] * np.broadcast_to(qq[:, None], (qb, 2 * qb))[None]
    tri =-2.0 * np.maximum(qq[:, None] - qq[None, :], 0).astype(np.float32)
    msk = np.where((qq[:, None] // CHUNK) <= (qq[None, :] // CHUNK), 0.0, -np.inf).astype(np.float32)
    tri = jnp.asarray(np.concatenate([tri, tri], axis=1))
    msk = jnp.asarray(np.concatenate([msk, msk], axis=1))
    kcol = jnp.asarray(kcol, F32)
    slopes =jnp.asarray(_alibi(B_HEADS) * LOG2E, F32)
    subln_t = jnp.broadcast_to(subln.astype(F32).reshape(LANES, 1), (LANES, qb))
    const = lambda b, i, *_: (0, 0)
    grid_spec = pltpu.PrefetchScalarGridSpec(
        num_scalar_prefetch=2, grid=(batch, nblk),
        in_specs=[
            pl.BlockSpec((B_HEADS, qb, LANES), lambda b, i, *_: (0, b * nblk + i, 0)),
            pl.BlockSpec((B_HEADS, t, LANES), lambda b, i, *_: (1, b, 0)),
            pl.BlockSpec((B_HEADS, LANES, t), lambda b, i, *_: (0, 0, b)),
            pl.BlockSpec(kcol.shape, lambda b, i, *_: (0, 0, 0)),
            pl.BlockSpec(tri.shape, const),
            pl.BlockSpec(msk.shape, const),
            pl.BlockSpec(subln_t.shape, const),
        ],
        out_specs=pl.BlockSpec((B_HEADS, qb, LANES), lambda b, i, *_: (0, b * nblk + i, 0)),
        scratch_shapes=[pltpu.VMEM((HEAD_GROUP, LANES, 2 * qb), F32)],
    )
    return pl.pallas_call(
        functools.partial(_attn_b_prompt_kernel, qb, lam_init), grid_spec=grid_spec,
        out_shape=jax.ShapeDtypeStruct((B_HEADS, batch * t, LANES), BF16),
        compiler_params=_cparams(("arbitrary", "arbitrary")), name="attn_b_prompt",
    )(slopes, lam.reshape(1).astype(F32), qkvh, qkvh, vt, kcol, tri, msk, subln_t)


def _attn_b_sample_kernel(dec_t, past, lam_init, slopes, lam_ref, q_ref, kn_ref, vn_ref, ck_ref, cv_ref,
                          tnew_ref, subln_ref, o_ref):
    lam = lam_ref[0]
    subln = subln_ref[...]
    colc = lax.broadcasted_iota(jnp.int32, (1, past), 1).astype(F32) - float(past)
    for h in range(B_HEADS):
        slope = float(slopes[h]) * LOG2E
        qq = _stack_maps(q_ref[h])
        kc = ck_ref[pl.ds(h, past, stride=B_HEADS), :].astype(BF16)
        vc = cv_ref[pl.ds(h, past, stride=B_HEADS), :].astype(BF16)
        dn = (((1,), (1,)), ((), ()))
        s_c = lax.dot_general(qq, kc, dn, preferred_element_type=F32) + slope * colc
        s_n = lax.dot_general(qq, kn_ref[h], dn, preferred_element_type=F32) + slope * tnew_ref[...]
        m = jnp.maximum(jnp.max(s_c, axis=-1, keepdims=True), jnp.max(s_n, axis=-1, keepdims=True))
        p_c = jnp.exp2(s_c - m)
        p_n = jnp.exp2(s_n - m)
        l = jnp.sum(p_c, axis=-1, keepdims=True) + jnp.sum(p_n, axis=-1, keepdims=True)
        acc = (jnp.dot(p_c.astype(BF16), vc, preferred_element_type=F32)
               + jnp.dot(p_n.astype(BF16), vn_ref[h], preferred_element_type=F32))
        o_ref[h] = _diff_finish(acc, l, dec_t, lam, lam_init, subln).astype(o_ref.dtype)


def attn_b_sample(qkvh, cache_k, cache_v, lam, subln, lam_init, row0, dec_batch, dec_t):
    past = cache_k.shape[1]
    qi = np.arange(dec_t)
    tnew = (qi[:, None] - np.abs(qi[:, None] - qi[None, :])).astype(np.float32)
    tnew = jnp.asarray(np.concatenate([tnew, tnew], axis=0))
    blk0 = row0 // dec_t
    rows = past * B_HEADS
    cache_k = cache_k.reshape(dec_batch, rows, LANES)
    cache_v = cache_v.reshape(dec_batch, rows, LANES)
    grid_spec = pltpu.PrefetchScalarGridSpec(
        num_scalar_prefetch=1, grid=(dec_batch,),
        in_specs=[
            pl.BlockSpec((B_HEADS, dec_t, LANES), lambda b, *_: (0, blk0 + b, 0)),
            pl.BlockSpec((B_HEADS, dec_t, LANES), lambda b, *_: (1, blk0 + b, 0)),
            pl.BlockSpec((B_HEADS, dec_t, LANES), lambda b, *_: (2, blk0 + b, 0)),
            pl.BlockSpec((None, rows, LANES), lambda b, *_: (b, 0, 0)),
            pl.BlockSpec((None, rows, LANES), lambda b, *_: (b, 0, 0)),
            pl.BlockSpec(tnew.shape, lambda b, *_: (0, 0)),
            pl.BlockSpec((1, LANES), lambda b, *_: (0, 0)),
        ],
        out_specs=pl.BlockSpec((B_HEADS, dec_t, LANES), lambda b, *_: (0, b, 0)),
    )
    return pl.pallas_call(
        functools.partial(_attn_b_sample_kernel, dec_t, past, lam_init, tuple(_alibi(B_HEADS))),
        grid_spec=grid_spec,
        out_shape=jax.ShapeDtypeStruct((B_HEADS, dec_batch * dec_t, LANES), BF16),
        compiler_params=_cparams(("arbitrary",)), name="attn_b_sample",
    )(lam.reshape(1).astype(F32), qkvh, qkvh, qkvh, cache_k, cache_v, tnew, subln.reshape(1, LANES).astype(F32))


def _route(logits):
    lane = lax.broadcasted_iota(jnp.int32, logits.shape, 1).astype(F32)
    big = float(LANES)
    lg = jnp.where(lane < N_GROUPS, logits, NEG_INF)
    gmax = jnp.max(lg, axis=-1, keepdims=True)
    g_idx = jnp.min(jnp.where(lg == gmax, lane, big), axis=-1, keepdims=True)
    g_prob = 1.0 / jnp.sum(jnp.exp(lg - gmax), axis=-1, keepdims=True)
    lo = N_GROUPS + EXPERTS_PER_GROUP * g_idx
    le = jnp.where((lane >= lo) & (lane < lo + EXPERTS_PER_GROUP), logits, NEG_INF)
    m1 = jnp.max(le, axis=-1, keepdims=True)
    i1 = jnp.min(jnp.where(le == m1, lane, big), axis=-1, keepdims=True)
    le2 = jnp.where(lane == i1, NEG_INF, le)
    m2 = jnp.max(le2, axis=-1, keepdims=True)
    i2 = jnp.min(jnp.where(le2 == m2, lane, big), axis=-1, keepdims=True)
    t = jnp.exp(m2 - m1)
    ga = 1.0 / (1.0 + t)
    gb = t * ga
    return lane, i1, i2, g_prob * ga, g_prob * gb


def _unpack_pairs(u):
    lo = pltpu.bitcast(u << 16, F32)
    hi = pltpu.bitcast(u & jnp.uint32(0xFFFF0000), F32)
    return jnp.concatenate([lo, hi], axis=1)


def _pack_pairs(x):
    bits = pltpu.bitcast(x.astype(BF16).astype(F32), jnp.uint32)
    w = x.shape[1] // 2
    return (bits[:, :w] >> 16) | bits[:, w:]


def _out_router_kernel(n_prompt_blocks, heads_in, two_o, two_res, *refs):
    refs = list(refs)
    oa_ref = refs.pop(0)
    ob_ref = refs.pop(0) if two_o else None
    ra_ref = refs.pop(0)
    rb_ref = refs.pop(0) if two_res else None
    wo_ref, g_ref, wr_ref, wr2_ref, br_ref, ltri_ref, h_ref, xn_ref, meta_ref, cnt_ref, carry_ref = refs
    is_prompt = pl.program_id(0) < n_prompt_blocks

    def load_o(ref):
        if heads_in:
            return jnp.concatenate([ref[hh] for hh in range(ref.shape[0])], axis=1)
        return ref[...]

    o = load_o(oa_ref)
    if two_o:
        o = jnp.where(is_prompt, o, load_o(ob_ref))
    res = ra_ref[...]
    if two_res:
        res = jnp.where(is_prompt, res, rb_ref[...])
    h = res + jnp.dot(o, wo_ref[...], preferred_element_type=F32)
    h_ref[...] = h
    ms = jnp.mean(h * h, axis=-1, keepdims=True)
    xn = h * lax.rsqrt(ms + NORM_EPS) * g_ref[...]
    x_hi = xn.astype(BF16)
    _store_planes(xn_ref, _pack_pairs(xn))
    x_lo = (xn - x_hi.astype(F32)).astype(BF16)
    r = jnp.dot(x_hi, wr_ref[...], preferred_element_type=F32)
    logits = (r[:, :LANES] + r[:, LANES:]) + jnp.dot(x_lo, wr2_ref[...], preferred_element_type=F32) + br_ref[...]
    lane, i1, i2, g1, g2 = _route(logits)

    @pl.when(pl.program_id(0) == 0)
    def _():
        carry_ref[...] = jnp.zeros_like(carry_ref)

    sel1 = lane == i1
    sel2 = lane == i2
    onehot = jnp.where(sel1 | sel2, 1.0, 0.0)
    before = jnp.dot(ltri_ref[...], onehot.astype(BF16), preferred_element_type=F32) + carry_ref[...]
    rank1 = jnp.sum(jnp.where(sel1, before, 0.0), axis=-1, keepdims=True)
    rank2 = jnp.sum(jnp.where(sel2, before, 0.0), axis=-1, keepdims=True)
    carry_ref[...] = carry_ref[...] + jnp.sum(onehot, axis=0, keepdims=True)
    cnt_ref[...] = carry_ref[...]
    cols = [i1 - N_GROUPS, i2 - N_GROUPS, g1, g2, rank1, rank2]
    meta = jnp.zeros_like(logits)
    for c, v in enumerate(cols):
        meta = jnp.where(lane == c, v, meta)
    meta_ref[...] = meta


def out_router(oa, ob, ra, rb, wo, g, w_rg, b_rg, w_re, b_re, heads_in, tr):
    d = ra.shape[1]
    if heads_in:
        na = oa.shape[1]
        nb = 0 if ob is None else ob.shape[1]
    else:
        na = oa.shape[0]
        nb = 0 if ob is None else ob.shape[0]
    n = na + nb
    npb = na // tr
    two_o = ob is not None
    two_res = rb is not None
    first = lambda i: (jnp.minimum(i, npb - 1), 0)
    second = lambda i: (jnp.maximum(i - npb, 0), 0)
    plain = lambda i: (i, 0)
    in_specs, args = [], []

    def add_o(x, imap):
        if heads_in:
            in_specs.append(pl.BlockSpec((x.shape[0], tr, LANES), lambda i: (0, imap(i)[0], 0)))
        else:
            in_specs.append(pl.BlockSpec((tr, x.shape[1]), imap))
        args.append(x)

    add_o(oa, first if two_o else plain)
    if two_o:
        add_o(ob, second)
    in_specs.append(pl.BlockSpec((tr, d), first if two_res else plain))
    args.append(ra)
    if two_res:
        in_specs.append(pl.BlockSpec((tr, d), second))
        args.append(rb)
    wr = jnp.zeros((d, LANES), F32).at[:, :N_GROUPS].set(w_rg.astype(F32))
    wr = wr.at[:, N_GROUPS:N_GROUPS + N_EXPERTS].set(w_re.astype(F32))
    wr_hi = wr.astype(BF16)
    wr_lo = (wr - wr_hi.astype(F32)).astype(BF16)
    br = jnp.zeros((1, LANES), F32).at[0, :N_GROUPS].set(b_rg.astype(F32))
    br = br.at[0, N_GROUPS:N_GROUPS + N_EXPERTS].set(b_re.astype(F32))
    ltri = jnp.asarray(np.tril(np.ones((tr, tr), np.float32), -1), BF16)
    consts = [wo, g.reshape(1, d).astype(F32), jnp.concatenate([wr_hi, wr_lo], axis=1), wr_hi, br, ltri]
    for cst in consts:
        in_specs.append(pl.BlockSpec(cst.shape, lambda i: (0, 0)))
        args.append(cst)
    parts = d // 2 // SC_ROW_WORDS
    out_shape = [jax.ShapeDtypeStruct((n, d), F32), jax.ShapeDtypeStruct((parts, n, SC_ROW_WORDS), jnp.uint32),
                 jax.ShapeDtypeStruct((n, LANES), F32), jax.ShapeDtypeStruct((1, LANES), F32)]
    out_specs = [pl.BlockSpec((tr, d), plain), pl.BlockSpec((parts, tr, SC_ROW_WORDS), lambda i: (0, i, 0)),
                 pl.BlockSpec((tr, LANES), plain), pl.BlockSpec((1, LANES), lambda i: (0, 0))]
    return pl.pallas_call(
        functools.partial(_out_router_kernel, npb, heads_in, two_o, two_res),
        grid=(n // tr,), in_specs=in_specs, out_specs=out_specs, out_shape=out_shape,
        scratch_shapes=[pltpu.VMEM((1, LANES), F32)],
        compiler_params=_cparams(("arbitrary",)), name="out_router")(*args)


def _expert_kernel(te_ref, nt_ref, x_ref, wg_ref, wu_ref, wd_ref, o_ref):
    @pl.when(pl.program_id(0) < nt_ref[0])
    def _():
        x = _unpack_pairs(_join_planes(x_ref, 0, x_ref.shape[0])).astype(BF16)
        a = jnp.dot(x, wg_ref[...].astype(BF16), preferred_element_type=F32)
        u = jnp.dot(x, wu_ref[...].astype(BF16), preferred_element_type=F32)
        hh = (a * (1.0 / (1.0 + jnp.exp(-a))) * u).astype(BF16)
        _store_planes(o_ref, _pack_pairs(jnp.dot(hh, wd_ref[...].astype(BF16), preferred_element_type=F32)))

    @pl.when(pl.program_id(0) >= nt_ref[0])
    def _():
        o_ref[...] = jnp.zeros_like(o_ref)


def expert_mlp(xs, tile_expert, n_tiles, w_gate, w_up, w_down, layer, tm):
    parts, p, w = xs.shape
    d, f = w_gate.shape[-2:]
    epg = w_gate.shape[2]
    wmap = lambda t, te, nt: (layer, te[t] // epg, te[t] % epg, 0, 0)
    grid_spec = pltpu.PrefetchScalarGridSpec(
        num_scalar_prefetch=2, grid=(p // tm,),
        in_specs=[
            pl.BlockSpec((parts, tm, w), lambda t, te, nt: (0, t, 0)),
            pl.BlockSpec((None, None, None, d, f), wmap),
            pl.BlockSpec((None, None, None, d, f), wmap),
            pl.BlockSpec((None, None, None, f, d), wmap),
        ],
        out_specs=pl.BlockSpec((parts, tm, w), lambda t, te, nt: (0, t, 0)),
    )
    return pl.pallas_call(
        _expert_kernel, grid_spec=grid_spec, out_shape=jax.ShapeDtypeStruct((parts, p, w), jnp.uint32),
        compiler_params=_cparams(("arbitrary",)), name="expert_mlp")(tile_expert, n_tiles, xs, w_gate, w_up, w_down)


def sc_scatter_rows(x2, idx, n_out):
    mesh = plsc.VectorSubcoreMesh(core_axis_name="c", subcore_axis_name="s")
    n_src_win = x2.shape[0] // SC_WINDOW

    @functools.partial(pl.kernel, out_type=jax.ShapeDtypeStruct((n_out, SC_ROW_WORDS), x2.dtype), mesh=mesh)
    def scatter(x_hbm, i_hbm, o_hbm):
        def body(x_vmem, i_vmem):
            pltpu.sync_copy(x_vmem, o_hbm.at[i_vmem.at[0]])

        pltpu.emit_pipeline(
            body, grid=(n_out // SC_WINDOW,),
            in_specs=[pl.BlockSpec((SC_WINDOW, SC_ROW_WORDS), lambda i: (lax.rem(i, n_src_win), 0)),
                      pl.BlockSpec((1, SC_WINDOW), lambda i: (0, i))],
            out_specs=[], core_axis_name=("c", "s"), dimension_semantics=(pltpu.PARALLEL,),
        )(x_hbm, i_hbm)

    return scatter(x2, idx)


def sc_gather_rows(x2, idx):
    mesh = plsc.VectorSubcoreMesh(core_axis_name="c", subcore_axis_name="s")
    n_out = idx.shape[1]

    @functools.partial(pl.kernel, out_type=jax.ShapeDtypeStruct((n_out, SC_ROW_WORDS), x2.dtype), mesh=mesh)
    def gather(x_hbm, i_hbm, o_hbm):
        def body(i_vmem, o_vmem):
            pltpu.sync_copy(x_hbm.at[i_vmem.at[0]], o_vmem)

        pltpu.emit_pipeline(
            body, grid=(n_out // SC_WINDOW,),
            in_specs=[pl.BlockSpec((1, SC_WINDOW), lambda i: (0, i))],
            out_specs=[pl.BlockSpec((SC_WINDOW, SC_ROW_WORDS), lambda i: (i, 0))],
            core_axis_name=("c", "s"), dimension_semantics=(pltpu.PARALLEL,),
        )(i_hbm, o_hbm)

    return gather(x2, idx)


def moe_dispatch(xn_packed, meta, counts, tm):
    parts, n, w = xn_packed.shape
    e = meta[:, 0:2].astype(jnp.int32)
    rank = meta[:, 4:6].astype(jnp.int32)
    counts = counts[0, N_GROUPS:N_GROUPS + N_EXPERTS].astype(jnp.int32)
    padded = ((counts + tm - 1) // tm) * tm
    pend = jnp.cumsum(padded)
    pstart = pend - padded
    ids = jnp.arange(N_EXPERTS, dtype=jnp.int32)
    pos = jnp.sum(jnp.where(e[..., None] == ids, pstart, 0), axis=-1) + rank
    p = 2 * n + N_EXPERTS * tm
    n_pad = p - 2 * n
    gap_len = jnp.concatenate([padded - counts, (p - pend[-1]).reshape(1)])
    gap_first = jnp.concatenate([pstart + counts, pend[-1:]])
    gap_end = jnp.cumsum(gap_len)
    k = jnp.arange(n_pad, dtype=jnp.int32)
    gap = jnp.sum((k[:, None] >= gap_end[None, :]).astype(jnp.int32), axis=1)
    shift = gap_first - (gap_end - gap_len)
    pad_pos = k + jnp.sum(jnp.where(gap[:, None] == jnp.arange(N_EXPERTS + 1), shift, 0), axis=-1)
    planes = lambda rows: jnp.concatenate([rows + q * p for q in range(parts)])
    scat_idx = jnp.concatenate([planes(pos[:, 0]), planes(pos[:, 1]), planes(pad_pos)]).reshape(1, parts * p)
    gath_idx = jnp.concatenate([planes(pos[:, 0]), planes(pos[:, 1])]).reshape(1, 2 * parts * n)
    xs = sc_scatter_rows(xn_packed.reshape(parts * n, w), scat_idx, parts * p).reshape(parts, p, w)
    tile_start = jnp.arange(p // tm, dtype=jnp.int32) * tm
    tile_expert = jnp.sum((tile_start[:, None] >= pend[None, :]).astype(jnp.int32), axis=1)
    tile_expert = jnp.minimum(tile_expert, N_EXPERTS - 1)
    n_tiles = (pend[-1] // tm).astype(jnp.int32).reshape(1)
    return xs, tile_expert, n_tiles, gath_idx


def moe_experts(xn_packed, meta, counts, w_gate, w_up, w_down, layer, tm):
    parts, n, w = xn_packed.shape
    xs, tile_expert, n_tiles, gath_idx = moe_dispatch(xn_packed, meta, counts, tm)
    o = expert_mlp(xs, tile_expert, n_tiles, w_gate, w_up, w_down, layer, tm)
    return sc_gather_rows(o.reshape(parts * o.shape[1], w), gath_idx).reshape(2 * parts, n, w)


def _final_norm_kernel(n_prompt_blocks, h_ref, og_ref, meta_ref, g_ref, yp_ref, ys_ref):
    x = _add_expert_outputs(h_ref[...], og_ref, meta_ref)
    ms = jnp.mean(x * x, axis=-1, keepdims=True)
    y = x * lax.rsqrt(ms + NORM_EPS) * g_ref[...]
    i = pl.program_id(0)

    @pl.when(i < n_prompt_blocks)
    def _():
        yp_ref[...] = y

    @pl.when(i >= n_prompt_blocks)
    def _():
        ys_ref[...] = y


def final_norm(h, og, meta, g, n_prompt, tr):
    n, d = h.shape
    npb = n_prompt // tr
    return pl.pallas_call(
        functools.partial(_final_norm_kernel, npb),
        grid=(n // tr,),
        in_specs=[pl.BlockSpec((tr, d), lambda i: (i, 0)),
                  pl.BlockSpec((og.shape[0], tr, og.shape[2]), lambda i: (0, i, 0)),
                  pl.BlockSpec((tr, LANES), lambda i: (i, 0)), pl.BlockSpec((1, d), lambda i: (0, 0))],
        out_specs=[pl.BlockSpec((tr, d), lambda i: (jnp.minimum(i, npb - 1), 0)),
                   pl.BlockSpec((tr, d), lambda i: (jnp.maximum(i - npb, 0), 0))],
        out_shape=[jax.ShapeDtypeStruct((n_prompt, d), F32), jax.ShapeDtypeStruct((n - n_prompt, d), F32)],
        compiler_params=_cparams(("arbitrary",)), name="final_norm")(h, og, meta, g.reshape(1, d).astype(F32))


def _pick_rows_tile(n_prompt, n_sample):
    for tr in (512, 256, 128, 64, 32, 16, 8):
        if n_prompt % tr == 0 and n_sample % tr == 0:
            return tr
    raise ValueError("row counts must be multiples of 8")


def kernel(x_prompt, x_sample, cache_a_k, cache_a_v, cache_b_k, cache_b_v, norm_mix, norm_ffn, norm_final,
           w_a_qkv, b_a_qkv, a_sinks, w_a_o, w_b_qkv, b_lambda, b_subln, w_b_o,
           w_route_group, b_route_group, w_route_expert, b_route_expert, w_gate, w_up, w_down):
    batch, t, d = x_prompt.shape
    dec_batch, dec_t, _ = x_sample.shape
    n_p, n_s = batch * t, dec_batch * dec_t
    tr = _pick_rows_tile(n_p, n_s)
    tm = min(512, tr * 4)
    xp = x_prompt.reshape(n_p, d)
    xs = x_sample.reshape(n_s, d)
    nq_a = A_HEADS * A_HD
    nkv_a = A_KV_HEADS * A_HD

    q_a, kv_ap, kv_as = norm_proj(xp, xs, norm_mix[0], w_a_qkv[0].astype(BF16), b_a_qkv[0],
                                  [(0, nq_a, "flat", BF16), (nq_a, nq_a + 2 * nkv_a, "split", F32)], tr, n_p)
    qb_a = min(256, t)
    o_ap = attn_a_prompt(q_a, kv_ap, a_sinks[0], batch, t, qb_a)
    kv_s = kv_as.reshape(dec_batch, dec_t, 2 * nkv_a)
    past_a = cache_a_k.shape[2]
    kband = jnp.concatenate([cache_a_k[0].reshape(dec_batch, past_a, nkv_a), kv_s[..., :nkv_a]], axis=1)
    vband = jnp.concatenate([cache_a_v[0].reshape(dec_batch, past_a, nkv_a), kv_s[..., nkv_a:]], axis=1)
    o_as = attn_a_sample(q_a, kband, vband, a_sinks[0], n_p, dec_batch, dec_t)
    h, xn, meta, counts = out_router(o_ap, o_as, xp, xs, w_a_o[0].astype(BF16), norm_ffn[0], w_route_group[0],
                                     b_route_group[0], w_route_expert[0], b_route_expert[0], False, tr)
    og = moe_experts(xn, meta, counts, w_gate, w_up, w_down, 0, tm)

    keep = min(CHUNK * WIN_CHUNKS, t)
    kv_p = kv_ap.reshape(batch, t, 2 * nkv_a)[:, t - keep:]
    a_k_prompt = kv_p[..., :nkv_a].reshape(1, batch, keep, A_KV_HEADS, A_HD)
    a_v_prompt = kv_p[..., nkv_a:].reshape(1, batch, keep, A_KV_HEADS, A_HD)
    a_k_sample = kband[:, dec_t:].reshape(1, dec_batch, past_a, A_KV_HEADS, A_HD)
    a_v_sample = vband[:, dec_t:].reshape(1, dec_batch, past_a, A_KV_HEADS, A_HD)

    lam_init = 0.8 - 0.6 * float(np.exp(-0.3 * 1))
    lp = b_lambda[0].astype(F32)
    lam = jnp.exp(jnp.sum(lp[0] * lp[1])) - jnp.exp(jnp.sum(lp[2] * lp[3])) + lam_init
    wb = B_HEADS * 2 * B_HD
    col_scale = jnp.where(jnp.arange(3 * wb) < wb, B_Q_SCALE, 1.0).astype(F32)
    h, qkvh, k_bp, k_bs, v_bp, v_bs, vt = norm_proj(
        h, None, norm_mix[1], (w_b_qkv[0] * col_scale).astype(BF16), None,
        [(0, 3 * wb, "heads", BF16), (wb, 2 * wb, "split", F32), (2 * wb, 3 * wb, "split", F32),
         (2 * wb, 3 * wb, "heads_t", BF16)], tr, n_p, og=og, meta=meta)
    qb_b = min(256, t)
    o_bp = attn_b_prompt(qkvh, vt, lam, b_subln[0], lam_init, batch, t, qb_b)
    o_bs = attn_b_sample(qkvh, cache_b_k[0], cache_b_v[0], lam, b_subln[0], lam_init, n_p, dec_batch, dec_t)
    h, xn, meta, counts = out_router(o_bp, o_bs, h, None, w_b_o[0].astype(BF16), norm_ffn[1], w_route_group[1],
                                     b_route_group[1], w_route_expert[1], b_route_expert[1], True, tr)
    og = moe_experts(xn, meta, counts, w_gate, w_up, w_down, 1, tm)

    y_p, y_s = final_norm(h, og, meta, norm_final, n_p, tr)
    hd2 = 2 * B_HD
    return (y_p.reshape(batch, t, d), y_s.reshape(dec_batch, dec_t, d),
            a_k_prompt, a_v_prompt, a_k_sample, a_v_sample,
            k_bp.reshape(1, batch, t, B_HEADS, hd2), v_bp.reshape(1, batch, t, B_HEADS, hd2),
            k_bs.reshape(1, dec_batch, dec_t, B_HEADS, hd2), v_bs.reshape(1, dec_batch, dec_t, B_HEADS, hd2))
```

```python
import functools

import jax
import jax.numpy as jnp
import numpy as np
from jax import lax
from jax.experimental import pallas as pl
from jax.experimental.pallas import tpu as pltpu
from jax.experimental.pallas import tpu_sc as plsc

F32 = jnp.float32
BF16 = jnp.bfloat16
NEG_INF = float("-inf")

CHUNK = 64
WIN_CHUNKS = 2
A_HEADS, A_KV_HEADS, A_GROUP, A_HD = 16, 4, 4, 64
B_HEADS, B_HD = 8, 64
N_GROUPS, EXPERTS_PER_GROUP = 4, 8
N_EXPERTS = N_GROUPS * EXPERTS_PER_GROUP
NORM_EPS = 1e-6
SUBLN_EPS = 1e-5

LANES = 128
COL_CHUNK = 512
HEAD_GROUP = 2
SC_WINDOW = 128
SC_ROW_WORDS = 256
VMEM_LIMIT = 48 * 1024 * 1024


def _cparams(sem, flags=None):
    return pltpu.CompilerParams(dimension_semantics=sem, vmem_limit_bytes=VMEM_LIMIT, flags=flags)


def _alibi(n):
    return 2.0 ** (-8.0 * np.arange(1, n + 1) / n)


def _store_planes(ref, x):
    w = ref.shape[2]
    for p in range(ref.shape[0]):
        ref[p] = x[:, p * w:(p + 1) * w]


def _join_planes(ref, first, count):
    return jnp.concatenate([ref[first + p] for p in range(count)], axis=1)


def _add_expert_outputs(x, og_ref, meta_ref):
    half = og_ref.shape[1] // 2
    meta = meta_ref[...]
    return (x + meta[:, 2:3] * _unpack_pairs(og_ref[:, :half])) + meta[:, 3:4] * _unpack_pairs(og_ref[:, half:])


def _norm_proj_kernel(n_prompt_blocks, two_src, has_bias, combine, outs, *refs):
    refs = list(refs)
    xa_ref = refs.pop(0)
    xb_ref = refs.pop(0) if two_src else None
    og_ref, meta_ref = (refs.pop(0), refs.pop(0)) if combine else (None, None)
    g_ref = refs.pop(0)
    w_ref = refs.pop(0)
    b_ref = refs.pop(0) if has_bias else None
    out_refs = refs
    x = xa_ref[...]
    if two_src:
        x = jnp.where(pl.program_id(0) < n_prompt_blocks, x, xb_ref[...])
    if combine:
        x = _add_expert_outputs(x, og_ref, meta_ref)
        out_refs.pop(0)[...] = x
    ms = jnp.mean(x * x, axis=-1, keepdims=True)
    xn = (x * lax.rsqrt(ms + NORM_EPS) * g_ref[...]).astype(BF16)
    is_prompt = pl.program_id(0) < n_prompt_blocks
    out_refs = list(out_refs)
    targets = []
    for c0, c1, layout in outs:
        targets.append((c0, c1, layout, out_refs.pop(0), out_refs.pop(0) if layout == "split" else None))
    for s in range(0, w_ref.shape[1], COL_CHUNK):
        e = s + COL_CHUNK
        r = jnp.dot(xn, w_ref[:, s:e], preferred_element_type=F32)
        if has_bias:
            r = r + b_ref[:, s:e]
        for c0, c1, layout, o_ref, o2_ref in targets:
            if not (c0 <= s and e <= c1):
                continue
            rr = r.astype(o_ref.dtype)
            if layout == "flat":
                o_ref[:, s - c0:e - c0] = rr
            elif layout == "split":

                @pl.when(is_prompt)
                def _(rr=rr, o_ref=o_ref, s=s, e=e, c0=c0):
                    o_ref[:, s - c0:e - c0] = rr

                @pl.when(jnp.logical_not(is_prompt))
                def _(rr=rr, o2_ref=o2_ref, s=s, e=e, c0=c0):
                    o2_ref[:, s - c0:e - c0] = rr
            elif layout == "heads":
                for hh in range((e - s) // LANES):
                    o_ref[(s - c0) // LANES + hh] = rr[:, hh * LANES:(hh + 1) * LANES]
            else:
                for hh in range((e - s) // LANES):
                    o_ref[(s - c0) // LANES + hh] = r[:, hh * LANES:(hh + 1) * LANES].T.astype(o_ref.dtype)


def norm_proj(xa, xb, g, w, b, outs, tr, n_prompt, og=None, meta=None):
    na, d = xa.shape
    nb = 0 if xb is None else xb.shape[0]
    n = na + nb
    npb = n_prompt // tr
    two = xb is not None
    combine = og is not None
    in_specs = [pl.BlockSpec((tr, d), (lambda i: (jnp.minimum(i, npb - 1), 0)) if two else (lambda i: (i, 0)))]
    args = [xa]
    if two:
        in_specs.append(pl.BlockSpec((tr, d), lambda i: (jnp.maximum(i - npb, 0), 0)))
        args.append(xb)
    if combine:
        in_specs += [pl.BlockSpec((tr, og.shape[1]), lambda i: (i, 0)), pl.BlockSpec((tr, LANES), lambda i: (i, 0))]
        args += [og, meta]
    in_specs.append(pl.BlockSpec((1, d), lambda i: (0, 0)))
    args.append(g.reshape(1, d).astype(F32))
    in_specs.append(pl.BlockSpec(w.shape, lambda i: (0, 0)))
    args.append(w)
    if b is not None:
        in_specs.append(pl.BlockSpec((1, w.shape[1]), lambda i: (0, 0)))
        args.append(b.reshape(1, -1).astype(F32))
    out_shapes, out_specs = [], []
    if combine:
        out_shapes.append(jax.ShapeDtypeStruct((n, d), F32))
        out_specs.append(pl.BlockSpec((tr, d), lambda i: (i, 0)))
    for c0, c1, layout, dt in outs:
        assert c0 % COL_CHUNK == 0 and c1 % COL_CHUNK == 0
        if layout == "flat":
            out_shapes.append(jax.ShapeDtypeStruct((n, c1 - c0), dt))
            out_specs.append(pl.BlockSpec((tr, c1 - c0), lambda i: (i, 0)))
        elif layout == "split":
            out_shapes.append(jax.ShapeDtypeStruct((n_prompt, c1 - c0), dt))
            out_specs.append(pl.BlockSpec((tr, c1 - c0), lambda i: (jnp.minimum(i, npb - 1), 0)))
            out_shapes.append(jax.ShapeDtypeStruct((n - n_prompt, c1 - c0), dt))
            out_specs.append(pl.BlockSpec((tr, c1 - c0), lambda i: (jnp.maximum(i - npb, 0), 0)))
        elif layout == "heads":
            nh = (c1 - c0) // LANES
            out_shapes.append(jax.ShapeDtypeStruct((nh, n, LANES), dt))
            out_specs.append(pl.BlockSpec((nh, tr, LANES), lambda i: (0, i, 0)))
        else:
            assert layout == "heads_t"
            nh = (c1 - c0) // LANES
            out_shapes.append(jax.ShapeDtypeStruct((nh, LANES, n), dt))
            out_specs.append(pl.BlockSpec((nh, LANES, tr), lambda i: (0, 0, i)))
    kern = functools.partial(_norm_proj_kernel, npb, two, b is not None, combine,
                             [(c0, c1, lay) for c0, c1, lay, _ in outs])
    return pl.pallas_call(
        kern, grid=(n // tr,), in_specs=in_specs, out_specs=out_specs, out_shape=out_shapes,
        compiler_params=_cparams(("arbitrary",)), name="norm_proj")(*args)


def _attn_a_chunk(q, k, v, bias_ref, sink_ref, valid):
    nq = q.shape[0]
    outs = []
    for kh in range(A_KV_HEADS):
        qs = jnp.concatenate(
            [q[:, (kh * A_GROUP + g) * A_HD:(kh * A_GROUP + g + 1) * A_HD] for g in range(A_GROUP)], axis=0)
        qs = qs * jnp.asarray(A_HD ** -0.5, BF16)
        kk = k[:, kh * A_HD:(kh + 1) * A_HD]
        s = lax.dot_general(qs, kk, (((1,), (1,)), ((), ())), preferred_element_type=F32) + bias_ref[kh]
        if valid is not None:
            s = jnp.where(valid, s, NEG_INF)
        sk = sink_ref[kh]
        m = jnp.maximum(jnp.max(s, axis=-1, keepdims=True), sk)
        e = jnp.exp(s - m)
        den = jnp.sum(e, axis=-1, keepdims=True) + jnp.exp(sk - m)
        p = (e * (1.0 / den)).astype(BF16)
        o = jnp.dot(p, v[:, kh * A_HD:(kh + 1) * A_HD], preferred_element_type=F32)
        for g in range(A_GROUP):
            outs.append(o[g * nq:(g + 1) * nq])
    return jnp.concatenate(outs, axis=1)


def _attn_a_prompt_kernel(qb, q_ref, kvc_ref, kvp_ref, bias_ref, sink_ref, o_ref):
    cb = pl.program_id(1)
    back = CHUNK * WIN_CHUNKS
    nk = back + qb
    nkv = A_KV_HEADS * A_HD
    kvfull = jnp.concatenate([kvp_ref[...], kvc_ref[...]], axis=0)
    k_all = kvfull[:, :nkv].astype(BF16)
    vt_all = kvfull[:, nkv:].T.astype(BF16)
    key_pos = lax.broadcasted_iota(jnp.int32, (nk, 1), 0) + (cb * qb - back)
    before_start = jnp.where(key_pos >= 0, 0.0, NEG_INF)
    dn = (((1,), (1,)), ((), ()))
    outs = []
    for kh in range(A_KV_HEADS):
        qs = jnp.concatenate(
            [q_ref[:, (kh * A_GROUP + g) * A_HD:(kh * A_GROUP + g + 1) * A_HD] for g in range(A_GROUP)], axis=0)
        qs = qs * jnp.asarray(A_HD ** -0.5, BF16)
        st = lax.dot_general(k_all[:, kh * A_HD:(kh + 1) * A_HD], qs, dn, preferred_element_type=F32)
        st = st + bias_ref[kh] + before_start
        sk = sink_ref[kh]
        m = jnp.maximum(jnp.max(st, axis=0, keepdims=True), sk)
        e = jnp.exp(st - m)
        den = jnp.sum(e, axis=0, keepdims=True) + jnp.exp(sk - m)
        p = (e * (1.0 / den)).astype(BF16)
        ot = jnp.dot(vt_all[kh * A_HD:(kh + 1) * A_HD, :], p, preferred_element_type=F32)
        o = ot.T
        for g in range(A_GROUP):
            outs.append(o[g * qb:(g + 1) * qb])
    o_ref[...] = jnp.concatenate(outs, axis=1).astype(o_ref.dtype)


def _a_prompt_tables(qb, sinks):
    back = CHUNK * WIN_CHUNKS
    slopes = _alibi(A_HEADS).reshape(A_KV_HEADS, A_GROUP)
    qpos = np.arange(qb)
    kpos = np.arange(back + qb) - back
    dist = np.abs(kpos[:, None] - qpos[None, :]).astype(np.float64)
    band_lo = (qpos // CHUNK) * CHUNK - back
    in_band = (kpos[:, None] >= band_lo[None, :]) & (kpos[:, None] < band_lo[None, :] + back + CHUNK)
    bias = np.where(in_band[None, None], -slopes[:, :, None, None] * dist[None, None], -np.inf)
    bias = np.transpose(bias, (0, 2, 1, 3)).reshape(A_KV_HEADS, back + qb, A_GROUP * qb)
    sk = jnp.repeat(sinks.astype(F32).reshape(A_KV_HEADS, A_GROUP), qb, axis=1)[:, None, :]
    return jnp.asarray(bias, F32), sk


def _a_tables(qpos, kpos, sinks):
    slopes = _alibi(A_HEADS).reshape(A_KV_HEADS, A_GROUP)
    dist = np.abs(qpos[:, None] - kpos[None, :]).astype(np.float64)
    bias = -slopes[:, :, None, None] * dist
    nq = len(qpos)
    bias = jnp.asarray(bias.reshape(A_KV_HEADS, A_GROUP * nq, len(kpos)), F32)
    sk = jnp.repeat(sinks.astype(F32).reshape(A_KV_HEADS, A_GROUP), nq, axis=1)[..., None]
    return bias, sk


def attn_a_prompt(q_all, kv_all, sinks, batch, t, qb):
    nq = A_HEADS * A_HD
    nkv2 = 2 * A_KV_HEADS * A_HD
    back = CHUNK * WIN_CHUNKS
    bias, sk = _a_prompt_tables(qb, sinks)
    nblk = t // qb
    r = qb // back
    return pl.pallas_call(
        functools.partial(_attn_a_prompt_kernel, qb),
        grid=(batch, nblk),
        in_specs=[
            pl.BlockSpec((qb, nq), lambda b, i: (b * nblk + i, 0)),
            pl.BlockSpec((qb, nkv2), lambda b, i: (b * nblk + i, 0)),
            pl.BlockSpec((back, nkv2), lambda b, i: (jnp.maximum((b * nblk + i) * r - 1, 0), 0)),
            pl.BlockSpec(bias.shape, lambda b, i: (0, 0, 0)),
            pl.BlockSpec(sk.shape, lambda b, i: (0, 0, 0)),
        ],
        out_specs=pl.BlockSpec((qb, nq), lambda b, i: (b * nblk + i, 0)),
        out_shape=jax.ShapeDtypeStruct((batch * t, nq), BF16),
        compiler_params=_cparams(("arbitrary", "arbitrary")), name="attn_a_prompt",
    )(q_all, kv_all, kv_all, bias, sk)


def _attn_a_sample_kernel(q_ref, k_ref, v_ref, bias_ref, sink_ref, o_ref):
    o = _attn_a_chunk(q_ref[...], k_ref[...].astype(BF16), v_ref[...].astype(BF16), bias_ref, sink_ref, None)
    o_ref[...] = o.astype(o_ref.dtype)


def attn_a_sample(q_all, kband, vband, sinks, row0, dec_batch, dec_t):
    nq = A_HEADS * A_HD
    s = kband.shape[1]
    past = s - dec_t
    kpos = np.concatenate([np.arange(past) - past, np.arange(dec_t)])
    bias, sk = _a_tables(np.arange(dec_t), kpos, sinks)
    blk0 = row0 // dec_t
    return pl.pallas_call(
        _attn_a_sample_kernel,
        grid=(dec_batch,),
        in_specs=[
            pl.BlockSpec((dec_t, nq), lambda b: (blk0 + b, 0)),
            pl.BlockSpec((None, s, kband.shape[2]), lambda b: (b, 0, 0)),
            pl.BlockSpec((None, s, vband.shape[2]), lambda b: (b, 0, 0)),
            pl.BlockSpec(bias.shape, lambda b: (0, 0, 0)),
            pl.BlockSpec(sk.shape, lambda b: (0, 0, 0)),
        ],
        out_specs=pl.BlockSpec((dec_t, nq), lambda b: (b, 0)),
        out_shape=jax.ShapeDtypeStruct((dec_batch * dec_t, nq), BF16),
        compiler_params=_cparams(("arbitrary",)), name="attn_a_sample",
    )(q_all, kband, vband, bias, sk)


B_Q_SCALE = B_HD ** -0.5 * float(np.log2(np.e))
LOG2E = float(np.log2(np.e))


def _stack_maps(qh):
    lane = lax.broadcasted_iota(jnp.int32, qh.shape, 1)
    zero = jnp.zeros_like(qh)
    return jnp.concatenate([jnp.where(lane < B_HD, qh, zero), jnp.where(lane >= B_HD, qh, zero)], axis=0)


def _diff_finish(acc, l, nq, lam, lam_init, subln):
    inv = 1.0 / l
    o = acc[:nq] * inv[:nq] - lam * (acc[nq:] * inv[nq:])
    ms = jnp.mean(o * o, axis=-1, keepdims=True)
    return o * lax.rsqrt(ms + SUBLN_EPS) * subln * (1.0 - lam_init)


def _attn_b_prompt_kernel(qb, lam_init, slope_ref, lam_ref, q_ref, k_ref, vt_ref, kcol_ref, tri_ref, msk_ref,
                          subln_ref, o_ref, acc_ref, t0_ref, t1_ref, p0_ref, p1_ref):
    i = pl.program_id(1)
    lam = lam_ref[0]
    dn = (((1,), (1,)), ((), ()))
    t_refs = (t0_ref, t1_ref)
    p_refs = (p0_ref, p1_ref)

    def group_body(hg, carry):
        heads = [hg * HEAD_GROUP + u for u in range(HEAD_GROUP)]
        slopes = [slope_ref[h] for h in heads]
        qqs = [_stack_maps(q_ref[h]) for h in heads]

        def scores(j, slot):
            start = pl.multiple_of(j * qb, qb)
            for u, h in enumerate(heads):
                t_refs[slot][u] = lax.dot_general(k_ref[h, pl.ds(start, qb), :], qqs[u], dn,
                                                  preferred_element_type=F32)

        def softmax_step(j, slot, c, diag):
            out = []
            for u, h in enumerate(heads):
                m, l, _ = c[u]
                t = t_refs[slot][u] + kcol_ref[h]
                if diag:
                    t = t + slopes[u] * tri_ref[...] + msk_ref[...]
                off = slopes[u] * jnp.full((1, 2 * qb), (j - i) * qb, jnp.int32).astype(F32)
                m_new = jnp.maximum(m, jnp.max(t, axis=0, keepdims=True) + off)
                alpha = jnp.exp2(m - m_new)
                p = jnp.exp2(t + (off - m_new))
                p_refs[slot][u] = p.astype(BF16)
                out.append((m_new, alpha * l + jnp.sum(p, axis=0, keepdims=True), alpha))
            return tuple(out)

        def accumulate(j, slot, c):
            start = pl.multiple_of(jnp.maximum(j, 0) * qb, qb)
            for u, h in enumerate(heads):
                pv = jnp.dot(vt_ref[h, :, pl.ds(start, qb)], p_refs[slot][u], preferred_element_type=F32)
                acc_ref[u] = c[u][2] * acc_ref[u] + pv

        def steady(j, c, slot):
            scores(j + 1, 1 - slot)
            c_new = softmax_step(j, slot, c, False)
            accumulate(j - 1, 1 - slot, c)
            return c_new

        def last(c, slot):
            c_new = softmax_step(i, slot, c, True)
            accumulate(i - 1, 1 - slot, c)
            accumulate(i, slot, c_new)
            return c_new

        acc_ref[...] = jnp.zeros_like(acc_ref)
        p1_ref[...] = jnp.zeros_like(p1_ref)
        scores(0, 0)
        init = tuple((jnp.full((1, 2 * qb), NEG_INF, F32), jnp.zeros((1, 2 * qb), F32), jnp.ones((1, 2 * qb), F32))
                     for _ in heads)
        c = lax.fori_loop(
            0, i, lambda j, c: lax.cond(j % 2 == 0, lambda: steady(j, c, 0), lambda: steady(j, c, 1)), init)
        c = lax.cond(i % 2 == 0, lambda: last(c, 0), lambda: last(c, 1))
        for u, h in enumerate(heads):
            inv = 1.0 / c[u][1]
            acc = acc_ref[u]
            ot = acc[:, :qb] * inv[:, :qb] - lam * (acc[:, qb:] * inv[:, qb:])
            ms = jnp.mean(ot * ot, axis=0, keepdims=True)
            ot = ot * lax.rsqrt(ms + SUBLN_EPS) * subln_ref[...] * (1.0 - lam_init)
            o_ref[h] = ot.T.astype(o_ref.dtype)
        return carry

    lax.fori_loop(0, B_HEADS // HEAD_GROUP, group_body, 0)


def attn_b_prompt(qkvh, vt, lam, subln, lam_init, batch, t, qb):
    nblk = t // qb
    qq = np.arange(qb)
    kcol = (_alibi(B_HEADS) * LOG2E)[:, None, None] * np.broadcast_to(qq[:, None], (qb, 2 * qb))[None]
    tri =-2.0 * np.maximum(qq[:, None] - qq[None, :], 0).astype(np.float32)
    msk = np.where((qq[:, None] // CHUNK) <= (qq[None, :] // CHUNK), 0.0, -np.inf).astype(np.float32)
    tri = jnp.asarray(np.concatenate([tri, tri], axis=1))
    msk = jnp.asarray(np.concatenate([msk, msk], axis=1))
    kcol = jnp.asarray(kcol, F32)
    slopes =jnp.asarray(_alibi(B_HEADS) * LOG2E, F32)
    subln_t = jnp.broadcast_to(subln.astype(F32).reshape(LANES, 1), (LANES, qb))
    const = lambda b, i, *_: (0, 0)
    grid_spec = pltpu.PrefetchScalarGridSpec(
        num_scalar_prefetch=2, grid=(batch, nblk),
        in_specs=[
            pl.BlockSpec((B_HEADS, qb, LANES), lambda b, i, *_: (0, b * nblk + i, 0)),
            pl.BlockSpec((B_HEADS, t, LANES), lambda b, i, *_: (1, b, 0)),
            pl.BlockSpec((B_HEADS, LANES, t), lambda b, i, *_: (0, 0, b)),
            pl.BlockSpec(kcol.shape, lambda b, i, *_: (0, 0, 0)),
            pl.BlockSpec(tri.shape, const),
            pl.BlockSpec(msk.shape, const),
            pl.BlockSpec(subln_t.shape, const),
        ],
        out_specs=pl.BlockSpec((B_HEADS, qb, LANES), lambda b, i, *_: (0, b * nblk + i, 0)),
        scratch_shapes=[pltpu.VMEM((HEAD_GROUP, LANES, 2 * qb), F32),
                        pltpu.VMEM((HEAD_GROUP, qb, 2 * qb), F32), pltpu.VMEM((HEAD_GROUP, qb, 2 * qb), F32),
                        pltpu.VMEM((HEAD_GROUP, qb, 2 * qb), BF16), pltpu.VMEM((HEAD_GROUP, qb, 2 * qb), BF16)],
    )
    return pl.pallas_call(
        functools.partial(_attn_b_prompt_kernel, qb, lam_init), grid_spec=grid_spec,
        out_shape=jax.ShapeDtypeStruct((B_HEADS, batch * t, LANES), BF16),
        compiler_params=_cparams(("arbitrary", "arbitrary")), name="attn_b_prompt",
    )(slopes, lam.reshape(1).astype(F32), qkvh, qkvh, vt, kcol, tri, msk, subln_t)


def _attn_b_sample_kernel(dec_t, past, lam_init, slopes, lam_ref, q_ref, kn_ref, vn_ref, ck_ref, cv_ref,
                          tnew_ref, subln_ref, o_ref):
    lam = lam_ref[0]
    subln = subln_ref[...]
    colc = lax.broadcasted_iota(jnp.int32, (1, past), 1).astype(F32) - float(past)
    for h in range(B_HEADS):
        slope = float(slopes[h]) * LOG2E
        qq = _stack_maps(q_ref[h])
        kc = ck_ref[pl.ds(h, past, stride=B_HEADS), :].astype(BF16)
        vc = cv_ref[pl.ds(h, past, stride=B_HEADS), :].astype(BF16)
        dn = (((1,), (1,)), ((), ()))
        s_c = lax.dot_general(qq, kc, dn, preferred_element_type=F32) + slope * colc
        s_n = lax.dot_general(qq, kn_ref[h], dn, preferred_element_type=F32) + slope * tnew_ref[...]
        m = jnp.maximum(jnp.max(s_c, axis=-1, keepdims=True), jnp.max(s_n, axis=-1, keepdims=True))
        p_c = jnp.exp2(s_c - m)
        p_n = jnp.exp2(s_n - m)
        l = jnp.sum(p_c, axis=-1, keepdims=True) + jnp.sum(p_n, axis=-1, keepdims=True)
        acc = (jnp.dot(p_c.astype(BF16), vc, preferred_element_type=F32)
               + jnp.dot(p_n.astype(BF16), vn_ref[h], preferred_element_type=F32))
        o_ref[h] = _diff_finish(acc, l, dec_t, lam, lam_init, subln).astype(o_ref.dtype)


def attn_b_sample(qkvh, cache_k, cache_v, lam, subln, lam_init, row0, dec_batch, dec_t):
    past = cache_k.shape[1]
    qi = np.arange(dec_t)
    tnew = (qi[:, None] - np.abs(qi[:, None] - qi[None, :])).astype(np.float32)
    tnew = jnp.asarray(np.concatenate([tnew, tnew], axis=0))
    blk0 = row0 // dec_t
    rows = past * B_HEADS
    cache_k = cache_k.reshape(dec_batch, rows, LANES)
    cache_v = cache_v.reshape(dec_batch, rows, LANES)
    grid_spec = pltpu.PrefetchScalarGridSpec(
        num_scalar_prefetch=1, grid=(dec_batch,),
        in_specs=[
            pl.BlockSpec((B_HEADS, dec_t, LANES), lambda b, *_: (0, blk0 + b, 0)),
            pl.BlockSpec((B_HEADS, dec_t, LANES), lambda b, *_: (1, blk0 + b, 0)),
            pl.BlockSpec((B_HEADS, dec_t, LANES), lambda b, *_: (2, blk0 + b, 0)),
            pl.BlockSpec((None, rows, LANES), lambda b, *_: (b, 0, 0)),
            pl.BlockSpec((None, rows, LANES), lambda b, *_: (b, 0, 0)),
            pl.BlockSpec(tnew.shape, lambda b, *_: (0, 0)),
            pl.BlockSpec((1, LANES), lambda b, *_: (0, 0)),
        ],
        out_specs=pl.BlockSpec((B_HEADS, dec_t, LANES), lambda b, *_: (0, b, 0)),
    )
    return pl.pallas_call(
        functools.partial(_attn_b_sample_kernel, dec_t, past, lam_init, tuple(_alibi(B_HEADS))),
        grid_spec=grid_spec,
        out_shape=jax.ShapeDtypeStruct((B_HEADS, dec_batch * dec_t, LANES), BF16),
        compiler_params=_cparams(("arbitrary",)), name="attn_b_sample",
    )(lam.reshape(1).astype(F32), qkvh, qkvh, qkvh, cache_k, cache_v, tnew, subln.reshape(1, LANES).astype(F32))


def _route(logits):
    lane = lax.broadcasted_iota(jnp.int32, logits.shape, 1).astype(F32)
    big = float(LANES)
    lg = jnp.where(lane < N_GROUPS, logits, NEG_INF)
    gmax = jnp.max(lg, axis=-1, keepdims=True)
    g_idx = jnp.min(jnp.where(lg == gmax, lane, big), axis=-1, keepdims=True)
    g_prob = 1.0 / jnp.sum(jnp.exp(lg - gmax), axis=-1, keepdims=True)
    lo = N_GROUPS + EXPERTS_PER_GROUP * g_idx
    le = jnp.where((lane >= lo) & (lane < lo + EXPERTS_PER_GROUP), logits, NEG_INF)
    m1 = jnp.max(le, axis=-1, keepdims=True)
    i1 = jnp.min(jnp.where(le == m1, lane, big), axis=-1, keepdims=True)
    le2 = jnp.where(lane == i1, NEG_INF, le)
    m2 = jnp.max(le2, axis=-1, keepdims=True)
    i2 = jnp.min(jnp.where(le2 == m2, lane, big), axis=-1, keepdims=True)
    t = jnp.exp(m2 - m1)
    ga = 1.0 / (1.0 + t)
    gb = t * ga
    return lane, i1, i2, g_prob * ga, g_prob * gb


def _unpack_pairs(u):
    lo = pltpu.bitcast(u << 16, F32)
    hi = pltpu.bitcast(u & jnp.uint32(0xFFFF0000), F32)
    return jnp.concatenate([lo, hi], axis=1)


def _pack_pairs(x):
    bits = pltpu.bitcast(x.astype(BF16).astype(F32), jnp.uint32)
    w = x.shape[1] // 2
    return (bits[:, :w] >> 16) | bits[:, w:]


def _out_router_kernel(n_prompt_blocks, heads_in, two_o, two_res, *refs):
    refs = list(refs)
    oa_ref = refs.pop(0)
    ob_ref = refs.pop(0) if two_o else None
    ra_ref = refs.pop(0)
    rb_ref = refs.pop(0) if two_res else None
    wo_ref, g_ref, wr_ref, wr2_ref, br_ref, ltri_ref, h_ref, xn_ref, meta_ref, cnt_ref, carry_ref = refs
    is_prompt = pl.program_id(0) < n_prompt_blocks

    def load_o(ref):
        if heads_in:
            return jnp.concatenate([ref[hh] for hh in range(ref.shape[0])], axis=1)
        return ref[...]

    o = load_o(oa_ref)
    if two_o:
        o = jnp.where(is_prompt, o, load_o(ob_ref))
    res = ra_ref[...]
    if two_res:
        res = jnp.where(is_prompt, res, rb_ref[...])
    h = res + jnp.dot(o, wo_ref[...], preferred_element_type=F32)
    h_ref[...] = h
    ms = jnp.mean(h * h, axis=-1, keepdims=True)
    xn = h * lax.rsqrt(ms + NORM_EPS) * g_ref[...]
    x_hi = xn.astype(BF16)
    _store_planes(xn_ref, _pack_pairs(xn))
    x_lo = (xn - x_hi.astype(F32)).astype(BF16)
    r = jnp.dot(x_hi, wr_ref[...], preferred_element_type=F32)
    logits = (r[:, :LANES] + r[:, LANES:]) + jnp.dot(x_lo, wr2_ref[...], preferred_element_type=F32) + br_ref[...]
    lane, i1, i2, g1, g2 = _route(logits)

    @pl.when(pl.program_id(0) == 0)
    def _():
        carry_ref[...] = jnp.zeros_like(carry_ref)

    sel1 = lane == i1
    sel2 = lane == i2
    onehot = jnp.where(sel1 | sel2, 1.0, 0.0)
    before = jnp.dot(ltri_ref[...], onehot.astype(BF16), preferred_element_type=F32) + carry_ref[...]
    rank1 = jnp.sum(jnp.where(sel1, before, 0.0), axis=-1, keepdims=True)
    rank2 = jnp.sum(jnp.where(sel2, before, 0.0), axis=-1, keepdims=True)
    carry_ref[...] = carry_ref[...] + jnp.sum(onehot, axis=0, keepdims=True)
    cnt_ref[...] = carry_ref[...]
    cols = [i1 - N_GROUPS, i2 - N_GROUPS, g1, g2, rank1, rank2]
    meta = jnp.zeros_like(logits)
    for c, v in enumerate(cols):
        meta = jnp.where(lane == c, v, meta)
    meta_ref[...] = meta


def out_router(oa, ob, ra, rb, wo, g, w_rg, b_rg, w_re, b_re, heads_in, tr):
    d = ra.shape[1]
    if heads_in:
        na = oa.shape[1]
        nb = 0 if ob is None else ob.shape[1]
    else:
        na = oa.shape[0]
        nb = 0 if ob is None else ob.shape[0]
    n = na + nb
    npb = na // tr
    two_o = ob is not None
    two_res = rb is not None
    first = lambda i: (jnp.minimum(i, npb - 1), 0)
    second = lambda i: (jnp.maximum(i - npb, 0), 0)
    plain = lambda i: (i, 0)
    in_specs, args = [], []

    def add_o(x, imap):
        if heads_in:
            in_specs.append(pl.BlockSpec((x.shape[0], tr, LANES), lambda i: (0, imap(i)[0], 0)))
        else:
            in_specs.append(pl.BlockSpec((tr, x.shape[1]), imap))
        args.append(x)

    add_o(oa, first if two_o else plain)
    if two_o:
        add_o(ob, second)
    in_specs.append(pl.BlockSpec((tr, d), first if two_res else plain))
    args.append(ra)
    if two_res:
        in_specs.append(pl.BlockSpec((tr, d), second))
        args.append(rb)
    wr = jnp.zeros((d, LANES), F32).at[:, :N_GROUPS].set(w_rg.astype(F32))
    wr = wr.at[:, N_GROUPS:N_GROUPS + N_EXPERTS].set(w_re.astype(F32))
    wr_hi = wr.astype(BF16)
    wr_lo = (wr - wr_hi.astype(F32)).astype(BF16)
    br = jnp.zeros((1, LANES), F32).at[0, :N_GROUPS].set(b_rg.astype(F32))
    br = br.at[0, N_GROUPS:N_GROUPS + N_EXPERTS].set(b_re.astype(F32))
    ltri = jnp.asarray(np.tril(np.ones((tr, tr), np.float32), -1), BF16)
    consts = [wo, g.reshape(1, d).astype(F32), jnp.concatenate([wr_hi, wr_lo], axis=1), wr_hi, br, ltri]
    for cst in consts:
        in_specs.append(pl.BlockSpec(cst.shape, lambda i: (0, 0)))
        args.append(cst)
    parts = d // 2 // SC_ROW_WORDS
    out_shape = [jax.ShapeDtypeStruct((n, d), F32), jax.ShapeDtypeStruct((parts, n, SC_ROW_WORDS), jnp.uint32),
                 jax.ShapeDtypeStruct((n, LANES), F32), jax.ShapeDtypeStruct((1, LANES), F32)]
    out_specs = [pl.BlockSpec((tr, d), plain), pl.BlockSpec((parts, tr, SC_ROW_WORDS), lambda i: (0, i, 0)),
                 pl.BlockSpec((tr, LANES), plain), pl.BlockSpec((1, LANES), lambda i: (0, 0))]
    return pl.pallas_call(
        functools.partial(_out_router_kernel, npb, heads_in, two_o, two_res),
        grid=(n // tr,), in_specs=in_specs, out_specs=out_specs, out_shape=out_shape,
        scratch_shapes=[pltpu.VMEM((1, LANES), F32)],
        compiler_params=_cparams(("arbitrary",)), name="out_router")(*args)


def _expert_kernel(te_ref, nt_ref, x_ref, wg_ref, wu_ref, wd_ref, o_ref):
    @pl.when(pl.program_id(0) < nt_ref[0])
    def _():
        x = _unpack_pairs(_join_planes(x_ref, 0, x_ref.shape[0])).astype(BF16)
        a = jnp.dot(x, wg_ref[...].astype(BF16), preferred_element_type=F32)
        u = jnp.dot(x, wu_ref[...].astype(BF16), preferred_element_type=F32)
        hh = (a * (1.0 / (1.0 + jnp.exp(-a))) * u).astype(BF16)
        _store_planes(o_ref, _pack_pairs(jnp.dot(hh, wd_ref[...].astype(BF16), preferred_element_type=F32)))

    @pl.when(pl.program_id(0) >= nt_ref[0])
    def _():
        o_ref[...] = jnp.zeros_like(o_ref)


def expert_mlp(xs, tile_expert, n_tiles, w_gate, w_up, w_down, layer, tm):
    parts, p, w = xs.shape
    d, f = w_gate.shape[-2:]
    epg = w_gate.shape[2]
    wmap = lambda t, te, nt: (layer, te[t] // epg, te[t] % epg, 0, 0)
    grid_spec = pltpu.PrefetchScalarGridSpec(
        num_scalar_prefetch=2, grid=(p // tm,),
        in_specs=[
            pl.BlockSpec((parts, tm, w), lambda t, te, nt: (0, t, 0)),
            pl.BlockSpec((None, None, None, d, f), wmap),
            pl.BlockSpec((None, None, None, d, f), wmap),
            pl.BlockSpec((None, None, None, f, d), wmap),
        ],
        out_specs=pl.BlockSpec((parts, tm, w), lambda t, te, nt: (0, t, 0)),
    )
    return pl.pallas_call(
        _expert_kernel, grid_spec=grid_spec, out_shape=jax.ShapeDtypeStruct((parts, p, w), jnp.uint32),
        compiler_params=_cparams(("arbitrary",)), name="expert_mlp")(tile_expert, n_tiles, xs, w_gate, w_up, w_down)


def sc_scatter_rows(x2, idx, n_out):
    mesh = plsc.VectorSubcoreMesh(core_axis_name="c", subcore_axis_name="s")
    n_src_win = x2.shape[0] // SC_WINDOW

    @functools.partial(pl.kernel, out_type=jax.ShapeDtypeStruct((n_out, SC_ROW_WORDS), x2.dtype), mesh=mesh)
    def scatter(x_hbm, i_hbm, o_hbm):
        def body(x_vmem, i_vmem):
            pltpu.sync_copy(x_vmem, o_hbm.at[i_vmem.at[0]])

        pltpu.emit_pipeline(
            body, grid=(n_out // SC_WINDOW,),
            in_specs=[pl.BlockSpec((SC_WINDOW, SC_ROW_WORDS), lambda i: (lax.rem(i, n_src_win), 0)),
                      pl.BlockSpec((1, SC_WINDOW), lambda i: (0, i))],
            out_specs=[], core_axis_name=("c", "s"), dimension_semantics=(pltpu.PARALLEL,),
        )(x_hbm, i_hbm)

    return scatter(x2, idx)


def sc_gather_rows(x2, idx):
    mesh = plsc.VectorSubcoreMesh(core_axis_name="c", subcore_axis_name="s")
    n_col, n = idx.shape

    @functools.partial(pl.kernel, out_type=jax.ShapeDtypeStruct((n, n_col * SC_ROW_WORDS), x2.dtype), mesh=mesh)
    def gather(x_hbm, i_hbm, o_hbm):
        def body(i_vmem, o_vmem):
            pltpu.sync_copy(x_hbm.at[i_vmem.at[0]], o_vmem)

        pltpu.emit_pipeline(
            body, grid=(n // SC_WINDOW, n_col),
            in_specs=[pl.BlockSpec((1, SC_WINDOW), lambda i, c: (c, i))],
            out_specs=[pl.BlockSpec((SC_WINDOW, SC_ROW_WORDS), lambda i, c: (i, c))],
            core_axis_name=("c", "s"), dimension_semantics=(pltpu.PARALLEL, pltpu.ARBITRARY),
        )(i_hbm, o_hbm)

    return gather(x2, idx)


def moe_dispatch(xn_packed, meta, counts, tm):
    parts, n, w = xn_packed.shape
    e = meta[:, 0:2].astype(jnp.int32)
    rank = meta[:, 4:6].astype(jnp.int32)
    counts = counts[0, N_GROUPS:N_GROUPS + N_EXPERTS].astype(jnp.int32)
    padded = ((counts + tm - 1) // tm) * tm
    pend = jnp.cumsum(padded)
    pstart = pend - padded
    ids = jnp.arange(N_EXPERTS, dtype=jnp.int32)
    pos = jnp.sum(jnp.where(e[..., None] == ids, pstart, 0), axis=-1) + rank
    p = 2 * n + N_EXPERTS * tm
    n_pad = p - 2 * n
    gap_len = jnp.concatenate([padded - counts, (p - pend[-1]).reshape(1)])
    gap_first = jnp.concatenate([pstart + counts, pend[-1:]])
    gap_end = jnp.cumsum(gap_len)
    k = jnp.arange(n_pad, dtype=jnp.int32)
    gap = jnp.sum((k[:, None] >= gap_end[None, :]).astype(jnp.int32), axis=1)
    shift = gap_first - (gap_end - gap_len)
    pad_pos = k + jnp.sum(jnp.where(gap[:, None] == jnp.arange(N_EXPERTS + 1), shift, 0), axis=-1)
    planes = lambda rows: jnp.concatenate([rows + q * p for q in range(parts)])
    scat_idx = jnp.concatenate([planes(pos[:, 0]), planes(pos[:, 1]), planes(pad_pos)]).reshape(1, parts * p)
    gath_idx = jnp.stack([pos[:, k] + q * p for k in range(2) for q in range(parts)])
    xs = sc_scatter_rows(xn_packed.reshape(parts * n, w), scat_idx, parts * p).reshape(parts, p, w)
    tile_start = jnp.arange(p // tm, dtype=jnp.int32) * tm
    tile_expert = jnp.sum((tile_start[:, None] >= pend[None, :]).astype(jnp.int32), axis=1)
    tile_expert = jnp.minimum(tile_expert, N_EXPERTS - 1)
    n_tiles = (pend[-1] // tm).astype(jnp.int32).reshape(1)
    return xs, tile_expert, n_tiles, gath_idx


def moe_experts(xn_packed, meta, counts, w_gate, w_up, w_down, layer, tm):
    parts, n, w = xn_packed.shape
    xs, tile_expert, n_tiles, gath_idx = moe_dispatch(xn_packed, meta, counts, tm)
    o = expert_mlp(xs, tile_expert, n_tiles, w_gate, w_up, w_down, layer, tm)
    return sc_gather_rows(o.reshape(parts * o.shape[1], w), gath_idx)


def _final_norm_kernel(n_prompt_blocks, h_ref, og_ref, meta_ref, g_ref, yp_ref, ys_ref):
    x = _add_expert_outputs(h_ref[...], og_ref, meta_ref)
    ms = jnp.mean(x * x, axis=-1, keepdims=True)
    y = x * lax.rsqrt(ms + NORM_EPS) * g_ref[...]
    i = pl.program_id(0)

    @pl.when(i < n_prompt_blocks)
    def _():
        yp_ref[...] = y

    @pl.when(i >= n_prompt_blocks)
    def _():
        ys_ref[...] = y


def final_norm(h, og, meta, g, n_prompt, tr):
    n, d = h.shape
    npb = n_prompt // tr
    return pl.pallas_call(
        functools.partial(_final_norm_kernel, npb),
        grid=(n // tr,),
        in_specs=[pl.BlockSpec((tr, d), lambda i: (i, 0)), pl.BlockSpec((tr, og.shape[1]), lambda i: (i, 0)),
                  pl.BlockSpec((tr, LANES), lambda i: (i, 0)), pl.BlockSpec((1, d), lambda i: (0, 0))],
        out_specs=[pl.BlockSpec((tr, d), lambda i: (jnp.minimum(i, npb - 1), 0)),
                   pl.BlockSpec((tr, d), lambda i: (jnp.maximum(i - npb, 0), 0))],
        out_shape=[jax.ShapeDtypeStruct((n_prompt, d), F32), jax.ShapeDtypeStruct((n - n_prompt, d), F32)],
        compiler_params=_cparams(("arbitrary",)), name="final_norm")(h, og, meta, g.reshape(1, d).astype(F32))


def _pick_rows_tile(n_prompt, n_sample):
    for tr in (512, 256, 128, 64, 32, 16, 8):
        if n_prompt % tr == 0 and n_sample % tr == 0:
            return tr
    raise ValueError("row counts must be multiples of 8")


def kernel(x_prompt, x_sample, cache_a_k, cache_a_v, cache_b_k, cache_b_v, norm_mix, norm_ffn, norm_final,
           w_a_qkv, b_a_qkv, a_sinks, w_a_o, w_b_qkv, b_lambda, b_subln, w_b_o,
           w_route_group, b_route_group, w_route_expert, b_route_expert, w_gate, w_up, w_down):
    batch, t, d = x_prompt.shape
    dec_batch, dec_t, _ = x_sample.shape
    n_p, n_s = batch * t, dec_batch * dec_t
    tr = _pick_rows_tile(n_p, n_s)
    tm = min(512, tr * 4)
    xp = x_prompt.reshape(n_p, d)
    xs = x_sample.reshape(n_s, d)
    nq_a = A_HEADS * A_HD
    nkv_a = A_KV_HEADS * A_HD

    q_a, kv_ap, kv_as = norm_proj(xp, xs, norm_mix[0], w_a_qkv[0].astype(BF16), b_a_qkv[0],
                                  [(0, nq_a, "flat", BF16), (nq_a, nq_a + 2 * nkv_a, "split", F32)], tr, n_p)
    qb_a = min(256, t)
    o_ap = attn_a_prompt(q_a, kv_ap, a_sinks[0], batch, t, qb_a)
    kv_s = kv_as.reshape(dec_batch, dec_t, 2 * nkv_a)
    past_a = cache_a_k.shape[2]
    kband = jnp.concatenate([cache_a_k[0].reshape(dec_batch, past_a, nkv_a), kv_s[..., :nkv_a]], axis=1)
    vband = jnp.concatenate([cache_a_v[0].reshape(dec_batch, past_a, nkv_a), kv_s[..., nkv_a:]], axis=1)
    o_as = attn_a_sample(q_a, kband, vband, a_sinks[0], n_p, dec_batch, dec_t)
    h, xn, meta, counts = out_router(o_ap, o_as, xp, xs, w_a_o[0].astype(BF16), norm_ffn[0], w_route_group[0],
                                     b_route_group[0], w_route_expert[0], b_route_expert[0], False, tr)
    og = moe_experts(xn, meta, counts, w_gate, w_up, w_down, 0, tm)

    keep = min(CHUNK * WIN_CHUNKS, t)
    kv_p = kv_ap.reshape(batch, t, 2 * nkv_a)[:, t - keep:]
    a_k_prompt = kv_p[..., :nkv_a].reshape(1, batch, keep, A_KV_HEADS, A_HD)
    a_v_prompt = kv_p[..., nkv_a:].reshape(1, batch, keep, A_KV_HEADS, A_HD)
    a_k_sample = kband[:, dec_t:].reshape(1, dec_batch, past_a, A_KV_HEADS, A_HD)
    a_v_sample = vband[:, dec_t:].reshape(1, dec_batch, past_a, A_KV_HEADS, A_HD)

    lam_init = 0.8 - 0.6 * float(np.exp(-0.3 * 1))
    lp = b_lambda[0].astype(F32)
    lam = jnp.exp(jnp.sum(lp[0] * lp[1])) - jnp.exp(jnp.sum(lp[2] * lp[3])) + lam_init
    wb = B_HEADS * 2 * B_HD
    col_scale = jnp.where(jnp.arange(3 * wb) < wb, B_Q_SCALE, 1.0).astype(F32)
    h, qkvh, k_bp, k_bs, v_bp, v_bs, vt = norm_proj(
        h, None, norm_mix[1], (w_b_qkv[0] * col_scale).astype(BF16), None,
        [(0, 3 * wb, "heads", BF16), (wb, 2 * wb, "split", F32), (2 * wb, 3 * wb, "split", F32),
         (2 * wb, 3 * wb, "heads_t", BF16)], tr, n_p, og=og, meta=meta)
    qb_b = min(256, t)
    o_bp = attn_b_prompt(qkvh, vt, lam, b_subln[0], lam_init, batch, t, qb_b)
    o_bs = attn_b_sample(qkvh, cache_b_k[0], cache_b_v[0], lam, b_subln[0], lam_init, n_p, dec_batch, dec_t)
    h, xn, meta, counts = out_router(o_bp, o_bs, h, None, w_b_o[0].astype(BF16), norm_ffn[1], w_route_group[1],
                                     b_route_group[1], w_route_expert[1], b_route_expert[1], True, tr)
    og = moe_experts(xn, meta, counts, w_gate, w_up, w_down, 1, tm)

    y_p, y_s = final_norm(h, og, meta, norm_final, n_p, tr)
    hd2 = 2 * B_HD
    return (y_p.reshape(batch, t, d), y_s.reshape(dec_batch, dec_t, d),
            a_k_prompt, a_v_prompt, a_k_sample, a_v_sample,
            k_bp.reshape(1, batch, t, B_HEADS, hd2), v_bp.reshape(1, batch, t, B_HEADS, hd2),
            k_bs.reshape(1, dec_batch, dec_t, B_HEADS, hd2), v_bs.reshape(1, dec_batch, dec_t, B_HEADS, hd2))
```

```python
import functools

import jax
import jax.numpy as jnp
import numpy as np
from jax import lax
from jax.experimental import pallas as pl
from jax.experimental.pallas import tpu as pltpu
from jax.experimental.pallas import tpu_sc as plsc

F32 = jnp.float32
BF16 = jnp.bfloat16
NEG_INF = float("-inf")

CHUNK = 64
WIN_CHUNKS = 2
A_HEADS, A_KV_HEADS, A_GROUP, A_HD = 16, 4, 4, 64
B_HEADS, B_HD = 8, 64
N_GROUPS, EXPERTS_PER_GROUP = 4, 8
N_EXPERTS = N_GROUPS * EXPERTS_PER_GROUP
NORM_EPS = 1e-6
SUBLN_EPS = 1e-5

LANES = 128
COL_CHUNK = 512
HEAD_GROUP = 2
SC_WINDOW = 128
SC_ROW_WORDS = 256
VMEM_LIMIT = 48 * 1024 * 1024


def _cparams(sem, flags=None):
    return pltpu.CompilerParams(dimension_semantics=sem, vmem_limit_bytes=VMEM_LIMIT, flags=flags)


def _alibi(n):
    return 2.0 ** (-8.0 * np.arange(1, n + 1) / n)


def _store_planes(ref, x):
    w = ref.shape[2]
    for p in range(ref.shape[0]):
        ref[p] = x[:, p * w:(p + 1) * w]


def _join_planes(ref, first, count):
    return jnp.concatenate([ref[first + p] for p in range(count)], axis=1)


def _add_expert_outputs(x, og_ref, meta_ref):
    half = og_ref.shape[1] // 2
    meta = meta_ref[...]
    return (x + meta[:, 2:3] * _unpack_pairs(og_ref[:, :half])) + meta[:, 3:4] * _unpack_pairs(og_ref[:, half:])


def _norm_proj_kernel(n_prompt_blocks, two_src, has_bias, combine, outs, *refs):
    refs = list(refs)
    xa_ref = refs.pop(0)
    xb_ref = refs.pop(0) if two_src else None
    og_ref, meta_ref = (refs.pop(0), refs.pop(0)) if combine else (None, None)
    g_ref = refs.pop(0)
    w_ref = refs.pop(0)
    b_ref = refs.pop(0) if has_bias else None
    out_refs = refs
    x = xa_ref[...]
    if two_src:
        x = jnp.where(pl.program_id(0) < n_prompt_blocks, x, xb_ref[...])
    if combine:
        x = _add_expert_outputs(x, og_ref, meta_ref)
        out_refs.pop(0)[...] = x
    ms = jnp.mean(x * x, axis=-1, keepdims=True)
    xn = (x * lax.rsqrt(ms + NORM_EPS) * g_ref[...]).astype(BF16)
    is_prompt = pl.program_id(0) < n_prompt_blocks
    out_refs = list(out_refs)
    targets = []
    for c0, c1, layout in outs:
        targets.append((c0, c1, layout, out_refs.pop(0), out_refs.pop(0) if layout == "split" else None))
    for s in range(0, w_ref.shape[1], COL_CHUNK):
        e = s + COL_CHUNK
        r = jnp.dot(xn, w_ref[:, s:e], preferred_element_type=F32)
        if has_bias:
            r = r + b_ref[:, s:e]
        for c0, c1, layout, o_ref, o2_ref in targets:
            if not (c0 <= s and e <= c1):
                continue
            rr = r.astype(o_ref.dtype)
            if layout == "flat":
                o_ref[:, s - c0:e - c0] = rr
            elif layout == "split":

                @pl.when(is_prompt)
                def _(rr=rr, o_ref=o_ref, s=s, e=e, c0=c0):
                    o_ref[:, s - c0:e - c0] = rr

                @pl.when(jnp.logical_not(is_prompt))
                def _(rr=rr, o2_ref=o2_ref, s=s, e=e, c0=c0):
                    o2_ref[:, s - c0:e - c0] = rr
            elif layout == "heads":
                for hh in range((e - s) // LANES):
                    o_ref[(s - c0) // LANES + hh] = rr[:, hh * LANES:(hh + 1) * LANES]
            else:
                for hh in range((e - s) // LANES):
                    o_ref[(s - c0) // LANES + hh] = r[:, hh * LANES:(hh + 1) * LANES].T.astype(o_ref.dtype)


def norm_proj(xa, xb, g, w, b, outs, tr, n_prompt, og=None, meta=None):
    na, d = xa.shape
    nb = 0 if xb is None else xb.shape[0]
    n = na + nb
    npb = n_prompt // tr
    two = xb is not None
    combine = og is not None
    in_specs = [pl.BlockSpec((tr, d), (lambda i: (jnp.minimum(i, npb - 1), 0)) if two else (lambda i: (i, 0)))]
    args = [xa]
    if two:
        in_specs.append(pl.BlockSpec((tr, d), lambda i: (jnp.maximum(i - npb, 0), 0)))
        args.append(xb)
    if combine:
        in_specs += [pl.BlockSpec((tr, og.shape[1]), lambda i: (i, 0)), pl.BlockSpec((tr, LANES), lambda i: (i, 0))]
        args += [og, meta]
    in_specs.append(pl.BlockSpec((1, d), lambda i: (0, 0)))
    args.append(g.reshape(1, d).astype(F32))
    in_specs.append(pl.BlockSpec(w.shape, lambda i: (0, 0)))
    args.append(w)
    if b is not None:
        in_specs.append(pl.BlockSpec((1, w.shape[1]), lambda i: (0, 0)))
        args.append(b.reshape(1, -1).astype(F32))
    out_shapes, out_specs = [], []
    if combine:
        out_shapes.append(jax.ShapeDtypeStruct((n, d), F32))
        out_specs.append(pl.BlockSpec((tr, d), lambda i: (i, 0)))
    for c0, c1, layout, dt in outs:
        assert c0 % COL_CHUNK == 0 and c1 % COL_CHUNK == 0
        if layout == "flat":
            out_shapes.append(jax.ShapeDtypeStruct((n, c1 - c0), dt))
            out_specs.append(pl.BlockSpec((tr, c1 - c0), lambda i: (i, 0)))
        elif layout == "split":
            out_shapes.append(jax.ShapeDtypeStruct((n_prompt, c1 - c0), dt))
            out_specs.append(pl.BlockSpec((tr, c1 - c0), lambda i: (jnp.minimum(i, npb - 1), 0)))
            out_shapes.append(jax.ShapeDtypeStruct((n - n_prompt, c1 - c0), dt))
            out_specs.append(pl.BlockSpec((tr, c1 - c0), lambda i: (jnp.maximum(i - npb, 0), 0)))
        elif layout == "heads":
            nh = (c1 - c0) // LANES
            out_shapes.append(jax.ShapeDtypeStruct((nh, n, LANES), dt))
            out_specs.append(pl.BlockSpec((nh, tr, LANES), lambda i: (0, i, 0)))
        else:
            assert layout == "heads_t"
            nh = (c1 - c0) // LANES
            out_shapes.append(jax.ShapeDtypeStruct((nh, LANES, n), dt))
            out_specs.append(pl.BlockSpec((nh, LANES, tr), lambda i: (0, 0, i)))
    kern = functools.partial(_norm_proj_kernel, npb, two, b is not None, combine,
                             [(c0, c1, lay) for c0, c1, lay, _ in outs])
    return pl.pallas_call(
        kern, grid=(n // tr,), in_specs=in_specs, out_specs=out_specs, out_shape=out_shapes,
        compiler_params=_cparams(("arbitrary",)), name="norm_proj")(*args)


def _attn_a_chunk(q, k, v, bias_ref, sink_ref, valid):
    nq = q.shape[0]
    outs = []
    for kh in range(A_KV_HEADS):
        qs = jnp.concatenate(
            [q[:, (kh * A_GROUP + g) * A_HD:(kh * A_GROUP + g + 1) * A_HD] for g in range(A_GROUP)], axis=0)
        qs = qs * jnp.asarray(A_HD ** -0.5, BF16)
        kk = k[:, kh * A_HD:(kh + 1) * A_HD]
        s = lax.dot_general(qs, kk, (((1,), (1,)), ((), ())), preferred_element_type=F32) + bias_ref[kh]
        if valid is not None:
            s = jnp.where(valid, s, NEG_INF)
        sk = sink_ref[kh]
        m = jnp.maximum(jnp.max(s, axis=-1, keepdims=True), sk)
        e = jnp.exp(s - m)
        den = jnp.sum(e, axis=-1, keepdims=True) + jnp.exp(sk - m)
        p = (e * (1.0 / den)).astype(BF16)
        o = jnp.dot(p, v[:, kh * A_HD:(kh + 1) * A_HD], preferred_element_type=F32)
        for g in range(A_GROUP):
            outs.append(o[g * nq:(g + 1) * nq])
    return jnp.concatenate(outs, axis=1)


def _attn_a_prompt_kernel(qb, q_ref, kvc_ref, kvp_ref, bias_ref, sink_ref, o_ref):
    cb = pl.program_id(1)
    back = CHUNK * WIN_CHUNKS
    nk = back + qb
    nkv = A_KV_HEADS * A_HD
    kvfull = jnp.concatenate([kvp_ref[...], kvc_ref[...]], axis=0)
    k_all = kvfull[:, :nkv].astype(BF16)
    vt_all = kvfull[:, nkv:].T.astype(BF16)
    key_pos = lax.broadcasted_iota(jnp.int32, (nk, 1), 0) + (cb * qb - back)
    before_start = jnp.where(key_pos >= 0, 0.0, NEG_INF)
    dn = (((1,), (1,)), ((), ()))
    outs = []
    for kh in range(A_KV_HEADS):
        qs = jnp.concatenate(
            [q_ref[:, (kh * A_GROUP + g) * A_HD:(kh * A_GROUP + g + 1) * A_HD] for g in range(A_GROUP)], axis=0)
        qs = qs * jnp.asarray(A_HD ** -0.5, BF16)
        st = lax.dot_general(k_all[:, kh * A_HD:(kh + 1) * A_HD], qs, dn, preferred_element_type=F32)
        st = st + bias_ref[kh] + before_start
        sk = sink_ref[kh]
        m = jnp.maximum(jnp.max(st, axis=0, keepdims=True), sk)
        e = jnp.exp(st - m)
        den = jnp.sum(e, axis=0, keepdims=True) + jnp.exp(sk - m)
        p = (e * (1.0 / den)).astype(BF16)
        ot = jnp.dot(vt_all[kh * A_HD:(kh + 1) * A_HD, :], p, preferred_element_type=F32)
        o = ot.T
        for g in range(A_GROUP):
            outs.append(o[g * qb:(g + 1) * qb])
    o_ref[...] = jnp.concatenate(outs, axis=1).astype(o_ref.dtype)


def _a_prompt_tables(qb, sinks):
    back = CHUNK * WIN_CHUNKS
    slopes = _alibi(A_HEADS).reshape(A_KV_HEADS, A_GROUP)
    qpos = np.arange(qb)
    kpos = np.arange(back + qb) - back
    dist = np.abs(kpos[:, None] - qpos[None, :]).astype(np.float64)
    band_lo = (qpos // CHUNK) * CHUNK - back
    in_band = (kpos[:, None] >= band_lo[None, :]) & (kpos[:, None] < band_lo[None, :] + back + CHUNK)
    bias = np.where(in_band[None, None], -slopes[:, :, None, None] * dist[None, None], -np.inf)
    bias = np.transpose(bias, (0, 2, 1, 3)).reshape(A_KV_HEADS, back + qb, A_GROUP * qb)
    sk = jnp.repeat(sinks.astype(F32).reshape(A_KV_HEADS, A_GROUP), qb, axis=1)[:, None, :]
    return jnp.asarray(bias, F32), sk


def _a_tables(qpos, kpos, sinks):
    slopes = _alibi(A_HEADS).reshape(A_KV_HEADS, A_GROUP)
    dist = np.abs(qpos[:, None] - kpos[None, :]).astype(np.float64)
    bias = -slopes[:, :, None, None] * dist
    nq = len(qpos)
    bias = jnp.asarray(bias.reshape(A_KV_HEADS, A_GROUP * nq, len(kpos)), F32)
    sk = jnp.repeat(sinks.astype(F32).reshape(A_KV_HEADS, A_GROUP), nq, axis=1)[..., None]
    return bias, sk


def attn_a_prompt(q_all, kv_all, sinks, batch, t, qb):
    nq = A_HEADS * A_HD
    nkv2 = 2 * A_KV_HEADS * A_HD
    back = CHUNK * WIN_CHUNKS
    bias, sk = _a_prompt_tables(qb, sinks)
    nblk = t // qb
    r = qb // back
    return pl.pallas_call(
        functools.partial(_attn_a_prompt_kernel, qb),
        grid=(batch, nblk),
        in_specs=[
            pl.BlockSpec((qb, nq), lambda b, i: (b * nblk + i, 0)),
            pl.BlockSpec((qb, nkv2), lambda b, i: (b * nblk + i, 0)),
            pl.BlockSpec((back, nkv2), lambda b, i: (jnp.maximum((b * nblk + i) * r - 1, 0), 0)),
            pl.BlockSpec(bias.shape, lambda b, i: (0, 0, 0)),
            pl.BlockSpec(sk.shape, lambda b, i: (0, 0, 0)),
        ],
        out_specs=pl.BlockSpec((qb, nq), lambda b, i: (b * nblk + i, 0)),
        out_shape=jax.ShapeDtypeStruct((batch * t, nq), BF16),
        compiler_params=_cparams(("arbitrary", "arbitrary")), name="attn_a_prompt",
    )(q_all, kv_all, kv_all, bias, sk)


def _attn_a_sample_kernel(q_ref, k_ref, v_ref, bias_ref, sink_ref, o_ref):
    o = _attn_a_chunk(q_ref[...], k_ref[...].astype(BF16), v_ref[...].astype(BF16), bias_ref, sink_ref, None)
    o_ref[...] = o.astype(o_ref.dtype)


def attn_a_sample(q_all, kband, vband, sinks, row0, dec_batch, dec_t):
    nq = A_HEADS * A_HD
    s = kband.shape[1]
    past = s - dec_t
    kpos = np.concatenate([np.arange(past) - past, np.arange(dec_t)])
    bias, sk = _a_tables(np.arange(dec_t), kpos, sinks)
    blk0 = row0 // dec_t
    return pl.pallas_call(
        _attn_a_sample_kernel,
        grid=(dec_batch,),
        in_specs=[
            pl.BlockSpec((dec_t, nq), lambda b: (blk0 + b, 0)),
            pl.BlockSpec((None, s, kband.shape[2]), lambda b: (b, 0, 0)),
            pl.BlockSpec((None, s, vband.shape[2]), lambda b: (b, 0, 0)),
            pl.BlockSpec(bias.shape, lambda b: (0, 0, 0)),
            pl.BlockSpec(sk.shape, lambda b: (0, 0, 0)),
        ],
        out_specs=pl.BlockSpec((dec_t, nq), lambda b: (b, 0)),
        out_shape=jax.ShapeDtypeStruct((dec_batch * dec_t, nq), BF16),
        compiler_params=_cparams(("arbitrary",)), name="attn_a_sample",
    )(q_all, kband, vband, bias, sk)


B_Q_SCALE = B_HD ** -0.5 * float(np.log2(np.e))
LOG2E = float(np.log2(np.e))


def _stack_maps(qh):
    lane = lax.broadcasted_iota(jnp.int32, qh.shape, 1)
    zero = jnp.zeros_like(qh)
    return jnp.concatenate([jnp.where(lane < B_HD, qh, zero), jnp.where(lane >= B_HD, qh, zero)], axis=0)


def _diff_finish(acc, l, nq, lam, lam_init, subln):
    inv = 1.0 / l
    o = acc[:nq] * inv[:nq] - lam * (acc[nq:] * inv[nq:])
    ms = jnp.mean(o * o, axis=-1, keepdims=True)
    return o * lax.rsqrt(ms + SUBLN_EPS) * subln * (1.0 - lam_init)


def _attn_b_prompt_kernel(qb, lam_init, slope_ref, lam_ref, q_ref, k_ref, vt_ref, kbias_ref,
                          subln_ref, o_ref, acc_ref, t0_ref, t1_ref, p0_ref, p1_ref):
    i = pl.program_id(1)
    lam = lam_ref[0]
    dn = (((1,), (1,)), ((), ()))
    t_refs = (t0_ref, t1_ref)
    p_refs = (p0_ref, p1_ref)

    def group_body(hg, carry):
        heads = [hg * HEAD_GROUP + u for u in range(HEAD_GROUP)]
        slopes = [slope_ref[h] for h in heads]
        qqs = [_stack_maps(q_ref[h]) for h in heads]

        def scores(j, slot):
            start = pl.multiple_of(j * qb, qb)
            for u, h in enumerate(heads):
                t_refs[slot][u] = lax.dot_general(k_ref[h, pl.ds(start, qb), :], qqs[u], dn,
                                                  preferred_element_type=F32)

        def softmax_step(j, slot, c):
            out = []
            on_diag = (j == i).astype(jnp.int32)
            for u, h in enumerate(heads):
                m, l, _ = c[u]
                t = t_refs[slot][u] + kbias_ref[h, on_diag]
                off = slopes[u] * jnp.full((1, 2 * qb), (j - i) * qb, jnp.int32).astype(F32)
                m_new = jnp.maximum(m, jnp.max(t, axis=0, keepdims=True) + off)
                alpha = jnp.exp2(m - m_new)
                p = jnp.exp2(t + (off - m_new))
                p_refs[slot][u] = p.astype(BF16)
                out.append((m_new, alpha * l + jnp.sum(p, axis=0, keepdims=True), alpha))
            return tuple(out)

        def accumulate(j, slot, c):
            start = pl.multiple_of(jnp.maximum(j, 0) * qb, qb)
            for u, h in enumerate(heads):
                pv = jnp.dot(vt_ref[h, :, pl.ds(start, qb)], p_refs[slot][u], preferred_element_type=F32)
                acc_ref[u] = c[u][2] * acc_ref[u] + pv

        def step(j, c, slot):
            scores(jnp.minimum(j + 1, i), 1 - slot)
            c_new = softmax_step(j, slot, c)
            accumulate(j - 1, 1 - slot, c)
            return c_new

        def drain(c, slot):
            accumulate(i, slot, c)
            return 0

        acc_ref[...] = jnp.zeros_like(acc_ref)
        p1_ref[...] = jnp.zeros_like(p1_ref)
        scores(0, 0)
        init = tuple((jnp.full((1, 2 * qb), NEG_INF, F32), jnp.zeros((1, 2 * qb), F32), jnp.ones((1, 2 * qb), F32))
                     for _ in heads)
        c = lax.fori_loop(
            0, i + 1, lambda j, c: lax.cond(j % 2 == 0, lambda: step(j, c, 0), lambda: step(j, c, 1)), init)
        lax.cond(i % 2 == 0, lambda: drain(c, 0), lambda: drain(c, 1))
        for u, h in enumerate(heads):
            inv = 1.0 / c[u][1]
            acc = acc_ref[u]
            ot = acc[:, :qb] * inv[:, :qb] - lam * (acc[:, qb:] * inv[:, qb:])
            ms = jnp.mean(ot * ot, axis=0, keepdims=True)
            ot = ot * lax.rsqrt(ms + SUBLN_EPS) * subln_ref[...] * (1.0 - lam_init)
            o_ref[h] = ot.T.astype(o_ref.dtype)
        return carry

    lax.fori_loop(0, B_HEADS // HEAD_GROUP, group_body, 0)


def attn_b_prompt(qkvh, vt, lam, subln, lam_init, batch, t, qb):
    nblk = t // qb
    qq = np.arange(qb)
    slope2 = (_alibi(B_HEADS) * LOG2E)[:, None, None]
    kcol = np.broadcast_to(qq[:, None], (qb, qb)).astype(np.float64)
    tri = -2.0 * np.maximum(qq[:, None] - qq[None, :], 0)
    msk = np.where((qq[:, None] // CHUNK) <= (qq[None, :] // CHUNK), 0.0, -np.inf)
    both = lambda a: np.concatenate([a, a], axis=-1)
    kbias = np.stack([both(slope2 * kcol[None]), both(slope2 * (kcol + tri)[None] + msk[None])], axis=1)
    kbias = jnp.asarray(kbias, F32)
    slopes = jnp.asarray(_alibi(B_HEADS) * LOG2E, F32)
    subln_t = jnp.broadcast_to(subln.astype(F32).reshape(LANES, 1), (LANES, qb))
    const = lambda b, i, *_: (0, 0)
    grid_spec = pltpu.PrefetchScalarGridSpec(
        num_scalar_prefetch=2, grid=(batch, nblk),
        in_specs=[
            pl.BlockSpec((B_HEADS, qb, LANES), lambda b, i, *_: (0, b * nblk + i, 0)),
            pl.BlockSpec((B_HEADS, t, LANES), lambda b, i, *_: (1, b, 0)),
            pl.BlockSpec((B_HEADS, LANES, t), lambda b, i, *_: (0, 0, b)),
            pl.BlockSpec(kbias.shape, lambda b, i, *_: (0, 0, 0, 0)),
            pl.BlockSpec(subln_t.shape, const),
        ],
        out_specs=pl.BlockSpec((B_HEADS, qb, LANES), lambda b, i, *_: (0, b * nblk + i, 0)),
        scratch_shapes=[pltpu.VMEM((HEAD_GROUP, LANES, 2 * qb), F32),
                        pltpu.VMEM((HEAD_GROUP, qb, 2 * qb), F32), pltpu.VMEM((HEAD_GROUP, qb, 2 * qb), F32),
                        pltpu.VMEM((HEAD_GROUP, qb, 2 * qb), BF16), pltpu.VMEM((HEAD_GROUP, qb, 2 * qb), BF16)],
    )
    return pl.pallas_call(
        functools.partial(_attn_b_prompt_kernel, qb, lam_init), grid_spec=grid_spec,
        out_shape=jax.ShapeDtypeStruct((B_HEADS, batch * t, LANES), BF16),
        compiler_params=_cparams(("arbitrary", "arbitrary")), name="attn_b_prompt",
    )(slopes, lam.reshape(1).astype(F32), qkvh, qkvh, vt, kbias, subln_t)


def _attn_b_sample_kernel(dec_t, past, lam_init, slopes, lam_ref, q_ref, kn_ref, vn_ref, ck_ref, cv_ref,
                          tnew_ref, subln_ref, o_ref):
    lam = lam_ref[0]
    subln = subln_ref[...]
    colc = lax.broadcasted_iota(jnp.int32, (1, past), 1).astype(F32) - float(past)
    for h in range(B_HEADS):
        slope = float(slopes[h]) * LOG2E
        qq = _stack_maps(q_ref[h])
        kc = ck_ref[pl.ds(h, past, stride=B_HEADS), :].astype(BF16)
        vc = cv_ref[pl.ds(h, past, stride=B_HEADS), :].astype(BF16)
        dn = (((1,), (1,)), ((), ()))
        s_c = lax.dot_general(qq, kc, dn, preferred_element_type=F32) + slope * colc
        s_n = lax.dot_general(qq, kn_ref[h], dn, preferred_element_type=F32) + slope * tnew_ref[...]
        m = jnp.maximum(jnp.max(s_c, axis=-1, keepdims=True), jnp.max(s_n, axis=-1, keepdims=True))
        p_c = jnp.exp2(s_c - m)
        p_n = jnp.exp2(s_n - m)
        l = jnp.sum(p_c, axis=-1, keepdims=True) + jnp.sum(p_n, axis=-1, keepdims=True)
        acc = (jnp.dot(p_c.astype(BF16), vc, preferred_element_type=F32)
               + jnp.dot(p_n.astype(BF16), vn_ref[h], preferred_element_type=F32))
        o_ref[h] = _diff_finish(acc, l, dec_t, lam, lam_init, subln).astype(o_ref.dtype)


def attn_b_sample(qkvh, cache_k, cache_v, lam, subln, lam_init, row0, dec_batch, dec_t):
    past = cache_k.shape[1]
    qi = np.arange(dec_t)
    tnew = (qi[:, None] - np.abs(qi[:, None] - qi[None, :])).astype(np.float32)
    tnew = jnp.asarray(np.concatenate([tnew, tnew], axis=0))
    blk0 = row0 // dec_t
    rows = past * B_HEADS
    cache_k = cache_k.reshape(dec_batch, rows, LANES)
    cache_v = cache_v.reshape(dec_batch, rows, LANES)
    grid_spec = pltpu.PrefetchScalarGridSpec(
        num_scalar_prefetch=1, grid=(dec_batch,),
        in_specs=[
            pl.BlockSpec((B_HEADS, dec_t, LANES), lambda b, *_: (0, blk0 + b, 0)),
            pl.BlockSpec((B_HEADS, dec_t, LANES), lambda b, *_: (1, blk0 + b, 0)),
            pl.BlockSpec((B_HEADS, dec_t, LANES), lambda b, *_: (2, blk0 + b, 0)),
            pl.BlockSpec((None, rows, LANES), lambda b, *_: (b, 0, 0)),
            pl.BlockSpec((None, rows, LANES), lambda b, *_: (b, 0, 0)),
            pl.BlockSpec(tnew.shape, lambda b, *_: (0, 0)),
            pl.BlockSpec((1, LANES), lambda b, *_: (0, 0)),
        ],
        out_specs=pl.BlockSpec((B_HEADS, dec_t, LANES), lambda b, *_: (0, b, 0)),
    )
    return pl.pallas_call(
        functools.partial(_attn_b_sample_kernel, dec_t, past, lam_init, tuple(_alibi(B_HEADS))),
        grid_spec=grid_spec,
        out_shape=jax.ShapeDtypeStruct((B_HEADS, dec_batch * dec_t, LANES), BF16),
        compiler_params=_cparams(("arbitrary",)), name="attn_b_sample",
    )(lam.reshape(1).astype(F32), qkvh, qkvh, qkvh, cache_k, cache_v, tnew, subln.reshape(1, LANES).astype(F32))


def _route(logits):
    lane = lax.broadcasted_iota(jnp.int32, logits.shape, 1).astype(F32)
    big = float(LANES)
    lg = jnp.where(lane < N_GROUPS, logits, NEG_INF)
    gmax = jnp.max(lg, axis=-1, keepdims=True)
    g_idx = jnp.min(jnp.where(lg == gmax, lane, big), axis=-1, keepdims=True)
    g_prob = 1.0 / jnp.sum(jnp.exp(lg - gmax), axis=-1, keepdims=True)
    lo = N_GROUPS + EXPERTS_PER_GROUP * g_idx
    le = jnp.where((lane >= lo) & (lane < lo + EXPERTS_PER_GROUP), logits, NEG_INF)
    m1 = jnp.max(le, axis=-1, keepdims=True)
    i1 = jnp.min(jnp.where(le == m1, lane, big), axis=-1, keepdims=True)
    le2 = jnp.where(lane == i1, NEG_INF, le)
    m2 = jnp.max(le2, axis=-1, keepdims=True)
    i2 = jnp.min(jnp.where(le2 == m2, lane, big), axis=-1, keepdims=True)
    t = jnp.exp(m2 - m1)
    ga = 1.0 / (1.0 + t)
    gb = t * ga
    return lane, i1, i2, g_prob * ga, g_prob * gb


def _unpack_pairs(u):
    lo = pltpu.bitcast(u << 16, F32)
    hi = pltpu.bitcast(u & jnp.uint32(0xFFFF0000), F32)
    return jnp.concatenate([lo, hi], axis=1)


def _pack_pairs(x):
    bits = pltpu.bitcast(x.astype(BF16).astype(F32), jnp.uint32)
    w = x.shape[1] // 2
    return (bits[:, :w] >> 16) | bits[:, w:]


def _out_router_kernel(n_prompt_blocks, heads_in, two_o, two_res, *refs):
    refs = list(refs)
    oa_ref = refs.pop(0)
    ob_ref = refs.pop(0) if two_o else None
    ra_ref = refs.pop(0)
    rb_ref = refs.pop(0) if two_res else None
    wo_ref, g_ref, wr_ref, wr2_ref, br_ref, ltri_ref, h_ref, xn_ref, meta_ref, cnt_ref, carry_ref = refs
    is_prompt = pl.program_id(0) < n_prompt_blocks

    def load_o(ref):
        if heads_in:
            return jnp.concatenate([ref[hh] for hh in range(ref.shape[0])], axis=1)
        return ref[...]

    o = load_o(oa_ref)
    if two_o:
        o = jnp.where(is_prompt, o, load_o(ob_ref))
    res = ra_ref[...]
    if two_res:
        res = jnp.where(is_prompt, res, rb_ref[...])
    h = res + jnp.dot(o, wo_ref[...], preferred_element_type=F32)
    h_ref[...] = h
    ms = jnp.mean(h * h, axis=-1, keepdims=True)
    xn = h * lax.rsqrt(ms + NORM_EPS) * g_ref[...]
    x_hi = xn.astype(BF16)
    _store_planes(xn_ref, _pack_pairs(xn))
    x_lo = (xn - x_hi.astype(F32)).astype(BF16)
    r = jnp.dot(x_hi, wr_ref[...], preferred_element_type=F32)
    logits = (r[:, :LANES] + r[:, LANES:]) + jnp.dot(x_lo, wr2_ref[...], preferred_element_type=F32) + br_ref[...]
    lane, i1, i2, g1, g2 = _route(logits)

    @pl.when(pl.program_id(0) == 0)
    def _():
        carry_ref[...] = jnp.zeros_like(carry_ref)

    sel1 = lane == i1
    sel2 = lane == i2
    onehot = jnp.where(sel1 | sel2, 1.0, 0.0)
    before = jnp.dot(ltri_ref[...], onehot.astype(BF16), preferred_element_type=F32) + carry_ref[...]
    rank1 = jnp.sum(jnp.where(sel1, before, 0.0), axis=-1, keepdims=True)
    rank2 = jnp.sum(jnp.where(sel2, before, 0.0), axis=-1, keepdims=True)
    carry_ref[...] = carry_ref[...] + jnp.sum(onehot, axis=0, keepdims=True)
    cnt_ref[...] = carry_ref[...]
    cols = [i1 - N_GROUPS, i2 - N_GROUPS, g1, g2, rank1, rank2]
    meta = jnp.zeros_like(logits)
    for c, v in enumerate(cols):
        meta = jnp.where(lane == c, v, meta)
    meta_ref[...] = meta


def out_router(oa, ob, ra, rb, wo, g, w_rg, b_rg, w_re, b_re, heads_in, tr):
    d = ra.shape[1]
    if heads_in:
        na = oa.shape[1]
        nb = 0 if ob is None else ob.shape[1]
    else:
        na = oa.shape[0]
        nb = 0 if ob is None else ob.shape[0]
    n = na + nb
    npb = na // tr
    two_o = ob is not None
    two_res = rb is not None
    first = lambda i: (jnp.minimum(i, npb - 1), 0)
    second = lambda i: (jnp.maximum(i - npb, 0), 0)
    plain = lambda i: (i, 0)
    in_specs, args = [], []

    def add_o(x, imap):
        if heads_in:
            in_specs.append(pl.BlockSpec((x.shape[0], tr, LANES), lambda i: (0, imap(i)[0], 0)))
        else:
            in_specs.append(pl.BlockSpec((tr, x.shape[1]), imap))
        args.append(x)

    add_o(oa, first if two_o else plain)
    if two_o:
        add_o(ob, second)
    in_specs.append(pl.BlockSpec((tr, d), first if two_res else plain))
    args.append(ra)
    if two_res:
        in_specs.append(pl.BlockSpec((tr, d), second))
        args.append(rb)
    wr = jnp.zeros((d, LANES), F32).at[:, :N_GROUPS].set(w_rg.astype(F32))
    wr = wr.at[:, N_GROUPS:N_GROUPS + N_EXPERTS].set(w_re.astype(F32))
    wr_hi = wr.astype(BF16)
    wr_lo = (wr - wr_hi.astype(F32)).astype(BF16)
    br = jnp.zeros((1, LANES), F32).at[0, :N_GROUPS].set(b_rg.astype(F32))
    br = br.at[0, N_GROUPS:N_GROUPS + N_EXPERTS].set(b_re.astype(F32))
    ltri = jnp.asarray(np.tril(np.ones((tr, tr), np.float32), -1), BF16)
    consts = [wo, g.reshape(1, d).astype(F32), jnp.concatenate([wr_hi, wr_lo], axis=1), wr_hi, br, ltri]
    for cst in consts:
        in_specs.append(pl.BlockSpec(cst.shape, lambda i: (0, 0)))
        args.append(cst)
    parts = d // 2 // SC_ROW_WORDS
    out_shape = [jax.ShapeDtypeStruct((n, d), F32), jax.ShapeDtypeStruct((parts, n, SC_ROW_WORDS), jnp.uint32),
                 jax.ShapeDtypeStruct((n, LANES), F32), jax.ShapeDtypeStruct((1, LANES), F32)]
    out_specs = [pl.BlockSpec((tr, d), plain), pl.BlockSpec((parts, tr, SC_ROW_WORDS), lambda i: (0, i, 0)),
                 pl.BlockSpec((tr, LANES), plain), pl.BlockSpec((1, LANES), lambda i: (0, 0))]
    return pl.pallas_call(
        functools.partial(_out_router_kernel, npb, heads_in, two_o, two_res),
        grid=(n // tr,), in_specs=in_specs, out_specs=out_specs, out_shape=out_shape,
        scratch_shapes=[pltpu.VMEM((1, LANES), F32)],
        compiler_params=_cparams(("arbitrary",)), name="out_router")(*args)


def _expert_kernel(te_ref, nt_ref, x_ref, wg_ref, wu_ref, wd_ref, o_ref):
    @pl.when(pl.program_id(0) < nt_ref[0])
    def _():
        x = _unpack_pairs(_join_planes(x_ref, 0, x_ref.shape[0])).astype(BF16)
        a = jnp.dot(x, wg_ref[...].astype(BF16), preferred_element_type=F32)
        u = jnp.dot(x, wu_ref[...].astype(BF16), preferred_element_type=F32)
        hh = (a * (1.0 / (1.0 + jnp.exp(-a))) * u).astype(BF16)
        _store_planes(o_ref, _pack_pairs(jnp.dot(hh, wd_ref[...].astype(BF16), preferred_element_type=F32)))

    @pl.when(pl.program_id(0) >= nt_ref[0])
    def _():
        o_ref[...] = jnp.zeros_like(o_ref)


def expert_mlp(xs, tile_expert, n_tiles, w_gate, w_up, w_down, layer, tm):
    parts, p, w = xs.shape
    d, f = w_gate.shape[-2:]
    epg = w_gate.shape[2]
    wmap = lambda t, te, nt: (layer, te[t] // epg, te[t] % epg, 0, 0)
    grid_spec = pltpu.PrefetchScalarGridSpec(
        num_scalar_prefetch=2, grid=(p // tm,),
        in_specs=[
            pl.BlockSpec((parts, tm, w), lambda t, te, nt: (0, t, 0)),
            pl.BlockSpec((None, None, None, d, f), wmap),
            pl.BlockSpec((None, None, None, d, f), wmap),
            pl.BlockSpec((None, None, None, f, d), wmap),
        ],
        out_specs=pl.BlockSpec((parts, tm, w), lambda t, te, nt: (0, t, 0)),
    )
    return pl.pallas_call(
        _expert_kernel, grid_spec=grid_spec, out_shape=jax.ShapeDtypeStruct((parts, p, w), jnp.uint32),
        compiler_params=_cparams(("arbitrary",)), name="expert_mlp")(tile_expert, n_tiles, xs, w_gate, w_up, w_down)


def sc_scatter_rows(x2, idx, n_out):
    mesh = plsc.VectorSubcoreMesh(core_axis_name="c", subcore_axis_name="s")
    n_src_win = x2.shape[0] // SC_WINDOW

    @functools.partial(pl.kernel, out_type=jax.ShapeDtypeStruct((n_out, SC_ROW_WORDS), x2.dtype), mesh=mesh)
    def scatter(x_hbm, i_hbm, o_hbm):
        def body(x_vmem, i_vmem):
            pltpu.sync_copy(x_vmem, o_hbm.at[i_vmem.at[0]])

        pltpu.emit_pipeline(
            body, grid=(n_out // SC_WINDOW,),
            in_specs=[pl.BlockSpec((SC_WINDOW, SC_ROW_WORDS), lambda i: (lax.rem(i, n_src_win), 0)),
                      pl.BlockSpec((1, SC_WINDOW), lambda i: (0, i))],
            out_specs=[], core_axis_name=("c", "s"), dimension_semantics=(pltpu.PARALLEL,),
        )(x_hbm, i_hbm)

    return scatter(x2, idx)


def sc_gather_rows(x2, idx):
    mesh = plsc.VectorSubcoreMesh(core_axis_name="c", subcore_axis_name="s")
    n_col, n = idx.shape

    @functools.partial(pl.kernel, out_type=jax.ShapeDtypeStruct((n, n_col * SC_ROW_WORDS), x2.dtype), mesh=mesh)
    def gather(x_hbm, i_hbm, o_hbm):
        def body(i_vmem, o_vmem):
            pltpu.sync_copy(x_hbm.at[i_vmem.at[0]], o_vmem)

        pltpu.emit_pipeline(
            body, grid=(n // SC_WINDOW, n_col),
            in_specs=[pl.BlockSpec((1, SC_WINDOW), lambda i, c: (c, i))],
            out_specs=[pl.BlockSpec((SC_WINDOW, SC_ROW_WORDS), lambda i, c: (i, c))],
            core_axis_name=("c", "s"), dimension_semantics=(pltpu.PARALLEL, pltpu.ARBITRARY),
        )(i_hbm, o_hbm)

    return gather(x2, idx)


def moe_dispatch(xn_packed, meta, counts, tm):
    parts, n, w = xn_packed.shape
    e = meta[:, 0:2].astype(jnp.int32)
    rank = meta[:, 4:6].astype(jnp.int32)
    counts = counts[0, N_GROUPS:N_GROUPS + N_EXPERTS].astype(jnp.int32)
    padded = ((counts + tm - 1) // tm) * tm
    pend = jnp.cumsum(padded)
    pstart = pend - padded
    ids = jnp.arange(N_EXPERTS, dtype=jnp.int32)
    pos = jnp.sum(jnp.where(e[..., None] == ids, pstart, 0), axis=-1) + rank
    p = 2 * n + N_EXPERTS * tm
    n_pad = p - 2 * n
    gap_len = jnp.concatenate([padded - counts, (p - pend[-1]).reshape(1)])
    gap_first = jnp.concatenate([pstart + counts, pend[-1:]])
    gap_end = jnp.cumsum(gap_len)
    k = jnp.arange(n_pad, dtype=jnp.int32)
    gap = jnp.sum((k[:, None] >= gap_end[None, :]).astype(jnp.int32), axis=1)
    shift = gap_first - (gap_end - gap_len)
    pad_pos = k + jnp.sum(jnp.where(gap[:, None] == jnp.arange(N_EXPERTS + 1), shift, 0), axis=-1)
    planes = lambda rows: jnp.concatenate([rows + q * p for q in range(parts)])
    scat_idx = jnp.concatenate([planes(pos[:, 0]), planes(pos[:, 1]), planes(pad_pos)]).reshape(1, parts * p)
    gath_idx = jnp.stack([pos[:, k] + q * p for k in range(2) for q in range(parts)])
    xs = sc_scatter_rows(xn_packed.reshape(parts * n, w), scat_idx, parts * p).reshape(parts, p, w)
    tile_start = jnp.arange(p // tm, dtype=jnp.int32) * tm
    tile_expert = jnp.sum((tile_start[:, None] >= pend[None, :]).astype(jnp.int32), axis=1)
    tile_expert = jnp.minimum(tile_expert, N_EXPERTS - 1)
    n_tiles = (pend[-1] // tm).astype(jnp.int32).reshape(1)
    return xs, tile_expert, n_tiles, gath_idx


def moe_experts(xn_packed, meta, counts, w_gate, w_up, w_down, layer, tm):
    parts, n, w = xn_packed.shape
    xs, tile_expert, n_tiles, gath_idx = moe_dispatch(xn_packed, meta, counts, tm)
    o = expert_mlp(xs, tile_expert, n_tiles, w_gate, w_up, w_down, layer, tm)
    return sc_gather_rows(o.reshape(parts * o.shape[1], w), gath_idx)


def _final_norm_kernel(n_prompt_blocks, h_ref, og_ref, meta_ref, g_ref, yp_ref, ys_ref):
    x = _add_expert_outputs(h_ref[...], og_ref, meta_ref)
    ms = jnp.mean(x * x, axis=-1, keepdims=True)
    y = x * lax.rsqrt(ms + NORM_EPS) * g_ref[...]
    i = pl.program_id(0)

    @pl.when(i < n_prompt_blocks)
    def _():
        yp_ref[...] = y

    @pl.when(i >= n_prompt_blocks)
    def _():
        ys_ref[...] = y


def final_norm(h, og, meta, g, n_prompt, tr):
    n, d = h.shape
    npb = n_prompt // tr
    return pl.pallas_call(
        functools.partial(_final_norm_kernel, npb),
        grid=(n // tr,),
        in_specs=[pl.BlockSpec((tr, d), lambda i: (i, 0)), pl.BlockSpec((tr, og.shape[1]), lambda i: (i, 0)),
                  pl.BlockSpec((tr, LANES), lambda i: (i, 0)), pl.BlockSpec((1, d), lambda i: (0, 0))],
        out_specs=[pl.BlockSpec((tr, d), lambda i: (jnp.minimum(i, npb - 1), 0)),
                   pl.BlockSpec((tr, d), lambda i: (jnp.maximum(i - npb, 0), 0))],
        out_shape=[jax.ShapeDtypeStruct((n_prompt, d), F32), jax.ShapeDtypeStruct((n - n_prompt, d), F32)],
        compiler_params=_cparams(("arbitrary",)), name="final_norm")(h, og, meta, g.reshape(1, d).astype(F32))


def _pick_rows_tile(n_prompt, n_sample, largest=512):
    for tr in (1024, 512, 256, 128, 64, 32, 16, 8):
        if tr <= largest and n_prompt % tr == 0 and n_sample % tr == 0:
            return tr
    raise ValueError("row counts must be multiples of 8")


def kernel(x_prompt, x_sample, cache_a_k, cache_a_v, cache_b_k, cache_b_v, norm_mix, norm_ffn, norm_final,
           w_a_qkv, b_a_qkv, a_sinks, w_a_o, w_b_qkv, b_lambda, b_subln, w_b_o,
           w_route_group, b_route_group, w_route_expert, b_route_expert, w_gate, w_up, w_down):
    batch, t, d = x_prompt.shape
    dec_batch, dec_t, _ = x_sample.shape
    n_p, n_s = batch * t, dec_batch * dec_t
    tr = _pick_rows_tile(n_p, n_s)
    tm = min(512, tr * 4)
    xp = x_prompt.reshape(n_p, d)
    xs = x_sample.reshape(n_s, d)
    nq_a = A_HEADS * A_HD
    nkv_a = A_KV_HEADS * A_HD

    q_a, kv_ap, kv_as = norm_proj(xp, xs, norm_mix[0], w_a_qkv[0].astype(BF16), b_a_qkv[0],
                                  [(0, nq_a, "flat", BF16), (nq_a, nq_a + 2 * nkv_a, "split", F32)], tr, n_p)
    qb_a = min(256, t)
    o_ap = attn_a_prompt(q_a, kv_ap, a_sinks[0], batch, t, qb_a)
    kv_s = kv_as.reshape(dec_batch, dec_t, 2 * nkv_a)
    past_a = cache_a_k.shape[2]
    kband = jnp.concatenate([cache_a_k[0].reshape(dec_batch, past_a, nkv_a), kv_s[..., :nkv_a]], axis=1)
    vband = jnp.concatenate([cache_a_v[0].reshape(dec_batch, past_a, nkv_a), kv_s[..., nkv_a:]], axis=1)
    o_as = attn_a_sample(q_a, kband, vband, a_sinks[0], n_p, dec_batch, dec_t)
    h, xn, meta, counts = out_router(o_ap, o_as, xp, xs, w_a_o[0].astype(BF16), norm_ffn[0], w_route_group[0],
                                     b_route_group[0], w_route_expert[0], b_route_expert[0], False, tr)
    og = moe_experts(xn, meta, counts, w_gate, w_up, w_down, 0, tm)

    keep = min(CHUNK * WIN_CHUNKS, t)
    kv_p = kv_ap.reshape(batch, t, 2 * nkv_a)[:, t - keep:]
    a_k_prompt = kv_p[..., :nkv_a].reshape(1, batch, keep, A_KV_HEADS, A_HD)
    a_v_prompt = kv_p[..., nkv_a:].reshape(1, batch, keep, A_KV_HEADS, A_HD)
    a_k_sample = kband[:, dec_t:].reshape(1, dec_batch, past_a, A_KV_HEADS, A_HD)
    a_v_sample = vband[:, dec_t:].reshape(1, dec_batch, past_a, A_KV_HEADS, A_HD)

    lam_init = 0.8 - 0.6 * float(np.exp(-0.3 * 1))
    lp = b_lambda[0].astype(F32)
    lam = jnp.exp(jnp.sum(lp[0] * lp[1])) - jnp.exp(jnp.sum(lp[2] * lp[3])) + lam_init
    wb = B_HEADS * 2 * B_HD
    col_scale = jnp.where(jnp.arange(3 * wb) < wb, B_Q_SCALE, 1.0).astype(F32)
    h, qkvh, k_bp, k_bs, v_bp, v_bs, vt = norm_proj(
        h, None, norm_mix[1], (w_b_qkv[0] * col_scale).astype(BF16), None,
        [(0, 3 * wb, "heads", BF16), (wb, 2 * wb, "split", F32), (2 * wb, 3 * wb, "split", F32),
         (2 * wb, 3 * wb, "heads_t", BF16)], tr, n_p, og=og, meta=meta)
    qb_b = min(256, t)
    o_bp = attn_b_prompt(qkvh, vt, lam, b_subln[0], lam_init, batch, t, qb_b)
    o_bs = attn_b_sample(qkvh, cache_b_k[0], cache_b_v[0], lam, b_subln[0], lam_init, n_p, dec_batch, dec_t)
    h, xn, meta, counts = out_router(o_bp, o_bs, h, None, w_b_o[0].astype(BF16), norm_ffn[1], w_route_group[1],
                                     b_route_group[1], w_route_expert[1], b_route_expert[1], True, tr)
    og = moe_experts(xn, meta, counts, w_gate, w_up, w_down, 1, tm)

    y_p, y_s = final_norm(h, og, meta, norm_final, n_p, _pick_rows_tile(n_p, n_s, 1024))
    hd2 = 2 * B_HD
    return (y_p.reshape(batch, t, d), y_s.reshape(dec_batch, dec_t, d),
            a_k_prompt, a_v_prompt, a_k_sample, a_v_sample,
            k_bp.reshape(1, batch, t, B_HEADS, hd2), v_bp.reshape(1, batch, t, B_HEADS, hd2),
            k_bs.reshape(1, dec_batch, dec_t, B_HEADS, hd2), v_bs.reshape(1, dec_batch, dec_t, B_HEADS, hd2))
```

```python
import functools

import jax
import jax.numpy as jnp
import numpy as np
from jax import lax
from jax.experimental import pallas as pl
from jax.experimental.pallas import tpu as pltpu
from jax.experimental.pallas import tpu_sc as plsc

F32 = jnp.float32
BF16 = jnp.bfloat16
NEG_INF = float("-inf")

CHUNK = 64
WIN_CHUNKS = 2
A_HEADS, A_KV_HEADS, A_GROUP, A_HD = 16, 4, 4, 64
B_HEADS, B_HD = 8, 64
N_GROUPS, EXPERTS_PER_GROUP = 4, 8
N_EXPERTS = N_GROUPS * EXPERTS_PER_GROUP
NORM_EPS = 1e-6
SUBLN_EPS = 1e-5

LANES = 128
COL_CHUNK = 512
HEAD_GROUP = 2
SC_WINDOW = 128
SC_ROW_WORDS = 256
VMEM_LIMIT = 48 * 1024 * 1024


def _cparams(sem, flags=None):
    return pltpu.CompilerParams(dimension_semantics=sem, vmem_limit_bytes=VMEM_LIMIT, flags=flags)


def _alibi(n):
    return 2.0 ** (-8.0 * np.arange(1, n + 1) / n)


def _store_planes(ref, x):
    w = ref.shape[2]
    for p in range(ref.shape[0]):
        ref[p] = x[:, p * w:(p + 1) * w]


def _join_planes(ref, first, count):
    return jnp.concatenate([ref[first + p] for p in range(count)], axis=1)


def _add_expert_outputs(x, og_ref, meta_ref):
    half = og_ref.shape[1] // 2
    meta = meta_ref[...]
    return (x + meta[:, 2:3] * _unpack_pairs(og_ref[:, :half])) + meta[:, 3:4] * _unpack_pairs(og_ref[:, half:])


def _norm_proj_kernel(n_prompt_blocks, two_src, has_bias, combine, outs, *refs):
    refs = list(refs)
    xa_ref = refs.pop(0)
    xb_ref = refs.pop(0) if two_src else None
    og_ref, meta_ref = (refs.pop(0), refs.pop(0)) if combine else (None, None)
    g_ref = refs.pop(0)
    w_ref = refs.pop(0)
    b_ref = refs.pop(0) if has_bias else None
    out_refs = refs
    x = xa_ref[...]
    if two_src:
        x = jnp.where(pl.program_id(0) < n_prompt_blocks, x, xb_ref[...])
    if combine:
        x = _add_expert_outputs(x, og_ref, meta_ref)
        out_refs.pop(0)[...] = x
    ms = jnp.mean(x * x, axis=-1, keepdims=True)
    xn = (x * lax.rsqrt(ms + NORM_EPS) * g_ref[...]).astype(BF16)
    is_prompt = pl.program_id(0) < n_prompt_blocks
    out_refs = list(out_refs)
    targets = []
    for c0, c1, layout in outs:
        targets.append((c0, c1, layout, out_refs.pop(0), out_refs.pop(0) if layout == "split" else None))
    for s in range(0, w_ref.shape[1], COL_CHUNK):
        e = s + COL_CHUNK
        r = jnp.dot(xn, w_ref[:, s:e], preferred_element_type=F32)
        if has_bias:
            r = r + b_ref[:, s:e]
        for c0, c1, layout, o_ref, o2_ref in targets:
            if not (c0 <= s and e <= c1):
                continue
            rr = r.astype(o_ref.dtype)
            if layout == "flat":
                o_ref[:, s - c0:e - c0] = rr
            elif layout == "split":

                @pl.when(is_prompt)
                def _(rr=rr, o_ref=o_ref, s=s, e=e, c0=c0):
                    o_ref[:, s - c0:e - c0] = rr

                @pl.when(jnp.logical_not(is_prompt))
                def _(rr=rr, o2_ref=o2_ref, s=s, e=e, c0=c0):
                    o2_ref[:, s - c0:e - c0] = rr
            elif layout == "heads":
                for hh in range((e - s) // LANES):
                    o_ref[(s - c0) // LANES + hh] = rr[:, hh * LANES:(hh + 1) * LANES]
            else:
                for hh in range((e - s) // LANES):
                    o_ref[(s - c0) // LANES + hh] = r[:, hh * LANES:(hh + 1) * LANES].T.astype(o_ref.dtype)


def norm_proj(xa, xb, g, w, b, outs, tr, n_prompt, og=None, meta=None):
    na, d = xa.shape
    nb = 0 if xb is None else xb.shape[0]
    n = na + nb
    npb = n_prompt // tr
    two = xb is not None
    combine = og is not None
    in_specs = [pl.BlockSpec((tr, d), (lambda i: (jnp.minimum(i, npb - 1), 0)) if two else (lambda i: (i, 0)))]
    args = [xa]
    if two:
        in_specs.append(pl.BlockSpec((tr, d), lambda i: (jnp.maximum(i - npb, 0), 0)))
        args.append(xb)
    if combine:
        in_specs += [pl.BlockSpec((tr, og.shape[1]), lambda i: (i, 0)), pl.BlockSpec((tr, LANES), lambda i: (i, 0))]
        args += [og, meta]
    in_specs.append(pl.BlockSpec((1, d), lambda i: (0, 0)))
    args.append(g.reshape(1, d).astype(F32))
    in_specs.append(pl.BlockSpec(w.shape, lambda i: (0, 0)))
    args.append(w)
    if b is not None:
        in_specs.append(pl.BlockSpec((1, w.shape[1]), lambda i: (0, 0)))
        args.append(b.reshape(1, -1).astype(F32))
    out_shapes, out_specs = [], []
    if combine:
        out_shapes.append(jax.ShapeDtypeStruct((n, d), F32))
        out_specs.append(pl.BlockSpec((tr, d), lambda i: (i, 0)))
    for c0, c1, layout, dt in outs:
        assert c0 % COL_CHUNK == 0 and c1 % COL_CHUNK == 0
        if layout == "flat":
            out_shapes.append(jax.ShapeDtypeStruct((n, c1 - c0), dt))
            out_specs.append(pl.BlockSpec((tr, c1 - c0), lambda i: (i, 0)))
        elif layout == "split":
            out_shapes.append(jax.ShapeDtypeStruct((n_prompt, c1 - c0), dt))
            out_specs.append(pl.BlockSpec((tr, c1 - c0), lambda i: (jnp.minimum(i, npb - 1), 0)))
            out_shapes.append(jax.ShapeDtypeStruct((n - n_prompt, c1 - c0), dt))
            out_specs.append(pl.BlockSpec((tr, c1 - c0), lambda i: (jnp.maximum(i - npb, 0), 0)))
        elif layout == "heads":
            nh = (c1 - c0) // LANES
            out_shapes.append(jax.ShapeDtypeStruct((nh, n, LANES), dt))
            out_specs.append(pl.BlockSpec((nh, tr, LANES), lambda i: (0, i, 0)))
        else:
            assert layout == "heads_t"
            nh = (c1 - c0) // LANES
            out_shapes.append(jax.ShapeDtypeStruct((nh, LANES, n), dt))
            out_specs.append(pl.BlockSpec((nh, LANES, tr), lambda i: (0, 0, i)))
    kern = functools.partial(_norm_proj_kernel, npb, two, b is not None, combine,
                             [(c0, c1, lay) for c0, c1, lay, _ in outs])
    return pl.pallas_call(
        kern, grid=(n // tr,), in_specs=in_specs, out_specs=out_specs, out_shape=out_shapes,
        compiler_params=_cparams(("arbitrary",)), name="norm_proj")(*args)


def _attn_a_chunk(q, k, v, bias_ref, sink_ref, valid):
    nq = q.shape[0]
    outs = []
    for kh in range(A_KV_HEADS):
        qs = jnp.concatenate(
            [q[:, (kh * A_GROUP + g) * A_HD:(kh * A_GROUP + g + 1) * A_HD] for g in range(A_GROUP)], axis=0)
        kk = k[:, kh * A_HD:(kh + 1) * A_HD]
        s = lax.dot_general(qs, kk, (((1,), (1,)), ((), ())), preferred_element_type=F32) + bias_ref[kh]
        if valid is not None:
            s = jnp.where(valid, s, NEG_INF)
        sk = sink_ref[kh]
        m = jnp.maximum(jnp.max(s, axis=-1, keepdims=True), sk)
        e = jnp.exp2(s - m)
        den = jnp.sum(e, axis=-1, keepdims=True) + jnp.exp2(sk - m)
        p = (e * (1.0 / den)).astype(BF16)
        o = jnp.dot(p, v[:, kh * A_HD:(kh + 1) * A_HD], preferred_element_type=F32)
        for g in range(A_GROUP):
            outs.append(o[g * nq:(g + 1) * nq])
    return jnp.concatenate(outs, axis=1)


def _attn_a_prompt_kernel(qb, q_ref, kvc_ref, kvp_ref, bias_ref, sink_ref, o_ref, t_ref, p_ref):
    cb = pl.program_id(1)
    back = CHUNK * WIN_CHUNKS
    nk = back + qb
    nkv = A_KV_HEADS * A_HD
    kvfull = jnp.concatenate([kvp_ref[...], kvc_ref[...]], axis=0)
    k_all = kvfull[:, :nkv].astype(BF16)
    vt_all = kvfull[:, nkv:].T.astype(BF16)
    key_pos = lax.broadcasted_iota(jnp.int32, (nk, 1), 0) + (cb * qb - back)
    before_start = jnp.where(key_pos >= 0, 0.0, NEG_INF)
    dn = (((1,), (1,)), ((), ()))
    for kh in range(A_KV_HEADS):
        qs = jnp.concatenate(
            [q_ref[:, (kh * A_GROUP + g) * A_HD:(kh * A_GROUP + g + 1) * A_HD] for g in range(A_GROUP)], axis=0)
        t_ref[kh] = lax.dot_general(k_all[:, kh * A_HD:(kh + 1) * A_HD], qs, dn, preferred_element_type=F32)
    inv_dens = []
    for kh in range(A_KV_HEADS):
        st = t_ref[kh] + bias_ref[kh] + before_start
        sk = sink_ref[kh]
        m = jnp.maximum(jnp.max(st, axis=0, keepdims=True), sk)
        e = jnp.exp2(st - m)
        inv_dens.append(1.0 / (jnp.sum(e, axis=0, keepdims=True) + jnp.exp2(sk - m)))
        p_ref[kh] = e.astype(BF16)
    outs = []
    for kh in range(A_KV_HEADS):
        ot = jnp.dot(vt_all[kh * A_HD:(kh + 1) * A_HD, :], p_ref[kh], preferred_element_type=F32)
        o = (ot * inv_dens[kh]).T
        for g in range(A_GROUP):
            outs.append(o[g * qb:(g + 1) * qb])
    o_ref[...] = jnp.concatenate(outs, axis=1).astype(o_ref.dtype)


def _a_prompt_tables(qb, sinks):
    back = CHUNK * WIN_CHUNKS
    slopes = _alibi(A_HEADS).reshape(A_KV_HEADS, A_GROUP)
    qpos = np.arange(qb)
    kpos = np.arange(back + qb) - back
    dist = np.abs(kpos[:, None] - qpos[None, :]).astype(np.float64)
    band_lo = (qpos // CHUNK) * CHUNK - back
    in_band = (kpos[:, None] >= band_lo[None, :]) & (kpos[:, None] < band_lo[None, :] + back + CHUNK)
    bias = np.where(in_band[None, None], -slopes[:, :, None, None] * dist[None, None], -np.inf)
    bias = np.transpose(bias, (0, 2, 1, 3)).reshape(A_KV_HEADS, back + qb, A_GROUP * qb) * LOG2E
    sk = jnp.repeat(sinks.astype(F32).reshape(A_KV_HEADS, A_GROUP), qb, axis=1)[:, None, :] * LOG2E
    return jnp.asarray(bias, F32), sk


def _a_tables(qpos, kpos, sinks):
    slopes = _alibi(A_HEADS).reshape(A_KV_HEADS, A_GROUP)
    dist = np.abs(qpos[:, None] - kpos[None, :]).astype(np.float64)
    bias = -slopes[:, :, None, None] * dist
    nq = len(qpos)
    bias = jnp.asarray(bias.reshape(A_KV_HEADS, A_GROUP * nq, len(kpos)) * LOG2E, F32)
    sk = jnp.repeat(sinks.astype(F32).reshape(A_KV_HEADS, A_GROUP), nq, axis=1)[..., None] * LOG2E
    return bias, sk


def attn_a_prompt(q_all, kv_all, sinks, batch, t, qb):
    nq = A_HEADS * A_HD
    nkv2 = 2 * A_KV_HEADS * A_HD
    back = CHUNK * WIN_CHUNKS
    bias, sk = _a_prompt_tables(qb, sinks)
    nblk = t // qb
    r = qb // back
    return pl.pallas_call(
        functools.partial(_attn_a_prompt_kernel, qb),
        grid=(batch, nblk),
        in_specs=[
            pl.BlockSpec((qb, nq), lambda b, i: (b * nblk + i, 0)),
            pl.BlockSpec((qb, nkv2), lambda b, i: (b * nblk + i, 0)),
            pl.BlockSpec((back, nkv2), lambda b, i: (jnp.maximum((b * nblk + i) * r - 1, 0), 0)),
            pl.BlockSpec(bias.shape, lambda b, i: (0, 0, 0)),
            pl.BlockSpec(sk.shape, lambda b, i: (0, 0, 0)),
        ],
        out_specs=pl.BlockSpec((qb, nq), lambda b, i: (b * nblk + i, 0)),
        out_shape=jax.ShapeDtypeStruct((batch * t, nq), BF16),
        scratch_shapes=[pltpu.VMEM((A_KV_HEADS, back + qb, A_GROUP * qb), F32),
                        pltpu.VMEM((A_KV_HEADS, back + qb, A_GROUP * qb), BF16)],
        compiler_params=_cparams(("arbitrary", "arbitrary")), name="attn_a_prompt",
    )(q_all, kv_all, kv_all, bias, sk)


def _attn_a_sample_kernel(q_ref, k_ref, v_ref, bias_ref, sink_ref, o_ref):
    o = _attn_a_chunk(q_ref[...], k_ref[...].astype(BF16), v_ref[...].astype(BF16), bias_ref, sink_ref, None)
    o_ref[...] = o.astype(o_ref.dtype)


def attn_a_sample(q_all, kband, vband, sinks, row0, dec_batch, dec_t):
    nq = A_HEADS * A_HD
    s = kband.shape[1]
    past = s - dec_t
    kpos = np.concatenate([np.arange(past) - past, np.arange(dec_t)])
    bias, sk = _a_tables(np.arange(dec_t), kpos, sinks)
    blk0 = row0 // dec_t
    return pl.pallas_call(
        _attn_a_sample_kernel,
        grid=(dec_batch,),
        in_specs=[
            pl.BlockSpec((dec_t, nq), lambda b: (blk0 + b, 0)),
            pl.BlockSpec((None, s, kband.shape[2]), lambda b: (b, 0, 0)),
            pl.BlockSpec((None, s, vband.shape[2]), lambda b: (b, 0, 0)),
            pl.BlockSpec(bias.shape, lambda b: (0, 0, 0)),
            pl.BlockSpec(sk.shape, lambda b: (0, 0, 0)),
        ],
        out_specs=pl.BlockSpec((dec_t, nq), lambda b: (b, 0)),
        out_shape=jax.ShapeDtypeStruct((dec_batch * dec_t, nq), BF16),
        compiler_params=_cparams(("arbitrary",)), name="attn_a_sample",
    )(q_all, kband, vband, bias, sk)


B_Q_SCALE = B_HD ** -0.5 * float(np.log2(np.e))
LOG2E = float(np.log2(np.e))


def _stack_maps(qh):
    lane = lax.broadcasted_iota(jnp.int32, qh.shape, 1)
    zero = jnp.zeros_like(qh)
    return jnp.concatenate([jnp.where(lane < B_HD, qh, zero), jnp.where(lane >= B_HD, qh, zero)], axis=0)


def _diff_finish(acc, l, nq, lam, lam_init, subln):
    inv = 1.0 / l
    o = acc[:nq] * inv[:nq] - lam * (acc[nq:] * inv[nq:])
    ms = jnp.mean(o * o, axis=-1, keepdims=True)
    return o * lax.rsqrt(ms + SUBLN_EPS) * subln * (1.0 - lam_init)


def _attn_b_prompt_kernel(qb, lam_init, slope_ref, lam_ref, q_ref, k_ref, vt_ref, kbias_ref,
                          subln_ref, o_ref, acc_ref, t0_ref, t1_ref, p0_ref, p1_ref):
    i = pl.program_id(1)
    lam = lam_ref[0]
    dn = (((1,), (1,)), ((), ()))
    t_refs = (t0_ref, t1_ref)
    p_refs = (p0_ref, p1_ref)

    def group_body(hg, carry):
        heads = [hg * HEAD_GROUP + u for u in range(HEAD_GROUP)]
        slopes = [slope_ref[h] for h in heads]
        qqs = [_stack_maps(q_ref[h]) for h in heads]

        def scores(j, slot):
            start = pl.multiple_of(j * qb, qb)
            for u, h in enumerate(heads):
                t_refs[slot][u] = lax.dot_general(k_ref[h, pl.ds(start, qb), :], qqs[u], dn,
                                                  preferred_element_type=F32)

        def softmax_step(j, slot, c):
            out = []
            on_diag = (j == i).astype(jnp.int32)
            for u, h in enumerate(heads):
                m, l, _ = c[u]
                t = t_refs[slot][u] + kbias_ref[h, on_diag]
                off = slopes[u] * jnp.full((1, 2 * qb), (j - i) * qb, jnp.int32).astype(F32)
                m_new = jnp.maximum(m, jnp.max(t, axis=0, keepdims=True) + off)
                alpha = jnp.exp2(m - m_new)
                p = jnp.exp2(t + (off - m_new))
                p_refs[slot][u] = p.astype(BF16)
                out.append((m_new, alpha * l + jnp.sum(p, axis=0, keepdims=True), alpha))
            return tuple(out)

        def accumulate(j, slot, c):
            start = pl.multiple_of(jnp.maximum(j, 0) * qb, qb)
            for u, h in enumerate(heads):
                pv = jnp.dot(vt_ref[h, :, pl.ds(start, qb)], p_refs[slot][u], preferred_element_type=F32)
                acc_ref[u] = c[u][2] * acc_ref[u] + pv

        def step(j, c, slot):
            scores(jnp.minimum(j + 1, i), 1 - slot)
            c_new = softmax_step(j, slot, c)
            accumulate(j - 1, 1 - slot, c)
            return c_new

        def drain(c, slot):
            accumulate(i, slot, c)
            return 0

        acc_ref[...] = jnp.zeros_like(acc_ref)
        p1_ref[...] = jnp.zeros_like(p1_ref)
        scores(0, 0)
        init = tuple((jnp.full((1, 2 * qb), NEG_INF, F32), jnp.zeros((1, 2 * qb), F32), jnp.ones((1, 2 * qb), F32))
                     for _ in heads)
        c = lax.fori_loop(
            0, i + 1, lambda j, c: lax.cond(j % 2 == 0, lambda: step(j, c, 0), lambda: step(j, c, 1)), init)
        lax.cond(i % 2 == 0, lambda: drain(c, 0), lambda: drain(c, 1))
        for u, h in enumerate(heads):
            inv = 1.0 / c[u][1]
            acc = acc_ref[u]
            ot = acc[:, :qb] * inv[:, :qb] - lam * (acc[:, qb:] * inv[:, qb:])
            ms = jnp.mean(ot * ot, axis=0, keepdims=True)
            ot = ot * lax.rsqrt(ms + SUBLN_EPS) * subln_ref[...] * (1.0 - lam_init)
            o_ref[h] = ot.T.astype(o_ref.dtype)
        return carry

    lax.fori_loop(0, B_HEADS // HEAD_GROUP, group_body, 0)


def attn_b_prompt(qkvh, vt, lam, subln, lam_init, batch, t, qb):
    nblk = t // qb
    qq = np.arange(qb)
    slope2 = (_alibi(B_HEADS) * LOG2E)[:, None, None]
    kcol = np.broadcast_to(qq[:, None], (qb, qb)).astype(np.float64)
    tri = -2.0 * np.maximum(qq[:, None] - qq[None, :], 0)
    msk = np.where((qq[:, None] // CHUNK) <= (qq[None, :] // CHUNK), 0.0, -np.inf)
    both = lambda a: np.concatenate([a, a], axis=-1)
    kbias = np.stack([both(slope2 * kcol[None]), both(slope2 * (kcol + tri)[None] + msk[None])], axis=1)
    kbias = jnp.asarray(kbias, F32)
    slopes = jnp.asarray(_alibi(B_HEADS) * LOG2E, F32)
    subln_t = jnp.broadcast_to(subln.astype(F32).reshape(LANES, 1), (LANES, qb))
    const = lambda b, i, *_: (0, 0)
    grid_spec = pltpu.PrefetchScalarGridSpec(
        num_scalar_prefetch=2, grid=(batch, nblk),
        in_specs=[
            pl.BlockSpec((B_HEADS, qb, LANES), lambda b, i, *_: (0, b * nblk + i, 0)),
            pl.BlockSpec((B_HEADS, t, LANES), lambda b, i, *_: (1, b, 0)),
            pl.BlockSpec((B_HEADS, LANES, t), lambda b, i, *_: (0, 0, b)),
            pl.BlockSpec(kbias.shape, lambda b, i, *_: (0, 0, 0, 0)),
            pl.BlockSpec(subln_t.shape, const),
        ],
        out_specs=pl.BlockSpec((B_HEADS, qb, LANES), lambda b, i, *_: (0, b * nblk + i, 0)),
        scratch_shapes=[pltpu.VMEM((HEAD_GROUP, LANES, 2 * qb), F32),
                        pltpu.VMEM((HEAD_GROUP, qb, 2 * qb), F32), pltpu.VMEM((HEAD_GROUP, qb, 2 * qb), F32),
                        pltpu.VMEM((HEAD_GROUP, qb, 2 * qb), BF16), pltpu.VMEM((HEAD_GROUP, qb, 2 * qb), BF16)],
    )
    return pl.pallas_call(
        functools.partial(_attn_b_prompt_kernel, qb, lam_init), grid_spec=grid_spec,
        out_shape=jax.ShapeDtypeStruct((B_HEADS, batch * t, LANES), BF16),
        compiler_params=_cparams(("arbitrary", "arbitrary")), name="attn_b_prompt",
    )(slopes, lam.reshape(1).astype(F32), qkvh, qkvh, vt, kbias, subln_t)


def _attn_b_sample_kernel(dec_t, past, lam_init, slopes, lam_ref, q_ref, kn_ref, vn_ref, ck_ref, cv_ref,
                          tnew_ref, subln_ref, o_ref):
    lam = lam_ref[0]
    subln = subln_ref[...]
    colc = lax.broadcasted_iota(jnp.int32, (1, past), 1).astype(F32) - float(past)
    for h in range(B_HEADS):
        slope = float(slopes[h]) * LOG2E
        qq = _stack_maps(q_ref[h])
        kc = ck_ref[pl.ds(h, past, stride=B_HEADS), :].astype(BF16)
        vc = cv_ref[pl.ds(h, past, stride=B_HEADS), :].astype(BF16)
        dn = (((1,), (1,)), ((), ()))
        s_c = lax.dot_general(qq, kc, dn, preferred_element_type=F32) + slope * colc
        s_n = lax.dot_general(qq, kn_ref[h], dn, preferred_element_type=F32) + slope * tnew_ref[...]
        m = jnp.maximum(jnp.max(s_c, axis=-1, keepdims=True), jnp.max(s_n, axis=-1, keepdims=True))
        p_c = jnp.exp2(s_c - m)
        p_n = jnp.exp2(s_n - m)
        l = jnp.sum(p_c, axis=-1, keepdims=True) + jnp.sum(p_n, axis=-1, keepdims=True)
        acc = (jnp.dot(p_c.astype(BF16), vc, preferred_element_type=F32)
               + jnp.dot(p_n.astype(BF16), vn_ref[h], preferred_element_type=F32))
        o_ref[h] = _diff_finish(acc, l, dec_t, lam, lam_init, subln).astype(o_ref.dtype)


def attn_b_sample(qkvh, cache_k, cache_v, lam, subln, lam_init, row0, dec_batch, dec_t):
    past = cache_k.shape[1]
    qi = np.arange(dec_t)
    tnew = (qi[:, None] - np.abs(qi[:, None] - qi[None, :])).astype(np.float32)
    tnew = jnp.asarray(np.concatenate([tnew, tnew], axis=0))
    blk0 = row0 // dec_t
    rows = past * B_HEADS
    cache_k = cache_k.reshape(dec_batch, rows, LANES)
    cache_v = cache_v.reshape(dec_batch, rows, LANES)
    grid_spec = pltpu.PrefetchScalarGridSpec(
        num_scalar_prefetch=1, grid=(dec_batch,),
        in_specs=[
            pl.BlockSpec((B_HEADS, dec_t, LANES), lambda b, *_: (0, blk0 + b, 0)),
            pl.BlockSpec((B_HEADS, dec_t, LANES), lambda b, *_: (1, blk0 + b, 0)),
            pl.BlockSpec((B_HEADS, dec_t, LANES), lambda b, *_: (2, blk0 + b, 0)),
            pl.BlockSpec((None, rows, LANES), lambda b, *_: (b, 0, 0)),
            pl.BlockSpec((None, rows, LANES), lambda b, *_: (b, 0, 0)),
            pl.BlockSpec(tnew.shape, lambda b, *_: (0, 0)),
            pl.BlockSpec((1, LANES), lambda b, *_: (0, 0)),
        ],
        out_specs=pl.BlockSpec((B_HEADS, dec_t, LANES), lambda b, *_: (0, b, 0)),
    )
    return pl.pallas_call(
        functools.partial(_attn_b_sample_kernel, dec_t, past, lam_init, tuple(_alibi(B_HEADS))),
        grid_spec=grid_spec,
        out_shape=jax.ShapeDtypeStruct((B_HEADS, dec_batch * dec_t, LANES), BF16),
        compiler_params=_cparams(("arbitrary",)), name="attn_b_sample",
    )(lam.reshape(1).astype(F32), qkvh, qkvh, qkvh, cache_k, cache_v, tnew, subln.reshape(1, LANES).astype(F32))


def _route(logits):
    lane = lax.broadcasted_iota(jnp.int32, logits.shape, 1).astype(F32)
    big = float(LANES)
    lg = jnp.where(lane < N_GROUPS, logits, NEG_INF)
    gmax = jnp.max(lg, axis=-1, keepdims=True)
    g_idx = jnp.min(jnp.where(lg == gmax, lane, big), axis=-1, keepdims=True)
    g_prob = 1.0 / jnp.sum(jnp.exp(lg - gmax), axis=-1, keepdims=True)
    lo = N_GROUPS + EXPERTS_PER_GROUP * g_idx
    le = jnp.where((lane >= lo) & (lane < lo + EXPERTS_PER_GROUP), logits, NEG_INF)
    m1 = jnp.max(le, axis=-1, keepdims=True)
    i1 = jnp.min(jnp.where(le == m1, lane, big), axis=-1, keepdims=True)
    le2 = jnp.where(lane == i1, NEG_INF, le)
    m2 = jnp.max(le2, axis=-1, keepdims=True)
    i2 = jnp.min(jnp.where(le2 == m2, lane, big), axis=-1, keepdims=True)
    t = jnp.exp(m2 - m1)
    ga = 1.0 / (1.0 + t)
    gb = t * ga
    return lane, i1, i2, g_prob * ga, g_prob * gb


def _unpack_pairs(u):
    lo = pltpu.bitcast(u << 16, F32)
    hi = pltpu.bitcast(u & jnp.uint32(0xFFFF0000), F32)
    return jnp.concatenate([lo, hi], axis=1)


def _pack_pairs(x):
    bits = pltpu.bitcast(x.astype(BF16).astype(F32), jnp.uint32)
    w = x.shape[1] // 2
    return (bits[:, :w] >> 16) | bits[:, w:]


def _out_router_kernel(n_prompt_blocks, heads_in, two_o, two_res, *refs):
    refs = list(refs)
    oa_ref = refs.pop(0)
    ob_ref = refs.pop(0) if two_o else None
    ra_ref = refs.pop(0)
    rb_ref = refs.pop(0) if two_res else None
    wo_ref, g_ref, wr_ref, wr2_ref, br_ref, ltri_ref, h_ref, xn_ref, meta_ref, cnt_ref, carry_ref = refs
    is_prompt = pl.program_id(0) < n_prompt_blocks

    def load_o(ref):
        if heads_in:
            return jnp.concatenate([ref[hh] for hh in range(ref.shape[0])], axis=1)
        return ref[...]

    o = load_o(oa_ref)
    if two_o:
        o = jnp.where(is_prompt, o, load_o(ob_ref))
    res = ra_ref[...]
    if two_res:
        res = jnp.where(is_prompt, res, rb_ref[...])
    h = res + jnp.dot(o, wo_ref[...], preferred_element_type=F32)
    h_ref[...] = h
    ms = jnp.mean(h * h, axis=-1, keepdims=True)
    xn = h * lax.rsqrt(ms + NORM_EPS) * g_ref[...]
    x_hi = xn.astype(BF16)
    _store_planes(xn_ref, _pack_pairs(xn))
    x_lo = (xn - x_hi.astype(F32)).astype(BF16)
    r = jnp.dot(x_hi, wr_ref[...], preferred_element_type=F32)
    logits = (r[:, :LANES] + r[:, LANES:]) + jnp.dot(x_lo, wr2_ref[...], preferred_element_type=F32) + br_ref[...]
    lane, i1, i2, g1, g2 = _route(logits)

    @pl.when(pl.program_id(0) == 0)
    def _():
        carry_ref[...] = jnp.zeros_like(carry_ref)

    sel1 = lane == i1
    sel2 = lane == i2
    onehot = jnp.where(sel1 | sel2, 1.0, 0.0)
    before = jnp.dot(ltri_ref[...], onehot.astype(BF16), preferred_element_type=F32) + carry_ref[...]
    rank1 = jnp.sum(jnp.where(sel1, before, 0.0), axis=-1, keepdims=True)
    rank2 = jnp.sum(jnp.where(sel2, before, 0.0), axis=-1, keepdims=True)
    carry_ref[...] = carry_ref[...] + jnp.sum(onehot, axis=0, keepdims=True)
    cnt_ref[...] = carry_ref[...]
    cols = [i1 - N_GROUPS, i2 - N_GROUPS, g1, g2, rank1, rank2]
    meta = jnp.zeros_like(logits)
    for c, v in enumerate(cols):
        meta = jnp.where(lane == c, v, meta)
    meta_ref[...] = meta


def out_router(oa, ob, ra, rb, wo, g, w_rg, b_rg, w_re, b_re, heads_in, tr):
    d = ra.shape[1]
    if heads_in:
        na = oa.shape[1]
        nb = 0 if ob is None else ob.shape[1]
    else:
        na = oa.shape[0]
        nb = 0 if ob is None else ob.shape[0]
    n = na + nb
    npb = na // tr
    two_o = ob is not None
    two_res = rb is not None
    first = lambda i: (jnp.minimum(i, npb - 1), 0)
    second = lambda i: (jnp.maximum(i - npb, 0), 0)
    plain = lambda i: (i, 0)
    in_specs, args = [], []

    def add_o(x, imap):
        if heads_in:
            in_specs.append(pl.BlockSpec((x.shape[0], tr, LANES), lambda i: (0, imap(i)[0], 0)))
        else:
            in_specs.append(pl.BlockSpec((tr, x.shape[1]), imap))
        args.append(x)

    add_o(oa, first if two_o else plain)
    if two_o:
        add_o(ob, second)
    in_specs.append(pl.BlockSpec((tr, d), first if two_res else plain))
    args.append(ra)
    if two_res:
        in_specs.append(pl.BlockSpec((tr, d), second))
        args.append(rb)
    wr = jnp.zeros((d, LANES), F32).at[:, :N_GROUPS].set(w_rg.astype(F32))
    wr = wr.at[:, N_GROUPS:N_GROUPS + N_EXPERTS].set(w_re.astype(F32))
    wr_hi = wr.astype(BF16)
    wr_lo = (wr - wr_hi.astype(F32)).astype(BF16)
    br = jnp.zeros((1, LANES), F32).at[0, :N_GROUPS].set(b_rg.astype(F32))
    br = br.at[0, N_GROUPS:N_GROUPS + N_EXPERTS].set(b_re.astype(F32))
    ltri = jnp.asarray(np.tril(np.ones((tr, tr), np.float32), -1), BF16)
    consts = [wo, g.reshape(1, d).astype(F32), jnp.concatenate([wr_hi, wr_lo], axis=1), wr_hi, br, ltri]
    for cst in consts:
        in_specs.append(pl.BlockSpec(cst.shape, lambda i: (0, 0)))
        args.append(cst)
    parts = d // 2 // SC_ROW_WORDS
    out_shape = [jax.ShapeDtypeStruct((n, d), F32), jax.ShapeDtypeStruct((parts, n, SC_ROW_WORDS), jnp.uint32),
                 jax.ShapeDtypeStruct((n, LANES), F32), jax.ShapeDtypeStruct((1, LANES), F32)]
    out_specs = [pl.BlockSpec((tr, d), plain), pl.BlockSpec((parts, tr, SC_ROW_WORDS), lambda i: (0, i, 0)),
                 pl.BlockSpec((tr, LANES), plain), pl.BlockSpec((1, LANES), lambda i: (0, 0))]
    return pl.pallas_call(
        functools.partial(_out_router_kernel, npb, heads_in, two_o, two_res),
        grid=(n // tr,), in_specs=in_specs, out_specs=out_specs, out_shape=out_shape,
        scratch_shapes=[pltpu.VMEM((1, LANES), F32)],
        compiler_params=_cparams(("arbitrary",)), name="out_router")(*args)


def _expert_kernel(te_ref, nt_ref, x_ref, wg_ref, wu_ref, wd_ref, o_ref):
    @pl.when(pl.program_id(0) < nt_ref[0])
    def _():
        x = _unpack_pairs(_join_planes(x_ref, 0, x_ref.shape[0])).astype(BF16)
        a = jnp.dot(x, wg_ref[...].astype(BF16), preferred_element_type=F32)
        u = jnp.dot(x, wu_ref[...].astype(BF16), preferred_element_type=F32)
        hh = (a * (1.0 / (1.0 + jnp.exp(-a))) * u).astype(BF16)
        _store_planes(o_ref, _pack_pairs(jnp.dot(hh, wd_ref[...].astype(BF16), preferred_element_type=F32)))

    @pl.when(pl.program_id(0) >= nt_ref[0])
    def _():
        o_ref[...] = jnp.zeros_like(o_ref)


def expert_mlp(xs, tile_expert, n_tiles, w_gate, w_up, w_down, layer, tm):
    parts, p, w = xs.shape
    d, f = w_gate.shape[-2:]
    epg = w_gate.shape[2]
    wmap = lambda t, te, nt: (layer, te[t] // epg, te[t] % epg, 0, 0)
    grid_spec = pltpu.PrefetchScalarGridSpec(
        num_scalar_prefetch=2, grid=(p // tm,),
        in_specs=[
            pl.BlockSpec((parts, tm, w), lambda t, te, nt: (0, t, 0)),
            pl.BlockSpec((None, None, None, d, f), wmap),
            pl.BlockSpec((None, None, None, d, f), wmap),
            pl.BlockSpec((None, None, None, f, d), wmap),
        ],
        out_specs=pl.BlockSpec((parts, tm, w), lambda t, te, nt: (0, t, 0)),
    )
    return pl.pallas_call(
        _expert_kernel, grid_spec=grid_spec, out_shape=jax.ShapeDtypeStruct((parts, p, w), jnp.uint32),
        compiler_params=_cparams(("arbitrary",)), name="expert_mlp")(tile_expert, n_tiles, xs, w_gate, w_up, w_down)


def sc_scatter_rows(x2, idx, n_out):
    mesh = plsc.VectorSubcoreMesh(core_axis_name="c", subcore_axis_name="s")
    n_src_win = x2.shape[0] // SC_WINDOW

    @functools.partial(pl.kernel, out_type=jax.ShapeDtypeStruct((n_out, SC_ROW_WORDS), x2.dtype), mesh=mesh)
    def scatter(x_hbm, i_hbm, o_hbm):
        def body(x_vmem, i_vmem):
            pltpu.sync_copy(x_vmem, o_hbm.at[i_vmem.at[0]])

        pltpu.emit_pipeline(
            body, grid=(n_out // SC_WINDOW,),
            in_specs=[pl.BlockSpec((SC_WINDOW, SC_ROW_WORDS), lambda i: (lax.rem(i, n_src_win), 0)),
                      pl.BlockSpec((1, SC_WINDOW), lambda i: (0, i))],
            out_specs=[], core_axis_name=("c", "s"), dimension_semantics=(pltpu.PARALLEL,),
        )(x_hbm, i_hbm)

    return scatter(x2, idx)


def sc_gather_rows(x2, idx):
    mesh = plsc.VectorSubcoreMesh(core_axis_name="c", subcore_axis_name="s")
    n_col, n = idx.shape

    @functools.partial(pl.kernel, out_type=jax.ShapeDtypeStruct((n, n_col * SC_ROW_WORDS), x2.dtype), mesh=mesh)
    def gather(x_hbm, i_hbm, o_hbm):
        def body(i_vmem, o_vmem):
            pltpu.sync_copy(x_hbm.at[i_vmem.at[0]], o_vmem)

        pltpu.emit_pipeline(
            body, grid=(n // SC_WINDOW, n_col),
            in_specs=[pl.BlockSpec((1, SC_WINDOW), lambda i, c: (c, i))],
            out_specs=[pl.BlockSpec((SC_WINDOW, SC_ROW_WORDS), lambda i, c: (i, c))],
            core_axis_name=("c", "s"), dimension_semantics=(pltpu.PARALLEL, pltpu.ARBITRARY),
        )(i_hbm, o_hbm)

    return gather(x2, idx)


def moe_dispatch(xn_packed, meta, counts, tm):
    parts, n, w = xn_packed.shape
    e = meta[:, 0:2].astype(jnp.int32)
    rank = meta[:, 4:6].astype(jnp.int32)
    counts = counts[0, N_GROUPS:N_GROUPS + N_EXPERTS].astype(jnp.int32)
    padded = ((counts + tm - 1) // tm) * tm
    pend = jnp.cumsum(padded)
    pstart = pend - padded
    ids = jnp.arange(N_EXPERTS, dtype=jnp.int32)
    pos = jnp.sum(jnp.where(e[..., None] == ids, pstart, 0), axis=-1) + rank
    p = 2 * n + N_EXPERTS * tm
    n_pad = p - 2 * n
    gap_len = jnp.concatenate([padded - counts, (p - pend[-1]).reshape(1)])
    gap_first = jnp.concatenate([pstart + counts, pend[-1:]])
    gap_end = jnp.cumsum(gap_len)
    k = jnp.arange(n_pad, dtype=jnp.int32)
    gap = jnp.sum((k[:, None] >= gap_end[None, :]).astype(jnp.int32), axis=1)
    shift = gap_first - (gap_end - gap_len)
    pad_pos = k + jnp.sum(jnp.where(gap[:, None] == jnp.arange(N_EXPERTS + 1), shift, 0), axis=-1)
    planes = lambda rows: jnp.concatenate([rows + q * p for q in range(parts)])
    scat_idx = jnp.concatenate([planes(pos[:, 0]), planes(pos[:, 1]), planes(pad_pos)]).reshape(1, parts * p)
    gath_idx = jnp.stack([pos[:, k] + q * p for k in range(2) for q in range(parts)])
    xs = sc_scatter_rows(xn_packed.reshape(parts * n, w), scat_idx, parts * p).reshape(parts, p, w)
    tile_start = jnp.arange(p // tm, dtype=jnp.int32) * tm
    tile_expert = jnp.sum((tile_start[:, None] >= pend[None, :]).astype(jnp.int32), axis=1)
    tile_expert = jnp.minimum(tile_expert, N_EXPERTS - 1)
    n_tiles = (pend[-1] // tm).astype(jnp.int32).reshape(1)
    return xs, tile_expert, n_tiles, gath_idx


def moe_experts(xn_packed, meta, counts, w_gate, w_up, w_down, layer, tm):
    parts, n, w = xn_packed.shape
    xs, tile_expert, n_tiles, gath_idx = moe_dispatch(xn_packed, meta, counts, tm)
    o = expert_mlp(xs, tile_expert, n_tiles, w_gate, w_up, w_down, layer, tm)
    return sc_gather_rows(o.reshape(parts * o.shape[1], w), gath_idx)


def _final_norm_kernel(n_prompt_blocks, h_ref, og_ref, meta_ref, g_ref, yp_ref, ys_ref):
    x = _add_expert_outputs(h_ref[...], og_ref, meta_ref)
    ms = jnp.mean(x * x, axis=-1, keepdims=True)
    y = x * lax.rsqrt(ms + NORM_EPS) * g_ref[...]
    i = pl.program_id(0)

    @pl.when(i < n_prompt_blocks)
    def _():
        yp_ref[...] = y

    @pl.when(i >= n_prompt_blocks)
    def _():
        ys_ref[...] = y


def final_norm(h, og, meta, g, n_prompt, tr):
    n, d = h.shape
    npb = n_prompt // tr
    return pl.pallas_call(
        functools.partial(_final_norm_kernel, npb),
        grid=(n // tr,),
        in_specs=[pl.BlockSpec((tr, d), lambda i: (i, 0)), pl.BlockSpec((tr, og.shape[1]), lambda i: (i, 0)),
                  pl.BlockSpec((tr, LANES), lambda i: (i, 0)), pl.BlockSpec((1, d), lambda i: (0, 0))],
        out_specs=[pl.BlockSpec((tr, d), lambda i: (jnp.minimum(i, npb - 1), 0)),
                   pl.BlockSpec((tr, d), lambda i: (jnp.maximum(i - npb, 0), 0))],
        out_shape=[jax.ShapeDtypeStruct((n_prompt, d), F32), jax.ShapeDtypeStruct((n - n_prompt, d), F32)],
        compiler_params=_cparams(("arbitrary",)), name="final_norm")(h, og, meta, g.reshape(1, d).astype(F32))


def _pick_rows_tile(n_prompt, n_sample, largest=512):
    for tr in (1024, 512, 256, 128, 64, 32, 16, 8):
        if tr <= largest and n_prompt % tr == 0 and n_sample % tr == 0:
            return tr
    raise ValueError("row counts must be multiples of 8")


def kernel(x_prompt, x_sample, cache_a_k, cache_a_v, cache_b_k, cache_b_v, norm_mix, norm_ffn, norm_final,
           w_a_qkv, b_a_qkv, a_sinks, w_a_o, w_b_qkv, b_lambda, b_subln, w_b_o,
           w_route_group, b_route_group, w_route_expert, b_route_expert, w_gate, w_up, w_down):
    batch, t, d = x_prompt.shape
    dec_batch, dec_t, _ = x_sample.shape
    n_p, n_s = batch * t, dec_batch * dec_t
    tr = _pick_rows_tile(n_p, n_s)
    tm = min(512, tr * 4)
    xp = x_prompt.reshape(n_p, d)
    xs = x_sample.reshape(n_s, d)
    nq_a = A_HEADS * A_HD
    nkv_a = A_KV_HEADS * A_HD

    a_scale = jnp.where(jnp.arange(nq_a + 2 * nkv_a) < nq_a, A_HD ** -0.5 * LOG2E, 1.0).astype(F32)
    q_a, kv_ap, kv_as = norm_proj(xp, xs, norm_mix[0], (w_a_qkv[0] * a_scale).astype(BF16), b_a_qkv[0] * a_scale,
                                  [(0, nq_a, "flat", BF16), (nq_a, nq_a + 2 * nkv_a, "split", F32)], tr, n_p)
    qb_a = min(256, t)
    o_ap = attn_a_prompt(q_a, kv_ap, a_sinks[0], batch, t, qb_a)
    kv_s = kv_as.reshape(dec_batch, dec_t, 2 * nkv_a)
    past_a = cache_a_k.shape[2]
    kband = jnp.concatenate([cache_a_k[0].reshape(dec_batch, past_a, nkv_a), kv_s[..., :nkv_a]], axis=1)
    vband = jnp.concatenate([cache_a_v[0].reshape(dec_batch, past_a, nkv_a), kv_s[..., nkv_a:]], axis=1)
    o_as = attn_a_sample(q_a, kband, vband, a_sinks[0], n_p, dec_batch, dec_t)
    h, xn, meta, counts = out_router(o_ap, o_as, xp, xs, w_a_o[0].astype(BF16), norm_ffn[0], w_route_group[0],
                                     b_route_group[0], w_route_expert[0], b_route_expert[0], False, tr)
    og = moe_experts(xn, meta, counts, w_gate, w_up, w_down, 0, tm)

    keep = min(CHUNK * WIN_CHUNKS, t)
    kv_p = kv_ap.reshape(batch, t, 2 * nkv_a)[:, t - keep:]
    a_k_prompt = kv_p[..., :nkv_a].reshape(1, batch, keep, A_KV_HEADS, A_HD)
    a_v_prompt = kv_p[..., nkv_a:].reshape(1, batch, keep, A_KV_HEADS, A_HD)
    a_k_sample = kband[:, dec_t:].reshape(1, dec_batch, past_a, A_KV_HEADS, A_HD)
    a_v_sample = vband[:, dec_t:].reshape(1, dec_batch, past_a, A_KV_HEADS, A_HD)

    lam_init = 0.8 - 0.6 * float(np.exp(-0.3 * 1))
    lp = b_lambda[0].astype(F32)
    lam = jnp.exp(jnp.sum(lp[0] * lp[1])) - jnp.exp(jnp.sum(lp[2] * lp[3])) + lam_init
    wb = B_HEADS * 2 * B_HD
    col_scale = jnp.where(jnp.arange(3 * wb) < wb, B_Q_SCALE, 1.0).astype(F32)
    h, qkvh, k_bp, k_bs, v_bp, v_bs, vt = norm_proj(
        h, None, norm_mix[1], (w_b_qkv[0] * col_scale).astype(BF16), None,
        [(0, 3 * wb, "heads", BF16), (wb, 2 * wb, "split", F32), (2 * wb, 3 * wb, "split", F32),
         (2 * wb, 3 * wb, "heads_t", BF16)], tr, n_p, og=og, meta=meta)
    qb_b = min(256, t)
    o_bp = attn_b_prompt(qkvh, vt, lam, b_subln[0], lam_init, batch, t, qb_b)
    o_bs = attn_b_sample(qkvh, cache_b_k[0], cache_b_v[0], lam, b_subln[0], lam_init, n_p, dec_batch, dec_t)
    h, xn, meta, counts = out_router(o_bp, o_bs, h, None, w_b_o[0].astype(BF16), norm_ffn[1], w_route_group[1],
                                     b_route_group[1], w_route_expert[1], b_route_expert[1], True, tr)
    og = moe_experts(xn, meta, counts, w_gate, w_up, w_down, 1, tm)

    y_p, y_s = final_norm(h, og, meta, norm_final, n_p, tr)
    hd2 = 2 * B_HD
    return (y_p.reshape(batch, t, d), y_s.reshape(dec_batch, dec_t, d),
            a_k_prompt, a_v_prompt, a_k_sample, a_v_sample,
            k_bp.reshape(1, batch, t, B_HEADS, hd2), v_bp.reshape(1, batch, t, B_HEADS, hd2),
            k_bs.reshape(1, dec_batch, dec_t, B_HEADS, hd2), v_bs.reshape(1, dec_batch, dec_t, B_HEADS, hd2))
```

```python
import functools

import jax
import jax.numpy as jnp
import numpy as np
from jax import lax
from jax.experimental import pallas as pl
from jax.experimental.pallas import tpu as pltpu
from jax.experimental.pallas import tpu_sc as plsc

F32 = jnp.float32
BF16 = jnp.bfloat16
NEG_INF = float("-inf")

CHUNK = 64
WIN_CHUNKS = 2
A_HEADS, A_KV_HEADS, A_GROUP, A_HD = 16, 4, 4, 64
B_HEADS, B_HD = 8, 64
N_GROUPS, EXPERTS_PER_GROUP = 4, 8
N_EXPERTS = N_GROUPS * EXPERTS_PER_GROUP
NORM_EPS = 1e-6
SUBLN_EPS = 1e-5

LANES = 128
COL_CHUNK = 512
HEAD_GROUP = 2
SC_WINDOW = 128
SC_ROW_WORDS = 256
VMEM_LIMIT = 48 * 1024 * 1024


def _cparams(sem, flags=None):
    return pltpu.CompilerParams(dimension_semantics=sem, vmem_limit_bytes=VMEM_LIMIT, flags=flags)


def _alibi(n):
    return 2.0 ** (-8.0 * np.arange(1, n + 1) / n)


def _store_planes(ref, x):
    w = ref.shape[2]
    for p in range(ref.shape[0]):
        ref[p] = x[:, p * w:(p + 1) * w]


def _join_planes(ref, first, count):
    return jnp.concatenate([ref[first + p] for p in range(count)], axis=1)


def _add_expert_outputs(x, og_ref, meta_ref):
    half = og_ref.shape[1] // 2
    meta = meta_ref[...]
    return (x + meta[:, 2:3] * _unpack_pairs(og_ref[:, :half])) + meta[:, 3:4] * _unpack_pairs(og_ref[:, half:])


def _norm_proj_kernel(n_prompt_blocks, two_src, has_bias, combine, outs, *refs):
    refs = list(refs)
    xa_ref = refs.pop(0)
    xb_ref = refs.pop(0) if two_src else None
    og_ref, meta_ref = (refs.pop(0), refs.pop(0)) if combine else (None, None)
    g_ref = refs.pop(0)
    w_ref = refs.pop(0)
    b_ref = refs.pop(0) if has_bias else None
    out_refs = refs
    x = xa_ref[...]
    if two_src:
        x = jnp.where(pl.program_id(0) < n_prompt_blocks, x, xb_ref[...])
    if combine:
        x = _add_expert_outputs(x, og_ref, meta_ref)
        out_refs.pop(0)[...] = x
    ms = jnp.mean(x * x, axis=-1, keepdims=True)
    xn = (x * lax.rsqrt(ms + NORM_EPS) * g_ref[...]).astype(BF16)
    is_prompt = pl.program_id(0) < n_prompt_blocks
    out_refs = list(out_refs)
    targets = []
    for c0, c1, layout in outs:
        targets.append((c0, c1, layout, out_refs.pop(0), out_refs.pop(0) if layout == "split" else None))
    for s in range(0, w_ref.shape[1], COL_CHUNK):
        e = s + COL_CHUNK
        r = jnp.dot(xn, w_ref[:, s:e], preferred_element_type=F32)
        if has_bias:
            r = r + b_ref[:, s:e]
        for c0, c1, layout, o_ref, o2_ref in targets:
            if not (c0 <= s and e <= c1):
                continue
            rr = r.astype(o_ref.dtype)
            if layout == "flat":
                o_ref[:, s - c0:e - c0] = rr
            elif layout == "split":

                @pl.when(is_prompt)
                def _(rr=rr, o_ref=o_ref, s=s, e=e, c0=c0):
                    o_ref[:, s - c0:e - c0] = rr

                @pl.when(jnp.logical_not(is_prompt))
                def _(rr=rr, o2_ref=o2_ref, s=s, e=e, c0=c0):
                    o2_ref[:, s - c0:e - c0] = rr
            elif layout == "heads":
                for hh in range((e - s) // LANES):
                    o_ref[(s - c0) // LANES + hh] = rr[:, hh * LANES:(hh + 1) * LANES]
            else:
                for hh in range((e - s) // LANES):
                    o_ref[(s - c0) // LANES + hh] = r[:, hh * LANES:(hh + 1) * LANES].T.astype(o_ref.dtype)


def norm_proj(xa, xb, g, w, b, outs, tr, n_prompt, og=None, meta=None):
    na, d = xa.shape
    nb = 0 if xb is None else xb.shape[0]
    n = na + nb
    npb = n_prompt // tr
    two = xb is not None
    combine = og is not None
    in_specs = [pl.BlockSpec((tr, d), (lambda i: (jnp.minimum(i, npb - 1), 0)) if two else (lambda i: (i, 0)))]
    args = [xa]
    if two:
        in_specs.append(pl.BlockSpec((tr, d), lambda i: (jnp.maximum(i - npb, 0), 0)))
        args.append(xb)
    if combine:
        in_specs += [pl.BlockSpec((tr, og.shape[1]), lambda i: (i, 0)), pl.BlockSpec((tr, LANES), lambda i: (i, 0))]
        args += [og, meta]
    in_specs.append(pl.BlockSpec((1, d), lambda i: (0, 0)))
    args.append(g.reshape(1, d).astype(F32))
    in_specs.append(pl.BlockSpec(w.shape, lambda i: (0, 0)))
    args.append(w)
    if b is not None:
        in_specs.append(pl.BlockSpec((1, w.shape[1]), lambda i: (0, 0)))
        args.append(b.reshape(1, -1).astype(F32))
    out_shapes, out_specs = [], []
    if combine:
        out_shapes.append(jax.ShapeDtypeStruct((n, d), F32))
        out_specs.append(pl.BlockSpec((tr, d), lambda i: (i, 0)))
    for c0, c1, layout, dt in outs:
        assert c0 % COL_CHUNK == 0 and c1 % COL_CHUNK == 0
        if layout == "flat":
            out_shapes.append(jax.ShapeDtypeStruct((n, c1 - c0), dt))
            out_specs.append(pl.BlockSpec((tr, c1 - c0), lambda i: (i, 0)))
        elif layout == "split":
            out_shapes.append(jax.ShapeDtypeStruct((n_prompt, c1 - c0), dt))
            out_specs.append(pl.BlockSpec((tr, c1 - c0), lambda i: (jnp.minimum(i, npb - 1), 0)))
            out_shapes.append(jax.ShapeDtypeStruct((n - n_prompt, c1 - c0), dt))
            out_specs.append(pl.BlockSpec((tr, c1 - c0), lambda i: (jnp.maximum(i - npb, 0), 0)))
        elif layout == "heads":
            nh = (c1 - c0) // LANES
            out_shapes.append(jax.ShapeDtypeStruct((nh, n, LANES), dt))
            out_specs.append(pl.BlockSpec((nh, tr, LANES), lambda i: (0, i, 0)))
        else:
            assert layout == "heads_t"
            nh = (c1 - c0) // LANES
            out_shapes.append(jax.ShapeDtypeStruct((nh, LANES, n), dt))
            out_specs.append(pl.BlockSpec((nh, LANES, tr), lambda i: (0, 0, i)))
    kern = functools.partial(_norm_proj_kernel, npb, two, b is not None, combine,
                             [(c0, c1, lay) for c0, c1, lay, _ in outs])
    return pl.pallas_call(
        kern, grid=(n // tr,), in_specs=in_specs, out_specs=out_specs, out_shape=out_shapes,
        compiler_params=_cparams(("arbitrary",)), name="norm_proj")(*args)


def _attn_a_chunk(q, k, v, bias_ref, sink_ref, valid):
    nq = q.shape[0]
    outs = []
    for kh in range(A_KV_HEADS):
        qs = jnp.concatenate(
            [q[:, (kh * A_GROUP + g) * A_HD:(kh * A_GROUP + g + 1) * A_HD] for g in range(A_GROUP)], axis=0)
        kk = k[:, kh * A_HD:(kh + 1) * A_HD]
        s = lax.dot_general(qs, kk, (((1,), (1,)), ((), ())), preferred_element_type=F32) + bias_ref[kh]
        if valid is not None:
            s = jnp.where(valid, s, NEG_INF)
        sk = sink_ref[kh]
        m = jnp.maximum(jnp.max(s, axis=-1, keepdims=True), sk)
        e = jnp.exp2(s - m)
        den = jnp.sum(e, axis=-1, keepdims=True) + jnp.exp2(sk - m)
        p = (e * (1.0 / den)).astype(BF16)
        o = jnp.dot(p, v[:, kh * A_HD:(kh + 1) * A_HD], preferred_element_type=F32)
        for g in range(A_GROUP):
            outs.append(o[g * nq:(g + 1) * nq])
    return jnp.concatenate(outs, axis=1)


def _attn_a_prompt_kernel(qb, q_ref, kvc_ref, kvp_ref, bias_ref, sink_ref, o_ref, t_ref, p_ref):
    cb = pl.program_id(1)
    back = CHUNK * WIN_CHUNKS
    nk = back + qb
    nkv = A_KV_HEADS * A_HD
    kvfull = jnp.concatenate([kvp_ref[...], kvc_ref[...]], axis=0)
    k_all = kvfull[:, :nkv].astype(BF16)
    vt_all = kvfull[:, nkv:].T.astype(BF16)
    key_pos = lax.broadcasted_iota(jnp.int32, (nk, 1), 0) + (cb * qb - back)
    before_start = jnp.where(key_pos >= 0, 0.0, NEG_INF)
    dn = (((1,), (1,)), ((), ()))
    for kh in range(A_KV_HEADS):
        qs = jnp.concatenate(
            [q_ref[:, (kh * A_GROUP + g) * A_HD:(kh * A_GROUP + g + 1) * A_HD] for g in range(A_GROUP)], axis=0)
        t_ref[kh] = lax.dot_general(k_all[:, kh * A_HD:(kh + 1) * A_HD], qs, dn, preferred_element_type=F32)
    inv_dens = []
    for kh in range(A_KV_HEADS):
        st = t_ref[kh] + bias_ref[kh] + before_start
        sk = sink_ref[kh]
        m = jnp.maximum(jnp.max(st, axis=0, keepdims=True), sk)
        e = jnp.exp2(st - m)
        inv_dens.append(1.0 / (jnp.sum(e, axis=0, keepdims=True) + jnp.exp2(sk - m)))
        p_ref[kh] = e.astype(BF16)
    outs = []
    for kh in range(A_KV_HEADS):
        ot = jnp.dot(vt_all[kh * A_HD:(kh + 1) * A_HD, :], p_ref[kh], preferred_element_type=F32)
        o = (ot * inv_dens[kh]).T
        for g in range(A_GROUP):
            outs.append(o[g * qb:(g + 1) * qb])
    o_ref[...] = jnp.concatenate(outs, axis=1).astype(o_ref.dtype)


def _a_prompt_tables(qb, sinks):
    back = CHUNK * WIN_CHUNKS
    slopes = _alibi(A_HEADS).reshape(A_KV_HEADS, A_GROUP)
    qpos = np.arange(qb)
    kpos = np.arange(back + qb) - back
    dist = np.abs(kpos[:, None] - qpos[None, :]).astype(np.float64)
    band_lo = (qpos // CHUNK) * CHUNK - back
    in_band = (kpos[:, None] >= band_lo[None, :]) & (kpos[:, None] < band_lo[None, :] + back + CHUNK)
    bias = np.where(in_band[None, None], -slopes[:, :, None, None] * dist[None, None], -np.inf)
    bias = np.transpose(bias, (0, 2, 1, 3)).reshape(A_KV_HEADS, back + qb, A_GROUP * qb) * LOG2E
    sk = jnp.repeat(sinks.astype(F32).reshape(A_KV_HEADS, A_GROUP), qb, axis=1)[:, None, :] * LOG2E
    return jnp.asarray(bias, F32), sk


def _a_tables(qpos, kpos, sinks):
    slopes = _alibi(A_HEADS).reshape(A_KV_HEADS, A_GROUP)
    dist = np.abs(qpos[:, None] - kpos[None, :]).astype(np.float64)
    bias = -slopes[:, :, None, None] * dist
    nq = len(qpos)
    bias = jnp.asarray(bias.reshape(A_KV_HEADS, A_GROUP * nq, len(kpos)) * LOG2E, F32)
    sk = jnp.repeat(sinks.astype(F32).reshape(A_KV_HEADS, A_GROUP), nq, axis=1)[..., None] * LOG2E
    return bias, sk


def attn_a_prompt(q_all, kv_all, sinks, batch, t, qb):
    nq = A_HEADS * A_HD
    nkv2 = 2 * A_KV_HEADS * A_HD
    back = CHUNK * WIN_CHUNKS
    bias, sk = _a_prompt_tables(qb, sinks)
    nblk = t // qb
    r = qb // back
    return pl.pallas_call(
        functools.partial(_attn_a_prompt_kernel, qb),
        grid=(batch, nblk),
        in_specs=[
            pl.BlockSpec((qb, nq), lambda b, i: (b * nblk + i, 0)),
            pl.BlockSpec((qb, nkv2), lambda b, i: (b * nblk + i, 0)),
            pl.BlockSpec((back, nkv2), lambda b, i: (jnp.maximum((b * nblk + i) * r - 1, 0), 0)),
            pl.BlockSpec(bias.shape, lambda b, i: (0, 0, 0)),
            pl.BlockSpec(sk.shape, lambda b, i: (0, 0, 0)),
        ],
        out_specs=pl.BlockSpec((qb, nq), lambda b, i: (b * nblk + i, 0)),
        out_shape=jax.ShapeDtypeStruct((batch * t, nq), BF16),
        scratch_shapes=[pltpu.VMEM((A_KV_HEADS, back + qb, A_GROUP * qb), F32),
                        pltpu.VMEM((A_KV_HEADS, back + qb, A_GROUP * qb), BF16)],
        compiler_params=_cparams(("arbitrary", "arbitrary")), name="attn_a_prompt",
    )(q_all, kv_all, kv_all, bias, sk)


def _attn_a_sample_kernel(q_ref, k_ref, v_ref, bias_ref, sink_ref, o_ref):
    o = _attn_a_chunk(q_ref[...], k_ref[...].astype(BF16), v_ref[...].astype(BF16), bias_ref, sink_ref, None)
    o_ref[...] = o.astype(o_ref.dtype)


def attn_a_sample(q_all, kband, vband, sinks, row0, dec_batch, dec_t):
    nq = A_HEADS * A_HD
    s = kband.shape[1]
    past = s - dec_t
    kpos = np.concatenate([np.arange(past) - past, np.arange(dec_t)])
    bias, sk = _a_tables(np.arange(dec_t), kpos, sinks)
    blk0 = row0 // dec_t
    return pl.pallas_call(
        _attn_a_sample_kernel,
        grid=(dec_batch,),
        in_specs=[
            pl.BlockSpec((dec_t, nq), lambda b: (blk0 + b, 0)),
            pl.BlockSpec((None, s, kband.shape[2]), lambda b: (b, 0, 0)),
            pl.BlockSpec((None, s, vband.shape[2]), lambda b: (b, 0, 0)),
            pl.BlockSpec(bias.shape, lambda b: (0, 0, 0)),
            pl.BlockSpec(sk.shape, lambda b: (0, 0, 0)),
        ],
        out_specs=pl.BlockSpec((dec_t, nq), lambda b: (b, 0)),
        out_shape=jax.ShapeDtypeStruct((dec_batch * dec_t, nq), BF16),
        compiler_params=_cparams(("arbitrary",)), name="attn_a_sample",
    )(q_all, kband, vband, bias, sk)


B_Q_SCALE = B_HD ** -0.5 * float(np.log2(np.e))
LOG2E = float(np.log2(np.e))


def _stack_maps(qh):
    lane = lax.broadcasted_iota(jnp.int32, qh.shape, 1)
    zero = jnp.zeros_like(qh)
    return jnp.concatenate([jnp.where(lane < B_HD, qh, zero), jnp.where(lane >= B_HD, qh, zero)], axis=0)


def _diff_finish(acc, l, nq, lam, lam_init, subln):
    inv = 1.0 / l
    o = acc[:nq] * inv[:nq] - lam * (acc[nq:] * inv[nq:])
    ms = jnp.mean(o * o, axis=-1, keepdims=True)
    return o * lax.rsqrt(ms + SUBLN_EPS) * subln * (1.0 - lam_init)


def _attn_b_prompt_kernel(qb, lam_init, slope_ref, lam_ref, q_ref, k_ref, vt_ref, kbias_ref,
                          subln_ref, o_ref, acc_ref, t0_ref, t1_ref, p0_ref, p1_ref):
    i = pl.program_id(1)
    lam = lam_ref[0]
    dn = (((1,), (1,)), ((), ()))
    t_refs = (t0_ref, t1_ref)
    p_refs = (p0_ref, p1_ref)

    def group_body(hg, carry):
        heads = [hg * HEAD_GROUP + u for u in range(HEAD_GROUP)]
        slopes = [slope_ref[h] for h in heads]
        qqs = [_stack_maps(q_ref[h]) for h in heads]

        def scores(j, slot):
            start = pl.multiple_of(j * qb, qb)
            for u, h in enumerate(heads):
                t_refs[slot][u] = lax.dot_general(k_ref[h, pl.ds(start, qb), :], qqs[u], dn,
                                                  preferred_element_type=F32)

        def softmax_step(j, slot, c):
            out = []
            on_diag = (j == i).astype(jnp.int32)
            for u, h in enumerate(heads):
                m, l, _ = c[u]
                t = t_refs[slot][u] + kbias_ref[h, on_diag]
                off = slopes[u] * jnp.full((1, 2 * qb), (j - i) * qb, jnp.int32).astype(F32)
                m_new = jnp.maximum(m, jnp.max(t, axis=0, keepdims=True) + off)
                alpha = jnp.exp2(m - m_new)
                p = jnp.exp2(t + (off - m_new))
                p_refs[slot][u] = p.astype(BF16)
                out.append((m_new, alpha * l + jnp.sum(p, axis=0, keepdims=True), alpha))
            return tuple(out)

        def accumulate(j, slot, c):
            start = pl.multiple_of(jnp.maximum(j, 0) * qb, qb)
            for u, h in enumerate(heads):
                pv = jnp.dot(vt_ref[h, :, pl.ds(start, qb)], p_refs[slot][u], preferred_element_type=F32)
                acc_ref[u] = c[u][2] * acc_ref[u] + pv

        def step(j, c, slot):
            scores(jnp.minimum(j + 1, i), 1 - slot)
            c_new = softmax_step(j, slot, c)
            accumulate(j - 1, 1 - slot, c)
            return c_new

        def drain(c, slot):
            accumulate(i, slot, c)
            return 0

        acc_ref[...] = jnp.zeros_like(acc_ref)
        p1_ref[...] = jnp.zeros_like(p1_ref)
        scores(0, 0)
        init = tuple((jnp.full((1, 2 * qb), NEG_INF, F32), jnp.zeros((1, 2 * qb), F32), jnp.ones((1, 2 * qb), F32))
                     for _ in heads)
        c = lax.fori_loop(
            0, i + 1, lambda j, c: lax.cond(j % 2 == 0, lambda: step(j, c, 0), lambda: step(j, c, 1)), init)
        lax.cond(i % 2 == 0, lambda: drain(c, 0), lambda: drain(c, 1))
        for u, h in enumerate(heads):
            inv = 1.0 / c[u][1]
            acc = acc_ref[u]
            ot = acc[:, :qb] * inv[:, :qb] - lam * (acc[:, qb:] * inv[:, qb:])
            ms = jnp.mean(ot * ot, axis=0, keepdims=True)
            ot = ot * lax.rsqrt(ms + SUBLN_EPS) * subln_ref[...] * (1.0 - lam_init)
            o_ref[h] = ot.T.astype(o_ref.dtype)
        return carry

    for hg in range(B_HEADS // HEAD_GROUP):
        group_body(hg, 0)


def attn_b_prompt(qkvh, vt, lam, subln, lam_init, batch, t, qb):
    nblk = t // qb
    qq = np.arange(qb)
    slope2 = (_alibi(B_HEADS) * LOG2E)[:, None, None]
    kcol = np.broadcast_to(qq[:, None], (qb, qb)).astype(np.float64)
    tri = -2.0 * np.maximum(qq[:, None] - qq[None, :], 0)
    msk = np.where((qq[:, None] // CHUNK) <= (qq[None, :] // CHUNK), 0.0, -np.inf)
    both = lambda a: np.concatenate([a, a], axis=-1)
    kbias = np.stack([both(slope2 * kcol[None]), both(slope2 * (kcol + tri)[None] + msk[None])], axis=1)
    kbias = jnp.asarray(kbias, F32)
    slopes = jnp.asarray(_alibi(B_HEADS) * LOG2E, F32)
    subln_t = jnp.broadcast_to(subln.astype(F32).reshape(LANES, 1), (LANES, qb))
    const = lambda b, i, *_: (0, 0)
    grid_spec = pltpu.PrefetchScalarGridSpec(
        num_scalar_prefetch=2, grid=(batch, nblk),
        in_specs=[
            pl.BlockSpec((B_HEADS, qb, LANES), lambda b, i, *_: (0, b * nblk + i, 0)),
            pl.BlockSpec((B_HEADS, t, LANES), lambda b, i, *_: (1, b, 0)),
            pl.BlockSpec((B_HEADS, LANES, t), lambda b, i, *_: (0, 0, b)),
            pl.BlockSpec(kbias.shape, lambda b, i, *_: (0, 0, 0, 0)),
            pl.BlockSpec(subln_t.shape, const),
        ],
        out_specs=pl.BlockSpec((B_HEADS, qb, LANES), lambda b, i, *_: (0, b * nblk + i, 0)),
        scratch_shapes=[pltpu.VMEM((HEAD_GROUP, LANES, 2 * qb), F32),
                        pltpu.VMEM((HEAD_GROUP, qb, 2 * qb), F32), pltpu.VMEM((HEAD_GROUP, qb, 2 * qb), F32),
                        pltpu.VMEM((HEAD_GROUP, qb, 2 * qb), BF16), pltpu.VMEM((HEAD_GROUP, qb, 2 * qb), BF16)],
    )
    return pl.pallas_call(
        functools.partial(_attn_b_prompt_kernel, qb, lam_init), grid_spec=grid_spec,
        out_shape=jax.ShapeDtypeStruct((B_HEADS, batch * t, LANES), BF16),
        compiler_params=_cparams(("arbitrary", "arbitrary")), name="attn_b_prompt",
    )(slopes, lam.reshape(1).astype(F32), qkvh, qkvh, vt, kbias, subln_t)


def _attn_b_sample_kernel(dec_t, past, lam_init, slopes, lam_ref, q_ref, kn_ref, vn_ref, ck_ref, cv_ref,
                          tnew_ref, subln_ref, o_ref):
    lam = lam_ref[0]
    subln = subln_ref[...]
    colc = lax.broadcasted_iota(jnp.int32, (1, past), 1).astype(F32) - float(past)
    for h in range(B_HEADS):
        slope = float(slopes[h]) * LOG2E
        qq = _stack_maps(q_ref[h])
        kc = ck_ref[pl.ds(h, past, stride=B_HEADS), :].astype(BF16)
        vc = cv_ref[pl.ds(h, past, stride=B_HEADS), :].astype(BF16)
        dn = (((1,), (1,)), ((), ()))
        s_c = lax.dot_general(qq, kc, dn, preferred_element_type=F32) + slope * colc
        s_n = lax.dot_general(qq, kn_ref[h], dn, preferred_element_type=F32) + slope * tnew_ref[...]
        m = jnp.maximum(jnp.max(s_c, axis=-1, keepdims=True), jnp.max(s_n, axis=-1, keepdims=True))
        p_c = jnp.exp2(s_c - m)
        p_n = jnp.exp2(s_n - m)
        l = jnp.sum(p_c, axis=-1, keepdims=True) + jnp.sum(p_n, axis=-1, keepdims=True)
        acc = (jnp.dot(p_c.astype(BF16), vc, preferred_element_type=F32)
               + jnp.dot(p_n.astype(BF16), vn_ref[h], preferred_element_type=F32))
        o_ref[h] = _diff_finish(acc, l, dec_t, lam, lam_init, subln).astype(o_ref.dtype)


def attn_b_sample(qkvh, cache_k, cache_v, lam, subln, lam_init, row0, dec_batch, dec_t):
    past = cache_k.shape[1]
    qi = np.arange(dec_t)
    tnew = (qi[:, None] - np.abs(qi[:, None] - qi[None, :])).astype(np.float32)
    tnew = jnp.asarray(np.concatenate([tnew, tnew], axis=0))
    blk0 = row0 // dec_t
    rows = past * B_HEADS
    cache_k = cache_k.reshape(dec_batch, rows, LANES)
    cache_v = cache_v.reshape(dec_batch, rows, LANES)
    grid_spec = pltpu.PrefetchScalarGridSpec(
        num_scalar_prefetch=1, grid=(dec_batch,),
        in_specs=[
            pl.BlockSpec((B_HEADS, dec_t, LANES), lambda b, *_: (0, blk0 + b, 0)),
            pl.BlockSpec((B_HEADS, dec_t, LANES), lambda b, *_: (1, blk0 + b, 0)),
            pl.BlockSpec((B_HEADS, dec_t, LANES), lambda b, *_: (2, blk0 + b, 0)),
            pl.BlockSpec((None, rows, LANES), lambda b, *_: (b, 0, 0)),
            pl.BlockSpec((None, rows, LANES), lambda b, *_: (b, 0, 0)),
            pl.BlockSpec(tnew.shape, lambda b, *_: (0, 0)),
            pl.BlockSpec((1, LANES), lambda b, *_: (0, 0)),
        ],
        out_specs=pl.BlockSpec((B_HEADS, dec_t, LANES), lambda b, *_: (0, b, 0)),
    )
    return pl.pallas_call(
        functools.partial(_attn_b_sample_kernel, dec_t, past, lam_init, tuple(_alibi(B_HEADS))),
        grid_spec=grid_spec,
        out_shape=jax.ShapeDtypeStruct((B_HEADS, dec_batch * dec_t, LANES), BF16),
        compiler_params=_cparams(("arbitrary",)), name="attn_b_sample",
    )(lam.reshape(1).astype(F32), qkvh, qkvh, qkvh, cache_k, cache_v, tnew, subln.reshape(1, LANES).astype(F32))


def _route(logits):
    lane = lax.broadcasted_iota(jnp.int32, logits.shape, 1).astype(F32)
    big = float(LANES)
    lg = jnp.where(lane < N_GROUPS, logits, NEG_INF)
    gmax = jnp.max(lg, axis=-1, keepdims=True)
    g_idx = jnp.min(jnp.where(lg == gmax, lane, big), axis=-1, keepdims=True)
    g_prob = 1.0 / jnp.sum(jnp.exp(lg - gmax), axis=-1, keepdims=True)
    lo = N_GROUPS + EXPERTS_PER_GROUP * g_idx
    le = jnp.where((lane >= lo) & (lane < lo + EXPERTS_PER_GROUP), logits, NEG_INF)
    m1 = jnp.max(le, axis=-1, keepdims=True)
    i1 = jnp.min(jnp.where(le == m1, lane, big), axis=-1, keepdims=True)
    le2 = jnp.where(lane == i1, NEG_INF, le)
    m2 = jnp.max(le2, axis=-1, keepdims=True)
    i2 = jnp.min(jnp.where(le2 == m2, lane, big), axis=-1, keepdims=True)
    t = jnp.exp(m2 - m1)
    ga = 1.0 / (1.0 + t)
    gb = t * ga
    return lane, i1, i2, g_prob * ga, g_prob * gb


def _unpack_pairs(u):
    lo = pltpu.bitcast(u << 16, F32)
    hi = pltpu.bitcast(u & jnp.uint32(0xFFFF0000), F32)
    return jnp.concatenate([lo, hi], axis=1)


def _pack_pairs(x):
    bits = pltpu.bitcast(x.astype(BF16).astype(F32), jnp.uint32)
    w = x.shape[1] // 2
    return (bits[:, :w] >> 16) | bits[:, w:]


def _out_router_kernel(n_prompt_blocks, heads_in, two_o, two_res, *refs):
    refs = list(refs)
    oa_ref = refs.pop(0)
    ob_ref = refs.pop(0) if two_o else None
    ra_ref = refs.pop(0)
    rb_ref = refs.pop(0) if two_res else None
    wo_ref, g_ref, wr_ref, wr2_ref, br_ref, ltri_ref, h_ref, xn_ref, meta_ref, cnt_ref, carry_ref = refs
    is_prompt = pl.program_id(0) < n_prompt_blocks

    def load_o(ref):
        if heads_in:
            return jnp.concatenate([ref[hh] for hh in range(ref.shape[0])], axis=1)
        return ref[...]

    o = load_o(oa_ref)
    if two_o:
        o = jnp.where(is_prompt, o, load_o(ob_ref))
    res = ra_ref[...]
    if two_res:
        res = jnp.where(is_prompt, res, rb_ref[...])
    h = res + jnp.dot(o, wo_ref[...], preferred_element_type=F32)
    h_ref[...] = h
    ms = jnp.mean(h * h, axis=-1, keepdims=True)
    xn = h * lax.rsqrt(ms + NORM_EPS) * g_ref[...]
    x_hi = xn.astype(BF16)
    _store_planes(xn_ref, _pack_pairs(xn))
    x_lo = (xn - x_hi.astype(F32)).astype(BF16)
    r = jnp.dot(x_hi, wr_ref[...], preferred_element_type=F32)
    logits = (r[:, :LANES] + r[:, LANES:]) + jnp.dot(x_lo, wr2_ref[...], preferred_element_type=F32) + br_ref[...]
    lane, i1, i2, g1, g2 = _route(logits)

    @pl.when(pl.program_id(0) == 0)
    def _():
        carry_ref[...] = jnp.zeros_like(carry_ref)

    sel1 = lane == i1
    sel2 = lane == i2
    onehot = jnp.where(sel1 | sel2, 1.0, 0.0)
    before = jnp.dot(ltri_ref[...], onehot.astype(BF16), preferred_element_type=F32) + carry_ref[...]
    rank1 = jnp.sum(jnp.where(sel1, before, 0.0), axis=-1, keepdims=True)
    rank2 = jnp.sum(jnp.where(sel2, before, 0.0), axis=-1, keepdims=True)
    carry_ref[...] = carry_ref[...] + jnp.sum(onehot, axis=0, keepdims=True)
    cnt_ref[...] = carry_ref[...]
    cols = [i1 - N_GROUPS, i2 - N_GROUPS, g1, g2, rank1, rank2]
    meta = jnp.zeros_like(logits)
    for c, v in enumerate(cols):
        meta = jnp.where(lane == c, v, meta)
    meta_ref[...] = meta


def out_router(oa, ob, ra, rb, wo, g, w_rg, b_rg, w_re, b_re, heads_in, tr):
    d = ra.shape[1]
    if heads_in:
        na = oa.shape[1]
        nb = 0 if ob is None else ob.shape[1]
    else:
        na = oa.shape[0]
        nb = 0 if ob is None else ob.shape[0]
    n = na + nb
    npb = na // tr
    two_o = ob is not None
    two_res = rb is not None
    first = lambda i: (jnp.minimum(i, npb - 1), 0)
    second = lambda i: (jnp.maximum(i - npb, 0), 0)
    plain = lambda i: (i, 0)
    in_specs, args = [], []

    def add_o(x, imap):
        if heads_in:
            in_specs.append(pl.BlockSpec((x.shape[0], tr, LANES), lambda i: (0, imap(i)[0], 0)))
        else:
            in_specs.append(pl.BlockSpec((tr, x.shape[1]), imap))
        args.append(x)

    add_o(oa, first if two_o else plain)
    if two_o:
        add_o(ob, second)
    in_specs.append(pl.BlockSpec((tr, d), first if two_res else plain))
    args.append(ra)
    if two_res:
        in_specs.append(pl.BlockSpec((tr, d), second))
        args.append(rb)
    wr = jnp.zeros((d, LANES), F32).at[:, :N_GROUPS].set(w_rg.astype(F32))
    wr = wr.at[:, N_GROUPS:N_GROUPS + N_EXPERTS].set(w_re.astype(F32))
    wr_hi = wr.astype(BF16)
    wr_lo = (wr - wr_hi.astype(F32)).astype(BF16)
    br = jnp.zeros((1, LANES), F32).at[0, :N_GROUPS].set(b_rg.astype(F32))
    br = br.at[0, N_GROUPS:N_GROUPS + N_EXPERTS].set(b_re.astype(F32))
    ltri = jnp.asarray(np.tril(np.ones((tr, tr), np.float32), -1), BF16)
    consts = [wo, g.reshape(1, d).astype(F32), jnp.concatenate([wr_hi, wr_lo], axis=1), wr_hi, br, ltri]
    for cst in consts:
        in_specs.append(pl.BlockSpec(cst.shape, lambda i: (0, 0)))
        args.append(cst)
    parts = d // 2 // SC_ROW_WORDS
    out_shape = [jax.ShapeDtypeStruct((n, d), F32), jax.ShapeDtypeStruct((parts, n, SC_ROW_WORDS), jnp.uint32),
                 jax.ShapeDtypeStruct((n, LANES), F32), jax.ShapeDtypeStruct((1, LANES), F32)]
    out_specs = [pl.BlockSpec((tr, d), plain), pl.BlockSpec((parts, tr, SC_ROW_WORDS), lambda i: (0, i, 0)),
                 pl.BlockSpec((tr, LANES), plain), pl.BlockSpec((1, LANES), lambda i: (0, 0))]
    return pl.pallas_call(
        functools.partial(_out_router_kernel, npb, heads_in, two_o, two_res),
        grid=(n // tr,), in_specs=in_specs, out_specs=out_specs, out_shape=out_shape,
        scratch_shapes=[pltpu.VMEM((1, LANES), F32)],
        compiler_params=_cparams(("arbitrary",)), name="out_router")(*args)


def _expert_kernel(te_ref, nt_ref, x_ref, wg_ref, wu_ref, wd_ref, o_ref):
    @pl.when(pl.program_id(0) < nt_ref[0])
    def _():
        x = _unpack_pairs(_join_planes(x_ref, 0, x_ref.shape[0])).astype(BF16)
        a = jnp.dot(x, wg_ref[...].astype(BF16), preferred_element_type=F32)
        u = jnp.dot(x, wu_ref[...].astype(BF16), preferred_element_type=F32)
        hh = (a * (1.0 / (1.0 + jnp.exp(-a))) * u).astype(BF16)
        _store_planes(o_ref, _pack_pairs(jnp.dot(hh, wd_ref[...].astype(BF16), preferred_element_type=F32)))

    @pl.when(pl.program_id(0) >= nt_ref[0])
    def _():
        o_ref[...] = jnp.zeros_like(o_ref)


def expert_mlp(xs, tile_expert, n_tiles, w_gate, w_up, w_down, layer, tm):
    parts, p, w = xs.shape
    d, f = w_gate.shape[-2:]
    epg = w_gate.shape[2]
    wmap = lambda t, te, nt: (layer, te[t] // epg, te[t] % epg, 0, 0)
    grid_spec = pltpu.PrefetchScalarGridSpec(
        num_scalar_prefetch=2, grid=(p // tm,),
        in_specs=[
            pl.BlockSpec((parts, tm, w), lambda t, te, nt: (0, t, 0)),
            pl.BlockSpec((None, None, None, d, f), wmap),
            pl.BlockSpec((None, None, None, d, f), wmap),
            pl.BlockSpec((None, None, None, f, d), wmap),
        ],
        out_specs=pl.BlockSpec((parts, tm, w), lambda t, te, nt: (0, t, 0)),
    )
    return pl.pallas_call(
        _expert_kernel, grid_spec=grid_spec, out_shape=jax.ShapeDtypeStruct((parts, p, w), jnp.uint32),
        compiler_params=_cparams(("arbitrary",)), name="expert_mlp")(tile_expert, n_tiles, xs, w_gate, w_up, w_down)


def sc_scatter_rows(x2, idx, n_out):
    mesh = plsc.VectorSubcoreMesh(core_axis_name="c", subcore_axis_name="s")
    n_src_win = x2.shape[0] // SC_WINDOW

    @functools.partial(pl.kernel, out_type=jax.ShapeDtypeStruct((n_out, SC_ROW_WORDS), x2.dtype), mesh=mesh)
    def scatter(x_hbm, i_hbm, o_hbm):
        def body(x_vmem, i_vmem):
            pltpu.sync_copy(x_vmem, o_hbm.at[i_vmem.at[0]])

        pltpu.emit_pipeline(
            body, grid=(n_out // SC_WINDOW,),
            in_specs=[pl.BlockSpec((SC_WINDOW, SC_ROW_WORDS), lambda i: (lax.rem(i, n_src_win), 0)),
                      pl.BlockSpec((1, SC_WINDOW), lambda i: (0, i))],
            out_specs=[], core_axis_name=("c", "s"), dimension_semantics=(pltpu.PARALLEL,),
        )(x_hbm, i_hbm)

    return scatter(x2, idx)


def sc_gather_rows(x2, idx):
    mesh = plsc.VectorSubcoreMesh(core_axis_name="c", subcore_axis_name="s")
    n_col, n = idx.shape

    @functools.partial(pl.kernel, out_type=jax.ShapeDtypeStruct((n, n_col * SC_ROW_WORDS), x2.dtype), mesh=mesh)
    def gather(x_hbm, i_hbm, o_hbm):
        def body(i_vmem, o_vmem):
            pltpu.sync_copy(x_hbm.at[i_vmem.at[0]], o_vmem)

        pltpu.emit_pipeline(
            body, grid=(n // SC_WINDOW, n_col),
            in_specs=[pl.BlockSpec((1, SC_WINDOW), lambda i, c: (c, i))],
            out_specs=[pl.BlockSpec((SC_WINDOW, SC_ROW_WORDS), lambda i, c: (i, c))],
            core_axis_name=("c", "s"), dimension_semantics=(pltpu.PARALLEL, pltpu.ARBITRARY),
        )(i_hbm, o_hbm)

    return gather(x2, idx)


def moe_dispatch(xn_packed, meta, counts, tm):
    parts, n, w = xn_packed.shape
    e = meta[:, 0:2].astype(jnp.int32)
    rank = meta[:, 4:6].astype(jnp.int32)
    counts = counts[0, N_GROUPS:N_GROUPS + N_EXPERTS].astype(jnp.int32)
    padded = ((counts + tm - 1) // tm) * tm
    pend = jnp.cumsum(padded)
    pstart = pend - padded
    ids = jnp.arange(N_EXPERTS, dtype=jnp.int32)
    pos = jnp.sum(jnp.where(e[..., None] == ids, pstart, 0), axis=-1) + rank
    p = 2 * n + N_EXPERTS * tm
    n_pad = p - 2 * n
    gap_len = jnp.concatenate([padded - counts, (p - pend[-1]).reshape(1)])
    gap_first = jnp.concatenate([pstart + counts, pend[-1:]])
    gap_end = jnp.cumsum(gap_len)
    k = jnp.arange(n_pad, dtype=jnp.int32)
    gap = jnp.sum((k[:, None] >= gap_end[None, :]).astype(jnp.int32), axis=1)
    shift = gap_first - (gap_end - gap_len)
    pad_pos = k + jnp.sum(jnp.where(gap[:, None] == jnp.arange(N_EXPERTS + 1), shift, 0), axis=-1)
    planes = lambda rows: jnp.concatenate([rows + q * p for q in range(parts)])
    scat_idx = jnp.concatenate([planes(pos[:, 0]), planes(pos[:, 1]), planes(pad_pos)]).reshape(1, parts * p)
    gath_idx = jnp.stack([pos[:, k] + q * p for k in range(2) for q in range(parts)])
    xs = sc_scatter_rows(xn_packed.reshape(parts * n, w), scat_idx, parts * p).reshape(parts, p, w)
    tile_start = jnp.arange(p // tm, dtype=jnp.int32) * tm
    tile_expert = jnp.sum((tile_start[:, None] >= pend[None, :]).astype(jnp.int32), axis=1)
    tile_expert = jnp.minimum(tile_expert, N_EXPERTS - 1)
    n_tiles = (pend[-1] // tm).astype(jnp.int32).reshape(1)
    return xs, tile_expert, n_tiles, gath_idx


def moe_experts(xn_packed, meta, counts, w_gate, w_up, w_down, layer, tm):
    parts, n, w = xn_packed.shape
    xs, tile_expert, n_tiles, gath_idx = moe_dispatch(xn_packed, meta, counts, tm)
    o = expert_mlp(xs, tile_expert, n_tiles, w_gate, w_up, w_down, layer, tm)
    return sc_gather_rows(o.reshape(parts * o.shape[1], w), gath_idx)


def _final_norm_kernel(n_prompt_blocks, h_ref, o0_ref, o1_ref, meta_ref, g_ref, yp_ref, ys_ref):
    meta = meta_ref[...]
    x = (h_ref[...] + meta[:, 2:3] * _unpack_pairs(o0_ref[...])) + meta[:, 3:4] * _unpack_pairs(o1_ref[...])
    ms = jnp.mean(x * x, axis=-1, keepdims=True)
    y = x * lax.rsqrt(ms + NORM_EPS) * g_ref[...]
    i = pl.program_id(0)

    @pl.when(i < n_prompt_blocks)
    def _():
        yp_ref[...] = y

    @pl.when(i >= n_prompt_blocks)
    def _():
        ys_ref[...] = y


def final_norm(h, og, meta, g, n_prompt, tr):
    n, d = h.shape
    npb = n_prompt // tr
    return pl.pallas_call(
        functools.partial(_final_norm_kernel, npb),
        grid=(n // tr,),
        in_specs=[pl.BlockSpec((tr, d), lambda i: (i, 0)),
                  pl.BlockSpec((tr, og.shape[1] // 2), lambda i: (i, 0)),
                  pl.BlockSpec((tr, og.shape[1] // 2), lambda i: (i, 1)),
                  pl.BlockSpec((tr, LANES), lambda i: (i, 0)), pl.BlockSpec((1, d), lambda i: (0, 0))],
        out_specs=[pl.BlockSpec((tr, d), lambda i: (jnp.minimum(i, npb - 1), 0)),
                   pl.BlockSpec((tr, d), lambda i: (jnp.maximum(i - npb, 0), 0))],
        out_shape=[jax.ShapeDtypeStruct((n_prompt, d), F32), jax.ShapeDtypeStruct((n - n_prompt, d), F32)],
        compiler_params=_cparams(("arbitrary",)), name="final_norm")(h, og, og, meta, g.reshape(1, d).astype(F32))


def _pick_rows_tile(n_prompt, n_sample, largest=512):
    for tr in (1024, 512, 256, 128, 64, 32, 16, 8):
        if tr <= largest and n_prompt % tr == 0 and n_sample % tr == 0:
            return tr
    raise ValueError("row counts must be multiples of 8")


def kernel(x_prompt, x_sample, cache_a_k, cache_a_v, cache_b_k, cache_b_v, norm_mix, norm_ffn, norm_final,
           w_a_qkv, b_a_qkv, a_sinks, w_a_o, w_b_qkv, b_lambda, b_subln, w_b_o,
           w_route_group, b_route_group, w_route_expert, b_route_expert, w_gate, w_up, w_down):
    batch, t, d = x_prompt.shape
    dec_batch, dec_t, _ = x_sample.shape
    n_p, n_s = batch * t, dec_batch * dec_t
    tr = _pick_rows_tile(n_p, n_s)
    tm = min(512, tr * 4)
    xp = x_prompt.reshape(n_p, d)
    xs = x_sample.reshape(n_s, d)
    nq_a = A_HEADS * A_HD
    nkv_a = A_KV_HEADS * A_HD

    a_scale = jnp.where(jnp.arange(nq_a + 2 * nkv_a) < nq_a, A_HD ** -0.5 * LOG2E, 1.0).astype(F32)
    q_a, kv_ap, kv_as = norm_proj(xp, xs, norm_mix[0], (w_a_qkv[0] * a_scale).astype(BF16), b_a_qkv[0] * a_scale,
                                  [(0, nq_a, "flat", BF16), (nq_a, nq_a + 2 * nkv_a, "split", F32)], tr, n_p)
    qb_a = min(256, t)
    o_ap = attn_a_prompt(q_a, kv_ap, a_sinks[0], batch, t, qb_a)
    kv_s = kv_as.reshape(dec_batch, dec_t, 2 * nkv_a)
    past_a = cache_a_k.shape[2]
    kband = jnp.concatenate([cache_a_k[0].reshape(dec_batch, past_a, nkv_a), kv_s[..., :nkv_a]], axis=1)
    vband = jnp.concatenate([cache_a_v[0].reshape(dec_batch, past_a, nkv_a), kv_s[..., nkv_a:]], axis=1)
    o_as = attn_a_sample(q_a, kband, vband, a_sinks[0], n_p, dec_batch, dec_t)
    h, xn, meta, counts = out_router(o_ap, o_as, xp, xs, w_a_o[0].astype(BF16), norm_ffn[0], w_route_group[0],
                                     b_route_group[0], w_route_expert[0], b_route_expert[0], False, tr)
    og = moe_experts(xn, meta, counts, w_gate, w_up, w_down, 0, tm)

    keep = min(CHUNK * WIN_CHUNKS, t)
    kv_p = kv_ap.reshape(batch, t, 2 * nkv_a)[:, t - keep:]
    a_k_prompt = kv_p[..., :nkv_a].reshape(1, batch, keep, A_KV_HEADS, A_HD)
    a_v_prompt = kv_p[..., nkv_a:].reshape(1, batch, keep, A_KV_HEADS, A_HD)
    a_k_sample = kband[:, dec_t:].reshape(1, dec_batch, past_a, A_KV_HEADS, A_HD)
    a_v_sample = vband[:, dec_t:].reshape(1, dec_batch, past_a, A_KV_HEADS, A_HD)

    lam_init = 0.8 - 0.6 * float(np.exp(-0.3 * 1))
    lp = b_lambda[0].astype(F32)
    lam = jnp.exp(jnp.sum(lp[0] * lp[1])) - jnp.exp(jnp.sum(lp[2] * lp[3])) + lam_init
    wb = B_HEADS * 2 * B_HD
    col_scale = jnp.where(jnp.arange(3 * wb) < wb, B_Q_SCALE, 1.0).astype(F32)
    h, qkvh, k_bp, k_bs, v_bp, v_bs, vt = norm_proj(
        h, None, norm_mix[1], (w_b_qkv[0] * col_scale).astype(BF16), None,
        [(0, 3 * wb, "heads", BF16), (wb, 2 * wb, "split", F32), (2 * wb, 3 * wb, "split", F32),
         (2 * wb, 3 * wb, "heads_t", BF16)], tr, n_p, og=og, meta=meta)
    qb_b = min(256, t)
    o_bp = attn_b_prompt(qkvh, vt, lam, b_subln[0], lam_init, batch, t, qb_b)
    o_bs = attn_b_sample(qkvh, cache_b_k[0], cache_b_v[0], lam, b_subln[0], lam_init, n_p, dec_batch, dec_t)
    h, xn, meta, counts = out_router(o_bp, o_bs, h, None, w_b_o[0].astype(BF16), norm_ffn[1], w_route_group[1],
                                     b_route_group[1], w_route_expert[1], b_route_expert[1], True, tr)
    og = moe_experts(xn, meta, counts, w_gate, w_up, w_down, 1, tm)

    y_p, y_s = final_norm(h, og, meta, norm_final, n_p, tr)
    hd2 = 2 * B_HD
    return (y_p.reshape(batch, t, d), y_s.reshape(dec_batch, dec_t, d),
            a_k_prompt, a_v_prompt, a_k_sample, a_v_sample,
            k_bp.reshape(1, batch, t, B_HEADS, hd2), v_bp.reshape(1, batch, t, B_HEADS, hd2),
            k_bs.reshape(1, dec_batch, dec_t, B_HEADS, hd2), v_bs.reshape(1, dec_batch, dec_t, B_HEADS, hd2))
```

```python
import functools

import jax
import jax.numpy as jnp
import numpy as np
from jax import lax
from jax.experimental import pallas as pl
from jax.experimental.pallas import tpu as pltpu
from jax.experimental.pallas import tpu_sc as plsc

F32 = jnp.float32
BF16 = jnp.bfloat16
NEG_INF = float("-inf")

CHUNK = 64
WIN_CHUNKS = 2
A_HEADS, A_KV_HEADS, A_GROUP, A_HD = 16, 4, 4, 64
B_HEADS, B_HD = 8, 64
N_GROUPS, EXPERTS_PER_GROUP = 4, 8
N_EXPERTS = N_GROUPS * EXPERTS_PER_GROUP
NORM_EPS = 1e-6
SUBLN_EPS = 1e-5

LANES = 128
COL_CHUNK = 512
HEAD_GROUP = 2
ONES_ROWS = 16
SC_WINDOW = 128
SC_ROW_WORDS = 256
VMEM_LIMIT = 48 * 1024 * 1024


def _cparams(sem, flags=None):
    return pltpu.CompilerParams(dimension_semantics=sem, vmem_limit_bytes=VMEM_LIMIT, flags=flags)


def _alibi(n):
    return 2.0 ** (-8.0 * np.arange(1, n + 1) / n)


def _store_planes(ref, x):
    w = ref.shape[2]
    for p in range(ref.shape[0]):
        ref[p] = x[:, p * w:(p + 1) * w]


def _join_planes(ref, first, count):
    return jnp.concatenate([ref[first + p] for p in range(count)], axis=1)


def _add_expert_outputs(x, og_ref, meta_ref):
    half = og_ref.shape[1] // 2
    meta = meta_ref[...]
    return (x + meta[:, 2:3] * _unpack_pairs(og_ref[:, :half])) + meta[:, 3:4] * _unpack_pairs(og_ref[:, half:])


def _norm_proj_kernel(n_prompt_blocks, two_src, has_bias, combine, outs, *refs):
    refs = list(refs)
    xa_ref = refs.pop(0)
    xb_ref = refs.pop(0) if two_src else None
    og_ref, meta_ref = (refs.pop(0), refs.pop(0)) if combine else (None, None)
    g_ref = refs.pop(0)
    w_ref = refs.pop(0)
    b_ref = refs.pop(0) if has_bias else None
    out_refs = refs
    x = xa_ref[...]
    if two_src:
        x = jnp.where(pl.program_id(0) < n_prompt_blocks, x, xb_ref[...])
    if combine:
        x = _add_expert_outputs(x, og_ref, meta_ref)
        out_refs.pop(0)[...] = x
    ms = jnp.mean(x * x, axis=-1, keepdims=True)
    xn = (x * lax.rsqrt(ms + NORM_EPS) * g_ref[...]).astype(BF16)
    is_prompt = pl.program_id(0) < n_prompt_blocks
    out_refs = list(out_refs)
    targets = []
    for c0, c1, layout in outs:
        targets.append((c0, c1, layout, out_refs.pop(0), out_refs.pop(0) if layout == "split" else None))
    for s in range(0, w_ref.shape[1], COL_CHUNK):
        e = s + COL_CHUNK
        r = jnp.dot(xn, w_ref[:, s:e], preferred_element_type=F32)
        if has_bias:
            r = r + b_ref[:, s:e]
        for c0, c1, layout, o_ref, o2_ref in targets:
            if not (c0 <= s and e <= c1):
                continue
            rr = r.astype(o_ref.dtype)
            if layout == "flat":
                o_ref[:, s - c0:e - c0] = rr
            elif layout == "split":

                @pl.when(is_prompt)
                def _(rr=rr, o_ref=o_ref, s=s, e=e, c0=c0):
                    o_ref[:, s - c0:e - c0] = rr

                @pl.when(jnp.logical_not(is_prompt))
                def _(rr=rr, o2_ref=o2_ref, s=s, e=e, c0=c0):
                    o2_ref[:, s - c0:e - c0] = rr
            elif layout == "heads":
                for hh in range((e - s) // LANES):
                    o_ref[(s - c0) // LANES + hh] = rr[:, hh * LANES:(hh + 1) * LANES]
            else:
                ones = jnp.ones((ONES_ROWS, r.shape[0]), o_ref.dtype)
                for hh in range((e - s) // LANES):
                    head = (s - c0) // LANES + hh
                    o_ref[head, :LANES, :] = r[:, hh * LANES:(hh + 1) * LANES].T.astype(o_ref.dtype)
                    o_ref[head, LANES:, :] = ones


def norm_proj(xa, xb, g, w, b, outs, tr, n_prompt, og=None, meta=None):
    na, d = xa.shape
    nb = 0 if xb is None else xb.shape[0]
    n = na + nb
    npb = n_prompt // tr
    two = xb is not None
    combine = og is not None
    in_specs = [pl.BlockSpec((tr, d), (lambda i: (jnp.minimum(i, npb - 1), 0)) if two else (lambda i: (i, 0)))]
    args = [xa]
    if two:
        in_specs.append(pl.BlockSpec((tr, d), lambda i: (jnp.maximum(i - npb, 0), 0)))
        args.append(xb)
    if combine:
        in_specs += [pl.BlockSpec((tr, og.shape[1]), lambda i: (i, 0)), pl.BlockSpec((tr, LANES), lambda i: (i, 0))]
        args += [og, meta]
    in_specs.append(pl.BlockSpec((1, d), lambda i: (0, 0)))
    args.append(g.reshape(1, d).astype(F32))
    in_specs.append(pl.BlockSpec(w.shape, lambda i: (0, 0)))
    args.append(w)
    if b is not None:
        in_specs.append(pl.BlockSpec((1, w.shape[1]), lambda i: (0, 0)))
        args.append(b.reshape(1, -1).astype(F32))
    out_shapes, out_specs = [], []
    if combine:
        out_shapes.append(jax.ShapeDtypeStruct((n, d), F32))
        out_specs.append(pl.BlockSpec((tr, d), lambda i: (i, 0)))
    for c0, c1, layout, dt in outs:
        assert c0 % COL_CHUNK == 0 and c1 % COL_CHUNK == 0
        if layout == "flat":
            out_shapes.append(jax.ShapeDtypeStruct((n, c1 - c0), dt))
            out_specs.append(pl.BlockSpec((tr, c1 - c0), lambda i: (i, 0)))
        elif layout == "split":
            out_shapes.append(jax.ShapeDtypeStruct((n_prompt, c1 - c0), dt))
            out_specs.append(pl.BlockSpec((tr, c1 - c0), lambda i: (jnp.minimum(i, npb - 1), 0)))
            out_shapes.append(jax.ShapeDtypeStruct((n - n_prompt, c1 - c0), dt))
            out_specs.append(pl.BlockSpec((tr, c1 - c0), lambda i: (jnp.maximum(i - npb, 0), 0)))
        elif layout == "heads":
            nh = (c1 - c0) // LANES
            out_shapes.append(jax.ShapeDtypeStruct((nh, n, LANES), dt))
            out_specs.append(pl.BlockSpec((nh, tr, LANES), lambda i: (0, i, 0)))
        else:
            assert layout == "heads_t"
            nh = (c1 - c0) // LANES
            out_shapes.append(jax.ShapeDtypeStruct((nh, LANES + ONES_ROWS, n), dt))
            out_specs.append(pl.BlockSpec((nh, LANES + ONES_ROWS, tr), lambda i: (0, 0, i)))
    kern = functools.partial(_norm_proj_kernel, npb, two, b is not None, combine,
                             [(c0, c1, lay) for c0, c1, lay, _ in outs])
    return pl.pallas_call(
        kern, grid=(n // tr,), in_specs=in_specs, out_specs=out_specs, out_shape=out_shapes,
        compiler_params=_cparams(("arbitrary",)), name="norm_proj")(*args)


def _attn_a_chunk(q, k, v, bias_ref, sink_ref, valid):
    nq = q.shape[0]
    outs = []
    for kh in range(A_KV_HEADS):
        qs = jnp.concatenate(
            [q[:, (kh * A_GROUP + g) * A_HD:(kh * A_GROUP + g + 1) * A_HD] for g in range(A_GROUP)], axis=0)
        kk = k[:, kh * A_HD:(kh + 1) * A_HD]
        s = lax.dot_general(qs, kk, (((1,), (1,)), ((), ())), preferred_element_type=F32) + bias_ref[kh]
        if valid is not None:
            s = jnp.where(valid, s, NEG_INF)
        sk = sink_ref[kh]
        m = jnp.maximum(jnp.max(s, axis=-1, keepdims=True), sk)
        e = jnp.exp2(s - m)
        den = jnp.sum(e, axis=-1, keepdims=True) + jnp.exp2(sk - m)
        p = (e * (1.0 / den)).astype(BF16)
        o = jnp.dot(p, v[:, kh * A_HD:(kh + 1) * A_HD], preferred_element_type=F32)
        for g in range(A_GROUP):
            outs.append(o[g * nq:(g + 1) * nq])
    return jnp.concatenate(outs, axis=1)


def _attn_a_prompt_kernel(qb, q_ref, kvc_ref, kvp_ref, bias_ref, sink_ref, o_ref, t_ref, p_ref):
    cb = pl.program_id(1)
    back = CHUNK * WIN_CHUNKS
    nkv = A_KV_HEADS * A_HD
    kvfull = jnp.concatenate([kvp_ref[...], kvc_ref[...]], axis=0)
    k_all = kvfull[:, :nkv].astype(BF16)
    vt_all = kvfull[:, nkv:].T.astype(BF16)
    key_pos = lax.broadcasted_iota(jnp.int32, (back + qb, 1), 0) + (cb * qb - back)
    before_start = jnp.where(key_pos >= 0, 0.0, NEG_INF)
    dn = (((1,), (1,)), ((), ()))
    for kh in range(A_KV_HEADS):
        qs = jnp.concatenate(
            [q_ref[:, (kh * A_GROUP + g) * A_HD:(kh * A_GROUP + g + 1) * A_HD] for g in range(A_GROUP)], axis=0)
        t_ref[kh] = lax.dot_general(k_all[:, kh * A_HD:(kh + 1) * A_HD], qs, dn, preferred_element_type=F32)
    inv_dens = []
    for kh in range(A_KV_HEADS):
        st = t_ref[kh] + bias_ref[kh] + before_start
        sk = sink_ref[kh]
        m = jnp.maximum(jnp.max(st, axis=0, keepdims=True), sk)
        e = jnp.exp2(st - m)
        inv_dens.append(1.0 / (jnp.sum(e, axis=0, keepdims=True) + jnp.exp2(sk - m)))
        p_ref[kh] = e.astype(BF16)
    outs = []
    for kh in range(A_KV_HEADS):
        ot = jnp.dot(vt_all[kh * A_HD:(kh + 1) * A_HD, :], p_ref[kh], preferred_element_type=F32)
        o = (ot * inv_dens[kh]).T
        for g in range(A_GROUP):
            outs.append(o[g * qb:(g + 1) * qb])
    o_ref[...] = jnp.concatenate(outs, axis=1).astype(o_ref.dtype)


def _a_prompt_tables(qb, sinks):
    back = CHUNK * WIN_CHUNKS
    slopes = _alibi(A_HEADS).reshape(A_KV_HEADS, A_GROUP)
    qpos = np.arange(qb)
    kpos = np.arange(back + qb) - back
    dist = np.abs(kpos[:, None] - qpos[None, :]).astype(np.float64)
    band_lo = (qpos // CHUNK) * CHUNK - back
    in_band = (kpos[:, None] >= band_lo[None, :]) & (kpos[:, None] < band_lo[None, :] + back + CHUNK)
    bias = np.where(in_band[None, None], -slopes[:, :, None, None] * dist[None, None], -np.inf)
    bias = np.transpose(bias, (0, 2, 1, 3)).reshape(A_KV_HEADS, back + qb, A_GROUP * qb) * LOG2E
    sk = jnp.repeat(sinks.astype(F32).reshape(A_KV_HEADS, A_GROUP), qb, axis=1)[:, None, :] * LOG2E
    return jnp.asarray(bias, F32), sk


def _a_tables(qpos, kpos, sinks):
    slopes = _alibi(A_HEADS).reshape(A_KV_HEADS, A_GROUP)
    dist = np.abs(qpos[:, None] - kpos[None, :]).astype(np.float64)
    bias = -slopes[:, :, None, None] * dist
    nq = len(qpos)
    bias = jnp.asarray(bias.reshape(A_KV_HEADS, A_GROUP * nq, len(kpos)) * LOG2E, F32)
    sk = jnp.repeat(sinks.astype(F32).reshape(A_KV_HEADS, A_GROUP), nq, axis=1)[..., None] * LOG2E
    return bias, sk


def attn_a_prompt(q_all, kv_all, sinks, batch, t, qb):
    nq = A_HEADS * A_HD
    nkv2 = 2 * A_KV_HEADS * A_HD
    back = CHUNK * WIN_CHUNKS
    bias, sk = _a_prompt_tables(qb, sinks)
    nblk = t // qb
    r = qb // back
    return pl.pallas_call(
        functools.partial(_attn_a_prompt_kernel, qb),
        grid=(batch, nblk),
        in_specs=[
            pl.BlockSpec((qb, nq), lambda b, i: (b * nblk + i, 0)),
            pl.BlockSpec((qb, nkv2), lambda b, i: (b * nblk + i, 0)),
            pl.BlockSpec((back, nkv2), lambda b, i: (jnp.maximum((b * nblk + i) * r - 1, 0), 0)),
            pl.BlockSpec(bias.shape, lambda b, i: (0, 0, 0)),
            pl.BlockSpec(sk.shape, lambda b, i: (0, 0, 0)),
        ],
        out_specs=pl.BlockSpec((qb, nq), lambda b, i: (b * nblk + i, 0)),
        out_shape=jax.ShapeDtypeStruct((batch * t, nq), BF16),
        scratch_shapes=[pltpu.VMEM((A_KV_HEADS, back + qb, A_GROUP * qb), F32),
                        pltpu.VMEM((A_KV_HEADS, back + qb, A_GROUP * qb), BF16)],
        compiler_params=_cparams(("arbitrary", "arbitrary")), name="attn_a_prompt",
    )(q_all, kv_all, kv_all, bias, sk)


def _attn_a_sample_kernel(q_ref, k_ref, v_ref, bias_ref, sink_ref, o_ref):
    o = _attn_a_chunk(q_ref[...], k_ref[...].astype(BF16), v_ref[...].astype(BF16), bias_ref, sink_ref, None)
    o_ref[...] = o.astype(o_ref.dtype)


def attn_a_sample(q_all, kband, vband, sinks, row0, dec_batch, dec_t):
    nq = A_HEADS * A_HD
    s = kband.shape[1]
    past = s - dec_t
    kpos = np.concatenate([np.arange(past) - past, np.arange(dec_t)])
    bias, sk = _a_tables(np.arange(dec_t), kpos, sinks)
    blk0 = row0 // dec_t
    return pl.pallas_call(
        _attn_a_sample_kernel,
        grid=(dec_batch,),
        in_specs=[
            pl.BlockSpec((dec_t, nq), lambda b: (blk0 + b, 0)),
            pl.BlockSpec((None, s, kband.shape[2]), lambda b: (b, 0, 0)),
            pl.BlockSpec((None, s, vband.shape[2]), lambda b: (b, 0, 0)),
            pl.BlockSpec(bias.shape, lambda b: (0, 0, 0)),
            pl.BlockSpec(sk.shape, lambda b: (0, 0, 0)),
        ],
        out_specs=pl.BlockSpec((dec_t, nq), lambda b: (b, 0)),
        out_shape=jax.ShapeDtypeStruct((dec_batch * dec_t, nq), BF16),
        compiler_params=_cparams(("arbitrary",)), name="attn_a_sample",
    )(q_all, kband, vband, bias, sk)


B_Q_SCALE = B_HD ** -0.5 * float(np.log2(np.e))
LOG2E = float(np.log2(np.e))


def _stack_maps(qh):
    lane = lax.broadcasted_iota(jnp.int32, qh.shape, 1)
    zero = jnp.zeros_like(qh)
    return jnp.concatenate([jnp.where(lane < B_HD, qh, zero), jnp.where(lane >= B_HD, qh, zero)], axis=0)


def _diff_finish(acc, l, nq, lam, lam_init, subln):
    inv = 1.0 / l
    o = acc[:nq] * inv[:nq] - lam * (acc[nq:] * inv[nq:])
    ms = jnp.mean(o * o, axis=-1, keepdims=True)
    return o * lax.rsqrt(ms + SUBLN_EPS) * subln * (1.0 - lam_init)


def _attn_b_prompt_kernel(qb, nb, lam_init, pair_ref, slope_ref, lam_ref, q_ref, k_ref, vt_ref, kbias_ref,
                          subln_ref, o_ref, acc_ref, m_ref, al_ref, qq_ref, t0_ref, t1_ref, p0_ref, p1_ref):
    g = pl.program_id(1)
    lam = lam_ref[0]
    dn = (((1,), (1,)), ((), ()))
    t_refs = (t0_ref, t1_ref)
    p_refs = (p0_ref, p1_ref)
    heads = range(HEAD_GROUP)
    n_pairs = nb * (nb + 1) // 2
    slopes = [slope_ref[g * HEAD_GROUP + u] for u in heads]

    def pair(n):
        return pair_ref[2 * n], pair_ref[2 * n + 1]

    def scores(n, slot):
        j, iq = pair(n)
        ks = pl.multiple_of(j * qb, qb)
        for u in heads:
            t_refs[slot][u] = lax.dot_general(k_ref[u, pl.ds(ks, qb), :], qq_ref[u * nb + iq], dn,
                                              preferred_element_type=F32)

    def softmax_step(n, slot):
        j, iq = pair(n)
        on_diag = (j == iq).astype(jnp.int32)
        for u in heads:
            st = u * nb + iq
            m = m_ref[st, 0:1, :]
            t = t_refs[slot][u] + kbias_ref[u, on_diag]
            off = slopes[u] * jnp.full((1, 2 * qb), (j - iq) * qb, jnp.int32).astype(F32)
            m_new = jnp.maximum(m, jnp.max(t, axis=0, keepdims=True) + off)
            p_refs[slot][u] = jnp.exp2(t + (off - m_new)).astype(BF16)
            al_ref[slot * HEAD_GROUP + u, 0:1, :] = jnp.exp2(m - m_new)
            m_ref[st, 0:1, :] = m_new

    def accumulate(n, slot):
        j, iq = pair(jnp.maximum(n, 0))
        ks = pl.multiple_of(j * qb, qb)
        for u in heads:
            st = u * nb + iq
            pv = jnp.dot(vt_ref[u, :, pl.ds(ks, qb)], p_refs[slot][u], preferred_element_type=F32)
            acc_ref[st] = al_ref[slot * HEAD_GROUP + u, 0:1, :] * acc_ref[st] + pv

    def step(n, slot):
        scores(jnp.minimum(n + 1, n_pairs - 1), 1 - slot)
        softmax_step(n, slot)
        accumulate(n - 1, 1 - slot)
        return 0

    acc_ref[...] = jnp.zeros_like(acc_ref)
    m_ref[...] = jnp.full(m_ref.shape, NEG_INF, F32)
    al_ref[...] = jnp.ones_like(al_ref)
    p1_ref[...] = jnp.zeros_like(p1_ref)
    for u in heads:
        for iq in range(nb):
            qq_ref[u * nb + iq] = _stack_maps(q_ref[u, iq * qb:(iq + 1) * qb, :])
    scores(0, 0)
    lax.fori_loop(0, n_pairs, lambda n, c: lax.cond(n % 2 == 0, lambda: step(n, 0), lambda: step(n, 1)), 0)
    accumulate(n_pairs - 1, (n_pairs - 1) % 2)
    for u in heads:
        for iq in range(nb):
            inv = 1.0 / acc_ref[u * nb + iq, LANES:LANES + 1, :]
            acc = acc_ref[u * nb + iq, :LANES, :]
            ot = acc[:, :qb] * inv[:, :qb] - lam * (acc[:, qb:] * inv[:, qb:])
            ms = jnp.mean(ot * ot, axis=0, keepdims=True)
            ot = ot * lax.rsqrt(ms + SUBLN_EPS) * subln_ref[...] * (1.0 - lam_init)
            o_ref[u, iq * qb:(iq + 1) * qb, :] = ot.T.astype(o_ref.dtype)


def attn_b_prompt(qkvh, vt, lam, subln, lam_init, batch, t, qb):
    nblk = t // qb
    qq = np.arange(qb)
    slope2 = (_alibi(B_HEADS) * LOG2E)[:, None, None]
    kcol = np.broadcast_to(qq[:, None], (qb, qb)).astype(np.float64)
    tri = -2.0 * np.maximum(qq[:, None] - qq[None, :], 0)
    msk = np.where((qq[:, None] // CHUNK) <= (qq[None, :] // CHUNK), 0.0, -np.inf)
    both = lambda a: np.concatenate([a, a], axis=-1)
    kbias = np.stack([both(slope2 * kcol[None]), both(slope2 * (kcol + tri)[None] + msk[None])], axis=1)
    kbias = jnp.asarray(kbias, F32)
    slopes = jnp.asarray(_alibi(B_HEADS) * LOG2E, F32)
    subln_t = jnp.broadcast_to(subln.astype(F32).reshape(LANES, 1), (LANES, qb))
    pairs = jnp.asarray([v for j in range(nblk) for i in range(j, nblk) for v in (j, i)], jnp.int32)
    hg = HEAD_GROUP
    n_groups = B_HEADS // hg
    grid_spec = pltpu.PrefetchScalarGridSpec(
        num_scalar_prefetch=3, grid=(batch, n_groups),
        in_specs=[
            pl.BlockSpec((hg, t, LANES), lambda b, g, *_: (g, b, 0)),
            pl.BlockSpec((hg, t, LANES), lambda b, g, *_: (n_groups + g, b, 0)),
            pl.BlockSpec((hg, LANES + ONES_ROWS, t), lambda b, g, *_: (g, 0, b)),
            pl.BlockSpec((hg,) + kbias.shape[1:], lambda b, g, *_: (g, 0, 0, 0)),
            pl.BlockSpec(subln_t.shape, lambda b, g, *_: (0, 0)),
        ],
        out_specs=pl.BlockSpec((hg, t, LANES), lambda b, g, *_: (g, b, 0)),
        scratch_shapes=[pltpu.VMEM((hg * nblk, LANES + ONES_ROWS, 2 * qb), F32),
                        pltpu.VMEM((hg * nblk, 8, 2 * qb), F32),
                        pltpu.VMEM((2 * hg, 8, 2 * qb), F32),
                        pltpu.VMEM((hg * nblk, 2 * qb, LANES), BF16),
                        pltpu.VMEM((hg, qb, 2 * qb), F32), pltpu.VMEM((hg, qb, 2 * qb), F32),
                        pltpu.VMEM((hg, qb, 2 * qb), BF16), pltpu.VMEM((hg, qb, 2 * qb), BF16)],
    )
    return pl.pallas_call(
        functools.partial(_attn_b_prompt_kernel, qb, nblk, lam_init), grid_spec=grid_spec,
        out_shape=jax.ShapeDtypeStruct((B_HEADS, batch * t, LANES), BF16),
        compiler_params=_cparams(("arbitrary", "arbitrary")), name="attn_b_prompt",
    )(pairs, slopes, lam.reshape(1).astype(F32), qkvh, qkvh, vt, kbias, subln_t)


def _attn_b_sample_kernel(dec_t, past, lam_init, slopes, lam_ref, q_ref, kn_ref, vn_ref, ck_ref, cv_ref,
                          tnew_ref, subln_ref, o_ref):
    lam = lam_ref[0]
    subln = subln_ref[...]
    colc = lax.broadcasted_iota(jnp.int32, (1, past), 1).astype(F32) - float(past)
    for h in range(B_HEADS):
        slope = float(slopes[h]) * LOG2E
        qq = _stack_maps(q_ref[h])
        kc = ck_ref[pl.ds(h, past, stride=B_HEADS), :].astype(BF16)
        vc = cv_ref[pl.ds(h, past, stride=B_HEADS), :].astype(BF16)
        dn = (((1,), (1,)), ((), ()))
        s_c = lax.dot_general(qq, kc, dn, preferred_element_type=F32) + slope * colc
        s_n = lax.dot_general(qq, kn_ref[h], dn, preferred_element_type=F32) + slope * tnew_ref[...]
        m = jnp.maximum(jnp.max(s_c, axis=-1, keepdims=True), jnp.max(s_n, axis=-1, keepdims=True))
        p_c = jnp.exp2(s_c - m)
        p_n = jnp.exp2(s_n - m)
        l = jnp.sum(p_c, axis=-1, keepdims=True) + jnp.sum(p_n, axis=-1, keepdims=True)
        acc = (jnp.dot(p_c.astype(BF16), vc, preferred_element_type=F32)
               + jnp.dot(p_n.astype(BF16), vn_ref[h], preferred_element_type=F32))
        o_ref[h] = _diff_finish(acc, l, dec_t, lam, lam_init, subln).astype(o_ref.dtype)


def attn_b_sample(qkvh, cache_k, cache_v, lam, subln, lam_init, row0, dec_batch, dec_t):
    past = cache_k.shape[1]
    qi = np.arange(dec_t)
    tnew = (qi[:, None] - np.abs(qi[:, None] - qi[None, :])).astype(np.float32)
    tnew = jnp.asarray(np.concatenate([tnew, tnew], axis=0))
    blk0 = row0 // dec_t
    rows = past * B_HEADS
    cache_k = cache_k.reshape(dec_batch, rows, LANES)
    cache_v = cache_v.reshape(dec_batch, rows, LANES)
    grid_spec = pltpu.PrefetchScalarGridSpec(
        num_scalar_prefetch=1, grid=(dec_batch,),
        in_specs=[
            pl.BlockSpec((B_HEADS, dec_t, LANES), lambda b, *_: (0, blk0 + b, 0)),
            pl.BlockSpec((B_HEADS, dec_t, LANES), lambda b, *_: (1, blk0 + b, 0)),
            pl.BlockSpec((B_HEADS, dec_t, LANES), lambda b, *_: (2, blk0 + b, 0)),
            pl.BlockSpec((None, rows, LANES), lambda b, *_: (b, 0, 0)),
            pl.BlockSpec((None, rows, LANES), lambda b, *_: (b, 0, 0)),
            pl.BlockSpec(tnew.shape, lambda b, *_: (0, 0)),
            pl.BlockSpec((1, LANES), lambda b, *_: (0, 0)),
        ],
        out_specs=pl.BlockSpec((B_HEADS, dec_t, LANES), lambda b, *_: (0, b, 0)),
    )
    return pl.pallas_call(
        functools.partial(_attn_b_sample_kernel, dec_t, past, lam_init, tuple(_alibi(B_HEADS))),
        grid_spec=grid_spec,
        out_shape=jax.ShapeDtypeStruct((B_HEADS, dec_batch * dec_t, LANES), BF16),
        compiler_params=_cparams(("arbitrary",)), name="attn_b_sample",
    )(lam.reshape(1).astype(F32), qkvh, qkvh, qkvh, cache_k, cache_v, tnew, subln.reshape(1, LANES).astype(F32))


def _route(logits):
    lane = lax.broadcasted_iota(jnp.int32, logits.shape, 1).astype(F32)
    big = float(LANES)
    lg = jnp.where(lane < N_GROUPS, logits, NEG_INF)
    gmax = jnp.max(lg, axis=-1, keepdims=True)
    g_idx = jnp.min(jnp.where(lg == gmax, lane, big), axis=-1, keepdims=True)
    g_prob = 1.0 / jnp.sum(jnp.exp(lg - gmax), axis=-1, keepdims=True)
    lo = N_GROUPS + EXPERTS_PER_GROUP * g_idx
    le = jnp.where((lane >= lo) & (lane < lo + EXPERTS_PER_GROUP), logits, NEG_INF)
    m1 = jnp.max(le, axis=-1, keepdims=True)
    i1 = jnp.min(jnp.where(le == m1, lane, big), axis=-1, keepdims=True)
    le2 = jnp.where(lane == i1, NEG_INF, le)
    m2 = jnp.max(le2, axis=-1, keepdims=True)
    i2 = jnp.min(jnp.where(le2 == m2, lane, big), axis=-1, keepdims=True)
    t = jnp.exp(m2 - m1)
    ga = 1.0 / (1.0 + t)
    gb = t * ga
    return lane, i1, i2, g_prob * ga, g_prob * gb


def _unpack_pairs(u):
    lo = pltpu.bitcast(u << 16, F32)
    hi = pltpu.bitcast(u & jnp.uint32(0xFFFF0000), F32)
    return jnp.concatenate([lo, hi], axis=1)


def _pack_pairs(x):
    bits = pltpu.bitcast(x.astype(BF16).astype(F32), jnp.uint32)
    w = x.shape[1] // 2
    return (bits[:, :w] >> 16) | bits[:, w:]


def _out_router_kernel(n_prompt_blocks, heads_in, two_o, two_res, *refs):
    refs = list(refs)
    oa_ref = refs.pop(0)
    ob_ref = refs.pop(0) if two_o else None
    ra_ref = refs.pop(0)
    rb_ref = refs.pop(0) if two_res else None
    wo_ref, g_ref, wr_ref, wr2_ref, br_ref, ltri_ref, h_ref, xn_ref, meta_ref, cnt_ref, carry_ref = refs
    is_prompt = pl.program_id(0) < n_prompt_blocks

    def load_o(ref):
        if heads_in:
            return jnp.concatenate([ref[hh] for hh in range(ref.shape[0])], axis=1)
        return ref[...]

    o = load_o(oa_ref)
    if two_o:
        o = jnp.where(is_prompt, o, load_o(ob_ref))
    res = ra_ref[...]
    if two_res:
        res = jnp.where(is_prompt, res, rb_ref[...])
    h = res + jnp.dot(o, wo_ref[...], preferred_element_type=F32)
    h_ref[...] = h
    ms = jnp.mean(h * h, axis=-1, keepdims=True)
    xn = h * lax.rsqrt(ms + NORM_EPS) * g_ref[...]
    x_hi = xn.astype(BF16)
    _store_planes(xn_ref, _pack_pairs(xn))
    x_lo = (xn - x_hi.astype(F32)).astype(BF16)
    r = jnp.dot(x_hi, wr_ref[...], preferred_element_type=F32)
    logits = (r[:, :LANES] + r[:, LANES:]) + jnp.dot(x_lo, wr2_ref[...], preferred_element_type=F32) + br_ref[...]
    lane, i1, i2, g1, g2 = _route(logits)

    @pl.when(pl.program_id(0) == 0)
    def _():
        carry_ref[...] = jnp.zeros_like(carry_ref)

    sel1 = lane == i1
    sel2 = lane == i2
    onehot = jnp.where(sel1 | sel2, 1.0, 0.0)
    before = jnp.dot(ltri_ref[...], onehot.astype(BF16), preferred_element_type=F32) + carry_ref[...]
    rank1 = jnp.sum(jnp.where(sel1, before, 0.0), axis=-1, keepdims=True)
    rank2 = jnp.sum(jnp.where(sel2, before, 0.0), axis=-1, keepdims=True)
    carry_ref[...] = carry_ref[...] + jnp.sum(onehot, axis=0, keepdims=True)
    cnt_ref[...] = carry_ref[...]
    cols = [i1 - N_GROUPS, i2 - N_GROUPS, g1, g2, rank1, rank2]
    meta = jnp.zeros_like(logits)
    for c, v in enumerate(cols):
        meta = jnp.where(lane == c, v, meta)
    meta_ref[...] = meta


def out_router(oa, ob, ra, rb, wo, g, w_rg, b_rg, w_re, b_re, heads_in, tr):
    d = ra.shape[1]
    if heads_in:
        na = oa.shape[1]
        nb = 0 if ob is None else ob.shape[1]
    else:
        na = oa.shape[0]
        nb = 0 if ob is None else ob.shape[0]
    n = na + nb
    npb = na // tr
    two_o = ob is not None
    two_res = rb is not None
    first = lambda i: (jnp.minimum(i, npb - 1), 0)
    second = lambda i: (jnp.maximum(i - npb, 0), 0)
    plain = lambda i: (i, 0)
    in_specs, args = [], []

    def add_o(x, imap):
        if heads_in:
            in_specs.append(pl.BlockSpec((x.shape[0], tr, LANES), lambda i: (0, imap(i)[0], 0)))
        else:
            in_specs.append(pl.BlockSpec((tr, x.shape[1]), imap))
        args.append(x)

    add_o(oa, first if two_o else plain)
    if two_o:
        add_o(ob, second)
    in_specs.append(pl.BlockSpec((tr, d), first if two_res else plain))
    args.append(ra)
    if two_res:
        in_specs.append(pl.BlockSpec((tr, d), second))
        args.append(rb)
    wr = jnp.zeros((d, LANES), F32).at[:, :N_GROUPS].set(w_rg.astype(F32))
    wr = wr.at[:, N_GROUPS:N_GROUPS + N_EXPERTS].set(w_re.astype(F32))
    wr_hi = wr.astype(BF16)
    wr_lo = (wr - wr_hi.astype(F32)).astype(BF16)
    br = jnp.zeros((1, LANES), F32).at[0, :N_GROUPS].set(b_rg.astype(F32))
    br = br.at[0, N_GROUPS:N_GROUPS + N_EXPERTS].set(b_re.astype(F32))
    ltri = jnp.asarray(np.tril(np.ones((tr, tr), np.float32), -1), BF16)
    consts = [wo, g.reshape(1, d).astype(F32), jnp.concatenate([wr_hi, wr_lo], axis=1), wr_hi, br, ltri]
    for cst in consts:
        in_specs.append(pl.BlockSpec(cst.shape, lambda i: (0, 0)))
        args.append(cst)
    parts = d // 2 // SC_ROW_WORDS
    out_shape = [jax.ShapeDtypeStruct((n, d), F32), jax.ShapeDtypeStruct((parts, n, SC_ROW_WORDS), jnp.uint32),
                 jax.ShapeDtypeStruct((n, LANES), F32), jax.ShapeDtypeStruct((1, LANES), F32)]
    out_specs = [pl.BlockSpec((tr, d), plain), pl.BlockSpec((parts, tr, SC_ROW_WORDS), lambda i: (0, i, 0)),
                 pl.BlockSpec((tr, LANES), plain), pl.BlockSpec((1, LANES), lambda i: (0, 0))]
    return pl.pallas_call(
        functools.partial(_out_router_kernel, npb, heads_in, two_o, two_res),
        grid=(n // tr,), in_specs=in_specs, out_specs=out_specs, out_shape=out_shape,
        scratch_shapes=[pltpu.VMEM((1, LANES), F32)],
        compiler_params=_cparams(("arbitrary",)), name="out_router")(*args)


def _expert_kernel(te_ref, nt_ref, x_ref, wg_ref, wu_ref, wd_ref, o_ref):
    @pl.when(pl.program_id(0) < nt_ref[0])
    def _():
        x = _unpack_pairs(_join_planes(x_ref, 0, x_ref.shape[0])).astype(BF16)
        a = jnp.dot(x, wg_ref[...].astype(BF16), preferred_element_type=F32)
        u = jnp.dot(x, wu_ref[...].astype(BF16), preferred_element_type=F32)
        hh = (a * (1.0 / (1.0 + jnp.exp(-a))) * u).astype(BF16)
        _store_planes(o_ref, _pack_pairs(jnp.dot(hh, wd_ref[...].astype(BF16), preferred_element_type=F32)))

    @pl.when(pl.program_id(0) >= nt_ref[0])
    def _():
        o_ref[...] = jnp.zeros_like(o_ref)


def expert_mlp(xs, tile_expert, n_tiles, w_gate, w_up, w_down, layer, tm):
    parts, p, w = xs.shape
    d, f = w_gate.shape[-2:]
    epg = w_gate.shape[2]
    wmap = lambda t, te, nt: (layer, te[t] // epg, te[t] % epg, 0, 0)
    grid_spec = pltpu.PrefetchScalarGridSpec(
        num_scalar_prefetch=2, grid=(p // tm,),
        in_specs=[
            pl.BlockSpec((parts, tm, w), lambda t, te, nt: (0, t, 0)),
            pl.BlockSpec((None, None, None, d, f), wmap),
            pl.BlockSpec((None, None, None, d, f), wmap),
            pl.BlockSpec((None, None, None, f, d), wmap),
        ],
        out_specs=pl.BlockSpec((parts, tm, w), lambda t, te, nt: (0, t, 0)),
    )
    return pl.pallas_call(
        _expert_kernel, grid_spec=grid_spec, out_shape=jax.ShapeDtypeStruct((parts, p, w), jnp.uint32),
        compiler_params=_cparams(("arbitrary",)), name="expert_mlp")(tile_expert, n_tiles, xs, w_gate, w_up, w_down)


def sc_scatter_rows(x2, idx, n_out):
    mesh = plsc.VectorSubcoreMesh(core_axis_name="c", subcore_axis_name="s")
    n_src_win = x2.shape[0] // SC_WINDOW

    @functools.partial(pl.kernel, out_type=jax.ShapeDtypeStruct((n_out, SC_ROW_WORDS), x2.dtype), mesh=mesh)
    def scatter(x_hbm, i_hbm, o_hbm):
        def body(x_vmem, i_vmem):
            pltpu.sync_copy(x_vmem, o_hbm.at[i_vmem.at[0]])

        pltpu.emit_pipeline(
            body, grid=(n_out // SC_WINDOW,),
            in_specs=[pl.BlockSpec((SC_WINDOW, SC_ROW_WORDS), lambda i: (lax.rem(i, n_src_win), 0)),
                      pl.BlockSpec((1, SC_WINDOW), lambda i: (0, i))],
            out_specs=[], core_axis_name=("c", "s"), dimension_semantics=(pltpu.PARALLEL,),
        )(x_hbm, i_hbm)

    return scatter(x2, idx)


def sc_gather_rows(x2, idx):
    mesh = plsc.VectorSubcoreMesh(core_axis_name="c", subcore_axis_name="s")
    n_col, n = idx.shape

    @functools.partial(pl.kernel, out_type=jax.ShapeDtypeStruct((n, n_col * SC_ROW_WORDS), x2.dtype), mesh=mesh)
    def gather(x_hbm, i_hbm, o_hbm):
        def body(i_vmem, o_vmem):
            pltpu.sync_copy(x_hbm.at[i_vmem.at[0]], o_vmem)

        pltpu.emit_pipeline(
            body, grid=(n // SC_WINDOW, n_col),
            in_specs=[pl.BlockSpec((1, SC_WINDOW), lambda i, c: (c, i))],
            out_specs=[pl.BlockSpec((SC_WINDOW, SC_ROW_WORDS), lambda i, c: (i, c))],
            core_axis_name=("c", "s"), dimension_semantics=(pltpu.PARALLEL, pltpu.ARBITRARY),
        )(i_hbm, o_hbm)

    return gather(x2, idx)


def moe_dispatch(xn_packed, meta, counts, tm):
    parts, n, w = xn_packed.shape
    e = meta[:, 0:2].astype(jnp.int32)
    rank = meta[:, 4:6].astype(jnp.int32)
    counts = counts[0, N_GROUPS:N_GROUPS + N_EXPERTS].astype(jnp.int32)
    padded = ((counts + tm - 1) // tm) * tm
    pend = jnp.cumsum(padded)
    pstart = pend - padded
    ids = jnp.arange(N_EXPERTS, dtype=jnp.int32)
    pos = jnp.sum(jnp.where(e[..., None] == ids, pstart, 0), axis=-1) + rank
    p = 2 * n + N_EXPERTS * tm
    n_pad = p - 2 * n
    gap_len = jnp.concatenate([padded - counts, (p - pend[-1]).reshape(1)])
    gap_first = jnp.concatenate([pstart + counts, pend[-1:]])
    gap_end = jnp.cumsum(gap_len)
    k = jnp.arange(n_pad, dtype=jnp.int32)
    gap = jnp.sum((k[:, None] >= gap_end[None, :]).astype(jnp.int32), axis=1)
    shift = gap_first - (gap_end - gap_len)
    pad_pos = k + jnp.sum(jnp.where(gap[:, None] == jnp.arange(N_EXPERTS + 1), shift, 0), axis=-1)
    planes = lambda rows: jnp.concatenate([rows + q * p for q in range(parts)])
    scat_idx = jnp.concatenate([planes(pos[:, 0]), planes(pos[:, 1]), planes(pad_pos)]).reshape(1, parts * p)
    gath_idx = jnp.stack([pos[:, k] + q * p for k in range(2) for q in range(parts)])
    xs = sc_scatter_rows(xn_packed.reshape(parts * n, w), scat_idx, parts * p).reshape(parts, p, w)
    tile_start = jnp.arange(p // tm, dtype=jnp.int32) * tm
    tile_expert = jnp.sum((tile_start[:, None] >= pend[None, :]).astype(jnp.int32), axis=1)
    tile_expert = jnp.minimum(tile_expert, N_EXPERTS - 1)
    n_tiles = (pend[-1] // tm).astype(jnp.int32).reshape(1)
    return xs, tile_expert, n_tiles, gath_idx


def moe_experts(xn_packed, meta, counts, w_gate, w_up, w_down, layer, tm):
    parts, n, w = xn_packed.shape
    xs, tile_expert, n_tiles, gath_idx = moe_dispatch(xn_packed, meta, counts, tm)
    o = expert_mlp(xs, tile_expert, n_tiles, w_gate, w_up, w_down, layer, tm)
    return sc_gather_rows(o.reshape(parts * o.shape[1], w), gath_idx)


def _final_norm_kernel(n_prompt_blocks, h_ref, o0_ref, o1_ref, meta_ref, g_ref, yp_ref, ys_ref):
    meta = meta_ref[...]
    x = (h_ref[...] + meta[:, 2:3] * _unpack_pairs(o0_ref[...])) + meta[:, 3:4] * _unpack_pairs(o1_ref[...])
    ms = jnp.mean(x * x, axis=-1, keepdims=True)
    y = x * lax.rsqrt(ms + NORM_EPS) * g_ref[...]
    i = pl.program_id(0)

    @pl.when(i < n_prompt_blocks)
    def _():
        yp_ref[...] = y

    @pl.when(i >= n_prompt_blocks)
    def _():
        ys_ref[...] = y


def final_norm(h, og, meta, g, n_prompt, tr):
    n, d = h.shape
    npb = n_prompt // tr
    return pl.pallas_call(
        functools.partial(_final_norm_kernel, npb),
        grid=(n // tr,),
        in_specs=[pl.BlockSpec((tr, d), lambda i: (i, 0)),
                  pl.BlockSpec((tr, og.shape[1] // 2), lambda i: (i, 0)),
                  pl.BlockSpec((tr, og.shape[1] // 2), lambda i: (i, 1)),
                  pl.BlockSpec((tr, LANES), lambda i: (i, 0)), pl.BlockSpec((1, d), lambda i: (0, 0))],
        out_specs=[pl.BlockSpec((tr, d), lambda i: (jnp.minimum(i, npb - 1), 0)),
                   pl.BlockSpec((tr, d), lambda i: (jnp.maximum(i - npb, 0), 0))],
        out_shape=[jax.ShapeDtypeStruct((n_prompt, d), F32), jax.ShapeDtypeStruct((n - n_prompt, d), F32)],
        compiler_params=_cparams(("arbitrary",)), name="final_norm")(h, og, og, meta, g.reshape(1, d).astype(F32))


def _pick_rows_tile(n_prompt, n_sample, largest=512):
    for tr in (1024, 512, 256, 128, 64, 32, 16, 8):
        if tr <= largest and n_prompt % tr == 0 and n_sample % tr == 0:
            return tr
    raise ValueError("row counts must be multiples of 8")


def kernel(x_prompt, x_sample, cache_a_k, cache_a_v, cache_b_k, cache_b_v, norm_mix, norm_ffn, norm_final,
           w_a_qkv, b_a_qkv, a_sinks, w_a_o, w_b_qkv, b_lambda, b_subln, w_b_o,
           w_route_group, b_route_group, w_route_expert, b_route_expert, w_gate, w_up, w_down):
    batch, t, d = x_prompt.shape
    dec_batch, dec_t, _ = x_sample.shape
    n_p, n_s = batch * t, dec_batch * dec_t
    tr = _pick_rows_tile(n_p, n_s)
    tm = min(512, tr * 4)
    xp = x_prompt.reshape(n_p, d)
    xs = x_sample.reshape(n_s, d)
    nq_a = A_HEADS * A_HD
    nkv_a = A_KV_HEADS * A_HD

    a_scale = jnp.where(jnp.arange(nq_a + 2 * nkv_a) < nq_a, A_HD ** -0.5 * LOG2E, 1.0).astype(F32)
    q_a, kv_ap, kv_as = norm_proj(xp, xs, norm_mix[0], (w_a_qkv[0] * a_scale).astype(BF16), b_a_qkv[0] * a_scale,
                                  [(0, nq_a, "flat", BF16), (nq_a, nq_a + 2 * nkv_a, "split", F32)], tr, n_p)
    qb_a = min(256, t)
    o_ap = attn_a_prompt(q_a, kv_ap, a_sinks[0], batch, t, qb_a)
    kv_s = kv_as.reshape(dec_batch, dec_t, 2 * nkv_a)
    past_a = cache_a_k.shape[2]
    kband = jnp.concatenate([cache_a_k[0].reshape(dec_batch, past_a, nkv_a), kv_s[..., :nkv_a]], axis=1)
    vband = jnp.concatenate([cache_a_v[0].reshape(dec_batch, past_a, nkv_a), kv_s[..., nkv_a:]], axis=1)
    o_as = attn_a_sample(q_a, kband, vband, a_sinks[0], n_p, dec_batch, dec_t)
    h, xn, meta, counts = out_router(o_ap, o_as, xp, xs, w_a_o[0].astype(BF16), norm_ffn[0], w_route_group[0],
                                     b_route_group[0], w_route_expert[0], b_route_expert[0], False, tr)
    og = moe_experts(xn, meta, counts, w_gate, w_up, w_down, 0, tm)

    keep = min(CHUNK * WIN_CHUNKS, t)
    kv_p = kv_ap.reshape(batch, t, 2 * nkv_a)[:, t - keep:]
    a_k_prompt = kv_p[..., :nkv_a].reshape(1, batch, keep, A_KV_HEADS, A_HD)
    a_v_prompt = kv_p[..., nkv_a:].reshape(1, batch, keep, A_KV_HEADS, A_HD)
    a_k_sample = kband[:, dec_t:].reshape(1, dec_batch, past_a, A_KV_HEADS, A_HD)
    a_v_sample = vband[:, dec_t:].reshape(1, dec_batch, past_a, A_KV_HEADS, A_HD)

    lam_init = 0.8 - 0.6 * float(np.exp(-0.3 * 1))
    lp = b_lambda[0].astype(F32)
    lam = jnp.exp(jnp.sum(lp[0] * lp[1])) - jnp.exp(jnp.sum(lp[2] * lp[3])) + lam_init
    wb = B_HEADS * 2 * B_HD
    col_scale = jnp.where(jnp.arange(3 * wb) < wb, B_Q_SCALE, 1.0).astype(F32)
    h, qkvh, k_bp, k_bs, v_bp, v_bs, vt = norm_proj(
        h, None, norm_mix[1], (w_b_qkv[0] * col_scale).astype(BF16), None,
        [(0, 3 * wb, "heads", BF16), (wb, 2 * wb, "split", F32), (2 * wb, 3 * wb, "split", F32),
         (2 * wb, 3 * wb, "heads_t", BF16)], tr, n_p, og=og, meta=meta)
    qb_b = min(256, t)
    o_bp = attn_b_prompt(qkvh, vt, lam, b_subln[0], lam_init, batch, t, qb_b)
    o_bs = attn_b_sample(qkvh, cache_b_k[0], cache_b_v[0], lam, b_subln[0], lam_init, n_p, dec_batch, dec_t)
    h, xn, meta, counts = out_router(o_bp, o_bs, h, None, w_b_o[0].astype(BF16), norm_ffn[1], w_route_group[1],
                                     b_route_group[1], w_route_expert[1], b_route_expert[1], True, tr)
    og = moe_experts(xn, meta, counts, w_gate, w_up, w_down, 1, tm)

    y_p, y_s = final_norm(h, og, meta, norm_final, n_p, tr)
    hd2 = 2 * B_HD
    return (y_p.reshape(batch, t, d), y_s.reshape(dec_batch, dec_t, d),
            a_k_prompt, a_v_prompt, a_k_sample, a_v_sample,
            k_bp.reshape(1, batch, t, B_HEADS, hd2), v_bp.reshape(1, batch, t, B_HEADS, hd2),
            k_bs.reshape(1, dec_batch, dec_t, B_HEADS, hd2), v_bs.reshape(1, dec_batch, dec_t, B_HEADS, hd2))
```

```python
import functools

import jax
import jax.numpy as jnp
import numpy as np
from jax import lax
from jax.experimental import pallas as pl
from jax.experimental.pallas import tpu as pltpu
from jax.experimental.pallas import tpu_sc as plsc

F32 = jnp.float32
BF16 = jnp.bfloat16
NEG_INF = float("-inf")

CHUNK = 64
WIN_CHUNKS = 2
A_HEADS, A_KV_HEADS, A_GROUP, A_HD = 16, 4, 4, 64
B_HEADS, B_HD = 8, 64
N_GROUPS, EXPERTS_PER_GROUP = 4, 8
N_EXPERTS = N_GROUPS * EXPERTS_PER_GROUP
NORM_EPS = 1e-6
SUBLN_EPS = 1e-5

LANES = 128
COL_CHUNK = 512
HEAD_GROUP = 2
ONES_ROWS = 16
SC_WINDOW = 128
SC_ROW_WORDS = 256
VMEM_LIMIT = 48 * 1024 * 1024


def _cparams(sem, flags=None):
    return pltpu.CompilerParams(dimension_semantics=sem, vmem_limit_bytes=VMEM_LIMIT, flags=flags)


def _alibi(n):
    return 2.0 ** (-8.0 * np.arange(1, n + 1) / n)


def _store_planes(ref, x):
    w = ref.shape[2]
    for p in range(ref.shape[0]):
        ref[p] = x[:, p * w:(p + 1) * w]


def _join_planes(ref, first, count):
    return jnp.concatenate([ref[first + p] for p in range(count)], axis=1)


def _add_expert_outputs(x, og_ref, meta_ref):
    half = og_ref.shape[1] // 2
    meta = meta_ref[...]
    return (x + meta[:, 2:3] * _unpack_pairs(og_ref[:, :half])) + meta[:, 3:4] * _unpack_pairs(og_ref[:, half:])


def _norm_proj_kernel(n_prompt_blocks, two_src, has_bias, combine, outs, *refs):
    refs = list(refs)
    xa_ref = refs.pop(0)
    xb_ref = refs.pop(0) if two_src else None
    og_ref, meta_ref = (refs.pop(0), refs.pop(0)) if combine else (None, None)
    g_ref = refs.pop(0)
    w_ref = refs.pop(0)
    b_ref = refs.pop(0) if has_bias else None
    out_refs = refs
    x = xa_ref[...]
    if two_src:
        x = jnp.where(pl.program_id(0) < n_prompt_blocks, x, xb_ref[...])
    if combine:
        x = _add_expert_outputs(x, og_ref, meta_ref)
        out_refs.pop(0)[...] = x
    ms = jnp.mean(x * x, axis=-1, keepdims=True)
    xn = (x * lax.rsqrt(ms + NORM_EPS) * g_ref[...]).astype(BF16)
    is_prompt = pl.program_id(0) < n_prompt_blocks
    out_refs = list(out_refs)
    targets = []
    for c0, c1, layout in outs:
        targets.append((c0, c1, layout, out_refs.pop(0), out_refs.pop(0) if layout == "split" else None))
    for s in range(0, w_ref.shape[1], COL_CHUNK):
        e = s + COL_CHUNK
        r = jnp.dot(xn, w_ref[:, s:e], preferred_element_type=F32)
        if has_bias:
            r = r + b_ref[:, s:e]
        for c0, c1, layout, o_ref, o2_ref in targets:
            if not (c0 <= s and e <= c1):
                continue
            rr = r.astype(o_ref.dtype)
            if layout == "flat":
                o_ref[:, s - c0:e - c0] = rr
            elif layout == "split":

                @pl.when(is_prompt)
                def _(rr=rr, o_ref=o_ref, s=s, e=e, c0=c0):
                    o_ref[:, s - c0:e - c0] = rr

                @pl.when(jnp.logical_not(is_prompt))
                def _(rr=rr, o2_ref=o2_ref, s=s, e=e, c0=c0):
                    o2_ref[:, s - c0:e - c0] = rr
            elif layout == "heads":
                for hh in range((e - s) // LANES):
                    o_ref[(s - c0) // LANES + hh] = rr[:, hh * LANES:(hh + 1) * LANES]
            else:
                ones = jnp.ones((ONES_ROWS, r.shape[0]), o_ref.dtype)
                for hh in range((e - s) // LANES):
                    head = (s - c0) // LANES + hh
                    o_ref[head, :LANES, :] = r[:, hh * LANES:(hh + 1) * LANES].T.astype(o_ref.dtype)
                    o_ref[head, LANES:, :] = ones


def norm_proj(xa, xb, g, w, b, outs, tr, n_prompt, og=None, meta=None):
    na, d = xa.shape
    nb = 0 if xb is None else xb.shape[0]
    n = na + nb
    npb = n_prompt // tr
    two = xb is not None
    combine = og is not None
    in_specs = [pl.BlockSpec((tr, d), (lambda i: (jnp.minimum(i, npb - 1), 0)) if two else (lambda i: (i, 0)))]
    args = [xa]
    if two:
        in_specs.append(pl.BlockSpec((tr, d), lambda i: (jnp.maximum(i - npb, 0), 0)))
        args.append(xb)
    if combine:
        in_specs += [pl.BlockSpec((tr, og.shape[1]), lambda i: (i, 0)), pl.BlockSpec((tr, LANES), lambda i: (i, 0))]
        args += [og, meta]
    in_specs.append(pl.BlockSpec((1, d), lambda i: (0, 0)))
    args.append(g.reshape(1, d).astype(F32))
    in_specs.append(pl.BlockSpec(w.shape, lambda i: (0, 0)))
    args.append(w)
    if b is not None:
        in_specs.append(pl.BlockSpec((1, w.shape[1]), lambda i: (0, 0)))
        args.append(b.reshape(1, -1).astype(F32))
    out_shapes, out_specs = [], []
    if combine:
        out_shapes.append(jax.ShapeDtypeStruct((n, d), F32))
        out_specs.append(pl.BlockSpec((tr, d), lambda i: (i, 0)))
    for c0, c1, layout, dt in outs:
        assert c0 % COL_CHUNK == 0 and c1 % COL_CHUNK == 0
        if layout == "flat":
            out_shapes.append(jax.ShapeDtypeStruct((n, c1 - c0), dt))
            out_specs.append(pl.BlockSpec((tr, c1 - c0), lambda i: (i, 0)))
        elif layout == "split":
            out_shapes.append(jax.ShapeDtypeStruct((n_prompt, c1 - c0), dt))
            out_specs.append(pl.BlockSpec((tr, c1 - c0), lambda i: (jnp.minimum(i, npb - 1), 0)))
            out_shapes.append(jax.ShapeDtypeStruct((n - n_prompt, c1 - c0), dt))
            out_specs.append(pl.BlockSpec((tr, c1 - c0), lambda i: (jnp.maximum(i - npb, 0), 0)))
        elif layout == "heads":
            nh = (c1 - c0) // LANES
            out_shapes.append(jax.ShapeDtypeStruct((nh, n, LANES), dt))
            out_specs.append(pl.BlockSpec((nh, tr, LANES), lambda i: (0, i, 0)))
        else:
            assert layout == "heads_t"
            nh = (c1 - c0) // LANES
            out_shapes.append(jax.ShapeDtypeStruct((nh, LANES + ONES_ROWS, n), dt))
            out_specs.append(pl.BlockSpec((nh, LANES + ONES_ROWS, tr), lambda i: (0, 0, i)))
    kern = functools.partial(_norm_proj_kernel, npb, two, b is not None, combine,
                             [(c0, c1, lay) for c0, c1, lay, _ in outs])
    return pl.pallas_call(
        kern, grid=(n // tr,), in_specs=in_specs, out_specs=out_specs, out_shape=out_shapes,
        compiler_params=_cparams(("arbitrary",)), name="norm_proj")(*args)


def _attn_a_chunk(q, k, v, bias_ref, sink_ref, valid):
    nq = q.shape[0]
    outs = []
    for kh in range(A_KV_HEADS):
        qs = jnp.concatenate(
            [q[:, (kh * A_GROUP + g) * A_HD:(kh * A_GROUP + g + 1) * A_HD] for g in range(A_GROUP)], axis=0)
        kk = k[:, kh * A_HD:(kh + 1) * A_HD]
        s = lax.dot_general(qs, kk, (((1,), (1,)), ((), ())), preferred_element_type=F32) + bias_ref[kh]
        if valid is not None:
            s = jnp.where(valid, s, NEG_INF)
        sk = sink_ref[kh]
        m = jnp.maximum(jnp.max(s, axis=-1, keepdims=True), sk)
        e = jnp.exp2(s - m)
        den = jnp.sum(e, axis=-1, keepdims=True) + jnp.exp2(sk - m)
        p = (e * (1.0 / den)).astype(BF16)
        o = jnp.dot(p, v[:, kh * A_HD:(kh + 1) * A_HD], preferred_element_type=F32)
        for g in range(A_GROUP):
            outs.append(o[g * nq:(g + 1) * nq])
    return jnp.concatenate(outs, axis=1)


def _attn_a_prompt_kernel(qb, q_ref, kvc_ref, kvp_ref, bias_ref, sink_ref, o_ref, t_ref, p_ref):
    cb = pl.program_id(1)
    back = CHUNK * WIN_CHUNKS
    nkv = A_KV_HEADS * A_HD
    kvfull = jnp.concatenate([kvp_ref[...], kvc_ref[...]], axis=0)
    k_all = kvfull[:, :nkv].astype(BF16)
    vt_all = kvfull[:, nkv:].T.astype(BF16)
    key_pos = lax.broadcasted_iota(jnp.int32, (back + qb, 1), 0) + (cb * qb - back)
    before_start = jnp.where(key_pos >= 0, 0.0, NEG_INF)
    dn = (((1,), (1,)), ((), ()))
    for kh in range(A_KV_HEADS):
        qs = jnp.concatenate(
            [q_ref[:, (kh * A_GROUP + g) * A_HD:(kh * A_GROUP + g + 1) * A_HD] for g in range(A_GROUP)], axis=0)
        t_ref[kh] = lax.dot_general(k_all[:, kh * A_HD:(kh + 1) * A_HD], qs, dn, preferred_element_type=F32)
    inv_dens = []
    for kh in range(A_KV_HEADS):
        st = t_ref[kh] + bias_ref[kh] + before_start
        sk = sink_ref[kh]
        m = jnp.maximum(jnp.max(st, axis=0, keepdims=True), sk)
        e = jnp.exp2(st - m)
        inv_dens.append(1.0 / (jnp.sum(e, axis=0, keepdims=True) + jnp.exp2(sk - m)))
        p_ref[kh] = e.astype(BF16)
    outs = []
    for kh in range(A_KV_HEADS):
        ot = jnp.dot(vt_all[kh * A_HD:(kh + 1) * A_HD, :], p_ref[kh], preferred_element_type=F32)
        o = (ot * inv_dens[kh]).T
        for g in range(A_GROUP):
            outs.append(o[g * qb:(g + 1) * qb])
    o_ref[...] = jnp.concatenate(outs, axis=1).astype(o_ref.dtype)


def _a_prompt_tables(qb, sinks):
    back = CHUNK * WIN_CHUNKS
    slopes = _alibi(A_HEADS).reshape(A_KV_HEADS, A_GROUP)
    qpos = np.arange(qb)
    kpos = np.arange(back + qb) - back
    dist = np.abs(kpos[:, None] - qpos[None, :]).astype(np.float64)
    band_lo = (qpos // CHUNK) * CHUNK - back
    in_band = (kpos[:, None] >= band_lo[None, :]) & (kpos[:, None] < band_lo[None, :] + back + CHUNK)
    bias = np.where(in_band[None, None], -slopes[:, :, None, None] * dist[None, None], -np.inf)
    bias = np.transpose(bias, (0, 2, 1, 3)).reshape(A_KV_HEADS, back + qb, A_GROUP * qb) * LOG2E
    sk = jnp.repeat(sinks.astype(F32).reshape(A_KV_HEADS, A_GROUP), qb, axis=1)[:, None, :] * LOG2E
    return jnp.asarray(bias, F32), sk


def _a_tables(qpos, kpos, sinks):
    slopes = _alibi(A_HEADS).reshape(A_KV_HEADS, A_GROUP)
    dist = np.abs(qpos[:, None] - kpos[None, :]).astype(np.float64)
    bias = -slopes[:, :, None, None] * dist
    nq = len(qpos)
    bias = jnp.asarray(bias.reshape(A_KV_HEADS, A_GROUP * nq, len(kpos)) * LOG2E, F32)
    sk = jnp.repeat(sinks.astype(F32).reshape(A_KV_HEADS, A_GROUP), nq, axis=1)[..., None] * LOG2E
    return bias, sk


def attn_a_prompt(q_all, kv_all, sinks, batch, t, qb):
    nq = A_HEADS * A_HD
    nkv2 = 2 * A_KV_HEADS * A_HD
    back = CHUNK * WIN_CHUNKS
    bias, sk = _a_prompt_tables(qb, sinks)
    nblk = t // qb
    r = qb // back
    return pl.pallas_call(
        functools.partial(_attn_a_prompt_kernel, qb),
        grid=(batch, nblk),
        in_specs=[
            pl.BlockSpec((qb, nq), lambda b, i: (b * nblk + i, 0)),
            pl.BlockSpec((qb, nkv2), lambda b, i: (b * nblk + i, 0)),
            pl.BlockSpec((back, nkv2), lambda b, i: (jnp.maximum((b * nblk + i) * r - 1, 0), 0)),
            pl.BlockSpec(bias.shape, lambda b, i: (0, 0, 0)),
            pl.BlockSpec(sk.shape, lambda b, i: (0, 0, 0)),
        ],
        out_specs=pl.BlockSpec((qb, nq), lambda b, i: (b * nblk + i, 0)),
        out_shape=jax.ShapeDtypeStruct((batch * t, nq), BF16),
        scratch_shapes=[pltpu.VMEM((A_KV_HEADS, back + qb, A_GROUP * qb), F32),
                        pltpu.VMEM((A_KV_HEADS, back + qb, A_GROUP * qb), BF16)],
        compiler_params=_cparams(("arbitrary", "arbitrary")), name="attn_a_prompt",
    )(q_all, kv_all, kv_all, bias, sk)


def _attn_a_sample_kernel(q_ref, k_ref, v_ref, bias_ref, sink_ref, o_ref):
    o = _attn_a_chunk(q_ref[...], k_ref[...].astype(BF16), v_ref[...].astype(BF16), bias_ref, sink_ref, None)
    o_ref[...] = o.astype(o_ref.dtype)


def attn_a_sample(q_all, kband, vband, sinks, row0, dec_batch, dec_t):
    nq = A_HEADS * A_HD
    s = kband.shape[1]
    past = s - dec_t
    kpos = np.concatenate([np.arange(past) - past, np.arange(dec_t)])
    bias, sk = _a_tables(np.arange(dec_t), kpos, sinks)
    blk0 = row0 // dec_t
    return pl.pallas_call(
        _attn_a_sample_kernel,
        grid=(dec_batch,),
        in_specs=[
            pl.BlockSpec((dec_t, nq), lambda b: (blk0 + b, 0)),
            pl.BlockSpec((None, s, kband.shape[2]), lambda b: (b, 0, 0)),
            pl.BlockSpec((None, s, vband.shape[2]), lambda b: (b, 0, 0)),
            pl.BlockSpec(bias.shape, lambda b: (0, 0, 0)),
            pl.BlockSpec(sk.shape, lambda b: (0, 0, 0)),
        ],
        out_specs=pl.BlockSpec((dec_t, nq), lambda b: (b, 0)),
        out_shape=jax.ShapeDtypeStruct((dec_batch * dec_t, nq), BF16),
        compiler_params=_cparams(("arbitrary",)), name="attn_a_sample",
    )(q_all, kband, vband, bias, sk)


B_Q_SCALE = B_HD ** -0.5 * float(np.log2(np.e))
LOG2E = float(np.log2(np.e))


def _stack_maps(qh):
    lane = lax.broadcasted_iota(jnp.int32, qh.shape, 1)
    zero = jnp.zeros_like(qh)
    return jnp.concatenate([jnp.where(lane < B_HD, qh, zero), jnp.where(lane >= B_HD, qh, zero)], axis=0)


def _diff_finish(acc, l, nq, lam, lam_init, subln):
    inv = 1.0 / l
    o = acc[:nq] * inv[:nq] - lam * (acc[nq:] * inv[nq:])
    ms = jnp.mean(o * o, axis=-1, keepdims=True)
    return o * lax.rsqrt(ms + SUBLN_EPS) * subln * (1.0 - lam_init)


def _attn_b_prompt_kernel(qb, nb, lam_init, pair_ref, slope_ref, lam_ref, q_ref, k_ref, vt_ref, kbias_ref,
                          subln_ref, o_ref, acc_ref, m_ref, al_ref, qq_ref, t0_ref, t1_ref, p0_ref, p1_ref):
    g = pl.program_id(1)
    lam = lam_ref[0]
    dn = (((1,), (1,)), ((), ()))
    t_refs = (t0_ref, t1_ref)
    p_refs = (p0_ref, p1_ref)
    heads = range(HEAD_GROUP)
    n_pairs = nb * (nb + 1) // 2
    slopes = [slope_ref[g * HEAD_GROUP + u] for u in heads]

    def pair(n):
        return pair_ref[2 * n], pair_ref[2 * n + 1]

    def scores(n, slot):
        j, iq = pair(n)
        ks = pl.multiple_of(j * qb, qb)
        for u in heads:
            t_refs[slot][u] = lax.dot_general(k_ref[u, pl.ds(ks, qb), :], qq_ref[u * nb + iq], dn,
                                              preferred_element_type=F32)

    def softmax_step(n, slot):
        j, iq = pair(n)
        on_diag = (j == iq).astype(jnp.int32)
        for u in heads:
            st = u * nb + iq
            m = m_ref[st, 0:1, :]
            t = t_refs[slot][u] + kbias_ref[u, on_diag]
            off = slopes[u] * jnp.full((1, 2 * qb), (j - iq) * qb, jnp.int32).astype(F32)
            m_new = jnp.maximum(m, jnp.max(t, axis=0, keepdims=True) + off)
            p_refs[slot][u] = jnp.exp2(t + (off - m_new)).astype(BF16)
            al_ref[slot * HEAD_GROUP + u, 0:1, :] = jnp.exp2(m - m_new)
            m_ref[st, 0:1, :] = m_new

    def accumulate(n, slot):
        j, iq = pair(jnp.maximum(n, 0))
        ks = pl.multiple_of(j * qb, qb)
        for u in heads:
            st = u * nb + iq
            pv = jnp.dot(vt_ref[u, :, pl.ds(ks, qb)], p_refs[slot][u], preferred_element_type=F32)
            acc_ref[st] = al_ref[slot * HEAD_GROUP + u, 0:1, :] * acc_ref[st] + pv

    def step(n, slot):
        scores(jnp.minimum(n + 1, n_pairs - 1), 1 - slot)
        softmax_step(n, slot)
        accumulate(n - 1, 1 - slot)
        return 0

    acc_ref[...] = jnp.zeros_like(acc_ref)
    m_ref[...] = jnp.full(m_ref.shape, NEG_INF, F32)
    al_ref[...] = jnp.ones_like(al_ref)
    p1_ref[...] = jnp.zeros_like(p1_ref)
    for u in heads:
        for iq in range(nb):
            qq_ref[u * nb + iq] = _stack_maps(q_ref[u, iq * qb:(iq + 1) * qb, :])
    scores(0, 0)
    lax.fori_loop(0, n_pairs, lambda n, c: lax.cond(n % 2 == 0, lambda: step(n, 0), lambda: step(n, 1)), 0)
    accumulate(n_pairs - 1, (n_pairs - 1) % 2)
    for u in heads:
        for iq in range(nb):
            inv = 1.0 / acc_ref[u * nb + iq, LANES:LANES + 1, :]
            acc = acc_ref[u * nb + iq, :LANES, :]
            ot = acc[:, :qb] * inv[:, :qb] - lam * (acc[:, qb:] * inv[:, qb:])
            ms = jnp.mean(ot * ot, axis=0, keepdims=True)
            ot = ot * lax.rsqrt(ms + SUBLN_EPS) * subln_ref[...] * (1.0 - lam_init)
            o_ref[u, iq * qb:(iq + 1) * qb, :] = ot.T.astype(o_ref.dtype)


def attn_b_prompt(qkvh, vt, lam, subln, lam_init, batch, t, qb):
    nblk = t // qb
    qq = np.arange(qb)
    slope2 = (_alibi(B_HEADS) * LOG2E)[:, None, None]
    kcol = np.broadcast_to(qq[:, None], (qb, qb)).astype(np.float64)
    tri = -2.0 * np.maximum(qq[:, None] - qq[None, :], 0)
    msk = np.where((qq[:, None] // CHUNK) <= (qq[None, :] // CHUNK), 0.0, -np.inf)
    both = lambda a: np.concatenate([a, a], axis=-1)
    kbias = np.stack([both(slope2 * kcol[None]), both(slope2 * (kcol + tri)[None] + msk[None])], axis=1)
    kbias = jnp.asarray(kbias, F32)
    slopes = jnp.asarray(_alibi(B_HEADS) * LOG2E, F32)
    subln_t = jnp.broadcast_to(subln.astype(F32).reshape(LANES, 1), (LANES, qb))
    pairs = jnp.asarray([v for j in range(nblk) for i in range(j, nblk) for v in (j, i)], jnp.int32)
    hg = HEAD_GROUP
    n_groups = B_HEADS // hg
    grid_spec = pltpu.PrefetchScalarGridSpec(
        num_scalar_prefetch=3, grid=(batch, n_groups),
        in_specs=[
            pl.BlockSpec((hg, t, LANES), lambda b, g, *_: (g, b, 0)),
            pl.BlockSpec((hg, t, LANES), lambda b, g, *_: (n_groups + g, b, 0)),
            pl.BlockSpec((hg, LANES + ONES_ROWS, t), lambda b, g, *_: (g, 0, b)),
            pl.BlockSpec((hg,) + kbias.shape[1:], lambda b, g, *_: (g, 0, 0, 0)),
            pl.BlockSpec(subln_t.shape, lambda b, g, *_: (0, 0)),
        ],
        out_specs=pl.BlockSpec((hg, t, LANES), lambda b, g, *_: (g, b, 0)),
        scratch_shapes=[pltpu.VMEM((hg * nblk, LANES + ONES_ROWS, 2 * qb), F32),
                        pltpu.VMEM((hg * nblk, 8, 2 * qb), F32),
                        pltpu.VMEM((2 * hg, 8, 2 * qb), F32),
                        pltpu.VMEM((hg * nblk, 2 * qb, LANES), BF16),
                        pltpu.VMEM((hg, qb, 2 * qb), F32), pltpu.VMEM((hg, qb, 2 * qb), F32),
                        pltpu.VMEM((hg, qb, 2 * qb), BF16), pltpu.VMEM((hg, qb, 2 * qb), BF16)],
    )
    return pl.pallas_call(
        functools.partial(_attn_b_prompt_kernel, qb, nblk, lam_init), grid_spec=grid_spec,
        out_shape=jax.ShapeDtypeStruct((B_HEADS, batch * t, LANES), BF16),
        compiler_params=_cparams(("arbitrary", "arbitrary")), name="attn_b_prompt",
    )(pairs, slopes, lam.reshape(1).astype(F32), qkvh, qkvh, vt, kbias, subln_t)


def _attn_b_sample_kernel(dec_t, past, lam_init, slopes, lam_ref, q_ref, kn_ref, vn_ref, ck_ref, cv_ref,
                          tnew_ref, subln_ref, o_ref):
    lam = lam_ref[0]
    subln = subln_ref[...]
    colc = lax.broadcasted_iota(jnp.int32, (1, past), 1).astype(F32) - float(past)
    for h in range(B_HEADS):
        slope = float(slopes[h]) * LOG2E
        qq = _stack_maps(q_ref[h])
        kc = ck_ref[pl.ds(h, past, stride=B_HEADS), :].astype(BF16)
        vc = cv_ref[pl.ds(h, past, stride=B_HEADS), :].astype(BF16)
        dn = (((1,), (1,)), ((), ()))
        s_c = lax.dot_general(qq, kc, dn, preferred_element_type=F32) + slope * colc
        s_n = lax.dot_general(qq, kn_ref[h], dn, preferred_element_type=F32) + slope * tnew_ref[...]
        m = jnp.maximum(jnp.max(s_c, axis=-1, keepdims=True), jnp.max(s_n, axis=-1, keepdims=True))
        p_c = jnp.exp2(s_c - m)
        p_n = jnp.exp2(s_n - m)
        l = jnp.sum(p_c, axis=-1, keepdims=True) + jnp.sum(p_n, axis=-1, keepdims=True)
        acc = (jnp.dot(p_c.astype(BF16), vc, preferred_element_type=F32)
               + jnp.dot(p_n.astype(BF16), vn_ref[h], preferred_element_type=F32))
        o_ref[h] = _diff_finish(acc, l, dec_t, lam, lam_init, subln).astype(o_ref.dtype)


def attn_b_sample(qkvh, cache_k, cache_v, lam, subln, lam_init, row0, dec_batch, dec_t):
    past = cache_k.shape[1]
    qi = np.arange(dec_t)
    tnew = (qi[:, None] - np.abs(qi[:, None] - qi[None, :])).astype(np.float32)
    tnew = jnp.asarray(np.concatenate([tnew, tnew], axis=0))
    blk0 = row0 // dec_t
    rows = past * B_HEADS
    cache_k = cache_k.reshape(dec_batch, rows, LANES)
    cache_v = cache_v.reshape(dec_batch, rows, LANES)
    grid_spec = pltpu.PrefetchScalarGridSpec(
        num_scalar_prefetch=1, grid=(dec_batch,),
        in_specs=[
            pl.BlockSpec((B_HEADS, dec_t, LANES), lambda b, *_: (0, blk0 + b, 0)),
            pl.BlockSpec((B_HEADS, dec_t, LANES), lambda b, *_: (1, blk0 + b, 0)),
            pl.BlockSpec((B_HEADS, dec_t, LANES), lambda b, *_: (2, blk0 + b, 0)),
            pl.BlockSpec((None, rows, LANES), lambda b, *_: (b, 0, 0)),
            pl.BlockSpec((None, rows, LANES), lambda b, *_: (b, 0, 0)),
            pl.BlockSpec(tnew.shape, lambda b, *_: (0, 0)),
            pl.BlockSpec((1, LANES), lambda b, *_: (0, 0)),
        ],
        out_specs=pl.BlockSpec((B_HEADS, dec_t, LANES), lambda b, *_: (0, b, 0)),
    )
    return pl.pallas_call(
        functools.partial(_attn_b_sample_kernel, dec_t, past, lam_init, tuple(_alibi(B_HEADS))),
        grid_spec=grid_spec,
        out_shape=jax.ShapeDtypeStruct((B_HEADS, dec_batch * dec_t, LANES), BF16),
        compiler_params=_cparams(("arbitrary",)), name="attn_b_sample",
    )(lam.reshape(1).astype(F32), qkvh, qkvh, qkvh, cache_k, cache_v, tnew, subln.reshape(1, LANES).astype(F32))


ROUTE_ROWS = 40
ROUTE_FIELDS = 8


def _route_t(lt):
    row = lax.broadcasted_iota(jnp.int32, lt.shape, 0).astype(F32)
    big = float(LANES)
    lg = jnp.where(row < N_GROUPS, lt, NEG_INF)
    gmax = jnp.max(lg, axis=0, keepdims=True)
    g_idx = jnp.min(jnp.where(lg == gmax, row, big), axis=0, keepdims=True)
    g_prob = 1.0 / jnp.sum(jnp.exp(lg - gmax), axis=0, keepdims=True)
    lo = N_GROUPS + EXPERTS_PER_GROUP * g_idx
    le = jnp.where((row >= lo) & (row < lo + EXPERTS_PER_GROUP), lt, NEG_INF)
    m1 = jnp.max(le, axis=0, keepdims=True)
    i1 = jnp.min(jnp.where(le == m1, row, big), axis=0, keepdims=True)
    le2 = jnp.where(row == i1, NEG_INF, le)
    m2 = jnp.max(le2, axis=0, keepdims=True)
    i2 = jnp.min(jnp.where(le2 == m2, row, big), axis=0, keepdims=True)
    t = jnp.exp(m2 - m1)
    ga = 1.0 / (1.0 + t)
    gb = t * ga
    return row, i1, i2, g_prob * ga, g_prob * gb


def _unpack_pairs(u):
    lo = pltpu.bitcast(u << 16, F32)
    hi = pltpu.bitcast(u & jnp.uint32(0xFFFF0000), F32)
    return jnp.concatenate([lo, hi], axis=1)


def _pack_pairs(x):
    bits = pltpu.bitcast(x.astype(BF16).astype(F32), jnp.uint32)
    w = x.shape[1] // 2
    return (bits[:, :w] >> 16) | bits[:, w:]


def _out_router_kernel(n_prompt_blocks, heads_in, two_o, two_res, *refs):
    refs = list(refs)
    oa_ref = refs.pop(0)
    ob_ref = refs.pop(0) if two_o else None
    ra_ref = refs.pop(0)
    rb_ref = refs.pop(0) if two_res else None
    (wo_ref, g_ref, wr_ref, wr2_ref, br_ref, utri_ref, h_ref, xn_ref, meta_ref, route_ref, cnt_ref,
     carry_ref) = refs
    is_prompt = pl.program_id(0) < n_prompt_blocks

    def load_o(ref):
        if heads_in:
            return jnp.concatenate([ref[hh] for hh in range(ref.shape[0])], axis=1)
        return ref[...]

    o = load_o(oa_ref)
    if two_o:
        o = jnp.where(is_prompt, o, load_o(ob_ref))
    res = ra_ref[...]
    if two_res:
        res = jnp.where(is_prompt, res, rb_ref[...])
    h = res + jnp.dot(o, wo_ref[...], preferred_element_type=F32)
    h_ref[...] = h
    ms = jnp.mean(h * h, axis=-1, keepdims=True)
    xn = h * lax.rsqrt(ms + NORM_EPS) * g_ref[...]
    x_hi = xn.astype(BF16)
    _store_planes(xn_ref, _pack_pairs(xn))
    x_lo = (xn - x_hi.astype(F32)).astype(BF16)
    r = jnp.dot(x_hi, wr_ref[...], preferred_element_type=F32)
    logits = (r[:, :LANES] + r[:, LANES:]) + jnp.dot(x_lo, wr2_ref[...], preferred_element_type=F32) + br_ref[...]
    row, i1, i2, g1, g2 = _route_t(logits.T[:ROUTE_ROWS])

    @pl.when(pl.program_id(0) == 0)
    def _():
        carry_ref[...] = jnp.zeros_like(carry_ref)

    sel1 = row == i1
    sel2 = row == i2
    onehot = jnp.where(sel1 | sel2, 1.0, 0.0)
    before = jnp.dot(onehot.astype(BF16), utri_ref[...], preferred_element_type=F32) + carry_ref[:, 0:1]
    rank1 = jnp.sum(jnp.where(sel1, before, 0.0), axis=0, keepdims=True)
    rank2 = jnp.sum(jnp.where(sel2, before, 0.0), axis=0, keepdims=True)
    carry_ref[...] = carry_ref[...] + jnp.sum(onehot, axis=1, keepdims=True)
    cnt_ref[...] = carry_ref[...]
    fields = [i1 - N_GROUPS, i2 - N_GROUPS, g1, g2, rank1, rank2]
    fields += [jnp.zeros_like(g1)] * (ROUTE_FIELDS - len(fields))
    route = jnp.concatenate(fields, axis=0)
    route_ref[...] = route
    meta_ref[...] = jnp.concatenate([route, jnp.zeros((LANES - ROUTE_FIELDS, route.shape[1]), F32)], axis=0).T


def out_router(oa, ob, ra, rb, wo, g, w_rg, b_rg, w_re, b_re, heads_in, tr):
    d = ra.shape[1]
    if heads_in:
        na = oa.shape[1]
        nb = 0 if ob is None else ob.shape[1]
    else:
        na = oa.shape[0]
        nb = 0 if ob is None else ob.shape[0]
    n = na + nb
    npb = na // tr
    two_o = ob is not None
    two_res = rb is not None
    first = lambda i: (jnp.minimum(i, npb - 1), 0)
    second = lambda i: (jnp.maximum(i - npb, 0), 0)
    plain = lambda i: (i, 0)
    in_specs, args = [], []

    def add_o(x, imap):
        if heads_in:
            in_specs.append(pl.BlockSpec((x.shape[0], tr, LANES), lambda i: (0, imap(i)[0], 0)))
        else:
            in_specs.append(pl.BlockSpec((tr, x.shape[1]), imap))
        args.append(x)

    add_o(oa, first if two_o else plain)
    if two_o:
        add_o(ob, second)
    in_specs.append(pl.BlockSpec((tr, d), first if two_res else plain))
    args.append(ra)
    if two_res:
        in_specs.append(pl.BlockSpec((tr, d), second))
        args.append(rb)
    wr = jnp.zeros((d, LANES), F32).at[:, :N_GROUPS].set(w_rg.astype(F32))
    wr = wr.at[:, N_GROUPS:N_GROUPS + N_EXPERTS].set(w_re.astype(F32))
    wr_hi = wr.astype(BF16)
    wr_lo = (wr - wr_hi.astype(F32)).astype(BF16)
    br = jnp.zeros((1, LANES), F32).at[0, :N_GROUPS].set(b_rg.astype(F32))
    br = br.at[0, N_GROUPS:N_GROUPS + N_EXPERTS].set(b_re.astype(F32))
    utri = jnp.asarray(np.triu(np.ones((tr, tr), np.float32), 1), BF16)
    consts = [wo, g.reshape(1, d).astype(F32), jnp.concatenate([wr_hi, wr_lo], axis=1), wr_hi, br, utri]
    for cst in consts:
        in_specs.append(pl.BlockSpec(cst.shape, lambda i: (0, 0)))
        args.append(cst)
    parts = d // 2 // SC_ROW_WORDS
    out_shape = [jax.ShapeDtypeStruct((n, d), F32), jax.ShapeDtypeStruct((parts, n, SC_ROW_WORDS), jnp.uint32),
                 jax.ShapeDtypeStruct((n, LANES), F32), jax.ShapeDtypeStruct((ROUTE_FIELDS, n), F32),
                 jax.ShapeDtypeStruct((ROUTE_ROWS, LANES), F32)]
    out_specs = [pl.BlockSpec((tr, d), plain), pl.BlockSpec((parts, tr, SC_ROW_WORDS), lambda i: (0, i, 0)),
                 pl.BlockSpec((tr, LANES), plain), pl.BlockSpec((ROUTE_FIELDS, tr), lambda i: (0, i)),
                 pl.BlockSpec((ROUTE_ROWS, LANES), lambda i: (0, 0))]
    return pl.pallas_call(
        functools.partial(_out_router_kernel, npb, heads_in, two_o, two_res),
        grid=(n // tr,), in_specs=in_specs, out_specs=out_specs, out_shape=out_shape,
        scratch_shapes=[pltpu.VMEM((ROUTE_ROWS, LANES), F32)],
        compiler_params=_cparams(("arbitrary",)), name="out_router")(*args)


def _expert_kernel(te_ref, nt_ref, x_ref, wg_ref, wu_ref, wd_ref, o_ref):
    @pl.when(pl.program_id(0) < nt_ref[0])
    def _():
        x = _unpack_pairs(_join_planes(x_ref, 0, x_ref.shape[0])).astype(BF16)
        a = jnp.dot(x, wg_ref[...].astype(BF16), preferred_element_type=F32)
        u = jnp.dot(x, wu_ref[...].astype(BF16), preferred_element_type=F32)
        hh = (a * (1.0 / (1.0 + jnp.exp(-a))) * u).astype(BF16)
        _store_planes(o_ref, _pack_pairs(jnp.dot(hh, wd_ref[...].astype(BF16), preferred_element_type=F32)))

    @pl.when(pl.program_id(0) >= nt_ref[0])
    def _():
        o_ref[...] = jnp.zeros_like(o_ref)


def expert_mlp(xs, tile_expert, n_tiles, w_gate, w_up, w_down, layer, tm):
    parts, p, w = xs.shape
    d, f = w_gate.shape[-2:]
    epg = w_gate.shape[2]
    wmap = lambda t, te, nt: (layer, te[t] // epg, te[t] % epg, 0, 0)
    grid_spec = pltpu.PrefetchScalarGridSpec(
        num_scalar_prefetch=2, grid=(p // tm,),
        in_specs=[
            pl.BlockSpec((parts, tm, w), lambda t, te, nt: (0, t, 0)),
            pl.BlockSpec((None, None, None, d, f), wmap),
            pl.BlockSpec((None, None, None, d, f), wmap),
            pl.BlockSpec((None, None, None, f, d), wmap),
        ],
        out_specs=pl.BlockSpec((parts, tm, w), lambda t, te, nt: (0, t, 0)),
    )
    return pl.pallas_call(
        _expert_kernel, grid_spec=grid_spec, out_shape=jax.ShapeDtypeStruct((parts, p, w), jnp.uint32),
        compiler_params=_cparams(("arbitrary",)), name="expert_mlp")(tile_expert, n_tiles, xs, w_gate, w_up, w_down)


def sc_scatter_rows(x2, idx, n_out):
    mesh = plsc.VectorSubcoreMesh(core_axis_name="c", subcore_axis_name="s")
    n_src_win = x2.shape[0] // SC_WINDOW

    @functools.partial(pl.kernel, out_type=jax.ShapeDtypeStruct((n_out, SC_ROW_WORDS), x2.dtype), mesh=mesh)
    def scatter(x_hbm, i_hbm, o_hbm):
        def body(x_vmem, i_vmem):
            pltpu.sync_copy(x_vmem, o_hbm.at[i_vmem.at[0]])

        pltpu.emit_pipeline(
            body, grid=(n_out // SC_WINDOW,),
            in_specs=[pl.BlockSpec((SC_WINDOW, SC_ROW_WORDS), lambda i: (lax.rem(i, n_src_win), 0)),
                      pl.BlockSpec((1, SC_WINDOW), lambda i: (0, i))],
            out_specs=[], core_axis_name=("c", "s"), dimension_semantics=(pltpu.PARALLEL,),
        )(x_hbm, i_hbm)

    return scatter(x2, idx)


def sc_gather_rows(x2, idx):
    mesh = plsc.VectorSubcoreMesh(core_axis_name="c", subcore_axis_name="s")
    n_col, n = idx.shape

    @functools.partial(pl.kernel, out_type=jax.ShapeDtypeStruct((n, n_col * SC_ROW_WORDS), x2.dtype), mesh=mesh)
    def gather(x_hbm, i_hbm, o_hbm):
        def body(i_vmem, o_vmem):
            pltpu.sync_copy(x_hbm.at[i_vmem.at[0]], o_vmem)

        pltpu.emit_pipeline(
            body, grid=(n // SC_WINDOW, n_col),
            in_specs=[pl.BlockSpec((1, SC_WINDOW), lambda i, c: (c, i))],
            out_specs=[pl.BlockSpec((SC_WINDOW, SC_ROW_WORDS), lambda i, c: (i, c))],
            core_axis_name=("c", "s"), dimension_semantics=(pltpu.PARALLEL, pltpu.ARBITRARY),
        )(i_hbm, o_hbm)

    return gather(x2, idx)


def moe_dispatch(xn_packed, route, counts, tm):
    parts, n, w = xn_packed.shape
    e = route[0:2].astype(jnp.int32)
    rank = route[4:6].astype(jnp.int32)
    counts = counts[N_GROUPS:N_GROUPS + N_EXPERTS, 0].astype(jnp.int32)
    padded = ((counts + tm - 1) // tm) * tm
    pend = jnp.cumsum(padded)
    pstart = pend - padded
    ids = jnp.arange(N_EXPERTS, dtype=jnp.int32)
    pos = jnp.sum(jnp.where(e[..., None] == ids, pstart, 0), axis=-1) + rank
    p = 2 * n + N_EXPERTS * tm
    n_pad = p - 2 * n
    gap_len = jnp.concatenate([padded - counts, (p - pend[-1]).reshape(1)])
    gap_first = jnp.concatenate([pstart + counts, pend[-1:]])
    gap_end = jnp.cumsum(gap_len)
    k = jnp.arange(n_pad, dtype=jnp.int32)
    gap = jnp.sum((k[:, None] >= gap_end[None, :]).astype(jnp.int32), axis=1)
    shift = gap_first - (gap_end - gap_len)
    pad_pos = k + jnp.sum(jnp.where(gap[:, None] == jnp.arange(N_EXPERTS + 1), shift, 0), axis=-1)
    planes = lambda rows: jnp.concatenate([rows + q * p for q in range(parts)])
    scat_idx = jnp.concatenate([planes(pos[0]), planes(pos[1]), planes(pad_pos)]).reshape(1, parts * p)
    gath_idx = jnp.stack([pos[k] + q * p for k in range(2) for q in range(parts)])
    xs = sc_scatter_rows(xn_packed.reshape(parts * n, w), scat_idx, parts * p).reshape(parts, p, w)
    tile_start = jnp.arange(p // tm, dtype=jnp.int32) * tm
    tile_expert = jnp.sum((tile_start[:, None] >= pend[None, :]).astype(jnp.int32), axis=1)
    tile_expert = jnp.minimum(tile_expert, N_EXPERTS - 1)
    n_tiles = (pend[-1] // tm).astype(jnp.int32).reshape(1)
    return xs, tile_expert, n_tiles, gath_idx


def moe_experts(xn_packed, route, counts, w_gate, w_up, w_down, layer, tm):
    parts, n, w = xn_packed.shape
    xs, tile_expert, n_tiles, gath_idx = moe_dispatch(xn_packed, route, counts, tm)
    o = expert_mlp(xs, tile_expert, n_tiles, w_gate, w_up, w_down, layer, tm)
    return sc_gather_rows(o.reshape(parts * o.shape[1], w), gath_idx)


def _final_norm_kernel(n_prompt_blocks, h_ref, o0_ref, o1_ref, meta_ref, g_ref, yp_ref, ys_ref):
    meta = meta_ref[...]
    x = (h_ref[...] + meta[:, 2:3] * _unpack_pairs(o0_ref[...])) + meta[:, 3:4] * _unpack_pairs(o1_ref[...])
    ms = jnp.mean(x * x, axis=-1, keepdims=True)
    y = x * lax.rsqrt(ms + NORM_EPS) * g_ref[...]
    i = pl.program_id(0)

    @pl.when(i < n_prompt_blocks)
    def _():
        yp_ref[...] = y

    @pl.when(i >= n_prompt_blocks)
    def _():
        ys_ref[...] = y


def final_norm(h, og, meta, g, n_prompt, tr):
    n, d = h.shape
    npb = n_prompt // tr
    return pl.pallas_call(
        functools.partial(_final_norm_kernel, npb),
        grid=(n // tr,),
        in_specs=[pl.BlockSpec((tr, d), lambda i: (i, 0)),
                  pl.BlockSpec((tr, og.shape[1] // 2), lambda i: (i, 0)),
                  pl.BlockSpec((tr, og.shape[1] // 2), lambda i: (i, 1)),
                  pl.BlockSpec((tr, LANES), lambda i: (i, 0)), pl.BlockSpec((1, d), lambda i: (0, 0))],
        out_specs=[pl.BlockSpec((tr, d), lambda i: (jnp.minimum(i, npb - 1), 0)),
                   pl.BlockSpec((tr, d), lambda i: (jnp.maximum(i - npb, 0), 0))],
        out_shape=[jax.ShapeDtypeStruct((n_prompt, d), F32), jax.ShapeDtypeStruct((n - n_prompt, d), F32)],
        compiler_params=_cparams(("arbitrary",)), name="final_norm")(h, og, og, meta, g.reshape(1, d).astype(F32))


def _pick_rows_tile(n_prompt, n_sample, largest=512):
    for tr in (1024, 512, 256, 128, 64, 32, 16, 8):
        if tr <= largest and n_prompt % tr == 0 and n_sample % tr == 0:
            return tr
    raise ValueError("row counts must be multiples of 8")


def kernel(x_prompt, x_sample, cache_a_k, cache_a_v, cache_b_k, cache_b_v, norm_mix, norm_ffn, norm_final,
           w_a_qkv, b_a_qkv, a_sinks, w_a_o, w_b_qkv, b_lambda, b_subln, w_b_o,
           w_route_group, b_route_group, w_route_expert, b_route_expert, w_gate, w_up, w_down):
    batch, t, d = x_prompt.shape
    dec_batch, dec_t, _ = x_sample.shape
    n_p, n_s = batch * t, dec_batch * dec_t
    tr = _pick_rows_tile(n_p, n_s)
    tm = min(512, tr * 4)
    xp = x_prompt.reshape(n_p, d)
    xs = x_sample.reshape(n_s, d)
    nq_a = A_HEADS * A_HD
    nkv_a = A_KV_HEADS * A_HD

    a_scale = jnp.where(jnp.arange(nq_a + 2 * nkv_a) < nq_a, A_HD ** -0.5 * LOG2E, 1.0).astype(F32)
    q_a, kv_ap, kv_as = norm_proj(xp, xs, norm_mix[0], (w_a_qkv[0] * a_scale).astype(BF16), b_a_qkv[0] * a_scale,
                                  [(0, nq_a, "flat", BF16), (nq_a, nq_a + 2 * nkv_a, "split", F32)], tr, n_p)
    qb_a = min(256, t)
    o_ap = attn_a_prompt(q_a, kv_ap, a_sinks[0], batch, t, qb_a)
    kv_s = kv_as.reshape(dec_batch, dec_t, 2 * nkv_a)
    past_a = cache_a_k.shape[2]
    kband = jnp.concatenate([cache_a_k[0].reshape(dec_batch, past_a, nkv_a), kv_s[..., :nkv_a]], axis=1)
    vband = jnp.concatenate([cache_a_v[0].reshape(dec_batch, past_a, nkv_a), kv_s[..., nkv_a:]], axis=1)
    o_as = attn_a_sample(q_a, kband, vband, a_sinks[0], n_p, dec_batch, dec_t)
    h, xn, meta, route, counts = out_router(o_ap, o_as, xp, xs, w_a_o[0].astype(BF16), norm_ffn[0], w_route_group[0],
                                            b_route_group[0], w_route_expert[0], b_route_expert[0], False, tr)
    og = moe_experts(xn, route, counts, w_gate, w_up, w_down, 0, tm)

    keep = min(CHUNK * WIN_CHUNKS, t)
    kv_p = kv_ap.reshape(batch, t, 2 * nkv_a)[:, t - keep:]
    a_k_prompt = kv_p[..., :nkv_a].reshape(1, batch, keep, A_KV_HEADS, A_HD)
    a_v_prompt = kv_p[..., nkv_a:].reshape(1, batch, keep, A_KV_HEADS, A_HD)
    a_k_sample = kband[:, dec_t:].reshape(1, dec_batch, past_a, A_KV_HEADS, A_HD)
    a_v_sample = vband[:, dec_t:].reshape(1, dec_batch, past_a, A_KV_HEADS, A_HD)

    lam_init = 0.8 - 0.6 * float(np.exp(-0.3 * 1))
    lp = b_lambda[0].astype(F32)
    lam = jnp.exp(jnp.sum(lp[0] * lp[1])) - jnp.exp(jnp.sum(lp[2] * lp[3])) + lam_init
    wb = B_HEADS * 2 * B_HD
    col_scale = jnp.where(jnp.arange(3 * wb) < wb, B_Q_SCALE, 1.0).astype(F32)
    h, qkvh, k_bp, k_bs, v_bp, v_bs, vt = norm_proj(
        h, None, norm_mix[1], (w_b_qkv[0] * col_scale).astype(BF16), None,
        [(0, 3 * wb, "heads", BF16), (wb, 2 * wb, "split", F32), (2 * wb, 3 * wb, "split", F32),
         (2 * wb, 3 * wb, "heads_t", BF16)], tr, n_p, og=og, meta=meta)
    qb_b = min(256, t)
    o_bp = attn_b_prompt(qkvh, vt, lam, b_subln[0], lam_init, batch, t, qb_b)
    o_bs = attn_b_sample(qkvh, cache_b_k[0], cache_b_v[0], lam, b_subln[0], lam_init, n_p, dec_batch, dec_t)
    h, xn, meta, route, counts = out_router(o_bp, o_bs, h, None, w_b_o[0].astype(BF16), norm_ffn[1], w_route_group[1],
                                            b_route_group[1], w_route_expert[1], b_route_expert[1], True, tr)
    og = moe_experts(xn, route, counts, w_gate, w_up, w_down, 1, tm)

    y_p, y_s = final_norm(h, og, meta, norm_final, n_p, tr)
    hd2 = 2 * B_HD
    return (y_p.reshape(batch, t, d), y_s.reshape(dec_batch, dec_t, d),
            a_k_prompt, a_v_prompt, a_k_sample, a_v_sample,
            k_bp.reshape(1, batch, t, B_HEADS, hd2), v_bp.reshape(1, batch, t, B_HEADS, hd2),
            k_bs.reshape(1, dec_batch, dec_t, B_HEADS, hd2), v_bs.reshape(1, dec_batch, dec_t, B_HEADS, hd2))
```

```python
import functools

import jax
import jax.numpy as jnp
import numpy as np
from jax import lax
from jax.experimental import pallas as pl
from jax.experimental.pallas import tpu as pltpu
from jax.experimental.pallas import tpu_sc as plsc

F32 = jnp.float32
BF16 = jnp.bfloat16
NEG_INF = float("-inf")

CHUNK = 64
WIN_CHUNKS = 2
A_HEADS, A_KV_HEADS, A_GROUP, A_HD = 16, 4, 4, 64
B_HEADS, B_HD = 8, 64
N_GROUPS, EXPERTS_PER_GROUP = 4, 8
N_EXPERTS = N_GROUPS * EXPERTS_PER_GROUP
NORM_EPS = 1e-6
SUBLN_EPS = 1e-5

LANES = 128
COL_CHUNK = 512
PAIR_UNROLL = 6
HEAD_GROUP = 2
ONES_ROWS = 16
SC_WINDOW = 128
SC_ROW_WORDS = 256
VMEM_LIMIT = 48 * 1024 * 1024


def _cparams(sem, flags=None):
    return pltpu.CompilerParams(dimension_semantics=sem, vmem_limit_bytes=VMEM_LIMIT, flags=flags)


def _alibi(n):
    return 2.0 ** (-8.0 * np.arange(1, n + 1) / n)


def _store_planes(ref, x):
    w = ref.shape[2]
    for p in range(ref.shape[0]):
        ref[p] = x[:, p * w:(p + 1) * w]


def _join_planes(ref, first, count):
    return jnp.concatenate([ref[first + p] for p in range(count)], axis=1)


def _add_expert_outputs(x, og_ref, meta_ref):
    half = og_ref.shape[1] // 2
    meta = meta_ref[...]
    return (x + meta[:, 2:3] * _unpack_pairs(og_ref[:, :half])) + meta[:, 3:4] * _unpack_pairs(og_ref[:, half:])


def _norm_proj_kernel(n_prompt_blocks, two_src, has_bias, combine, outs, *refs):
    refs = list(refs)
    xa_ref = refs.pop(0)
    xb_ref = refs.pop(0) if two_src else None
    og_ref, meta_ref = (refs.pop(0), refs.pop(0)) if combine else (None, None)
    g_ref = refs.pop(0)
    w_ref = refs.pop(0)
    b_ref = refs.pop(0) if has_bias else None
    out_refs = refs
    x = xa_ref[...]
    if two_src:
        x = jnp.where(pl.program_id(0) < n_prompt_blocks, x, xb_ref[...])
    if combine:
        x = _add_expert_outputs(x, og_ref, meta_ref)
        out_refs.pop(0)[...] = x
    ms = jnp.mean(x * x, axis=-1, keepdims=True)
    xn = (x * lax.rsqrt(ms + NORM_EPS) * g_ref[...]).astype(BF16)
    is_prompt = pl.program_id(0) < n_prompt_blocks
    out_refs = list(out_refs)
    targets = []
    for c0, c1, layout in outs:
        targets.append((c0, c1, layout, out_refs.pop(0), out_refs.pop(0) if layout == "split" else None))
    for s in range(0, w_ref.shape[1], COL_CHUNK):
        e = s + COL_CHUNK
        r = jnp.dot(xn, w_ref[:, s:e], preferred_element_type=F32)
        if has_bias:
            r = r + b_ref[:, s:e]
        for c0, c1, layout, o_ref, o2_ref in targets:
            if not (c0 <= s and e <= c1):
                continue
            rr = r.astype(o_ref.dtype)
            if layout == "flat":
                o_ref[:, s - c0:e - c0] = rr
            elif layout == "split":

                @pl.when(is_prompt)
                def _(rr=rr, o_ref=o_ref, s=s, e=e, c0=c0):
                    o_ref[:, s - c0:e - c0] = rr

                @pl.when(jnp.logical_not(is_prompt))
                def _(rr=rr, o2_ref=o2_ref, s=s, e=e, c0=c0):
                    o2_ref[:, s - c0:e - c0] = rr
            elif layout == "heads":
                for hh in range((e - s) // LANES):
                    o_ref[(s - c0) // LANES + hh] = rr[:, hh * LANES:(hh + 1) * LANES]
            else:
                ones = jnp.ones((ONES_ROWS, r.shape[0]), o_ref.dtype)
                for hh in range((e - s) // LANES):
                    head = (s - c0) // LANES + hh
                    o_ref[head, :LANES, :] = r[:, hh * LANES:(hh + 1) * LANES].T.astype(o_ref.dtype)
                    o_ref[head, LANES:, :] = ones


def norm_proj(xa, xb, g, w, b, outs, tr, n_prompt, og=None, meta=None):
    na, d = xa.shape
    nb = 0 if xb is None else xb.shape[0]
    n = na + nb
    npb = n_prompt // tr
    two = xb is not None
    combine = og is not None
    in_specs = [pl.BlockSpec((tr, d), (lambda i: (jnp.minimum(i, npb - 1), 0)) if two else (lambda i: (i, 0)))]
    args = [xa]
    if two:
        in_specs.append(pl.BlockSpec((tr, d), lambda i: (jnp.maximum(i - npb, 0), 0)))
        args.append(xb)
    if combine:
        in_specs += [pl.BlockSpec((tr, og.shape[1]), lambda i: (i, 0)), pl.BlockSpec((tr, LANES), lambda i: (i, 0))]
        args += [og, meta]
    in_specs.append(pl.BlockSpec((1, d), lambda i: (0, 0)))
    args.append(g.reshape(1, d).astype(F32))
    in_specs.append(pl.BlockSpec(w.shape, lambda i: (0, 0)))
    args.append(w)
    if b is not None:
        in_specs.append(pl.BlockSpec((1, w.shape[1]), lambda i: (0, 0)))
        args.append(b.reshape(1, -1).astype(F32))
    out_shapes, out_specs = [], []
    if combine:
        out_shapes.append(jax.ShapeDtypeStruct((n, d), F32))
        out_specs.append(pl.BlockSpec((tr, d), lambda i: (i, 0)))
    for c0, c1, layout, dt in outs:
        assert c0 % COL_CHUNK == 0 and c1 % COL_CHUNK == 0
        if layout == "flat":
            out_shapes.append(jax.ShapeDtypeStruct((n, c1 - c0), dt))
            out_specs.append(pl.BlockSpec((tr, c1 - c0), lambda i: (i, 0)))
        elif layout == "split":
            out_shapes.append(jax.ShapeDtypeStruct((n_prompt, c1 - c0), dt))
            out_specs.append(pl.BlockSpec((tr, c1 - c0), lambda i: (jnp.minimum(i, npb - 1), 0)))
            out_shapes.append(jax.ShapeDtypeStruct((n - n_prompt, c1 - c0), dt))
            out_specs.append(pl.BlockSpec((tr, c1 - c0), lambda i: (jnp.maximum(i - npb, 0), 0)))
        elif layout == "heads":
            nh = (c1 - c0) // LANES
            out_shapes.append(jax.ShapeDtypeStruct((nh, n, LANES), dt))
            out_specs.append(pl.BlockSpec((nh, tr, LANES), lambda i: (0, i, 0)))
        else:
            assert layout == "heads_t"
            nh = (c1 - c0) // LANES
            out_shapes.append(jax.ShapeDtypeStruct((nh, LANES + ONES_ROWS, n), dt))
            out_specs.append(pl.BlockSpec((nh, LANES + ONES_ROWS, tr), lambda i: (0, 0, i)))
    kern = functools.partial(_norm_proj_kernel, npb, two, b is not None, combine,
                             [(c0, c1, lay) for c0, c1, lay, _ in outs])
    return pl.pallas_call(
        kern, grid=(n // tr,), in_specs=in_specs, out_specs=out_specs, out_shape=out_shapes,
        compiler_params=_cparams(("arbitrary",)), name="norm_proj")(*args)


def _attn_a_chunk(q, k, v, bias_ref, sink_ref, valid):
    nq = q.shape[0]
    outs = []
    for kh in range(A_KV_HEADS):
        qs = jnp.concatenate(
            [q[:, (kh * A_GROUP + g) * A_HD:(kh * A_GROUP + g + 1) * A_HD] for g in range(A_GROUP)], axis=0)
        kk = k[:, kh * A_HD:(kh + 1) * A_HD]
        s = lax.dot_general(qs, kk, (((1,), (1,)), ((), ())), preferred_element_type=F32) + bias_ref[kh]
        if valid is not None:
            s = jnp.where(valid, s, NEG_INF)
        sk = sink_ref[kh]
        m = jnp.maximum(jnp.max(s, axis=-1, keepdims=True), sk)
        e = jnp.exp2(s - m)
        den = jnp.sum(e, axis=-1, keepdims=True) + jnp.exp2(sk - m)
        p = (e * (1.0 / den)).astype(BF16)
        o = jnp.dot(p, v[:, kh * A_HD:(kh + 1) * A_HD], preferred_element_type=F32)
        for g in range(A_GROUP):
            outs.append(o[g * nq:(g + 1) * nq])
    return jnp.concatenate(outs, axis=1)


def _attn_a_prompt_kernel(qb, q_ref, kvc_ref, kvp_ref, bias_ref, sink_ref, o_ref, t_ref, p_ref):
    cb = pl.program_id(1)
    back = CHUNK * WIN_CHUNKS
    nkv = A_KV_HEADS * A_HD
    kvfull = jnp.concatenate([kvp_ref[...], kvc_ref[...]], axis=0)
    k_all = kvfull[:, :nkv].astype(BF16)
    vt_all = kvfull[:, nkv:].T.astype(BF16)
    key_pos = lax.broadcasted_iota(jnp.int32, (back + qb, 1), 0) + (cb * qb - back)
    before_start = jnp.where(key_pos >= 0, 0.0, NEG_INF)
    dn = (((1,), (1,)), ((), ()))
    for kh in range(A_KV_HEADS):
        qs = jnp.concatenate(
            [q_ref[:, (kh * A_GROUP + g) * A_HD:(kh * A_GROUP + g + 1) * A_HD] for g in range(A_GROUP)], axis=0)
        t_ref[kh] = lax.dot_general(k_all[:, kh * A_HD:(kh + 1) * A_HD], qs, dn, preferred_element_type=F32)
    inv_dens = []
    for kh in range(A_KV_HEADS):
        st = t_ref[kh] + bias_ref[kh] + before_start
        sk = sink_ref[kh]
        m = jnp.maximum(jnp.max(st, axis=0, keepdims=True), sk)
        e = jnp.exp2(st - m)
        inv_dens.append(1.0 / (jnp.sum(e, axis=0, keepdims=True) + jnp.exp2(sk - m)))
        p_ref[kh] = e.astype(BF16)
    outs = []
    for kh in range(A_KV_HEADS):
        ot = jnp.dot(vt_all[kh * A_HD:(kh + 1) * A_HD, :], p_ref[kh], preferred_element_type=F32)
        o = (ot * inv_dens[kh]).T
        for g in range(A_GROUP):
            outs.append(o[g * qb:(g + 1) * qb])
    o_ref[...] = jnp.concatenate(outs, axis=1).astype(o_ref.dtype)


def _a_prompt_tables(qb, sinks):
    back = CHUNK * WIN_CHUNKS
    slopes = _alibi(A_HEADS).reshape(A_KV_HEADS, A_GROUP)
    qpos = np.arange(qb)
    kpos = np.arange(back + qb) - back
    dist = np.abs(kpos[:, None] - qpos[None, :]).astype(np.float64)
    band_lo = (qpos // CHUNK) * CHUNK - back
    in_band = (kpos[:, None] >= band_lo[None, :]) & (kpos[:, None] < band_lo[None, :] + back + CHUNK)
    bias = np.where(in_band[None, None], -slopes[:, :, None, None] * dist[None, None], -np.inf)
    bias = np.transpose(bias, (0, 2, 1, 3)).reshape(A_KV_HEADS, back + qb, A_GROUP * qb) * LOG2E
    sk = jnp.repeat(sinks.astype(F32).reshape(A_KV_HEADS, A_GROUP), qb, axis=1)[:, None, :] * LOG2E
    return jnp.asarray(bias, F32), sk


def _a_tables(qpos, kpos, sinks):
    slopes = _alibi(A_HEADS).reshape(A_KV_HEADS, A_GROUP)
    dist = np.abs(qpos[:, None] - kpos[None, :]).astype(np.float64)
    bias = -slopes[:, :, None, None] * dist
    nq = len(qpos)
    bias = jnp.asarray(bias.reshape(A_KV_HEADS, A_GROUP * nq, len(kpos)) * LOG2E, F32)
    sk = jnp.repeat(sinks.astype(F32).reshape(A_KV_HEADS, A_GROUP), nq, axis=1)[..., None] * LOG2E
    return bias, sk


def attn_a_prompt(q_all, kv_all, sinks, batch, t, qb):
    nq = A_HEADS * A_HD
    nkv2 = 2 * A_KV_HEADS * A_HD
    back = CHUNK * WIN_CHUNKS
    bias, sk = _a_prompt_tables(qb, sinks)
    nblk = t // qb
    r = qb // back
    return pl.pallas_call(
        functools.partial(_attn_a_prompt_kernel, qb),
        grid=(batch, nblk),
        in_specs=[
            pl.BlockSpec((qb, nq), lambda b, i: (b * nblk + i, 0)),
            pl.BlockSpec((qb, nkv2), lambda b, i: (b * nblk + i, 0)),
            pl.BlockSpec((back, nkv2), lambda b, i: (jnp.maximum((b * nblk + i) * r - 1, 0), 0)),
            pl.BlockSpec(bias.shape, lambda b, i: (0, 0, 0)),
            pl.BlockSpec(sk.shape, lambda b, i: (0, 0, 0)),
        ],
        out_specs=pl.BlockSpec((qb, nq), lambda b, i: (b * nblk + i, 0)),
        out_shape=jax.ShapeDtypeStruct((batch * t, nq), BF16),
        scratch_shapes=[pltpu.VMEM((A_KV_HEADS, back + qb, A_GROUP * qb), F32),
                        pltpu.VMEM((A_KV_HEADS, back + qb, A_GROUP * qb), BF16)],
        compiler_params=_cparams(("arbitrary", "arbitrary")), name="attn_a_prompt",
    )(q_all, kv_all, kv_all, bias, sk)


def _attn_a_sample_kernel(q_ref, k_ref, v_ref, bias_ref, sink_ref, o_ref):
    o = _attn_a_chunk(q_ref[...], k_ref[...].astype(BF16), v_ref[...].astype(BF16), bias_ref, sink_ref, None)
    o_ref[...] = o.astype(o_ref.dtype)


def attn_a_sample(q_all, kband, vband, sinks, row0, dec_batch, dec_t):
    nq = A_HEADS * A_HD
    s = kband.shape[1]
    past = s - dec_t
    kpos = np.concatenate([np.arange(past) - past, np.arange(dec_t)])
    bias, sk = _a_tables(np.arange(dec_t), kpos, sinks)
    blk0 = row0 // dec_t
    return pl.pallas_call(
        _attn_a_sample_kernel,
        grid=(dec_batch,),
        in_specs=[
            pl.BlockSpec((dec_t, nq), lambda b: (blk0 + b, 0)),
            pl.BlockSpec((None, s, kband.shape[2]), lambda b: (b, 0, 0)),
            pl.BlockSpec((None, s, vband.shape[2]), lambda b: (b, 0, 0)),
            pl.BlockSpec(bias.shape, lambda b: (0, 0, 0)),
            pl.BlockSpec(sk.shape, lambda b: (0, 0, 0)),
        ],
        out_specs=pl.BlockSpec((dec_t, nq), lambda b: (b, 0)),
        out_shape=jax.ShapeDtypeStruct((dec_batch * dec_t, nq), BF16),
        compiler_params=_cparams(("arbitrary",)), name="attn_a_sample",
    )(q_all, kband, vband, bias, sk)


B_Q_SCALE = B_HD ** -0.5 * float(np.log2(np.e))
LOG2E = float(np.log2(np.e))


def _stack_maps(qh):
    lane = lax.broadcasted_iota(jnp.int32, qh.shape, 1)
    zero = jnp.zeros_like(qh)
    return jnp.concatenate([jnp.where(lane < B_HD, qh, zero), jnp.where(lane >= B_HD, qh, zero)], axis=0)


def _diff_finish(acc, l, nq, lam, lam_init, subln):
    inv = 1.0 / l
    o = acc[:nq] * inv[:nq] - lam * (acc[nq:] * inv[nq:])
    ms = jnp.mean(o * o, axis=-1, keepdims=True)
    return o * lax.rsqrt(ms + SUBLN_EPS) * subln * (1.0 - lam_init)


def _attn_b_prompt_kernel(qb, nb, lam_init, pair_ref, slope_ref, lam_ref, q_ref, k_ref, vt_ref, kbias_ref,
                          subln_ref, o_ref, acc_ref, m_ref, al_ref, qq_ref, t0_ref, t1_ref, p0_ref, p1_ref):
    g = pl.program_id(1)
    lam = lam_ref[0]
    dn = (((1,), (1,)), ((), ()))
    t_refs = (t0_ref, t1_ref)
    p_refs = (p0_ref, p1_ref)
    heads = range(HEAD_GROUP)
    n_pairs = nb * (nb + 1) // 2
    slopes = [slope_ref[g * HEAD_GROUP + u] for u in heads]

    def pair(n):
        return pair_ref[2 * n], pair_ref[2 * n + 1]

    def scores(n, slot):
        j, iq = pair(n)
        ks = pl.multiple_of(j * qb, qb)
        for u in heads:
            t_refs[slot][u] = lax.dot_general(k_ref[u, pl.ds(ks, qb), :], qq_ref[u * nb + iq], dn,
                                              preferred_element_type=F32)

    def softmax_step(n, slot):
        j, iq = pair(n)
        on_diag = (j == iq).astype(jnp.int32)
        for u in heads:
            st = u * nb + iq
            m = m_ref[st, 0:1, :]
            t = t_refs[slot][u] + kbias_ref[u, on_diag]
            off = slopes[u] * jnp.full((1, 2 * qb), (j - iq) * qb, jnp.int32).astype(F32)
            m_new = jnp.maximum(m, jnp.max(t, axis=0, keepdims=True) + off)
            p_refs[slot][u] = jnp.exp2(t + (off - m_new)).astype(BF16)
            al_ref[slot * HEAD_GROUP + u, 0:1, :] = jnp.exp2(m - m_new)
            m_ref[st, 0:1, :] = m_new

    def accumulate(n, slot):
        j, iq = pair(jnp.maximum(n, 0))
        ks = pl.multiple_of(j * qb, qb)
        for u in heads:
            st = u * nb + iq
            pv = jnp.dot(vt_ref[u, :, pl.ds(ks, qb)], p_refs[slot][u], preferred_element_type=F32)
            acc_ref[st] = al_ref[slot * HEAD_GROUP + u, 0:1, :] * acc_ref[st] + pv

    def step(n, slot):
        softmax_step(n, slot)
        accumulate(n - 1, 1 - slot)
        scores(jnp.minimum(n + 1, n_pairs - 1), 1 - slot)
        return 0

    acc_ref[...] = jnp.zeros_like(acc_ref)
    m_ref[...] = jnp.full(m_ref.shape, NEG_INF, F32)
    al_ref[...] = jnp.ones_like(al_ref)
    p1_ref[...] = jnp.zeros_like(p1_ref)
    for u in heads:
        for iq in range(nb):
            qq_ref[u * nb + iq] = _stack_maps(q_ref[u, iq * qb:(iq + 1) * qb, :])
    scores(0, 0)
    def steps(k, c):
        for r in range(PAIR_UNROLL):
            step(PAIR_UNROLL * k + r, r % 2)
        return c

    lax.fori_loop(0, n_pairs // PAIR_UNROLL, steps, 0)
    for n in range(n_pairs - n_pairs % PAIR_UNROLL, n_pairs):
        step(n, n % 2)
    accumulate(n_pairs - 1, (n_pairs - 1) % 2)
    for u in heads:
        for iq in range(nb):
            inv = 1.0 / acc_ref[u * nb + iq, LANES:LANES + 1, :]
            acc = acc_ref[u * nb + iq, :LANES, :]
            ot = acc[:, :qb] * inv[:, :qb] - lam * (acc[:, qb:] * inv[:, qb:])
            ms = jnp.mean(ot * ot, axis=0, keepdims=True)
            ot = ot * lax.rsqrt(ms + SUBLN_EPS) * subln_ref[...] * (1.0 - lam_init)
            o_ref[u, iq * qb:(iq + 1) * qb, :] = ot.T.astype(o_ref.dtype)


def attn_b_prompt(qkvh, vt, lam, subln, lam_init, batch, t, qb):
    nblk = t // qb
    qq = np.arange(qb)
    slope2 = (_alibi(B_HEADS) * LOG2E)[:, None, None]
    kcol = np.broadcast_to(qq[:, None], (qb, qb)).astype(np.float64)
    tri = -2.0 * np.maximum(qq[:, None] - qq[None, :], 0)
    msk = np.where((qq[:, None] // CHUNK) <= (qq[None, :] // CHUNK), 0.0, -np.inf)
    both = lambda a: np.concatenate([a, a], axis=-1)
    kbias = np.stack([both(slope2 * kcol[None]), both(slope2 * (kcol + tri)[None] + msk[None])], axis=1)
    kbias = jnp.asarray(kbias, F32)
    slopes = jnp.asarray(_alibi(B_HEADS) * LOG2E, F32)
    subln_t = jnp.broadcast_to(subln.astype(F32).reshape(LANES, 1), (LANES, qb))
    pairs = jnp.asarray([v for j in range(nblk) for i in range(j, nblk) for v in (j, i)], jnp.int32)
    hg = HEAD_GROUP
    n_groups = B_HEADS // hg
    grid_spec = pltpu.PrefetchScalarGridSpec(
        num_scalar_prefetch=3, grid=(batch, n_groups),
        in_specs=[
            pl.BlockSpec((hg, t, LANES), lambda b, g, *_: (g, b, 0)),
            pl.BlockSpec((hg, t, LANES), lambda b, g, *_: (n_groups + g, b, 0)),
            pl.BlockSpec((hg, LANES + ONES_ROWS, t), lambda b, g, *_: (g, 0, b)),
            pl.BlockSpec((hg,) + kbias.shape[1:], lambda b, g, *_: (g, 0, 0, 0)),
            pl.BlockSpec(subln_t.shape, lambda b, g, *_: (0, 0)),
        ],
        out_specs=pl.BlockSpec((hg, t, LANES), lambda b, g, *_: (g, b, 0)),
        scratch_shapes=[pltpu.VMEM((hg * nblk, LANES + ONES_ROWS, 2 * qb), F32),
                        pltpu.VMEM((hg * nblk, 8, 2 * qb), F32),
                        pltpu.VMEM((2 * hg, 8, 2 * qb), F32),
                        pltpu.VMEM((hg * nblk, 2 * qb, LANES), BF16),
                        pltpu.VMEM((hg, qb, 2 * qb), F32), pltpu.VMEM((hg, qb, 2 * qb), F32),
                        pltpu.VMEM((hg, qb, 2 * qb), BF16), pltpu.VMEM((hg, qb, 2 * qb), BF16)],
    )
    return pl.pallas_call(
        functools.partial(_attn_b_prompt_kernel, qb, nblk, lam_init), grid_spec=grid_spec,
        out_shape=jax.ShapeDtypeStruct((B_HEADS, batch * t, LANES), BF16),
        compiler_params=_cparams(("arbitrary", "arbitrary")), name="attn_b_prompt",
    )(pairs, slopes, lam.reshape(1).astype(F32), qkvh, qkvh, vt, kbias, subln_t)


def _attn_b_sample_kernel(dec_t, past, lam_init, slopes, lam_ref, q_ref, kn_ref, vn_ref, ck_ref, cv_ref,
                          tnew_ref, subln_ref, o_ref):
    lam = lam_ref[0]
    subln = subln_ref[...]
    colc = lax.broadcasted_iota(jnp.int32, (1, past), 1).astype(F32) - float(past)
    for h in range(B_HEADS):
        slope = float(slopes[h]) * LOG2E
        qq = _stack_maps(q_ref[h])
        kc = ck_ref[pl.ds(h, past, stride=B_HEADS), :].astype(BF16)
        vc = cv_ref[pl.ds(h, past, stride=B_HEADS), :].astype(BF16)
        dn = (((1,), (1,)), ((), ()))
        s_c = lax.dot_general(qq, kc, dn, preferred_element_type=F32) + slope * colc
        s_n = lax.dot_general(qq, kn_ref[h], dn, preferred_element_type=F32) + slope * tnew_ref[...]
        m = jnp.maximum(jnp.max(s_c, axis=-1, keepdims=True), jnp.max(s_n, axis=-1, keepdims=True))
        p_c = jnp.exp2(s_c - m)
        p_n = jnp.exp2(s_n - m)
        l = jnp.sum(p_c, axis=-1, keepdims=True) + jnp.sum(p_n, axis=-1, keepdims=True)
        acc = (jnp.dot(p_c.astype(BF16), vc, preferred_element_type=F32)
               + jnp.dot(p_n.astype(BF16), vn_ref[h], preferred_element_type=F32))
        o_ref[h] = _diff_finish(acc, l, dec_t, lam, lam_init, subln).astype(o_ref.dtype)


def attn_b_sample(qkvh, cache_k, cache_v, lam, subln, lam_init, row0, dec_batch, dec_t):
    past = cache_k.shape[1]
    qi = np.arange(dec_t)
    tnew = (qi[:, None] - np.abs(qi[:, None] - qi[None, :])).astype(np.float32)
    tnew = jnp.asarray(np.concatenate([tnew, tnew], axis=0))
    blk0 = row0 // dec_t
    rows = past * B_HEADS
    cache_k = cache_k.reshape(dec_batch, rows, LANES)
    cache_v = cache_v.reshape(dec_batch, rows, LANES)
    grid_spec = pltpu.PrefetchScalarGridSpec(
        num_scalar_prefetch=1, grid=(dec_batch,),
        in_specs=[
            pl.BlockSpec((B_HEADS, dec_t, LANES), lambda b, *_: (0, blk0 + b, 0)),
            pl.BlockSpec((B_HEADS, dec_t, LANES), lambda b, *_: (1, blk0 + b, 0)),
            pl.BlockSpec((B_HEADS, dec_t, LANES), lambda b, *_: (2, blk0 + b, 0)),
            pl.BlockSpec((None, rows, LANES), lambda b, *_: (b, 0, 0)),
            pl.BlockSpec((None, rows, LANES), lambda b, *_: (b, 0, 0)),
            pl.BlockSpec(tnew.shape, lambda b, *_: (0, 0)),
            pl.BlockSpec((1, LANES), lambda b, *_: (0, 0)),
        ],
        out_specs=pl.BlockSpec((B_HEADS, dec_t, LANES), lambda b, *_: (0, b, 0)),
    )
    return pl.pallas_call(
        functools.partial(_attn_b_sample_kernel, dec_t, past, lam_init, tuple(_alibi(B_HEADS))),
        grid_spec=grid_spec,
        out_shape=jax.ShapeDtypeStruct((B_HEADS, dec_batch * dec_t, LANES), BF16),
        compiler_params=_cparams(("arbitrary",)), name="attn_b_sample",
    )(lam.reshape(1).astype(F32), qkvh, qkvh, qkvh, cache_k, cache_v, tnew, subln.reshape(1, LANES).astype(F32))


ROUTE_ROWS = 40
ROUTE_FIELDS = 8


def _route_t(lt):
    row = lax.broadcasted_iota(jnp.int32, lt.shape, 0).astype(F32)
    big = float(LANES)
    lg = jnp.where(row < N_GROUPS, lt, NEG_INF)
    gmax = jnp.max(lg, axis=0, keepdims=True)
    g_idx = jnp.min(jnp.where(lg == gmax, row, big), axis=0, keepdims=True)
    g_prob = 1.0 / jnp.sum(jnp.exp(lg - gmax), axis=0, keepdims=True)
    lo = N_GROUPS + EXPERTS_PER_GROUP * g_idx
    le = jnp.where((row >= lo) & (row < lo + EXPERTS_PER_GROUP), lt, NEG_INF)
    m1 = jnp.max(le, axis=0, keepdims=True)
    i1 = jnp.min(jnp.where(le == m1, row, big), axis=0, keepdims=True)
    le2 = jnp.where(row == i1, NEG_INF, le)
    m2 = jnp.max(le2, axis=0, keepdims=True)
    i2 = jnp.min(jnp.where(le2 == m2, row, big), axis=0, keepdims=True)
    t = jnp.exp(m2 - m1)
    ga = 1.0 / (1.0 + t)
    gb = t * ga
    return row, i1, i2, g_prob * ga, g_prob * gb


def _unpack_pairs(u):
    lo = pltpu.bitcast(u << 16, F32)
    hi = pltpu.bitcast(u & jnp.uint32(0xFFFF0000), F32)
    return jnp.concatenate([lo, hi], axis=1)


def _pack_pairs(x):
    bits = pltpu.bitcast(x.astype(BF16).astype(F32), jnp.uint32)
    w = x.shape[1] // 2
    return (bits[:, :w] >> 16) | bits[:, w:]


def _out_router_kernel(n_prompt_blocks, heads_in, two_o, two_res, *refs):
    refs = list(refs)
    oa_ref = refs.pop(0)
    ob_ref = refs.pop(0) if two_o else None
    ra_ref = refs.pop(0)
    rb_ref = refs.pop(0) if two_res else None
    (wo_ref, g_ref, wr_ref, wr2_ref, br_ref, utri_ref, h_ref, xn_ref, meta_ref, route_ref, cnt_ref,
     carry_ref) = refs
    is_prompt = pl.program_id(0) < n_prompt_blocks

    def load_o(ref):
        if heads_in:
            return jnp.concatenate([ref[hh] for hh in range(ref.shape[0])], axis=1)
        return ref[...]

    o = load_o(oa_ref)
    if two_o:
        o = jnp.where(is_prompt, o, load_o(ob_ref))
    res = ra_ref[...]
    if two_res:
        res = jnp.where(is_prompt, res, rb_ref[...])
    h = res + jnp.dot(o, wo_ref[...], preferred_element_type=F32)
    h_ref[...] = h
    ms = jnp.mean(h * h, axis=-1, keepdims=True)
    xn = h * lax.rsqrt(ms + NORM_EPS) * g_ref[...]
    x_hi = xn.astype(BF16)
    _store_planes(xn_ref, _pack_pairs(xn))
    x_lo = (xn - x_hi.astype(F32)).astype(BF16)
    r = jnp.dot(x_hi, wr_ref[...], preferred_element_type=F32)
    logits = (r[:, :LANES] + r[:, LANES:]) + jnp.dot(x_lo, wr2_ref[...], preferred_element_type=F32) + br_ref[...]
    row, i1, i2, g1, g2 = _route_t(logits.T[:ROUTE_ROWS])

    @pl.when(pl.program_id(0) == 0)
    def _():
        carry_ref[...] = jnp.zeros_like(carry_ref)

    sel1 = row == i1
    sel2 = row == i2
    onehot = jnp.where(sel1 | sel2, 1.0, 0.0)
    before = jnp.dot(onehot.astype(BF16), utri_ref[...], preferred_element_type=F32) + carry_ref[:, 0:1]
    rank1 = jnp.sum(jnp.where(sel1, before, 0.0), axis=0, keepdims=True)
    rank2 = jnp.sum(jnp.where(sel2, before, 0.0), axis=0, keepdims=True)
    carry_ref[...] = carry_ref[...] + jnp.sum(onehot, axis=1, keepdims=True)
    cnt_ref[...] = carry_ref[...]
    fields = [i1 - N_GROUPS, i2 - N_GROUPS, g1, g2, rank1, rank2]
    fields += [jnp.zeros_like(g1)] * (ROUTE_FIELDS - len(fields))
    route = jnp.concatenate(fields, axis=0)
    route_ref[...] = route
    meta_ref[...] = jnp.concatenate([route, jnp.zeros((LANES - ROUTE_FIELDS, route.shape[1]), F32)], axis=0).T


def out_router(oa, ob, ra, rb, wo, g, w_rg, b_rg, w_re, b_re, heads_in, tr):
    d = ra.shape[1]
    if heads_in:
        na = oa.shape[1]
        nb = 0 if ob is None else ob.shape[1]
    else:
        na = oa.shape[0]
        nb = 0 if ob is None else ob.shape[0]
    n = na + nb
    npb = na // tr
    two_o = ob is not None
    two_res = rb is not None
    first = lambda i: (jnp.minimum(i, npb - 1), 0)
    second = lambda i: (jnp.maximum(i - npb, 0), 0)
    plain = lambda i: (i, 0)
    in_specs, args = [], []

    def add_o(x, imap):
        if heads_in:
            in_specs.append(pl.BlockSpec((x.shape[0], tr, LANES), lambda i: (0, imap(i)[0], 0)))
        else:
            in_specs.append(pl.BlockSpec((tr, x.shape[1]), imap))
        args.append(x)

    add_o(oa, first if two_o else plain)
    if two_o:
        add_o(ob, second)
    in_specs.append(pl.BlockSpec((tr, d), first if two_res else plain))
    args.append(ra)
    if two_res:
        in_specs.append(pl.BlockSpec((tr, d), second))
        args.append(rb)
    wr = jnp.zeros((d, LANES), F32).at[:, :N_GROUPS].set(w_rg.astype(F32))
    wr = wr.at[:, N_GROUPS:N_GROUPS + N_EXPERTS].set(w_re.astype(F32))
    wr_hi = wr.astype(BF16)
    wr_lo = (wr - wr_hi.astype(F32)).astype(BF16)
    br = jnp.zeros((1, LANES), F32).at[0, :N_GROUPS].set(b_rg.astype(F32))
    br = br.at[0, N_GROUPS:N_GROUPS + N_EXPERTS].set(b_re.astype(F32))
    utri = jnp.asarray(np.triu(np.ones((tr, tr), np.float32), 1), BF16)
    consts = [wo, g.reshape(1, d).astype(F32), jnp.concatenate([wr_hi, wr_lo], axis=1), wr_hi, br, utri]
    for cst in consts:
        in_specs.append(pl.BlockSpec(cst.shape, lambda i: (0, 0)))
        args.append(cst)
    parts = d // 2 // SC_ROW_WORDS
    out_shape = [jax.ShapeDtypeStruct((n, d), F32), jax.ShapeDtypeStruct((parts, n, SC_ROW_WORDS), jnp.uint32),
                 jax.ShapeDtypeStruct((n, LANES), F32), jax.ShapeDtypeStruct((ROUTE_FIELDS, n), F32),
                 jax.ShapeDtypeStruct((ROUTE_ROWS, LANES), F32)]
    out_specs = [pl.BlockSpec((tr, d), plain), pl.BlockSpec((parts, tr, SC_ROW_WORDS), lambda i: (0, i, 0)),
                 pl.BlockSpec((tr, LANES), plain), pl.BlockSpec((ROUTE_FIELDS, tr), lambda i: (0, i)),
                 pl.BlockSpec((ROUTE_ROWS, LANES), lambda i: (0, 0))]
    return pl.pallas_call(
        functools.partial(_out_router_kernel, npb, heads_in, two_o, two_res),
        grid=(n // tr,), in_specs=in_specs, out_specs=out_specs, out_shape=out_shape,
        scratch_shapes=[pltpu.VMEM((ROUTE_ROWS, LANES), F32)],
        compiler_params=_cparams(("arbitrary",)), name="out_router")(*args)


def _expert_kernel(te_ref, nt_ref, x_ref, wg_ref, wu_ref, wd_ref, o_ref):
    @pl.when(pl.program_id(0) < nt_ref[0])
    def _():
        x = _unpack_pairs(_join_planes(x_ref, 0, x_ref.shape[0])).astype(BF16)
        a = jnp.dot(x, wg_ref[...].astype(BF16), preferred_element_type=F32)
        u = jnp.dot(x, wu_ref[...].astype(BF16), preferred_element_type=F32)
        hh = (a * (1.0 / (1.0 + jnp.exp(-a))) * u).astype(BF16)
        _store_planes(o_ref, _pack_pairs(jnp.dot(hh, wd_ref[...].astype(BF16), preferred_element_type=F32)))

    @pl.when(pl.program_id(0) >= nt_ref[0])
    def _():
        o_ref[...] = jnp.zeros_like(o_ref)


def expert_mlp(xs, tile_expert, n_tiles, w_gate, w_up, w_down, layer, tm):
    parts, p, w = xs.shape
    d, f = w_gate.shape[-2:]
    epg = w_gate.shape[2]
    wmap = lambda t, te, nt: (layer, te[t] // epg, te[t] % epg, 0, 0)
    grid_spec = pltpu.PrefetchScalarGridSpec(
        num_scalar_prefetch=2, grid=(p // tm,),
        in_specs=[
            pl.BlockSpec((parts, tm, w), lambda t, te, nt: (0, t, 0)),
            pl.BlockSpec((None, None, None, d, f), wmap),
            pl.BlockSpec((None, None, None, d, f), wmap),
            pl.BlockSpec((None, None, None, f, d), wmap),
        ],
        out_specs=pl.BlockSpec((parts, tm, w), lambda t, te, nt: (0, t, 0)),
    )
    return pl.pallas_call(
        _expert_kernel, grid_spec=grid_spec, out_shape=jax.ShapeDtypeStruct((parts, p, w), jnp.uint32),
        compiler_params=_cparams(("arbitrary",)), name="expert_mlp")(tile_expert, n_tiles, xs, w_gate, w_up, w_down)


def sc_scatter_rows(x2, idx, n_out):
    mesh = plsc.VectorSubcoreMesh(core_axis_name="c", subcore_axis_name="s")
    n_src_win = x2.shape[0] // SC_WINDOW

    @functools.partial(pl.kernel, out_type=jax.ShapeDtypeStruct((n_out, SC_ROW_WORDS), x2.dtype), mesh=mesh)
    def scatter(x_hbm, i_hbm, o_hbm):
        def body(x_vmem, i_vmem):
            pltpu.sync_copy(x_vmem, o_hbm.at[i_vmem.at[0]])

        pltpu.emit_pipeline(
            body, grid=(n_out // SC_WINDOW,),
            in_specs=[pl.BlockSpec((SC_WINDOW, SC_ROW_WORDS), lambda i: (lax.rem(i, n_src_win), 0)),
                      pl.BlockSpec((1, SC_WINDOW), lambda i: (0, i))],
            out_specs=[], core_axis_name=("c", "s"), dimension_semantics=(pltpu.PARALLEL,),
        )(x_hbm, i_hbm)

    return scatter(x2, idx)


def sc_gather_rows(x2, idx):
    mesh = plsc.VectorSubcoreMesh(core_axis_name="c", subcore_axis_name="s")
    n_col, n = idx.shape

    @functools.partial(pl.kernel, out_type=jax.ShapeDtypeStruct((n, n_col * SC_ROW_WORDS), x2.dtype), mesh=mesh)
    def gather(x_hbm, i_hbm, o_hbm):
        def body(i_vmem, o_vmem):
            pltpu.sync_copy(x_hbm.at[i_vmem.at[0]], o_vmem)

        pltpu.emit_pipeline(
            body, grid=(n // SC_WINDOW, n_col),
            in_specs=[pl.BlockSpec((1, SC_WINDOW), lambda i, c: (c, i))],
            out_specs=[pl.BlockSpec((SC_WINDOW, SC_ROW_WORDS), lambda i, c: (i, c))],
            core_axis_name=("c", "s"), dimension_semantics=(pltpu.PARALLEL, pltpu.ARBITRARY),
        )(i_hbm, o_hbm)

    return gather(x2, idx)


def moe_dispatch(xn_packed, route, counts, tm):
    parts, n, w = xn_packed.shape
    e = route[0:2].astype(jnp.int32)
    rank = route[4:6].astype(jnp.int32)
    counts = counts[N_GROUPS:N_GROUPS + N_EXPERTS, 0].astype(jnp.int32)
    padded = ((counts + tm - 1) // tm) * tm
    pend = jnp.cumsum(padded)
    pstart = pend - padded
    ids = jnp.arange(N_EXPERTS, dtype=jnp.int32)
    pos = jnp.sum(jnp.where(e[..., None] == ids, pstart, 0), axis=-1) + rank
    p = 2 * n + N_EXPERTS * tm
    n_pad = p - 2 * n
    gap_len = jnp.concatenate([padded - counts, (p - pend[-1]).reshape(1)])
    gap_first = jnp.concatenate([pstart + counts, pend[-1:]])
    gap_end = jnp.cumsum(gap_len)
    k = jnp.arange(n_pad, dtype=jnp.int32)
    gap = jnp.sum((k[:, None] >= gap_end[None, :]).astype(jnp.int32), axis=1)
    shift = gap_first - (gap_end - gap_len)
    pad_pos = k + jnp.sum(jnp.where(gap[:, None] == jnp.arange(N_EXPERTS + 1), shift, 0), axis=-1)
    planes = lambda rows: jnp.concatenate([rows + q * p for q in range(parts)])
    scat_idx = jnp.concatenate([planes(pos[0]), planes(pos[1]), planes(pad_pos)]).reshape(1, parts * p)
    gath_idx = jnp.stack([pos[k] + q * p for k in range(2) for q in range(parts)])
    xs = sc_scatter_rows(xn_packed.reshape(parts * n, w), scat_idx, parts * p).reshape(parts, p, w)
    tile_start = jnp.arange(p // tm, dtype=jnp.int32) * tm
    tile_expert = jnp.sum((tile_start[:, None] >= pend[None, :]).astype(jnp.int32), axis=1)
    tile_expert = jnp.minimum(tile_expert, N_EXPERTS - 1)
    n_tiles = (pend[-1] // tm).astype(jnp.int32).reshape(1)
    return xs, tile_expert, n_tiles, gath_idx


def moe_experts(xn_packed, route, counts, w_gate, w_up, w_down, layer, tm):
    parts, n, w = xn_packed.shape
    xs, tile_expert, n_tiles, gath_idx = moe_dispatch(xn_packed, route, counts, tm)
    o = expert_mlp(xs, tile_expert, n_tiles, w_gate, w_up, w_down, layer, tm)
    return sc_gather_rows(o.reshape(parts * o.shape[1], w), gath_idx)


def _final_norm_kernel(n_prompt_blocks, h_ref, o0_ref, o1_ref, meta_ref, g_ref, yp_ref, ys_ref):
    meta = meta_ref[...]
    x = (h_ref[...] + meta[:, 2:3] * _unpack_pairs(o0_ref[...])) + meta[:, 3:4] * _unpack_pairs(o1_ref[...])
    ms = jnp.mean(x * x, axis=-1, keepdims=True)
    y = x * lax.rsqrt(ms + NORM_EPS) * g_ref[...]
    i = pl.program_id(0)

    @pl.when(i < n_prompt_blocks)
    def _():
        yp_ref[...] = y

    @pl.when(i >= n_prompt_blocks)
    def _():
        ys_ref[...] = y


def final_norm(h, og, meta, g, n_prompt, tr):
    n, d = h.shape
    npb = n_prompt // tr
    return pl.pallas_call(
        functools.partial(_final_norm_kernel, npb),
        grid=(n // tr,),
        in_specs=[pl.BlockSpec((tr, d), lambda i: (i, 0)),
                  pl.BlockSpec((tr, og.shape[1] // 2), lambda i: (i, 0)),
                  pl.BlockSpec((tr, og.shape[1] // 2), lambda i: (i, 1)),
                  pl.BlockSpec((tr, LANES), lambda i: (i, 0)), pl.BlockSpec((1, d), lambda i: (0, 0))],
        out_specs=[pl.BlockSpec((tr, d), lambda i: (jnp.minimum(i, npb - 1), 0)),
                   pl.BlockSpec((tr, d), lambda i: (jnp.maximum(i - npb, 0), 0))],
        out_shape=[jax.ShapeDtypeStruct((n_prompt, d), F32), jax.ShapeDtypeStruct((n - n_prompt, d), F32)],
        compiler_params=_cparams(("arbitrary",)), name="final_norm")(h, og, og, meta, g.reshape(1, d).astype(F32))


def _pick_rows_tile(n_prompt, n_sample, largest=512):
    for tr in (1024, 512, 256, 128, 64, 32, 16, 8):
        if tr <= largest and n_prompt % tr == 0 and n_sample % tr == 0:
            return tr
    raise ValueError("row counts must be multiples of 8")


def kernel(x_prompt, x_sample, cache_a_k, cache_a_v, cache_b_k, cache_b_v, norm_mix, norm_ffn, norm_final,
           w_a_qkv, b_a_qkv, a_sinks, w_a_o, w_b_qkv, b_lambda, b_subln, w_b_o,
           w_route_group, b_route_group, w_route_expert, b_route_expert, w_gate, w_up, w_down):
    batch, t, d = x_prompt.shape
    dec_batch, dec_t, _ = x_sample.shape
    n_p, n_s = batch * t, dec_batch * dec_t
    tr = _pick_rows_tile(n_p, n_s)
    tm = min(512, tr * 4)
    xp = x_prompt.reshape(n_p, d)
    xs = x_sample.reshape(n_s, d)
    nq_a = A_HEADS * A_HD
    nkv_a = A_KV_HEADS * A_HD

    a_scale = jnp.where(jnp.arange(nq_a + 2 * nkv_a) < nq_a, A_HD ** -0.5 * LOG2E, 1.0).astype(F32)
    q_a, kv_ap, kv_as = norm_proj(xp, xs, norm_mix[0], (w_a_qkv[0] * a_scale).astype(BF16), b_a_qkv[0] * a_scale,
                                  [(0, nq_a, "flat", BF16), (nq_a, nq_a + 2 * nkv_a, "split", F32)], tr, n_p)
    qb_a = min(256, t)
    o_ap = attn_a_prompt(q_a, kv_ap, a_sinks[0], batch, t, qb_a)
    kv_s = kv_as.reshape(dec_batch, dec_t, 2 * nkv_a)
    past_a = cache_a_k.shape[2]
    kband = jnp.concatenate([cache_a_k[0].reshape(dec_batch, past_a, nkv_a), kv_s[..., :nkv_a]], axis=1)
    vband = jnp.concatenate([cache_a_v[0].reshape(dec_batch, past_a, nkv_a), kv_s[..., nkv_a:]], axis=1)
    o_as = attn_a_sample(q_a, kband, vband, a_sinks[0], n_p, dec_batch, dec_t)
    h, xn, meta, route, counts = out_router(o_ap, o_as, xp, xs, w_a_o[0].astype(BF16), norm_ffn[0], w_route_group[0],
                                            b_route_group[0], w_route_expert[0], b_route_expert[0], False, tr)
    og = moe_experts(xn, route, counts, w_gate, w_up, w_down, 0, tm)

    keep = min(CHUNK * WIN_CHUNKS, t)
    kv_p = kv_ap.reshape(batch, t, 2 * nkv_a)[:, t - keep:]
    a_k_prompt = kv_p[..., :nkv_a].reshape(1, batch, keep, A_KV_HEADS, A_HD)
    a_v_prompt = kv_p[..., nkv_a:].reshape(1, batch, keep, A_KV_HEADS, A_HD)
    a_k_sample = kband[:, dec_t:].reshape(1, dec_batch, past_a, A_KV_HEADS, A_HD)
    a_v_sample = vband[:, dec_t:].reshape(1, dec_batch, past_a, A_KV_HEADS, A_HD)

    lam_init = 0.8 - 0.6 * float(np.exp(-0.3 * 1))
    lp = b_lambda[0].astype(F32)
    lam = jnp.exp(jnp.sum(lp[0] * lp[1])) - jnp.exp(jnp.sum(lp[2] * lp[3])) + lam_init
    wb = B_HEADS * 2 * B_HD
    col_scale = jnp.where(jnp.arange(3 * wb) < wb, B_Q_SCALE, 1.0).astype(F32)
    h, qkvh, k_bp, k_bs, v_bp, v_bs, vt = norm_proj(
        h, None, norm_mix[1], (w_b_qkv[0] * col_scale).astype(BF16), None,
        [(0, 3 * wb, "heads", BF16), (wb, 2 * wb, "split", F32), (2 * wb, 3 * wb, "split", F32),
         (2 * wb, 3 * wb, "heads_t", BF16)], tr, n_p, og=og, meta=meta)
    qb_b = min(256, t)
    o_bp = attn_b_prompt(qkvh, vt, lam, b_subln[0], lam_init, batch, t, qb_b)
    o_bs = attn_b_sample(qkvh, cache_b_k[0], cache_b_v[0], lam, b_subln[0], lam_init, n_p, dec_batch, dec_t)
    h, xn, meta, route, counts = out_router(o_bp, o_bs, h, None, w_b_o[0].astype(BF16), norm_ffn[1], w_route_group[1],
                                            b_route_group[1], w_route_expert[1], b_route_expert[1], True, tr)
    og = moe_experts(xn, route, counts, w_gate, w_up, w_down, 1, tm)

    y_p, y_s = final_norm(h, og, meta, norm_final, n_p, tr)
    hd2 = 2 * B_HD
    return (y_p.reshape(batch, t, d), y_s.reshape(dec_batch, dec_t, d),
            a_k_prompt, a_v_prompt, a_k_sample, a_v_sample,
            k_bp.reshape(1, batch, t, B_HEADS, hd2), v_bp.reshape(1, batch, t, B_HEADS, hd2),
            k_bs.reshape(1, dec_batch, dec_t, B_HEADS, hd2), v_bs.reshape(1, dec_batch, dec_t, B_HEADS, hd2))
```

```python
import functools

import jax
import jax.numpy as jnp
import numpy as np
from jax import lax
from jax.experimental import pallas as pl
from jax.experimental.pallas import tpu as pltpu
from jax.experimental.pallas import tpu_sc as plsc

F32 = jnp.float32
BF16 = jnp.bfloat16
NEG_INF = float("-inf")

CHUNK = 64
WIN_CHUNKS = 2
A_HEADS, A_KV_HEADS, A_GROUP, A_HD = 16, 4, 4, 64
B_HEADS, B_HD = 8, 64
N_GROUPS, EXPERTS_PER_GROUP = 4, 8
N_EXPERTS = N_GROUPS * EXPERTS_PER_GROUP
NORM_EPS = 1e-6
SUBLN_EPS = 1e-5

LANES = 128
COL_CHUNK = 512
PAIR_UNROLL = 6
HEAD_GROUP = 2
ONES_ROWS = 16
SC_WINDOW = 128
SC_ROW_WORDS = 256
VMEM_LIMIT = 48 * 1024 * 1024


def _cparams(sem, flags=None):
    return pltpu.CompilerParams(dimension_semantics=sem, vmem_limit_bytes=VMEM_LIMIT, flags=flags)


def _alibi(n):
    return 2.0 ** (-8.0 * np.arange(1, n + 1) / n)


def _store_planes(ref, x):
    w = ref.shape[2]
    for p in range(ref.shape[0]):
        ref[p] = x[:, p * w:(p + 1) * w]


def _join_planes(ref, first, count):
    return jnp.concatenate([ref[first + p] for p in range(count)], axis=1)


def _add_expert_outputs(x, og_ref, meta_ref):
    half = og_ref.shape[1] // 2
    meta = meta_ref[...]
    return (x + meta[:, 2:3] * _unpack_pairs(og_ref[:, :half])) + meta[:, 3:4] * _unpack_pairs(og_ref[:, half:])


def _norm_proj_kernel(n_prompt_blocks, two_src, has_bias, combine, outs, *refs):
    refs = list(refs)
    xa_ref = refs.pop(0)
    xb_ref = refs.pop(0) if two_src else None
    og_ref, meta_ref = (refs.pop(0), refs.pop(0)) if combine else (None, None)
    g_ref = refs.pop(0)
    w_ref = refs.pop(0)
    b_ref = refs.pop(0) if has_bias else None
    out_refs = refs
    x = xa_ref[...]
    if two_src:
        x = jnp.where(pl.program_id(0) < n_prompt_blocks, x, xb_ref[...])
    if combine:
        x = _add_expert_outputs(x, og_ref, meta_ref)
        out_refs.pop(0)[...] = x
    ms = jnp.mean(x * x, axis=-1, keepdims=True)
    xn = (x * lax.rsqrt(ms + NORM_EPS) * g_ref[...]).astype(BF16)
    is_prompt = pl.program_id(0) < n_prompt_blocks
    out_refs = list(out_refs)
    targets = []
    for c0, c1, layout in outs:
        targets.append((c0, c1, layout, out_refs.pop(0), out_refs.pop(0) if layout == "split" else None))
    for s in range(0, w_ref.shape[1], COL_CHUNK):
        e = s + COL_CHUNK
        r = jnp.dot(xn, w_ref[:, s:e], preferred_element_type=F32)
        if has_bias:
            r = r + b_ref[:, s:e]
        for c0, c1, layout, o_ref, o2_ref in targets:
            if not (c0 <= s and e <= c1):
                continue
            rr = r.astype(o_ref.dtype)
            if layout == "flat":
                o_ref[:, s - c0:e - c0] = rr
            elif layout == "split":

                @pl.when(is_prompt)
                def _(rr=rr, o_ref=o_ref, s=s, e=e, c0=c0):
                    o_ref[:, s - c0:e - c0] = rr

                @pl.when(jnp.logical_not(is_prompt))
                def _(rr=rr, o2_ref=o2_ref, s=s, e=e, c0=c0):
                    o2_ref[:, s - c0:e - c0] = rr
            elif layout == "heads":
                for hh in range((e - s) // LANES):
                    o_ref[(s - c0) // LANES + hh] = rr[:, hh * LANES:(hh + 1) * LANES]
            else:
                ones = jnp.ones((ONES_ROWS, r.shape[0]), o_ref.dtype)
                for hh in range((e - s) // LANES):
                    head = (s - c0) // LANES + hh
                    o_ref[head, :LANES, :] = r[:, hh * LANES:(hh + 1) * LANES].T.astype(o_ref.dtype)
                    o_ref[head, LANES:, :] = ones


def norm_proj(xa, xb, g, w, b, outs, tr, n_prompt, og=None, meta=None):
    na, d = xa.shape
    nb = 0 if xb is None else xb.shape[0]
    n = na + nb
    npb = n_prompt // tr
    two = xb is not None
    combine = og is not None
    in_specs = [pl.BlockSpec((tr, d), (lambda i: (jnp.minimum(i, npb - 1), 0)) if two else (lambda i: (i, 0)))]
    args = [xa]
    if two:
        in_specs.append(pl.BlockSpec((tr, d), lambda i: (jnp.maximum(i - npb, 0), 0)))
        args.append(xb)
    if combine:
        in_specs += [pl.BlockSpec((tr, og.shape[1]), lambda i: (i, 0)), pl.BlockSpec((tr, LANES), lambda i: (i, 0))]
        args += [og, meta]
    in_specs.append(pl.BlockSpec((1, d), lambda i: (0, 0)))
    args.append(g.reshape(1, d).astype(F32))
    in_specs.append(pl.BlockSpec(w.shape, lambda i: (0, 0)))
    args.append(w)
    if b is not None:
        in_specs.append(pl.BlockSpec((1, w.shape[1]), lambda i: (0, 0)))
        args.append(b.reshape(1, -1).astype(F32))
    out_shapes, out_specs = [], []
    if combine:
        out_shapes.append(jax.ShapeDtypeStruct((n, d), F32))
        out_specs.append(pl.BlockSpec((tr, d), lambda i: (i, 0)))
    for c0, c1, layout, dt in outs:
        assert c0 % COL_CHUNK == 0 and c1 % COL_CHUNK == 0
        if layout == "flat":
            out_shapes.append(jax.ShapeDtypeStruct((n, c1 - c0), dt))
            out_specs.append(pl.BlockSpec((tr, c1 - c0), lambda i: (i, 0)))
        elif layout == "split":
            out_shapes.append(jax.ShapeDtypeStruct((n_prompt, c1 - c0), dt))
            out_specs.append(pl.BlockSpec((tr, c1 - c0), lambda i: (jnp.minimum(i, npb - 1), 0)))
            out_shapes.append(jax.ShapeDtypeStruct((n - n_prompt, c1 - c0), dt))
            out_specs.append(pl.BlockSpec((tr, c1 - c0), lambda i: (jnp.maximum(i - npb, 0), 0)))
        elif layout == "heads":
            nh = (c1 - c0) // LANES
            out_shapes.append(jax.ShapeDtypeStruct((nh, n, LANES), dt))
            out_specs.append(pl.BlockSpec((nh, tr, LANES), lambda i: (0, i, 0)))
        else:
            assert layout == "heads_t"
            nh = (c1 - c0) // LANES
            out_shapes.append(jax.ShapeDtypeStruct((nh, LANES + ONES_ROWS, n), dt))
            out_specs.append(pl.BlockSpec((nh, LANES + ONES_ROWS, tr), lambda i: (0, 0, i)))
    kern = functools.partial(_norm_proj_kernel, npb, two, b is not None, combine,
                             [(c0, c1, lay) for c0, c1, lay, _ in outs])
    return pl.pallas_call(
        kern, grid=(n // tr,), in_specs=in_specs, out_specs=out_specs, out_shape=out_shapes,
        compiler_params=_cparams(("arbitrary",)), name="norm_proj")(*args)


def _attn_a_chunk(q, k, v, bias_ref, sink_ref, valid):
    nq = q.shape[0]
    outs = []
    for kh in range(A_KV_HEADS):
        qs = jnp.concatenate(
            [q[:, (kh * A_GROUP + g) * A_HD:(kh * A_GROUP + g + 1) * A_HD] for g in range(A_GROUP)], axis=0)
        kk = k[:, kh * A_HD:(kh + 1) * A_HD]
        s = lax.dot_general(qs, kk, (((1,), (1,)), ((), ())), preferred_element_type=F32) + bias_ref[kh]
        if valid is not None:
            s = jnp.where(valid, s, NEG_INF)
        sk = sink_ref[kh]
        m = jnp.maximum(jnp.max(s, axis=-1, keepdims=True), sk)
        e = jnp.exp2(s - m)
        den = jnp.sum(e, axis=-1, keepdims=True) + jnp.exp2(sk - m)
        p = (e * (1.0 / den)).astype(BF16)
        o = jnp.dot(p, v[:, kh * A_HD:(kh + 1) * A_HD], preferred_element_type=F32)
        for g in range(A_GROUP):
            outs.append(o[g * nq:(g + 1) * nq])
    return jnp.concatenate(outs, axis=1)


def _attn_a_prompt_kernel(qb, nsub, q_ref, kvc_ref, kvp_ref, bias_ref, sink_ref, o_ref, t_ref, p_ref):
    cb = pl.program_id(1)
    back = CHUNK * WIN_CHUNKS
    nk = back + qb
    nkv = A_KV_HEADS * A_HD
    kvfull = jnp.concatenate([kvp_ref[...], kvc_ref[...]], axis=0)
    k_all = kvfull[:, :nkv].astype(BF16)
    vt_all = kvfull[:, nkv:].T.astype(BF16)
    dn = (((1,), (1,)), ((), ()))
    units = [(s, kh) for s in range(nsub) for kh in range(A_KV_HEADS)]
    for s, kh in units:
        qs = jnp.concatenate([q_ref[s * qb:(s + 1) * qb, (kh * A_GROUP + g) * A_HD:(kh * A_GROUP + g + 1) * A_HD]
                              for g in range(A_GROUP)], axis=0)
        t_ref[s * A_KV_HEADS + kh] = lax.dot_general(k_all[s * qb:s * qb + nk, kh * A_HD:(kh + 1) * A_HD], qs, dn,
                                                     preferred_element_type=F32)
    for s in range(nsub):
        outs = []
        for kh in range(A_KV_HEADS):
            u = s * A_KV_HEADS + kh
            st = t_ref[u] + bias_ref[kh]
            if s == 0:
                key_pos = lax.broadcasted_iota(jnp.int32, (nk, 1), 0) + (cb * nsub * qb - back)
                st = st + jnp.where(key_pos >= 0, 0.0, NEG_INF)
            sk = sink_ref[kh]
            m = jnp.maximum(jnp.max(st, axis=0, keepdims=True), sk)
            e = jnp.exp2(st - m)
            inv_den = 1.0 / (jnp.sum(e, axis=0, keepdims=True) + jnp.exp2(sk - m))
            p_ref[u] = e.astype(BF16)
            ot = jnp.dot(vt_all[kh * A_HD:(kh + 1) * A_HD, s * qb:s * qb + nk], p_ref[u],
                         preferred_element_type=F32)
            o = (ot * inv_den).T
            for g in range(A_GROUP):
                outs.append(o[g * qb:(g + 1) * qb])
        o_ref[s * qb:(s + 1) * qb, :] = jnp.concatenate(outs, axis=1).astype(o_ref.dtype)


def _a_prompt_tables(qb, sinks):
    back = CHUNK * WIN_CHUNKS
    slopes = _alibi(A_HEADS).reshape(A_KV_HEADS, A_GROUP)
    qpos = np.arange(qb)
    kpos = np.arange(back + qb) - back
    dist = np.abs(kpos[:, None] - qpos[None, :]).astype(np.float64)
    band_lo = (qpos // CHUNK) * CHUNK - back
    in_band = (kpos[:, None] >= band_lo[None, :]) & (kpos[:, None] < band_lo[None, :] + back + CHUNK)
    bias = np.where(in_band[None, None], -slopes[:, :, None, None] * dist[None, None], -np.inf)
    bias = np.transpose(bias, (0, 2, 1, 3)).reshape(A_KV_HEADS, back + qb, A_GROUP * qb) * LOG2E
    sk = jnp.repeat(sinks.astype(F32).reshape(A_KV_HEADS, A_GROUP), qb, axis=1)[:, None, :] * LOG2E
    return jnp.asarray(bias, F32), sk


def _a_tables(qpos, kpos, sinks):
    slopes = _alibi(A_HEADS).reshape(A_KV_HEADS, A_GROUP)
    dist = np.abs(qpos[:, None] - kpos[None, :]).astype(np.float64)
    bias = -slopes[:, :, None, None] * dist
    nq = len(qpos)
    bias = jnp.asarray(bias.reshape(A_KV_HEADS, A_GROUP * nq, len(kpos)) * LOG2E, F32)
    sk = jnp.repeat(sinks.astype(F32).reshape(A_KV_HEADS, A_GROUP), nq, axis=1)[..., None] * LOG2E
    return bias, sk


def attn_a_prompt(q_all, kv_all, sinks, batch, t, qb):
    nq = A_HEADS * A_HD
    nkv2 = 2 * A_KV_HEADS * A_HD
    back = CHUNK * WIN_CHUNKS
    bias, sk = _a_prompt_tables(qb, sinks)
    nsub = 2 if t % (2 * qb) == 0 else 1
    rows = nsub * qb
    nblk = t // rows
    r = rows // back
    return pl.pallas_call(
        functools.partial(_attn_a_prompt_kernel, qb, nsub),
        grid=(batch, nblk),
        in_specs=[
            pl.BlockSpec((rows, nq), lambda b, i: (b * nblk + i, 0)),
            pl.BlockSpec((rows, nkv2), lambda b, i: (b * nblk + i, 0)),
            pl.BlockSpec((back, nkv2), lambda b, i: (jnp.maximum((b * nblk + i) * r - 1, 0), 0)),
            pl.BlockSpec(bias.shape, lambda b, i: (0, 0, 0)),
            pl.BlockSpec(sk.shape, lambda b, i: (0, 0, 0)),
        ],
        out_specs=pl.BlockSpec((rows, nq), lambda b, i: (b * nblk + i, 0)),
        out_shape=jax.ShapeDtypeStruct((batch * t, nq), BF16),
        scratch_shapes=[pltpu.VMEM((nsub * A_KV_HEADS, back + qb, A_GROUP * qb), F32),
                        pltpu.VMEM((nsub * A_KV_HEADS, back + qb, A_GROUP * qb), BF16)],
        compiler_params=_cparams(("arbitrary", "arbitrary")), name="attn_a_prompt",
    )(q_all, kv_all, kv_all, bias, sk)


def _attn_a_sample_kernel(q_ref, k_ref, v_ref, bias_ref, sink_ref, o_ref):
    o = _attn_a_chunk(q_ref[...], k_ref[...].astype(BF16), v_ref[...].astype(BF16), bias_ref, sink_ref, None)
    o_ref[...] = o.astype(o_ref.dtype)


def attn_a_sample(q_all, kband, vband, sinks, row0, dec_batch, dec_t):
    nq = A_HEADS * A_HD
    s = kband.shape[1]
    past = s - dec_t
    kpos = np.concatenate([np.arange(past) - past, np.arange(dec_t)])
    bias, sk = _a_tables(np.arange(dec_t), kpos, sinks)
    blk0 = row0 // dec_t
    return pl.pallas_call(
        _attn_a_sample_kernel,
        grid=(dec_batch,),
        in_specs=[
            pl.BlockSpec((dec_t, nq), lambda b: (blk0 + b, 0)),
            pl.BlockSpec((None, s, kband.shape[2]), lambda b: (b, 0, 0)),
            pl.BlockSpec((None, s, vband.shape[2]), lambda b: (b, 0, 0)),
            pl.BlockSpec(bias.shape, lambda b: (0, 0, 0)),
            pl.BlockSpec(sk.shape, lambda b: (0, 0, 0)),
        ],
        out_specs=pl.BlockSpec((dec_t, nq), lambda b: (b, 0)),
        out_shape=jax.ShapeDtypeStruct((dec_batch * dec_t, nq), BF16),
        compiler_params=_cparams(("arbitrary",)), name="attn_a_sample",
    )(q_all, kband, vband, bias, sk)


B_Q_SCALE = B_HD ** -0.5 * float(np.log2(np.e))
LOG2E = float(np.log2(np.e))


def _stack_maps(qh):
    lane = lax.broadcasted_iota(jnp.int32, qh.shape, 1)
    zero = jnp.zeros_like(qh)
    return jnp.concatenate([jnp.where(lane < B_HD, qh, zero), jnp.where(lane >= B_HD, qh, zero)], axis=0)


def _diff_finish(acc, l, nq, lam, lam_init, subln):
    inv = 1.0 / l
    o = acc[:nq] * inv[:nq] - lam * (acc[nq:] * inv[nq:])
    ms = jnp.mean(o * o, axis=-1, keepdims=True)
    return o * lax.rsqrt(ms + SUBLN_EPS) * subln * (1.0 - lam_init)


def _attn_b_prompt_kernel(qb, nb, lam_init, pair_ref, slope_ref, lam_ref, q_ref, k_ref, vt_ref, kbias_ref,
                          subln_ref, o_ref, acc_ref, m_ref, al_ref, qq_ref, t0_ref, t1_ref, p0_ref, p1_ref):
    g = pl.program_id(1)
    lam = lam_ref[0]
    dn = (((1,), (1,)), ((), ()))
    t_refs = (t0_ref, t1_ref)
    p_refs = (p0_ref, p1_ref)
    heads = range(HEAD_GROUP)
    n_pairs = nb * (nb + 1) // 2
    slopes = [slope_ref[g * HEAD_GROUP + u] for u in heads]

    def pair(n):
        return pair_ref[2 * n], pair_ref[2 * n + 1]

    def scores(n, slot):
        j, iq = pair(n)
        ks = pl.multiple_of(j * qb, qb)
        for u in heads:
            t_refs[slot][u] = lax.dot_general(k_ref[u, pl.ds(ks, qb), :], qq_ref[u * nb + iq], dn,
                                              preferred_element_type=F32)

    def softmax_step(n, slot):
        j, iq = pair(n)
        on_diag = (j == iq).astype(jnp.int32)
        for u in heads:
            st = u * nb + iq
            m = m_ref[st, 0:1, :]
            t = t_refs[slot][u] + kbias_ref[u, on_diag]
            off = slopes[u] * jnp.full((1, 2 * qb), (j - iq) * qb, jnp.int32).astype(F32)
            m_new = jnp.maximum(m, jnp.max(t, axis=0, keepdims=True) + off)
            p_refs[slot][u] = jnp.exp2(t + (off - m_new)).astype(BF16)
            al_ref[slot * HEAD_GROUP + u, 0:1, :] = jnp.exp2(m - m_new)
            m_ref[st, 0:1, :] = m_new

    def accumulate(n, slot):
        j, iq = pair(jnp.maximum(n, 0))
        ks = pl.multiple_of(j * qb, qb)
        for u in heads:
            st = u * nb + iq
            pv = jnp.dot(vt_ref[u, :, pl.ds(ks, qb)], p_refs[slot][u], preferred_element_type=F32)
            acc_ref[st] = al_ref[slot * HEAD_GROUP + u, 0:1, :] * acc_ref[st] + pv

    def step(n, slot):
        softmax_step(n, slot)
        accumulate(n - 1, 1 - slot)
        scores(jnp.minimum(n + 1, n_pairs - 1), 1 - slot)
        return 0

    acc_ref[...] = jnp.zeros_like(acc_ref)
    m_ref[...] = jnp.full(m_ref.shape, NEG_INF, F32)
    al_ref[...] = jnp.ones_like(al_ref)
    p1_ref[...] = jnp.zeros_like(p1_ref)
    for u in heads:
        for iq in range(nb):
            qq_ref[u * nb + iq] = _stack_maps(q_ref[u, iq * qb:(iq + 1) * qb, :])
    scores(0, 0)
    def steps(k, c):
        for r in range(PAIR_UNROLL):
            step(PAIR_UNROLL * k + r, r % 2)
        return c

    lax.fori_loop(0, n_pairs // PAIR_UNROLL, steps, 0)
    for n in range(n_pairs - n_pairs % PAIR_UNROLL, n_pairs):
        step(n, n % 2)
    accumulate(n_pairs - 1, (n_pairs - 1) % 2)
    for u in heads:
        for iq in range(nb):
            inv = 1.0 / acc_ref[u * nb + iq, LANES:LANES + 1, :]
            acc = acc_ref[u * nb + iq, :LANES, :]
            ot = acc[:, :qb] * inv[:, :qb] - lam * (acc[:, qb:] * inv[:, qb:])
            ms = jnp.mean(ot * ot, axis=0, keepdims=True)
            ot = ot * lax.rsqrt(ms + SUBLN_EPS) * subln_ref[...] * (1.0 - lam_init)
            o_ref[u, iq * qb:(iq + 1) * qb, :] = ot.T.astype(o_ref.dtype)


def attn_b_prompt(qkvh, vt, lam, subln, lam_init, batch, t, qb):
    nblk = t // qb
    qq = np.arange(qb)
    slope2 = (_alibi(B_HEADS) * LOG2E)[:, None, None]
    kcol = np.broadcast_to(qq[:, None], (qb, qb)).astype(np.float64)
    tri = -2.0 * np.maximum(qq[:, None] - qq[None, :], 0)
    msk = np.where((qq[:, None] // CHUNK) <= (qq[None, :] // CHUNK), 0.0, -np.inf)
    both = lambda a: np.concatenate([a, a], axis=-1)
    kbias = np.stack([both(slope2 * kcol[None]), both(slope2 * (kcol + tri)[None] + msk[None])], axis=1)
    kbias = jnp.asarray(kbias, F32)
    slopes = jnp.asarray(_alibi(B_HEADS) * LOG2E, F32)
    subln_t = jnp.broadcast_to(subln.astype(F32).reshape(LANES, 1), (LANES, qb))
    pairs = jnp.asarray([v for j in range(nblk) for i in range(j, nblk) for v in (j, i)], jnp.int32)
    hg = HEAD_GROUP
    n_groups = B_HEADS // hg
    grid_spec = pltpu.PrefetchScalarGridSpec(
        num_scalar_prefetch=3, grid=(batch, n_groups),
        in_specs=[
            pl.BlockSpec((hg, t, LANES), lambda b, g, *_: (g, b, 0)),
            pl.BlockSpec((hg, t, LANES), lambda b, g, *_: (n_groups + g, b, 0)),
            pl.BlockSpec((hg, LANES + ONES_ROWS, t), lambda b, g, *_: (g, 0, b)),
            pl.BlockSpec((hg,) + kbias.shape[1:], lambda b, g, *_: (g, 0, 0, 0)),
            pl.BlockSpec(subln_t.shape, lambda b, g, *_: (0, 0)),
        ],
        out_specs=pl.BlockSpec((hg, t, LANES), lambda b, g, *_: (g, b, 0)),
        scratch_shapes=[pltpu.VMEM((hg * nblk, LANES + ONES_ROWS, 2 * qb), F32),
                        pltpu.VMEM((hg * nblk, 8, 2 * qb), F32),
                        pltpu.VMEM((2 * hg, 8, 2 * qb), F32),
                        pltpu.VMEM((hg * nblk, 2 * qb, LANES), BF16),
                        pltpu.VMEM((hg, qb, 2 * qb), F32), pltpu.VMEM((hg, qb, 2 * qb), F32),
                        pltpu.VMEM((hg, qb, 2 * qb), BF16), pltpu.VMEM((hg, qb, 2 * qb), BF16)],
    )
    return pl.pallas_call(
        functools.partial(_attn_b_prompt_kernel, qb, nblk, lam_init), grid_spec=grid_spec,
        out_shape=jax.ShapeDtypeStruct((B_HEADS, batch * t, LANES), BF16),
        compiler_params=_cparams(("arbitrary", "arbitrary")), name="attn_b_prompt",
    )(pairs, slopes, lam.reshape(1).astype(F32), qkvh, qkvh, vt, kbias, subln_t)


def _attn_b_sample_kernel(dec_t, past, lam_init, slopes, lam_ref, q_ref, kn_ref, vn_ref, ck_ref, cv_ref,
                          tnew_ref, subln_ref, o_ref):
    lam = lam_ref[0]
    subln = subln_ref[...]
    colc = lax.broadcasted_iota(jnp.int32, (1, past), 1).astype(F32) - float(past)
    for h in range(B_HEADS):
        slope = float(slopes[h]) * LOG2E
        qq = _stack_maps(q_ref[h])
        kc = ck_ref[pl.ds(h, past, stride=B_HEADS), :].astype(BF16)
        vc = cv_ref[pl.ds(h, past, stride=B_HEADS), :].astype(BF16)
        dn = (((1,), (1,)), ((), ()))
        s_c = lax.dot_general(qq, kc, dn, preferred_element_type=F32) + slope * colc
        s_n = lax.dot_general(qq, kn_ref[h], dn, preferred_element_type=F32) + slope * tnew_ref[...]
        m = jnp.maximum(jnp.max(s_c, axis=-1, keepdims=True), jnp.max(s_n, axis=-1, keepdims=True))
        p_c = jnp.exp2(s_c - m)
        p_n = jnp.exp2(s_n - m)
        l = jnp.sum(p_c, axis=-1, keepdims=True) + jnp.sum(p_n, axis=-1, keepdims=True)
        acc = (jnp.dot(p_c.astype(BF16), vc, preferred_element_type=F32)
               + jnp.dot(p_n.astype(BF16), vn_ref[h], preferred_element_type=F32))
        o_ref[h] = _diff_finish(acc, l, dec_t, lam, lam_init, subln).astype(o_ref.dtype)


def attn_b_sample(qkvh, cache_k, cache_v, lam, subln, lam_init, row0, dec_batch, dec_t):
    past = cache_k.shape[1]
    qi = np.arange(dec_t)
    tnew = (qi[:, None] - np.abs(qi[:, None] - qi[None, :])).astype(np.float32)
    tnew = jnp.asarray(np.concatenate([tnew, tnew], axis=0))
    blk0 = row0 // dec_t
    rows = past * B_HEADS
    cache_k = cache_k.reshape(dec_batch, rows, LANES)
    cache_v = cache_v.reshape(dec_batch, rows, LANES)
    grid_spec = pltpu.PrefetchScalarGridSpec(
        num_scalar_prefetch=1, grid=(dec_batch,),
        in_specs=[
            pl.BlockSpec((B_HEADS, dec_t, LANES), lambda b, *_: (0, blk0 + b, 0)),
            pl.BlockSpec((B_HEADS, dec_t, LANES), lambda b, *_: (1, blk0 + b, 0)),
            pl.BlockSpec((B_HEADS, dec_t, LANES), lambda b, *_: (2, blk0 + b, 0)),
            pl.BlockSpec((None, rows, LANES), lambda b, *_: (b, 0, 0)),
            pl.BlockSpec((None, rows, LANES), lambda b, *_: (b, 0, 0)),
            pl.BlockSpec(tnew.shape, lambda b, *_: (0, 0)),
            pl.BlockSpec((1, LANES), lambda b, *_: (0, 0)),
        ],
        out_specs=pl.BlockSpec((B_HEADS, dec_t, LANES), lambda b, *_: (0, b, 0)),
    )
    return pl.pallas_call(
        functools.partial(_attn_b_sample_kernel, dec_t, past, lam_init, tuple(_alibi(B_HEADS))),
        grid_spec=grid_spec,
        out_shape=jax.ShapeDtypeStruct((B_HEADS, dec_batch * dec_t, LANES), BF16),
        compiler_params=_cparams(("arbitrary",)), name="attn_b_sample",
    )(lam.reshape(1).astype(F32), qkvh, qkvh, qkvh, cache_k, cache_v, tnew, subln.reshape(1, LANES).astype(F32))


ROUTE_ROWS = 40
ROUTE_FIELDS = 8


def _route_t(lt):
    row = lax.broadcasted_iota(jnp.int32, lt.shape, 0).astype(F32)
    big = float(LANES)
    lg = jnp.where(row < N_GROUPS, lt, NEG_INF)
    gmax = jnp.max(lg, axis=0, keepdims=True)
    g_idx = jnp.min(jnp.where(lg == gmax, row, big), axis=0, keepdims=True)
    g_prob = 1.0 / jnp.sum(jnp.exp(lg - gmax), axis=0, keepdims=True)
    lo = N_GROUPS + EXPERTS_PER_GROUP * g_idx
    le = jnp.where((row >= lo) & (row < lo + EXPERTS_PER_GROUP), lt, NEG_INF)
    m1 = jnp.max(le, axis=0, keepdims=True)
    i1 = jnp.min(jnp.where(le == m1, row, big), axis=0, keepdims=True)
    le2 = jnp.where(row == i1, NEG_INF, le)
    m2 = jnp.max(le2, axis=0, keepdims=True)
    i2 = jnp.min(jnp.where(le2 == m2, row, big), axis=0, keepdims=True)
    t = jnp.exp(m2 - m1)
    ga = 1.0 / (1.0 + t)
    gb = t * ga
    return row, i1, i2, g_prob * ga, g_prob * gb


def _unpack_pairs(u):
    lo = pltpu.bitcast(u << 16, F32)
    hi = pltpu.bitcast(u & jnp.uint32(0xFFFF0000), F32)
    return jnp.concatenate([lo, hi], axis=1)


def _pack_pairs(x):
    bits = pltpu.bitcast(x.astype(BF16).astype(F32), jnp.uint32)
    w = x.shape[1] // 2
    return (bits[:, :w] >> 16) | bits[:, w:]


def _out_router_kernel(n_prompt_blocks, heads_in, two_o, two_res, *refs):
    refs = list(refs)
    oa_ref = refs.pop(0)
    ob_ref = refs.pop(0) if two_o else None
    ra_ref = refs.pop(0)
    rb_ref = refs.pop(0) if two_res else None
    (wo_ref, g_ref, wr_ref, wr2_ref, br_ref, utri_ref, h_ref, xn_ref, meta_ref, route_ref, cnt_ref,
     carry_ref) = refs
    is_prompt = pl.program_id(0) < n_prompt_blocks

    def load_o(ref):
        if heads_in:
            return jnp.concatenate([ref[hh] for hh in range(ref.shape[0])], axis=1)
        return ref[...]

    o = load_o(oa_ref)
    if two_o:
        o = jnp.where(is_prompt, o, load_o(ob_ref))
    res = ra_ref[...]
    if two_res:
        res = jnp.where(is_prompt, res, rb_ref[...])
    h = res + jnp.dot(o, wo_ref[...], preferred_element_type=F32)
    h_ref[...] = h
    ms = jnp.mean(h * h, axis=-1, keepdims=True)
    xn = h * lax.rsqrt(ms + NORM_EPS) * g_ref[...]
    x_hi = xn.astype(BF16)
    _store_planes(xn_ref, _pack_pairs(xn))
    x_lo = (xn - x_hi.astype(F32)).astype(BF16)
    r = jnp.dot(x_hi, wr_ref[...], preferred_element_type=F32)
    logits = (r[:, :LANES] + r[:, LANES:]) + jnp.dot(x_lo, wr2_ref[...], preferred_element_type=F32) + br_ref[...]
    row, i1, i2, g1, g2 = _route_t(logits.T[:ROUTE_ROWS])

    @pl.when(pl.program_id(0) == 0)
    def _():
        carry_ref[...] = jnp.zeros_like(carry_ref)

    sel1 = row == i1
    sel2 = row == i2
    onehot = jnp.where(sel1 | sel2, 1.0, 0.0)
    before = jnp.dot(onehot.astype(BF16), utri_ref[...], preferred_element_type=F32) + carry_ref[:, 0:1]
    rank1 = jnp.sum(jnp.where(sel1, before, 0.0), axis=0, keepdims=True)
    rank2 = jnp.sum(jnp.where(sel2, before, 0.0), axis=0, keepdims=True)
    carry_ref[...] = carry_ref[...] + jnp.sum(onehot, axis=1, keepdims=True)
    cnt_ref[...] = carry_ref[...]
    fields = [i1 - N_GROUPS, i2 - N_GROUPS, g1, g2, rank1, rank2]
    fields += [jnp.zeros_like(g1)] * (ROUTE_FIELDS - len(fields))
    route = jnp.concatenate(fields, axis=0)
    route_ref[...] = route
    meta_ref[...] = jnp.concatenate([route, jnp.zeros((LANES - ROUTE_FIELDS, route.shape[1]), F32)], axis=0).T


def out_router(oa, ob, ra, rb, wo, g, w_rg, b_rg, w_re, b_re, heads_in, tr):
    d = ra.shape[1]
    if heads_in:
        na = oa.shape[1]
        nb = 0 if ob is None else ob.shape[1]
    else:
        na = oa.shape[0]
        nb = 0 if ob is None else ob.shape[0]
    n = na + nb
    npb = na // tr
    two_o = ob is not None
    two_res = rb is not None
    first = lambda i: (jnp.minimum(i, npb - 1), 0)
    second = lambda i: (jnp.maximum(i - npb, 0), 0)
    plain = lambda i: (i, 0)
    in_specs, args = [], []

    def add_o(x, imap):
        if heads_in:
            in_specs.append(pl.BlockSpec((x.shape[0], tr, LANES), lambda i: (0, imap(i)[0], 0)))
        else:
            in_specs.append(pl.BlockSpec((tr, x.shape[1]), imap))
        args.append(x)

    add_o(oa, first if two_o else plain)
    if two_o:
        add_o(ob, second)
    in_specs.append(pl.BlockSpec((tr, d), first if two_res else plain))
    args.append(ra)
    if two_res:
        in_specs.append(pl.BlockSpec((tr, d), second))
        args.append(rb)
    wr = jnp.zeros((d, LANES), F32).at[:, :N_GROUPS].set(w_rg.astype(F32))
    wr = wr.at[:, N_GROUPS:N_GROUPS + N_EXPERTS].set(w_re.astype(F32))
    wr_hi = wr.astype(BF16)
    wr_lo = (wr - wr_hi.astype(F32)).astype(BF16)
    br = jnp.zeros((1, LANES), F32).at[0, :N_GROUPS].set(b_rg.astype(F32))
    br = br.at[0, N_GROUPS:N_GROUPS + N_EXPERTS].set(b_re.astype(F32))
    utri = jnp.asarray(np.triu(np.ones((tr, tr), np.float32), 1), BF16)
    consts = [wo, g.reshape(1, d).astype(F32), jnp.concatenate([wr_hi, wr_lo], axis=1), wr_hi, br, utri]
    for cst in consts:
        in_specs.append(pl.BlockSpec(cst.shape, lambda i: (0, 0)))
        args.append(cst)
    parts = d // 2 // SC_ROW_WORDS
    out_shape = [jax.ShapeDtypeStruct((n, d), F32), jax.ShapeDtypeStruct((parts, n, SC_ROW_WORDS), jnp.uint32),
                 jax.ShapeDtypeStruct((n, LANES), F32), jax.ShapeDtypeStruct((ROUTE_FIELDS, n), F32),
                 jax.ShapeDtypeStruct((ROUTE_ROWS, LANES), F32)]
    out_specs = [pl.BlockSpec((tr, d), plain), pl.BlockSpec((parts, tr, SC_ROW_WORDS), lambda i: (0, i, 0)),
                 pl.BlockSpec((tr, LANES), plain), pl.BlockSpec((ROUTE_FIELDS, tr), lambda i: (0, i)),
                 pl.BlockSpec((ROUTE_ROWS, LANES), lambda i: (0, 0))]
    return pl.pallas_call(
        functools.partial(_out_router_kernel, npb, heads_in, two_o, two_res),
        grid=(n // tr,), in_specs=in_specs, out_specs=out_specs, out_shape=out_shape,
        scratch_shapes=[pltpu.VMEM((ROUTE_ROWS, LANES), F32)],
        compiler_params=_cparams(("arbitrary",)), name="out_router")(*args)


def _expert_kernel(tm, tps, te_ref, nt_ref, x_ref, *refs):
    w_refs, o_ref = refs[:3 * tps], refs[3 * tps]
    first = pl.program_id(0) * tps
    n_valid = jnp.clip(nt_ref[0] - first, 0, tps)

    def tile(k):
        wg_ref, wu_ref, wd_ref = w_refs[3 * k:3 * k + 3]
        rows = slice(k * tm, (k + 1) * tm)
        x = _unpack_pairs(jnp.concatenate([x_ref[q, rows, :] for q in range(x_ref.shape[0])], axis=1)).astype(BF16)
        a = jnp.dot(x, wg_ref[...].astype(BF16), preferred_element_type=F32)
        u = jnp.dot(x, wu_ref[...].astype(BF16), preferred_element_type=F32)
        hh = (a * (1.0 / (1.0 + jnp.exp(-a))) * u).astype(BF16)
        packed = _pack_pairs(jnp.dot(hh, wd_ref[...].astype(BF16), preferred_element_type=F32))
        for q in range(o_ref.shape[0]):
            o_ref[q, rows, :] = packed[:, q * o_ref.shape[2]:(q + 1) * o_ref.shape[2]]

    for nv in range(tps + 1):

        @pl.when(n_valid == nv)
        def _(nv=nv):
            for k in range(nv):
                tile(k)
            if nv < tps:
                o_ref[:, nv * tm:, :] = jnp.zeros((o_ref.shape[0], (tps - nv) * tm, o_ref.shape[2]), o_ref.dtype)


def expert_mlp(xs, tile_expert, n_tiles, w_gate, w_up, w_down, layer, tm):
    parts, p, w = xs.shape
    d, f = w_gate.shape[-2:]
    epg = w_gate.shape[2]
    tps = 2 if (p // tm) % 2 == 0 else 1
    w_specs, w_args = [], []
    for k in range(tps):
        wmap = lambda t, te, nt, k=k: (layer, te[tps * t + k] // epg, te[tps * t + k] % epg, 0, 0)
        w_specs += [pl.BlockSpec((None, None, None, d, f), wmap), pl.BlockSpec((None, None, None, d, f), wmap),
                    pl.BlockSpec((None, None, None, f, d), wmap)]
        w_args += [w_gate, w_up, w_down]
    grid_spec = pltpu.PrefetchScalarGridSpec(
        num_scalar_prefetch=2, grid=(p // (tps * tm),),
        in_specs=[pl.BlockSpec((parts, tps * tm, w), lambda t, te, nt: (0, t, 0))] + w_specs,
        out_specs=pl.BlockSpec((parts, tps * tm, w), lambda t, te, nt: (0, t, 0)),
    )
    return pl.pallas_call(
        functools.partial(_expert_kernel, tm, tps), grid_spec=grid_spec,
        out_shape=jax.ShapeDtypeStruct((parts, p, w), jnp.uint32),
        compiler_params=_cparams(("arbitrary",)), name="expert_mlp")(tile_expert, n_tiles, xs, *w_args)


def sc_scatter_rows(x2, idx, n_out):
    mesh = plsc.VectorSubcoreMesh(core_axis_name="c", subcore_axis_name="s")
    n_src_win = x2.shape[0] // SC_WINDOW

    @functools.partial(pl.kernel, out_type=jax.ShapeDtypeStruct((n_out, SC_ROW_WORDS), x2.dtype), mesh=mesh)
    def scatter(x_hbm, i_hbm, o_hbm):
        def body(x_vmem, i_vmem):
            pltpu.sync_copy(x_vmem, o_hbm.at[i_vmem.at[0]])

        pltpu.emit_pipeline(
            body, grid=(n_out // SC_WINDOW,),
            in_specs=[pl.BlockSpec((SC_WINDOW, SC_ROW_WORDS), lambda i: (lax.rem(i, n_src_win), 0)),
                      pl.BlockSpec((1, SC_WINDOW), lambda i: (0, i))],
            out_specs=[], core_axis_name=("c", "s"), dimension_semantics=(pltpu.PARALLEL,),
        )(x_hbm, i_hbm)

    return scatter(x2, idx)


def sc_gather_rows(x2, idx):
    mesh = plsc.VectorSubcoreMesh(core_axis_name="c", subcore_axis_name="s")
    n_col, n = idx.shape

    @functools.partial(pl.kernel, out_type=jax.ShapeDtypeStruct((n, n_col * SC_ROW_WORDS), x2.dtype), mesh=mesh)
    def gather(x_hbm, i_hbm, o_hbm):
        def body(i_vmem, o_vmem):
            pltpu.sync_copy(x_hbm.at[i_vmem.at[0]], o_vmem)

        pltpu.emit_pipeline(
            body, grid=(n // SC_WINDOW, n_col),
            in_specs=[pl.BlockSpec((1, SC_WINDOW), lambda i, c: (c, i))],
            out_specs=[pl.BlockSpec((SC_WINDOW, SC_ROW_WORDS), lambda i, c: (i, c))],
            core_axis_name=("c", "s"), dimension_semantics=(pltpu.PARALLEL, pltpu.ARBITRARY),
        )(i_hbm, o_hbm)

    return gather(x2, idx)


def moe_dispatch(xn_packed, route, counts, tm):
    parts, n, w = xn_packed.shape
    e = route[0:2].astype(jnp.int32)
    rank = route[4:6].astype(jnp.int32)
    counts = counts[N_GROUPS:N_GROUPS + N_EXPERTS, 0].astype(jnp.int32)
    padded = ((counts + tm - 1) // tm) * tm
    pend = jnp.cumsum(padded)
    pstart = pend - padded
    ids = jnp.arange(N_EXPERTS, dtype=jnp.int32)
    pos = jnp.sum(jnp.where(e[..., None] == ids, pstart, 0), axis=-1) + rank
    p = 2 * n + N_EXPERTS * tm
    n_pad = p - 2 * n
    gap_len = jnp.concatenate([padded - counts, (p - pend[-1]).reshape(1)])
    gap_first = jnp.concatenate([pstart + counts, pend[-1:]])
    gap_end = jnp.cumsum(gap_len)
    k = jnp.arange(n_pad, dtype=jnp.int32)
    gap = jnp.sum((k[:, None] >= gap_end[None, :]).astype(jnp.int32), axis=1)
    shift = gap_first - (gap_end - gap_len)
    pad_pos = k + jnp.sum(jnp.where(gap[:, None] == jnp.arange(N_EXPERTS + 1), shift, 0), axis=-1)
    planes = lambda rows: jnp.concatenate([rows + q * p for q in range(parts)])
    scat_idx = jnp.concatenate([planes(pos[0]), planes(pos[1]), planes(pad_pos)]).reshape(1, parts * p)
    gath_idx = jnp.stack([pos[k] + q * p for k in range(2) for q in range(parts)])
    xs = sc_scatter_rows(xn_packed.reshape(parts * n, w), scat_idx, parts * p).reshape(parts, p, w)
    tile_start = jnp.arange(p // tm, dtype=jnp.int32) * tm
    tile_expert = jnp.sum((tile_start[:, None] >= pend[None, :]).astype(jnp.int32), axis=1)
    tile_expert = jnp.minimum(tile_expert, N_EXPERTS - 1)
    n_tiles = (pend[-1] // tm).astype(jnp.int32).reshape(1)
    return xs, tile_expert, n_tiles, gath_idx


def moe_experts(xn_packed, route, counts, w_gate, w_up, w_down, layer, tm):
    parts, n, w = xn_packed.shape
    xs, tile_expert, n_tiles, gath_idx = moe_dispatch(xn_packed, route, counts, tm)
    o = expert_mlp(xs, tile_expert, n_tiles, w_gate, w_up, w_down, layer, tm)
    return sc_gather_rows(o.reshape(parts * o.shape[1], w), gath_idx)


def _final_norm_kernel(n_prompt_blocks, h_ref, o0_ref, o1_ref, meta_ref, g_ref, yp_ref, ys_ref):
    meta = meta_ref[...]
    x = (h_ref[...] + meta[:, 2:3] * _unpack_pairs(o0_ref[...])) + meta[:, 3:4] * _unpack_pairs(o1_ref[...])
    ms = jnp.mean(x * x, axis=-1, keepdims=True)
    y = x * lax.rsqrt(ms + NORM_EPS) * g_ref[...]
    i = pl.program_id(0)

    @pl.when(i < n_prompt_blocks)
    def _():
        yp_ref[...] = y

    @pl.when(i >= n_prompt_blocks)
    def _():
        ys_ref[...] = y


def final_norm(h, og, meta, g, n_prompt, tr):
    n, d = h.shape
    npb = n_prompt // tr
    return pl.pallas_call(
        functools.partial(_final_norm_kernel, npb),
        grid=(n // tr,),
        in_specs=[pl.BlockSpec((tr, d), lambda i: (i, 0)),
                  pl.BlockSpec((tr, og.shape[1] // 2), lambda i: (i, 0)),
                  pl.BlockSpec((tr, og.shape[1] // 2), lambda i: (i, 1)),
                  pl.BlockSpec((tr, LANES), lambda i: (i, 0)), pl.BlockSpec((1, d), lambda i: (0, 0))],
        out_specs=[pl.BlockSpec((tr, d), lambda i: (jnp.minimum(i, npb - 1), 0)),
                   pl.BlockSpec((tr, d), lambda i: (jnp.maximum(i - npb, 0), 0))],
        out_shape=[jax.ShapeDtypeStruct((n_prompt, d), F32), jax.ShapeDtypeStruct((n - n_prompt, d), F32)],
        compiler_params=_cparams(("arbitrary",)), name="final_norm")(h, og, og, meta, g.reshape(1, d).astype(F32))


def _pick_rows_tile(n_prompt, n_sample, largest=512):
    for tr in (1024, 512, 256, 128, 64, 32, 16, 8):
        if tr <= largest and n_prompt % tr == 0 and n_sample % tr == 0:
            return tr
    raise ValueError("row counts must be multiples of 8")


def kernel(x_prompt, x_sample, cache_a_k, cache_a_v, cache_b_k, cache_b_v, norm_mix, norm_ffn, norm_final,
           w_a_qkv, b_a_qkv, a_sinks, w_a_o, w_b_qkv, b_lambda, b_subln, w_b_o,
           w_route_group, b_route_group, w_route_expert, b_route_expert, w_gate, w_up, w_down):
    batch, t, d = x_prompt.shape
    dec_batch, dec_t, _ = x_sample.shape
    n_p, n_s = batch * t, dec_batch * dec_t
    tr = _pick_rows_tile(n_p, n_s)
    tm = min(512, tr * 4)
    xp = x_prompt.reshape(n_p, d)
    xs = x_sample.reshape(n_s, d)
    nq_a = A_HEADS * A_HD
    nkv_a = A_KV_HEADS * A_HD

    a_scale = jnp.where(jnp.arange(nq_a + 2 * nkv_a) < nq_a, A_HD ** -0.5 * LOG2E, 1.0).astype(F32)
    q_a, kv_ap, kv_as = norm_proj(xp, xs, norm_mix[0], (w_a_qkv[0] * a_scale).astype(BF16), b_a_qkv[0] * a_scale,
                                  [(0, nq_a, "flat", BF16), (nq_a, nq_a + 2 * nkv_a, "split", F32)], tr, n_p)
    qb_a = min(256, t)
    o_ap = attn_a_prompt(q_a, kv_ap, a_sinks[0], batch, t, qb_a)
    kv_s = kv_as.reshape(dec_batch, dec_t, 2 * nkv_a)
    past_a = cache_a_k.shape[2]
    kband = jnp.concatenate([cache_a_k[0].reshape(dec_batch, past_a, nkv_a), kv_s[..., :nkv_a]], axis=1)
    vband = jnp.concatenate([cache_a_v[0].reshape(dec_batch, past_a, nkv_a), kv_s[..., nkv_a:]], axis=1)
    o_as = attn_a_sample(q_a, kband, vband, a_sinks[0], n_p, dec_batch, dec_t)
    h, xn, meta, route, counts = out_router(o_ap, o_as, xp, xs, w_a_o[0].astype(BF16), norm_ffn[0], w_route_group[0],
                                            b_route_group[0], w_route_expert[0], b_route_expert[0], False, tr)
    og = moe_experts(xn, route, counts, w_gate, w_up, w_down, 0, tm)

    keep = min(CHUNK * WIN_CHUNKS, t)
    kv_p = kv_ap.reshape(batch, t, 2 * nkv_a)[:, t - keep:]
    a_k_prompt = kv_p[..., :nkv_a].reshape(1, batch, keep, A_KV_HEADS, A_HD)
    a_v_prompt = kv_p[..., nkv_a:].reshape(1, batch, keep, A_KV_HEADS, A_HD)
    a_k_sample = kband[:, dec_t:].reshape(1, dec_batch, past_a, A_KV_HEADS, A_HD)
    a_v_sample = vband[:, dec_t:].reshape(1, dec_batch, past_a, A_KV_HEADS, A_HD)

    lam_init = 0.8 - 0.6 * float(np.exp(-0.3 * 1))
    lp = b_lambda[0].astype(F32)
    lam = jnp.exp(jnp.sum(lp[0] * lp[1])) - jnp.exp(jnp.sum(lp[2] * lp[3])) + lam_init
    wb = B_HEADS * 2 * B_HD
    col_scale = jnp.where(jnp.arange(3 * wb) < wb, B_Q_SCALE, 1.0).astype(F32)
    h, qkvh, k_bp, k_bs, v_bp, v_bs, vt = norm_proj(
        h, None, norm_mix[1], (w_b_qkv[0] * col_scale).astype(BF16), None,
        [(0, 3 * wb, "heads", BF16), (wb, 2 * wb, "split", F32), (2 * wb, 3 * wb, "split", F32),
         (2 * wb, 3 * wb, "heads_t", BF16)], tr, n_p, og=og, meta=meta)
    qb_b = min(256, t)
    o_bp = attn_b_prompt(qkvh, vt, lam, b_subln[0], lam_init, batch, t, qb_b)
    o_bs = attn_b_sample(qkvh, cache_b_k[0], cache_b_v[0], lam, b_subln[0], lam_init, n_p, dec_batch, dec_t)
    h, xn, meta, route, counts = out_router(o_bp, o_bs, h, None, w_b_o[0].astype(BF16), norm_ffn[1], w_route_group[1],
                                            b_route_group[1], w_route_expert[1], b_route_expert[1], True, tr)
    og = moe_experts(xn, route, counts, w_gate, w_up, w_down, 1, tm)

    y_p, y_s = final_norm(h, og, meta, norm_final, n_p, tr)
    hd2 = 2 * B_HD
    return (y_p.reshape(batch, t, d), y_s.reshape(dec_batch, dec_t, d),
            a_k_prompt, a_v_prompt, a_k_sample, a_v_sample,
            k_bp.reshape(1, batch, t, B_HEADS, hd2), v_bp.reshape(1, batch, t, B_HEADS, hd2),
            k_bs.reshape(1, dec_batch, dec_t, B_HEADS, hd2), v_bs.reshape(1, dec_batch, dec_t, B_HEADS, hd2))
```

```python
import functools

import jax
import jax.numpy as jnp
import numpy as np
from jax import lax
from jax.experimental import pallas as pl
from jax.experimental.pallas import tpu as pltpu
from jax.experimental.pallas import tpu_sc as plsc

F32 = jnp.float32
BF16 = jnp.bfloat16
NEG_INF = float("-inf")

CHUNK = 64
WIN_CHUNKS = 2
A_HEADS, A_KV_HEADS, A_GROUP, A_HD = 16, 4, 4, 64
B_HEADS, B_HD = 8, 64
N_GROUPS, EXPERTS_PER_GROUP = 4, 8
N_EXPERTS = N_GROUPS * EXPERTS_PER_GROUP
NORM_EPS = 1e-6
SUBLN_EPS = 1e-5

LANES = 128
SUBLANES = 8
ROW_TILE = 512
EXPERT_TILE = 512
ATTN_BLOCK = 256
COL_CHUNK = 512
PAIR_UNROLL = 12
HEAD_GROUP = 2
ONES_ROWS = 16
SC_WINDOW = 128
SC_ROW_WORDS = 256
VMEM_LIMIT = 48 * 1024 * 1024


def _cparams(sem, flags=None):
    return pltpu.CompilerParams(dimension_semantics=sem, vmem_limit_bytes=VMEM_LIMIT, flags=flags)


def _alibi(n):
    return 2.0 ** (-8.0 * np.arange(1, n + 1) / n)


def _store_planes(ref, x):
    w = ref.shape[2]
    for p in range(ref.shape[0]):
        ref[p] = x[:, p * w:(p + 1) * w]


def _add_expert_outputs(x, og_ref, meta_ref):
    half = og_ref.shape[1] // 2
    meta = meta_ref[...]
    return (x + meta[:, 2:3] * _unpack_pairs(og_ref[:, :half])) + meta[:, 3:4] * _unpack_pairs(og_ref[:, half:])


def _norm_proj_kernel(n_prompt_blocks, two_src, has_bias, combine, outs, *refs):
    refs = list(refs)
    xa_ref = refs.pop(0)
    xb_ref = refs.pop(0) if two_src else None
    og_ref, meta_ref = (refs.pop(0), refs.pop(0)) if combine else (None, None)
    g_ref = refs.pop(0)
    w_ref = refs.pop(0)
    b_ref = refs.pop(0) if has_bias else None
    out_refs = refs
    x = xa_ref[...]
    if two_src:
        x = jnp.where(pl.program_id(0) < n_prompt_blocks, x, xb_ref[...])
    if combine:
        x = _add_expert_outputs(x, og_ref, meta_ref)
        out_refs.pop(0)[...] = x
    ms = jnp.mean(x * x, axis=-1, keepdims=True)
    xn = (x * lax.rsqrt(ms + NORM_EPS) * g_ref[...]).astype(BF16)
    is_prompt = pl.program_id(0) < n_prompt_blocks
    out_refs = list(out_refs)
    targets = []
    for c0, c1, layout in outs:
        targets.append((c0, c1, layout, out_refs.pop(0), out_refs.pop(0) if layout == "split" else None))
    for s in range(0, w_ref.shape[1], COL_CHUNK):
        e = s + COL_CHUNK
        r = jnp.dot(xn, w_ref[:, s:e], preferred_element_type=F32)
        if has_bias:
            r = r + b_ref[:, s:e]
        for c0, c1, layout, o_ref, o2_ref in targets:
            if not (c0 <= s and e <= c1):
                continue
            rr = r.astype(o_ref.dtype)
            if layout == "flat":
                o_ref[:, s - c0:e - c0] = rr
            elif layout == "split":

                @pl.when(is_prompt)
                def _(rr=rr, o_ref=o_ref, s=s, e=e, c0=c0):
                    o_ref[:, s - c0:e - c0] = rr

                @pl.when(jnp.logical_not(is_prompt))
                def _(rr=rr, o2_ref=o2_ref, s=s, e=e, c0=c0):
                    o2_ref[:, s - c0:e - c0] = rr
            elif layout == "heads":
                for hh in range((e - s) // LANES):
                    o_ref[(s - c0) // LANES + hh] = rr[:, hh * LANES:(hh + 1) * LANES]
            else:
                ones = jnp.ones((ONES_ROWS, r.shape[0]), o_ref.dtype)
                for hh in range((e - s) // LANES):
                    head = (s - c0) // LANES + hh
                    o_ref[head, :LANES, :] = r[:, hh * LANES:(hh + 1) * LANES].T.astype(o_ref.dtype)
                    o_ref[head, LANES:, :] = ones


def norm_proj(xa, xb, g, w, b, outs, tr, n_prompt, og=None, meta=None):
    na, d = xa.shape
    nb = 0 if xb is None else xb.shape[0]
    n = na + nb
    npb = n_prompt // tr
    two = xb is not None
    combine = og is not None
    in_specs = [pl.BlockSpec((tr, d), (lambda i: (jnp.minimum(i, npb - 1), 0)) if two else (lambda i: (i, 0)))]
    args = [xa]
    if two:
        in_specs.append(pl.BlockSpec((tr, d), lambda i: (jnp.maximum(i - npb, 0), 0)))
        args.append(xb)
    if combine:
        in_specs += [pl.BlockSpec((tr, og.shape[1]), lambda i: (i, 0)), pl.BlockSpec((tr, LANES), lambda i: (i, 0))]
        args += [og, meta]
    in_specs.append(pl.BlockSpec((1, d), lambda i: (0, 0)))
    args.append(g.reshape(1, d).astype(F32))
    in_specs.append(pl.BlockSpec(w.shape, lambda i: (0, 0)))
    args.append(w)
    if b is not None:
        in_specs.append(pl.BlockSpec((1, w.shape[1]), lambda i: (0, 0)))
        args.append(b.reshape(1, -1).astype(F32))
    out_shapes, out_specs = [], []
    if combine:
        out_shapes.append(jax.ShapeDtypeStruct((n, d), F32))
        out_specs.append(pl.BlockSpec((tr, d), lambda i: (i, 0)))
    for c0, c1, layout, dt in outs:
        assert c0 % COL_CHUNK == 0 and c1 % COL_CHUNK == 0
        if layout == "flat":
            out_shapes.append(jax.ShapeDtypeStruct((n, c1 - c0), dt))
            out_specs.append(pl.BlockSpec((tr, c1 - c0), lambda i: (i, 0)))
        elif layout == "split":
            out_shapes.append(jax.ShapeDtypeStruct((n_prompt, c1 - c0), dt))
            out_specs.append(pl.BlockSpec((tr, c1 - c0), lambda i: (jnp.minimum(i, npb - 1), 0)))
            out_shapes.append(jax.ShapeDtypeStruct((n - n_prompt, c1 - c0), dt))
            out_specs.append(pl.BlockSpec((tr, c1 - c0), lambda i: (jnp.maximum(i - npb, 0), 0)))
        elif layout == "heads":
            nh = (c1 - c0) // LANES
            out_shapes.append(jax.ShapeDtypeStruct((nh, n, LANES), dt))
            out_specs.append(pl.BlockSpec((nh, tr, LANES), lambda i: (0, i, 0)))
        else:
            assert layout == "heads_t"
            nh = (c1 - c0) // LANES
            out_shapes.append(jax.ShapeDtypeStruct((nh, LANES + ONES_ROWS, n), dt))
            out_specs.append(pl.BlockSpec((nh, LANES + ONES_ROWS, tr), lambda i: (0, 0, i)))
    kern = functools.partial(_norm_proj_kernel, npb, two, b is not None, combine,
                             [(c0, c1, lay) for c0, c1, lay, _ in outs])
    return pl.pallas_call(
        kern, grid=(n // tr,), in_specs=in_specs, out_specs=out_specs, out_shape=out_shapes,
        compiler_params=_cparams(("arbitrary",)), name="norm_proj")(*args)


def _attn_a_chunk(q, k, v, bias_ref, sink_ref, valid):
    nq = q.shape[0]
    outs = []
    for kh in range(A_KV_HEADS):
        qs = jnp.concatenate(
            [q[:, (kh * A_GROUP + g) * A_HD:(kh * A_GROUP + g + 1) * A_HD] for g in range(A_GROUP)], axis=0)
        kk = k[:, kh * A_HD:(kh + 1) * A_HD]
        s = lax.dot_general(qs, kk, (((1,), (1,)), ((), ())), preferred_element_type=F32) + bias_ref[kh]
        if valid is not None:
            s = jnp.where(valid, s, NEG_INF)
        sk = sink_ref[kh]
        m = jnp.maximum(jnp.max(s, axis=-1, keepdims=True), sk)
        e = jnp.exp2(s - m)
        den = jnp.sum(e, axis=-1, keepdims=True) + jnp.exp2(sk - m)
        p = (e * (1.0 / den)).astype(BF16)
        o = jnp.dot(p, v[:, kh * A_HD:(kh + 1) * A_HD], preferred_element_type=F32)
        for g in range(A_GROUP):
            outs.append(o[g * nq:(g + 1) * nq])
    return jnp.concatenate(outs, axis=1)


def _attn_a_prompt_kernel(qb, nsub, q_ref, kvc_ref, kvp_ref, bias_ref, sink_ref, o_ref, t_ref, p_ref):
    cb = pl.program_id(1)
    back = CHUNK * WIN_CHUNKS
    nk = back + qb
    nkv = A_KV_HEADS * A_HD
    kvfull = jnp.concatenate([kvp_ref[...], kvc_ref[...]], axis=0)
    k_all = kvfull[:, :nkv].astype(BF16)
    vt_all = kvfull[:, nkv:].T.astype(BF16)
    dn = (((1,), (1,)), ((), ()))
    units = [(s, kh) for s in range(nsub) for kh in range(A_KV_HEADS)]
    for s, kh in units:
        qs = jnp.concatenate([q_ref[s * qb:(s + 1) * qb, (kh * A_GROUP + g) * A_HD:(kh * A_GROUP + g + 1) * A_HD]
                              for g in range(A_GROUP)], axis=0)
        t_ref[s * A_KV_HEADS + kh] = lax.dot_general(k_all[s * qb:s * qb + nk, kh * A_HD:(kh + 1) * A_HD], qs, dn,
                                                     preferred_element_type=F32)
    for s in range(nsub):
        outs = []
        for kh in range(A_KV_HEADS):
            u = s * A_KV_HEADS + kh
            st = t_ref[u] + bias_ref[kh]
            if s == 0:
                key_pos = lax.broadcasted_iota(jnp.int32, (nk, 1), 0) + (cb * nsub * qb - back)
                st = st + jnp.where(key_pos >= 0, 0.0, NEG_INF)
            sk = sink_ref[kh]
            m = jnp.maximum(jnp.max(st, axis=0, keepdims=True), sk)
            e = jnp.exp2(st - m)
            inv_den = 1.0 / (jnp.sum(e, axis=0, keepdims=True) + jnp.exp2(sk - m))
            p_ref[u] = e.astype(BF16)
            ot = jnp.dot(vt_all[kh * A_HD:(kh + 1) * A_HD, s * qb:s * qb + nk], p_ref[u],
                         preferred_element_type=F32)
            o = (ot * inv_den).T
            for g in range(A_GROUP):
                outs.append(o[g * qb:(g + 1) * qb])
        o_ref[s * qb:(s + 1) * qb, :] = jnp.concatenate(outs, axis=1).astype(o_ref.dtype)


def _a_prompt_tables(qb, sinks):
    back = CHUNK * WIN_CHUNKS
    slopes = _alibi(A_HEADS).reshape(A_KV_HEADS, A_GROUP)
    qpos = np.arange(qb)
    kpos = np.arange(back + qb) - back
    dist = np.abs(kpos[:, None] - qpos[None, :]).astype(np.float64)
    band_lo = (qpos // CHUNK) * CHUNK - back
    in_band = (kpos[:, None] >= band_lo[None, :]) & (kpos[:, None] < band_lo[None, :] + back + CHUNK)
    bias = np.where(in_band[None, None], -slopes[:, :, None, None] * dist[None, None], -np.inf)
    bias = np.transpose(bias, (0, 2, 1, 3)).reshape(A_KV_HEADS, back + qb, A_GROUP * qb) * LOG2E
    sk = jnp.repeat(sinks.astype(F32).reshape(A_KV_HEADS, A_GROUP), qb, axis=1)[:, None, :] * LOG2E
    return jnp.asarray(bias, F32), sk


def _a_tables(qpos, kpos, sinks):
    slopes = _alibi(A_HEADS).reshape(A_KV_HEADS, A_GROUP)
    dist = np.abs(qpos[:, None] - kpos[None, :]).astype(np.float64)
    bias = -slopes[:, :, None, None] * dist
    nq = len(qpos)
    bias = jnp.asarray(bias.reshape(A_KV_HEADS, A_GROUP * nq, len(kpos)) * LOG2E, F32)
    sk = jnp.repeat(sinks.astype(F32).reshape(A_KV_HEADS, A_GROUP), nq, axis=1)[..., None] * LOG2E
    return bias, sk


def attn_a_prompt(q_all, kv_all, sinks, batch, t, qb):
    nq = A_HEADS * A_HD
    nkv2 = 2 * A_KV_HEADS * A_HD
    back = CHUNK * WIN_CHUNKS
    bias, sk = _a_prompt_tables(qb, sinks)
    nsub = 2 if t % (2 * qb) == 0 else 1
    rows = nsub * qb
    nblk = t // rows
    r = rows // back
    return pl.pallas_call(
        functools.partial(_attn_a_prompt_kernel, qb, nsub),
        grid=(batch, nblk),
        in_specs=[
            pl.BlockSpec((rows, nq), lambda b, i: (b * nblk + i, 0)),
            pl.BlockSpec((rows, nkv2), lambda b, i: (b * nblk + i, 0)),
            pl.BlockSpec((back, nkv2), lambda b, i: (jnp.maximum((b * nblk + i) * r - 1, 0), 0)),
            pl.BlockSpec(bias.shape, lambda b, i: (0, 0, 0)),
            pl.BlockSpec(sk.shape, lambda b, i: (0, 0, 0)),
        ],
        out_specs=pl.BlockSpec((rows, nq), lambda b, i: (b * nblk + i, 0)),
        out_shape=jax.ShapeDtypeStruct((batch * t, nq), BF16),
        scratch_shapes=[pltpu.VMEM((nsub * A_KV_HEADS, back + qb, A_GROUP * qb), F32),
                        pltpu.VMEM((nsub * A_KV_HEADS, back + qb, A_GROUP * qb), BF16)],
        compiler_params=_cparams(("arbitrary", "arbitrary")), name="attn_a_prompt",
    )(q_all, kv_all, kv_all, bias, sk)


def _attn_a_sample_kernel(q_ref, k_ref, v_ref, bias_ref, sink_ref, o_ref):
    o = _attn_a_chunk(q_ref[...], k_ref[...].astype(BF16), v_ref[...].astype(BF16), bias_ref, sink_ref, None)
    o_ref[...] = o.astype(o_ref.dtype)


def attn_a_sample(q_all, kband, vband, sinks, row0, dec_batch, dec_t):
    nq = A_HEADS * A_HD
    s = kband.shape[1]
    past = s - dec_t
    kpos = np.concatenate([np.arange(past) - past, np.arange(dec_t)])
    bias, sk = _a_tables(np.arange(dec_t), kpos, sinks)
    blk0 = row0 // dec_t
    return pl.pallas_call(
        _attn_a_sample_kernel,
        grid=(dec_batch,),
        in_specs=[
            pl.BlockSpec((dec_t, nq), lambda b: (blk0 + b, 0)),
            pl.BlockSpec((None, s, kband.shape[2]), lambda b: (b, 0, 0)),
            pl.BlockSpec((None, s, vband.shape[2]), lambda b: (b, 0, 0)),
            pl.BlockSpec(bias.shape, lambda b: (0, 0, 0)),
            pl.BlockSpec(sk.shape, lambda b: (0, 0, 0)),
        ],
        out_specs=pl.BlockSpec((dec_t, nq), lambda b: (b, 0)),
        out_shape=jax.ShapeDtypeStruct((dec_batch * dec_t, nq), BF16),
        compiler_params=_cparams(("arbitrary",)), name="attn_a_sample",
    )(q_all, kband, vband, bias, sk)


B_Q_SCALE = B_HD ** -0.5 * float(np.log2(np.e))
LOG2E = float(np.log2(np.e))


def _stack_maps(qh):
    lane = lax.broadcasted_iota(jnp.int32, qh.shape, 1)
    zero = jnp.zeros_like(qh)
    return jnp.concatenate([jnp.where(lane < B_HD, qh, zero), jnp.where(lane >= B_HD, qh, zero)], axis=0)


def _diff_finish(acc, l, nq, lam, lam_init, subln):
    inv = 1.0 / l
    o = acc[:nq] * inv[:nq] - lam * (acc[nq:] * inv[nq:])
    ms = jnp.mean(o * o, axis=-1, keepdims=True)
    return o * lax.rsqrt(ms + SUBLN_EPS) * subln * (1.0 - lam_init)


def _attn_b_prompt_kernel(qb, nb, lam_init, pair_ref, slope_ref, lam_ref, q_ref, k_ref, vt_ref, kbias_ref,
                          subln_ref, o_ref, acc_ref, m_ref, al_ref, qq_ref, t0_ref, t1_ref, p0_ref, p1_ref):
    g = pl.program_id(1)
    lam = lam_ref[0]
    dn = (((1,), (1,)), ((), ()))
    t_refs = (t0_ref, t1_ref)
    p_refs = (p0_ref, p1_ref)
    heads = range(HEAD_GROUP)
    n_pairs = nb * (nb + 1) // 2
    slopes = [slope_ref[g * HEAD_GROUP + u] for u in heads]

    def pair(n):
        return pair_ref[2 * n], pair_ref[2 * n + 1]

    def scores(n, slot):
        j, iq = pair(n)
        ks = pl.multiple_of(j * qb, qb)
        for u in heads:
            t_refs[slot][u] = lax.dot_general(k_ref[u, pl.ds(ks, qb), :], qq_ref[u * nb + iq], dn,
                                              preferred_element_type=F32)

    def softmax_step(n, slot):
        j, iq = pair(n)
        on_diag = (j == iq).astype(jnp.int32)
        for u in heads:
            st = u * nb + iq
            m = m_ref[st, 0:1, :]
            t = t_refs[slot][u] + kbias_ref[u, on_diag]
            off = slopes[u] * jnp.full((1, 2 * qb), (j - iq) * qb, jnp.int32).astype(F32)
            m_new = jnp.maximum(m, jnp.max(t, axis=0, keepdims=True) + off)
            p_refs[slot][u] = jnp.exp2(t + (off - m_new)).astype(BF16)
            al_ref[slot * HEAD_GROUP + u, 0:1, :] = jnp.exp2(m - m_new)
            m_ref[st, 0:1, :] = m_new

    def accumulate(n, slot):
        j, iq = pair(jnp.maximum(n, 0))
        ks = pl.multiple_of(j * qb, qb)
        for u in heads:
            st = u * nb + iq
            pv = jnp.dot(vt_ref[u, :, pl.ds(ks, qb)], p_refs[slot][u], preferred_element_type=F32)
            acc_ref[st] = al_ref[slot * HEAD_GROUP + u, 0:1, :] * acc_ref[st] + pv

    def step(n, slot):
        softmax_step(n, slot)
        accumulate(n - 1, 1 - slot)
        scores(jnp.minimum(n + 1, n_pairs - 1), 1 - slot)
        return 0

    acc_ref[...] = jnp.zeros_like(acc_ref)
    m_ref[...] = jnp.full(m_ref.shape, NEG_INF, F32)
    al_ref[...] = jnp.ones_like(al_ref)
    p1_ref[...] = jnp.zeros_like(p1_ref)
    for u in heads:
        for iq in range(nb):
            qq_ref[u * nb + iq] = _stack_maps(q_ref[u, iq * qb:(iq + 1) * qb, :])
    scores(0, 0)
    def steps(k, c):
        for r in range(PAIR_UNROLL):
            step(PAIR_UNROLL * k + r, r % 2)
        return c

    lax.fori_loop(0, n_pairs // PAIR_UNROLL, steps, 0)
    for n in range(n_pairs - n_pairs % PAIR_UNROLL, n_pairs):
        step(n, n % 2)
    accumulate(n_pairs - 1, (n_pairs - 1) % 2)
    for u in heads:
        for iq in range(nb):
            inv = 1.0 / acc_ref[u * nb + iq, LANES:LANES + 1, :]
            acc = acc_ref[u * nb + iq, :LANES, :]
            ot = acc[:, :qb] * inv[:, :qb] - lam * (acc[:, qb:] * inv[:, qb:])
            ms = jnp.mean(ot * ot, axis=0, keepdims=True)
            ot = ot * lax.rsqrt(ms + SUBLN_EPS) * subln_ref[...] * (1.0 - lam_init)
            o_ref[u, iq * qb:(iq + 1) * qb, :] = ot.T.astype(o_ref.dtype)


def attn_b_prompt(qkvh, vt, lam, subln, lam_init, batch, t, qb):
    nblk = t // qb
    qq = np.arange(qb)
    slope2 = (_alibi(B_HEADS) * LOG2E)[:, None, None]
    kcol = np.broadcast_to(qq[:, None], (qb, qb)).astype(np.float64)
    tri = -2.0 * np.maximum(qq[:, None] - qq[None, :], 0)
    msk = np.where((qq[:, None] // CHUNK) <= (qq[None, :] // CHUNK), 0.0, -np.inf)
    both = lambda a: np.concatenate([a, a], axis=-1)
    kbias = np.stack([both(slope2 * kcol[None]), both(slope2 * (kcol + tri)[None] + msk[None])], axis=1)
    kbias = jnp.asarray(kbias, F32)
    slopes = jnp.asarray(_alibi(B_HEADS) * LOG2E, F32)
    subln_t = jnp.broadcast_to(subln.astype(F32).reshape(LANES, 1), (LANES, qb))
    pairs = jnp.asarray([v for j in range(nblk) for i in range(j, nblk) for v in (j, i)], jnp.int32)
    hg = HEAD_GROUP
    n_groups = B_HEADS // hg
    grid_spec = pltpu.PrefetchScalarGridSpec(
        num_scalar_prefetch=3, grid=(batch, n_groups),
        in_specs=[
            pl.BlockSpec((hg, t, LANES), lambda b, g, *_: (g, b, 0)),
            pl.BlockSpec((hg, t, LANES), lambda b, g, *_: (n_groups + g, b, 0)),
            pl.BlockSpec((hg, LANES + ONES_ROWS, t), lambda b, g, *_: (g, 0, b)),
            pl.BlockSpec((hg,) + kbias.shape[1:], lambda b, g, *_: (g, 0, 0, 0)),
            pl.BlockSpec(subln_t.shape, lambda b, g, *_: (0, 0)),
        ],
        out_specs=pl.BlockSpec((hg, t, LANES), lambda b, g, *_: (g, b, 0)),
        scratch_shapes=[pltpu.VMEM((hg * nblk, LANES + ONES_ROWS, 2 * qb), F32),
                        pltpu.VMEM((hg * nblk, SUBLANES, 2 * qb), F32),
                        pltpu.VMEM((2 * hg, SUBLANES, 2 * qb), F32),
                        pltpu.VMEM((hg * nblk, 2 * qb, LANES), BF16),
                        pltpu.VMEM((hg, qb, 2 * qb), F32), pltpu.VMEM((hg, qb, 2 * qb), F32),
                        pltpu.VMEM((hg, qb, 2 * qb), BF16), pltpu.VMEM((hg, qb, 2 * qb), BF16)],
    )
    return pl.pallas_call(
        functools.partial(_attn_b_prompt_kernel, qb, nblk, lam_init), grid_spec=grid_spec,
        out_shape=jax.ShapeDtypeStruct((B_HEADS, batch * t, LANES), BF16),
        compiler_params=_cparams(("arbitrary", "arbitrary")), name="attn_b_prompt",
    )(pairs, slopes, lam.reshape(1).astype(F32), qkvh, qkvh, vt, kbias, subln_t)


def _attn_b_sample_kernel(dec_t, past, lam_init, slopes, lam_ref, q_ref, kn_ref, vn_ref, ck_ref, cv_ref,
                          tnew_ref, subln_ref, o_ref):
    lam = lam_ref[0]
    subln = subln_ref[...]
    colc = lax.broadcasted_iota(jnp.int32, (1, past), 1).astype(F32) - float(past)
    for h in range(B_HEADS):
        slope = float(slopes[h]) * LOG2E
        qq = _stack_maps(q_ref[h])
        kc = ck_ref[pl.ds(h, past, stride=B_HEADS), :].astype(BF16)
        vc = cv_ref[pl.ds(h, past, stride=B_HEADS), :].astype(BF16)
        dn = (((1,), (1,)), ((), ()))
        s_c = lax.dot_general(qq, kc, dn, preferred_element_type=F32) + slope * colc
        s_n = lax.dot_general(qq, kn_ref[h], dn, preferred_element_type=F32) + slope * tnew_ref[...]
        m = jnp.maximum(jnp.max(s_c, axis=-1, keepdims=True), jnp.max(s_n, axis=-1, keepdims=True))
        p_c = jnp.exp2(s_c - m)
        p_n = jnp.exp2(s_n - m)
        l = jnp.sum(p_c, axis=-1, keepdims=True) + jnp.sum(p_n, axis=-1, keepdims=True)
        acc = (jnp.dot(p_c.astype(BF16), vc, preferred_element_type=F32)
               + jnp.dot(p_n.astype(BF16), vn_ref[h], preferred_element_type=F32))
        o_ref[h] = _diff_finish(acc, l, dec_t, lam, lam_init, subln).astype(o_ref.dtype)


def attn_b_sample(qkvh, cache_k, cache_v, lam, subln, lam_init, row0, dec_batch, dec_t):
    past = cache_k.shape[1]
    qi = np.arange(dec_t)
    tnew = (qi[:, None] - np.abs(qi[:, None] - qi[None, :])).astype(np.float32)
    tnew = jnp.asarray(np.concatenate([tnew, tnew], axis=0))
    blk0 = row0 // dec_t
    rows = past * B_HEADS
    cache_k = cache_k.reshape(dec_batch, rows, LANES)
    cache_v = cache_v.reshape(dec_batch, rows, LANES)
    grid_spec = pltpu.PrefetchScalarGridSpec(
        num_scalar_prefetch=1, grid=(dec_batch,),
        in_specs=[
            pl.BlockSpec((B_HEADS, dec_t, LANES), lambda b, *_: (0, blk0 + b, 0)),
            pl.BlockSpec((B_HEADS, dec_t, LANES), lambda b, *_: (1, blk0 + b, 0)),
            pl.BlockSpec((B_HEADS, dec_t, LANES), lambda b, *_: (2, blk0 + b, 0)),
            pl.BlockSpec((None, rows, LANES), lambda b, *_: (b, 0, 0)),
            pl.BlockSpec((None, rows, LANES), lambda b, *_: (b, 0, 0)),
            pl.BlockSpec(tnew.shape, lambda b, *_: (0, 0)),
            pl.BlockSpec((1, LANES), lambda b, *_: (0, 0)),
        ],
        out_specs=pl.BlockSpec((B_HEADS, dec_t, LANES), lambda b, *_: (0, b, 0)),
    )
    return pl.pallas_call(
        functools.partial(_attn_b_sample_kernel, dec_t, past, lam_init, tuple(_alibi(B_HEADS))),
        grid_spec=grid_spec,
        out_shape=jax.ShapeDtypeStruct((B_HEADS, dec_batch * dec_t, LANES), BF16),
        compiler_params=_cparams(("arbitrary",)), name="attn_b_sample",
    )(lam.reshape(1).astype(F32), qkvh, qkvh, qkvh, cache_k, cache_v, tnew, subln.reshape(1, LANES).astype(F32))


ROUTE_ROWS = 40
ROUTE_FIELDS = 8


def _route_t(lt):
    row = lax.broadcasted_iota(jnp.int32, lt.shape, 0).astype(F32)
    big = float(LANES)
    lg = jnp.where(row < N_GROUPS, lt, NEG_INF)
    gmax = jnp.max(lg, axis=0, keepdims=True)
    g_idx = jnp.min(jnp.where(lg == gmax, row, big), axis=0, keepdims=True)
    g_prob = 1.0 / jnp.sum(jnp.exp(lg - gmax), axis=0, keepdims=True)
    lo = N_GROUPS + EXPERTS_PER_GROUP * g_idx
    le = jnp.where((row >= lo) & (row < lo + EXPERTS_PER_GROUP), lt, NEG_INF)
    m1 = jnp.max(le, axis=0, keepdims=True)
    i1 = jnp.min(jnp.where(le == m1, row, big), axis=0, keepdims=True)
    le2 = jnp.where(row == i1, NEG_INF, le)
    m2 = jnp.max(le2, axis=0, keepdims=True)
    i2 = jnp.min(jnp.where(le2 == m2, row, big), axis=0, keepdims=True)
    t = jnp.exp(m2 - m1)
    ga = 1.0 / (1.0 + t)
    gb = t * ga
    return row, i1, i2, g_prob * ga, g_prob * gb


def _unpack_pairs(u):
    lo = pltpu.bitcast(u << 16, F32)
    hi = pltpu.bitcast(u & jnp.uint32(0xFFFF0000), F32)
    return jnp.concatenate([lo, hi], axis=1)


def _pack_pairs(x):
    bits = pltpu.bitcast(x.astype(BF16).astype(F32), jnp.uint32)
    w = x.shape[1] // 2
    return (bits[:, :w] >> 16) | bits[:, w:]


def _out_router_kernel(n_prompt_blocks, heads_in, two_o, two_res, *refs):
    refs = list(refs)
    oa_ref = refs.pop(0)
    ob_ref = refs.pop(0) if two_o else None
    ra_ref = refs.pop(0)
    rb_ref = refs.pop(0) if two_res else None
    (wo_ref, g_ref, wr_ref, wr2_ref, br_ref, utri_ref, h_ref, xn_ref, meta_ref, route_ref, cnt_ref,
     carry_ref) = refs
    is_prompt = pl.program_id(0) < n_prompt_blocks

    def load_o(ref):
        if heads_in:
            return jnp.concatenate([ref[hh] for hh in range(ref.shape[0])], axis=1)
        return ref[...]

    o = load_o(oa_ref)
    if two_o:
        o = jnp.where(is_prompt, o, load_o(ob_ref))
    res = ra_ref[...]
    if two_res:
        res = jnp.where(is_prompt, res, rb_ref[...])
    h = res + jnp.dot(o, wo_ref[...], preferred_element_type=F32)
    h_ref[...] = h
    ms = jnp.mean(h * h, axis=-1, keepdims=True)
    xn = h * lax.rsqrt(ms + NORM_EPS) * g_ref[...]
    x_hi = xn.astype(BF16)
    _store_planes(xn_ref, _pack_pairs(xn))
    x_lo = (xn - x_hi.astype(F32)).astype(BF16)
    r = jnp.dot(x_hi, wr_ref[...], preferred_element_type=F32)
    logits = (r[:, :LANES] + r[:, LANES:]) + jnp.dot(x_lo, wr2_ref[...], preferred_element_type=F32) + br_ref[...]
    row, i1, i2, g1, g2 = _route_t(logits.T[:ROUTE_ROWS])

    @pl.when(pl.program_id(0) == 0)
    def _():
        carry_ref[...] = jnp.zeros_like(carry_ref)

    sel1 = row == i1
    sel2 = row == i2
    onehot = jnp.where(sel1 | sel2, 1.0, 0.0)
    before = jnp.dot(onehot.astype(BF16), utri_ref[...], preferred_element_type=F32) + carry_ref[:, 0:1]
    rank1 = jnp.sum(jnp.where(sel1, before, 0.0), axis=0, keepdims=True)
    rank2 = jnp.sum(jnp.where(sel2, before, 0.0), axis=0, keepdims=True)
    carry_ref[...] = carry_ref[...] + jnp.sum(onehot, axis=1, keepdims=True)
    cnt_ref[...] = carry_ref[...]
    fields = [i1 - N_GROUPS, i2 - N_GROUPS, g1, g2, rank1, rank2]
    fields += [jnp.zeros_like(g1)] * (ROUTE_FIELDS - len(fields))
    route = jnp.concatenate(fields, axis=0)
    route_ref[...] = route
    meta_ref[...] = jnp.concatenate([route, jnp.zeros((LANES - ROUTE_FIELDS, route.shape[1]), F32)], axis=0).T


def out_router(oa, ob, ra, rb, wo, g, w_rg, b_rg, w_re, b_re, heads_in, tr):
    d = ra.shape[1]
    if heads_in:
        na = oa.shape[1]
        nb = 0 if ob is None else ob.shape[1]
    else:
        na = oa.shape[0]
        nb = 0 if ob is None else ob.shape[0]
    n = na + nb
    npb = na // tr
    two_o = ob is not None
    two_res = rb is not None
    first = lambda i: (jnp.minimum(i, npb - 1), 0)
    second = lambda i: (jnp.maximum(i - npb, 0), 0)
    plain = lambda i: (i, 0)
    in_specs, args = [], []

    def add_o(x, imap):
        if heads_in:
            in_specs.append(pl.BlockSpec((x.shape[0], tr, LANES), lambda i: (0, imap(i)[0], 0)))
        else:
            in_specs.append(pl.BlockSpec((tr, x.shape[1]), imap))
        args.append(x)

    add_o(oa, first if two_o else plain)
    if two_o:
        add_o(ob, second)
    in_specs.append(pl.BlockSpec((tr, d), first if two_res else plain))
    args.append(ra)
    if two_res:
        in_specs.append(pl.BlockSpec((tr, d), second))
        args.append(rb)
    wr = jnp.zeros((d, LANES), F32).at[:, :N_GROUPS].set(w_rg.astype(F32))
    wr = wr.at[:, N_GROUPS:N_GROUPS + N_EXPERTS].set(w_re.astype(F32))
    wr_hi = wr.astype(BF16)
    wr_lo = (wr - wr_hi.astype(F32)).astype(BF16)
    br = jnp.zeros((1, LANES), F32).at[0, :N_GROUPS].set(b_rg.astype(F32))
    br = br.at[0, N_GROUPS:N_GROUPS + N_EXPERTS].set(b_re.astype(F32))
    utri = jnp.asarray(np.triu(np.ones((tr, tr), np.float32), 1), BF16)
    consts = [wo, g.reshape(1, d).astype(F32), jnp.concatenate([wr_hi, wr_lo], axis=1), wr_hi, br, utri]
    for cst in consts:
        in_specs.append(pl.BlockSpec(cst.shape, lambda i: (0, 0)))
        args.append(cst)
    parts = d // 2 // SC_ROW_WORDS
    out_shape = [jax.ShapeDtypeStruct((n, d), F32), jax.ShapeDtypeStruct((parts, n, SC_ROW_WORDS), jnp.uint32),
                 jax.ShapeDtypeStruct((n, LANES), F32), jax.ShapeDtypeStruct((ROUTE_FIELDS, n), F32),
                 jax.ShapeDtypeStruct((ROUTE_ROWS, LANES), F32)]
    out_specs = [pl.BlockSpec((tr, d), plain), pl.BlockSpec((parts, tr, SC_ROW_WORDS), lambda i: (0, i, 0)),
                 pl.BlockSpec((tr, LANES), plain), pl.BlockSpec((ROUTE_FIELDS, tr), lambda i: (0, i)),
                 pl.BlockSpec((ROUTE_ROWS, LANES), lambda i: (0, 0))]
    return pl.pallas_call(
        functools.partial(_out_router_kernel, npb, heads_in, two_o, two_res),
        grid=(n // tr,), in_specs=in_specs, out_specs=out_specs, out_shape=out_shape,
        scratch_shapes=[pltpu.VMEM((ROUTE_ROWS, LANES), F32)],
        compiler_params=_cparams(("arbitrary",)), name="out_router")(*args)


def _expert_kernel(tm, tps, te_ref, nt_ref, x_ref, *refs):
    w_refs, o_ref = refs[:3 * tps], refs[3 * tps]
    first = pl.program_id(0) * tps
    n_valid = jnp.clip(nt_ref[0] - first, 0, tps)

    def tile(k):
        wg_ref, wu_ref, wd_ref = w_refs[3 * k:3 * k + 3]
        rows = slice(k * tm, (k + 1) * tm)
        x = _unpack_pairs(jnp.concatenate([x_ref[q, rows, :] for q in range(x_ref.shape[0])], axis=1)).astype(BF16)
        a = jnp.dot(x, wg_ref[...].astype(BF16), preferred_element_type=F32)
        u = jnp.dot(x, wu_ref[...].astype(BF16), preferred_element_type=F32)
        hh = (a * (1.0 / (1.0 + jnp.exp(-a))) * u).astype(BF16)
        packed = _pack_pairs(jnp.dot(hh, wd_ref[...].astype(BF16), preferred_element_type=F32))
        for q in range(o_ref.shape[0]):
            o_ref[q, rows, :] = packed[:, q * o_ref.shape[2]:(q + 1) * o_ref.shape[2]]

    for nv in range(tps + 1):

        @pl.when(n_valid == nv)
        def _(nv=nv):
            for k in range(nv):
                tile(k)
            if nv < tps:
                o_ref[:, nv * tm:, :] = jnp.zeros((o_ref.shape[0], (tps - nv) * tm, o_ref.shape[2]), o_ref.dtype)


def expert_mlp(xs, tile_expert, n_tiles, w_gate, w_up, w_down, layer, tm):
    parts, p, w = xs.shape
    d, f = w_gate.shape[-2:]
    epg = w_gate.shape[2]
    tps = 2 if (p // tm) % 2 == 0 else 1
    w_specs, w_args = [], []
    for k in range(tps):
        wmap = lambda t, te, nt, k=k: (layer, te[tps * t + k] // epg, te[tps * t + k] % epg, 0, 0)
        w_specs += [pl.BlockSpec((None, None, None, d, f), wmap), pl.BlockSpec((None, None, None, d, f), wmap),
                    pl.BlockSpec((None, None, None, f, d), wmap)]
        w_args += [w_gate, w_up, w_down]
    grid_spec = pltpu.PrefetchScalarGridSpec(
        num_scalar_prefetch=2, grid=(p // (tps * tm),),
        in_specs=[pl.BlockSpec((parts, tps * tm, w), lambda t, te, nt: (0, t, 0))] + w_specs,
        out_specs=pl.BlockSpec((parts, tps * tm, w), lambda t, te, nt: (0, t, 0)),
    )
    return pl.pallas_call(
        functools.partial(_expert_kernel, tm, tps), grid_spec=grid_spec,
        out_shape=jax.ShapeDtypeStruct((parts, p, w), jnp.uint32),
        compiler_params=_cparams(("arbitrary",)), name="expert_mlp")(tile_expert, n_tiles, xs, *w_args)


def sc_scatter_rows(x2, idx, n_out):
    mesh = plsc.VectorSubcoreMesh(core_axis_name="c", subcore_axis_name="s")
    n_src_win = x2.shape[0] // SC_WINDOW

    @functools.partial(pl.kernel, out_type=jax.ShapeDtypeStruct((n_out, SC_ROW_WORDS), x2.dtype), mesh=mesh)
    def scatter(x_hbm, i_hbm, o_hbm):
        def body(x_vmem, i_vmem):
            pltpu.sync_copy(x_vmem, o_hbm.at[i_vmem.at[0]])

        pltpu.emit_pipeline(
            body, grid=(n_out // SC_WINDOW,),
            in_specs=[pl.BlockSpec((SC_WINDOW, SC_ROW_WORDS), lambda i: (lax.rem(i, n_src_win), 0)),
                      pl.BlockSpec((1, SC_WINDOW), lambda i: (0, i))],
            out_specs=[], core_axis_name=("c", "s"), dimension_semantics=(pltpu.PARALLEL,),
        )(x_hbm, i_hbm)

    return scatter(x2, idx)


def sc_gather_rows(x2, idx):
    mesh = plsc.VectorSubcoreMesh(core_axis_name="c", subcore_axis_name="s")
    n_col, n = idx.shape

    @functools.partial(pl.kernel, out_type=jax.ShapeDtypeStruct((n, n_col * SC_ROW_WORDS), x2.dtype), mesh=mesh)
    def gather(x_hbm, i_hbm, o_hbm):
        def body(i_vmem, o_vmem):
            pltpu.sync_copy(x_hbm.at[i_vmem.at[0]], o_vmem)

        pltpu.emit_pipeline(
            body, grid=(n // SC_WINDOW, n_col),
            in_specs=[pl.BlockSpec((1, SC_WINDOW), lambda i, c: (c, i))],
            out_specs=[pl.BlockSpec((SC_WINDOW, SC_ROW_WORDS), lambda i, c: (i, c))],
            core_axis_name=("c", "s"), dimension_semantics=(pltpu.PARALLEL, pltpu.ARBITRARY),
        )(i_hbm, o_hbm)

    return gather(x2, idx)


def moe_dispatch(xn_packed, route, counts, tm):
    parts, n, w = xn_packed.shape
    e = route[0:2].astype(jnp.int32)
    rank = route[4:6].astype(jnp.int32)
    counts = counts[N_GROUPS:N_GROUPS + N_EXPERTS, 0].astype(jnp.int32)
    padded = ((counts + tm - 1) // tm) * tm
    pend = jnp.cumsum(padded)
    pstart = pend - padded
    ids = jnp.arange(N_EXPERTS, dtype=jnp.int32)
    pos = jnp.sum(jnp.where(e[..., None] == ids, pstart, 0), axis=-1) + rank
    p = 2 * n + N_EXPERTS * tm
    n_pad = p - 2 * n
    gap_len = jnp.concatenate([padded - counts, (p - pend[-1]).reshape(1)])
    gap_first = jnp.concatenate([pstart + counts, pend[-1:]])
    gap_end = jnp.cumsum(gap_len)
    k = jnp.arange(n_pad, dtype=jnp.int32)
    gap = jnp.sum((k[:, None] >= gap_end[None, :]).astype(jnp.int32), axis=1)
    shift = gap_first - (gap_end - gap_len)
    pad_pos = k + jnp.sum(jnp.where(gap[:, None] == jnp.arange(N_EXPERTS + 1), shift, 0), axis=-1)
    planes = lambda rows: jnp.concatenate([rows + q * p for q in range(parts)])
    scat_idx = jnp.concatenate([planes(pos[0]), planes(pos[1]), planes(pad_pos)]).reshape(1, parts * p)
    gath_idx = jnp.stack([pos[k] + q * p for k in range(2) for q in range(parts)])
    xs = sc_scatter_rows(xn_packed.reshape(parts * n, w), scat_idx, parts * p).reshape(parts, p, w)
    tile_start = jnp.arange(p // tm, dtype=jnp.int32) * tm
    tile_expert = jnp.sum((tile_start[:, None] >= pend[None, :]).astype(jnp.int32), axis=1)
    tile_expert = jnp.minimum(tile_expert, N_EXPERTS - 1)
    n_tiles = (pend[-1] // tm).astype(jnp.int32).reshape(1)
    return xs, tile_expert, n_tiles, gath_idx


def moe_experts(xn_packed, route, counts, w_gate, w_up, w_down, layer, tm):
    parts, n, w = xn_packed.shape
    xs, tile_expert, n_tiles, gath_idx = moe_dispatch(xn_packed, route, counts, tm)
    o = expert_mlp(xs, tile_expert, n_tiles, w_gate, w_up, w_down, layer, tm)
    return sc_gather_rows(o.reshape(parts * o.shape[1], w), gath_idx)


def _final_norm_kernel(n_prompt_blocks, h_ref, o0_ref, o1_ref, meta_ref, g_ref, yp_ref, ys_ref):
    meta = meta_ref[...]
    x = (h_ref[...] + meta[:, 2:3] * _unpack_pairs(o0_ref[...])) + meta[:, 3:4] * _unpack_pairs(o1_ref[...])
    ms = jnp.mean(x * x, axis=-1, keepdims=True)
    y = x * lax.rsqrt(ms + NORM_EPS) * g_ref[...]
    i = pl.program_id(0)

    @pl.when(i < n_prompt_blocks)
    def _():
        yp_ref[...] = y

    @pl.when(i >= n_prompt_blocks)
    def _():
        ys_ref[...] = y


def final_norm(h, og, meta, g, n_prompt, tr):
    n, d = h.shape
    npb = n_prompt // tr
    return pl.pallas_call(
        functools.partial(_final_norm_kernel, npb),
        grid=(n // tr,),
        in_specs=[pl.BlockSpec((tr, d), lambda i: (i, 0)),
                  pl.BlockSpec((tr, og.shape[1] // 2), lambda i: (i, 0)),
                  pl.BlockSpec((tr, og.shape[1] // 2), lambda i: (i, 1)),
                  pl.BlockSpec((tr, LANES), lambda i: (i, 0)), pl.BlockSpec((1, d), lambda i: (0, 0))],
        out_specs=[pl.BlockSpec((tr, d), lambda i: (jnp.minimum(i, npb - 1), 0)),
                   pl.BlockSpec((tr, d), lambda i: (jnp.maximum(i - npb, 0), 0))],
        out_shape=[jax.ShapeDtypeStruct((n_prompt, d), F32), jax.ShapeDtypeStruct((n - n_prompt, d), F32)],
        compiler_params=_cparams(("arbitrary",)), name="final_norm")(h, og, og, meta, g.reshape(1, d).astype(F32))


def _pick_rows_tile(n_prompt, n_sample):
    tr = ROW_TILE
    while tr >= SUBLANES:
        if n_prompt % tr == 0 and n_sample % tr == 0:
            return tr
        tr //= 2
    raise ValueError("row counts must be multiples of the sublane count")


def kernel(x_prompt, x_sample, cache_a_k, cache_a_v, cache_b_k, cache_b_v, norm_mix, norm_ffn, norm_final,
           w_a_qkv, b_a_qkv, a_sinks, w_a_o, w_b_qkv, b_lambda, b_subln, w_b_o,
           w_route_group, b_route_group, w_route_expert, b_route_expert, w_gate, w_up, w_down):
    batch, t, d = x_prompt.shape
    dec_batch, dec_t, _ = x_sample.shape
    n_p, n_s = batch * t, dec_batch * dec_t
    tr = _pick_rows_tile(n_p, n_s)
    tm = min(EXPERT_TILE, 4 * tr)
    xp = x_prompt.reshape(n_p, d)
    xs = x_sample.reshape(n_s, d)
    nq_a = A_HEADS * A_HD
    nkv_a = A_KV_HEADS * A_HD

    a_scale = jnp.where(jnp.arange(nq_a + 2 * nkv_a) < nq_a, A_HD ** -0.5 * LOG2E, 1.0).astype(F32)
    q_a, kv_ap, kv_as = norm_proj(xp, xs, norm_mix[0], (w_a_qkv[0] * a_scale).astype(BF16), b_a_qkv[0] * a_scale,
                                  [(0, nq_a, "flat", BF16), (nq_a, nq_a + 2 * nkv_a, "split", F32)], tr, n_p)
    qb_a = min(ATTN_BLOCK, t)
    o_ap = attn_a_prompt(q_a, kv_ap, a_sinks[0], batch, t, qb_a)
    kv_s = kv_as.reshape(dec_batch, dec_t, 2 * nkv_a)
    past_a = cache_a_k.shape[2]
    kband = jnp.concatenate([cache_a_k[0].reshape(dec_batch, past_a, nkv_a), kv_s[..., :nkv_a]], axis=1)
    vband = jnp.concatenate([cache_a_v[0].reshape(dec_batch, past_a, nkv_a), kv_s[..., nkv_a:]], axis=1)
    o_as = attn_a_sample(q_a, kband, vband, a_sinks[0], n_p, dec_batch, dec_t)
    h, xn, meta, route, counts = out_router(o_ap, o_as, xp, xs, w_a_o[0].astype(BF16), norm_ffn[0], w_route_group[0],
                                            b_route_group[0], w_route_expert[0], b_route_expert[0], False, tr)
    og = moe_experts(xn, route, counts, w_gate, w_up, w_down, 0, tm)

    keep = min(CHUNK * WIN_CHUNKS, t)
    kv_p = kv_ap.reshape(batch, t, 2 * nkv_a)[:, t - keep:]
    a_k_prompt = kv_p[..., :nkv_a].reshape(1, batch, keep, A_KV_HEADS, A_HD)
    a_v_prompt = kv_p[..., nkv_a:].reshape(1, batch, keep, A_KV_HEADS, A_HD)
    a_k_sample = kband[:, dec_t:].reshape(1, dec_batch, past_a, A_KV_HEADS, A_HD)
    a_v_sample = vband[:, dec_t:].reshape(1, dec_batch, past_a, A_KV_HEADS, A_HD)

    lam_init = 0.8 - 0.6 * float(np.exp(-0.3 * 1))
    lp = b_lambda[0].astype(F32)
    lam = jnp.exp(jnp.sum(lp[0] * lp[1])) - jnp.exp(jnp.sum(lp[2] * lp[3])) + lam_init
    wb = B_HEADS * 2 * B_HD
    col_scale = jnp.where(jnp.arange(3 * wb) < wb, B_Q_SCALE, 1.0).astype(F32)
    h, qkvh, k_bp, k_bs, v_bp, v_bs, vt = norm_proj(
        h, None, norm_mix[1], (w_b_qkv[0] * col_scale).astype(BF16), None,
        [(0, 3 * wb, "heads", BF16), (wb, 2 * wb, "split", F32), (2 * wb, 3 * wb, "split", F32),
         (2 * wb, 3 * wb, "heads_t", BF16)], tr, n_p, og=og, meta=meta)
    qb_b = min(ATTN_BLOCK, t)
    o_bp = attn_b_prompt(qkvh, vt, lam, b_subln[0], lam_init, batch, t, qb_b)
    o_bs = attn_b_sample(qkvh, cache_b_k[0], cache_b_v[0], lam, b_subln[0], lam_init, n_p, dec_batch, dec_t)
    h, xn, meta, route, counts = out_router(o_bp, o_bs, h, None, w_b_o[0].astype(BF16), norm_ffn[1], w_route_group[1],
                                            b_route_group[1], w_route_expert[1], b_route_expert[1], True, tr)
    og = moe_experts(xn, route, counts, w_gate, w_up, w_down, 1, tm)

    y_p, y_s = final_norm(h, og, meta, norm_final, n_p, tr)
    hd2 = 2 * B_HD
    return (y_p.reshape(batch, t, d), y_s.reshape(dec_batch, dec_t, d),
            a_k_prompt, a_v_prompt, a_k_sample, a_v_sample,
            k_bp.reshape(1, batch, t, B_HEADS, hd2), v_bp.reshape(1, batch, t, B_HEADS, hd2),
            k_bs.reshape(1, dec_batch, dec_t, B_HEADS, hd2), v_bs.reshape(1, dec_batch, dec_t, B_HEADS, hd2))
```

```python
import functools

import jax
import jax.numpy as jnp
import numpy as np
from jax import lax
from jax.experimental import pallas as pl
from jax.experimental.pallas import tpu as pltpu
from jax.experimental.pallas import tpu_sc as plsc

F32 = jnp.float32
BF16 = jnp.bfloat16
NEG_INF = float("-inf")

CHUNK = 64
WIN_CHUNKS = 2
A_HEADS, A_KV_HEADS, A_GROUP, A_HD = 16, 4, 4, 64
B_HEADS, B_HD = 8, 64
N_GROUPS, EXPERTS_PER_GROUP = 4, 8
N_EXPERTS = N_GROUPS * EXPERTS_PER_GROUP
NORM_EPS = 1e-6
SUBLN_EPS = 1e-5

LANES = 128
SUBLANES = 8
ROW_TILE = 512
EXPERT_TILE = 512
ATTN_BLOCK = 256
COL_CHUNK = 512
PAIR_UNROLL = 12
HEAD_GROUP = 2
ONES_ROWS = 16
SC_WINDOW = 128
SC_ROW_WORDS = 256
VMEM_LIMIT = 48 * 1024 * 1024


def _cparams(sem, flags=None):
    return pltpu.CompilerParams(dimension_semantics=sem, vmem_limit_bytes=VMEM_LIMIT, flags=flags)


def _alibi(n):
    return 2.0 ** (-8.0 * np.arange(1, n + 1) / n)


def _store_planes(ref, x):
    w = ref.shape[2]
    for p in range(ref.shape[0]):
        ref[p] = x[:, p * w:(p + 1) * w]


def _add_expert_outputs(x, og_ref, meta_ref):
    half = og_ref.shape[1] // 2
    meta = meta_ref[...]
    return (x + meta[:, 2:3] * _unpack_pairs(og_ref[:, :half])) + meta[:, 3:4] * _unpack_pairs(og_ref[:, half:])


def _norm_proj_kernel(n_prompt_blocks, two_src, has_bias, combine, outs, *refs):
    refs = list(refs)
    xa_ref = refs.pop(0)
    xb_ref = refs.pop(0) if two_src else None
    og_ref, meta_ref = (refs.pop(0), refs.pop(0)) if combine else (None, None)
    g_ref = refs.pop(0)
    w_ref = refs.pop(0)
    b_ref = refs.pop(0) if has_bias else None
    out_refs = refs
    x = xa_ref[...]
    if two_src:
        x = jnp.where(pl.program_id(0) < n_prompt_blocks, x, xb_ref[...])
    if combine:
        x = _add_expert_outputs(x, og_ref, meta_ref)
        out_refs.pop(0)[...] = x
    ms = jnp.mean(x * x, axis=-1, keepdims=True)
    xn = (x * lax.rsqrt(ms + NORM_EPS) * g_ref[...]).astype(BF16)
    is_prompt = pl.program_id(0) < n_prompt_blocks
    out_refs = list(out_refs)
    targets = []
    for c0, c1, layout in outs:
        targets.append((c0, c1, layout, out_refs.pop(0), out_refs.pop(0) if layout == "split" else None))
    for s in range(0, w_ref.shape[1], COL_CHUNK):
        e = s + COL_CHUNK
        r = jnp.dot(xn, w_ref[:, s:e], preferred_element_type=F32)
        if has_bias:
            r = r + b_ref[:, s:e]
        for c0, c1, layout, o_ref, o2_ref in targets:
            if not (c0 <= s and e <= c1):
                continue
            rr = r.astype(o_ref.dtype)
            if layout == "flat":
                o_ref[:, s - c0:e - c0] = rr
            elif layout == "split":

                @pl.when(is_prompt)
                def _(rr=rr, o_ref=o_ref, s=s, e=e, c0=c0):
                    o_ref[:, s - c0:e - c0] = rr

                @pl.when(jnp.logical_not(is_prompt))
                def _(rr=rr, o2_ref=o2_ref, s=s, e=e, c0=c0):
                    o2_ref[:, s - c0:e - c0] = rr
            elif layout == "heads":
                for hh in range((e - s) // LANES):
                    o_ref[(s - c0) // LANES + hh] = rr[:, hh * LANES:(hh + 1) * LANES]
            else:
                ones = jnp.ones((ONES_ROWS, r.shape[0]), o_ref.dtype)
                for hh in range((e - s) // LANES):
                    head = (s - c0) // LANES + hh
                    o_ref[head, :LANES, :] = r[:, hh * LANES:(hh + 1) * LANES].T.astype(o_ref.dtype)
                    o_ref[head, LANES:, :] = ones


def norm_proj(xa, xb, g, w, b, outs, tr, n_prompt, og=None, meta=None):
    na, d = xa.shape
    nb = 0 if xb is None else xb.shape[0]
    n = na + nb
    npb = n_prompt // tr
    two = xb is not None
    combine = og is not None
    in_specs = [pl.BlockSpec((tr, d), (lambda i: (jnp.minimum(i, npb - 1), 0)) if two else (lambda i: (i, 0)))]
    args = [xa]
    if two:
        in_specs.append(pl.BlockSpec((tr, d), lambda i: (jnp.maximum(i - npb, 0), 0)))
        args.append(xb)
    if combine:
        in_specs += [pl.BlockSpec((tr, og.shape[1]), lambda i: (i, 0)), pl.BlockSpec((tr, LANES), lambda i: (i, 0))]
        args += [og, meta]
    in_specs.append(pl.BlockSpec((1, d), lambda i: (0, 0)))
    args.append(g.reshape(1, d).astype(F32))
    in_specs.append(pl.BlockSpec(w.shape, lambda i: (0, 0)))
    args.append(w)
    if b is not None:
        in_specs.append(pl.BlockSpec((1, w.shape[1]), lambda i: (0, 0)))
        args.append(b.reshape(1, -1).astype(F32))
    out_shapes, out_specs = [], []
    if combine:
        out_shapes.append(jax.ShapeDtypeStruct((n, d), F32))
        out_specs.append(pl.BlockSpec((tr, d), lambda i: (i, 0)))
    for c0, c1, layout, dt in outs:
        assert c0 % COL_CHUNK == 0 and c1 % COL_CHUNK == 0
        if layout == "flat":
            out_shapes.append(jax.ShapeDtypeStruct((n, c1 - c0), dt))
            out_specs.append(pl.BlockSpec((tr, c1 - c0), lambda i: (i, 0)))
        elif layout == "split":
            out_shapes.append(jax.ShapeDtypeStruct((n_prompt, c1 - c0), dt))
            out_specs.append(pl.BlockSpec((tr, c1 - c0), lambda i: (jnp.minimum(i, npb - 1), 0)))
            out_shapes.append(jax.ShapeDtypeStruct((n - n_prompt, c1 - c0), dt))
            out_specs.append(pl.BlockSpec((tr, c1 - c0), lambda i: (jnp.maximum(i - npb, 0), 0)))
        elif layout == "heads":
            nh = (c1 - c0) // LANES
            out_shapes.append(jax.ShapeDtypeStruct((nh, n, LANES), dt))
            out_specs.append(pl.BlockSpec((nh, tr, LANES), lambda i: (0, i, 0)))
        else:
            assert layout == "heads_t"
            nh = (c1 - c0) // LANES
            out_shapes.append(jax.ShapeDtypeStruct((nh, LANES + ONES_ROWS, n), dt))
            out_specs.append(pl.BlockSpec((nh, LANES + ONES_ROWS, tr), lambda i: (0, 0, i)))
    kern = functools.partial(_norm_proj_kernel, npb, two, b is not None, combine,
                             [(c0, c1, lay) for c0, c1, lay, _ in outs])
    return pl.pallas_call(
        kern, grid=(n // tr,), in_specs=in_specs, out_specs=out_specs, out_shape=out_shapes,
        compiler_params=_cparams(("arbitrary",)), name="norm_proj")(*args)


def _attn_a_chunk(q, k, v, bias_ref, sink_ref, valid):
    nq = q.shape[0]
    outs = []
    for kh in range(A_KV_HEADS):
        qs = jnp.concatenate(
            [q[:, (kh * A_GROUP + g) * A_HD:(kh * A_GROUP + g + 1) * A_HD] for g in range(A_GROUP)], axis=0)
        kk = k[:, kh * A_HD:(kh + 1) * A_HD]
        s = lax.dot_general(qs, kk, (((1,), (1,)), ((), ())), preferred_element_type=F32) + bias_ref[kh]
        if valid is not None:
            s = jnp.where(valid, s, NEG_INF)
        sk = sink_ref[kh]
        m = jnp.maximum(jnp.max(s, axis=-1, keepdims=True), sk)
        e = jnp.exp2(s - m)
        den = jnp.sum(e, axis=-1, keepdims=True) + jnp.exp2(sk - m)
        p = (e * (1.0 / den)).astype(BF16)
        o = jnp.dot(p, v[:, kh * A_HD:(kh + 1) * A_HD], preferred_element_type=F32)
        for g in range(A_GROUP):
            outs.append(o[g * nq:(g + 1) * nq])
    return jnp.concatenate(outs, axis=1)


def _attn_a_prompt_kernel(qb, nsub, q_ref, kvc_ref, kvp_ref, bias_ref, sink_ref, o_ref, t_ref, p_ref):
    cb = pl.program_id(1)
    back = CHUNK * WIN_CHUNKS
    nk = back + qb
    nkv = A_KV_HEADS * A_HD
    kvfull = jnp.concatenate([kvp_ref[...], kvc_ref[...]], axis=0)
    k_all = kvfull[:, :nkv].astype(BF16)
    vt_all = kvfull[:, nkv:].T.astype(BF16)
    dn = (((1,), (1,)), ((), ()))
    units = [(s, kh) for s in range(nsub) for kh in range(A_KV_HEADS)]
    for s, kh in units:
        qs = jnp.concatenate([q_ref[s * qb:(s + 1) * qb, (kh * A_GROUP + g) * A_HD:(kh * A_GROUP + g + 1) * A_HD]
                              for g in range(A_GROUP)], axis=0)
        t_ref[s * A_KV_HEADS + kh] = lax.dot_general(k_all[s * qb:s * qb + nk, kh * A_HD:(kh + 1) * A_HD], qs, dn,
                                                     preferred_element_type=F32)
    for s in range(nsub):
        outs = []
        for kh in range(A_KV_HEADS):
            u = s * A_KV_HEADS + kh
            st = t_ref[u] + bias_ref[kh]
            if s == 0:
                key_pos = lax.broadcasted_iota(jnp.int32, (nk, 1), 0) + (cb * nsub * qb - back)
                st = st + jnp.where(key_pos >= 0, 0.0, NEG_INF)
            sk = sink_ref[kh]
            m = jnp.maximum(jnp.max(st, axis=0, keepdims=True), sk)
            e = jnp.exp2(st - m)
            inv_den = 1.0 / (jnp.sum(e, axis=0, keepdims=True) + jnp.exp2(sk - m))
            p_ref[u] = e.astype(BF16)
            ot = jnp.dot(vt_all[kh * A_HD:(kh + 1) * A_HD, s * qb:s * qb + nk], p_ref[u],
                         preferred_element_type=F32)
            o = (ot * inv_den).T
            for g in range(A_GROUP):
                outs.append(o[g * qb:(g + 1) * qb])
        o_ref[s * qb:(s + 1) * qb, :] = jnp.concatenate(outs, axis=1).astype(o_ref.dtype)


def _a_prompt_tables(qb, sinks):
    back = CHUNK * WIN_CHUNKS
    slopes = _alibi(A_HEADS).reshape(A_KV_HEADS, A_GROUP)
    qpos = np.arange(qb)
    kpos = np.arange(back + qb) - back
    dist = np.abs(kpos[:, None] - qpos[None, :]).astype(np.float64)
    band_lo = (qpos // CHUNK) * CHUNK - back
    in_band = (kpos[:, None] >= band_lo[None, :]) & (kpos[:, None] < band_lo[None, :] + back + CHUNK)
    bias = np.where(in_band[None, None], -slopes[:, :, None, None] * dist[None, None], -np.inf)
    bias = np.transpose(bias, (0, 2, 1, 3)).reshape(A_KV_HEADS, back + qb, A_GROUP * qb) * LOG2E
    sk = jnp.repeat(sinks.astype(F32).reshape(A_KV_HEADS, A_GROUP), qb, axis=1)[:, None, :] * LOG2E
    return jnp.asarray(bias, F32), sk


def _a_tables(qpos, kpos, sinks):
    slopes = _alibi(A_HEADS).reshape(A_KV_HEADS, A_GROUP)
    dist = np.abs(qpos[:, None] - kpos[None, :]).astype(np.float64)
    bias = -slopes[:, :, None, None] * dist
    nq = len(qpos)
    bias = jnp.asarray(bias.reshape(A_KV_HEADS, A_GROUP * nq, len(kpos)) * LOG2E, F32)
    sk = jnp.repeat(sinks.astype(F32).reshape(A_KV_HEADS, A_GROUP), nq, axis=1)[..., None] * LOG2E
    return bias, sk


def attn_a_prompt(q_all, kv_all, sinks, batch, t, qb):
    nq = A_HEADS * A_HD
    nkv2 = 2 * A_KV_HEADS * A_HD
    back = CHUNK * WIN_CHUNKS
    bias, sk = _a_prompt_tables(qb, sinks)
    nsub = max(s for s in (1, 2, 4, 8) if t % (s * qb) == 0)
    rows = nsub * qb
    nblk = t // rows
    r = rows // back
    return pl.pallas_call(
        functools.partial(_attn_a_prompt_kernel, qb, nsub),
        grid=(batch, nblk),
        in_specs=[
            pl.BlockSpec((rows, nq), lambda b, i: (b * nblk + i, 0)),
            pl.BlockSpec((rows, nkv2), lambda b, i: (b * nblk + i, 0)),
            pl.BlockSpec((back, nkv2), lambda b, i: (jnp.maximum((b * nblk + i) * r - 1, 0), 0)),
            pl.BlockSpec(bias.shape, lambda b, i: (0, 0, 0)),
            pl.BlockSpec(sk.shape, lambda b, i: (0, 0, 0)),
        ],
        out_specs=pl.BlockSpec((rows, nq), lambda b, i: (b * nblk + i, 0)),
        out_shape=jax.ShapeDtypeStruct((batch * t, nq), BF16),
        scratch_shapes=[pltpu.VMEM((nsub * A_KV_HEADS, back + qb, A_GROUP * qb), F32),
                        pltpu.VMEM((nsub * A_KV_HEADS, back + qb, A_GROUP * qb), BF16)],
        compiler_params=_cparams(("arbitrary", "arbitrary")), name="attn_a_prompt",
    )(q_all, kv_all, kv_all, bias, sk)


def _attn_a_sample_kernel(q_ref, k_ref, v_ref, bias_ref, sink_ref, o_ref):
    o = _attn_a_chunk(q_ref[...], k_ref[...].astype(BF16), v_ref[...].astype(BF16), bias_ref, sink_ref, None)
    o_ref[...] = o.astype(o_ref.dtype)


def attn_a_sample(q_all, kband, vband, sinks, row0, dec_batch, dec_t):
    nq = A_HEADS * A_HD
    s = kband.shape[1]
    past = s - dec_t
    kpos = np.concatenate([np.arange(past) - past, np.arange(dec_t)])
    bias, sk = _a_tables(np.arange(dec_t), kpos, sinks)
    blk0 = row0 // dec_t
    return pl.pallas_call(
        _attn_a_sample_kernel,
        grid=(dec_batch,),
        in_specs=[
            pl.BlockSpec((dec_t, nq), lambda b: (blk0 + b, 0)),
            pl.BlockSpec((None, s, kband.shape[2]), lambda b: (b, 0, 0)),
            pl.BlockSpec((None, s, vband.shape[2]), lambda b: (b, 0, 0)),
            pl.BlockSpec(bias.shape, lambda b: (0, 0, 0)),
            pl.BlockSpec(sk.shape, lambda b: (0, 0, 0)),
        ],
        out_specs=pl.BlockSpec((dec_t, nq), lambda b: (b, 0)),
        out_shape=jax.ShapeDtypeStruct((dec_batch * dec_t, nq), BF16),
        compiler_params=_cparams(("arbitrary",)), name="attn_a_sample",
    )(q_all, kband, vband, bias, sk)


B_Q_SCALE = B_HD ** -0.5 * float(np.log2(np.e))
LOG2E = float(np.log2(np.e))


def _stack_maps(qh):
    lane = lax.broadcasted_iota(jnp.int32, qh.shape, 1)
    zero = jnp.zeros_like(qh)
    return jnp.concatenate([jnp.where(lane < B_HD, qh, zero), jnp.where(lane >= B_HD, qh, zero)], axis=0)


def _diff_finish(acc, l, nq, lam, lam_init, subln):
    inv = 1.0 / l
    o = acc[:nq] * inv[:nq] - lam * (acc[nq:] * inv[nq:])
    ms = jnp.mean(o * o, axis=-1, keepdims=True)
    return o * lax.rsqrt(ms + SUBLN_EPS) * subln * (1.0 - lam_init)


def _attn_b_prompt_kernel(qb, nb, lam_init, pair_ref, slope_ref, lam_ref, q_ref, k_ref, vt_ref, kbias_ref,
                          subln_ref, o_ref, acc_ref, m_ref, al_ref, qq_ref, t0_ref, t1_ref, p0_ref, p1_ref):
    g = pl.program_id(1)
    lam = lam_ref[0]
    dn = (((1,), (1,)), ((), ()))
    t_refs = (t0_ref, t1_ref)
    p_refs = (p0_ref, p1_ref)
    heads = range(HEAD_GROUP)
    n_pairs = nb * (nb + 1) // 2
    slopes = [slope_ref[g * HEAD_GROUP + u] for u in heads]

    def pair(n):
        return pair_ref[2 * n], pair_ref[2 * n + 1]

    def scores(n, slot):
        j, iq = pair(n)
        ks = pl.multiple_of(j * qb, qb)
        for u in heads:
            t_refs[slot][u] = lax.dot_general(k_ref[u, pl.ds(ks, qb), :], qq_ref[u * nb + iq], dn,
                                              preferred_element_type=F32)

    def softmax_step(n, slot):
        j, iq = pair(n)
        on_diag = (j == iq).astype(jnp.int32)
        for u in heads:
            st = u * nb + iq
            m = m_ref[st, 0:1, :]
            t = t_refs[slot][u] + kbias_ref[u, on_diag]
            off = slopes[u] * jnp.full((1, 2 * qb), (j - iq) * qb, jnp.int32).astype(F32)
            m_new = jnp.maximum(m, jnp.max(t, axis=0, keepdims=True) + off)
            p_refs[slot][u] = jnp.exp2(t + (off - m_new)).astype(BF16)
            al_ref[slot * HEAD_GROUP + u, 0:1, :] = jnp.exp2(m - m_new)
            m_ref[st, 0:1, :] = m_new

    def accumulate(n, slot):
        j, iq = pair(jnp.maximum(n, 0))
        ks = pl.multiple_of(j * qb, qb)
        for u in heads:
            st = u * nb + iq
            pv = jnp.dot(vt_ref[u, :, pl.ds(ks, qb)], p_refs[slot][u], preferred_element_type=F32)
            acc_ref[st] = al_ref[slot * HEAD_GROUP + u, 0:1, :] * acc_ref[st] + pv

    def step(n, slot):
        softmax_step(n, slot)
        accumulate(n - 1, 1 - slot)
        scores(jnp.minimum(n + 1, n_pairs - 1), 1 - slot)
        return 0

    acc_ref[...] = jnp.zeros_like(acc_ref)
    m_ref[...] = jnp.full(m_ref.shape, NEG_INF, F32)
    al_ref[...] = jnp.ones_like(al_ref)
    p1_ref[...] = jnp.zeros_like(p1_ref)
    for u in heads:
        for iq in range(nb):
            qq_ref[u * nb + iq] = _stack_maps(q_ref[u, iq * qb:(iq + 1) * qb, :])
    scores(0, 0)
    def steps(k, c):
        for r in range(PAIR_UNROLL):
            step(PAIR_UNROLL * k + r, r % 2)
        return c

    lax.fori_loop(0, n_pairs // PAIR_UNROLL, steps, 0)
    for n in range(n_pairs - n_pairs % PAIR_UNROLL, n_pairs):
        step(n, n % 2)
    accumulate(n_pairs - 1, (n_pairs - 1) % 2)
    for u in heads:
        for iq in range(nb):
            inv = 1.0 / acc_ref[u * nb + iq, LANES:LANES + 1, :]
            acc = acc_ref[u * nb + iq, :LANES, :]
            ot = acc[:, :qb] * inv[:, :qb] - lam * (acc[:, qb:] * inv[:, qb:])
            ms = jnp.mean(ot * ot, axis=0, keepdims=True)
            ot = ot * lax.rsqrt(ms + SUBLN_EPS) * subln_ref[...] * (1.0 - lam_init)
            o_ref[u, iq * qb:(iq + 1) * qb, :] = ot.T.astype(o_ref.dtype)


def attn_b_prompt(qkvh, vt, lam, subln, lam_init, batch, t, qb):
    nblk = t // qb
    qq = np.arange(qb)
    slope2 = (_alibi(B_HEADS) * LOG2E)[:, None, None]
    kcol = np.broadcast_to(qq[:, None], (qb, qb)).astype(np.float64)
    tri = -2.0 * np.maximum(qq[:, None] - qq[None, :], 0)
    msk = np.where((qq[:, None] // CHUNK) <= (qq[None, :] // CHUNK), 0.0, -np.inf)
    both = lambda a: np.concatenate([a, a], axis=-1)
    kbias = np.stack([both(slope2 * kcol[None]), both(slope2 * (kcol + tri)[None] + msk[None])], axis=1)
    kbias = jnp.asarray(kbias, F32)
    slopes = jnp.asarray(_alibi(B_HEADS) * LOG2E, F32)
    subln_t = jnp.broadcast_to(subln.astype(F32).reshape(LANES, 1), (LANES, qb))
    pairs = jnp.asarray([v for j in range(nblk) for i in range(j, nblk) for v in (j, i)], jnp.int32)
    hg = HEAD_GROUP
    n_groups = B_HEADS // hg
    grid_spec = pltpu.PrefetchScalarGridSpec(
        num_scalar_prefetch=3, grid=(batch, n_groups),
        in_specs=[
            pl.BlockSpec((hg, t, LANES), lambda b, g, *_: (g, b, 0)),
            pl.BlockSpec((hg, t, LANES), lambda b, g, *_: (n_groups + g, b, 0)),
            pl.BlockSpec((hg, LANES + ONES_ROWS, t), lambda b, g, *_: (g, 0, b)),
            pl.BlockSpec((hg,) + kbias.shape[1:], lambda b, g, *_: (g, 0, 0, 0)),
            pl.BlockSpec(subln_t.shape, lambda b, g, *_: (0, 0)),
        ],
        out_specs=pl.BlockSpec((hg, t, LANES), lambda b, g, *_: (g, b, 0)),
        scratch_shapes=[pltpu.VMEM((hg * nblk, LANES + ONES_ROWS, 2 * qb), F32),
                        pltpu.VMEM((hg * nblk, SUBLANES, 2 * qb), F32),
                        pltpu.VMEM((2 * hg, SUBLANES, 2 * qb), F32),
                        pltpu.VMEM((hg * nblk, 2 * qb, LANES), BF16),
                        pltpu.VMEM((hg, qb, 2 * qb), F32), pltpu.VMEM((hg, qb, 2 * qb), F32),
                        pltpu.VMEM((hg, qb, 2 * qb), BF16), pltpu.VMEM((hg, qb, 2 * qb), BF16)],
    )
    return pl.pallas_call(
        functools.partial(_attn_b_prompt_kernel, qb, nblk, lam_init), grid_spec=grid_spec,
        out_shape=jax.ShapeDtypeStruct((B_HEADS, batch * t, LANES), BF16),
        compiler_params=_cparams(("arbitrary", "arbitrary")), name="attn_b_prompt",
    )(pairs, slopes, lam.reshape(1).astype(F32), qkvh, qkvh, vt, kbias, subln_t)


def _attn_b_sample_kernel(dec_t, past, lam_init, slopes, lam_ref, q_ref, kn_ref, vn_ref, ck_ref, cv_ref,
                          tnew_ref, subln_ref, o_ref):
    lam = lam_ref[0]
    subln = subln_ref[...]
    colc = lax.broadcasted_iota(jnp.int32, (1, past), 1).astype(F32) - float(past)
    for h in range(B_HEADS):
        slope = float(slopes[h]) * LOG2E
        qq = _stack_maps(q_ref[h])
        kc = ck_ref[pl.ds(h, past, stride=B_HEADS), :].astype(BF16)
        vc = cv_ref[pl.ds(h, past, stride=B_HEADS), :].astype(BF16)
        dn = (((1,), (1,)), ((), ()))
        s_c = lax.dot_general(qq, kc, dn, preferred_element_type=F32) + slope * colc
        s_n = lax.dot_general(qq, kn_ref[h], dn, preferred_element_type=F32) + slope * tnew_ref[...]
        m = jnp.maximum(jnp.max(s_c, axis=-1, keepdims=True), jnp.max(s_n, axis=-1, keepdims=True))
        p_c = jnp.exp2(s_c - m)
        p_n = jnp.exp2(s_n - m)
        l = jnp.sum(p_c, axis=-1, keepdims=True) + jnp.sum(p_n, axis=-1, keepdims=True)
        acc = (jnp.dot(p_c.astype(BF16), vc, preferred_element_type=F32)
               + jnp.dot(p_n.astype(BF16), vn_ref[h], preferred_element_type=F32))
        o_ref[h] = _diff_finish(acc, l, dec_t, lam, lam_init, subln).astype(o_ref.dtype)


def attn_b_sample(qkvh, cache_k, cache_v, lam, subln, lam_init, row0, dec_batch, dec_t):
    past = cache_k.shape[1]
    qi = np.arange(dec_t)
    tnew = (qi[:, None] - np.abs(qi[:, None] - qi[None, :])).astype(np.float32)
    tnew = jnp.asarray(np.concatenate([tnew, tnew], axis=0))
    blk0 = row0 // dec_t
    rows = past * B_HEADS
    cache_k = cache_k.reshape(dec_batch, rows, LANES)
    cache_v = cache_v.reshape(dec_batch, rows, LANES)
    grid_spec = pltpu.PrefetchScalarGridSpec(
        num_scalar_prefetch=1, grid=(dec_batch,),
        in_specs=[
            pl.BlockSpec((B_HEADS, dec_t, LANES), lambda b, *_: (0, blk0 + b, 0)),
            pl.BlockSpec((B_HEADS, dec_t, LANES), lambda b, *_: (1, blk0 + b, 0)),
            pl.BlockSpec((B_HEADS, dec_t, LANES), lambda b, *_: (2, blk0 + b, 0)),
            pl.BlockSpec((None, rows, LANES), lambda b, *_: (b, 0, 0)),
            pl.BlockSpec((None, rows, LANES), lambda b, *_: (b, 0, 0)),
            pl.BlockSpec(tnew.shape, lambda b, *_: (0, 0)),
            pl.BlockSpec((1, LANES), lambda b, *_: (0, 0)),
        ],
        out_specs=pl.BlockSpec((B_HEADS, dec_t, LANES), lambda b, *_: (0, b, 0)),
    )
    return pl.pallas_call(
        functools.partial(_attn_b_sample_kernel, dec_t, past, lam_init, tuple(_alibi(B_HEADS))),
        grid_spec=grid_spec,
        out_shape=jax.ShapeDtypeStruct((B_HEADS, dec_batch * dec_t, LANES), BF16),
        compiler_params=_cparams(("arbitrary",)), name="attn_b_sample",
    )(lam.reshape(1).astype(F32), qkvh, qkvh, qkvh, cache_k, cache_v, tnew, subln.reshape(1, LANES).astype(F32))


ROUTE_ROWS = 40
ROUTE_FIELDS = 8


def _route_t(lt):
    row = lax.broadcasted_iota(jnp.int32, lt.shape, 0).astype(F32)
    big = float(LANES)
    lg = jnp.where(row < N_GROUPS, lt, NEG_INF)
    gmax = jnp.max(lg, axis=0, keepdims=True)
    g_idx = jnp.min(jnp.where(lg == gmax, row, big), axis=0, keepdims=True)
    g_prob = 1.0 / jnp.sum(jnp.exp(lg - gmax), axis=0, keepdims=True)
    lo = N_GROUPS + EXPERTS_PER_GROUP * g_idx
    le = jnp.where((row >= lo) & (row < lo + EXPERTS_PER_GROUP), lt, NEG_INF)
    m1 = jnp.max(le, axis=0, keepdims=True)
    i1 = jnp.min(jnp.where(le == m1, row, big), axis=0, keepdims=True)
    le2 = jnp.where(row == i1, NEG_INF, le)
    m2 = jnp.max(le2, axis=0, keepdims=True)
    i2 = jnp.min(jnp.where(le2 == m2, row, big), axis=0, keepdims=True)
    t = jnp.exp(m2 - m1)
    ga = 1.0 / (1.0 + t)
    gb = t * ga
    return row, i1, i2, g_prob * ga, g_prob * gb


def _unpack_pairs(u):
    lo = pltpu.bitcast(u << 16, F32)
    hi = pltpu.bitcast(u & jnp.uint32(0xFFFF0000), F32)
    return jnp.concatenate([lo, hi], axis=1)


def _pack_pairs(x):
    bits = pltpu.bitcast(x.astype(BF16).astype(F32), jnp.uint32)
    w = x.shape[1] // 2
    return (bits[:, :w] >> 16) | bits[:, w:]


def _out_router_kernel(n_prompt_blocks, heads_in, two_o, two_res, *refs):
    refs = list(refs)
    oa_ref = refs.pop(0)
    ob_ref = refs.pop(0) if two_o else None
    ra_ref = refs.pop(0)
    rb_ref = refs.pop(0) if two_res else None
    (wo_ref, g_ref, wr_ref, wr2_ref, br_ref, utri_ref, h_ref, xn_ref, meta_ref, route_ref, cnt_ref,
     carry_ref) = refs
    is_prompt = pl.program_id(0) < n_prompt_blocks

    def load_o(ref):
        if heads_in:
            return jnp.concatenate([ref[hh] for hh in range(ref.shape[0])], axis=1)
        return ref[...]

    o = load_o(oa_ref)
    if two_o:
        o = jnp.where(is_prompt, o, load_o(ob_ref))
    res = ra_ref[...]
    if two_res:
        res = jnp.where(is_prompt, res, rb_ref[...])
    h = res + jnp.dot(o, wo_ref[...], preferred_element_type=F32)
    h_ref[...] = h
    ms = jnp.mean(h * h, axis=-1, keepdims=True)
    xn = h * lax.rsqrt(ms + NORM_EPS) * g_ref[...]
    x_hi = xn.astype(BF16)
    _store_planes(xn_ref, _pack_pairs(xn))
    x_lo = (xn - x_hi.astype(F32)).astype(BF16)
    r = jnp.dot(x_hi, wr_ref[...], preferred_element_type=F32)
    logits = (r[:, :LANES] + r[:, LANES:]) + jnp.dot(x_lo, wr2_ref[...], preferred_element_type=F32) + br_ref[...]
    row, i1, i2, g1, g2 = _route_t(logits.T[:ROUTE_ROWS])

    @pl.when(pl.program_id(0) == 0)
    def _():
        carry_ref[...] = jnp.zeros_like(carry_ref)

    sel1 = row == i1
    sel2 = row == i2
    onehot = jnp.where(sel1 | sel2, 1.0, 0.0)
    before = jnp.dot(onehot.astype(BF16), utri_ref[...], preferred_element_type=F32) + carry_ref[:, 0:1]
    rank1 = jnp.sum(jnp.where(sel1, before, 0.0), axis=0, keepdims=True)
    rank2 = jnp.sum(jnp.where(sel2, before, 0.0), axis=0, keepdims=True)
    carry_ref[...] = carry_ref[...] + jnp.sum(onehot, axis=1, keepdims=True)
    cnt_ref[...] = carry_ref[...]
    fields = [i1 - N_GROUPS, i2 - N_GROUPS, g1, g2, rank1, rank2]
    fields += [jnp.zeros_like(g1)] * (ROUTE_FIELDS - len(fields))
    route = jnp.concatenate(fields, axis=0)
    route_ref[...] = route
    meta_ref[...] = jnp.concatenate([route, jnp.zeros((LANES - ROUTE_FIELDS, route.shape[1]), F32)], axis=0).T


def out_router(oa, ob, ra, rb, wo, g, w_rg, b_rg, w_re, b_re, heads_in, tr):
    d = ra.shape[1]
    if heads_in:
        na = oa.shape[1]
        nb = 0 if ob is None else ob.shape[1]
    else:
        na = oa.shape[0]
        nb = 0 if ob is None else ob.shape[0]
    n = na + nb
    npb = na // tr
    two_o = ob is not None
    two_res = rb is not None
    first = lambda i: (jnp.minimum(i, npb - 1), 0)
    second = lambda i: (jnp.maximum(i - npb, 0), 0)
    plain = lambda i: (i, 0)
    in_specs, args = [], []

    def add_o(x, imap):
        if heads_in:
            in_specs.append(pl.BlockSpec((x.shape[0], tr, LANES), lambda i: (0, imap(i)[0], 0)))
        else:
            in_specs.append(pl.BlockSpec((tr, x.shape[1]), imap))
        args.append(x)

    add_o(oa, first if two_o else plain)
    if two_o:
        add_o(ob, second)
    in_specs.append(pl.BlockSpec((tr, d), first if two_res else plain))
    args.append(ra)
    if two_res:
        in_specs.append(pl.BlockSpec((tr, d), second))
        args.append(rb)
    wr = jnp.zeros((d, LANES), F32).at[:, :N_GROUPS].set(w_rg.astype(F32))
    wr = wr.at[:, N_GROUPS:N_GROUPS + N_EXPERTS].set(w_re.astype(F32))
    wr_hi = wr.astype(BF16)
    wr_lo = (wr - wr_hi.astype(F32)).astype(BF16)
    br = jnp.zeros((1, LANES), F32).at[0, :N_GROUPS].set(b_rg.astype(F32))
    br = br.at[0, N_GROUPS:N_GROUPS + N_EXPERTS].set(b_re.astype(F32))
    utri = jnp.asarray(np.triu(np.ones((tr, tr), np.float32), 1), BF16)
    consts = [wo, g.reshape(1, d).astype(F32), jnp.concatenate([wr_hi, wr_lo], axis=1), wr_hi, br, utri]
    for cst in consts:
        in_specs.append(pl.BlockSpec(cst.shape, lambda i: (0, 0)))
        args.append(cst)
    parts = d // 2 // SC_ROW_WORDS
    out_shape = [jax.ShapeDtypeStruct((n, d), F32), jax.ShapeDtypeStruct((parts, n, SC_ROW_WORDS), jnp.uint32),
                 jax.ShapeDtypeStruct((n, LANES), F32), jax.ShapeDtypeStruct((ROUTE_FIELDS, n), F32),
                 jax.ShapeDtypeStruct((ROUTE_ROWS, LANES), F32)]
    out_specs = [pl.BlockSpec((tr, d), plain), pl.BlockSpec((parts, tr, SC_ROW_WORDS), lambda i: (0, i, 0)),
                 pl.BlockSpec((tr, LANES), plain), pl.BlockSpec((ROUTE_FIELDS, tr), lambda i: (0, i)),
                 pl.BlockSpec((ROUTE_ROWS, LANES), lambda i: (0, 0))]
    return pl.pallas_call(
        functools.partial(_out_router_kernel, npb, heads_in, two_o, two_res),
        grid=(n // tr,), in_specs=in_specs, out_specs=out_specs, out_shape=out_shape,
        scratch_shapes=[pltpu.VMEM((ROUTE_ROWS, LANES), F32)],
        compiler_params=_cparams(("arbitrary",)), name="out_router")(*args)


def _expert_kernel(tm, tps, te_ref, nt_ref, x_ref, *refs):
    w_refs, o_ref = refs[:3 * tps], refs[3 * tps]
    first = pl.program_id(0) * tps
    n_valid = jnp.clip(nt_ref[0] - first, 0, tps)

    def tile(k):
        wg_ref, wu_ref, wd_ref = w_refs[3 * k:3 * k + 3]
        rows = slice(k * tm, (k + 1) * tm)
        x = _unpack_pairs(jnp.concatenate([x_ref[q, rows, :] for q in range(x_ref.shape[0])], axis=1)).astype(BF16)
        a = jnp.dot(x, wg_ref[...].astype(BF16), preferred_element_type=F32)
        u = jnp.dot(x, wu_ref[...].astype(BF16), preferred_element_type=F32)
        hh = (a * (1.0 / (1.0 + jnp.exp(-a))) * u).astype(BF16)
        packed = _pack_pairs(jnp.dot(hh, wd_ref[...].astype(BF16), preferred_element_type=F32))
        for q in range(o_ref.shape[0]):
            o_ref[q, rows, :] = packed[:, q * o_ref.shape[2]:(q + 1) * o_ref.shape[2]]

    for nv in range(tps + 1):

        @pl.when(n_valid == nv)
        def _(nv=nv):
            for k in range(nv):
                tile(k)
            if nv < tps:
                o_ref[:, nv * tm:, :] = jnp.zeros((o_ref.shape[0], (tps - nv) * tm, o_ref.shape[2]), o_ref.dtype)


def expert_mlp(xs, tile_expert, n_tiles, w_gate, w_up, w_down, layer, tm):
    parts, p, w = xs.shape
    d, f = w_gate.shape[-2:]
    epg = w_gate.shape[2]
    tps = 2 if (p // tm) % 2 == 0 else 1
    w_specs, w_args = [], []
    for k in range(tps):
        wmap = lambda t, te, nt, k=k: (layer, te[tps * t + k] // epg, te[tps * t + k] % epg, 0, 0)
        w_specs += [pl.BlockSpec((None, None, None, d, f), wmap), pl.BlockSpec((None, None, None, d, f), wmap),
                    pl.BlockSpec((None, None, None, f, d), wmap)]
        w_args += [w_gate, w_up, w_down]
    grid_spec = pltpu.PrefetchScalarGridSpec(
        num_scalar_prefetch=2, grid=(p // (tps * tm),),
        in_specs=[pl.BlockSpec((parts, tps * tm, w), lambda t, te, nt: (0, t, 0))] + w_specs,
        out_specs=pl.BlockSpec((parts, tps * tm, w), lambda t, te, nt: (0, t, 0)),
    )
    return pl.pallas_call(
        functools.partial(_expert_kernel, tm, tps), grid_spec=grid_spec,
        out_shape=jax.ShapeDtypeStruct((parts, p, w), jnp.uint32),
        compiler_params=_cparams(("arbitrary",)), name="expert_mlp")(tile_expert, n_tiles, xs, *w_args)


def sc_scatter_rows(x2, idx, n_out):
    mesh = plsc.VectorSubcoreMesh(core_axis_name="c", subcore_axis_name="s")
    n_src_win = x2.shape[0] // SC_WINDOW

    @functools.partial(pl.kernel, out_type=jax.ShapeDtypeStruct((n_out, SC_ROW_WORDS), x2.dtype), mesh=mesh)
    def scatter(x_hbm, i_hbm, o_hbm):
        def body(x_vmem, i_vmem):
            pltpu.sync_copy(x_vmem, o_hbm.at[i_vmem.at[0]])

        pltpu.emit_pipeline(
            body, grid=(n_out // SC_WINDOW,),
            in_specs=[pl.BlockSpec((SC_WINDOW, SC_ROW_WORDS), lambda i: (lax.rem(i, n_src_win), 0)),
                      pl.BlockSpec((1, SC_WINDOW), lambda i: (0, i))],
            out_specs=[], core_axis_name=("c", "s"), dimension_semantics=(pltpu.PARALLEL,),
        )(x_hbm, i_hbm)

    return scatter(x2, idx)


def sc_gather_rows(x2, idx):
    mesh = plsc.VectorSubcoreMesh(core_axis_name="c", subcore_axis_name="s")
    n_col, n = idx.shape

    @functools.partial(pl.kernel, out_type=jax.ShapeDtypeStruct((n, n_col * SC_ROW_WORDS), x2.dtype), mesh=mesh)
    def gather(x_hbm, i_hbm, o_hbm):
        def body(i_vmem, o_vmem):
            pltpu.sync_copy(x_hbm.at[i_vmem.at[0]], o_vmem)

        pltpu.emit_pipeline(
            body, grid=(n // SC_WINDOW, n_col),
            in_specs=[pl.BlockSpec((1, SC_WINDOW), lambda i, c: (c, i))],
            out_specs=[pl.BlockSpec((SC_WINDOW, SC_ROW_WORDS), lambda i, c: (i, c))],
            core_axis_name=("c", "s"), dimension_semantics=(pltpu.PARALLEL, pltpu.ARBITRARY),
        )(i_hbm, o_hbm)

    return gather(x2, idx)


def moe_dispatch(xn_packed, route, counts, tm):
    parts, n, w = xn_packed.shape
    e = route[0:2].astype(jnp.int32)
    rank = route[4:6].astype(jnp.int32)
    counts = counts[N_GROUPS:N_GROUPS + N_EXPERTS, 0].astype(jnp.int32)
    padded = ((counts + tm - 1) // tm) * tm
    pend = jnp.cumsum(padded)
    pstart = pend - padded
    ids = jnp.arange(N_EXPERTS, dtype=jnp.int32)
    pos = jnp.sum(jnp.where(e[..., None] == ids, pstart, 0), axis=-1) + rank
    p = 2 * n + N_EXPERTS * tm
    n_pad = p - 2 * n
    gap_len = jnp.concatenate([padded - counts, (p - pend[-1]).reshape(1)])
    gap_first = jnp.concatenate([pstart + counts, pend[-1:]])
    gap_end = jnp.cumsum(gap_len)
    k = jnp.arange(n_pad, dtype=jnp.int32)
    gap = jnp.sum((k[:, None] >= gap_end[None, :]).astype(jnp.int32), axis=1)
    shift = gap_first - (gap_end - gap_len)
    pad_pos = k + jnp.sum(jnp.where(gap[:, None] == jnp.arange(N_EXPERTS + 1), shift, 0), axis=-1)
    planes = lambda rows: jnp.concatenate([rows + q * p for q in range(parts)])
    scat_idx = jnp.concatenate([planes(pos[0]), planes(pos[1]), planes(pad_pos)]).reshape(1, parts * p)
    gath_idx = jnp.stack([pos[k] + q * p for k in range(2) for q in range(parts)])
    xs = sc_scatter_rows(xn_packed.reshape(parts * n, w), scat_idx, parts * p).reshape(parts, p, w)
    tile_start = jnp.arange(p // tm, dtype=jnp.int32) * tm
    tile_expert = jnp.sum((tile_start[:, None] >= pend[None, :]).astype(jnp.int32), axis=1)
    tile_expert = jnp.minimum(tile_expert, N_EXPERTS - 1)
    n_tiles = (pend[-1] // tm).astype(jnp.int32).reshape(1)
    return xs, tile_expert, n_tiles, gath_idx


def moe_experts(xn_packed, route, counts, w_gate, w_up, w_down, layer, tm):
    parts, n, w = xn_packed.shape
    xs, tile_expert, n_tiles, gath_idx = moe_dispatch(xn_packed, route, counts, tm)
    o = expert_mlp(xs, tile_expert, n_tiles, w_gate, w_up, w_down, layer, tm)
    return sc_gather_rows(o.reshape(parts * o.shape[1], w), gath_idx)


def _final_norm_kernel(n_prompt_blocks, h_ref, o0_ref, o1_ref, meta_ref, g_ref, yp_ref, ys_ref):
    meta = meta_ref[...]
    x = (h_ref[...] + meta[:, 2:3] * _unpack_pairs(o0_ref[...])) + meta[:, 3:4] * _unpack_pairs(o1_ref[...])
    ms = jnp.mean(x * x, axis=-1, keepdims=True)
    y = x * lax.rsqrt(ms + NORM_EPS) * g_ref[...]
    i = pl.program_id(0)

    @pl.when(i < n_prompt_blocks)
    def _():
        yp_ref[...] = y

    @pl.when(i >= n_prompt_blocks)
    def _():
        ys_ref[...] = y


def final_norm(h, og, meta, g, n_prompt, tr):
    n, d = h.shape
    npb = n_prompt // tr
    return pl.pallas_call(
        functools.partial(_final_norm_kernel, npb),
        grid=(n // tr,),
        in_specs=[pl.BlockSpec((tr, d), lambda i: (i, 0)),
                  pl.BlockSpec((tr, og.shape[1] // 2), lambda i: (i, 0)),
                  pl.BlockSpec((tr, og.shape[1] // 2), lambda i: (i, 1)),
                  pl.BlockSpec((tr, LANES), lambda i: (i, 0)), pl.BlockSpec((1, d), lambda i: (0, 0))],
        out_specs=[pl.BlockSpec((tr, d), lambda i: (jnp.minimum(i, npb - 1), 0)),
                   pl.BlockSpec((tr, d), lambda i: (jnp.maximum(i - npb, 0), 0))],
        out_shape=[jax.ShapeDtypeStruct((n_prompt, d), F32), jax.ShapeDtypeStruct((n - n_prompt, d), F32)],
        compiler_params=_cparams(("arbitrary",)), name="final_norm")(h, og, og, meta, g.reshape(1, d).astype(F32))


def _pick_rows_tile(n_prompt, n_sample):
    tr = ROW_TILE
    while tr >= SUBLANES:
        if n_prompt % tr == 0 and n_sample % tr == 0:
            return tr
        tr //= 2
    raise ValueError("row counts must be multiples of the sublane count")


def kernel(x_prompt, x_sample, cache_a_k, cache_a_v, cache_b_k, cache_b_v, norm_mix, norm_ffn, norm_final,
           w_a_qkv, b_a_qkv, a_sinks, w_a_o, w_b_qkv, b_lambda, b_subln, w_b_o,
           w_route_group, b_route_group, w_route_expert, b_route_expert, w_gate, w_up, w_down):
    batch, t, d = x_prompt.shape
    dec_batch, dec_t, _ = x_sample.shape
    n_p, n_s = batch * t, dec_batch * dec_t
    tr = _pick_rows_tile(n_p, n_s)
    tm = min(EXPERT_TILE, 4 * tr)
    xp = x_prompt.reshape(n_p, d)
    xs = x_sample.reshape(n_s, d)
    nq_a = A_HEADS * A_HD
    nkv_a = A_KV_HEADS * A_HD

    a_scale = jnp.where(jnp.arange(nq_a + 2 * nkv_a) < nq_a, A_HD ** -0.5 * LOG2E, 1.0).astype(F32)
    q_a, kv_ap, kv_as = norm_proj(xp, xs, norm_mix[0], (w_a_qkv[0] * a_scale).astype(BF16), b_a_qkv[0] * a_scale,
                                  [(0, nq_a, "flat", BF16), (nq_a, nq_a + 2 * nkv_a, "split", F32)], tr, n_p)
    qb_a = min(CHUNK * WIN_CHUNKS, t)
    o_ap = attn_a_prompt(q_a, kv_ap, a_sinks[0], batch, t, qb_a)
    kv_s = kv_as.reshape(dec_batch, dec_t, 2 * nkv_a)
    past_a = cache_a_k.shape[2]
    kband = jnp.concatenate([cache_a_k[0].reshape(dec_batch, past_a, nkv_a), kv_s[..., :nkv_a]], axis=1)
    vband = jnp.concatenate([cache_a_v[0].reshape(dec_batch, past_a, nkv_a), kv_s[..., nkv_a:]], axis=1)
    o_as = attn_a_sample(q_a, kband, vband, a_sinks[0], n_p, dec_batch, dec_t)
    h, xn, meta, route, counts = out_router(o_ap, o_as, xp, xs, w_a_o[0].astype(BF16), norm_ffn[0], w_route_group[0],
                                            b_route_group[0], w_route_expert[0], b_route_expert[0], False, tr)
    og = moe_experts(xn, route, counts, w_gate, w_up, w_down, 0, tm)

    keep = min(CHUNK * WIN_CHUNKS, t)
    kv_p = kv_ap.reshape(batch, t, 2 * nkv_a)[:, t - keep:]
    a_k_prompt = kv_p[..., :nkv_a].reshape(1, batch, keep, A_KV_HEADS, A_HD)
    a_v_prompt = kv_p[..., nkv_a:].reshape(1, batch, keep, A_KV_HEADS, A_HD)
    a_k_sample = kband[:, dec_t:].reshape(1, dec_batch, past_a, A_KV_HEADS, A_HD)
    a_v_sample = vband[:, dec_t:].reshape(1, dec_batch, past_a, A_KV_HEADS, A_HD)

    lam_init = 0.8 - 0.6 * float(np.exp(-0.3 * 1))
    lp = b_lambda[0].astype(F32)
    lam = jnp.exp(jnp.sum(lp[0] * lp[1])) - jnp.exp(jnp.sum(lp[2] * lp[3])) + lam_init
    wb = B_HEADS * 2 * B_HD
    col_scale = jnp.where(jnp.arange(3 * wb) < wb, B_Q_SCALE, 1.0).astype(F32)
    h, qkvh, k_bp, k_bs, v_bp, v_bs, vt = norm_proj(
        h, None, norm_mix[1], (w_b_qkv[0] * col_scale).astype(BF16), None,
        [(0, 3 * wb, "heads", BF16), (wb, 2 * wb, "split", F32), (2 * wb, 3 * wb, "split", F32),
         (2 * wb, 3 * wb, "heads_t", BF16)], tr, n_p, og=og, meta=meta)
    qb_b = min(ATTN_BLOCK, t)
    o_bp = attn_b_prompt(qkvh, vt, lam, b_subln[0], lam_init, batch, t, qb_b)
    o_bs = attn_b_sample(qkvh, cache_b_k[0], cache_b_v[0], lam, b_subln[0], lam_init, n_p, dec_batch, dec_t)
    h, xn, meta, route, counts = out_router(o_bp, o_bs, h, None, w_b_o[0].astype(BF16), norm_ffn[1], w_route_group[1],
                                            b_route_group[1], w_route_expert[1], b_route_expert[1], True, tr)
    og = moe_experts(xn, route, counts, w_gate, w_up, w_down, 1, tm)

    y_p, y_s = final_norm(h, og, meta, norm_final, n_p, tr)
    hd2 = 2 * B_HD
    return (y_p.reshape(batch, t, d), y_s.reshape(dec_batch, dec_t, d),
            a_k_prompt, a_v_prompt, a_k_sample, a_v_sample,
            k_bp.reshape(1, batch, t, B_HEADS, hd2), v_bp.reshape(1, batch, t, B_HEADS, hd2),
            k_bs.reshape(1, dec_batch, dec_t, B_HEADS, hd2), v_bs.reshape(1, dec_batch, dec_t, B_HEADS, hd2))
```

```python
import functools

import jax
import jax.numpy as jnp
import numpy as np
from jax import lax
from jax.experimental import pallas as pl
from jax.experimental.pallas import tpu as pltpu
from jax.experimental.pallas import tpu_sc as plsc

F32 = jnp.float32
BF16 = jnp.bfloat16
NEG_INF = float("-inf")

CHUNK = 64
WIN_CHUNKS = 2
A_HEADS, A_KV_HEADS, A_GROUP, A_HD = 16, 4, 4, 64
B_HEADS, B_HD = 8, 64
N_GROUPS, EXPERTS_PER_GROUP = 4, 8
N_EXPERTS = N_GROUPS * EXPERTS_PER_GROUP
NORM_EPS = 1e-6
SUBLN_EPS = 1e-5

LANES = 128
SUBLANES = 8
ROW_TILE = 512
EXPERT_TILE = 512
ATTN_BLOCK = 256
COL_CHUNK = 512
PAIR_UNROLL = 12
HEAD_GROUP = 2
ONES_ROWS = 16
SC_WINDOW = 128
SC_ROW_WORDS = 256
VMEM_LIMIT = 48 * 1024 * 1024


def _cparams(sem, flags=None):
    return pltpu.CompilerParams(dimension_semantics=sem, vmem_limit_bytes=VMEM_LIMIT, flags=flags)


def _alibi(n):
    return 2.0 ** (-8.0 * np.arange(1, n + 1) / n)


def _store_planes(ref, x):
    w = ref.shape[2]
    for p in range(ref.shape[0]):
        ref[p] = x[:, p * w:(p + 1) * w]


def _add_expert_outputs(x, og_ref, meta_ref):
    half = og_ref.shape[1] // 2
    meta = meta_ref[...]
    return (x + meta[:, 2:3] * _unpack_pairs(og_ref[:, :half])) + meta[:, 3:4] * _unpack_pairs(og_ref[:, half:])


def _norm_proj_kernel(n_prompt_blocks, two_src, has_bias, combine, outs, *refs):
    refs = list(refs)
    xa_ref = refs.pop(0)
    xb_ref = refs.pop(0) if two_src else None
    og_ref, meta_ref = (refs.pop(0), refs.pop(0)) if combine else (None, None)
    g_ref = refs.pop(0)
    w_ref = refs.pop(0)
    b_ref = refs.pop(0) if has_bias else None
    out_refs = refs
    x = xa_ref[...]
    if two_src:
        x = jnp.where(pl.program_id(0) < n_prompt_blocks, x, xb_ref[...])
    if combine:
        x = _add_expert_outputs(x, og_ref, meta_ref)
        out_refs.pop(0)[...] = x
    ms = jnp.mean(x * x, axis=-1, keepdims=True)
    xn = (x * lax.rsqrt(ms + NORM_EPS) * g_ref[...]).astype(BF16)
    is_prompt = pl.program_id(0) < n_prompt_blocks
    out_refs = list(out_refs)
    targets = []
    for c0, c1, layout in outs:
        targets.append((c0, c1, layout, out_refs.pop(0), out_refs.pop(0) if layout == "split" else None))
    for s in range(0, w_ref.shape[1], COL_CHUNK):
        e = s + COL_CHUNK
        r = jnp.dot(xn, w_ref[:, s:e], preferred_element_type=F32)
        if has_bias:
            r = r + b_ref[:, s:e]
        for c0, c1, layout, o_ref, o2_ref in targets:
            if not (c0 <= s and e <= c1):
                continue
            rr = r.astype(o_ref.dtype)
            if layout == "flat":
                o_ref[:, s - c0:e - c0] = rr
            elif layout == "split":

                @pl.when(is_prompt)
                def _(rr=rr, o_ref=o_ref, s=s, e=e, c0=c0):
                    o_ref[:, s - c0:e - c0] = rr

                @pl.when(jnp.logical_not(is_prompt))
                def _(rr=rr, o2_ref=o2_ref, s=s, e=e, c0=c0):
                    o2_ref[:, s - c0:e - c0] = rr
            elif layout == "heads":
                for hh in range((e - s) // LANES):
                    o_ref[(s - c0) // LANES + hh] = rr[:, hh * LANES:(hh + 1) * LANES]
            else:
                ones = jnp.ones((ONES_ROWS, r.shape[0]), o_ref.dtype)
                for hh in range((e - s) // LANES):
                    head = (s - c0) // LANES + hh
                    o_ref[head, :LANES, :] = r[:, hh * LANES:(hh + 1) * LANES].T.astype(o_ref.dtype)
                    o_ref[head, LANES:, :] = ones


def norm_proj(xa, xb, g, w, b, outs, tr, n_prompt, og=None, meta=None):
    na, d = xa.shape
    nb = 0 if xb is None else xb.shape[0]
    n = na + nb
    npb = n_prompt // tr
    two = xb is not None
    combine = og is not None
    in_specs = [pl.BlockSpec((tr, d), (lambda i: (jnp.minimum(i, npb - 1), 0)) if two else (lambda i: (i, 0)))]
    args = [xa]
    if two:
        in_specs.append(pl.BlockSpec((tr, d), lambda i: (jnp.maximum(i - npb, 0), 0)))
        args.append(xb)
    if combine:
        in_specs += [pl.BlockSpec((tr, og.shape[1]), lambda i: (i, 0)), pl.BlockSpec((tr, LANES), lambda i: (i, 0))]
        args += [og, meta]
    in_specs.append(pl.BlockSpec((1, d), lambda i: (0, 0)))
    args.append(g.reshape(1, d).astype(F32))
    in_specs.append(pl.BlockSpec(w.shape, lambda i: (0, 0)))
    args.append(w)
    if b is not None:
        in_specs.append(pl.BlockSpec((1, w.shape[1]), lambda i: (0, 0)))
        args.append(b.reshape(1, -1).astype(F32))
    out_shapes, out_specs = [], []
    if combine:
        out_shapes.append(jax.ShapeDtypeStruct((n, d), F32))
        out_specs.append(pl.BlockSpec((tr, d), lambda i: (i, 0)))
    for c0, c1, layout, dt in outs:
        assert c0 % COL_CHUNK == 0 and c1 % COL_CHUNK == 0
        if layout == "flat":
            out_shapes.append(jax.ShapeDtypeStruct((n, c1 - c0), dt))
            out_specs.append(pl.BlockSpec((tr, c1 - c0), lambda i: (i, 0)))
        elif layout == "split":
            out_shapes.append(jax.ShapeDtypeStruct((n_prompt, c1 - c0), dt))
            out_specs.append(pl.BlockSpec((tr, c1 - c0), lambda i: (jnp.minimum(i, npb - 1), 0)))
            out_shapes.append(jax.ShapeDtypeStruct((n - n_prompt, c1 - c0), dt))
            out_specs.append(pl.BlockSpec((tr, c1 - c0), lambda i: (jnp.maximum(i - npb, 0), 0)))
        elif layout == "heads":
            nh = (c1 - c0) // LANES
            out_shapes.append(jax.ShapeDtypeStruct((nh, n, LANES), dt))
            out_specs.append(pl.BlockSpec((nh, tr, LANES), lambda i: (0, i, 0)))
        else:
            assert layout == "heads_t"
            nh = (c1 - c0) // LANES
            out_shapes.append(jax.ShapeDtypeStruct((nh, LANES + ONES_ROWS, n), dt))
            out_specs.append(pl.BlockSpec((nh, LANES + ONES_ROWS, tr), lambda i: (0, 0, i)))
    kern = functools.partial(_norm_proj_kernel, npb, two, b is not None, combine,
                             [(c0, c1, lay) for c0, c1, lay, _ in outs])
    return pl.pallas_call(
        kern, grid=(n // tr,), in_specs=in_specs, out_specs=out_specs, out_shape=out_shapes,
        compiler_params=_cparams(("arbitrary",)), name="norm_proj")(*args)


def _attn_a_chunk(q, k, v, bias_ref, sink_ref, valid):
    nq = q.shape[0]
    outs = []
    for kh in range(A_KV_HEADS):
        qs = jnp.concatenate(
            [q[:, (kh * A_GROUP + g) * A_HD:(kh * A_GROUP + g + 1) * A_HD] for g in range(A_GROUP)], axis=0)
        kk = k[:, kh * A_HD:(kh + 1) * A_HD]
        s = lax.dot_general(qs, kk, (((1,), (1,)), ((), ())), preferred_element_type=F32) + bias_ref[kh]
        if valid is not None:
            s = jnp.where(valid, s, NEG_INF)
        sk = sink_ref[kh]
        m = jnp.maximum(jnp.max(s, axis=-1, keepdims=True), sk)
        e = jnp.exp2(s - m)
        den = jnp.sum(e, axis=-1, keepdims=True) + jnp.exp2(sk - m)
        p = (e * (1.0 / den)).astype(BF16)
        o = jnp.dot(p, v[:, kh * A_HD:(kh + 1) * A_HD], preferred_element_type=F32)
        for g in range(A_GROUP):
            outs.append(o[g * nq:(g + 1) * nq])
    return jnp.concatenate(outs, axis=1)


def _attn_a_prompt_kernel(qb, nsub, q_ref, kvc_ref, kvp_ref, bias_ref, sink_ref, o_ref, t_ref, p_ref):
    cb = pl.program_id(1)
    back = CHUNK * WIN_CHUNKS
    nk = back + qb
    nkv = A_KV_HEADS * A_HD
    kvfull = jnp.concatenate([kvp_ref[...], kvc_ref[...]], axis=0)
    k_all = kvfull[:, :nkv].astype(BF16)
    vt_all = kvfull[:, nkv:].T.astype(BF16)
    dn = (((1,), (1,)), ((), ()))
    units = [(s, kh) for s in range(nsub) for kh in range(A_KV_HEADS)]
    for s, kh in units:
        qs = jnp.concatenate([q_ref[s * qb:(s + 1) * qb, (kh * A_GROUP + g) * A_HD:(kh * A_GROUP + g + 1) * A_HD]
                              for g in range(A_GROUP)], axis=0)
        t_ref[s * A_KV_HEADS + kh] = lax.dot_general(k_all[s * qb:s * qb + nk, kh * A_HD:(kh + 1) * A_HD], qs, dn,
                                                     preferred_element_type=F32)
    for s in range(nsub):
        outs = []
        for kh in range(A_KV_HEADS):
            u = s * A_KV_HEADS + kh
            st = t_ref[u] + bias_ref[kh]
            if s == 0:
                key_pos = lax.broadcasted_iota(jnp.int32, (nk, 1), 0) + (cb * nsub * qb - back)
                st = st + jnp.where(key_pos >= 0, 0.0, NEG_INF)
            sk = sink_ref[kh]
            m = jnp.maximum(jnp.max(st, axis=0, keepdims=True), sk)
            e = jnp.exp2(st - m)
            inv_den = 1.0 / (jnp.sum(e, axis=0, keepdims=True) + jnp.exp2(sk - m))
            p_ref[u] = e.astype(BF16)
            ot = jnp.dot(vt_all[kh * A_HD:(kh + 1) * A_HD, s * qb:s * qb + nk], p_ref[u],
                         preferred_element_type=F32)
            o = (ot * inv_den).T
            for g in range(A_GROUP):
                outs.append(o[g * qb:(g + 1) * qb])
        o_ref[s * qb:(s + 1) * qb, :] = jnp.concatenate(outs, axis=1).astype(o_ref.dtype)


def _a_prompt_tables(qb, sinks):
    back = CHUNK * WIN_CHUNKS
    slopes = _alibi(A_HEADS).reshape(A_KV_HEADS, A_GROUP)
    qpos = np.arange(qb)
    kpos = np.arange(back + qb) - back
    dist = np.abs(kpos[:, None] - qpos[None, :]).astype(np.float64)
    band_lo = (qpos // CHUNK) * CHUNK - back
    in_band = (kpos[:, None] >= band_lo[None, :]) & (kpos[:, None] < band_lo[None, :] + back + CHUNK)
    bias = np.where(in_band[None, None], -slopes[:, :, None, None] * dist[None, None], -np.inf)
    bias = np.transpose(bias, (0, 2, 1, 3)).reshape(A_KV_HEADS, back + qb, A_GROUP * qb) * LOG2E
    sk = jnp.repeat(sinks.astype(F32).reshape(A_KV_HEADS, A_GROUP), qb, axis=1)[:, None, :] * LOG2E
    return jnp.asarray(bias, F32), sk


def _a_tables(qpos, kpos, sinks):
    slopes = _alibi(A_HEADS).reshape(A_KV_HEADS, A_GROUP)
    dist = np.abs(qpos[:, None] - kpos[None, :]).astype(np.float64)
    bias = -slopes[:, :, None, None] * dist
    nq = len(qpos)
    bias = jnp.asarray(bias.reshape(A_KV_HEADS, A_GROUP * nq, len(kpos)) * LOG2E, F32)
    sk = jnp.repeat(sinks.astype(F32).reshape(A_KV_HEADS, A_GROUP), nq, axis=1)[..., None] * LOG2E
    return bias, sk


def attn_a_prompt(q_all, kv_all, sinks, batch, t, qb):
    nq = A_HEADS * A_HD
    nkv2 = 2 * A_KV_HEADS * A_HD
    back = CHUNK * WIN_CHUNKS
    bias, sk = _a_prompt_tables(qb, sinks)
    nsub = max(s for s in (1, 2, 4, 8) if t % (s * qb) == 0)
    rows = nsub * qb
    nblk = t // rows
    r = rows // back
    return pl.pallas_call(
        functools.partial(_attn_a_prompt_kernel, qb, nsub),
        grid=(batch, nblk),
        in_specs=[
            pl.BlockSpec((rows, nq), lambda b, i: (b * nblk + i, 0)),
            pl.BlockSpec((rows, nkv2), lambda b, i: (b * nblk + i, 0)),
            pl.BlockSpec((back, nkv2), lambda b, i: (jnp.maximum((b * nblk + i) * r - 1, 0), 0)),
            pl.BlockSpec(bias.shape, lambda b, i: (0, 0, 0)),
            pl.BlockSpec(sk.shape, lambda b, i: (0, 0, 0)),
        ],
        out_specs=pl.BlockSpec((rows, nq), lambda b, i: (b * nblk + i, 0)),
        out_shape=jax.ShapeDtypeStruct((batch * t, nq), BF16),
        scratch_shapes=[pltpu.VMEM((nsub * A_KV_HEADS, back + qb, A_GROUP * qb), F32),
                        pltpu.VMEM((nsub * A_KV_HEADS, back + qb, A_GROUP * qb), BF16)],
        compiler_params=_cparams(("arbitrary", "arbitrary")), name="attn_a_prompt",
    )(q_all, kv_all, kv_all, bias, sk)


def _attn_a_sample_kernel(q_ref, k_ref, v_ref, bias_ref, sink_ref, o_ref):
    o = _attn_a_chunk(q_ref[...], k_ref[...].astype(BF16), v_ref[...].astype(BF16), bias_ref, sink_ref, None)
    o_ref[...] = o.astype(o_ref.dtype)


def attn_a_sample(q_all, kband, vband, sinks, row0, dec_batch, dec_t):
    nq = A_HEADS * A_HD
    s = kband.shape[1]
    past = s - dec_t
    kpos = np.concatenate([np.arange(past) - past, np.arange(dec_t)])
    bias, sk = _a_tables(np.arange(dec_t), kpos, sinks)
    blk0 = row0 // dec_t
    return pl.pallas_call(
        _attn_a_sample_kernel,
        grid=(dec_batch,),
        in_specs=[
            pl.BlockSpec((dec_t, nq), lambda b: (blk0 + b, 0)),
            pl.BlockSpec((None, s, kband.shape[2]), lambda b: (b, 0, 0)),
            pl.BlockSpec((None, s, vband.shape[2]), lambda b: (b, 0, 0)),
            pl.BlockSpec(bias.shape, lambda b: (0, 0, 0)),
            pl.BlockSpec(sk.shape, lambda b: (0, 0, 0)),
        ],
        out_specs=pl.BlockSpec((dec_t, nq), lambda b: (b, 0)),
        out_shape=jax.ShapeDtypeStruct((dec_batch * dec_t, nq), BF16),
        compiler_params=_cparams(("arbitrary",)), name="attn_a_sample",
    )(q_all, kband, vband, bias, sk)


B_Q_SCALE = B_HD ** -0.5 * float(np.log2(np.e))
LOG2E = float(np.log2(np.e))


def _stack_maps(qh):
    lane = lax.broadcasted_iota(jnp.int32, qh.shape, 1)
    zero = jnp.zeros_like(qh)
    return jnp.concatenate([jnp.where(lane < B_HD, qh, zero), jnp.where(lane >= B_HD, qh, zero)], axis=0)


def _diff_finish(acc, l, nq, lam, lam_init, subln):
    inv = 1.0 / l
    o = acc[:nq] * inv[:nq] - lam * (acc[nq:] * inv[nq:])
    ms = jnp.mean(o * o, axis=-1, keepdims=True)
    return o * lax.rsqrt(ms + SUBLN_EPS) * subln * (1.0 - lam_init)


def _attn_b_prompt_kernel(qb, nb, lam_init, pair_ref, slope_ref, lam_ref, q_ref, k_ref, vt_ref, kbias_ref,
                          subln_ref, o_ref, acc_ref, m_ref, al_ref, qq_ref, t0_ref, t1_ref, p0_ref, p1_ref):
    g = pl.program_id(1)
    lam = lam_ref[0]
    dn = (((1,), (1,)), ((), ()))
    t_refs = (t0_ref, t1_ref)
    p_refs = (p0_ref, p1_ref)
    heads = range(HEAD_GROUP)
    n_pairs = nb * (nb + 1) // 2
    slopes = [slope_ref[g * HEAD_GROUP + u] for u in heads]

    def pair(n):
        return pair_ref[2 * n], pair_ref[2 * n + 1]

    def scores(n, slot):
        j, iq = pair(n)
        ks = pl.multiple_of(j * qb, qb)
        for u in heads:
            t_refs[slot][u] = lax.dot_general(k_ref[u, pl.ds(ks, qb), :], qq_ref[u * nb + iq], dn,
                                              preferred_element_type=F32)

    def softmax_step(n, slot):
        j, iq = pair(n)
        on_diag = (j == iq).astype(jnp.int32)
        for u in heads:
            st = u * nb + iq
            m = m_ref[st, 0:1, :]
            t = t_refs[slot][u] + kbias_ref[u, on_diag]
            off = slopes[u] * jnp.full((1, 2 * qb), (j - iq) * qb, jnp.int32).astype(F32)
            m_new = jnp.maximum(m, jnp.max(t, axis=0, keepdims=True) + off)
            p_refs[slot][u] = jnp.exp2(t + (off - m_new)).astype(BF16)
            al_ref[slot * HEAD_GROUP + u, 0:1, :] = jnp.exp2(m - m_new)
            m_ref[st, 0:1, :] = m_new

    def accumulate(n, slot):
        j, iq = pair(jnp.maximum(n, 0))
        ks = pl.multiple_of(j * qb, qb)
        for u in heads:
            st = u * nb + iq
            pv = jnp.dot(vt_ref[u, :, pl.ds(ks, qb)], p_refs[slot][u], preferred_element_type=F32)
            acc_ref[st] = al_ref[slot * HEAD_GROUP + u, 0:1, :] * acc_ref[st] + pv

    def step(n, slot):
        softmax_step(n, slot)
        accumulate(n - 1, 1 - slot)
        scores(jnp.minimum(n + 1, n_pairs - 1), 1 - slot)
        return 0

    acc_ref[...] = jnp.zeros_like(acc_ref)
    m_ref[...] = jnp.full(m_ref.shape, NEG_INF, F32)
    al_ref[...] = jnp.ones_like(al_ref)
    p1_ref[...] = jnp.zeros_like(p1_ref)
    for u in heads:
        for iq in range(nb):
            qq_ref[u * nb + iq] = _stack_maps(q_ref[u, iq * qb:(iq + 1) * qb, :])
    scores(0, 0)
    def steps(k, c):
        for r in range(PAIR_UNROLL):
            step(PAIR_UNROLL * k + r, r % 2)
        return c

    lax.fori_loop(0, n_pairs // PAIR_UNROLL, steps, 0)
    for n in range(n_pairs - n_pairs % PAIR_UNROLL, n_pairs):
        step(n, n % 2)
    accumulate(n_pairs - 1, (n_pairs - 1) % 2)
    for u in heads:
        for iq in range(nb):
            inv = 1.0 / acc_ref[u * nb + iq, LANES:LANES + 1, :]
            acc = acc_ref[u * nb + iq, :LANES, :]
            ot = acc[:, :qb] * inv[:, :qb] - lam * (acc[:, qb:] * inv[:, qb:])
            ms = jnp.mean(ot * ot, axis=0, keepdims=True)
            ot = ot * lax.rsqrt(ms + SUBLN_EPS) * subln_ref[...] * (1.0 - lam_init)
            o_ref[u, iq * qb:(iq + 1) * qb, :] = ot.T.astype(o_ref.dtype)


def attn_b_prompt(qkvh, vt, lam, subln, lam_init, batch, t, qb):
    nblk = t // qb
    qq = np.arange(qb)
    slope2 = (_alibi(B_HEADS) * LOG2E)[:, None, None]
    kcol = np.broadcast_to(qq[:, None], (qb, qb)).astype(np.float64)
    tri = -2.0 * np.maximum(qq[:, None] - qq[None, :], 0)
    msk = np.where((qq[:, None] // CHUNK) <= (qq[None, :] // CHUNK), 0.0, -np.inf)
    both = lambda a: np.concatenate([a, a], axis=-1)
    kbias = np.stack([both(slope2 * kcol[None]), both(slope2 * (kcol + tri)[None] + msk[None])], axis=1)
    kbias = jnp.asarray(kbias, F32)
    slopes = jnp.asarray(_alibi(B_HEADS) * LOG2E, F32)
    subln_t = jnp.broadcast_to(subln.astype(F32).reshape(LANES, 1), (LANES, qb))
    pairs = jnp.asarray([v for j in range(nblk) for i in range(j, nblk) for v in (j, i)], jnp.int32)
    hg = HEAD_GROUP
    n_groups = B_HEADS // hg
    grid_spec = pltpu.PrefetchScalarGridSpec(
        num_scalar_prefetch=3, grid=(batch, n_groups),
        in_specs=[
            pl.BlockSpec((hg, t, LANES), lambda b, g, *_: (g, b, 0)),
            pl.BlockSpec((hg, t, LANES), lambda b, g, *_: (n_groups + g, b, 0)),
            pl.BlockSpec((hg, LANES + ONES_ROWS, t), lambda b, g, *_: (g, 0, b)),
            pl.BlockSpec((hg,) + kbias.shape[1:], lambda b, g, *_: (g, 0, 0, 0)),
            pl.BlockSpec(subln_t.shape, lambda b, g, *_: (0, 0)),
        ],
        out_specs=pl.BlockSpec((hg, t, LANES), lambda b, g, *_: (g, b, 0)),
        scratch_shapes=[pltpu.VMEM((hg * nblk, LANES + ONES_ROWS, 2 * qb), F32),
                        pltpu.VMEM((hg * nblk, SUBLANES, 2 * qb), F32),
                        pltpu.VMEM((2 * hg, SUBLANES, 2 * qb), F32),
                        pltpu.VMEM((hg * nblk, 2 * qb, LANES), BF16),
                        pltpu.VMEM((hg, qb, 2 * qb), F32), pltpu.VMEM((hg, qb, 2 * qb), F32),
                        pltpu.VMEM((hg, qb, 2 * qb), BF16), pltpu.VMEM((hg, qb, 2 * qb), BF16)],
    )
    return pl.pallas_call(
        functools.partial(_attn_b_prompt_kernel, qb, nblk, lam_init), grid_spec=grid_spec,
        out_shape=jax.ShapeDtypeStruct((B_HEADS, batch * t, LANES), BF16),
        compiler_params=_cparams(("arbitrary", "arbitrary")), name="attn_b_prompt",
    )(pairs, slopes, lam.reshape(1).astype(F32), qkvh, qkvh, vt, kbias, subln_t)


def _attn_b_sample_kernel(dec_t, past, lam_init, slopes, lam_ref, q_ref, kn_ref, vn_ref, ck_ref, cv_ref,
                          tnew_ref, subln_ref, o_ref):
    lam = lam_ref[0]
    subln = subln_ref[...]
    colc = lax.broadcasted_iota(jnp.int32, (1, past), 1).astype(F32) - float(past)
    for h in range(B_HEADS):
        slope = float(slopes[h]) * LOG2E
        qq = _stack_maps(q_ref[h])
        kc = ck_ref[pl.ds(h, past, stride=B_HEADS), :].astype(BF16)
        vc = cv_ref[pl.ds(h, past, stride=B_HEADS), :].astype(BF16)
        dn = (((1,), (1,)), ((), ()))
        s_c = lax.dot_general(qq, kc, dn, preferred_element_type=F32) + slope * colc
        s_n = lax.dot_general(qq, kn_ref[h], dn, preferred_element_type=F32) + slope * tnew_ref[...]
        m = jnp.maximum(jnp.max(s_c, axis=-1, keepdims=True), jnp.max(s_n, axis=-1, keepdims=True))
        p_c = jnp.exp2(s_c - m)
        p_n = jnp.exp2(s_n - m)
        l = jnp.sum(p_c, axis=-1, keepdims=True) + jnp.sum(p_n, axis=-1, keepdims=True)
        acc = (jnp.dot(p_c.astype(BF16), vc, preferred_element_type=F32)
               + jnp.dot(p_n.astype(BF16), vn_ref[h], preferred_element_type=F32))
        o_ref[h] = _diff_finish(acc, l, dec_t, lam, lam_init, subln).astype(o_ref.dtype)


def attn_b_sample(qkvh, cache_k, cache_v, lam, subln, lam_init, row0, dec_batch, dec_t):
    past = cache_k.shape[1]
    qi = np.arange(dec_t)
    tnew = (qi[:, None] - np.abs(qi[:, None] - qi[None, :])).astype(np.float32)
    tnew = jnp.asarray(np.concatenate([tnew, tnew], axis=0))
    blk0 = row0 // dec_t
    rows = past * B_HEADS
    cache_k = cache_k.reshape(dec_batch, rows, LANES)
    cache_v = cache_v.reshape(dec_batch, rows, LANES)
    grid_spec = pltpu.PrefetchScalarGridSpec(
        num_scalar_prefetch=1, grid=(dec_batch,),
        in_specs=[
            pl.BlockSpec((B_HEADS, dec_t, LANES), lambda b, *_: (0, blk0 + b, 0)),
            pl.BlockSpec((B_HEADS, dec_t, LANES), lambda b, *_: (1, blk0 + b, 0)),
            pl.BlockSpec((B_HEADS, dec_t, LANES), lambda b, *_: (2, blk0 + b, 0)),
            pl.BlockSpec((None, rows, LANES), lambda b, *_: (b, 0, 0)),
            pl.BlockSpec((None, rows, LANES), lambda b, *_: (b, 0, 0)),
            pl.BlockSpec(tnew.shape, lambda b, *_: (0, 0)),
            pl.BlockSpec((1, LANES), lambda b, *_: (0, 0)),
        ],
        out_specs=pl.BlockSpec((B_HEADS, dec_t, LANES), lambda b, *_: (0, b, 0)),
    )
    return pl.pallas_call(
        functools.partial(_attn_b_sample_kernel, dec_t, past, lam_init, tuple(_alibi(B_HEADS))),
        grid_spec=grid_spec,
        out_shape=jax.ShapeDtypeStruct((B_HEADS, dec_batch * dec_t, LANES), BF16),
        compiler_params=_cparams(("arbitrary",)), name="attn_b_sample",
    )(lam.reshape(1).astype(F32), qkvh, qkvh, qkvh, cache_k, cache_v, tnew, subln.reshape(1, LANES).astype(F32))


ROUTE_ROWS = 40
ROUTE_FIELDS = 8


def _route_t(lt):
    row = lax.broadcasted_iota(jnp.int32, lt.shape, 0).astype(F32)
    big = float(LANES)
    lg = jnp.where(row < N_GROUPS, lt, NEG_INF)
    gmax = jnp.max(lg, axis=0, keepdims=True)
    g_idx = jnp.min(jnp.where(lg == gmax, row, big), axis=0, keepdims=True)
    g_prob = 1.0 / jnp.sum(jnp.exp(lg - gmax), axis=0, keepdims=True)
    lo = N_GROUPS + EXPERTS_PER_GROUP * g_idx
    le = jnp.where((row >= lo) & (row < lo + EXPERTS_PER_GROUP), lt, NEG_INF)
    m1 = jnp.max(le, axis=0, keepdims=True)
    i1 = jnp.min(jnp.where(le == m1, row, big), axis=0, keepdims=True)
    le2 = jnp.where(row == i1, NEG_INF, le)
    m2 = jnp.max(le2, axis=0, keepdims=True)
    i2 = jnp.min(jnp.where(le2 == m2, row, big), axis=0, keepdims=True)
    t = jnp.exp(m2 - m1)
    ga = 1.0 / (1.0 + t)
    gb = t * ga
    return row, i1, i2, g_prob * ga, g_prob * gb


def _unpack_pairs(u):
    lo = pltpu.bitcast(u << 16, F32)
    hi = pltpu.bitcast(u & jnp.uint32(0xFFFF0000), F32)
    return jnp.concatenate([lo, hi], axis=1)


def _pack_pairs(x):
    bits = pltpu.bitcast(x.astype(BF16).astype(F32), jnp.uint32)
    w = x.shape[1] // 2
    return (bits[:, :w] >> 16) | bits[:, w:]


def _out_router_kernel(n_prompt_blocks, heads_in, two_o, two_res, *refs):
    refs = list(refs)
    oa_ref = refs.pop(0)
    ob_ref = refs.pop(0) if two_o else None
    ra_ref = refs.pop(0)
    rb_ref = refs.pop(0) if two_res else None
    (wo_ref, g_ref, wr_ref, wr2_ref, br_ref, utri_ref, h_ref, xn_ref, meta_ref, route_ref, cnt_ref,
     carry_ref) = refs
    is_prompt = pl.program_id(0) < n_prompt_blocks

    def load_o(ref):
        if heads_in:
            return jnp.concatenate([ref[hh] for hh in range(ref.shape[0])], axis=1)
        return ref[...]

    o = load_o(oa_ref)
    if two_o:
        o = jnp.where(is_prompt, o, load_o(ob_ref))
    res = ra_ref[...]
    if two_res:
        res = jnp.where(is_prompt, res, rb_ref[...])
    h = res + jnp.dot(o, wo_ref[...], preferred_element_type=F32)
    h_ref[...] = h
    ms = jnp.mean(h * h, axis=-1, keepdims=True)
    xn = h * lax.rsqrt(ms + NORM_EPS) * g_ref[...]
    x_hi = xn.astype(BF16)
    _store_planes(xn_ref, _pack_pairs(xn))
    x_lo = (xn - x_hi.astype(F32)).astype(BF16)
    r = jnp.dot(x_hi, wr_ref[...], preferred_element_type=F32)
    logits = (r[:, :LANES] + r[:, LANES:]) + jnp.dot(x_lo, wr2_ref[...], preferred_element_type=F32) + br_ref[...]
    row, i1, i2, g1, g2 = _route_t(logits.T[:ROUTE_ROWS])

    @pl.when(pl.program_id(0) == 0)
    def _():
        carry_ref[...] = jnp.zeros_like(carry_ref)

    sel1 = row == i1
    sel2 = row == i2
    onehot = jnp.where(sel1 | sel2, 1.0, 0.0)
    before = jnp.dot(onehot.astype(BF16), utri_ref[...], preferred_element_type=F32) + carry_ref[:, 0:1]
    rank1 = jnp.sum(jnp.where(sel1, before, 0.0), axis=0, keepdims=True)
    rank2 = jnp.sum(jnp.where(sel2, before, 0.0), axis=0, keepdims=True)
    carry_ref[...] = carry_ref[...] + jnp.sum(onehot, axis=1, keepdims=True)
    cnt_ref[...] = carry_ref[...]
    fields = [i1 - N_GROUPS, i2 - N_GROUPS, g1, g2, rank1, rank2]
    fields += [jnp.zeros_like(g1)] * (ROUTE_FIELDS - len(fields))
    route = jnp.concatenate(fields, axis=0)
    route_ref[...] = route
    meta_ref[...] = jnp.concatenate([route, jnp.zeros((LANES - ROUTE_FIELDS, route.shape[1]), F32)], axis=0).T


def out_router(oa, ob, ra, rb, wo, g, w_rg, b_rg, w_re, b_re, heads_in, tr):
    d = ra.shape[1]
    if heads_in:
        na = oa.shape[1]
        nb = 0 if ob is None else ob.shape[1]
    else:
        na = oa.shape[0]
        nb = 0 if ob is None else ob.shape[0]
    n = na + nb
    npb = na // tr
    two_o = ob is not None
    two_res = rb is not None
    first = lambda i: (jnp.minimum(i, npb - 1), 0)
    second = lambda i: (jnp.maximum(i - npb, 0), 0)
    plain = lambda i: (i, 0)
    in_specs, args = [], []

    def add_o(x, imap):
        if heads_in:
            in_specs.append(pl.BlockSpec((x.shape[0], tr, LANES), lambda i: (0, imap(i)[0], 0)))
        else:
            in_specs.append(pl.BlockSpec((tr, x.shape[1]), imap))
        args.append(x)

    add_o(oa, first if two_o else plain)
    if two_o:
        add_o(ob, second)
    in_specs.append(pl.BlockSpec((tr, d), first if two_res else plain))
    args.append(ra)
    if two_res:
        in_specs.append(pl.BlockSpec((tr, d), second))
        args.append(rb)
    wr = jnp.zeros((d, LANES), F32).at[:, :N_GROUPS].set(w_rg.astype(F32))
    wr = wr.at[:, N_GROUPS:N_GROUPS + N_EXPERTS].set(w_re.astype(F32))
    wr_hi = wr.astype(BF16)
    wr_lo = (wr - wr_hi.astype(F32)).astype(BF16)
    br = jnp.zeros((1, LANES), F32).at[0, :N_GROUPS].set(b_rg.astype(F32))
    br = br.at[0, N_GROUPS:N_GROUPS + N_EXPERTS].set(b_re.astype(F32))
    utri = jnp.asarray(np.triu(np.ones((tr, tr), np.float32), 1), BF16)
    consts = [wo, g.reshape(1, d).astype(F32), jnp.concatenate([wr_hi, wr_lo], axis=1), wr_hi, br, utri]
    for cst in consts:
        in_specs.append(pl.BlockSpec(cst.shape, lambda i: (0, 0)))
        args.append(cst)
    parts = d // 2 // SC_ROW_WORDS
    out_shape = [jax.ShapeDtypeStruct((n, d), F32), jax.ShapeDtypeStruct((parts, n, SC_ROW_WORDS), jnp.uint32),
                 jax.ShapeDtypeStruct((n, LANES), F32), jax.ShapeDtypeStruct((ROUTE_FIELDS, n), F32),
                 jax.ShapeDtypeStruct((ROUTE_ROWS, LANES), F32)]
    out_specs = [pl.BlockSpec((tr, d), plain), pl.BlockSpec((parts, tr, SC_ROW_WORDS), lambda i: (0, i, 0)),
                 pl.BlockSpec((tr, LANES), plain), pl.BlockSpec((ROUTE_FIELDS, tr), lambda i: (0, i)),
                 pl.BlockSpec((ROUTE_ROWS, LANES), lambda i: (0, 0))]
    return pl.pallas_call(
        functools.partial(_out_router_kernel, npb, heads_in, two_o, two_res),
        grid=(n // tr,), in_specs=in_specs, out_specs=out_specs, out_shape=out_shape,
        scratch_shapes=[pltpu.VMEM((ROUTE_ROWS, LANES), F32)],
        compiler_params=_cparams(("arbitrary",)), name="out_router")(*args)


def _expert_kernel(tm, tps, te_ref, nt_ref, x_ref, *refs):
    w_refs, o_ref = refs[:3 * tps], refs[3 * tps]
    first = pl.program_id(0) * tps
    n_valid = jnp.clip(nt_ref[0] - first, 0, tps)

    def tile(k):
        wg_ref, wu_ref, wd_ref = w_refs[3 * k:3 * k + 3]
        rows = slice(k * tm, (k + 1) * tm)
        x = _unpack_pairs(jnp.concatenate([x_ref[q, rows, :] for q in range(x_ref.shape[0])], axis=1)).astype(BF16)
        a = jnp.dot(x, wg_ref[...].astype(BF16), preferred_element_type=F32)
        u = jnp.dot(x, wu_ref[...].astype(BF16), preferred_element_type=F32)
        hh = (a * (1.0 / (1.0 + jnp.exp(-a))) * u).astype(BF16)
        packed = _pack_pairs(jnp.dot(hh, wd_ref[...].astype(BF16), preferred_element_type=F32))
        for q in range(o_ref.shape[0]):
            o_ref[q, rows, :] = packed[:, q * o_ref.shape[2]:(q + 1) * o_ref.shape[2]]

    for nv in range(tps + 1):

        @pl.when(n_valid == nv)
        def _(nv=nv):
            for k in range(nv):
                tile(k)
            if nv < tps:
                o_ref[:, nv * tm:, :] = jnp.zeros((o_ref.shape[0], (tps - nv) * tm, o_ref.shape[2]), o_ref.dtype)


def expert_mlp(xs, tile_expert, n_tiles, w_gate, w_up, w_down, layer, tm):
    parts, p, w = xs.shape
    d, f = w_gate.shape[-2:]
    epg = w_gate.shape[2]
    tps = 2 if (p // tm) % 2 == 0 else 1
    w_specs, w_args = [], []
    for k in range(tps):
        wmap = lambda t, te, nt, k=k: (layer, te[tps * t + k] // epg, te[tps * t + k] % epg, 0, 0)
        w_specs += [pl.BlockSpec((None, None, None, d, f), wmap), pl.BlockSpec((None, None, None, d, f), wmap),
                    pl.BlockSpec((None, None, None, f, d), wmap)]
        w_args += [w_gate, w_up, w_down]
    grid_spec = pltpu.PrefetchScalarGridSpec(
        num_scalar_prefetch=2, grid=(p // (tps * tm),),
        in_specs=[pl.BlockSpec((parts, tps * tm, w), lambda t, te, nt: (0, t, 0))] + w_specs,
        out_specs=pl.BlockSpec((parts, tps * tm, w), lambda t, te, nt: (0, t, 0)),
    )
    return pl.pallas_call(
        functools.partial(_expert_kernel, tm, tps), grid_spec=grid_spec,
        out_shape=jax.ShapeDtypeStruct((parts, p, w), jnp.uint32),
        compiler_params=_cparams(("arbitrary",)), name="expert_mlp")(tile_expert, n_tiles, xs, *w_args)


def sc_scatter_rows(x2, idx, n_out):
    mesh = plsc.VectorSubcoreMesh(core_axis_name="c", subcore_axis_name="s")
    n_src_win = x2.shape[0] // SC_WINDOW

    @functools.partial(pl.kernel, out_type=jax.ShapeDtypeStruct((n_out, SC_ROW_WORDS), x2.dtype), mesh=mesh)
    def scatter(x_hbm, i_hbm, o_hbm):
        def body(x_vmem, i_vmem):
            pltpu.sync_copy(x_vmem, o_hbm.at[i_vmem.at[0]])

        pltpu.emit_pipeline(
            body, grid=(n_out // SC_WINDOW,),
            in_specs=[pl.BlockSpec((SC_WINDOW, SC_ROW_WORDS), lambda i: (lax.rem(i, n_src_win), 0)),
                      pl.BlockSpec((1, SC_WINDOW), lambda i: (0, i))],
            out_specs=[], core_axis_name=("c", "s"), dimension_semantics=(pltpu.PARALLEL,),
        )(x_hbm, i_hbm)

    return scatter(x2, idx)


def sc_gather_rows(x2, idx):
    mesh = plsc.VectorSubcoreMesh(core_axis_name="c", subcore_axis_name="s")
    n_col, n = idx.shape

    @functools.partial(pl.kernel, out_type=jax.ShapeDtypeStruct((n, n_col * SC_ROW_WORDS), x2.dtype), mesh=mesh)
    def gather(x_hbm, i_hbm, o_hbm):
        def body(i_vmem, o_vmem):
            pltpu.sync_copy(x_hbm.at[i_vmem.at[0]], o_vmem)

        pltpu.emit_pipeline(
            body, grid=(n // SC_WINDOW, n_col),
            in_specs=[pl.BlockSpec((1, SC_WINDOW), lambda i, c: (c, i))],
            out_specs=[pl.BlockSpec((SC_WINDOW, SC_ROW_WORDS), lambda i, c: (i, c))],
            core_axis_name=("c", "s"), dimension_semantics=(pltpu.PARALLEL, pltpu.ARBITRARY),
        )(i_hbm, o_hbm)

    return gather(x2, idx)


def moe_dispatch(xn_packed, route, counts, tm):
    parts, n, w = xn_packed.shape
    e = route[0:2].astype(jnp.int32)
    rank = route[4:6].astype(jnp.int32)
    counts = counts[N_GROUPS:N_GROUPS + N_EXPERTS, 0].astype(jnp.int32)
    padded = ((counts + tm - 1) // tm) * tm
    pend = jnp.cumsum(padded)
    pstart = pend - padded
    ids = jnp.arange(N_EXPERTS, dtype=jnp.int32)
    pos = jnp.sum(jnp.where(e[..., None] == ids, pstart, 0), axis=-1) + rank
    p = 2 * n + N_EXPERTS * tm
    n_pad = p - 2 * n
    gap_len = jnp.concatenate([padded - counts, (p - pend[-1]).reshape(1)])
    gap_first = jnp.concatenate([pstart + counts, pend[-1:]])
    gap_end = jnp.cumsum(gap_len)
    k = jnp.arange(n_pad, dtype=jnp.int32)
    gap = jnp.sum((k[:, None] >= gap_end[None, :]).astype(jnp.int32), axis=1)
    shift = gap_first - (gap_end - gap_len)
    pad_pos = k + jnp.sum(jnp.where(gap[:, None] == jnp.arange(N_EXPERTS + 1), shift, 0), axis=-1)
    planes = lambda rows: jnp.concatenate([rows + q * p for q in range(parts)])
    scat_idx = jnp.concatenate([planes(pos[0]), planes(pos[1]), planes(pad_pos)]).reshape(1, parts * p)
    gath_idx = jnp.stack([pos[k] + q * p for k in range(2) for q in range(parts)])
    xs = sc_scatter_rows(xn_packed.reshape(parts * n, w), scat_idx, parts * p).reshape(parts, p, w)
    tile_start = jnp.arange(p // tm, dtype=jnp.int32) * tm
    tile_expert = jnp.sum((tile_start[:, None] >= pend[None, :]).astype(jnp.int32), axis=1)
    tile_expert = jnp.minimum(tile_expert, N_EXPERTS - 1)
    n_tiles = (pend[-1] // tm).astype(jnp.int32).reshape(1)
    return xs, tile_expert, n_tiles, gath_idx


def moe_experts(xn_packed, route, counts, w_gate, w_up, w_down, layer, tm):
    parts, n, w = xn_packed.shape
    xs, tile_expert, n_tiles, gath_idx = moe_dispatch(xn_packed, route, counts, tm)
    o = expert_mlp(xs, tile_expert, n_tiles, w_gate, w_up, w_down, layer, tm)
    return sc_gather_rows(o.reshape(parts * o.shape[1], w), gath_idx)


def _final_norm_kernel(n_prompt_blocks, h_ref, o0_ref, o1_ref, meta_ref, g_ref, yp_ref, ys_ref):
    meta = meta_ref[...]
    x = (h_ref[...] + meta[:, 2:3] * _unpack_pairs(o0_ref[...])) + meta[:, 3:4] * _unpack_pairs(o1_ref[...])
    ms = jnp.mean(x * x, axis=-1, keepdims=True)
    y = x * lax.rsqrt(ms + NORM_EPS) * g_ref[...]
    i = pl.program_id(0)

    @pl.when(i < n_prompt_blocks)
    def _():
        yp_ref[...] = y

    @pl.when(i >= n_prompt_blocks)
    def _():
        ys_ref[...] = y


def final_norm(h, og, meta, g, n_prompt, tr):
    n, d = h.shape
    npb = n_prompt // tr
    return pl.pallas_call(
        functools.partial(_final_norm_kernel, npb),
        grid=(n // tr,),
        in_specs=[pl.BlockSpec((tr, d), lambda i: (i, 0)),
                  pl.BlockSpec((tr, og.shape[1] // 2), lambda i: (i, 0)),
                  pl.BlockSpec((tr, og.shape[1] // 2), lambda i: (i, 1)),
                  pl.BlockSpec((tr, LANES), lambda i: (i, 0)), pl.BlockSpec((1, d), lambda i: (0, 0))],
        out_specs=[pl.BlockSpec((tr, d), lambda i: (jnp.minimum(i, npb - 1), 0)),
                   pl.BlockSpec((tr, d), lambda i: (jnp.maximum(i - npb, 0), 0))],
        out_shape=[jax.ShapeDtypeStruct((n_prompt, d), F32), jax.ShapeDtypeStruct((n - n_prompt, d), F32)],
        compiler_params=_cparams(("arbitrary",)), name="final_norm")(h, og, og, meta, g.reshape(1, d).astype(F32))


def _pick_rows_tile(n_prompt, n_sample):
    tr = ROW_TILE
    while tr >= SUBLANES:
        if n_prompt % tr == 0 and n_sample % tr == 0:
            return tr
        tr //= 2
    raise ValueError("row counts must be multiples of the sublane count")


def kernel(x_prompt, x_sample, cache_a_k, cache_a_v, cache_b_k, cache_b_v, norm_mix, norm_ffn, norm_final,
           w_a_qkv, b_a_qkv, a_sinks, w_a_o, w_b_qkv, b_lambda, b_subln, w_b_o,
           w_route_group, b_route_group, w_route_expert, b_route_expert, w_gate, w_up, w_down):
    batch, t, d = x_prompt.shape
    dec_batch, dec_t, _ = x_sample.shape
    n_p, n_s = batch * t, dec_batch * dec_t
    tr = _pick_rows_tile(n_p, n_s)
    tm = min(EXPERT_TILE, 4 * tr)
    xp = x_prompt.reshape(n_p, d)
    xs = x_sample.reshape(n_s, d)
    nq_a = A_HEADS * A_HD
    nkv_a = A_KV_HEADS * A_HD

    a_scale = jnp.where(jnp.arange(nq_a + 2 * nkv_a) < nq_a, A_HD ** -0.5 * LOG2E, 1.0).astype(F32)
    q_a, kv_ap, kv_as = norm_proj(xp, xs, norm_mix[0], (w_a_qkv[0] * a_scale).astype(BF16), b_a_qkv[0] * a_scale,
                                  [(0, nq_a, "flat", BF16), (nq_a, nq_a + 2 * nkv_a, "split", F32)], tr, n_p)
    qb_a = min(CHUNK * WIN_CHUNKS, t)
    o_ap = attn_a_prompt(q_a, kv_ap, a_sinks[0], batch, t, qb_a)
    kv_s = kv_as.reshape(dec_batch, dec_t, 2 * nkv_a)
    past_a = cache_a_k.shape[2]
    kband = jnp.concatenate([cache_a_k[0].reshape(dec_batch, past_a, nkv_a), kv_s[..., :nkv_a]], axis=1)
    vband = jnp.concatenate([cache_a_v[0].reshape(dec_batch, past_a, nkv_a), kv_s[..., nkv_a:]], axis=1)
    o_as = attn_a_sample(q_a, kband, vband, a_sinks[0], n_p, dec_batch, dec_t)
    h, xn, meta, route, counts = out_router(o_ap, o_as, xp, xs, w_a_o[0].astype(BF16), norm_ffn[0], w_route_group[0],
                                            b_route_group[0], w_route_expert[0], b_route_expert[0], False, tr)
    og = moe_experts(xn, route, counts, w_gate, w_up, w_down, 0, tm)

    keep = min(CHUNK * WIN_CHUNKS, t)
    kv_p = kv_ap.reshape(batch, t, 2 * nkv_a)[:, t - keep:]
    a_k_prompt = kv_p[..., :nkv_a].reshape(1, batch, keep, A_KV_HEADS, A_HD)
    a_v_prompt = kv_p[..., nkv_a:].reshape(1, batch, keep, A_KV_HEADS, A_HD)
    a_k_sample = kband[:, dec_t:].reshape(1, dec_batch, past_a, A_KV_HEADS, A_HD)
    a_v_sample = vband[:, dec_t:].reshape(1, dec_batch, past_a, A_KV_HEADS, A_HD)

    lam_init = 0.8 - 0.6 * float(np.exp(-0.3 * 1))
    lp = b_lambda[0].astype(F32)
    lam = jnp.exp(jnp.sum(lp[0] * lp[1])) - jnp.exp(jnp.sum(lp[2] * lp[3])) + lam_init
    wb = B_HEADS * 2 * B_HD
    col_scale = jnp.where(jnp.arange(3 * wb) < wb, B_Q_SCALE, 1.0).astype(F32)
    h, qkvh, k_bp, k_bs, v_bp, v_bs, vt = norm_proj(
        h, None, norm_mix[1], (w_b_qkv[0] * col_scale).astype(BF16), None,
        [(0, 3 * wb, "heads", BF16), (wb, 2 * wb, "split", F32), (2 * wb, 3 * wb, "split", F32),
         (2 * wb, 3 * wb, "heads_t", BF16)], tr, n_p, og=og, meta=meta)
    qb_b = min(ATTN_BLOCK, t)
    o_bp = attn_b_prompt(qkvh, vt, lam, b_subln[0], lam_init, batch, t, qb_b)
    o_bs = attn_b_sample(qkvh, cache_b_k[0], cache_b_v[0], lam, b_subln[0], lam_init, n_p, dec_batch, dec_t)
    hd2 = 2 * B_HD
    b_k_prompt = k_bp.reshape(1, batch, t, B_HEADS, hd2)
    b_v_prompt = v_bp.reshape(1, batch, t, B_HEADS, hd2)
    o_bp, b_k_prompt, b_v_prompt = lax.optimization_barrier((o_bp, b_k_prompt, b_v_prompt))
    h, xn, meta, route, counts = out_router(o_bp, o_bs, h, None, w_b_o[0].astype(BF16), norm_ffn[1], w_route_group[1],
                                            b_route_group[1], w_route_expert[1], b_route_expert[1], True, tr)
    og = moe_experts(xn, route, counts, w_gate, w_up, w_down, 1, tm)

    y_p, y_s = final_norm(h, og, meta, norm_final, n_p, tr)
    return (y_p.reshape(batch, t, d), y_s.reshape(dec_batch, dec_t, d),
            a_k_prompt, a_v_prompt, a_k_sample, a_v_sample, b_k_prompt, b_v_prompt,
            k_bs.reshape(1, dec_batch, dec_t, B_HEADS, hd2), v_bs.reshape(1, dec_batch, dec_t, B_HEADS, hd2))
```

```python
import functools

import jax
import jax.numpy as jnp
import numpy as np
from jax import lax
from jax.experimental import pallas as pl
from jax.experimental.pallas import tpu as pltpu
from jax.experimental.pallas import tpu_sc as plsc

F32 = jnp.float32
BF16 = jnp.bfloat16
NEG_INF = float("-inf")

CHUNK = 64
WIN_CHUNKS = 2
A_HEADS, A_KV_HEADS, A_GROUP, A_HD = 16, 4, 4, 64
B_HEADS, B_HD = 8, 64
N_GROUPS, EXPERTS_PER_GROUP = 4, 8
N_EXPERTS = N_GROUPS * EXPERTS_PER_GROUP
NORM_EPS = 1e-6
SUBLN_EPS = 1e-5

LANES = 128
SUBLANES = 8
ROW_TILE = 512
EXPERT_TILE = 512
ATTN_BLOCK = 256
COL_CHUNK = 512
PAIR_UNROLL = 12
HEAD_GROUP = 2
ONES_ROWS = 16
SC_WINDOW = 128
SC_ROW_WORDS = 256
VMEM_LIMIT = 48 * 1024 * 1024


def _cparams(sem, flags=None):
    return pltpu.CompilerParams(dimension_semantics=sem, vmem_limit_bytes=VMEM_LIMIT, flags=flags)


def _alibi(n):
    return 2.0 ** (-8.0 * np.arange(1, n + 1) / n)


def _store_planes(ref, x):
    w = ref.shape[2]
    for p in range(ref.shape[0]):
        ref[p] = x[:, p * w:(p + 1) * w]


def _add_expert_outputs(x, og_ref, meta_ref):
    half = og_ref.shape[1] // 2
    meta = meta_ref[...]
    return (x + meta[:, 2:3] * _unpack_pairs(og_ref[:, :half])) + meta[:, 3:4] * _unpack_pairs(og_ref[:, half:])


def _norm_proj_kernel(n_prompt_blocks, two_src, has_bias, combine, outs, *refs):
    refs = list(refs)
    xa_ref = refs.pop(0)
    xb_ref = refs.pop(0) if two_src else None
    og_ref, meta_ref = (refs.pop(0), refs.pop(0)) if combine else (None, None)
    g_ref = refs.pop(0)
    w_ref = refs.pop(0)
    b_ref = refs.pop(0) if has_bias else None
    out_refs = refs
    x = xa_ref[...]
    if two_src:
        x = jnp.where(pl.program_id(0) < n_prompt_blocks, x, xb_ref[...])
    if combine:
        x = _add_expert_outputs(x, og_ref, meta_ref)
        out_refs.pop(0)[...] = x
    ms = jnp.mean(x * x, axis=-1, keepdims=True)
    xn = (x * lax.rsqrt(ms + NORM_EPS) * g_ref[...]).astype(BF16)
    is_prompt = pl.program_id(0) < n_prompt_blocks
    out_refs = list(out_refs)
    targets = []
    for c0, c1, layout in outs:
        targets.append((c0, c1, layout, out_refs.pop(0), out_refs.pop(0) if layout == "split" else None))
    for s in range(0, w_ref.shape[1], COL_CHUNK):
        e = s + COL_CHUNK
        r = jnp.dot(xn, w_ref[:, s:e], preferred_element_type=F32)
        if has_bias:
            r = r + b_ref[:, s:e]
        for c0, c1, layout, o_ref, o2_ref in targets:
            if not (c0 <= s and e <= c1):
                continue
            rr = r.astype(o_ref.dtype)
            if layout == "flat":
                o_ref[:, s - c0:e - c0] = rr
            elif layout == "split":

                @pl.when(is_prompt)
                def _(rr=rr, o_ref=o_ref, s=s, e=e, c0=c0):
                    o_ref[:, s - c0:e - c0] = rr

                @pl.when(jnp.logical_not(is_prompt))
                def _(rr=rr, o2_ref=o2_ref, s=s, e=e, c0=c0):
                    o2_ref[:, s - c0:e - c0] = rr
            elif layout == "heads":
                for hh in range((e - s) // LANES):
                    o_ref[(s - c0) // LANES + hh] = rr[:, hh * LANES:(hh + 1) * LANES]
            else:
                ones = jnp.ones((ONES_ROWS, r.shape[0]), o_ref.dtype)
                for hh in range((e - s) // LANES):
                    head = (s - c0) // LANES + hh
                    o_ref[head, :LANES, :] = r[:, hh * LANES:(hh + 1) * LANES].T.astype(o_ref.dtype)
                    o_ref[head, LANES:, :] = ones


def norm_proj(xa, xb, g, w, b, outs, tr, n_prompt, og=None, meta=None):
    na, d = xa.shape
    nb = 0 if xb is None else xb.shape[0]
    n = na + nb
    npb = n_prompt // tr
    two = xb is not None
    combine = og is not None
    in_specs = [pl.BlockSpec((tr, d), (lambda i: (jnp.minimum(i, npb - 1), 0)) if two else (lambda i: (i, 0)))]
    args = [xa]
    if two:
        in_specs.append(pl.BlockSpec((tr, d), lambda i: (jnp.maximum(i - npb, 0), 0)))
        args.append(xb)
    if combine:
        in_specs += [pl.BlockSpec((tr, og.shape[1]), lambda i: (i, 0)), pl.BlockSpec((tr, LANES), lambda i: (i, 0))]
        args += [og, meta]
    in_specs.append(pl.BlockSpec((1, d), lambda i: (0, 0)))
    args.append(g.reshape(1, d).astype(F32))
    in_specs.append(pl.BlockSpec(w.shape, lambda i: (0, 0)))
    args.append(w)
    if b is not None:
        in_specs.append(pl.BlockSpec((1, w.shape[1]), lambda i: (0, 0)))
        args.append(b.reshape(1, -1).astype(F32))
    out_shapes, out_specs = [], []
    if combine:
        out_shapes.append(jax.ShapeDtypeStruct((n, d), F32))
        out_specs.append(pl.BlockSpec((tr, d), lambda i: (i, 0)))
    for c0, c1, layout, dt in outs:
        assert c0 % COL_CHUNK == 0 and c1 % COL_CHUNK == 0
        if layout == "flat":
            out_shapes.append(jax.ShapeDtypeStruct((n, c1 - c0), dt))
            out_specs.append(pl.BlockSpec((tr, c1 - c0), lambda i: (i, 0)))
        elif layout == "split":
            out_shapes.append(jax.ShapeDtypeStruct((n_prompt, c1 - c0), dt))
            out_specs.append(pl.BlockSpec((tr, c1 - c0), lambda i: (jnp.minimum(i, npb - 1), 0)))
            out_shapes.append(jax.ShapeDtypeStruct((n - n_prompt, c1 - c0), dt))
            out_specs.append(pl.BlockSpec((tr, c1 - c0), lambda i: (jnp.maximum(i - npb, 0), 0)))
        elif layout == "heads":
            nh = (c1 - c0) // LANES
            out_shapes.append(jax.ShapeDtypeStruct((nh, n, LANES), dt))
            out_specs.append(pl.BlockSpec((nh, tr, LANES), lambda i: (0, i, 0)))
        else:
            assert layout == "heads_t"
            nh = (c1 - c0) // LANES
            out_shapes.append(jax.ShapeDtypeStruct((nh, LANES + ONES_ROWS, n), dt))
            out_specs.append(pl.BlockSpec((nh, LANES + ONES_ROWS, tr), lambda i: (0, 0, i)))
    kern = functools.partial(_norm_proj_kernel, npb, two, b is not None, combine,
                             [(c0, c1, lay) for c0, c1, lay, _ in outs])
    return pl.pallas_call(
        kern, grid=(n // tr,), in_specs=in_specs, out_specs=out_specs, out_shape=out_shapes,
        compiler_params=_cparams(("arbitrary",)), name="norm_proj")(*args)


def _attn_a_chunk(q, k, v, bias_ref, sink_ref, valid):
    nq = q.shape[0]
    outs = []
    for kh in range(A_KV_HEADS):
        qs = jnp.concatenate(
            [q[:, (kh * A_GROUP + g) * A_HD:(kh * A_GROUP + g + 1) * A_HD] for g in range(A_GROUP)], axis=0)
        kk = k[:, kh * A_HD:(kh + 1) * A_HD]
        s = lax.dot_general(qs, kk, (((1,), (1,)), ((), ())), preferred_element_type=F32) + bias_ref[kh]
        if valid is not None:
            s = jnp.where(valid, s, NEG_INF)
        sk = sink_ref[kh]
        m = jnp.maximum(jnp.max(s, axis=-1, keepdims=True), sk)
        e = jnp.exp2(s - m)
        den = jnp.sum(e, axis=-1, keepdims=True) + jnp.exp2(sk - m)
        p = (e * (1.0 / den)).astype(BF16)
        o = jnp.dot(p, v[:, kh * A_HD:(kh + 1) * A_HD], preferred_element_type=F32)
        for g in range(A_GROUP):
            outs.append(o[g * nq:(g + 1) * nq])
    return jnp.concatenate(outs, axis=1)


def _attn_a_prompt_kernel(qb, nsub, q_ref, kvc_ref, kvp_ref, bias_ref, sink_ref, o_ref, t_ref, p_ref):
    cb = pl.program_id(1)
    back = CHUNK * WIN_CHUNKS
    nk = back + qb
    nkv = A_KV_HEADS * A_HD
    kvfull = jnp.concatenate([kvp_ref[...], kvc_ref[...]], axis=0)
    k_all = kvfull[:, :nkv].astype(BF16)
    vt_all = kvfull[:, nkv:].T.astype(BF16)
    dn = (((1,), (1,)), ((), ()))
    units = [(s, kh) for s in range(nsub) for kh in range(A_KV_HEADS)]
    for s, kh in units:
        qs = jnp.concatenate([q_ref[s * qb:(s + 1) * qb, (kh * A_GROUP + g) * A_HD:(kh * A_GROUP + g + 1) * A_HD]
                              for g in range(A_GROUP)], axis=0)
        t_ref[s * A_KV_HEADS + kh] = lax.dot_general(k_all[s * qb:s * qb + nk, kh * A_HD:(kh + 1) * A_HD], qs, dn,
                                                     preferred_element_type=F32)
    for s in range(nsub):
        outs = []
        for kh in range(A_KV_HEADS):
            u = s * A_KV_HEADS + kh
            st = t_ref[u] + bias_ref[kh]
            if s == 0:
                key_pos = lax.broadcasted_iota(jnp.int32, (nk, 1), 0) + (cb * nsub * qb - back)
                st = st + jnp.where(key_pos >= 0, 0.0, NEG_INF)
            sk = sink_ref[kh]
            m = jnp.maximum(jnp.max(st, axis=0, keepdims=True), sk)
            e = jnp.exp2(st - m)
            inv_den = 1.0 / (jnp.sum(e, axis=0, keepdims=True) + jnp.exp2(sk - m))
            p_ref[u] = e.astype(BF16)
            ot = jnp.dot(vt_all[kh * A_HD:(kh + 1) * A_HD, s * qb:s * qb + nk], p_ref[u],
                         preferred_element_type=F32)
            o = (ot * inv_den).T
            for g in range(A_GROUP):
                outs.append(o[g * qb:(g + 1) * qb])
        o_ref[s * qb:(s + 1) * qb, :] = jnp.concatenate(outs, axis=1).astype(o_ref.dtype)


def _a_prompt_tables(qb, sinks):
    back = CHUNK * WIN_CHUNKS
    slopes = _alibi(A_HEADS).reshape(A_KV_HEADS, A_GROUP)
    qpos = np.arange(qb)
    kpos = np.arange(back + qb) - back
    dist = np.abs(kpos[:, None] - qpos[None, :]).astype(np.float64)
    band_lo = (qpos // CHUNK) * CHUNK - back
    in_band = (kpos[:, None] >= band_lo[None, :]) & (kpos[:, None] < band_lo[None, :] + back + CHUNK)
    bias = np.where(in_band[None, None], -slopes[:, :, None, None] * dist[None, None], -np.inf)
    bias = np.transpose(bias, (0, 2, 1, 3)).reshape(A_KV_HEADS, back + qb, A_GROUP * qb) * LOG2E
    sk = jnp.repeat(sinks.astype(F32).reshape(A_KV_HEADS, A_GROUP), qb, axis=1)[:, None, :] * LOG2E
    return jnp.asarray(bias, F32), sk


def _a_tables(qpos, kpos, sinks):
    slopes = _alibi(A_HEADS).reshape(A_KV_HEADS, A_GROUP)
    dist = np.abs(qpos[:, None] - kpos[None, :]).astype(np.float64)
    bias = -slopes[:, :, None, None] * dist
    nq = len(qpos)
    bias = jnp.asarray(bias.reshape(A_KV_HEADS, A_GROUP * nq, len(kpos)) * LOG2E, F32)
    sk = jnp.repeat(sinks.astype(F32).reshape(A_KV_HEADS, A_GROUP), nq, axis=1)[..., None] * LOG2E
    return bias, sk


def attn_a_prompt(q_all, kv_all, sinks, batch, t, qb):
    nq = A_HEADS * A_HD
    nkv2 = 2 * A_KV_HEADS * A_HD
    back = CHUNK * WIN_CHUNKS
    bias, sk = _a_prompt_tables(qb, sinks)
    nsub = max(s for s in (1, 2, 4, 8) if t % (s * qb) == 0)
    rows = nsub * qb
    nblk = t // rows
    r = rows // back
    return pl.pallas_call(
        functools.partial(_attn_a_prompt_kernel, qb, nsub),
        grid=(batch, nblk),
        in_specs=[
            pl.BlockSpec((rows, nq), lambda b, i: (b * nblk + i, 0)),
            pl.BlockSpec((rows, nkv2), lambda b, i: (b * nblk + i, 0)),
            pl.BlockSpec((back, nkv2), lambda b, i: (jnp.maximum((b * nblk + i) * r - 1, 0), 0)),
            pl.BlockSpec(bias.shape, lambda b, i: (0, 0, 0)),
            pl.BlockSpec(sk.shape, lambda b, i: (0, 0, 0)),
        ],
        out_specs=pl.BlockSpec((rows, nq), lambda b, i: (b * nblk + i, 0)),
        out_shape=jax.ShapeDtypeStruct((batch * t, nq), BF16),
        scratch_shapes=[pltpu.VMEM((nsub * A_KV_HEADS, back + qb, A_GROUP * qb), F32),
                        pltpu.VMEM((nsub * A_KV_HEADS, back + qb, A_GROUP * qb), BF16)],
        compiler_params=_cparams(("arbitrary", "arbitrary")), name="attn_a_prompt",
    )(q_all, kv_all, kv_all, bias, sk)


def _attn_a_sample_kernel(q_ref, k_ref, v_ref, bias_ref, sink_ref, o_ref):
    o = _attn_a_chunk(q_ref[...], k_ref[...].astype(BF16), v_ref[...].astype(BF16), bias_ref, sink_ref, None)
    o_ref[...] = o.astype(o_ref.dtype)


def attn_a_sample(q_all, kband, vband, sinks, row0, dec_batch, dec_t):
    nq = A_HEADS * A_HD
    s = kband.shape[1]
    past = s - dec_t
    kpos = np.concatenate([np.arange(past) - past, np.arange(dec_t)])
    bias, sk = _a_tables(np.arange(dec_t), kpos, sinks)
    blk0 = row0 // dec_t
    return pl.pallas_call(
        _attn_a_sample_kernel,
        grid=(dec_batch,),
        in_specs=[
            pl.BlockSpec((dec_t, nq), lambda b: (blk0 + b, 0)),
            pl.BlockSpec((None, s, kband.shape[2]), lambda b: (b, 0, 0)),
            pl.BlockSpec((None, s, vband.shape[2]), lambda b: (b, 0, 0)),
            pl.BlockSpec(bias.shape, lambda b: (0, 0, 0)),
            pl.BlockSpec(sk.shape, lambda b: (0, 0, 0)),
        ],
        out_specs=pl.BlockSpec((dec_t, nq), lambda b: (b, 0)),
        out_shape=jax.ShapeDtypeStruct((dec_batch * dec_t, nq), BF16),
        compiler_params=_cparams(("arbitrary",)), name="attn_a_sample",
    )(q_all, kband, vband, bias, sk)


B_Q_SCALE = B_HD ** -0.5 * float(np.log2(np.e))
LOG2E = float(np.log2(np.e))


def _stack_maps(qh):
    lane = lax.broadcasted_iota(jnp.int32, qh.shape, 1)
    zero = jnp.zeros_like(qh)
    return jnp.concatenate([jnp.where(lane < B_HD, qh, zero), jnp.where(lane >= B_HD, qh, zero)], axis=0)


def _diff_finish(acc, l, nq, lam, lam_init, subln):
    inv = 1.0 / l
    o = acc[:nq] * inv[:nq] - lam * (acc[nq:] * inv[nq:])
    ms = jnp.mean(o * o, axis=-1, keepdims=True)
    return o * lax.rsqrt(ms + SUBLN_EPS) * subln * (1.0 - lam_init)


def _attn_b_prompt_kernel(qb, nb, lam_init, pair_ref, slope_ref, lam_ref, q_ref, k_ref, vt_ref, kbias_ref,
                          subln_ref, o_ref, acc_ref, m_ref, al_ref, qq_ref, t0_ref, t1_ref, p0_ref, p1_ref):
    g = pl.program_id(1)
    lam = lam_ref[0]
    dn = (((1,), (1,)), ((), ()))
    t_refs = (t0_ref, t1_ref)
    p_refs = (p0_ref, p1_ref)
    heads = range(HEAD_GROUP)
    n_pairs = nb * (nb + 1) // 2
    slopes = [slope_ref[g * HEAD_GROUP + u] for u in heads]

    def pair(n):
        return pair_ref[2 * n], pair_ref[2 * n + 1]

    def scores(n, slot):
        j, iq = pair(n)
        ks = pl.multiple_of(j * qb, qb)
        for u in heads:
            t_refs[slot][u] = lax.dot_general(k_ref[u, pl.ds(ks, qb), :], qq_ref[u * nb + iq], dn,
                                              preferred_element_type=F32)

    def softmax_step(n, slot):
        j, iq = pair(n)
        on_diag = (j == iq).astype(jnp.int32)
        for u in heads:
            st = u * nb + iq
            m = m_ref[st, 0:1, :]
            t = t_refs[slot][u] + kbias_ref[u, on_diag]
            off = slopes[u] * jnp.full((1, 2 * qb), (j - iq) * qb, jnp.int32).astype(F32)
            m_new = jnp.maximum(m, jnp.max(t, axis=0, keepdims=True) + off)
            p_refs[slot][u] = jnp.exp2(t + (off - m_new)).astype(BF16)
            al_ref[slot * HEAD_GROUP + u, 0:1, :] = jnp.exp2(m - m_new)
            m_ref[st, 0:1, :] = m_new

    def accumulate(n, slot):
        j, iq = pair(jnp.maximum(n, 0))
        ks = pl.multiple_of(j * qb, qb)
        for u in heads:
            st = u * nb + iq
            pv = jnp.dot(vt_ref[u, :, pl.ds(ks, qb)], p_refs[slot][u], preferred_element_type=F32)
            acc_ref[st] = al_ref[slot * HEAD_GROUP + u, 0:1, :] * acc_ref[st] + pv

    def step(n, slot):
        softmax_step(n, slot)
        accumulate(n - 1, 1 - slot)
        scores(jnp.minimum(n + 1, n_pairs - 1), 1 - slot)
        return 0

    acc_ref[...] = jnp.zeros_like(acc_ref)
    m_ref[...] = jnp.full(m_ref.shape, NEG_INF, F32)
    al_ref[...] = jnp.ones_like(al_ref)
    p1_ref[...] = jnp.zeros_like(p1_ref)
    for u in heads:
        for iq in range(nb):
            qq_ref[u * nb + iq] = _stack_maps(q_ref[u, iq * qb:(iq + 1) * qb, :])
    scores(0, 0)
    def steps(k, c):
        for r in range(PAIR_UNROLL):
            step(PAIR_UNROLL * k + r, r % 2)
        return c

    lax.fori_loop(0, n_pairs // PAIR_UNROLL, steps, 0)
    for n in range(n_pairs - n_pairs % PAIR_UNROLL, n_pairs):
        step(n, n % 2)
    accumulate(n_pairs - 1, (n_pairs - 1) % 2)
    for u in heads:
        for iq in range(nb):
            inv = 1.0 / acc_ref[u * nb + iq, LANES:LANES + 1, :]
            acc = acc_ref[u * nb + iq, :LANES, :]
            ot = acc[:, :qb] * inv[:, :qb] - lam * (acc[:, qb:] * inv[:, qb:])
            ms = jnp.mean(ot * ot, axis=0, keepdims=True)
            ot = ot * lax.rsqrt(ms + SUBLN_EPS) * subln_ref[...] * (1.0 - lam_init)
            o_ref[u, iq * qb:(iq + 1) * qb, :] = ot.T.astype(o_ref.dtype)


def attn_b_prompt(qkvh, vt, lam, subln, lam_init, batch, t, qb):
    nblk = t // qb
    qq = np.arange(qb)
    slope2 = (_alibi(B_HEADS) * LOG2E)[:, None, None]
    kcol = np.broadcast_to(qq[:, None], (qb, qb)).astype(np.float64)
    tri = -2.0 * np.maximum(qq[:, None] - qq[None, :], 0)
    msk = np.where((qq[:, None] // CHUNK) <= (qq[None, :] // CHUNK), 0.0, -np.inf)
    both = lambda a: np.concatenate([a, a], axis=-1)
    kbias = np.stack([both(slope2 * kcol[None]), both(slope2 * (kcol + tri)[None] + msk[None])], axis=1)
    kbias = jnp.asarray(kbias, F32)
    slopes = jnp.asarray(_alibi(B_HEADS) * LOG2E, F32)
    subln_t = jnp.broadcast_to(subln.astype(F32).reshape(LANES, 1), (LANES, qb))
    pairs = jnp.asarray([v for j in range(nblk) for i in range(j, nblk) for v in (j, i)], jnp.int32)
    hg = HEAD_GROUP
    n_groups = B_HEADS // hg
    grid_spec = pltpu.PrefetchScalarGridSpec(
        num_scalar_prefetch=3, grid=(batch, n_groups),
        in_specs=[
            pl.BlockSpec((hg, t, LANES), lambda b, g, *_: (g, b, 0)),
            pl.BlockSpec((hg, t, LANES), lambda b, g, *_: (n_groups + g, b, 0)),
            pl.BlockSpec((hg, LANES + ONES_ROWS, t), lambda b, g, *_: (g, 0, b)),
            pl.BlockSpec((hg,) + kbias.shape[1:], lambda b, g, *_: (g, 0, 0, 0)),
            pl.BlockSpec(subln_t.shape, lambda b, g, *_: (0, 0)),
        ],
        out_specs=pl.BlockSpec((hg, t, LANES), lambda b, g, *_: (g, b, 0)),
        scratch_shapes=[pltpu.VMEM((hg * nblk, LANES + ONES_ROWS, 2 * qb), F32),
                        pltpu.VMEM((hg * nblk, SUBLANES, 2 * qb), F32),
                        pltpu.VMEM((2 * hg, SUBLANES, 2 * qb), F32),
                        pltpu.VMEM((hg * nblk, 2 * qb, LANES), BF16),
                        pltpu.VMEM((hg, qb, 2 * qb), F32), pltpu.VMEM((hg, qb, 2 * qb), F32),
                        pltpu.VMEM((hg, qb, 2 * qb), BF16), pltpu.VMEM((hg, qb, 2 * qb), BF16)],
    )
    return pl.pallas_call(
        functools.partial(_attn_b_prompt_kernel, qb, nblk, lam_init), grid_spec=grid_spec,
        out_shape=jax.ShapeDtypeStruct((B_HEADS, batch * t, LANES), BF16),
        compiler_params=_cparams(("arbitrary", "arbitrary")), name="attn_b_prompt",
    )(pairs, slopes, lam.reshape(1).astype(F32), qkvh, qkvh, vt, kbias, subln_t)


def _attn_b_sample_kernel(dec_t, past, lam_init, slopes, lam_ref, q_ref, kn_ref, vn_ref, ck_ref, cv_ref,
                          tnew_ref, subln_ref, o_ref):
    lam = lam_ref[0]
    subln = subln_ref[...]
    colc = lax.broadcasted_iota(jnp.int32, (1, past), 1).astype(F32) - float(past)
    for h in range(B_HEADS):
        slope = float(slopes[h]) * LOG2E
        qq = _stack_maps(q_ref[h])
        kc = ck_ref[pl.ds(h, past, stride=B_HEADS), :].astype(BF16)
        vc = cv_ref[pl.ds(h, past, stride=B_HEADS), :].astype(BF16)
        dn = (((1,), (1,)), ((), ()))
        s_c = lax.dot_general(qq, kc, dn, preferred_element_type=F32) + slope * colc
        s_n = lax.dot_general(qq, kn_ref[h], dn, preferred_element_type=F32) + slope * tnew_ref[...]
        m = jnp.maximum(jnp.max(s_c, axis=-1, keepdims=True), jnp.max(s_n, axis=-1, keepdims=True))
        p_c = jnp.exp2(s_c - m)
        p_n = jnp.exp2(s_n - m)
        l = jnp.sum(p_c, axis=-1, keepdims=True) + jnp.sum(p_n, axis=-1, keepdims=True)
        acc = (jnp.dot(p_c.astype(BF16), vc, preferred_element_type=F32)
               + jnp.dot(p_n.astype(BF16), vn_ref[h], preferred_element_type=F32))
        o_ref[h] = _diff_finish(acc, l, dec_t, lam, lam_init, subln).astype(o_ref.dtype)


def attn_b_sample(qkvh, cache_k, cache_v, lam, subln, lam_init, row0, dec_batch, dec_t):
    past = cache_k.shape[1]
    qi = np.arange(dec_t)
    tnew = (qi[:, None] - np.abs(qi[:, None] - qi[None, :])).astype(np.float32)
    tnew = jnp.asarray(np.concatenate([tnew, tnew], axis=0))
    blk0 = row0 // dec_t
    rows = past * B_HEADS
    cache_k = cache_k.reshape(dec_batch, rows, LANES)
    cache_v = cache_v.reshape(dec_batch, rows, LANES)
    grid_spec = pltpu.PrefetchScalarGridSpec(
        num_scalar_prefetch=1, grid=(dec_batch,),
        in_specs=[
            pl.BlockSpec((B_HEADS, dec_t, LANES), lambda b, *_: (0, blk0 + b, 0)),
            pl.BlockSpec((B_HEADS, dec_t, LANES), lambda b, *_: (1, blk0 + b, 0)),
            pl.BlockSpec((B_HEADS, dec_t, LANES), lambda b, *_: (2, blk0 + b, 0)),
            pl.BlockSpec((None, rows, LANES), lambda b, *_: (b, 0, 0)),
            pl.BlockSpec((None, rows, LANES), lambda b, *_: (b, 0, 0)),
            pl.BlockSpec(tnew.shape, lambda b, *_: (0, 0)),
            pl.BlockSpec((1, LANES), lambda b, *_: (0, 0)),
        ],
        out_specs=pl.BlockSpec((B_HEADS, dec_t, LANES), lambda b, *_: (0, b, 0)),
    )
    return pl.pallas_call(
        functools.partial(_attn_b_sample_kernel, dec_t, past, lam_init, tuple(_alibi(B_HEADS))),
        grid_spec=grid_spec,
        out_shape=jax.ShapeDtypeStruct((B_HEADS, dec_batch * dec_t, LANES), BF16),
        compiler_params=_cparams(("arbitrary",)), name="attn_b_sample",
    )(lam.reshape(1).astype(F32), qkvh, qkvh, qkvh, cache_k, cache_v, tnew, subln.reshape(1, LANES).astype(F32))


ROUTE_ROWS = 40
ROUTE_FIELDS = 8


def _route_t(lt):
    row = lax.broadcasted_iota(jnp.int32, lt.shape, 0).astype(F32)
    big = float(LANES)
    lg = jnp.where(row < N_GROUPS, lt, NEG_INF)
    gmax = jnp.max(lg, axis=0, keepdims=True)
    g_idx = jnp.min(jnp.where(lg == gmax, row, big), axis=0, keepdims=True)
    g_prob = 1.0 / jnp.sum(jnp.exp(lg - gmax), axis=0, keepdims=True)
    lo = N_GROUPS + EXPERTS_PER_GROUP * g_idx
    le = jnp.where((row >= lo) & (row < lo + EXPERTS_PER_GROUP), lt, NEG_INF)
    m1 = jnp.max(le, axis=0, keepdims=True)
    i1 = jnp.min(jnp.where(le == m1, row, big), axis=0, keepdims=True)
    le2 = jnp.where(row == i1, NEG_INF, le)
    m2 = jnp.max(le2, axis=0, keepdims=True)
    i2 = jnp.min(jnp.where(le2 == m2, row, big), axis=0, keepdims=True)
    t = jnp.exp(m2 - m1)
    ga = 1.0 / (1.0 + t)
    gb = t * ga
    return row, i1, i2, g_prob * ga, g_prob * gb


def _unpack_pairs(u):
    lo = pltpu.bitcast(u << 16, F32)
    hi = pltpu.bitcast(u & jnp.uint32(0xFFFF0000), F32)
    return jnp.concatenate([lo, hi], axis=1)


def _pack_pairs(x):
    bits = pltpu.bitcast(x.astype(BF16).astype(F32), jnp.uint32)
    w = x.shape[1] // 2
    return (bits[:, :w] >> 16) | bits[:, w:]


def _out_router_kernel(n_prompt_blocks, heads_in, two_o, two_res, *refs):
    refs = list(refs)
    oa_ref = refs.pop(0)
    ob_ref = refs.pop(0) if two_o else None
    ra_ref = refs.pop(0)
    rb_ref = refs.pop(0) if two_res else None
    (wo_ref, g_ref, wr_ref, wr2_ref, br_ref, utri_ref, h_ref, xn_ref, meta_ref, route_ref, cnt_ref,
     carry_ref) = refs
    is_prompt = pl.program_id(0) < n_prompt_blocks

    def load_o(ref):
        if heads_in:
            return jnp.concatenate([ref[hh] for hh in range(ref.shape[0])], axis=1)
        return ref[...]

    o = load_o(oa_ref)
    if two_o:
        o = jnp.where(is_prompt, o, load_o(ob_ref))
    res = ra_ref[...]
    if two_res:
        res = jnp.where(is_prompt, res, rb_ref[...])
    h = res + jnp.dot(o, wo_ref[...], preferred_element_type=F32)
    h_ref[...] = h
    ms = jnp.mean(h * h, axis=-1, keepdims=True)
    xn = h * lax.rsqrt(ms + NORM_EPS) * g_ref[...]
    x_hi = xn.astype(BF16)
    _store_planes(xn_ref, _pack_pairs(xn))
    x_lo = (xn - x_hi.astype(F32)).astype(BF16)
    r = jnp.dot(x_hi, wr_ref[...], preferred_element_type=F32)
    logits = (r[:, :LANES] + r[:, LANES:]) + jnp.dot(x_lo, wr2_ref[...], preferred_element_type=F32) + br_ref[...]
    row, i1, i2, g1, g2 = _route_t(logits.T[:ROUTE_ROWS])

    @pl.when(pl.program_id(0) == 0)
    def _():
        carry_ref[...] = jnp.zeros_like(carry_ref)

    sel1 = row == i1
    sel2 = row == i2
    onehot = jnp.where(sel1 | sel2, 1.0, 0.0)
    before = jnp.dot(onehot.astype(BF16), utri_ref[...], preferred_element_type=F32) + carry_ref[:, 0:1]
    rank1 = jnp.sum(jnp.where(sel1, before, 0.0), axis=0, keepdims=True)
    rank2 = jnp.sum(jnp.where(sel2, before, 0.0), axis=0, keepdims=True)
    carry_ref[...] = carry_ref[...] + jnp.sum(onehot, axis=1, keepdims=True)
    cnt_ref[...] = carry_ref[...]
    fields = [i1 - N_GROUPS, i2 - N_GROUPS, g1, g2, rank1, rank2]
    fields += [jnp.zeros_like(g1)] * (ROUTE_FIELDS - len(fields))
    route = jnp.concatenate(fields, axis=0)
    route_ref[...] = route
    meta_ref[...] = jnp.concatenate([route, jnp.zeros((LANES - ROUTE_FIELDS, route.shape[1]), F32)], axis=0).T


def out_router(oa, ob, ra, rb, wo, g, w_rg, b_rg, w_re, b_re, heads_in, tr):
    d = ra.shape[1]
    if heads_in:
        na = oa.shape[1]
        nb = 0 if ob is None else ob.shape[1]
    else:
        na = oa.shape[0]
        nb = 0 if ob is None else ob.shape[0]
    n = na + nb
    npb = na // tr
    two_o = ob is not None
    two_res = rb is not None
    first = lambda i: (jnp.minimum(i, npb - 1), 0)
    second = lambda i: (jnp.maximum(i - npb, 0), 0)
    plain = lambda i: (i, 0)
    in_specs, args = [], []

    def add_o(x, imap):
        if heads_in:
            in_specs.append(pl.BlockSpec((x.shape[0], tr, LANES), lambda i: (0, imap(i)[0], 0)))
        else:
            in_specs.append(pl.BlockSpec((tr, x.shape[1]), imap))
        args.append(x)

    add_o(oa, first if two_o else plain)
    if two_o:
        add_o(ob, second)
    in_specs.append(pl.BlockSpec((tr, d), first if two_res else plain))
    args.append(ra)
    if two_res:
        in_specs.append(pl.BlockSpec((tr, d), second))
        args.append(rb)
    wr = jnp.zeros((d, LANES), F32).at[:, :N_GROUPS].set(w_rg.astype(F32))
    wr = wr.at[:, N_GROUPS:N_GROUPS + N_EXPERTS].set(w_re.astype(F32))
    wr_hi = wr.astype(BF16)
    wr_lo = (wr - wr_hi.astype(F32)).astype(BF16)
    br = jnp.zeros((1, LANES), F32).at[0, :N_GROUPS].set(b_rg.astype(F32))
    br = br.at[0, N_GROUPS:N_GROUPS + N_EXPERTS].set(b_re.astype(F32))
    utri = jnp.asarray(np.triu(np.ones((tr, tr), np.float32), 1), BF16)
    consts = [wo, g.reshape(1, d).astype(F32), jnp.concatenate([wr_hi, wr_lo], axis=1), wr_hi, br, utri]
    for cst in consts:
        in_specs.append(pl.BlockSpec(cst.shape, lambda i: (0, 0)))
        args.append(cst)
    parts = d // 2 // SC_ROW_WORDS
    out_shape = [jax.ShapeDtypeStruct((n, d), F32), jax.ShapeDtypeStruct((parts, n, SC_ROW_WORDS), jnp.uint32),
                 jax.ShapeDtypeStruct((n, LANES), F32), jax.ShapeDtypeStruct((ROUTE_FIELDS, n), F32),
                 jax.ShapeDtypeStruct((ROUTE_ROWS, LANES), F32)]
    out_specs = [pl.BlockSpec((tr, d), plain), pl.BlockSpec((parts, tr, SC_ROW_WORDS), lambda i: (0, i, 0)),
                 pl.BlockSpec((tr, LANES), plain), pl.BlockSpec((ROUTE_FIELDS, tr), lambda i: (0, i)),
                 pl.BlockSpec((ROUTE_ROWS, LANES), lambda i: (0, 0))]
    return pl.pallas_call(
        functools.partial(_out_router_kernel, npb, heads_in, two_o, two_res),
        grid=(n // tr,), in_specs=in_specs, out_specs=out_specs, out_shape=out_shape,
        scratch_shapes=[pltpu.VMEM((ROUTE_ROWS, LANES), F32)],
        compiler_params=_cparams(("arbitrary",)), name="out_router")(*args)


def _expert_kernel(tm, tps, te_ref, nt_ref, x_ref, *refs):
    w_refs, o_ref = refs[:3 * tps], refs[3 * tps]
    first = pl.program_id(0) * tps
    n_valid = jnp.clip(nt_ref[0] - first, 0, tps)

    def tile(k):
        wg_ref, wu_ref, wd_ref = w_refs[3 * k:3 * k + 3]
        rows = slice(k * tm, (k + 1) * tm)
        x = _unpack_pairs(jnp.concatenate([x_ref[q, rows, :] for q in range(x_ref.shape[0])], axis=1)).astype(BF16)
        a = jnp.dot(x, wg_ref[...].astype(BF16), preferred_element_type=F32)
        u = jnp.dot(x, wu_ref[...].astype(BF16), preferred_element_type=F32)
        hh = (a * (1.0 / (1.0 + jnp.exp(-a))) * u).astype(BF16)
        packed = _pack_pairs(jnp.dot(hh, wd_ref[...].astype(BF16), preferred_element_type=F32))
        for q in range(o_ref.shape[0]):
            o_ref[q, rows, :] = packed[:, q * o_ref.shape[2]:(q + 1) * o_ref.shape[2]]

    for nv in range(tps + 1):

        @pl.when(n_valid == nv)
        def _(nv=nv):
            for k in range(nv):
                tile(k)
            if nv < tps:
                o_ref[:, nv * tm:, :] = jnp.zeros((o_ref.shape[0], (tps - nv) * tm, o_ref.shape[2]), o_ref.dtype)


def expert_mlp(xs, tile_expert, n_tiles, w_gate, w_up, w_down, layer, tm):
    parts, p, w = xs.shape
    d, f = w_gate.shape[-2:]
    epg = w_gate.shape[2]
    tps = 2 if (p // tm) % 2 == 0 else 1
    w_specs, w_args = [], []
    for k in range(tps):
        wmap = lambda t, te, nt, k=k: (layer, te[tps * t + k] // epg, te[tps * t + k] % epg, 0, 0)
        w_specs += [pl.BlockSpec((None, None, None, d, f), wmap), pl.BlockSpec((None, None, None, d, f), wmap),
                    pl.BlockSpec((None, None, None, f, d), wmap)]
        w_args += [w_gate, w_up, w_down]
    grid_spec = pltpu.PrefetchScalarGridSpec(
        num_scalar_prefetch=2, grid=(p // (tps * tm),),
        in_specs=[pl.BlockSpec((parts, tps * tm, w), lambda t, te, nt: (0, t, 0))] + w_specs,
        out_specs=pl.BlockSpec((parts, tps * tm, w), lambda t, te, nt: (0, t, 0)),
    )
    return pl.pallas_call(
        functools.partial(_expert_kernel, tm, tps), grid_spec=grid_spec,
        out_shape=jax.ShapeDtypeStruct((parts, p, w), jnp.uint32),
        compiler_params=_cparams(("arbitrary",)), name="expert_mlp")(tile_expert, n_tiles, xs, *w_args)


def sc_scatter_rows(x2, idx, n_out):
    mesh = plsc.VectorSubcoreMesh(core_axis_name="c", subcore_axis_name="s")
    n_src_win = x2.shape[0] // SC_WINDOW

    @functools.partial(pl.kernel, out_type=jax.ShapeDtypeStruct((n_out, SC_ROW_WORDS), x2.dtype), mesh=mesh)
    def scatter(x_hbm, i_hbm, o_hbm):
        def body(x_vmem, i_vmem):
            pltpu.sync_copy(x_vmem, o_hbm.at[i_vmem.at[0]])

        pltpu.emit_pipeline(
            body, grid=(n_out // SC_WINDOW,),
            in_specs=[pl.BlockSpec((SC_WINDOW, SC_ROW_WORDS), lambda i: (lax.rem(i, n_src_win), 0)),
                      pl.BlockSpec((1, SC_WINDOW), lambda i: (0, i))],
            out_specs=[], core_axis_name=("c", "s"), dimension_semantics=(pltpu.PARALLEL,),
        )(x_hbm, i_hbm)

    return scatter(x2, idx)


def sc_gather_rows(x2, idx):
    mesh = plsc.VectorSubcoreMesh(core_axis_name="c", subcore_axis_name="s")
    n_col, n = idx.shape

    @functools.partial(pl.kernel, out_type=jax.ShapeDtypeStruct((n, n_col * SC_ROW_WORDS), x2.dtype), mesh=mesh)
    def gather(x_hbm, i_hbm, o_hbm):
        def body(i_vmem, o_vmem):
            pltpu.sync_copy(x_hbm.at[i_vmem.at[0]], o_vmem)

        pltpu.emit_pipeline(
            body, grid=(n // SC_WINDOW, n_col),
            in_specs=[pl.BlockSpec((1, SC_WINDOW), lambda i, c: (c, i))],
            out_specs=[pl.BlockSpec((SC_WINDOW, SC_ROW_WORDS), lambda i, c: (i, c))],
            core_axis_name=("c", "s"), dimension_semantics=(pltpu.PARALLEL, pltpu.ARBITRARY),
        )(i_hbm, o_hbm)

    return gather(x2, idx)


def moe_dispatch(xn_packed, route, counts, tm):
    parts, n, w = xn_packed.shape
    e = route[0:2].astype(jnp.int32)
    rank = route[4:6].astype(jnp.int32)
    counts = counts[N_GROUPS:N_GROUPS + N_EXPERTS, 0].astype(jnp.int32)
    padded = ((counts + tm - 1) // tm) * tm
    pend = jnp.cumsum(padded)
    pstart = pend - padded
    ids = jnp.arange(N_EXPERTS, dtype=jnp.int32)
    pos = jnp.sum(jnp.where(e[..., None] == ids, pstart, 0), axis=-1) + rank
    p = 2 * n + N_EXPERTS * tm
    n_pad = p - 2 * n
    gap_len = jnp.concatenate([padded - counts, (p - pend[-1]).reshape(1)])
    gap_first = jnp.concatenate([pstart + counts, pend[-1:]])
    gap_end = jnp.cumsum(gap_len)
    k = jnp.arange(n_pad, dtype=jnp.int32)
    gap = jnp.sum((k[:, None] >= gap_end[None, :]).astype(jnp.int32), axis=1)
    shift = gap_first - (gap_end - gap_len)
    pad_pos = k + jnp.sum(jnp.where(gap[:, None] == jnp.arange(N_EXPERTS + 1), shift, 0), axis=-1)
    planes = lambda rows: jnp.concatenate([rows + q * p for q in range(parts)])
    scat_idx = jnp.concatenate([planes(pos[0]), planes(pos[1]), planes(pad_pos)]).reshape(1, parts * p)
    gath_idx = jnp.stack([pos[k] + q * p for k in range(2) for q in range(parts)])
    xs = sc_scatter_rows(xn_packed.reshape(parts * n, w), scat_idx, parts * p).reshape(parts, p, w)
    tile_start = jnp.arange(p // tm, dtype=jnp.int32) * tm
    tile_expert = jnp.sum((tile_start[:, None] >= pend[None, :]).astype(jnp.int32), axis=1)
    tile_expert = jnp.minimum(tile_expert, N_EXPERTS - 1)
    n_tiles = (pend[-1] // tm).astype(jnp.int32).reshape(1)
    return xs, tile_expert, n_tiles, gath_idx


def moe_experts(xn_packed, route, counts, w_gate, w_up, w_down, layer, tm):
    parts, n, w = xn_packed.shape
    xs, tile_expert, n_tiles, gath_idx = moe_dispatch(xn_packed, route, counts, tm)
    o = expert_mlp(xs, tile_expert, n_tiles, w_gate, w_up, w_down, layer, tm)
    return sc_gather_rows(o.reshape(parts * o.shape[1], w), gath_idx)


def _final_norm_kernel(n_prompt_blocks, h_ref, o0_ref, o1_ref, meta_ref, g_ref, yp_ref, ys_ref):
    meta = meta_ref[...]
    x = (h_ref[...] + meta[:, 2:3] * _unpack_pairs(o0_ref[...])) + meta[:, 3:4] * _unpack_pairs(o1_ref[...])
    ms = jnp.mean(x * x, axis=-1, keepdims=True)
    y = x * lax.rsqrt(ms + NORM_EPS) * g_ref[...]
    i = pl.program_id(0)

    @pl.when(i < n_prompt_blocks)
    def _():
        yp_ref[...] = y

    @pl.when(i >= n_prompt_blocks)
    def _():
        ys_ref[...] = y


def final_norm(h, og, meta, g, n_prompt, tr):
    n, d = h.shape
    npb = n_prompt // tr
    return pl.pallas_call(
        functools.partial(_final_norm_kernel, npb),
        grid=(n // tr,),
        in_specs=[pl.BlockSpec((tr, d), lambda i: (i, 0)),
                  pl.BlockSpec((tr, og.shape[1] // 2), lambda i: (i, 0)),
                  pl.BlockSpec((tr, og.shape[1] // 2), lambda i: (i, 1)),
                  pl.BlockSpec((tr, LANES), lambda i: (i, 0)), pl.BlockSpec((1, d), lambda i: (0, 0))],
        out_specs=[pl.BlockSpec((tr, d), lambda i: (jnp.minimum(i, npb - 1), 0)),
                   pl.BlockSpec((tr, d), lambda i: (jnp.maximum(i - npb, 0), 0))],
        out_shape=[jax.ShapeDtypeStruct((n_prompt, d), F32), jax.ShapeDtypeStruct((n - n_prompt, d), F32)],
        compiler_params=_cparams(("arbitrary",)), name="final_norm")(h, og, og, meta, g.reshape(1, d).astype(F32))


def _pick_rows_tile(n_prompt, n_sample):
    tr = ROW_TILE
    while tr >= SUBLANES:
        if n_prompt % tr == 0 and n_sample % tr == 0:
            return tr
        tr //= 2
    raise ValueError("row counts must be multiples of the sublane count")


def kernel(x_prompt, x_sample, cache_a_k, cache_a_v, cache_b_k, cache_b_v, norm_mix, norm_ffn, norm_final,
           w_a_qkv, b_a_qkv, a_sinks, w_a_o, w_b_qkv, b_lambda, b_subln, w_b_o,
           w_route_group, b_route_group, w_route_expert, b_route_expert, w_gate, w_up, w_down):
    batch, t, d = x_prompt.shape
    dec_batch, dec_t, _ = x_sample.shape
    n_p, n_s = batch * t, dec_batch * dec_t
    tr = _pick_rows_tile(n_p, n_s)
    tm = min(EXPERT_TILE, 4 * tr)
    xp = x_prompt.reshape(n_p, d)
    xs = x_sample.reshape(n_s, d)
    nq_a = A_HEADS * A_HD
    nkv_a = A_KV_HEADS * A_HD

    a_scale = jnp.where(jnp.arange(nq_a + 2 * nkv_a) < nq_a, A_HD ** -0.5 * LOG2E, 1.0).astype(F32)
    q_a, kv_ap, kv_as = norm_proj(xp, xs, norm_mix[0], (w_a_qkv[0] * a_scale).astype(BF16), b_a_qkv[0] * a_scale,
                                  [(0, nq_a, "flat", BF16), (nq_a, nq_a + 2 * nkv_a, "split", F32)], tr, n_p)
    qb_a = min(CHUNK * WIN_CHUNKS, t)
    o_ap = attn_a_prompt(q_a, kv_ap, a_sinks[0], batch, t, qb_a)
    kv_s = kv_as.reshape(dec_batch, dec_t, 2 * nkv_a)
    past_a = cache_a_k.shape[2]
    kband = jnp.concatenate([cache_a_k[0].reshape(dec_batch, past_a, nkv_a), kv_s[..., :nkv_a]], axis=1)
    vband = jnp.concatenate([cache_a_v[0].reshape(dec_batch, past_a, nkv_a), kv_s[..., nkv_a:]], axis=1)
    o_as = attn_a_sample(q_a, kband, vband, a_sinks[0], n_p, dec_batch, dec_t)
    h, xn, meta, route, counts = out_router(o_ap, o_as, xp, xs, w_a_o[0].astype(BF16), norm_ffn[0], w_route_group[0],
                                            b_route_group[0], w_route_expert[0], b_route_expert[0], False, tr)
    og = moe_experts(xn, route, counts, w_gate, w_up, w_down, 0, tm)

    keep = min(CHUNK * WIN_CHUNKS, t)
    kv_p = kv_ap.reshape(batch, t, 2 * nkv_a)[:, t - keep:]
    a_k_prompt = kv_p[..., :nkv_a].reshape(1, batch, keep, A_KV_HEADS, A_HD)
    a_v_prompt = kv_p[..., nkv_a:].reshape(1, batch, keep, A_KV_HEADS, A_HD)
    a_k_sample = kband[:, dec_t:].reshape(1, dec_batch, past_a, A_KV_HEADS, A_HD)
    a_v_sample = vband[:, dec_t:].reshape(1, dec_batch, past_a, A_KV_HEADS, A_HD)

    lam_init = 0.8 - 0.6 * float(np.exp(-0.3 * 1))
    lp = b_lambda[0].astype(F32)
    lam = jnp.exp(jnp.sum(lp[0] * lp[1])) - jnp.exp(jnp.sum(lp[2] * lp[3])) + lam_init
    wb = B_HEADS * 2 * B_HD
    col_scale = jnp.where(jnp.arange(3 * wb) < wb, B_Q_SCALE, 1.0).astype(F32)
    h, qkvh, k_bp, k_bs, v_bp, v_bs, vt = norm_proj(
        h, None, norm_mix[1], (w_b_qkv[0] * col_scale).astype(BF16), None,
        [(0, 3 * wb, "heads", BF16), (wb, 2 * wb, "split", F32), (2 * wb, 3 * wb, "split", F32),
         (2 * wb, 3 * wb, "heads_t", BF16)], tr, n_p, og=og, meta=meta)
    qb_b = min(ATTN_BLOCK, t)
    o_bp = attn_b_prompt(qkvh, vt, lam, b_subln[0], lam_init, batch, t, qb_b)
    o_bs = attn_b_sample(qkvh, cache_b_k[0], cache_b_v[0], lam, b_subln[0], lam_init, n_p, dec_batch, dec_t)
    hd2 = 2 * B_HD
    o_bs, k_bp, v_bp = lax.optimization_barrier((o_bs, k_bp, v_bp))
    b_k_prompt = k_bp.reshape(1, batch, t, B_HEADS, hd2)
    b_v_prompt = v_bp.reshape(1, batch, t, B_HEADS, hd2)
    o_bp, b_k_prompt, b_v_prompt = lax.optimization_barrier((o_bp, b_k_prompt, b_v_prompt))
    h, xn, meta, route, counts = out_router(o_bp, o_bs, h, None, w_b_o[0].astype(BF16), norm_ffn[1], w_route_group[1],
                                            b_route_group[1], w_route_expert[1], b_route_expert[1], True, tr)
    og = moe_experts(xn, route, counts, w_gate, w_up, w_down, 1, tm)

    y_p, y_s = final_norm(h, og, meta, norm_final, n_p, tr)
    return (y_p.reshape(batch, t, d), y_s.reshape(dec_batch, dec_t, d),
            a_k_prompt, a_v_prompt, a_k_sample, a_v_sample, b_k_prompt, b_v_prompt,
            k_bs.reshape(1, dec_batch, dec_t, B_HEADS, hd2), v_bs.reshape(1, dec_batch, dec_t, B_HEADS, hd2))
```

```python
import functools

import jax
import jax.numpy as jnp
import numpy as np
from jax import lax
from jax.experimental import pallas as pl
from jax.experimental.pallas import tpu as pltpu
from jax.experimental.pallas import tpu_sc as plsc

F32 = jnp.float32
BF16 = jnp.bfloat16
NEG_INF = float("-inf")

CHUNK = 64
WIN_CHUNKS = 2
A_HEADS, A_KV_HEADS, A_GROUP, A_HD = 16, 4, 4, 64
B_HEADS, B_HD = 8, 64
N_GROUPS, EXPERTS_PER_GROUP = 4, 8
N_EXPERTS = N_GROUPS * EXPERTS_PER_GROUP
NORM_EPS = 1e-6
SUBLN_EPS = 1e-5

LANES = 128
SUBLANES = 8
ROW_TILE = 512
EXPERT_TILE = 512
ATTN_BLOCK = 256
COL_CHUNK = 512
PAIR_UNROLL = 12
HEAD_GROUP = 2
ONES_ROWS = 16
SC_WINDOW = 128
SC_ROW_WORDS = 256
VMEM_LIMIT = 48 * 1024 * 1024


def _cparams(sem, flags=None):
    return pltpu.CompilerParams(dimension_semantics=sem, vmem_limit_bytes=VMEM_LIMIT, flags=flags)


def _alibi(n):
    return 2.0 ** (-8.0 * np.arange(1, n + 1) / n)


def _store_planes(ref, x):
    w = ref.shape[2]
    for p in range(ref.shape[0]):
        ref[p] = x[:, p * w:(p + 1) * w]


def _add_expert_outputs(x, og_ref, meta_ref):
    half = og_ref.shape[1] // 2
    meta = meta_ref[...]
    return (x + meta[:, 2:3] * _unpack_pairs(og_ref[:, :half])) + meta[:, 3:4] * _unpack_pairs(og_ref[:, half:])


def _norm_proj_kernel(n_prompt_blocks, two_src, has_bias, combine, outs, *refs):
    refs = list(refs)
    xa_ref = refs.pop(0)
    xb_ref = refs.pop(0) if two_src else None
    og_ref, meta_ref = (refs.pop(0), refs.pop(0)) if combine else (None, None)
    g_ref = refs.pop(0)
    w_ref = refs.pop(0)
    b_ref = refs.pop(0) if has_bias else None
    out_refs = refs
    x = xa_ref[...]
    if two_src:
        x = jnp.where(pl.program_id(0) < n_prompt_blocks, x, xb_ref[...])
    if combine:
        x = _add_expert_outputs(x, og_ref, meta_ref)
        out_refs.pop(0)[...] = x
    ms = jnp.mean(x * x, axis=-1, keepdims=True)
    xn = (x * lax.rsqrt(ms + NORM_EPS) * g_ref[...]).astype(BF16)
    is_prompt = pl.program_id(0) < n_prompt_blocks
    out_refs = list(out_refs)
    targets = []
    for c0, c1, layout in outs:
        targets.append((c0, c1, layout, out_refs.pop(0), out_refs.pop(0) if layout == "split" else None))
    for s in range(0, w_ref.shape[1], COL_CHUNK):
        e = s + COL_CHUNK
        r = jnp.dot(xn, w_ref[:, s:e], preferred_element_type=F32)
        if has_bias:
            r = r + b_ref[:, s:e]
        for c0, c1, layout, o_ref, o2_ref in targets:
            if not (c0 <= s and e <= c1):
                continue
            rr = r.astype(o_ref.dtype)
            if layout == "flat":
                o_ref[:, s - c0:e - c0] = rr
            elif layout == "split":

                @pl.when(is_prompt)
                def _(rr=rr, o_ref=o_ref, s=s, e=e, c0=c0):
                    o_ref[:, s - c0:e - c0] = rr

                @pl.when(jnp.logical_not(is_prompt))
                def _(rr=rr, o2_ref=o2_ref, s=s, e=e, c0=c0):
                    o2_ref[:, s - c0:e - c0] = rr
            elif layout == "heads":
                for hh in range((e - s) // LANES):
                    o_ref[(s - c0) // LANES + hh] = rr[:, hh * LANES:(hh + 1) * LANES]
            else:
                ones = jnp.ones((ONES_ROWS, r.shape[0]), o_ref.dtype)
                for hh in range((e - s) // LANES):
                    head = (s - c0) // LANES + hh
                    o_ref[head, :LANES, :] = r[:, hh * LANES:(hh + 1) * LANES].T.astype(o_ref.dtype)
                    o_ref[head, LANES:, :] = ones


def norm_proj(xa, xb, g, w, b, outs, tr, n_prompt, og=None, meta=None):
    na, d = xa.shape
    nb = 0 if xb is None else xb.shape[0]
    n = na + nb
    npb = n_prompt // tr
    two = xb is not None
    combine = og is not None
    in_specs = [pl.BlockSpec((tr, d), (lambda i: (jnp.minimum(i, npb - 1), 0)) if two else (lambda i: (i, 0)))]
    args = [xa]
    if two:
        in_specs.append(pl.BlockSpec((tr, d), lambda i: (jnp.maximum(i - npb, 0), 0)))
        args.append(xb)
    if combine:
        in_specs += [pl.BlockSpec((tr, og.shape[1]), lambda i: (i, 0)), pl.BlockSpec((tr, LANES), lambda i: (i, 0))]
        args += [og, meta]
    in_specs.append(pl.BlockSpec((1, d), lambda i: (0, 0)))
    args.append(g.reshape(1, d).astype(F32))
    in_specs.append(pl.BlockSpec(w.shape, lambda i: (0, 0)))
    args.append(w)
    if b is not None:
        in_specs.append(pl.BlockSpec((1, w.shape[1]), lambda i: (0, 0)))
        args.append(b.reshape(1, -1).astype(F32))
    out_shapes, out_specs = [], []
    if combine:
        out_shapes.append(jax.ShapeDtypeStruct((n, d), F32))
        out_specs.append(pl.BlockSpec((tr, d), lambda i: (i, 0)))
    for c0, c1, layout, dt in outs:
        assert c0 % COL_CHUNK == 0 and c1 % COL_CHUNK == 0
        if layout == "flat":
            out_shapes.append(jax.ShapeDtypeStruct((n, c1 - c0), dt))
            out_specs.append(pl.BlockSpec((tr, c1 - c0), lambda i: (i, 0)))
        elif layout == "split":
            out_shapes.append(jax.ShapeDtypeStruct((n_prompt, c1 - c0), dt))
            out_specs.append(pl.BlockSpec((tr, c1 - c0), lambda i: (jnp.minimum(i, npb - 1), 0)))
            out_shapes.append(jax.ShapeDtypeStruct((n - n_prompt, c1 - c0), dt))
            out_specs.append(pl.BlockSpec((tr, c1 - c0), lambda i: (jnp.maximum(i - npb, 0), 0)))
        elif layout == "heads":
            nh = (c1 - c0) // LANES
            out_shapes.append(jax.ShapeDtypeStruct((nh, n, LANES), dt))
            out_specs.append(pl.BlockSpec((nh, tr, LANES), lambda i: (0, i, 0)))
        else:
            assert layout == "heads_t"
            nh = (c1 - c0) // LANES
            out_shapes.append(jax.ShapeDtypeStruct((nh, LANES + ONES_ROWS, n), dt))
            out_specs.append(pl.BlockSpec((nh, LANES + ONES_ROWS, tr), lambda i: (0, 0, i)))
    kern = functools.partial(_norm_proj_kernel, npb, two, b is not None, combine,
                             [(c0, c1, lay) for c0, c1, lay, _ in outs])
    return pl.pallas_call(
        kern, grid=(n // tr,), in_specs=in_specs, out_specs=out_specs, out_shape=out_shapes,
        compiler_params=_cparams(("arbitrary",)), name="norm_proj")(*args)


def _attn_a_chunk(q, k, v, bias_ref, sink_ref, valid):
    nq = q.shape[0]
    outs = []
    for kh in range(A_KV_HEADS):
        qs = jnp.concatenate(
            [q[:, (kh * A_GROUP + g) * A_HD:(kh * A_GROUP + g + 1) * A_HD] for g in range(A_GROUP)], axis=0)
        kk = k[:, kh * A_HD:(kh + 1) * A_HD]
        s = lax.dot_general(qs, kk, (((1,), (1,)), ((), ())), preferred_element_type=F32) + bias_ref[kh]
        if valid is not None:
            s = jnp.where(valid, s, NEG_INF)
        sk = sink_ref[kh]
        m = jnp.maximum(jnp.max(s, axis=-1, keepdims=True), sk)
        e = jnp.exp2(s - m)
        den = jnp.sum(e, axis=-1, keepdims=True) + jnp.exp2(sk - m)
        p = (e * (1.0 / den)).astype(BF16)
        o = jnp.dot(p, v[:, kh * A_HD:(kh + 1) * A_HD], preferred_element_type=F32)
        for g in range(A_GROUP):
            outs.append(o[g * nq:(g + 1) * nq])
    return jnp.concatenate(outs, axis=1)


def _attn_a_prompt_kernel(qb, nsub, q_ref, kvc_ref, kvp_ref, bias_ref, sink_ref, o_ref, t_ref, p_ref):
    cb = pl.program_id(1)
    back = CHUNK * WIN_CHUNKS
    nk = back + qb
    nkv = A_KV_HEADS * A_HD
    kvfull = jnp.concatenate([kvp_ref[...], kvc_ref[...]], axis=0)
    k_all = kvfull[:, :nkv].astype(BF16)
    vt_all = kvfull[:, nkv:].T.astype(BF16)
    dn = (((1,), (1,)), ((), ()))
    units = [(s, kh) for s in range(nsub) for kh in range(A_KV_HEADS)]
    for s, kh in units:
        qs = jnp.concatenate([q_ref[s * qb:(s + 1) * qb, (kh * A_GROUP + g) * A_HD:(kh * A_GROUP + g + 1) * A_HD]
                              for g in range(A_GROUP)], axis=0)
        t_ref[s * A_KV_HEADS + kh] = lax.dot_general(k_all[s * qb:s * qb + nk, kh * A_HD:(kh + 1) * A_HD], qs, dn,
                                                     preferred_element_type=F32)
    for s in range(nsub):
        outs = []
        for kh in range(A_KV_HEADS):
            u = s * A_KV_HEADS + kh
            st = t_ref[u] + bias_ref[kh]
            if s == 0:
                key_pos = lax.broadcasted_iota(jnp.int32, (nk, 1), 0) + (cb * nsub * qb - back)
                st = st + jnp.where(key_pos >= 0, 0.0, NEG_INF)
            sk = sink_ref[kh]
            m = jnp.maximum(jnp.max(st, axis=0, keepdims=True), sk)
            e = jnp.exp2(st - m)
            inv_den = 1.0 / (jnp.sum(e, axis=0, keepdims=True) + jnp.exp2(sk - m))
            p_ref[u] = e.astype(BF16)
            ot = jnp.dot(vt_all[kh * A_HD:(kh + 1) * A_HD, s * qb:s * qb + nk], p_ref[u],
                         preferred_element_type=F32)
            o = (ot * inv_den).T
            for g in range(A_GROUP):
                outs.append(o[g * qb:(g + 1) * qb])
        o_ref[s * qb:(s + 1) * qb, :] = jnp.concatenate(outs, axis=1).astype(o_ref.dtype)


def _a_prompt_tables(qb, sinks):
    back = CHUNK * WIN_CHUNKS
    slopes = _alibi(A_HEADS).reshape(A_KV_HEADS, A_GROUP)
    qpos = np.arange(qb)
    kpos = np.arange(back + qb) - back
    dist = np.abs(kpos[:, None] - qpos[None, :]).astype(np.float64)
    band_lo = (qpos // CHUNK) * CHUNK - back
    in_band = (kpos[:, None] >= band_lo[None, :]) & (kpos[:, None] < band_lo[None, :] + back + CHUNK)
    bias = np.where(in_band[None, None], -slopes[:, :, None, None] * dist[None, None], -np.inf)
    bias = np.transpose(bias, (0, 2, 1, 3)).reshape(A_KV_HEADS, back + qb, A_GROUP * qb) * LOG2E
    sk = jnp.repeat(sinks.astype(F32).reshape(A_KV_HEADS, A_GROUP), qb, axis=1)[:, None, :] * LOG2E
    return jnp.asarray(bias, F32), sk


def _a_tables(qpos, kpos, sinks):
    slopes = _alibi(A_HEADS).reshape(A_KV_HEADS, A_GROUP)
    dist = np.abs(qpos[:, None] - kpos[None, :]).astype(np.float64)
    bias = -slopes[:, :, None, None] * dist
    nq = len(qpos)
    bias = jnp.asarray(bias.reshape(A_KV_HEADS, A_GROUP * nq, len(kpos)) * LOG2E, F32)
    sk = jnp.repeat(sinks.astype(F32).reshape(A_KV_HEADS, A_GROUP), nq, axis=1)[..., None] * LOG2E
    return bias, sk


def attn_a_prompt(q_all, kv_all, sinks, batch, t, qb):
    nq = A_HEADS * A_HD
    nkv2 = 2 * A_KV_HEADS * A_HD
    back = CHUNK * WIN_CHUNKS
    bias, sk = _a_prompt_tables(qb, sinks)
    nsub = max(s for s in (1, 2, 4, 8) if t % (s * qb) == 0)
    rows = nsub * qb
    nblk = t // rows
    r = rows // back
    return pl.pallas_call(
        functools.partial(_attn_a_prompt_kernel, qb, nsub),
        grid=(batch, nblk),
        in_specs=[
            pl.BlockSpec((rows, nq), lambda b, i: (b * nblk + i, 0)),
            pl.BlockSpec((rows, nkv2), lambda b, i: (b * nblk + i, 0)),
            pl.BlockSpec((back, nkv2), lambda b, i: (jnp.maximum((b * nblk + i) * r - 1, 0), 0)),
            pl.BlockSpec(bias.shape, lambda b, i: (0, 0, 0)),
            pl.BlockSpec(sk.shape, lambda b, i: (0, 0, 0)),
        ],
        out_specs=pl.BlockSpec((rows, nq), lambda b, i: (b * nblk + i, 0)),
        out_shape=jax.ShapeDtypeStruct((batch * t, nq), BF16),
        scratch_shapes=[pltpu.VMEM((nsub * A_KV_HEADS, back + qb, A_GROUP * qb), F32),
                        pltpu.VMEM((nsub * A_KV_HEADS, back + qb, A_GROUP * qb), BF16)],
        compiler_params=_cparams(("arbitrary", "arbitrary")), name="attn_a_prompt",
    )(q_all, kv_all, kv_all, bias, sk)


def _attn_a_sample_kernel(q_ref, k_ref, v_ref, bias_ref, sink_ref, o_ref):
    o = _attn_a_chunk(q_ref[...], k_ref[...].astype(BF16), v_ref[...].astype(BF16), bias_ref, sink_ref, None)
    o_ref[...] = o.astype(o_ref.dtype)


def attn_a_sample(q_all, kband, vband, sinks, row0, dec_batch, dec_t):
    nq = A_HEADS * A_HD
    s = kband.shape[1]
    past = s - dec_t
    kpos = np.concatenate([np.arange(past) - past, np.arange(dec_t)])
    bias, sk = _a_tables(np.arange(dec_t), kpos, sinks)
    blk0 = row0 // dec_t
    return pl.pallas_call(
        _attn_a_sample_kernel,
        grid=(dec_batch,),
        in_specs=[
            pl.BlockSpec((dec_t, nq), lambda b: (blk0 + b, 0)),
            pl.BlockSpec((None, s, kband.shape[2]), lambda b: (b, 0, 0)),
            pl.BlockSpec((None, s, vband.shape[2]), lambda b: (b, 0, 0)),
            pl.BlockSpec(bias.shape, lambda b: (0, 0, 0)),
            pl.BlockSpec(sk.shape, lambda b: (0, 0, 0)),
        ],
        out_specs=pl.BlockSpec((dec_t, nq), lambda b: (b, 0)),
        out_shape=jax.ShapeDtypeStruct((dec_batch * dec_t, nq), BF16),
        compiler_params=_cparams(("arbitrary",)), name="attn_a_sample",
    )(q_all, kband, vband, bias, sk)


B_Q_SCALE = B_HD ** -0.5 * float(np.log2(np.e))
LOG2E = float(np.log2(np.e))


def _stack_maps(qh):
    lane = lax.broadcasted_iota(jnp.int32, qh.shape, 1)
    zero = jnp.zeros_like(qh)
    return jnp.concatenate([jnp.where(lane < B_HD, qh, zero), jnp.where(lane >= B_HD, qh, zero)], axis=0)


def _diff_finish(acc, l, nq, lam, lam_init, subln):
    inv = 1.0 / l
    o = acc[:nq] * inv[:nq] - lam * (acc[nq:] * inv[nq:])
    ms = jnp.mean(o * o, axis=-1, keepdims=True)
    return o * lax.rsqrt(ms + SUBLN_EPS) * subln * (1.0 - lam_init)


def _attn_b_prompt_kernel(qb, nb, lam_init, pair_ref, slope_ref, lam_ref, q_ref, k_ref, vt_ref, kbias_ref,
                          subln_ref, o_ref, acc_ref, m_ref, al_ref, qq_ref, t0_ref, t1_ref, p0_ref, p1_ref):
    g = pl.program_id(1)
    lam = lam_ref[0]
    dn = (((1,), (1,)), ((), ()))
    t_refs = (t0_ref, t1_ref)
    p_refs = (p0_ref, p1_ref)
    heads = range(HEAD_GROUP)
    n_pairs = nb * (nb + 1) // 2
    slopes = [slope_ref[g * HEAD_GROUP + u] for u in heads]

    def pair(n):
        return pair_ref[2 * n], pair_ref[2 * n + 1]

    def scores(n, slot):
        j, iq = pair(n)
        ks = pl.multiple_of(j * qb, qb)
        for u in heads:
            t_refs[slot][u] = lax.dot_general(k_ref[u, pl.ds(ks, qb), :], qq_ref[u * nb + iq], dn,
                                              preferred_element_type=F32)

    def softmax_step(n, slot):
        j, iq = pair(n)
        on_diag = (j == iq).astype(jnp.int32)
        for u in heads:
            st = u * nb + iq
            m = m_ref[st, 0:1, :]
            t = t_refs[slot][u] + kbias_ref[u, on_diag]
            off = slopes[u] * jnp.full((1, 2 * qb), (j - iq) * qb, jnp.int32).astype(F32)
            m_new = jnp.maximum(m, jnp.max(t, axis=0, keepdims=True) + off)
            p_refs[slot][u] = jnp.exp2(t + (off - m_new)).astype(BF16)
            al_ref[slot * HEAD_GROUP + u, 0:1, :] = jnp.exp2(m - m_new)
            m_ref[st, 0:1, :] = m_new

    def accumulate(n, slot):
        j, iq = pair(jnp.maximum(n, 0))
        ks = pl.multiple_of(j * qb, qb)
        for u in heads:
            st = u * nb + iq
            pv = jnp.dot(vt_ref[u, :, pl.ds(ks, qb)], p_refs[slot][u], preferred_element_type=F32)
            acc_ref[st] = al_ref[slot * HEAD_GROUP + u, 0:1, :] * acc_ref[st] + pv

    def step(n, slot):
        softmax_step(n, slot)
        accumulate(n - 1, 1 - slot)
        scores(jnp.minimum(n + 1, n_pairs - 1), 1 - slot)
        return 0

    acc_ref[...] = jnp.zeros_like(acc_ref)
    m_ref[...] = jnp.full(m_ref.shape, NEG_INF, F32)
    al_ref[...] = jnp.ones_like(al_ref)
    p1_ref[...] = jnp.zeros_like(p1_ref)
    for u in heads:
        for iq in range(nb):
            qq_ref[u * nb + iq] = _stack_maps(q_ref[u, iq * qb:(iq + 1) * qb, :])
    scores(0, 0)
    def steps(k, c):
        for r in range(PAIR_UNROLL):
            step(PAIR_UNROLL * k + r, r % 2)
        return c

    lax.fori_loop(0, n_pairs // PAIR_UNROLL, steps, 0)
    for n in range(n_pairs - n_pairs % PAIR_UNROLL, n_pairs):
        step(n, n % 2)
    accumulate(n_pairs - 1, (n_pairs - 1) % 2)
    for u in heads:
        for iq in range(nb):
            inv = 1.0 / acc_ref[u * nb + iq, LANES:LANES + 1, :]
            acc = acc_ref[u * nb + iq, :LANES, :]
            ot = acc[:, :qb] * inv[:, :qb] - lam * (acc[:, qb:] * inv[:, qb:])
            ms = jnp.mean(ot * ot, axis=0, keepdims=True)
            ot = ot * lax.rsqrt(ms + SUBLN_EPS) * subln_ref[...] * (1.0 - lam_init)
            o_ref[u, iq * qb:(iq + 1) * qb, :] = ot.T.astype(o_ref.dtype)


def attn_b_prompt(qkvh, vt, lam, subln, lam_init, batch, t, qb):
    nblk = t // qb
    qq = np.arange(qb)
    slope2 = (_alibi(B_HEADS) * LOG2E)[:, None, None]
    kcol = np.broadcast_to(qq[:, None], (qb, qb)).astype(np.float64)
    tri = -2.0 * np.maximum(qq[:, None] - qq[None, :], 0)
    msk = np.where((qq[:, None] // CHUNK) <= (qq[None, :] // CHUNK), 0.0, -np.inf)
    both = lambda a: np.concatenate([a, a], axis=-1)
    kbias = np.stack([both(slope2 * kcol[None]), both(slope2 * (kcol + tri)[None] + msk[None])], axis=1)
    kbias = jnp.asarray(kbias, F32)
    slopes = jnp.asarray(_alibi(B_HEADS) * LOG2E, F32)
    subln_t = jnp.broadcast_to(subln.astype(F32).reshape(LANES, 1), (LANES, qb))
    pairs = jnp.asarray([v for j in range(nblk) for i in range(j, nblk) for v in (j, i)], jnp.int32)
    hg = HEAD_GROUP
    n_groups = B_HEADS // hg
    grid_spec = pltpu.PrefetchScalarGridSpec(
        num_scalar_prefetch=3, grid=(batch, n_groups),
        in_specs=[
            pl.BlockSpec((hg, t, LANES), lambda b, g, *_: (g, b, 0)),
            pl.BlockSpec((hg, t, LANES), lambda b, g, *_: (n_groups + g, b, 0)),
            pl.BlockSpec((hg, LANES + ONES_ROWS, t), lambda b, g, *_: (g, 0, b)),
            pl.BlockSpec((hg,) + kbias.shape[1:], lambda b, g, *_: (g, 0, 0, 0)),
            pl.BlockSpec(subln_t.shape, lambda b, g, *_: (0, 0)),
        ],
        out_specs=pl.BlockSpec((hg, t, LANES), lambda b, g, *_: (g, b, 0)),
        scratch_shapes=[pltpu.VMEM((hg * nblk, LANES + ONES_ROWS, 2 * qb), F32),
                        pltpu.VMEM((hg * nblk, SUBLANES, 2 * qb), F32),
                        pltpu.VMEM((2 * hg, SUBLANES, 2 * qb), F32),
                        pltpu.VMEM((hg * nblk, 2 * qb, LANES), BF16),
                        pltpu.VMEM((hg, qb, 2 * qb), F32), pltpu.VMEM((hg, qb, 2 * qb), F32),
                        pltpu.VMEM((hg, qb, 2 * qb), BF16), pltpu.VMEM((hg, qb, 2 * qb), BF16)],
    )
    return pl.pallas_call(
        functools.partial(_attn_b_prompt_kernel, qb, nblk, lam_init), grid_spec=grid_spec,
        out_shape=jax.ShapeDtypeStruct((B_HEADS, batch * t, LANES), BF16),
        compiler_params=_cparams(("arbitrary", "arbitrary")), name="attn_b_prompt",
    )(pairs, slopes, lam.reshape(1).astype(F32), qkvh, qkvh, vt, kbias, subln_t)


def _attn_b_sample_kernel(dec_t, past, lam_init, slopes, lam_ref, q_ref, kn_ref, vn_ref, ck_ref, cv_ref,
                          tnew_ref, subln_ref, o_ref):
    lam = lam_ref[0]
    subln = subln_ref[...]
    colc = lax.broadcasted_iota(jnp.int32, (1, past), 1).astype(F32) - float(past)
    for h in range(B_HEADS):
        slope = float(slopes[h]) * LOG2E
        qq = _stack_maps(q_ref[h])
        kc = ck_ref[pl.ds(h, past, stride=B_HEADS), :].astype(BF16)
        vc = cv_ref[pl.ds(h, past, stride=B_HEADS), :].astype(BF16)
        dn = (((1,), (1,)), ((), ()))
        s_c = lax.dot_general(qq, kc, dn, preferred_element_type=F32) + slope * colc
        s_n = lax.dot_general(qq, kn_ref[h], dn, preferred_element_type=F32) + slope * tnew_ref[...]
        m = jnp.maximum(jnp.max(s_c, axis=-1, keepdims=True), jnp.max(s_n, axis=-1, keepdims=True))
        p_c = jnp.exp2(s_c - m)
        p_n = jnp.exp2(s_n - m)
        l = jnp.sum(p_c, axis=-1, keepdims=True) + jnp.sum(p_n, axis=-1, keepdims=True)
        acc = (jnp.dot(p_c.astype(BF16), vc, preferred_element_type=F32)
               + jnp.dot(p_n.astype(BF16), vn_ref[h], preferred_element_type=F32))
        o_ref[h] = _diff_finish(acc, l, dec_t, lam, lam_init, subln).astype(o_ref.dtype)


def attn_b_sample(qkvh, cache_k, cache_v, lam, subln, lam_init, row0, dec_batch, dec_t):
    past = cache_k.shape[1]
    qi = np.arange(dec_t)
    tnew = (qi[:, None] - np.abs(qi[:, None] - qi[None, :])).astype(np.float32)
    tnew = jnp.asarray(np.concatenate([tnew, tnew], axis=0))
    blk0 = row0 // dec_t
    rows = past * B_HEADS
    cache_k = cache_k.reshape(dec_batch, rows, LANES)
    cache_v = cache_v.reshape(dec_batch, rows, LANES)
    grid_spec = pltpu.PrefetchScalarGridSpec(
        num_scalar_prefetch=1, grid=(dec_batch,),
        in_specs=[
            pl.BlockSpec((B_HEADS, dec_t, LANES), lambda b, *_: (0, blk0 + b, 0)),
            pl.BlockSpec((B_HEADS, dec_t, LANES), lambda b, *_: (1, blk0 + b, 0)),
            pl.BlockSpec((B_HEADS, dec_t, LANES), lambda b, *_: (2, blk0 + b, 0)),
            pl.BlockSpec((None, rows, LANES), lambda b, *_: (b, 0, 0)),
            pl.BlockSpec((None, rows, LANES), lambda b, *_: (b, 0, 0)),
            pl.BlockSpec(tnew.shape, lambda b, *_: (0, 0)),
            pl.BlockSpec((1, LANES), lambda b, *_: (0, 0)),
        ],
        out_specs=pl.BlockSpec((B_HEADS, dec_t, LANES), lambda b, *_: (0, b, 0)),
    )
    return pl.pallas_call(
        functools.partial(_attn_b_sample_kernel, dec_t, past, lam_init, tuple(_alibi(B_HEADS))),
        grid_spec=grid_spec,
        out_shape=jax.ShapeDtypeStruct((B_HEADS, dec_batch * dec_t, LANES), BF16),
        compiler_params=_cparams(("arbitrary",)), name="attn_b_sample",
    )(lam.reshape(1).astype(F32), qkvh, qkvh, qkvh, cache_k, cache_v, tnew, subln.reshape(1, LANES).astype(F32))


ROUTE_ROWS = 40
ROUTE_FIELDS = 8


def _route_t(lt):
    row = lax.broadcasted_iota(jnp.int32, lt.shape, 0).astype(F32)
    big = float(LANES)
    lg = jnp.where(row < N_GROUPS, lt, NEG_INF)
    gmax = jnp.max(lg, axis=0, keepdims=True)
    g_idx = jnp.min(jnp.where(lg == gmax, row, big), axis=0, keepdims=True)
    g_prob = 1.0 / jnp.sum(jnp.exp(lg - gmax), axis=0, keepdims=True)
    lo = N_GROUPS + EXPERTS_PER_GROUP * g_idx
    le = jnp.where((row >= lo) & (row < lo + EXPERTS_PER_GROUP), lt, NEG_INF)
    m1 = jnp.max(le, axis=0, keepdims=True)
    i1 = jnp.min(jnp.where(le == m1, row, big), axis=0, keepdims=True)
    le2 = jnp.where(row == i1, NEG_INF, le)
    m2 = jnp.max(le2, axis=0, keepdims=True)
    i2 = jnp.min(jnp.where(le2 == m2, row, big), axis=0, keepdims=True)
    t = jnp.exp(m2 - m1)
    ga = 1.0 / (1.0 + t)
    gb = t * ga
    return row, i1, i2, g_prob * ga, g_prob * gb


def _unpack_pairs(u):
    lo = pltpu.bitcast(u << 16, F32)
    hi = pltpu.bitcast(u & jnp.uint32(0xFFFF0000), F32)
    return jnp.concatenate([lo, hi], axis=1)


def _pack_pairs(x):
    bits = pltpu.bitcast(x.astype(BF16).astype(F32), jnp.uint32)
    w = x.shape[1] // 2
    return (bits[:, :w] >> 16) | bits[:, w:]


def _out_router_kernel(n_prompt_blocks, heads_in, two_o, two_res, *refs):
    refs = list(refs)
    oa_ref = refs.pop(0)
    ob_ref = refs.pop(0) if two_o else None
    ra_ref = refs.pop(0)
    rb_ref = refs.pop(0) if two_res else None
    (wo_ref, g_ref, wr_ref, wr2_ref, br_ref, utri_ref, h_ref, xn_ref, meta_ref, route_ref, cnt_ref,
     carry_ref) = refs
    is_prompt = pl.program_id(0) < n_prompt_blocks

    def load_o(ref):
        if heads_in:
            return jnp.concatenate([ref[hh] for hh in range(ref.shape[0])], axis=1)
        return ref[...]

    o = load_o(oa_ref)
    if two_o:
        o = jnp.where(is_prompt, o, load_o(ob_ref))
    res = ra_ref[...]
    if two_res:
        res = jnp.where(is_prompt, res, rb_ref[...])
    h = res + jnp.dot(o, wo_ref[...], preferred_element_type=F32)
    h_ref[...] = h
    ms = jnp.mean(h * h, axis=-1, keepdims=True)
    xn = h * lax.rsqrt(ms + NORM_EPS) * g_ref[...]
    x_hi = xn.astype(BF16)
    _store_planes(xn_ref, _pack_pairs(xn))
    x_lo = (xn - x_hi.astype(F32)).astype(BF16)
    r = jnp.dot(x_hi, wr_ref[...], preferred_element_type=F32)
    logits = (r[:, :LANES] + r[:, LANES:]) + jnp.dot(x_lo, wr2_ref[...], preferred_element_type=F32) + br_ref[...]
    row, i1, i2, g1, g2 = _route_t(logits.T[:ROUTE_ROWS])

    @pl.when(pl.program_id(0) == 0)
    def _():
        carry_ref[...] = jnp.zeros_like(carry_ref)

    sel1 = row == i1
    sel2 = row == i2
    onehot = jnp.where(sel1 | sel2, 1.0, 0.0)
    before = jnp.dot(onehot.astype(BF16), utri_ref[...], preferred_element_type=F32) + carry_ref[:, 0:1]
    rank1 = jnp.sum(jnp.where(sel1, before, 0.0), axis=0, keepdims=True)
    rank2 = jnp.sum(jnp.where(sel2, before, 0.0), axis=0, keepdims=True)
    carry_ref[...] = carry_ref[...] + jnp.sum(onehot, axis=1, keepdims=True)
    cnt_ref[...] = carry_ref[...]
    fields = [i1 - N_GROUPS, i2 - N_GROUPS, g1, g2, rank1, rank2]
    fields += [jnp.zeros_like(g1)] * (ROUTE_FIELDS - len(fields))
    route = jnp.concatenate(fields, axis=0)
    route_ref[...] = route
    meta_ref[...] = jnp.concatenate([route, jnp.zeros((LANES - ROUTE_FIELDS, route.shape[1]), F32)], axis=0).T


def out_router(oa, ob, ra, rb, wo, g, w_rg, b_rg, w_re, b_re, heads_in, tr):
    d = ra.shape[1]
    if heads_in:
        na = oa.shape[1]
        nb = 0 if ob is None else ob.shape[1]
    else:
        na = oa.shape[0]
        nb = 0 if ob is None else ob.shape[0]
    n = na + nb
    npb = na // tr
    two_o = ob is not None
    two_res = rb is not None
    first = lambda i: (jnp.minimum(i, npb - 1), 0)
    second = lambda i: (jnp.maximum(i - npb, 0), 0)
    plain = lambda i: (i, 0)
    in_specs, args = [], []

    def add_o(x, imap):
        if heads_in:
            in_specs.append(pl.BlockSpec((x.shape[0], tr, LANES), lambda i: (0, imap(i)[0], 0)))
        else:
            in_specs.append(pl.BlockSpec((tr, x.shape[1]), imap))
        args.append(x)

    add_o(oa, first if two_o else plain)
    if two_o:
        add_o(ob, second)
    in_specs.append(pl.BlockSpec((tr, d), first if two_res else plain))
    args.append(ra)
    if two_res:
        in_specs.append(pl.BlockSpec((tr, d), second))
        args.append(rb)
    wr = jnp.zeros((d, LANES), F32).at[:, :N_GROUPS].set(w_rg.astype(F32))
    wr = wr.at[:, N_GROUPS:N_GROUPS + N_EXPERTS].set(w_re.astype(F32))
    wr_hi = wr.astype(BF16)
    wr_lo = (wr - wr_hi.astype(F32)).astype(BF16)
    br = jnp.zeros((1, LANES), F32).at[0, :N_GROUPS].set(b_rg.astype(F32))
    br = br.at[0, N_GROUPS:N_GROUPS + N_EXPERTS].set(b_re.astype(F32))
    utri = jnp.asarray(np.triu(np.ones((tr, tr), np.float32), 1), BF16)
    consts = [wo, g.reshape(1, d).astype(F32), jnp.concatenate([wr_hi, wr_lo], axis=1), wr_hi, br, utri]
    for cst in consts:
        in_specs.append(pl.BlockSpec(cst.shape, lambda i: (0, 0)))
        args.append(cst)
    parts = d // 2 // SC_ROW_WORDS
    out_shape = [jax.ShapeDtypeStruct((n, d), F32), jax.ShapeDtypeStruct((parts, n, SC_ROW_WORDS), jnp.uint32),
                 jax.ShapeDtypeStruct((n, LANES), F32), jax.ShapeDtypeStruct((ROUTE_FIELDS, n), F32),
                 jax.ShapeDtypeStruct((ROUTE_ROWS, LANES), F32)]
    out_specs = [pl.BlockSpec((tr, d), plain), pl.BlockSpec((parts, tr, SC_ROW_WORDS), lambda i: (0, i, 0)),
                 pl.BlockSpec((tr, LANES), plain), pl.BlockSpec((ROUTE_FIELDS, tr), lambda i: (0, i)),
                 pl.BlockSpec((ROUTE_ROWS, LANES), lambda i: (0, 0))]
    return pl.pallas_call(
        functools.partial(_out_router_kernel, npb, heads_in, two_o, two_res),
        grid=(n // tr,), in_specs=in_specs, out_specs=out_specs, out_shape=out_shape,
        scratch_shapes=[pltpu.VMEM((ROUTE_ROWS, LANES), F32)],
        compiler_params=_cparams(("arbitrary",)), name="out_router")(*args)


def _expert_kernel(tm, tps, te_ref, nt_ref, x_ref, *refs):
    w_refs, o_ref = refs[:3 * tps], refs[3 * tps]
    first = pl.program_id(0) * tps
    n_valid = jnp.clip(nt_ref[0] - first, 0, tps)

    def tile(k):
        wg_ref, wu_ref, wd_ref = w_refs[3 * k:3 * k + 3]
        rows = slice(k * tm, (k + 1) * tm)
        x = _unpack_pairs(jnp.concatenate([x_ref[q, rows, :] for q in range(x_ref.shape[0])], axis=1)).astype(BF16)
        a = jnp.dot(x, wg_ref[...].astype(BF16), preferred_element_type=F32)
        u = jnp.dot(x, wu_ref[...].astype(BF16), preferred_element_type=F32)
        hh = (a * (1.0 / (1.0 + jnp.exp(-a))) * u).astype(BF16)
        packed = _pack_pairs(jnp.dot(hh, wd_ref[...].astype(BF16), preferred_element_type=F32))
        for q in range(o_ref.shape[0]):
            o_ref[q, rows, :] = packed[:, q * o_ref.shape[2]:(q + 1) * o_ref.shape[2]]

    for nv in range(tps + 1):

        @pl.when(n_valid == nv)
        def _(nv=nv):
            for k in range(nv):
                tile(k)
            if nv < tps:
                o_ref[:, nv * tm:, :] = jnp.zeros((o_ref.shape[0], (tps - nv) * tm, o_ref.shape[2]), o_ref.dtype)


def expert_mlp(xs, tile_expert, n_tiles, w_gate, w_up, w_down, layer, tm):
    parts, p, w = xs.shape
    d, f = w_gate.shape[-2:]
    epg = w_gate.shape[2]
    tps = 2 if (p // tm) % 2 == 0 else 1
    w_specs, w_args = [], []
    for k in range(tps):
        wmap = lambda t, te, nt, k=k: (layer, te[tps * t + k] // epg, te[tps * t + k] % epg, 0, 0)
        w_specs += [pl.BlockSpec((None, None, None, d, f), wmap), pl.BlockSpec((None, None, None, d, f), wmap),
                    pl.BlockSpec((None, None, None, f, d), wmap)]
        w_args += [w_gate, w_up, w_down]
    grid_spec = pltpu.PrefetchScalarGridSpec(
        num_scalar_prefetch=2, grid=(p // (tps * tm),),
        in_specs=[pl.BlockSpec((parts, tps * tm, w), lambda t, te, nt: (0, t, 0))] + w_specs,
        out_specs=pl.BlockSpec((parts, tps * tm, w), lambda t, te, nt: (0, t, 0)),
    )
    return pl.pallas_call(
        functools.partial(_expert_kernel, tm, tps), grid_spec=grid_spec,
        out_shape=jax.ShapeDtypeStruct((parts, p, w), jnp.uint32),
        compiler_params=_cparams(("arbitrary",)), name="expert_mlp")(tile_expert, n_tiles, xs, *w_args)


def sc_scatter_rows(x2, idx, n_out):
    mesh = plsc.VectorSubcoreMesh(core_axis_name="c", subcore_axis_name="s")
    n_src_win = x2.shape[0] // SC_WINDOW

    @functools.partial(pl.kernel, out_type=jax.ShapeDtypeStruct((n_out, SC_ROW_WORDS), x2.dtype), mesh=mesh)
    def scatter(x_hbm, i_hbm, o_hbm):
        def body(x_vmem, i_vmem):
            pltpu.sync_copy(x_vmem, o_hbm.at[i_vmem.at[0]])

        pltpu.emit_pipeline(
            body, grid=(n_out // SC_WINDOW,),
            in_specs=[pl.BlockSpec((SC_WINDOW, SC_ROW_WORDS), lambda i: (lax.rem(i, n_src_win), 0)),
                      pl.BlockSpec((1, SC_WINDOW), lambda i: (0, i))],
            out_specs=[], core_axis_name=("c", "s"), dimension_semantics=(pltpu.PARALLEL,),
        )(x_hbm, i_hbm)

    return scatter(x2, idx)


def sc_gather_rows(x2, idx):
    mesh = plsc.VectorSubcoreMesh(core_axis_name="c", subcore_axis_name="s")
    n_col, n = idx.shape

    @functools.partial(pl.kernel, out_type=jax.ShapeDtypeStruct((n, n_col * SC_ROW_WORDS), x2.dtype), mesh=mesh)
    def gather(x_hbm, i_hbm, o_hbm):
        def body(i_vmem, o_vmem):
            pltpu.sync_copy(x_hbm.at[i_vmem.at[0]], o_vmem)

        pltpu.emit_pipeline(
            body, grid=(n // SC_WINDOW, n_col),
            in_specs=[pl.BlockSpec((1, SC_WINDOW), lambda i, c: (c, i))],
            out_specs=[pl.BlockSpec((SC_WINDOW, SC_ROW_WORDS), lambda i, c: (i, c))],
            core_axis_name=("c", "s"), dimension_semantics=(pltpu.PARALLEL, pltpu.ARBITRARY),
        )(i_hbm, o_hbm)

    return gather(x2, idx)


def moe_dispatch(xn_packed, route, counts, tm, side=None):
    parts, n, w = xn_packed.shape
    e = route[0:2].astype(jnp.int32)
    rank = route[4:6].astype(jnp.int32)
    counts = counts[N_GROUPS:N_GROUPS + N_EXPERTS, 0].astype(jnp.int32)
    padded = ((counts + tm - 1) // tm) * tm
    pend = jnp.cumsum(padded)
    pstart = pend - padded
    ids = jnp.arange(N_EXPERTS, dtype=jnp.int32)
    pos = jnp.sum(jnp.where(e[..., None] == ids, pstart, 0), axis=-1) + rank
    p = 2 * n + N_EXPERTS * tm
    n_pad = p - 2 * n
    gap_len = jnp.concatenate([padded - counts, (p - pend[-1]).reshape(1)])
    gap_first = jnp.concatenate([pstart + counts, pend[-1:]])
    gap_end = jnp.cumsum(gap_len)
    k = jnp.arange(n_pad, dtype=jnp.int32)
    gap = jnp.sum((k[:, None] >= gap_end[None, :]).astype(jnp.int32), axis=1)
    shift = gap_first - (gap_end - gap_len)
    pad_pos = k + jnp.sum(jnp.where(gap[:, None] == jnp.arange(N_EXPERTS + 1), shift, 0), axis=-1)
    planes = lambda rows: jnp.concatenate([rows + q * p for q in range(parts)])
    scat_idx = jnp.concatenate([planes(pos[0]), planes(pos[1]), planes(pad_pos)]).reshape(1, parts * p)
    gath_idx = jnp.stack([pos[k] + q * p for k in range(2) for q in range(parts)])
    tile_start = jnp.arange(p // tm, dtype=jnp.int32) * tm
    tile_expert = jnp.sum((tile_start[:, None] >= pend[None, :]).astype(jnp.int32), axis=1)
    tile_expert = jnp.minimum(tile_expert, N_EXPERTS - 1)
    n_tiles = (pend[-1] // tm).astype(jnp.int32).reshape(1)
    side_out = None
    if side is not None:
        fn, args = side
        scat_idx, args = lax.optimization_barrier((scat_idx, args))
        side_out = fn(*args)
    xs = sc_scatter_rows(xn_packed.reshape(parts * n, w), scat_idx, parts * p).reshape(parts, p, w)
    if side is not None:
        xs, side_out = lax.optimization_barrier((xs, side_out))
    return xs, tile_expert, n_tiles, gath_idx, side_out


def moe_experts(xn_packed, route, counts, w_gate, w_up, w_down, layer, tm, side=None):
    parts, n, w = xn_packed.shape
    xs, tile_expert, n_tiles, gath_idx, side_out = moe_dispatch(xn_packed, route, counts, tm, side)
    o = expert_mlp(xs, tile_expert, n_tiles, w_gate, w_up, w_down, layer, tm)
    return sc_gather_rows(o.reshape(parts * o.shape[1], w), gath_idx), side_out


def _final_norm_kernel(n_prompt_blocks, h_ref, o0_ref, o1_ref, meta_ref, g_ref, yp_ref, ys_ref):
    meta = meta_ref[...]
    x = (h_ref[...] + meta[:, 2:3] * _unpack_pairs(o0_ref[...])) + meta[:, 3:4] * _unpack_pairs(o1_ref[...])
    ms = jnp.mean(x * x, axis=-1, keepdims=True)
    y = x * lax.rsqrt(ms + NORM_EPS) * g_ref[...]
    i = pl.program_id(0)

    @pl.when(i < n_prompt_blocks)
    def _():
        yp_ref[...] = y

    @pl.when(i >= n_prompt_blocks)
    def _():
        ys_ref[...] = y


def final_norm(h, og, meta, g, n_prompt, tr):
    n, d = h.shape
    npb = n_prompt // tr
    return pl.pallas_call(
        functools.partial(_final_norm_kernel, npb),
        grid=(n // tr,),
        in_specs=[pl.BlockSpec((tr, d), lambda i: (i, 0)),
                  pl.BlockSpec((tr, og.shape[1] // 2), lambda i: (i, 0)),
                  pl.BlockSpec((tr, og.shape[1] // 2), lambda i: (i, 1)),
                  pl.BlockSpec((tr, LANES), lambda i: (i, 0)), pl.BlockSpec((1, d), lambda i: (0, 0))],
        out_specs=[pl.BlockSpec((tr, d), lambda i: (jnp.minimum(i, npb - 1), 0)),
                   pl.BlockSpec((tr, d), lambda i: (jnp.maximum(i - npb, 0), 0))],
        out_shape=[jax.ShapeDtypeStruct((n_prompt, d), F32), jax.ShapeDtypeStruct((n - n_prompt, d), F32)],
        compiler_params=_cparams(("arbitrary",)), name="final_norm")(h, og, og, meta, g.reshape(1, d).astype(F32))


def _pick_rows_tile(n_prompt, n_sample):
    tr = ROW_TILE
    while tr >= SUBLANES:
        if n_prompt % tr == 0 and n_sample % tr == 0:
            return tr
        tr //= 2
    raise ValueError("row counts must be multiples of the sublane count")


def kernel(x_prompt, x_sample, cache_a_k, cache_a_v, cache_b_k, cache_b_v, norm_mix, norm_ffn, norm_final,
           w_a_qkv, b_a_qkv, a_sinks, w_a_o, w_b_qkv, b_lambda, b_subln, w_b_o,
           w_route_group, b_route_group, w_route_expert, b_route_expert, w_gate, w_up, w_down):
    batch, t, d = x_prompt.shape
    dec_batch, dec_t, _ = x_sample.shape
    n_p, n_s = batch * t, dec_batch * dec_t
    tr = _pick_rows_tile(n_p, n_s)
    tm = min(EXPERT_TILE, 4 * tr)
    xp = x_prompt.reshape(n_p, d)
    xs = x_sample.reshape(n_s, d)
    nq_a = A_HEADS * A_HD
    nkv_a = A_KV_HEADS * A_HD

    a_scale = jnp.where(jnp.arange(nq_a + 2 * nkv_a) < nq_a, A_HD ** -0.5 * LOG2E, 1.0).astype(F32)
    q_a, kv_ap, kv_as = norm_proj(xp, xs, norm_mix[0], (w_a_qkv[0] * a_scale).astype(BF16), b_a_qkv[0] * a_scale,
                                  [(0, nq_a, "flat", BF16), (nq_a, nq_a + 2 * nkv_a, "split", F32)], tr, n_p)
    qb_a = min(CHUNK * WIN_CHUNKS, t)
    o_ap = attn_a_prompt(q_a, kv_ap, a_sinks[0], batch, t, qb_a)
    kv_s = kv_as.reshape(dec_batch, dec_t, 2 * nkv_a)
    past_a = cache_a_k.shape[2]
    kband = jnp.concatenate([cache_a_k[0].reshape(dec_batch, past_a, nkv_a), kv_s[..., :nkv_a]], axis=1)
    vband = jnp.concatenate([cache_a_v[0].reshape(dec_batch, past_a, nkv_a), kv_s[..., nkv_a:]], axis=1)
    o_as = attn_a_sample(q_a, kband, vband, a_sinks[0], n_p, dec_batch, dec_t)
    h, xn, meta, route, counts = out_router(o_ap, o_as, xp, xs, w_a_o[0].astype(BF16), norm_ffn[0], w_route_group[0],
                                            b_route_group[0], w_route_expert[0], b_route_expert[0], False, tr)
    wb = B_HEADS * 2 * B_HD
    keep = min(CHUNK * WIN_CHUNKS, t)

    def cache_outputs_and_next_weights(kv_ap, kband, vband, w_b):
        kv_p = kv_ap.reshape(batch, t, 2 * nkv_a)[:, t - keep:]
        col_scale = jnp.where(jnp.arange(3 * wb) < wb, B_Q_SCALE, 1.0).astype(F32)
        return (kv_p[..., :nkv_a].reshape(1, batch, keep, A_KV_HEADS, A_HD),
                kv_p[..., nkv_a:].reshape(1, batch, keep, A_KV_HEADS, A_HD),
                kband[:, dec_t:].reshape(1, dec_batch, past_a, A_KV_HEADS, A_HD),
                vband[:, dec_t:].reshape(1, dec_batch, past_a, A_KV_HEADS, A_HD),
                (w_b * col_scale).astype(BF16))

    og, side_out = moe_experts(xn, route, counts, w_gate, w_up, w_down, 0, tm,
                               side=(cache_outputs_and_next_weights, (kv_ap, kband, vband, w_b_qkv[0])))
    a_k_prompt, a_v_prompt, a_k_sample, a_v_sample, w_b_bf16 = side_out

    lam_init = 0.8 - 0.6 * float(np.exp(-0.3 * 1))
    lp = b_lambda[0].astype(F32)
    lam = jnp.exp(jnp.sum(lp[0] * lp[1])) - jnp.exp(jnp.sum(lp[2] * lp[3])) + lam_init
    h, qkvh, k_bp, k_bs, v_bp, v_bs, vt = norm_proj(
        h, None, norm_mix[1], w_b_bf16, None,
        [(0, 3 * wb, "heads", BF16), (wb, 2 * wb, "split", F32), (2 * wb, 3 * wb, "split", F32),
         (2 * wb, 3 * wb, "heads_t", BF16)], tr, n_p, og=og, meta=meta)
    qb_b = min(ATTN_BLOCK, t)
    o_bp = attn_b_prompt(qkvh, vt, lam, b_subln[0], lam_init, batch, t, qb_b)
    o_bs = attn_b_sample(qkvh, cache_b_k[0], cache_b_v[0], lam, b_subln[0], lam_init, n_p, dec_batch, dec_t)
    hd2 = 2 * B_HD
    o_bs, k_bp, v_bp = lax.optimization_barrier((o_bs, k_bp, v_bp))
    b_k_prompt = k_bp.reshape(1, batch, t, B_HEADS, hd2)
    b_v_prompt = v_bp.reshape(1, batch, t, B_HEADS, hd2)
    o_bp, b_k_prompt, b_v_prompt = lax.optimization_barrier((o_bp, b_k_prompt, b_v_prompt))
    h, xn, meta, route, counts = out_router(o_bp, o_bs, h, None, w_b_o[0].astype(BF16), norm_ffn[1], w_route_group[1],
                                            b_route_group[1], w_route_expert[1], b_route_expert[1], True, tr)
    og, _ = moe_experts(xn, route, counts, w_gate, w_up, w_down, 1, tm)

    y_p, y_s = final_norm(h, og, meta, norm_final, n_p, tr)
    return (y_p.reshape(batch, t, d), y_s.reshape(dec_batch, dec_t, d),
            a_k_prompt, a_v_prompt, a_k_sample, a_v_sample, b_k_prompt, b_v_prompt,
            k_bs.reshape(1, dec_batch, dec_t, B_HEADS, hd2), v_bs.reshape(1, dec_batch, dec_t, B_HEADS, hd2))
```

```python
import functools

import jax
import jax.numpy as jnp
import numpy as np
from jax import lax
from jax.experimental import pallas as pl
from jax.experimental.pallas import tpu as pltpu
from jax.experimental.pallas import tpu_sc as plsc

F32 = jnp.float32
BF16 = jnp.bfloat16
NEG_INF = float("-inf")

CHUNK = 64
WIN_CHUNKS = 2
A_HEADS, A_KV_HEADS, A_GROUP, A_HD = 16, 4, 4, 64
B_HEADS, B_HD = 8, 64
N_GROUPS, EXPERTS_PER_GROUP = 4, 8
N_EXPERTS = N_GROUPS * EXPERTS_PER_GROUP
NORM_EPS = 1e-6
SUBLN_EPS = 1e-5

LANES = 128
SUBLANES = 8
ROW_TILE = 512
EXPERT_TILE = 512
ATTN_BLOCK = 256
COL_CHUNK = 512
PAIR_UNROLL = 12
HEAD_GROUP = 2
ONES_ROWS = 16
SC_WINDOW = 128
SC_ROW_WORDS = 256
VMEM_LIMIT = 48 * 1024 * 1024


def _cparams(sem, flags=None):
    return pltpu.CompilerParams(dimension_semantics=sem, vmem_limit_bytes=VMEM_LIMIT, flags=flags)


def _alibi(n):
    return 2.0 ** (-8.0 * np.arange(1, n + 1) / n)


def _store_planes(ref, x):
    w = ref.shape[2]
    for p in range(ref.shape[0]):
        ref[p] = x[:, p * w:(p + 1) * w]


def _add_expert_outputs(x, og_ref, meta_ref):
    half = og_ref.shape[1] // 2
    meta = meta_ref[...]
    return (x + meta[:, 2:3] * _unpack_pairs(og_ref[:, :half])) + meta[:, 3:4] * _unpack_pairs(og_ref[:, half:])


def _norm_proj_kernel(n_prompt_blocks, two_src, has_bias, combine, outs, *refs):
    refs = list(refs)
    xa_ref = refs.pop(0)
    xb_ref = refs.pop(0) if two_src else None
    og_ref, meta_ref = (refs.pop(0), refs.pop(0)) if combine else (None, None)
    g_ref = refs.pop(0)
    w_ref = refs.pop(0)
    b_ref = refs.pop(0) if has_bias else None
    out_refs = refs
    x = xa_ref[...]
    if two_src:
        x = jnp.where(pl.program_id(0) < n_prompt_blocks, x, xb_ref[...])
    if combine:
        x = _add_expert_outputs(x, og_ref, meta_ref)
        out_refs.pop(0)[...] = x
    ms = jnp.mean(x * x, axis=-1, keepdims=True)
    xn = (x * lax.rsqrt(ms + NORM_EPS) * g_ref[...]).astype(BF16)
    is_prompt = pl.program_id(0) < n_prompt_blocks
    out_refs = list(out_refs)
    targets = []
    for c0, c1, layout in outs:
        targets.append((c0, c1, layout, out_refs.pop(0), out_refs.pop(0) if layout == "split" else None))
    for s in range(0, w_ref.shape[1], COL_CHUNK):
        e = s + COL_CHUNK
        r = jnp.dot(xn, w_ref[:, s:e], preferred_element_type=F32)
        if has_bias:
            r = r + b_ref[:, s:e]
        for c0, c1, layout, o_ref, o2_ref in targets:
            if not (c0 <= s and e <= c1):
                continue
            rr = r.astype(o_ref.dtype)
            if layout == "flat":
                o_ref[:, s - c0:e - c0] = rr
            elif layout == "split":

                @pl.when(is_prompt)
                def _(rr=rr, o_ref=o_ref, s=s, e=e, c0=c0):
                    o_ref[:, s - c0:e - c0] = rr

                @pl.when(jnp.logical_not(is_prompt))
                def _(rr=rr, o2_ref=o2_ref, s=s, e=e, c0=c0):
                    o2_ref[:, s - c0:e - c0] = rr
            elif layout == "heads":
                for hh in range((e - s) // LANES):
                    o_ref[(s - c0) // LANES + hh] = rr[:, hh * LANES:(hh + 1) * LANES]
            elif layout == "heads_sample":

                @pl.when(jnp.logical_not(is_prompt))
                def _(rr=rr, o_ref=o_ref, s=s, e=e, c0=c0):
                    for hh in range((e - s) // LANES):
                        o_ref[(s - c0) // LANES + hh] = rr[:, hh * LANES:(hh + 1) * LANES]
            else:
                ones = jnp.ones((ONES_ROWS, r.shape[0]), o_ref.dtype)
                for hh in range((e - s) // LANES):
                    head = (s - c0) // LANES + hh
                    o_ref[head, :LANES, :] = r[:, hh * LANES:(hh + 1) * LANES].T.astype(o_ref.dtype)
                    o_ref[head, LANES:, :] = ones


def norm_proj(xa, xb, g, w, b, outs, tr, n_prompt, og=None, meta=None):
    na, d = xa.shape
    nb = 0 if xb is None else xb.shape[0]
    n = na + nb
    npb = n_prompt // tr
    two = xb is not None
    combine = og is not None
    in_specs = [pl.BlockSpec((tr, d), (lambda i: (jnp.minimum(i, npb - 1), 0)) if two else (lambda i: (i, 0)))]
    args = [xa]
    if two:
        in_specs.append(pl.BlockSpec((tr, d), lambda i: (jnp.maximum(i - npb, 0), 0)))
        args.append(xb)
    if combine:
        in_specs += [pl.BlockSpec((tr, og.shape[1]), lambda i: (i, 0)), pl.BlockSpec((tr, LANES), lambda i: (i, 0))]
        args += [og, meta]
    in_specs.append(pl.BlockSpec((1, d), lambda i: (0, 0)))
    args.append(g.reshape(1, d).astype(F32))
    in_specs.append(pl.BlockSpec(w.shape, lambda i: (0, 0)))
    args.append(w)
    if b is not None:
        in_specs.append(pl.BlockSpec((1, w.shape[1]), lambda i: (0, 0)))
        args.append(b.reshape(1, -1).astype(F32))
    out_shapes, out_specs = [], []
    if combine:
        out_shapes.append(jax.ShapeDtypeStruct((n, d), F32))
        out_specs.append(pl.BlockSpec((tr, d), lambda i: (i, 0)))
    for c0, c1, layout, dt in outs:
        assert c0 % COL_CHUNK == 0 and c1 % COL_CHUNK == 0
        if layout == "flat":
            out_shapes.append(jax.ShapeDtypeStruct((n, c1 - c0), dt))
            out_specs.append(pl.BlockSpec((tr, c1 - c0), lambda i: (i, 0)))
        elif layout == "split":
            out_shapes.append(jax.ShapeDtypeStruct((n_prompt, c1 - c0), dt))
            out_specs.append(pl.BlockSpec((tr, c1 - c0), lambda i: (jnp.minimum(i, npb - 1), 0)))
            out_shapes.append(jax.ShapeDtypeStruct((n - n_prompt, c1 - c0), dt))
            out_specs.append(pl.BlockSpec((tr, c1 - c0), lambda i: (jnp.maximum(i - npb, 0), 0)))
        elif layout == "heads":
            nh = (c1 - c0) // LANES
            out_shapes.append(jax.ShapeDtypeStruct((nh, n, LANES), dt))
            out_specs.append(pl.BlockSpec((nh, tr, LANES), lambda i: (0, i, 0)))
        elif layout == "heads_sample":
            nh = (c1 - c0) // LANES
            out_shapes.append(jax.ShapeDtypeStruct((nh, n - n_prompt, LANES), dt))
            out_specs.append(pl.BlockSpec((nh, tr, LANES), lambda i: (0, jnp.maximum(i - npb, 0), 0)))
        else:
            assert layout == "heads_t"
            nh = (c1 - c0) // LANES
            out_shapes.append(jax.ShapeDtypeStruct((nh, LANES + ONES_ROWS, n), dt))
            out_specs.append(pl.BlockSpec((nh, LANES + ONES_ROWS, tr), lambda i: (0, 0, i)))
    kern = functools.partial(_norm_proj_kernel, npb, two, b is not None, combine,
                             [(c0, c1, lay) for c0, c1, lay, _ in outs])
    return pl.pallas_call(
        kern, grid=(n // tr,), in_specs=in_specs, out_specs=out_specs, out_shape=out_shapes,
        compiler_params=_cparams(("arbitrary",)), name="norm_proj")(*args)


def _attn_a_chunk(q, k, v, bias_ref, sink_ref, valid):
    nq = q.shape[0]
    outs = []
    for kh in range(A_KV_HEADS):
        qs = jnp.concatenate(
            [q[:, (kh * A_GROUP + g) * A_HD:(kh * A_GROUP + g + 1) * A_HD] for g in range(A_GROUP)], axis=0)
        kk = k[:, kh * A_HD:(kh + 1) * A_HD]
        s = lax.dot_general(qs, kk, (((1,), (1,)), ((), ())), preferred_element_type=F32) + bias_ref[kh]
        if valid is not None:
            s = jnp.where(valid, s, NEG_INF)
        sk = sink_ref[kh]
        m = jnp.maximum(jnp.max(s, axis=-1, keepdims=True), sk)
        e = jnp.exp2(s - m)
        den = jnp.sum(e, axis=-1, keepdims=True) + jnp.exp2(sk - m)
        p = (e * (1.0 / den)).astype(BF16)
        o = jnp.dot(p, v[:, kh * A_HD:(kh + 1) * A_HD], preferred_element_type=F32)
        for g in range(A_GROUP):
            outs.append(o[g * nq:(g + 1) * nq])
    return jnp.concatenate(outs, axis=1)


def _attn_a_prompt_kernel(qb, nsub, q_ref, kvc_ref, kvp_ref, bias_ref, sink_ref, o_ref, t_ref, p_ref):
    cb = pl.program_id(1)
    back = CHUNK * WIN_CHUNKS
    nk = back + qb
    nkv = A_KV_HEADS * A_HD
    kvfull = jnp.concatenate([kvp_ref[...], kvc_ref[...]], axis=0)
    k_all = kvfull[:, :nkv].astype(BF16)
    vt_all = kvfull[:, nkv:].T.astype(BF16)
    dn = (((1,), (1,)), ((), ()))
    units = [(s, kh) for s in range(nsub) for kh in range(A_KV_HEADS)]
    for s, kh in units:
        qs = jnp.concatenate([q_ref[s * qb:(s + 1) * qb, (kh * A_GROUP + g) * A_HD:(kh * A_GROUP + g + 1) * A_HD]
                              for g in range(A_GROUP)], axis=0)
        t_ref[s * A_KV_HEADS + kh] = lax.dot_general(k_all[s * qb:s * qb + nk, kh * A_HD:(kh + 1) * A_HD], qs, dn,
                                                     preferred_element_type=F32)
    for s in range(nsub):
        outs = []
        for kh in range(A_KV_HEADS):
            u = s * A_KV_HEADS + kh
            st = t_ref[u] + bias_ref[kh]
            if s == 0:
                key_pos = lax.broadcasted_iota(jnp.int32, (nk, 1), 0) + (cb * nsub * qb - back)
                st = st + jnp.where(key_pos >= 0, 0.0, NEG_INF)
            sk = sink_ref[kh]
            m = jnp.maximum(jnp.max(st, axis=0, keepdims=True), sk)
            e = jnp.exp2(st - m)
            inv_den = 1.0 / (jnp.sum(e, axis=0, keepdims=True) + jnp.exp2(sk - m))
            p_ref[u] = e.astype(BF16)
            ot = jnp.dot(vt_all[kh * A_HD:(kh + 1) * A_HD, s * qb:s * qb + nk], p_ref[u],
                         preferred_element_type=F32)
            o = (ot * inv_den).T
            for g in range(A_GROUP):
                outs.append(o[g * qb:(g + 1) * qb])
        o_ref[s * qb:(s + 1) * qb, :] = jnp.concatenate(outs, axis=1).astype(o_ref.dtype)


def _a_prompt_tables(qb, sinks):
    back = CHUNK * WIN_CHUNKS
    slopes = _alibi(A_HEADS).reshape(A_KV_HEADS, A_GROUP)
    qpos = np.arange(qb)
    kpos = np.arange(back + qb) - back
    dist = np.abs(kpos[:, None] - qpos[None, :]).astype(np.float64)
    band_lo = (qpos // CHUNK) * CHUNK - back
    in_band = (kpos[:, None] >= band_lo[None, :]) & (kpos[:, None] < band_lo[None, :] + back + CHUNK)
    bias = np.where(in_band[None, None], -slopes[:, :, None, None] * dist[None, None], -np.inf)
    bias = np.transpose(bias, (0, 2, 1, 3)).reshape(A_KV_HEADS, back + qb, A_GROUP * qb) * LOG2E
    sk = jnp.repeat(sinks.astype(F32).reshape(A_KV_HEADS, A_GROUP), qb, axis=1)[:, None, :] * LOG2E
    return jnp.asarray(bias, F32), sk


def _a_tables(qpos, kpos, sinks):
    slopes = _alibi(A_HEADS).reshape(A_KV_HEADS, A_GROUP)
    dist = np.abs(qpos[:, None] - kpos[None, :]).astype(np.float64)
    bias = -slopes[:, :, None, None] * dist
    nq = len(qpos)
    bias = jnp.asarray(bias.reshape(A_KV_HEADS, A_GROUP * nq, len(kpos)) * LOG2E, F32)
    sk = jnp.repeat(sinks.astype(F32).reshape(A_KV_HEADS, A_GROUP), nq, axis=1)[..., None] * LOG2E
    return bias, sk


def attn_a_prompt(q_all, kv_all, sinks, batch, t, qb):
    nq = A_HEADS * A_HD
    nkv2 = 2 * A_KV_HEADS * A_HD
    back = CHUNK * WIN_CHUNKS
    bias, sk = _a_prompt_tables(qb, sinks)
    nsub = max(s for s in (1, 2, 4, 8) if t % (s * qb) == 0)
    rows = nsub * qb
    nblk = t // rows
    r = rows // back
    return pl.pallas_call(
        functools.partial(_attn_a_prompt_kernel, qb, nsub),
        grid=(batch, nblk),
        in_specs=[
            pl.BlockSpec((rows, nq), lambda b, i: (b * nblk + i, 0)),
            pl.BlockSpec((rows, nkv2), lambda b, i: (b * nblk + i, 0)),
            pl.BlockSpec((back, nkv2), lambda b, i: (jnp.maximum((b * nblk + i) * r - 1, 0), 0)),
            pl.BlockSpec(bias.shape, lambda b, i: (0, 0, 0)),
            pl.BlockSpec(sk.shape, lambda b, i: (0, 0, 0)),
        ],
        out_specs=pl.BlockSpec((rows, nq), lambda b, i: (b * nblk + i, 0)),
        out_shape=jax.ShapeDtypeStruct((batch * t, nq), BF16),
        scratch_shapes=[pltpu.VMEM((nsub * A_KV_HEADS, back + qb, A_GROUP * qb), F32),
                        pltpu.VMEM((nsub * A_KV_HEADS, back + qb, A_GROUP * qb), BF16)],
        compiler_params=_cparams(("arbitrary", "arbitrary")), name="attn_a_prompt",
    )(q_all, kv_all, kv_all, bias, sk)


def _attn_a_sample_kernel(q_ref, k_ref, v_ref, bias_ref, sink_ref, o_ref):
    o = _attn_a_chunk(q_ref[...], k_ref[...].astype(BF16), v_ref[...].astype(BF16), bias_ref, sink_ref, None)
    o_ref[...] = o.astype(o_ref.dtype)


def attn_a_sample(q_all, kband, vband, sinks, row0, dec_batch, dec_t):
    nq = A_HEADS * A_HD
    s = kband.shape[1]
    past = s - dec_t
    kpos = np.concatenate([np.arange(past) - past, np.arange(dec_t)])
    bias, sk = _a_tables(np.arange(dec_t), kpos, sinks)
    blk0 = row0 // dec_t
    return pl.pallas_call(
        _attn_a_sample_kernel,
        grid=(dec_batch,),
        in_specs=[
            pl.BlockSpec((dec_t, nq), lambda b: (blk0 + b, 0)),
            pl.BlockSpec((None, s, kband.shape[2]), lambda b: (b, 0, 0)),
            pl.BlockSpec((None, s, vband.shape[2]), lambda b: (b, 0, 0)),
            pl.BlockSpec(bias.shape, lambda b: (0, 0, 0)),
            pl.BlockSpec(sk.shape, lambda b: (0, 0, 0)),
        ],
        out_specs=pl.BlockSpec((dec_t, nq), lambda b: (b, 0)),
        out_shape=jax.ShapeDtypeStruct((dec_batch * dec_t, nq), BF16),
        compiler_params=_cparams(("arbitrary",)), name="attn_a_sample",
    )(q_all, kband, vband, bias, sk)


B_Q_SCALE = B_HD ** -0.5 * float(np.log2(np.e))
LOG2E = float(np.log2(np.e))


def _stack_maps(qh):
    lane = lax.broadcasted_iota(jnp.int32, qh.shape, 1)
    zero = jnp.zeros_like(qh)
    return jnp.concatenate([jnp.where(lane < B_HD, qh, zero), jnp.where(lane >= B_HD, qh, zero)], axis=0)


def _diff_finish(acc, l, nq, lam, lam_init, subln):
    inv = 1.0 / l
    o = acc[:nq] * inv[:nq] - lam * (acc[nq:] * inv[nq:])
    ms = jnp.mean(o * o, axis=-1, keepdims=True)
    return o * lax.rsqrt(ms + SUBLN_EPS) * subln * (1.0 - lam_init)


def _attn_b_prompt_kernel(qb, nb, lam_init, pair_ref, slope_ref, lam_ref, q_ref, k_ref, vt_ref, kbias_ref,
                          subln_ref, o_ref, acc_ref, m_ref, al_ref, qq_ref, t0_ref, t1_ref, p0_ref, p1_ref):
    g = pl.program_id(1)
    lam = lam_ref[0]
    dn = (((1,), (1,)), ((), ()))
    t_refs = (t0_ref, t1_ref)
    p_refs = (p0_ref, p1_ref)
    heads = range(HEAD_GROUP)
    n_pairs = nb * (nb + 1) // 2
    slopes = [slope_ref[g * HEAD_GROUP + u] for u in heads]

    def pair(n):
        return pair_ref[2 * n], pair_ref[2 * n + 1]

    def scores(n, slot):
        j, iq = pair(n)
        ks = pl.multiple_of(j * qb, qb)
        for u in heads:
            t_refs[slot][u] = lax.dot_general(k_ref[u, pl.ds(ks, qb), :], qq_ref[u * nb + iq], dn,
                                              preferred_element_type=F32)

    def softmax_step(n, slot):
        j, iq = pair(n)
        on_diag = (j == iq).astype(jnp.int32)
        for u in heads:
            st = u * nb + iq
            m = m_ref[st, 0:1, :]
            t = t_refs[slot][u] + kbias_ref[u, on_diag]
            off = slopes[u] * jnp.full((1, 2 * qb), (j - iq) * qb, jnp.int32).astype(F32)
            m_new = jnp.maximum(m, jnp.max(t, axis=0, keepdims=True) + off)
            p_refs[slot][u] = jnp.exp2(t + (off - m_new)).astype(BF16)
            al_ref[slot * HEAD_GROUP + u, 0:1, :] = jnp.exp2(m - m_new)
            m_ref[st, 0:1, :] = m_new

    def accumulate(n, slot):
        j, iq = pair(jnp.maximum(n, 0))
        ks = pl.multiple_of(j * qb, qb)
        for u in heads:
            st = u * nb + iq
            pv = jnp.dot(vt_ref[u, :, pl.ds(ks, qb)], p_refs[slot][u], preferred_element_type=F32)
            acc_ref[st] = al_ref[slot * HEAD_GROUP + u, 0:1, :] * acc_ref[st] + pv

    def step(n, slot):
        softmax_step(n, slot)
        accumulate(n - 1, 1 - slot)
        scores(jnp.minimum(n + 1, n_pairs - 1), 1 - slot)
        return 0

    acc_ref[...] = jnp.zeros_like(acc_ref)
    m_ref[...] = jnp.full(m_ref.shape, NEG_INF, F32)
    al_ref[...] = jnp.ones_like(al_ref)
    p1_ref[...] = jnp.zeros_like(p1_ref)
    for u in heads:
        for iq in range(nb):
            qq_ref[u * nb + iq] = _stack_maps(q_ref[u, iq * qb:(iq + 1) * qb, :])
    scores(0, 0)
    def steps(k, c):
        for r in range(PAIR_UNROLL):
            step(PAIR_UNROLL * k + r, r % 2)
        return c

    lax.fori_loop(0, n_pairs // PAIR_UNROLL, steps, 0)
    for n in range(n_pairs - n_pairs % PAIR_UNROLL, n_pairs):
        step(n, n % 2)
    accumulate(n_pairs - 1, (n_pairs - 1) % 2)
    for u in heads:
        for iq in range(nb):
            inv = 1.0 / acc_ref[u * nb + iq, LANES:LANES + 1, :]
            acc = acc_ref[u * nb + iq, :LANES, :]
            ot = acc[:, :qb] * inv[:, :qb] - lam * (acc[:, qb:] * inv[:, qb:])
            ms = jnp.mean(ot * ot, axis=0, keepdims=True)
            ot = ot * lax.rsqrt(ms + SUBLN_EPS) * subln_ref[...] * (1.0 - lam_init)
            o_ref[u, iq * qb:(iq + 1) * qb, :] = ot.T.astype(o_ref.dtype)


def attn_b_prompt(qkvh, vt, lam, subln, lam_init, batch, t, qb):
    nblk = t // qb
    qq = np.arange(qb)
    slope2 = (_alibi(B_HEADS) * LOG2E)[:, None, None]
    kcol = np.broadcast_to(qq[:, None], (qb, qb)).astype(np.float64)
    tri = -2.0 * np.maximum(qq[:, None] - qq[None, :], 0)
    msk = np.where((qq[:, None] // CHUNK) <= (qq[None, :] // CHUNK), 0.0, -np.inf)
    both = lambda a: np.concatenate([a, a], axis=-1)
    kbias = np.stack([both(slope2 * kcol[None]), both(slope2 * (kcol + tri)[None] + msk[None])], axis=1)
    kbias = jnp.asarray(kbias, F32)
    slopes = jnp.asarray(_alibi(B_HEADS) * LOG2E, F32)
    subln_t = jnp.broadcast_to(subln.astype(F32).reshape(LANES, 1), (LANES, qb))
    pairs = jnp.asarray([v for j in range(nblk) for i in range(j, nblk) for v in (j, i)], jnp.int32)
    hg = HEAD_GROUP
    n_groups = B_HEADS // hg
    grid_spec = pltpu.PrefetchScalarGridSpec(
        num_scalar_prefetch=3, grid=(batch, n_groups),
        in_specs=[
            pl.BlockSpec((hg, t, LANES), lambda b, g, *_: (g, b, 0)),
            pl.BlockSpec((hg, t, LANES), lambda b, g, *_: (n_groups + g, b, 0)),
            pl.BlockSpec((hg, LANES + ONES_ROWS, t), lambda b, g, *_: (g, 0, b)),
            pl.BlockSpec((hg,) + kbias.shape[1:], lambda b, g, *_: (g, 0, 0, 0)),
            pl.BlockSpec(subln_t.shape, lambda b, g, *_: (0, 0)),
        ],
        out_specs=pl.BlockSpec((hg, t, LANES), lambda b, g, *_: (g, b, 0)),
        scratch_shapes=[pltpu.VMEM((hg * nblk, LANES + ONES_ROWS, 2 * qb), F32),
                        pltpu.VMEM((hg * nblk, SUBLANES, 2 * qb), F32),
                        pltpu.VMEM((2 * hg, SUBLANES, 2 * qb), F32),
                        pltpu.VMEM((hg * nblk, 2 * qb, LANES), BF16),
                        pltpu.VMEM((hg, qb, 2 * qb), F32), pltpu.VMEM((hg, qb, 2 * qb), F32),
                        pltpu.VMEM((hg, qb, 2 * qb), BF16), pltpu.VMEM((hg, qb, 2 * qb), BF16)],
    )
    return pl.pallas_call(
        functools.partial(_attn_b_prompt_kernel, qb, nblk, lam_init), grid_spec=grid_spec,
        out_shape=jax.ShapeDtypeStruct((B_HEADS, batch * t, LANES), BF16),
        compiler_params=_cparams(("arbitrary", "arbitrary")), name="attn_b_prompt",
    )(pairs, slopes, lam.reshape(1).astype(F32), qkvh, qkvh, vt, kbias, subln_t)


def _attn_b_sample_kernel(dec_t, past, lam_init, slopes, lam_ref, q_ref, kn_ref, vn_ref, ck_ref, cv_ref,
                          tnew_ref, subln_ref, o_ref):
    lam = lam_ref[0]
    subln = subln_ref[...]
    colc = lax.broadcasted_iota(jnp.int32, (1, past), 1).astype(F32) - float(past)
    for h in range(B_HEADS):
        slope = float(slopes[h]) * LOG2E
        qq = _stack_maps(q_ref[h])
        kc = ck_ref[pl.ds(h, past, stride=B_HEADS), :].astype(BF16)
        vc = cv_ref[pl.ds(h, past, stride=B_HEADS), :].astype(BF16)
        dn = (((1,), (1,)), ((), ()))
        s_c = lax.dot_general(qq, kc, dn, preferred_element_type=F32) + slope * colc
        s_n = lax.dot_general(qq, kn_ref[h], dn, preferred_element_type=F32) + slope * tnew_ref[...]
        m = jnp.maximum(jnp.max(s_c, axis=-1, keepdims=True), jnp.max(s_n, axis=-1, keepdims=True))
        p_c = jnp.exp2(s_c - m)
        p_n = jnp.exp2(s_n - m)
        l = jnp.sum(p_c, axis=-1, keepdims=True) + jnp.sum(p_n, axis=-1, keepdims=True)
        acc = (jnp.dot(p_c.astype(BF16), vc, preferred_element_type=F32)
               + jnp.dot(p_n.astype(BF16), vn_ref[h], preferred_element_type=F32))
        o_ref[h] = _diff_finish(acc, l, dec_t, lam, lam_init, subln).astype(o_ref.dtype)


def attn_b_sample(qkh, vh_s, cache_k, cache_v, lam, subln, lam_init, row0, dec_batch, dec_t):
    past = cache_k.shape[1]
    qi = np.arange(dec_t)
    tnew = (qi[:, None] - np.abs(qi[:, None] - qi[None, :])).astype(np.float32)
    tnew = jnp.asarray(np.concatenate([tnew, tnew], axis=0))
    blk0 = row0 // dec_t
    rows = past * B_HEADS
    cache_k = cache_k.reshape(dec_batch, rows, LANES)
    cache_v = cache_v.reshape(dec_batch, rows, LANES)
    grid_spec = pltpu.PrefetchScalarGridSpec(
        num_scalar_prefetch=1, grid=(dec_batch,),
        in_specs=[
            pl.BlockSpec((B_HEADS, dec_t, LANES), lambda b, *_: (0, blk0 + b, 0)),
            pl.BlockSpec((B_HEADS, dec_t, LANES), lambda b, *_: (1, blk0 + b, 0)),
            pl.BlockSpec((B_HEADS, dec_t, LANES), lambda b, *_: (0, b, 0)),
            pl.BlockSpec((None, rows, LANES), lambda b, *_: (b, 0, 0)),
            pl.BlockSpec((None, rows, LANES), lambda b, *_: (b, 0, 0)),
            pl.BlockSpec(tnew.shape, lambda b, *_: (0, 0)),
            pl.BlockSpec((1, LANES), lambda b, *_: (0, 0)),
        ],
        out_specs=pl.BlockSpec((B_HEADS, dec_t, LANES), lambda b, *_: (0, b, 0)),
    )
    return pl.pallas_call(
        functools.partial(_attn_b_sample_kernel, dec_t, past, lam_init, tuple(_alibi(B_HEADS))),
        grid_spec=grid_spec,
        out_shape=jax.ShapeDtypeStruct((B_HEADS, dec_batch * dec_t, LANES), BF16),
        compiler_params=_cparams(("arbitrary",)), name="attn_b_sample",
    )(lam.reshape(1).astype(F32), qkh, qkh, vh_s, cache_k, cache_v, tnew, subln.reshape(1, LANES).astype(F32))


ROUTE_ROWS = 40
ROUTE_FIELDS = 8


def _route_t(lt):
    row = lax.broadcasted_iota(jnp.int32, lt.shape, 0).astype(F32)
    big = float(LANES)
    lg = jnp.where(row < N_GROUPS, lt, NEG_INF)
    gmax = jnp.max(lg, axis=0, keepdims=True)
    g_idx = jnp.min(jnp.where(lg == gmax, row, big), axis=0, keepdims=True)
    g_prob = 1.0 / jnp.sum(jnp.exp(lg - gmax), axis=0, keepdims=True)
    lo = N_GROUPS + EXPERTS_PER_GROUP * g_idx
    le = jnp.where((row >= lo) & (row < lo + EXPERTS_PER_GROUP), lt, NEG_INF)
    m1 = jnp.max(le, axis=0, keepdims=True)
    i1 = jnp.min(jnp.where(le == m1, row, big), axis=0, keepdims=True)
    le2 = jnp.where(row == i1, NEG_INF, le)
    m2 = jnp.max(le2, axis=0, keepdims=True)
    i2 = jnp.min(jnp.where(le2 == m2, row, big), axis=0, keepdims=True)
    t = jnp.exp(m2 - m1)
    ga = 1.0 / (1.0 + t)
    gb = t * ga
    return row, i1, i2, g_prob * ga, g_prob * gb


def _unpack_pairs(u):
    lo = pltpu.bitcast(u << 16, F32)
    hi = pltpu.bitcast(u & jnp.uint32(0xFFFF0000), F32)
    return jnp.concatenate([lo, hi], axis=1)


def _pack_pairs(x):
    bits = pltpu.bitcast(x.astype(BF16).astype(F32), jnp.uint32)
    w = x.shape[1] // 2
    return (bits[:, :w] >> 16) | bits[:, w:]


def _out_router_kernel(n_prompt_blocks, heads_in, two_o, two_res, *refs):
    refs = list(refs)
    oa_ref = refs.pop(0)
    ob_ref = refs.pop(0) if two_o else None
    ra_ref = refs.pop(0)
    rb_ref = refs.pop(0) if two_res else None
    (wo_ref, g_ref, wr_ref, wr2_ref, br_ref, utri_ref, h_ref, xn_ref, meta_ref, route_ref, cnt_ref,
     carry_ref) = refs
    is_prompt = pl.program_id(0) < n_prompt_blocks

    def load_o(ref):
        if heads_in:
            return jnp.concatenate([ref[hh] for hh in range(ref.shape[0])], axis=1)
        return ref[...]

    o = load_o(oa_ref)
    if two_o:
        o = jnp.where(is_prompt, o, load_o(ob_ref))
    res = ra_ref[...]
    if two_res:
        res = jnp.where(is_prompt, res, rb_ref[...])
    h = res + jnp.dot(o, wo_ref[...], preferred_element_type=F32)
    h_ref[...] = h
    ms = jnp.mean(h * h, axis=-1, keepdims=True)
    xn = h * lax.rsqrt(ms + NORM_EPS) * g_ref[...]
    x_hi = xn.astype(BF16)
    _store_planes(xn_ref, _pack_pairs(xn))
    x_lo = (xn - x_hi.astype(F32)).astype(BF16)
    r = jnp.dot(x_hi, wr_ref[...], preferred_element_type=F32)
    logits = (r[:, :LANES] + r[:, LANES:]) + jnp.dot(x_lo, wr2_ref[...], preferred_element_type=F32) + br_ref[...]
    row, i1, i2, g1, g2 = _route_t(logits.T[:ROUTE_ROWS])

    @pl.when(pl.program_id(0) == 0)
    def _():
        carry_ref[...] = jnp.zeros_like(carry_ref)

    sel1 = row == i1
    sel2 = row == i2
    onehot = jnp.where(sel1 | sel2, 1.0, 0.0)
    before = jnp.dot(onehot.astype(BF16), utri_ref[...], preferred_element_type=F32) + carry_ref[:, 0:1]
    rank1 = jnp.sum(jnp.where(sel1, before, 0.0), axis=0, keepdims=True)
    rank2 = jnp.sum(jnp.where(sel2, before, 0.0), axis=0, keepdims=True)
    carry_ref[...] = carry_ref[...] + jnp.sum(onehot, axis=1, keepdims=True)
    cnt_ref[...] = carry_ref[...]
    fields = [i1 - N_GROUPS, i2 - N_GROUPS, g1, g2, rank1, rank2]
    fields += [jnp.zeros_like(g1)] * (ROUTE_FIELDS - len(fields))
    route = jnp.concatenate(fields, axis=0)
    route_ref[...] = route
    meta_ref[...] = jnp.concatenate([route, jnp.zeros((LANES - ROUTE_FIELDS, route.shape[1]), F32)], axis=0).T


def out_router(oa, ob, ra, rb, wo, g, w_rg, b_rg, w_re, b_re, heads_in, tr):
    d = ra.shape[1]
    if heads_in:
        na = oa.shape[1]
        nb = 0 if ob is None else ob.shape[1]
    else:
        na = oa.shape[0]
        nb = 0 if ob is None else ob.shape[0]
    n = na + nb
    npb = na // tr
    two_o = ob is not None
    two_res = rb is not None
    first = lambda i: (jnp.minimum(i, npb - 1), 0)
    second = lambda i: (jnp.maximum(i - npb, 0), 0)
    plain = lambda i: (i, 0)
    in_specs, args = [], []

    def add_o(x, imap):
        if heads_in:
            in_specs.append(pl.BlockSpec((x.shape[0], tr, LANES), lambda i: (0, imap(i)[0], 0)))
        else:
            in_specs.append(pl.BlockSpec((tr, x.shape[1]), imap))
        args.append(x)

    add_o(oa, first if two_o else plain)
    if two_o:
        add_o(ob, second)
    in_specs.append(pl.BlockSpec((tr, d), first if two_res else plain))
    args.append(ra)
    if two_res:
        in_specs.append(pl.BlockSpec((tr, d), second))
        args.append(rb)
    wr = jnp.zeros((d, LANES), F32).at[:, :N_GROUPS].set(w_rg.astype(F32))
    wr = wr.at[:, N_GROUPS:N_GROUPS + N_EXPERTS].set(w_re.astype(F32))
    wr_hi = wr.astype(BF16)
    wr_lo = (wr - wr_hi.astype(F32)).astype(BF16)
    br = jnp.zeros((1, LANES), F32).at[0, :N_GROUPS].set(b_rg.astype(F32))
    br = br.at[0, N_GROUPS:N_GROUPS + N_EXPERTS].set(b_re.astype(F32))
    utri = jnp.asarray(np.triu(np.ones((tr, tr), np.float32), 1), BF16)
    consts = [wo, g.reshape(1, d).astype(F32), jnp.concatenate([wr_hi, wr_lo], axis=1), wr_hi, br, utri]
    for cst in consts:
        in_specs.append(pl.BlockSpec(cst.shape, lambda i: (0, 0)))
        args.append(cst)
    parts = d // 2 // SC_ROW_WORDS
    out_shape = [jax.ShapeDtypeStruct((n, d), F32), jax.ShapeDtypeStruct((parts, n, SC_ROW_WORDS), jnp.uint32),
                 jax.ShapeDtypeStruct((n, LANES), F32), jax.ShapeDtypeStruct((ROUTE_FIELDS, n), F32),
                 jax.ShapeDtypeStruct((ROUTE_ROWS, LANES), F32)]
    out_specs = [pl.BlockSpec((tr, d), plain), pl.BlockSpec((parts, tr, SC_ROW_WORDS), lambda i: (0, i, 0)),
                 pl.BlockSpec((tr, LANES), plain), pl.BlockSpec((ROUTE_FIELDS, tr), lambda i: (0, i)),
                 pl.BlockSpec((ROUTE_ROWS, LANES), lambda i: (0, 0))]
    return pl.pallas_call(
        functools.partial(_out_router_kernel, npb, heads_in, two_o, two_res),
        grid=(n // tr,), in_specs=in_specs, out_specs=out_specs, out_shape=out_shape,
        scratch_shapes=[pltpu.VMEM((ROUTE_ROWS, LANES), F32)],
        compiler_params=_cparams(("arbitrary",)), name="out_router")(*args)


def _expert_kernel(tm, tps, te_ref, nt_ref, x_ref, *refs):
    w_refs, o_ref = refs[:3 * tps], refs[3 * tps]
    first = pl.program_id(0) * tps
    n_valid = jnp.clip(nt_ref[0] - first, 0, tps)

    def tile(k):
        wg_ref, wu_ref, wd_ref = w_refs[3 * k:3 * k + 3]
        rows = slice(k * tm, (k + 1) * tm)
        x = _unpack_pairs(jnp.concatenate([x_ref[q, rows, :] for q in range(x_ref.shape[0])], axis=1)).astype(BF16)
        a = jnp.dot(x, wg_ref[...].astype(BF16), preferred_element_type=F32)
        u = jnp.dot(x, wu_ref[...].astype(BF16), preferred_element_type=F32)
        hh = (a * (1.0 / (1.0 + jnp.exp(-a))) * u).astype(BF16)
        packed = _pack_pairs(jnp.dot(hh, wd_ref[...].astype(BF16), preferred_element_type=F32))
        for q in range(o_ref.shape[0]):
            o_ref[q, rows, :] = packed[:, q * o_ref.shape[2]:(q + 1) * o_ref.shape[2]]

    for nv in range(tps + 1):

        @pl.when(n_valid == nv)
        def _(nv=nv):
            for k in range(nv):
                tile(k)
            if nv < tps:
                o_ref[:, nv * tm:, :] = jnp.zeros((o_ref.shape[0], (tps - nv) * tm, o_ref.shape[2]), o_ref.dtype)


def expert_mlp(xs, tile_expert, n_tiles, w_gate, w_up, w_down, layer, tm):
    parts, p, w = xs.shape
    d, f = w_gate.shape[-2:]
    epg = w_gate.shape[2]
    tps = 2 if (p // tm) % 2 == 0 else 1
    w_specs, w_args = [], []
    for k in range(tps):
        wmap = lambda t, te, nt, k=k: (layer, te[tps * t + k] // epg, te[tps * t + k] % epg, 0, 0)
        w_specs += [pl.BlockSpec((None, None, None, d, f), wmap), pl.BlockSpec((None, None, None, d, f), wmap),
                    pl.BlockSpec((None, None, None, f, d), wmap)]
        w_args += [w_gate, w_up, w_down]
    grid_spec = pltpu.PrefetchScalarGridSpec(
        num_scalar_prefetch=2, grid=(p // (tps * tm),),
        in_specs=[pl.BlockSpec((parts, tps * tm, w), lambda t, te, nt: (0, t, 0))] + w_specs,
        out_specs=pl.BlockSpec((parts, tps * tm, w), lambda t, te, nt: (0, t, 0)),
    )
    return pl.pallas_call(
        functools.partial(_expert_kernel, tm, tps), grid_spec=grid_spec,
        out_shape=jax.ShapeDtypeStruct((parts, p, w), jnp.uint32),
        compiler_params=_cparams(("arbitrary",)), name="expert_mlp")(tile_expert, n_tiles, xs, *w_args)


def sc_scatter_rows(x2, idx, n_out):
    mesh = plsc.VectorSubcoreMesh(core_axis_name="c", subcore_axis_name="s")
    n_src_win = x2.shape[0] // SC_WINDOW

    @functools.partial(pl.kernel, out_type=jax.ShapeDtypeStruct((n_out, SC_ROW_WORDS), x2.dtype), mesh=mesh)
    def scatter(x_hbm, i_hbm, o_hbm):
        def body(x_vmem, i_vmem):
            pltpu.sync_copy(x_vmem, o_hbm.at[i_vmem.at[0]])

        pltpu.emit_pipeline(
            body, grid=(n_out // SC_WINDOW,),
            in_specs=[pl.BlockSpec((SC_WINDOW, SC_ROW_WORDS), lambda i: (lax.rem(i, n_src_win), 0)),
                      pl.BlockSpec((1, SC_WINDOW), lambda i: (0, i))],
            out_specs=[], core_axis_name=("c", "s"), dimension_semantics=(pltpu.PARALLEL,),
        )(x_hbm, i_hbm)

    return scatter(x2, idx)


def sc_gather_rows(x2, idx):
    mesh = plsc.VectorSubcoreMesh(core_axis_name="c", subcore_axis_name="s")
    n_col, n = idx.shape

    @functools.partial(pl.kernel, out_type=jax.ShapeDtypeStruct((n, n_col * SC_ROW_WORDS), x2.dtype), mesh=mesh)
    def gather(x_hbm, i_hbm, o_hbm):
        def body(i_vmem, o_vmem):
            pltpu.sync_copy(x_hbm.at[i_vmem.at[0]], o_vmem)

        pltpu.emit_pipeline(
            body, grid=(n // SC_WINDOW, n_col),
            in_specs=[pl.BlockSpec((1, SC_WINDOW), lambda i, c: (c, i))],
            out_specs=[pl.BlockSpec((SC_WINDOW, SC_ROW_WORDS), lambda i, c: (i, c))],
            core_axis_name=("c", "s"), dimension_semantics=(pltpu.PARALLEL, pltpu.ARBITRARY),
        )(i_hbm, o_hbm)

    return gather(x2, idx)


def moe_dispatch(xn_packed, route, counts, tm):
    parts, n, w = xn_packed.shape
    e = route[0:2].astype(jnp.int32)
    rank = route[4:6].astype(jnp.int32)
    counts = counts[N_GROUPS:N_GROUPS + N_EXPERTS, 0].astype(jnp.int32)
    padded = ((counts + tm - 1) // tm) * tm
    pend = jnp.cumsum(padded)
    pstart = pend - padded
    ids = jnp.arange(N_EXPERTS, dtype=jnp.int32)
    pos = jnp.sum(jnp.where(e[..., None] == ids, pstart, 0), axis=-1) + rank
    p = 2 * n + N_EXPERTS * tm
    n_pad = p - 2 * n
    gap_len = jnp.concatenate([padded - counts, (p - pend[-1]).reshape(1)])
    gap_first = jnp.concatenate([pstart + counts, pend[-1:]])
    gap_end = jnp.cumsum(gap_len)
    k = jnp.arange(n_pad, dtype=jnp.int32)
    gap = jnp.sum((k[:, None] >= gap_end[None, :]).astype(jnp.int32), axis=1)
    shift = gap_first - (gap_end - gap_len)
    pad_pos = k + jnp.sum(jnp.where(gap[:, None] == jnp.arange(N_EXPERTS + 1), shift, 0), axis=-1)
    planes = lambda rows: jnp.concatenate([rows + q * p for q in range(parts)])
    scat_idx = jnp.concatenate([planes(pos[0]), planes(pos[1]), planes(pad_pos)]).reshape(1, parts * p)
    gath_idx = jnp.stack([pos[k] + q * p for k in range(2) for q in range(parts)])
    xs = sc_scatter_rows(xn_packed.reshape(parts * n, w), scat_idx, parts * p).reshape(parts, p, w)
    tile_start = jnp.arange(p // tm, dtype=jnp.int32) * tm
    tile_expert = jnp.sum((tile_start[:, None] >= pend[None, :]).astype(jnp.int32), axis=1)
    tile_expert = jnp.minimum(tile_expert, N_EXPERTS - 1)
    n_tiles = (pend[-1] // tm).astype(jnp.int32).reshape(1)
    return xs, tile_expert, n_tiles, gath_idx


def moe_experts(xn_packed, route, counts, w_gate, w_up, w_down, layer, tm):
    parts, n, w = xn_packed.shape
    xs, tile_expert, n_tiles, gath_idx = moe_dispatch(xn_packed, route, counts, tm)
    o = expert_mlp(xs, tile_expert, n_tiles, w_gate, w_up, w_down, layer, tm)
    return sc_gather_rows(o.reshape(parts * o.shape[1], w), gath_idx)


def _final_norm_kernel(n_prompt_blocks, h_ref, o0_ref, o1_ref, meta_ref, g_ref, yp_ref, ys_ref):
    meta = meta_ref[...]
    x = (h_ref[...] + meta[:, 2:3] * _unpack_pairs(o0_ref[...])) + meta[:, 3:4] * _unpack_pairs(o1_ref[...])
    ms = jnp.mean(x * x, axis=-1, keepdims=True)
    y = x * lax.rsqrt(ms + NORM_EPS) * g_ref[...]
    i = pl.program_id(0)

    @pl.when(i < n_prompt_blocks)
    def _():
        yp_ref[...] = y

    @pl.when(i >= n_prompt_blocks)
    def _():
        ys_ref[...] = y


def final_norm(h, og, meta, g, n_prompt, tr):
    n, d = h.shape
    npb = n_prompt // tr
    return pl.pallas_call(
        functools.partial(_final_norm_kernel, npb),
        grid=(n // tr,),
        in_specs=[pl.BlockSpec((tr, d), lambda i: (i, 0)),
                  pl.BlockSpec((tr, og.shape[1] // 2), lambda i: (i, 0)),
                  pl.BlockSpec((tr, og.shape[1] // 2), lambda i: (i, 1)),
                  pl.BlockSpec((tr, LANES), lambda i: (i, 0)), pl.BlockSpec((1, d), lambda i: (0, 0))],
        out_specs=[pl.BlockSpec((tr, d), lambda i: (jnp.minimum(i, npb - 1), 0)),
                   pl.BlockSpec((tr, d), lambda i: (jnp.maximum(i - npb, 0), 0))],
        out_shape=[jax.ShapeDtypeStruct((n_prompt, d), F32), jax.ShapeDtypeStruct((n - n_prompt, d), F32)],
        compiler_params=_cparams(("arbitrary",)), name="final_norm")(h, og, og, meta, g.reshape(1, d).astype(F32))


def _pick_rows_tile(n_prompt, n_sample):
    tr = ROW_TILE
    while tr >= SUBLANES:
        if n_prompt % tr == 0 and n_sample % tr == 0:
            return tr
        tr //= 2
    raise ValueError("row counts must be multiples of the sublane count")


def kernel(x_prompt, x_sample, cache_a_k, cache_a_v, cache_b_k, cache_b_v, norm_mix, norm_ffn, norm_final,
           w_a_qkv, b_a_qkv, a_sinks, w_a_o, w_b_qkv, b_lambda, b_subln, w_b_o,
           w_route_group, b_route_group, w_route_expert, b_route_expert, w_gate, w_up, w_down):
    batch, t, d = x_prompt.shape
    dec_batch, dec_t, _ = x_sample.shape
    n_p, n_s = batch * t, dec_batch * dec_t
    tr = _pick_rows_tile(n_p, n_s)
    tm = min(EXPERT_TILE, 4 * tr)
    xp = x_prompt.reshape(n_p, d)
    xs = x_sample.reshape(n_s, d)
    nq_a = A_HEADS * A_HD
    nkv_a = A_KV_HEADS * A_HD

    a_scale = jnp.where(jnp.arange(nq_a + 2 * nkv_a) < nq_a, A_HD ** -0.5 * LOG2E, 1.0).astype(F32)
    q_a, kv_ap, kv_as = norm_proj(xp, xs, norm_mix[0], (w_a_qkv[0] * a_scale).astype(BF16), b_a_qkv[0] * a_scale,
                                  [(0, nq_a, "flat", BF16), (nq_a, nq_a + 2 * nkv_a, "split", F32)], tr, n_p)
    qb_a = min(CHUNK * WIN_CHUNKS, t)
    o_ap = attn_a_prompt(q_a, kv_ap, a_sinks[0], batch, t, qb_a)
    kv_s = kv_as.reshape(dec_batch, dec_t, 2 * nkv_a)
    past_a = cache_a_k.shape[2]
    kband = jnp.concatenate([cache_a_k[0].reshape(dec_batch, past_a, nkv_a), kv_s[..., :nkv_a]], axis=1)
    vband = jnp.concatenate([cache_a_v[0].reshape(dec_batch, past_a, nkv_a), kv_s[..., nkv_a:]], axis=1)
    o_as = attn_a_sample(q_a, kband, vband, a_sinks[0], n_p, dec_batch, dec_t)
    h, xn, meta, route, counts = out_router(o_ap, o_as, xp, xs, w_a_o[0].astype(BF16), norm_ffn[0], w_route_group[0],
                                            b_route_group[0], w_route_expert[0], b_route_expert[0], False, tr)
    og = moe_experts(xn, route, counts, w_gate, w_up, w_down, 0, tm)

    keep = min(CHUNK * WIN_CHUNKS, t)
    kv_p = kv_ap.reshape(batch, t, 2 * nkv_a)[:, t - keep:]
    a_k_prompt = kv_p[..., :nkv_a].reshape(1, batch, keep, A_KV_HEADS, A_HD)
    a_v_prompt = kv_p[..., nkv_a:].reshape(1, batch, keep, A_KV_HEADS, A_HD)
    a_k_sample = kband[:, dec_t:].reshape(1, dec_batch, past_a, A_KV_HEADS, A_HD)
    a_v_sample = vband[:, dec_t:].reshape(1, dec_batch, past_a, A_KV_HEADS, A_HD)

    lam_init = 0.8 - 0.6 * float(np.exp(-0.3 * 1))
    lp = b_lambda[0].astype(F32)
    lam = jnp.exp(jnp.sum(lp[0] * lp[1])) - jnp.exp(jnp.sum(lp[2] * lp[3])) + lam_init
    wb = B_HEADS * 2 * B_HD
    col_scale = jnp.where(jnp.arange(3 * wb) < wb, B_Q_SCALE, 1.0).astype(F32)
    h, qkh, vh_s, k_bp, k_bs, v_bp, v_bs, vt = norm_proj(
        h, None, norm_mix[1], (w_b_qkv[0] * col_scale).astype(BF16), None,
        [(0, 2 * wb, "heads", BF16), (2 * wb, 3 * wb, "heads_sample", BF16), (wb, 2 * wb, "split", F32),
         (2 * wb, 3 * wb, "split", F32), (2 * wb, 3 * wb, "heads_t", BF16)], tr, n_p, og=og, meta=meta)
    qb_b = min(ATTN_BLOCK, t)
    o_bp = attn_b_prompt(qkh, vt, lam, b_subln[0], lam_init, batch, t, qb_b)
    o_bs = attn_b_sample(qkh, vh_s, cache_b_k[0], cache_b_v[0], lam, b_subln[0], lam_init, n_p, dec_batch, dec_t)
    hd2 = 2 * B_HD
    o_bs, k_bp, v_bp = lax.optimization_barrier((o_bs, k_bp, v_bp))
    b_k_prompt = k_bp.reshape(1, batch, t, B_HEADS, hd2)
    b_v_prompt = v_bp.reshape(1, batch, t, B_HEADS, hd2)
    o_bp, b_k_prompt, b_v_prompt = lax.optimization_barrier((o_bp, b_k_prompt, b_v_prompt))
    h, xn, meta, route, counts = out_router(o_bp, o_bs, h, None, w_b_o[0].astype(BF16), norm_ffn[1], w_route_group[1],
                                            b_route_group[1], w_route_expert[1], b_route_expert[1], True, tr)
    og = moe_experts(xn, route, counts, w_gate, w_up, w_down, 1, tm)

    y_p, y_s = final_norm(h, og, meta, norm_final, n_p, tr)
    return (y_p.reshape(batch, t, d), y_s.reshape(dec_batch, dec_t, d),
            a_k_prompt, a_v_prompt, a_k_sample, a_v_sample, b_k_prompt, b_v_prompt,
            k_bs.reshape(1, dec_batch, dec_t, B_HEADS, hd2), v_bs.reshape(1, dec_batch, dec_t, B_HEADS, hd2))
```

```python
import functools

import jax
import jax.numpy as jnp
import numpy as np
from jax import lax
from jax.experimental import pallas as pl
from jax.experimental.pallas import tpu as pltpu
from jax.experimental.pallas import tpu_sc as plsc

F32 = jnp.float32
BF16 = jnp.bfloat16
NEG_INF = float("-inf")

CHUNK = 64
WIN_CHUNKS = 2
A_HEADS, A_KV_HEADS, A_GROUP, A_HD = 16, 4, 4, 64
B_HEADS, B_HD = 8, 64
N_GROUPS, EXPERTS_PER_GROUP = 4, 8
N_EXPERTS = N_GROUPS * EXPERTS_PER_GROUP
NORM_EPS = 1e-6
SUBLN_EPS = 1e-5

LANES = 128
SUBLANES = 8
ROW_TILE = 512
EXPERT_TILE = 512
ATTN_BLOCK = 256
COL_CHUNK = 512
PAIR_UNROLL = 18
HEAD_GROUP = 2
ONES_ROWS = 16
SC_WINDOW = 128
SC_ROW_WORDS = 256
VMEM_LIMIT = 48 * 1024 * 1024


def _cparams(sem, flags=None):
    return pltpu.CompilerParams(dimension_semantics=sem, vmem_limit_bytes=VMEM_LIMIT, flags=flags)


def _alibi(n):
    return 2.0 ** (-8.0 * np.arange(1, n + 1) / n)


def _store_planes(ref, x):
    w = ref.shape[2]
    for p in range(ref.shape[0]):
        ref[p] = x[:, p * w:(p + 1) * w]


def _add_expert_outputs(x, og_ref, meta_ref):
    half = og_ref.shape[1] // 2
    meta = meta_ref[...]
    return (x + meta[:, 2:3] * _unpack_pairs(og_ref[:, :half])) + meta[:, 3:4] * _unpack_pairs(og_ref[:, half:])


def _norm_proj_kernel(n_prompt_blocks, two_src, has_bias, combine, outs, *refs):
    refs = list(refs)
    xa_ref = refs.pop(0)
    xb_ref = refs.pop(0) if two_src else None
    og_ref, meta_ref = (refs.pop(0), refs.pop(0)) if combine else (None, None)
    g_ref = refs.pop(0)
    w_ref = refs.pop(0)
    b_ref = refs.pop(0) if has_bias else None
    out_refs = refs
    x = xa_ref[...]
    if two_src:
        x = jnp.where(pl.program_id(0) < n_prompt_blocks, x, xb_ref[...])
    if combine:
        x = _add_expert_outputs(x, og_ref, meta_ref)
        out_refs.pop(0)[...] = x
    ms = jnp.mean(x * x, axis=-1, keepdims=True)
    xn = (x * lax.rsqrt(ms + NORM_EPS) * g_ref[...]).astype(BF16)
    is_prompt = pl.program_id(0) < n_prompt_blocks
    out_refs = list(out_refs)
    targets = []
    for c0, c1, layout in outs:
        targets.append((c0, c1, layout, out_refs.pop(0), out_refs.pop(0) if layout == "split" else None))
    starts = list(range(0, w_ref.shape[1], COL_CHUNK))
    transposed = [s for s in starts if any(lay == "heads_t" and c0 <= s < c1 for c0, c1, lay in outs)]
    for s in transposed + [s for s in starts if s not in transposed]:
        e = s + COL_CHUNK
        r = jnp.dot(xn, w_ref[:, s:e], preferred_element_type=F32)
        if has_bias:
            r = r + b_ref[:, s:e]
        for c0, c1, layout, o_ref, o2_ref in targets:
            if not (c0 <= s and e <= c1):
                continue
            rr = r.astype(o_ref.dtype)
            if layout == "flat":
                o_ref[:, s - c0:e - c0] = rr
            elif layout == "split":

                @pl.when(is_prompt)
                def _(rr=rr, o_ref=o_ref, s=s, e=e, c0=c0):
                    o_ref[:, s - c0:e - c0] = rr

                @pl.when(jnp.logical_not(is_prompt))
                def _(rr=rr, o2_ref=o2_ref, s=s, e=e, c0=c0):
                    o2_ref[:, s - c0:e - c0] = rr
            elif layout == "heads":
                for hh in range((e - s) // LANES):
                    o_ref[(s - c0) // LANES + hh] = rr[:, hh * LANES:(hh + 1) * LANES]
            else:
                ones = jnp.ones((ONES_ROWS, r.shape[0]), o_ref.dtype)
                for hh in range((e - s) // LANES):
                    head = (s - c0) // LANES + hh
                    o_ref[head, :LANES, :] = r[:, hh * LANES:(hh + 1) * LANES].T.astype(o_ref.dtype)
                    o_ref[head, LANES:, :] = ones


def norm_proj(xa, xb, g, w, b, outs, tr, n_prompt, og=None, meta=None):
    na, d = xa.shape
    nb = 0 if xb is None else xb.shape[0]
    n = na + nb
    npb = n_prompt // tr
    two = xb is not None
    combine = og is not None
    in_specs = [pl.BlockSpec((tr, d), (lambda i: (jnp.minimum(i, npb - 1), 0)) if two else (lambda i: (i, 0)))]
    args = [xa]
    if two:
        in_specs.append(pl.BlockSpec((tr, d), lambda i: (jnp.maximum(i - npb, 0), 0)))
        args.append(xb)
    if combine:
        in_specs += [pl.BlockSpec((tr, og.shape[1]), lambda i: (i, 0)), pl.BlockSpec((tr, LANES), lambda i: (i, 0))]
        args += [og, meta]
    in_specs.append(pl.BlockSpec((1, d), lambda i: (0, 0)))
    args.append(g.reshape(1, d).astype(F32))
    in_specs.append(pl.BlockSpec(w.shape, lambda i: (0, 0)))
    args.append(w)
    if b is not None:
        in_specs.append(pl.BlockSpec((1, w.shape[1]), lambda i: (0, 0)))
        args.append(b.reshape(1, -1).astype(F32))
    out_shapes, out_specs = [], []
    if combine:
        out_shapes.append(jax.ShapeDtypeStruct((n, d), F32))
        out_specs.append(pl.BlockSpec((tr, d), lambda i: (i, 0)))
    for c0, c1, layout, dt in outs:
        assert c0 % COL_CHUNK == 0 and c1 % COL_CHUNK == 0
        if layout == "flat":
            out_shapes.append(jax.ShapeDtypeStruct((n, c1 - c0), dt))
            out_specs.append(pl.BlockSpec((tr, c1 - c0), lambda i: (i, 0)))
        elif layout == "split":
            out_shapes.append(jax.ShapeDtypeStruct((n_prompt, c1 - c0), dt))
            out_specs.append(pl.BlockSpec((tr, c1 - c0), lambda i: (jnp.minimum(i, npb - 1), 0)))
            out_shapes.append(jax.ShapeDtypeStruct((n - n_prompt, c1 - c0), dt))
            out_specs.append(pl.BlockSpec((tr, c1 - c0), lambda i: (jnp.maximum(i - npb, 0), 0)))
        elif layout == "heads":
            nh = (c1 - c0) // LANES
            out_shapes.append(jax.ShapeDtypeStruct((nh, n, LANES), dt))
            out_specs.append(pl.BlockSpec((nh, tr, LANES), lambda i: (0, i, 0)))
        else:
            assert layout == "heads_t"
            nh = (c1 - c0) // LANES
            out_shapes.append(jax.ShapeDtypeStruct((nh, LANES + ONES_ROWS, n), dt))
            out_specs.append(pl.BlockSpec((nh, LANES + ONES_ROWS, tr), lambda i: (0, 0, i)))
    kern = functools.partial(_norm_proj_kernel, npb, two, b is not None, combine,
                             [(c0, c1, lay) for c0, c1, lay, _ in outs])
    return pl.pallas_call(
        kern, grid=(n // tr,), in_specs=in_specs, out_specs=out_specs, out_shape=out_shapes,
        compiler_params=_cparams(("arbitrary",)), name="norm_proj")(*args)


def _attn_a_chunk(q, k, v, bias_ref, sink_ref, valid):
    nq = q.shape[0]
    outs = []
    for kh in range(A_KV_HEADS):
        qs = jnp.concatenate(
            [q[:, (kh * A_GROUP + g) * A_HD:(kh * A_GROUP + g + 1) * A_HD] for g in range(A_GROUP)], axis=0)
        kk = k[:, kh * A_HD:(kh + 1) * A_HD]
        s = lax.dot_general(qs, kk, (((1,), (1,)), ((), ())), preferred_element_type=F32) + bias_ref[kh]
        if valid is not None:
            s = jnp.where(valid, s, NEG_INF)
        sk = sink_ref[kh]
        m = jnp.maximum(jnp.max(s, axis=-1, keepdims=True), sk)
        e = jnp.exp2(s - m)
        den = jnp.sum(e, axis=-1, keepdims=True) + jnp.exp2(sk - m)
        p = (e * (1.0 / den)).astype(BF16)
        o = jnp.dot(p, v[:, kh * A_HD:(kh + 1) * A_HD], preferred_element_type=F32)
        for g in range(A_GROUP):
            outs.append(o[g * nq:(g + 1) * nq])
    return jnp.concatenate(outs, axis=1)


def _attn_a_prompt_kernel(qb, nsub, q_ref, kvc_ref, kvp_ref, bias_ref, sink_ref, o_ref, t_ref, p_ref):
    cb = pl.program_id(1)
    back = CHUNK * WIN_CHUNKS
    nk = back + qb
    nkv = A_KV_HEADS * A_HD
    kvfull = jnp.concatenate([kvp_ref[...], kvc_ref[...]], axis=0)
    k_all = kvfull[:, :nkv].astype(BF16)
    vt_all = kvfull[:, nkv:].T.astype(BF16)
    dn = (((1,), (1,)), ((), ()))
    units = [(s, kh) for s in range(nsub) for kh in range(A_KV_HEADS)]
    for s, kh in units:
        qs = jnp.concatenate([q_ref[s * qb:(s + 1) * qb, (kh * A_GROUP + g) * A_HD:(kh * A_GROUP + g + 1) * A_HD]
                              for g in range(A_GROUP)], axis=0)
        t_ref[s * A_KV_HEADS + kh] = lax.dot_general(k_all[s * qb:s * qb + nk, kh * A_HD:(kh + 1) * A_HD], qs, dn,
                                                     preferred_element_type=F32)
    for s in range(nsub):
        outs = []
        for kh in range(A_KV_HEADS):
            u = s * A_KV_HEADS + kh
            st = t_ref[u] + bias_ref[kh]
            if s == 0:
                key_pos = lax.broadcasted_iota(jnp.int32, (nk, 1), 0) + (cb * nsub * qb - back)
                st = st + jnp.where(key_pos >= 0, 0.0, NEG_INF)
            sk = sink_ref[kh]
            m = jnp.maximum(jnp.max(st, axis=0, keepdims=True), sk)
            e = jnp.exp2(st - m)
            inv_den = 1.0 / (jnp.sum(e, axis=0, keepdims=True) + jnp.exp2(sk - m))
            p_ref[u] = e.astype(BF16)
            ot = jnp.dot(vt_all[kh * A_HD:(kh + 1) * A_HD, s * qb:s * qb + nk], p_ref[u],
                         preferred_element_type=F32)
            o = (ot * inv_den).T
            for g in range(A_GROUP):
                outs.append(o[g * qb:(g + 1) * qb])
        o_ref[s * qb:(s + 1) * qb, :] = jnp.concatenate(outs, axis=1).astype(o_ref.dtype)


def _a_prompt_tables(qb, sinks):
    back = CHUNK * WIN_CHUNKS
    slopes = _alibi(A_HEADS).reshape(A_KV_HEADS, A_GROUP)
    qpos = np.arange(qb)
    kpos = np.arange(back + qb) - back
    dist = np.abs(kpos[:, None] - qpos[None, :]).astype(np.float64)
    band_lo = (qpos // CHUNK) * CHUNK - back
    in_band = (kpos[:, None] >= band_lo[None, :]) & (kpos[:, None] < band_lo[None, :] + back + CHUNK)
    bias = np.where(in_band[None, None], -slopes[:, :, None, None] * dist[None, None], -np.inf)
    bias = np.transpose(bias, (0, 2, 1, 3)).reshape(A_KV_HEADS, back + qb, A_GROUP * qb) * LOG2E
    sk = jnp.repeat(sinks.astype(F32).reshape(A_KV_HEADS, A_GROUP), qb, axis=1)[:, None, :] * LOG2E
    return jnp.asarray(bias, F32), sk


def _a_tables(qpos, kpos, sinks):
    slopes = _alibi(A_HEADS).reshape(A_KV_HEADS, A_GROUP)
    dist = np.abs(qpos[:, None] - kpos[None, :]).astype(np.float64)
    bias = -slopes[:, :, None, None] * dist
    nq = len(qpos)
    bias = jnp.asarray(bias.reshape(A_KV_HEADS, A_GROUP * nq, len(kpos)) * LOG2E, F32)
    sk = jnp.repeat(sinks.astype(F32).reshape(A_KV_HEADS, A_GROUP), nq, axis=1)[..., None] * LOG2E
    return bias, sk


def attn_a_prompt(q_all, kv_all, sinks, batch, t, qb):
    nq = A_HEADS * A_HD
    nkv2 = 2 * A_KV_HEADS * A_HD
    back = CHUNK * WIN_CHUNKS
    bias, sk = _a_prompt_tables(qb, sinks)
    nsub = max(s for s in (1, 2, 4, 8) if t % (s * qb) == 0)
    rows = nsub * qb
    nblk = t // rows
    r = rows // back
    return pl.pallas_call(
        functools.partial(_attn_a_prompt_kernel, qb, nsub),
        grid=(batch, nblk),
        in_specs=[
            pl.BlockSpec((rows, nq), lambda b, i: (b * nblk + i, 0)),
            pl.BlockSpec((rows, nkv2), lambda b, i: (b * nblk + i, 0)),
            pl.BlockSpec((back, nkv2), lambda b, i: (jnp.maximum((b * nblk + i) * r - 1, 0), 0)),
            pl.BlockSpec(bias.shape, lambda b, i: (0, 0, 0)),
            pl.BlockSpec(sk.shape, lambda b, i: (0, 0, 0)),
        ],
        out_specs=pl.BlockSpec((rows, nq), lambda b, i: (b * nblk + i, 0)),
        out_shape=jax.ShapeDtypeStruct((batch * t, nq), BF16),
        scratch_shapes=[pltpu.VMEM((nsub * A_KV_HEADS, back + qb, A_GROUP * qb), F32),
                        pltpu.VMEM((nsub * A_KV_HEADS, back + qb, A_GROUP * qb), BF16)],
        compiler_params=_cparams(("arbitrary", "arbitrary")), name="attn_a_prompt",
    )(q_all, kv_all, kv_all, bias, sk)


def _attn_a_sample_kernel(q_ref, k_ref, v_ref, bias_ref, sink_ref, o_ref):
    o = _attn_a_chunk(q_ref[...], k_ref[...].astype(BF16), v_ref[...].astype(BF16), bias_ref, sink_ref, None)
    o_ref[...] = o.astype(o_ref.dtype)


def attn_a_sample(q_all, kband, vband, sinks, row0, dec_batch, dec_t):
    nq = A_HEADS * A_HD
    s = kband.shape[1]
    past = s - dec_t
    kpos = np.concatenate([np.arange(past) - past, np.arange(dec_t)])
    bias, sk = _a_tables(np.arange(dec_t), kpos, sinks)
    blk0 = row0 // dec_t
    return pl.pallas_call(
        _attn_a_sample_kernel,
        grid=(dec_batch,),
        in_specs=[
            pl.BlockSpec((dec_t, nq), lambda b: (blk0 + b, 0)),
            pl.BlockSpec((None, s, kband.shape[2]), lambda b: (b, 0, 0)),
            pl.BlockSpec((None, s, vband.shape[2]), lambda b: (b, 0, 0)),
            pl.BlockSpec(bias.shape, lambda b: (0, 0, 0)),
            pl.BlockSpec(sk.shape, lambda b: (0, 0, 0)),
        ],
        out_specs=pl.BlockSpec((dec_t, nq), lambda b: (b, 0)),
        out_shape=jax.ShapeDtypeStruct((dec_batch * dec_t, nq), BF16),
        compiler_params=_cparams(("arbitrary",)), name="attn_a_sample",
    )(q_all, kband, vband, bias, sk)


B_Q_SCALE = B_HD ** -0.5 * float(np.log2(np.e))
LOG2E = float(np.log2(np.e))


def _stack_maps(qh):
    lane = lax.broadcasted_iota(jnp.int32, qh.shape, 1)
    zero = jnp.zeros_like(qh)
    return jnp.concatenate([jnp.where(lane < B_HD, qh, zero), jnp.where(lane >= B_HD, qh, zero)], axis=0)


def _diff_finish(acc, l, nq, lam, lam_init, subln):
    inv = 1.0 / l
    o = acc[:nq] * inv[:nq] - lam * (acc[nq:] * inv[nq:])
    ms = jnp.mean(o * o, axis=-1, keepdims=True)
    return o * lax.rsqrt(ms + SUBLN_EPS) * subln * (1.0 - lam_init)


def _attn_b_prompt_kernel(qb, nb, lam_init, pair_ref, slope_ref, lam_ref, q_ref, k_ref, vt_ref, kbias_ref,
                          subln_ref, o_ref, acc_ref, m_ref, al_ref, qq_ref, t0_ref, t1_ref, p0_ref, p1_ref):
    g = pl.program_id(1)
    lam = lam_ref[0]
    dn = (((1,), (1,)), ((), ()))
    t_refs = (t0_ref, t1_ref)
    p_refs = (p0_ref, p1_ref)
    heads = range(HEAD_GROUP)
    n_pairs = nb * (nb + 1) // 2
    slopes = [slope_ref[g * HEAD_GROUP + u] for u in heads]

    def pair(n):
        return pair_ref[2 * n], pair_ref[2 * n + 1]

    def scores(n, slot):
        j, iq = pair(n)
        ks = pl.multiple_of(j * qb, qb)
        for u in heads:
            t_refs[slot][u] = lax.dot_general(k_ref[u, pl.ds(ks, qb), :], qq_ref[u * nb + iq], dn,
                                              preferred_element_type=F32)

    def softmax_step(n, slot):
        j, iq = pair(n)
        on_diag = (j == iq).astype(jnp.int32)
        for u in heads:
            st = u * nb + iq
            m = m_ref[st, 0:1, :]
            t = t_refs[slot][u] + kbias_ref[u, on_diag]
            off = slopes[u] * jnp.full((1, 2 * qb), (j - iq) * qb, jnp.int32).astype(F32)
            m_new = jnp.maximum(m, jnp.max(t, axis=0, keepdims=True) + off)
            p_refs[slot][u] = jnp.exp2(t + (off - m_new)).astype(BF16)
            al_ref[slot * HEAD_GROUP + u, 0:1, :] = jnp.exp2(m - m_new)
            m_ref[st, 0:1, :] = m_new

    def accumulate(n, slot):
        j, iq = pair(jnp.maximum(n, 0))
        ks = pl.multiple_of(j * qb, qb)
        for u in heads:
            st = u * nb + iq
            pv = jnp.dot(vt_ref[u, :, pl.ds(ks, qb)], p_refs[slot][u], preferred_element_type=F32)
            acc_ref[st] = al_ref[slot * HEAD_GROUP + u, 0:1, :] * acc_ref[st] + pv

    def step(n, slot):
        softmax_step(n, slot)
        accumulate(n - 1, 1 - slot)
        scores(jnp.minimum(n + 1, n_pairs - 1), 1 - slot)
        return 0

    acc_ref[...] = jnp.zeros_like(acc_ref)
    m_ref[...] = jnp.full(m_ref.shape, NEG_INF, F32)
    al_ref[...] = jnp.ones_like(al_ref)
    p1_ref[...] = jnp.zeros_like(p1_ref)
    for u in heads:
        for iq in range(nb):
            qq_ref[u * nb + iq] = _stack_maps(q_ref[u, iq * qb:(iq + 1) * qb, :])
    scores(0, 0)
    def steps(k, c):
        for r in range(PAIR_UNROLL):
            step(PAIR_UNROLL * k + r, r % 2)
        return c

    lax.fori_loop(0, n_pairs // PAIR_UNROLL, steps, 0)
    for n in range(n_pairs - n_pairs % PAIR_UNROLL, n_pairs):
        step(n, n % 2)
    accumulate(n_pairs - 1, (n_pairs - 1) % 2)
    for u in heads:
        for iq in range(nb):
            inv = 1.0 / acc_ref[u * nb + iq, LANES:LANES + 1, :]
            acc = acc_ref[u * nb + iq, :LANES, :]
            ot = acc[:, :qb] * inv[:, :qb] - lam * (acc[:, qb:] * inv[:, qb:])
            ms = jnp.mean(ot * ot, axis=0, keepdims=True)
            ot = ot * lax.rsqrt(ms + SUBLN_EPS) * subln_ref[...] * (1.0 - lam_init)
            o_ref[u, iq * qb:(iq + 1) * qb, :] = ot.T.astype(o_ref.dtype)


def attn_b_prompt(qkvh, vt, lam, subln, lam_init, batch, t, qb):
    nblk = t // qb
    qq = np.arange(qb)
    slope2 = (_alibi(B_HEADS) * LOG2E)[:, None, None]
    kcol = np.broadcast_to(qq[:, None], (qb, qb)).astype(np.float64)
    tri = -2.0 * np.maximum(qq[:, None] - qq[None, :], 0)
    msk = np.where((qq[:, None] // CHUNK) <= (qq[None, :] // CHUNK), 0.0, -np.inf)
    both = lambda a: np.concatenate([a, a], axis=-1)
    kbias = np.stack([both(slope2 * kcol[None]), both(slope2 * (kcol + tri)[None] + msk[None])], axis=1)
    kbias = jnp.asarray(kbias, F32)
    slopes = jnp.asarray(_alibi(B_HEADS) * LOG2E, F32)
    subln_t = jnp.broadcast_to(subln.astype(F32).reshape(LANES, 1), (LANES, qb))
    pairs = jnp.asarray([v for j in range(nblk) for i in range(j, nblk) for v in (j, i)], jnp.int32)
    hg = HEAD_GROUP
    n_groups = B_HEADS // hg
    grid_spec = pltpu.PrefetchScalarGridSpec(
        num_scalar_prefetch=3, grid=(batch, n_groups),
        in_specs=[
            pl.BlockSpec((hg, t, LANES), lambda b, g, *_: (g, b, 0)),
            pl.BlockSpec((hg, t, LANES), lambda b, g, *_: (n_groups + g, b, 0)),
            pl.BlockSpec((hg, LANES + ONES_ROWS, t), lambda b, g, *_: (g, 0, b)),
            pl.BlockSpec((hg,) + kbias.shape[1:], lambda b, g, *_: (g, 0, 0, 0)),
            pl.BlockSpec(subln_t.shape, lambda b, g, *_: (0, 0)),
        ],
        out_specs=pl.BlockSpec((hg, t, LANES), lambda b, g, *_: (g, b, 0)),
        scratch_shapes=[pltpu.VMEM((hg * nblk, LANES + ONES_ROWS, 2 * qb), F32),
                        pltpu.VMEM((hg * nblk, SUBLANES, 2 * qb), F32),
                        pltpu.VMEM((2 * hg, SUBLANES, 2 * qb), F32),
                        pltpu.VMEM((hg * nblk, 2 * qb, LANES), BF16),
                        pltpu.VMEM((hg, qb, 2 * qb), F32), pltpu.VMEM((hg, qb, 2 * qb), F32),
                        pltpu.VMEM((hg, qb, 2 * qb), BF16), pltpu.VMEM((hg, qb, 2 * qb), BF16)],
    )
    return pl.pallas_call(
        functools.partial(_attn_b_prompt_kernel, qb, nblk, lam_init), grid_spec=grid_spec,
        out_shape=jax.ShapeDtypeStruct((B_HEADS, batch * t, LANES), BF16),
        compiler_params=_cparams(("arbitrary", "arbitrary")), name="attn_b_prompt",
    )(pairs, slopes, lam.reshape(1).astype(F32), qkvh, qkvh, vt, kbias, subln_t)


def _attn_b_sample_kernel(dec_t, past, lam_init, slopes, lam_ref, q_ref, kn_ref, vn_ref, ck_ref, cv_ref,
                          tnew_ref, subln_ref, o_ref, tc_ref, tn_ref, pc_ref, pn_ref):
    lam = lam_ref[0]
    subln = subln_ref[...]
    colc = lax.broadcasted_iota(jnp.int32, (1, past), 1).astype(F32) - float(past)
    dn = (((1,), (1,)), ((), ()))
    for h in range(B_HEADS):
        qq = _stack_maps(q_ref[h])
        kc = ck_ref[pl.ds(h, past, stride=B_HEADS), :].astype(BF16)
        tc_ref[h] = lax.dot_general(qq, kc, dn, preferred_element_type=F32)
        tn_ref[h] = lax.dot_general(qq, kn_ref[h], dn, preferred_element_type=F32)
    sums = []
    for h in range(B_HEADS):
        slope = float(slopes[h]) * LOG2E
        s_c = tc_ref[h] + slope * colc
        s_n = tn_ref[h] + slope * tnew_ref[...]
        m = jnp.maximum(jnp.max(s_c, axis=-1, keepdims=True), jnp.max(s_n, axis=-1, keepdims=True))
        p_c = jnp.exp2(s_c - m)
        p_n = jnp.exp2(s_n - m)
        sums.append(jnp.sum(p_c, axis=-1, keepdims=True) + jnp.sum(p_n, axis=-1, keepdims=True))
        pc_ref[h] = p_c.astype(BF16)
        pn_ref[h] = p_n.astype(BF16)
    for h in range(B_HEADS):
        vc = cv_ref[pl.ds(h, past, stride=B_HEADS), :].astype(BF16)
        acc = (jnp.dot(pc_ref[h], vc, preferred_element_type=F32)
               + jnp.dot(pn_ref[h], vn_ref[h], preferred_element_type=F32))
        o_ref[h] = _diff_finish(acc, sums[h], dec_t, lam, lam_init, subln).astype(o_ref.dtype)


def attn_b_sample(qkvh, cache_k, cache_v, lam, subln, lam_init, row0, dec_batch, dec_t):
    past = cache_k.shape[1]
    qi = np.arange(dec_t)
    tnew = (qi[:, None] - np.abs(qi[:, None] - qi[None, :])).astype(np.float32)
    tnew = jnp.asarray(np.concatenate([tnew, tnew], axis=0))
    blk0 = row0 // dec_t
    rows = past * B_HEADS
    cache_k = cache_k.reshape(dec_batch, rows, LANES)
    cache_v = cache_v.reshape(dec_batch, rows, LANES)
    grid_spec = pltpu.PrefetchScalarGridSpec(
        num_scalar_prefetch=1, grid=(dec_batch,),
        in_specs=[
            pl.BlockSpec((B_HEADS, dec_t, LANES), lambda b, *_: (0, blk0 + b, 0)),
            pl.BlockSpec((B_HEADS, dec_t, LANES), lambda b, *_: (1, blk0 + b, 0)),
            pl.BlockSpec((B_HEADS, dec_t, LANES), lambda b, *_: (2, blk0 + b, 0)),
            pl.BlockSpec((None, rows, LANES), lambda b, *_: (b, 0, 0)),
            pl.BlockSpec((None, rows, LANES), lambda b, *_: (b, 0, 0)),
            pl.BlockSpec(tnew.shape, lambda b, *_: (0, 0)),
            pl.BlockSpec((1, LANES), lambda b, *_: (0, 0)),
        ],
        out_specs=pl.BlockSpec((B_HEADS, dec_t, LANES), lambda b, *_: (0, b, 0)),
        scratch_shapes=[pltpu.VMEM((B_HEADS, 2 * dec_t, past), F32), pltpu.VMEM((B_HEADS, 2 * dec_t, dec_t), F32),
                        pltpu.VMEM((B_HEADS, 2 * dec_t, past), BF16), pltpu.VMEM((B_HEADS, 2 * dec_t, dec_t), BF16)],
    )
    return pl.pallas_call(
        functools.partial(_attn_b_sample_kernel, dec_t, past, lam_init, tuple(_alibi(B_HEADS))),
        grid_spec=grid_spec,
        out_shape=jax.ShapeDtypeStruct((B_HEADS, dec_batch * dec_t, LANES), BF16),
        compiler_params=_cparams(("arbitrary",)), name="attn_b_sample",
    )(lam.reshape(1).astype(F32), qkvh, qkvh, qkvh, cache_k, cache_v, tnew, subln.reshape(1, LANES).astype(F32))


ROUTE_ROWS = 40
ROUTE_FIELDS = 8


def _route_t(lt):
    row = lax.broadcasted_iota(jnp.int32, lt.shape, 0).astype(F32)
    big = float(LANES)
    lg = jnp.where(row < N_GROUPS, lt, NEG_INF)
    gmax = jnp.max(lg, axis=0, keepdims=True)
    g_idx = jnp.min(jnp.where(lg == gmax, row, big), axis=0, keepdims=True)
    g_prob = 1.0 / jnp.sum(jnp.exp(lg - gmax), axis=0, keepdims=True)
    lo = N_GROUPS + EXPERTS_PER_GROUP * g_idx
    le = jnp.where((row >= lo) & (row < lo + EXPERTS_PER_GROUP), lt, NEG_INF)
    m1 = jnp.max(le, axis=0, keepdims=True)
    i1 = jnp.min(jnp.where(le == m1, row, big), axis=0, keepdims=True)
    le2 = jnp.where(row == i1, NEG_INF, le)
    m2 = jnp.max(le2, axis=0, keepdims=True)
    i2 = jnp.min(jnp.where(le2 == m2, row, big), axis=0, keepdims=True)
    t = jnp.exp(m2 - m1)
    ga = 1.0 / (1.0 + t)
    gb = t * ga
    return row, i1, i2, g_prob * ga, g_prob * gb


def _unpack_pairs(u):
    lo = pltpu.bitcast(u << 16, F32)
    hi = pltpu.bitcast(u & jnp.uint32(0xFFFF0000), F32)
    return jnp.concatenate([lo, hi], axis=1)


def _pack_pairs(x):
    bits = pltpu.bitcast(x.astype(BF16).astype(F32), jnp.uint32)
    w = x.shape[1] // 2
    return (bits[:, :w] >> 16) | bits[:, w:]


def _out_router_kernel(n_prompt_blocks, heads_in, two_o, two_res, *refs):
    refs = list(refs)
    oa_ref = refs.pop(0)
    ob_ref = refs.pop(0) if two_o else None
    ra_ref = refs.pop(0)
    rb_ref = refs.pop(0) if two_res else None
    (wo_ref, g_ref, wr_ref, wr2_ref, br_ref, utri_ref, h_ref, xn_ref, meta_ref, route_ref, cnt_ref,
     carry_ref) = refs
    is_prompt = pl.program_id(0) < n_prompt_blocks

    def load_o(ref):
        if heads_in:
            return jnp.concatenate([ref[hh] for hh in range(ref.shape[0])], axis=1)
        return ref[...]

    o = load_o(oa_ref)
    if two_o:
        o = jnp.where(is_prompt, o, load_o(ob_ref))
    res = ra_ref[...]
    if two_res:
        res = jnp.where(is_prompt, res, rb_ref[...])
    h = res + jnp.dot(o, wo_ref[...], preferred_element_type=F32)
    h_ref[...] = h
    ms = jnp.mean(h * h, axis=-1, keepdims=True)
    xn = h * lax.rsqrt(ms + NORM_EPS) * g_ref[...]
    x_hi = xn.astype(BF16)
    _store_planes(xn_ref, _pack_pairs(xn))
    x_lo = (xn - x_hi.astype(F32)).astype(BF16)
    r = jnp.dot(x_hi, wr_ref[...], preferred_element_type=F32)
    logits = (r[:, :LANES] + r[:, LANES:]) + jnp.dot(x_lo, wr2_ref[...], preferred_element_type=F32) + br_ref[...]
    row, i1, i2, g1, g2 = _route_t(logits.T[:ROUTE_ROWS])

    @pl.when(pl.program_id(0) == 0)
    def _():
        carry_ref[...] = jnp.zeros_like(carry_ref)

    sel1 = row == i1
    sel2 = row == i2
    onehot = jnp.where(sel1 | sel2, 1.0, 0.0)
    before = jnp.dot(onehot.astype(BF16), utri_ref[...], preferred_element_type=F32) + carry_ref[:, 0:1]
    rank1 = jnp.sum(jnp.where(sel1, before, 0.0), axis=0, keepdims=True)
    rank2 = jnp.sum(jnp.where(sel2, before, 0.0), axis=0, keepdims=True)
    carry_ref[...] = carry_ref[...] + jnp.sum(onehot, axis=1, keepdims=True)
    cnt_ref[...] = carry_ref[...]
    fields = [i1 - N_GROUPS, i2 - N_GROUPS, g1, g2, rank1, rank2]
    fields += [jnp.zeros_like(g1)] * (ROUTE_FIELDS - len(fields))
    route = jnp.concatenate(fields, axis=0)
    route_ref[...] = route
    meta_ref[...] = jnp.concatenate([route, jnp.zeros((LANES - ROUTE_FIELDS, route.shape[1]), F32)], axis=0).T


def out_router(oa, ob, ra, rb, wo, g, w_rg, b_rg, w_re, b_re, heads_in, tr):
    d = ra.shape[1]
    if heads_in:
        na = oa.shape[1]
        nb = 0 if ob is None else ob.shape[1]
    else:
        na = oa.shape[0]
        nb = 0 if ob is None else ob.shape[0]
    n = na + nb
    npb = na // tr
    two_o = ob is not None
    two_res = rb is not None
    first = lambda i: (jnp.minimum(i, npb - 1), 0)
    second = lambda i: (jnp.maximum(i - npb, 0), 0)
    plain = lambda i: (i, 0)
    in_specs, args = [], []

    def add_o(x, imap):
        if heads_in:
            in_specs.append(pl.BlockSpec((x.shape[0], tr, LANES), lambda i: (0, imap(i)[0], 0)))
        else:
            in_specs.append(pl.BlockSpec((tr, x.shape[1]), imap))
        args.append(x)

    add_o(oa, first if two_o else plain)
    if two_o:
        add_o(ob, second)
    in_specs.append(pl.BlockSpec((tr, d), first if two_res else plain))
    args.append(ra)
    if two_res:
        in_specs.append(pl.BlockSpec((tr, d), second))
        args.append(rb)
    wr = jnp.zeros((d, LANES), F32).at[:, :N_GROUPS].set(w_rg.astype(F32))
    wr = wr.at[:, N_GROUPS:N_GROUPS + N_EXPERTS].set(w_re.astype(F32))
    wr_hi = wr.astype(BF16)
    wr_lo = (wr - wr_hi.astype(F32)).astype(BF16)
    br = jnp.zeros((1, LANES), F32).at[0, :N_GROUPS].set(b_rg.astype(F32))
    br = br.at[0, N_GROUPS:N_GROUPS + N_EXPERTS].set(b_re.astype(F32))
    utri = jnp.asarray(np.triu(np.ones((tr, tr), np.float32), 1), BF16)
    consts = [wo, g.reshape(1, d).astype(F32), jnp.concatenate([wr_hi, wr_lo], axis=1), wr_hi, br, utri]
    for cst in consts:
        in_specs.append(pl.BlockSpec(cst.shape, lambda i: (0, 0)))
        args.append(cst)
    parts = d // 2 // SC_ROW_WORDS
    out_shape = [jax.ShapeDtypeStruct((n, d), F32), jax.ShapeDtypeStruct((parts, n, SC_ROW_WORDS), jnp.uint32),
                 jax.ShapeDtypeStruct((n, LANES), F32), jax.ShapeDtypeStruct((ROUTE_FIELDS, n), F32),
                 jax.ShapeDtypeStruct((ROUTE_ROWS, LANES), F32)]
    out_specs = [pl.BlockSpec((tr, d), plain), pl.BlockSpec((parts, tr, SC_ROW_WORDS), lambda i: (0, i, 0)),
                 pl.BlockSpec((tr, LANES), plain), pl.BlockSpec((ROUTE_FIELDS, tr), lambda i: (0, i)),
                 pl.BlockSpec((ROUTE_ROWS, LANES), lambda i: (0, 0))]
    return pl.pallas_call(
        functools.partial(_out_router_kernel, npb, heads_in, two_o, two_res),
        grid=(n // tr,), in_specs=in_specs, out_specs=out_specs, out_shape=out_shape,
        scratch_shapes=[pltpu.VMEM((ROUTE_ROWS, LANES), F32)],
        compiler_params=_cparams(("arbitrary",)), name="out_router")(*args)


def _expert_kernel(tm, tps, te_ref, nt_ref, x_ref, *refs):
    w_refs, o_ref = refs[:3 * tps], refs[3 * tps]
    first = pl.program_id(0) * tps
    n_valid = jnp.clip(nt_ref[0] - first, 0, tps)

    def tile(k):
        wg_ref, wu_ref, wd_ref = w_refs[3 * k:3 * k + 3]
        rows = slice(k * tm, (k + 1) * tm)
        x = _unpack_pairs(jnp.concatenate([x_ref[q, rows, :] for q in range(x_ref.shape[0])], axis=1)).astype(BF16)
        a = jnp.dot(x, wg_ref[...].astype(BF16), preferred_element_type=F32)
        u = jnp.dot(x, wu_ref[...].astype(BF16), preferred_element_type=F32)
        hh = (a * (1.0 / (1.0 + jnp.exp(-a))) * u).astype(BF16)
        packed = _pack_pairs(jnp.dot(hh, wd_ref[...].astype(BF16), preferred_element_type=F32))
        for q in range(o_ref.shape[0]):
            o_ref[q, rows, :] = packed[:, q * o_ref.shape[2]:(q + 1) * o_ref.shape[2]]

    for nv in range(tps + 1):

        @pl.when(n_valid == nv)
        def _(nv=nv):
            for k in range(nv):
                tile(k)
            if nv < tps:
                o_ref[:, nv * tm:, :] = jnp.zeros((o_ref.shape[0], (tps - nv) * tm, o_ref.shape[2]), o_ref.dtype)


def expert_mlp(xs, tile_expert, n_tiles, w_gate, w_up, w_down, layer, tm):
    parts, p, w = xs.shape
    d, f = w_gate.shape[-2:]
    epg = w_gate.shape[2]
    tps = 2 if (p // tm) % 2 == 0 else 1
    w_specs, w_args = [], []
    for k in range(tps):
        wmap = lambda t, te, nt, k=k: (layer, te[tps * t + k] // epg, te[tps * t + k] % epg, 0, 0)
        w_specs += [pl.BlockSpec((None, None, None, d, f), wmap), pl.BlockSpec((None, None, None, d, f), wmap),
                    pl.BlockSpec((None, None, None, f, d), wmap)]
        w_args += [w_gate, w_up, w_down]
    grid_spec = pltpu.PrefetchScalarGridSpec(
        num_scalar_prefetch=2, grid=(p // (tps * tm),),
        in_specs=[pl.BlockSpec((parts, tps * tm, w), lambda t, te, nt: (0, t, 0))] + w_specs,
        out_specs=pl.BlockSpec((parts, tps * tm, w), lambda t, te, nt: (0, t, 0)),
    )
    return pl.pallas_call(
        functools.partial(_expert_kernel, tm, tps), grid_spec=grid_spec,
        out_shape=jax.ShapeDtypeStruct((parts, p, w), jnp.uint32),
        compiler_params=_cparams(("arbitrary",)), name="expert_mlp")(tile_expert, n_tiles, xs, *w_args)


def sc_scatter_rows(x2, idx, n_out):
    mesh = plsc.VectorSubcoreMesh(core_axis_name="c", subcore_axis_name="s")
    n_src_win = x2.shape[0] // SC_WINDOW

    @functools.partial(pl.kernel, out_type=jax.ShapeDtypeStruct((n_out, SC_ROW_WORDS), x2.dtype), mesh=mesh)
    def scatter(x_hbm, i_hbm, o_hbm):
        def body(x_vmem, i_vmem):
            pltpu.sync_copy(x_vmem, o_hbm.at[i_vmem.at[0]])

        pltpu.emit_pipeline(
            body, grid=(n_out // SC_WINDOW,),
            in_specs=[pl.BlockSpec((SC_WINDOW, SC_ROW_WORDS), lambda i: (lax.rem(i, n_src_win), 0)),
                      pl.BlockSpec((1, SC_WINDOW), lambda i: (0, i))],
            out_specs=[], core_axis_name=("c", "s"), dimension_semantics=(pltpu.PARALLEL,),
        )(x_hbm, i_hbm)

    return scatter(x2, idx)


def sc_gather_rows(x2, idx):
    mesh = plsc.VectorSubcoreMesh(core_axis_name="c", subcore_axis_name="s")
    n_col, n = idx.shape

    @functools.partial(pl.kernel, out_type=jax.ShapeDtypeStruct((n, n_col * SC_ROW_WORDS), x2.dtype), mesh=mesh)
    def gather(x_hbm, i_hbm, o_hbm):
        def body(i_vmem, o_vmem):
            pltpu.sync_copy(x_hbm.at[i_vmem.at[0]], o_vmem)

        pltpu.emit_pipeline(
            body, grid=(n // SC_WINDOW, n_col),
            in_specs=[pl.BlockSpec((1, SC_WINDOW), lambda i, c: (c, i))],
            out_specs=[pl.BlockSpec((SC_WINDOW, SC_ROW_WORDS), lambda i, c: (i, c))],
            core_axis_name=("c", "s"), dimension_semantics=(pltpu.PARALLEL, pltpu.ARBITRARY),
        )(i_hbm, o_hbm)

    return gather(x2, idx)


def moe_dispatch(xn_packed, route, counts, tm):
    parts, n, w = xn_packed.shape
    e = route[0:2].astype(jnp.int32)
    rank = route[4:6].astype(jnp.int32)
    counts = counts[N_GROUPS:N_GROUPS + N_EXPERTS, 0].astype(jnp.int32)
    padded = ((counts + tm - 1) // tm) * tm
    pend = jnp.cumsum(padded)
    pstart = pend - padded
    ids = jnp.arange(N_EXPERTS, dtype=jnp.int32)
    pos = jnp.sum(jnp.where(e[..., None] == ids, pstart, 0), axis=-1) + rank
    p = 2 * n + N_EXPERTS * tm
    n_pad = p - 2 * n
    gap_len = jnp.concatenate([padded - counts, (p - pend[-1]).reshape(1)])
    gap_first = jnp.concatenate([pstart + counts, pend[-1:]])
    gap_end = jnp.cumsum(gap_len)
    k = jnp.arange(n_pad, dtype=jnp.int32)
    gap = jnp.sum((k[:, None] >= gap_end[None, :]).astype(jnp.int32), axis=1)
    shift = gap_first - (gap_end - gap_len)
    pad_pos = k + jnp.sum(jnp.where(gap[:, None] == jnp.arange(N_EXPERTS + 1), shift, 0), axis=-1)
    planes = lambda rows: jnp.concatenate([rows + q * p for q in range(parts)])
    scat_idx = jnp.concatenate([planes(pos[0]), planes(pos[1]), planes(pad_pos)]).reshape(1, parts * p)
    gath_idx = jnp.stack([pos[k] + q * p for k in range(2) for q in range(parts)])
    xs = sc_scatter_rows(xn_packed.reshape(parts * n, w), scat_idx, parts * p).reshape(parts, p, w)
    tile_start = jnp.arange(p // tm, dtype=jnp.int32) * tm
    tile_expert = jnp.sum((tile_start[:, None] >= pend[None, :]).astype(jnp.int32), axis=1)
    tile_expert = jnp.minimum(tile_expert, N_EXPERTS - 1)
    n_tiles = (pend[-1] // tm).astype(jnp.int32).reshape(1)
    return xs, tile_expert, n_tiles, gath_idx


def moe_experts(xn_packed, route, counts, w_gate, w_up, w_down, layer, tm):
    parts, n, w = xn_packed.shape
    xs, tile_expert, n_tiles, gath_idx = moe_dispatch(xn_packed, route, counts, tm)
    o = expert_mlp(xs, tile_expert, n_tiles, w_gate, w_up, w_down, layer, tm)
    return sc_gather_rows(o.reshape(parts * o.shape[1], w), gath_idx)


def _final_norm_kernel(n_prompt_blocks, h_ref, o0_ref, o1_ref, meta_ref, g_ref, yp_ref, ys_ref):
    meta = meta_ref[...]
    x = (h_ref[...] + meta[:, 2:3] * _unpack_pairs(o0_ref[...])) + meta[:, 3:4] * _unpack_pairs(o1_ref[...])
    ms = jnp.mean(x * x, axis=-1, keepdims=True)
    y = x * lax.rsqrt(ms + NORM_EPS) * g_ref[...]
    i = pl.program_id(0)

    @pl.when(i < n_prompt_blocks)
    def _():
        yp_ref[...] = y

    @pl.when(i >= n_prompt_blocks)
    def _():
        ys_ref[...] = y


def final_norm(h, og, meta, g, n_prompt, tr):
    n, d = h.shape
    npb = n_prompt // tr
    return pl.pallas_call(
        functools.partial(_final_norm_kernel, npb),
        grid=(n // tr,),
        in_specs=[pl.BlockSpec((tr, d), lambda i: (i, 0)),
                  pl.BlockSpec((tr, og.shape[1] // 2), lambda i: (i, 0)),
                  pl.BlockSpec((tr, og.shape[1] // 2), lambda i: (i, 1)),
                  pl.BlockSpec((tr, LANES), lambda i: (i, 0)), pl.BlockSpec((1, d), lambda i: (0, 0))],
        out_specs=[pl.BlockSpec((tr, d), lambda i: (jnp.minimum(i, npb - 1), 0)),
                   pl.BlockSpec((tr, d), lambda i: (jnp.maximum(i - npb, 0), 0))],
        out_shape=[jax.ShapeDtypeStruct((n_prompt, d), F32), jax.ShapeDtypeStruct((n - n_prompt, d), F32)],
        compiler_params=_cparams(("arbitrary",)), name="final_norm")(h, og, og, meta, g.reshape(1, d).astype(F32))


def _pick_rows_tile(n_prompt, n_sample):
    tr = ROW_TILE
    while tr >= SUBLANES:
        if n_prompt % tr == 0 and n_sample % tr == 0:
            return tr
        tr //= 2
    raise ValueError("row counts must be multiples of the sublane count")


def kernel(x_prompt, x_sample, cache_a_k, cache_a_v, cache_b_k, cache_b_v, norm_mix, norm_ffn, norm_final,
           w_a_qkv, b_a_qkv, a_sinks, w_a_o, w_b_qkv, b_lambda, b_subln, w_b_o,
           w_route_group, b_route_group, w_route_expert, b_route_expert, w_gate, w_up, w_down):
    batch, t, d = x_prompt.shape
    dec_batch, dec_t, _ = x_sample.shape
    n_p, n_s = batch * t, dec_batch * dec_t
    tr = _pick_rows_tile(n_p, n_s)
    tm = min(EXPERT_TILE, 4 * tr)
    xp = x_prompt.reshape(n_p, d)
    xs = x_sample.reshape(n_s, d)
    nq_a = A_HEADS * A_HD
    nkv_a = A_KV_HEADS * A_HD

    a_scale = jnp.where(jnp.arange(nq_a + 2 * nkv_a) < nq_a, A_HD ** -0.5 * LOG2E, 1.0).astype(F32)
    q_a, kv_ap, kv_as = norm_proj(xp, xs, norm_mix[0], (w_a_qkv[0] * a_scale).astype(BF16), b_a_qkv[0] * a_scale,
                                  [(0, nq_a, "flat", BF16), (nq_a, nq_a + 2 * nkv_a, "split", F32)], tr, n_p)
    qb_a = min(CHUNK * WIN_CHUNKS, t)
    o_ap = attn_a_prompt(q_a, kv_ap, a_sinks[0], batch, t, qb_a)
    kv_s = kv_as.reshape(dec_batch, dec_t, 2 * nkv_a)
    past_a = cache_a_k.shape[2]
    kband = jnp.concatenate([cache_a_k[0].reshape(dec_batch, past_a, nkv_a), kv_s[..., :nkv_a]], axis=1)
    vband = jnp.concatenate([cache_a_v[0].reshape(dec_batch, past_a, nkv_a), kv_s[..., nkv_a:]], axis=1)
    o_as = attn_a_sample(q_a, kband, vband, a_sinks[0], n_p, dec_batch, dec_t)
    h, xn, meta, route, counts = out_router(o_ap, o_as, xp, xs, w_a_o[0].astype(BF16), norm_ffn[0], w_route_group[0],
                                            b_route_group[0], w_route_expert[0], b_route_expert[0], False, tr)
    og = moe_experts(xn, route, counts, w_gate, w_up, w_down, 0, tm)

    keep = min(CHUNK * WIN_CHUNKS, t)
    kv_p = kv_ap.reshape(batch, t, 2 * nkv_a)[:, t - keep:]
    a_k_prompt = kv_p[..., :nkv_a].reshape(1, batch, keep, A_KV_HEADS, A_HD)
    a_v_prompt = kv_p[..., nkv_a:].reshape(1, batch, keep, A_KV_HEADS, A_HD)
    a_k_sample = kband[:, dec_t:].reshape(1, dec_batch, past_a, A_KV_HEADS, A_HD)
    a_v_sample = vband[:, dec_t:].reshape(1, dec_batch, past_a, A_KV_HEADS, A_HD)

    lam_init = 0.8 - 0.6 * float(np.exp(-0.3 * 1))
    lp = b_lambda[0].astype(F32)
    lam = jnp.exp(jnp.sum(lp[0] * lp[1])) - jnp.exp(jnp.sum(lp[2] * lp[3])) + lam_init
    wb = B_HEADS * 2 * B_HD
    col_scale = jnp.where(jnp.arange(3 * wb) < wb, B_Q_SCALE, 1.0).astype(F32)
    h, qkvh, k_bp, k_bs, v_bp, v_bs, vt = norm_proj(
        h, None, norm_mix[1], (w_b_qkv[0] * col_scale).astype(BF16), None,
        [(0, 3 * wb, "heads", BF16), (wb, 2 * wb, "split", F32), (2 * wb, 3 * wb, "split", F32),
         (2 * wb, 3 * wb, "heads_t", BF16)], tr, n_p, og=og, meta=meta)
    qb_b = min(ATTN_BLOCK, t)
    o_bp = attn_b_prompt(qkvh, vt, lam, b_subln[0], lam_init, batch, t, qb_b)
    o_bs = attn_b_sample(qkvh, cache_b_k[0], cache_b_v[0], lam, b_subln[0], lam_init, n_p, dec_batch, dec_t)
    hd2 = 2 * B_HD
    o_bs, k_bp, v_bp = lax.optimization_barrier((o_bs, k_bp, v_bp))
    b_k_prompt = k_bp.reshape(1, batch, t, B_HEADS, hd2)
    b_v_prompt = v_bp.reshape(1, batch, t, B_HEADS, hd2)
    o_bp, b_k_prompt, b_v_prompt = lax.optimization_barrier((o_bp, b_k_prompt, b_v_prompt))
    h, xn, meta, route, counts = out_router(o_bp, o_bs, h, None, w_b_o[0].astype(BF16), norm_ffn[1], w_route_group[1],
                                            b_route_group[1], w_route_expert[1], b_route_expert[1], True, tr)
    og = moe_experts(xn, route, counts, w_gate, w_up, w_down, 1, tm)

    y_p, y_s = final_norm(h, og, meta, norm_final, n_p, tr)
    return (y_p.reshape(batch, t, d), y_s.reshape(dec_batch, dec_t, d),
            a_k_prompt, a_v_prompt, a_k_sample, a_v_sample, b_k_prompt, b_v_prompt,
            k_bs.reshape(1, dec_batch, dec_t, B_HEADS, hd2), v_bs.reshape(1, dec_batch, dec_t, B_HEADS, hd2))
```

```python
import functools

import jax
import jax.numpy as jnp
import numpy as np
from jax import lax
from jax.experimental import pallas as pl
from jax.experimental.pallas import tpu as pltpu
from jax.experimental.pallas import tpu_sc as plsc

F32 = jnp.float32
BF16 = jnp.bfloat16
NEG_INF = float("-inf")

CHUNK = 64
WIN_CHUNKS = 2
A_HEADS, A_KV_HEADS, A_GROUP, A_HD = 16, 4, 4, 64
B_HEADS, B_HD = 8, 64
N_GROUPS, EXPERTS_PER_GROUP = 4, 8
N_EXPERTS = N_GROUPS * EXPERTS_PER_GROUP
NORM_EPS = 1e-6
SUBLN_EPS = 1e-5

LANES = 128
SUBLANES = 8
ROW_TILE = 512
EXPERT_TILE = 512
ATTN_BLOCK = 256
COL_CHUNK = 512
PAIR_UNROLL = 18
HEAD_GROUP = 2
ONES_ROWS = 16
SC_WINDOW = 128
SC_ROW_WORDS = 256
VMEM_LIMIT = 48 * 1024 * 1024


def _cparams(sem, flags=None):
    return pltpu.CompilerParams(dimension_semantics=sem, vmem_limit_bytes=VMEM_LIMIT, flags=flags)


def _alibi(n):
    return 2.0 ** (-8.0 * np.arange(1, n + 1) / n)


def _store_planes(ref, x):
    w = ref.shape[2]
    for p in range(ref.shape[0]):
        ref[p] = x[:, p * w:(p + 1) * w]


def _add_expert_outputs(x, og_ref, meta_ref):
    half = og_ref.shape[1] // 2
    meta = meta_ref[...]
    return (x + meta[:, 2:3] * _unpack_pairs(og_ref[:, :half])) + meta[:, 3:4] * _unpack_pairs(og_ref[:, half:])


def _norm_proj_kernel(n_prompt_blocks, two_src, has_bias, combine, outs, *refs):
    refs = list(refs)
    xa_ref = refs.pop(0)
    xb_ref = refs.pop(0) if two_src else None
    og_ref, meta_ref = (refs.pop(0), refs.pop(0)) if combine else (None, None)
    g_ref = refs.pop(0)
    w_ref = refs.pop(0)
    b_ref = refs.pop(0) if has_bias else None
    out_refs = refs
    x = xa_ref[...]
    if two_src:
        x = jnp.where(pl.program_id(0) < n_prompt_blocks, x, xb_ref[...])
    if combine:
        x = _add_expert_outputs(x, og_ref, meta_ref)
        out_refs.pop(0)[...] = x
    ms = jnp.mean(x * x, axis=-1, keepdims=True)
    xn = (x * lax.rsqrt(ms + NORM_EPS) * g_ref[...]).astype(BF16)
    is_prompt = pl.program_id(0) < n_prompt_blocks
    out_refs = list(out_refs)
    targets = []
    for c0, c1, layout in outs:
        targets.append((c0, c1, layout, out_refs.pop(0), out_refs.pop(0) if layout == "split" else None))
    starts = list(range(0, w_ref.shape[1], COL_CHUNK))
    transposed = [s for s in starts if any(lay == "heads_t" and c0 <= s < c1 for c0, c1, lay in outs)]
    for s in transposed + [s for s in starts if s not in transposed]:
        e = s + COL_CHUNK
        r = jnp.dot(xn, w_ref[:, s:e], preferred_element_type=F32)
        if has_bias:
            r = r + b_ref[:, s:e]
        for c0, c1, layout, o_ref, o2_ref in targets:
            if not (c0 <= s and e <= c1):
                continue
            rr = r.astype(o_ref.dtype)
            if layout == "flat":
                o_ref[:, s - c0:e - c0] = rr
            elif layout == "split":

                @pl.when(is_prompt)
                def _(rr=rr, o_ref=o_ref, s=s, e=e, c0=c0):
                    o_ref[:, s - c0:e - c0] = rr

                @pl.when(jnp.logical_not(is_prompt))
                def _(rr=rr, o2_ref=o2_ref, s=s, e=e, c0=c0):
                    o2_ref[:, s - c0:e - c0] = rr
            elif layout == "heads":
                for hh in range((e - s) // LANES):
                    o_ref[(s - c0) // LANES + hh] = rr[:, hh * LANES:(hh + 1) * LANES]
            else:
                ones = jnp.ones((ONES_ROWS, r.shape[0]), o_ref.dtype)
                for hh in range((e - s) // LANES):
                    head = (s - c0) // LANES + hh
                    o_ref[head, :LANES, :] = r[:, hh * LANES:(hh + 1) * LANES].T.astype(o_ref.dtype)
                    o_ref[head, LANES:, :] = ones


def norm_proj(xa, xb, g, w, b, outs, tr, n_prompt, og=None, meta=None):
    na, d = xa.shape
    nb = 0 if xb is None else xb.shape[0]
    n = na + nb
    npb = n_prompt // tr
    two = xb is not None
    combine = og is not None
    in_specs = [pl.BlockSpec((tr, d), (lambda i: (jnp.minimum(i, npb - 1), 0)) if two else (lambda i: (i, 0)))]
    args = [xa]
    if two:
        in_specs.append(pl.BlockSpec((tr, d), lambda i: (jnp.maximum(i - npb, 0), 0)))
        args.append(xb)
    if combine:
        in_specs += [pl.BlockSpec((tr, og.shape[1]), lambda i: (i, 0)), pl.BlockSpec((tr, LANES), lambda i: (i, 0))]
        args += [og, meta]
    in_specs.append(pl.BlockSpec((1, d), lambda i: (0, 0)))
    args.append(g.reshape(1, d).astype(F32))
    in_specs.append(pl.BlockSpec(w.shape, lambda i: (0, 0)))
    args.append(w)
    if b is not None:
        in_specs.append(pl.BlockSpec((1, w.shape[1]), lambda i: (0, 0)))
        args.append(b.reshape(1, -1).astype(F32))
    out_shapes, out_specs = [], []
    if combine:
        out_shapes.append(jax.ShapeDtypeStruct((n, d), F32))
        out_specs.append(pl.BlockSpec((tr, d), lambda i: (i, 0)))
    for c0, c1, layout, dt in outs:
        assert c0 % COL_CHUNK == 0 and c1 % COL_CHUNK == 0
        if layout == "flat":
            out_shapes.append(jax.ShapeDtypeStruct((n, c1 - c0), dt))
            out_specs.append(pl.BlockSpec((tr, c1 - c0), lambda i: (i, 0)))
        elif layout == "split":
            out_shapes.append(jax.ShapeDtypeStruct((n_prompt, c1 - c0), dt))
            out_specs.append(pl.BlockSpec((tr, c1 - c0), lambda i: (jnp.minimum(i, npb - 1), 0)))
            out_shapes.append(jax.ShapeDtypeStruct((n - n_prompt, c1 - c0), dt))
            out_specs.append(pl.BlockSpec((tr, c1 - c0), lambda i: (jnp.maximum(i - npb, 0), 0)))
        elif layout == "heads":
            nh = (c1 - c0) // LANES
            out_shapes.append(jax.ShapeDtypeStruct((nh, n, LANES), dt))
            out_specs.append(pl.BlockSpec((nh, tr, LANES), lambda i: (0, i, 0)))
        else:
            assert layout == "heads_t"
            nh = (c1 - c0) // LANES
            out_shapes.append(jax.ShapeDtypeStruct((nh, LANES + ONES_ROWS, n), dt))
            out_specs.append(pl.BlockSpec((nh, LANES + ONES_ROWS, tr), lambda i: (0, 0, i)))
    kern = functools.partial(_norm_proj_kernel, npb, two, b is not None, combine,
                             [(c0, c1, lay) for c0, c1, lay, _ in outs])
    return pl.pallas_call(
        kern, grid=(n // tr,), in_specs=in_specs, out_specs=out_specs, out_shape=out_shapes,
        compiler_params=_cparams(("arbitrary",)), name="norm_proj")(*args)


def _attn_a_chunk(q, k, v, bias_ref, sink_ref, valid):
    nq = q.shape[0]
    outs = []
    for kh in range(A_KV_HEADS):
        qs = jnp.concatenate(
            [q[:, (kh * A_GROUP + g) * A_HD:(kh * A_GROUP + g + 1) * A_HD] for g in range(A_GROUP)], axis=0)
        kk = k[:, kh * A_HD:(kh + 1) * A_HD]
        s = lax.dot_general(qs, kk, (((1,), (1,)), ((), ())), preferred_element_type=F32) + bias_ref[kh]
        if valid is not None:
            s = jnp.where(valid, s, NEG_INF)
        sk = sink_ref[kh]
        m = jnp.maximum(jnp.max(s, axis=-1, keepdims=True), sk)
        e = jnp.exp2(s - m)
        den = jnp.sum(e, axis=-1, keepdims=True) + jnp.exp2(sk - m)
        p = (e * (1.0 / den)).astype(BF16)
        o = jnp.dot(p, v[:, kh * A_HD:(kh + 1) * A_HD], preferred_element_type=F32)
        for g in range(A_GROUP):
            outs.append(o[g * nq:(g + 1) * nq])
    return jnp.concatenate(outs, axis=1)


def _attn_a_prompt_kernel(qb, nsub, q_ref, kvc_ref, kvp_ref, bias_ref, sink_ref, o_ref, t_ref, p_ref):
    cb = pl.program_id(1)
    back = CHUNK * WIN_CHUNKS
    nk = back + qb
    nkv = A_KV_HEADS * A_HD
    kvfull = jnp.concatenate([kvp_ref[...], kvc_ref[...]], axis=0)
    k_all = kvfull[:, :nkv].astype(BF16)
    vt_all = kvfull[:, nkv:].T.astype(BF16)
    dn = (((1,), (1,)), ((), ()))
    units = [(s, kh) for s in range(nsub) for kh in range(A_KV_HEADS)]
    for s, kh in units:
        qs = jnp.concatenate([q_ref[s * qb:(s + 1) * qb, (kh * A_GROUP + g) * A_HD:(kh * A_GROUP + g + 1) * A_HD]
                              for g in range(A_GROUP)], axis=0)
        t_ref[s * A_KV_HEADS + kh] = lax.dot_general(k_all[s * qb:s * qb + nk, kh * A_HD:(kh + 1) * A_HD], qs, dn,
                                                     preferred_element_type=F32)
    for s in range(nsub):
        outs = []
        for kh in range(A_KV_HEADS):
            u = s * A_KV_HEADS + kh
            st = t_ref[u] + bias_ref[kh]
            if s == 0:
                key_pos = lax.broadcasted_iota(jnp.int32, (nk, 1), 0) + (cb * nsub * qb - back)
                st = st + jnp.where(key_pos >= 0, 0.0, NEG_INF)
            sk = sink_ref[kh]
            m = jnp.maximum(jnp.max(st, axis=0, keepdims=True), sk)
            e = jnp.exp2(st - m)
            inv_den = 1.0 / (jnp.sum(e, axis=0, keepdims=True) + jnp.exp2(sk - m))
            p_ref[u] = e.astype(BF16)
            ot = jnp.dot(vt_all[kh * A_HD:(kh + 1) * A_HD, s * qb:s * qb + nk], p_ref[u],
                         preferred_element_type=F32)
            o = (ot * inv_den).T
            for g in range(A_GROUP):
                outs.append(o[g * qb:(g + 1) * qb])
        o_ref[s * qb:(s + 1) * qb, :] = jnp.concatenate(outs, axis=1).astype(o_ref.dtype)


def _a_prompt_tables(qb, sinks):
    back = CHUNK * WIN_CHUNKS
    slopes = _alibi(A_HEADS).reshape(A_KV_HEADS, A_GROUP)
    qpos = np.arange(qb)
    kpos = np.arange(back + qb) - back
    dist = np.abs(kpos[:, None] - qpos[None, :]).astype(np.float64)
    band_lo = (qpos // CHUNK) * CHUNK - back
    in_band = (kpos[:, None] >= band_lo[None, :]) & (kpos[:, None] < band_lo[None, :] + back + CHUNK)
    bias = np.where(in_band[None, None], -slopes[:, :, None, None] * dist[None, None], -np.inf)
    bias = np.transpose(bias, (0, 2, 1, 3)).reshape(A_KV_HEADS, back + qb, A_GROUP * qb) * LOG2E
    sk = jnp.repeat(sinks.astype(F32).reshape(A_KV_HEADS, A_GROUP), qb, axis=1)[:, None, :] * LOG2E
    return jnp.asarray(bias, F32), sk


def _a_tables(qpos, kpos, sinks):
    slopes = _alibi(A_HEADS).reshape(A_KV_HEADS, A_GROUP)
    dist = np.abs(qpos[:, None] - kpos[None, :]).astype(np.float64)
    bias = -slopes[:, :, None, None] * dist
    nq = len(qpos)
    bias = jnp.asarray(bias.reshape(A_KV_HEADS, A_GROUP * nq, len(kpos)) * LOG2E, F32)
    sk = jnp.repeat(sinks.astype(F32).reshape(A_KV_HEADS, A_GROUP), nq, axis=1)[..., None] * LOG2E
    return bias, sk


def attn_a_prompt(q_all, kv_all, sinks, batch, t, qb):
    nq = A_HEADS * A_HD
    nkv2 = 2 * A_KV_HEADS * A_HD
    back = CHUNK * WIN_CHUNKS
    bias, sk = _a_prompt_tables(qb, sinks)
    nsub = max(s for s in (1, 2, 4, 8) if t % (s * qb) == 0)
    rows = nsub * qb
    nblk = t // rows
    r = rows // back
    return pl.pallas_call(
        functools.partial(_attn_a_prompt_kernel, qb, nsub),
        grid=(batch, nblk),
        in_specs=[
            pl.BlockSpec((rows, nq), lambda b, i: (b * nblk + i, 0)),
            pl.BlockSpec((rows, nkv2), lambda b, i: (b * nblk + i, 0)),
            pl.BlockSpec((back, nkv2), lambda b, i: (jnp.maximum((b * nblk + i) * r - 1, 0), 0)),
            pl.BlockSpec(bias.shape, lambda b, i: (0, 0, 0)),
            pl.BlockSpec(sk.shape, lambda b, i: (0, 0, 0)),
        ],
        out_specs=pl.BlockSpec((rows, nq), lambda b, i: (b * nblk + i, 0)),
        out_shape=jax.ShapeDtypeStruct((batch * t, nq), BF16),
        scratch_shapes=[pltpu.VMEM((nsub * A_KV_HEADS, back + qb, A_GROUP * qb), F32),
                        pltpu.VMEM((nsub * A_KV_HEADS, back + qb, A_GROUP * qb), BF16)],
        compiler_params=_cparams(("arbitrary", "arbitrary")), name="attn_a_prompt",
    )(q_all, kv_all, kv_all, bias, sk)


def _attn_a_sample_kernel(q_ref, k_ref, v_ref, bias_ref, sink_ref, o_ref):
    o = _attn_a_chunk(q_ref[...], k_ref[...].astype(BF16), v_ref[...].astype(BF16), bias_ref, sink_ref, None)
    o_ref[...] = o.astype(o_ref.dtype)


def attn_a_sample(q_all, kband, vband, sinks, row0, dec_batch, dec_t):
    nq = A_HEADS * A_HD
    s = kband.shape[1]
    past = s - dec_t
    kpos = np.concatenate([np.arange(past) - past, np.arange(dec_t)])
    bias, sk = _a_tables(np.arange(dec_t), kpos, sinks)
    blk0 = row0 // dec_t
    return pl.pallas_call(
        _attn_a_sample_kernel,
        grid=(dec_batch,),
        in_specs=[
            pl.BlockSpec((dec_t, nq), lambda b: (blk0 + b, 0)),
            pl.BlockSpec((None, s, kband.shape[2]), lambda b: (b, 0, 0)),
            pl.BlockSpec((None, s, vband.shape[2]), lambda b: (b, 0, 0)),
            pl.BlockSpec(bias.shape, lambda b: (0, 0, 0)),
            pl.BlockSpec(sk.shape, lambda b: (0, 0, 0)),
        ],
        out_specs=pl.BlockSpec((dec_t, nq), lambda b: (b, 0)),
        out_shape=jax.ShapeDtypeStruct((dec_batch * dec_t, nq), BF16),
        compiler_params=_cparams(("arbitrary",)), name="attn_a_sample",
    )(q_all, kband, vband, bias, sk)


B_Q_SCALE = B_HD ** -0.5 * float(np.log2(np.e))
LOG2E = float(np.log2(np.e))


def _stack_maps(qh):
    lane = lax.broadcasted_iota(jnp.int32, qh.shape, 1)
    zero = jnp.zeros_like(qh)
    return jnp.concatenate([jnp.where(lane < B_HD, qh, zero), jnp.where(lane >= B_HD, qh, zero)], axis=0)


def _diff_finish(acc, l, nq, lam, lam_init, subln):
    inv = 1.0 / l
    o = acc[:nq] * inv[:nq] - lam * (acc[nq:] * inv[nq:])
    ms = jnp.mean(o * o, axis=-1, keepdims=True)
    return o * lax.rsqrt(ms + SUBLN_EPS) * subln * (1.0 - lam_init)


def _attn_b_prompt_kernel(qb, nb, lam_init, pair_ref, slope_ref, lam_ref, q_ref, k_ref, vt_ref, kbias_ref,
                          subln_ref, o_ref, acc_ref, m_ref, al_ref, qq_ref, t0_ref, t1_ref, p0_ref, p1_ref):
    g = pl.program_id(1)
    lam = lam_ref[0]
    dn = (((1,), (1,)), ((), ()))
    t_refs = (t0_ref, t1_ref)
    p_refs = (p0_ref, p1_ref)
    heads = range(HEAD_GROUP)
    n_pairs = nb * (nb + 1) // 2
    slopes = [slope_ref[g * HEAD_GROUP + u] for u in heads]

    pairs = [(j, iq) for j in range(nb) for iq in range(j, nb)]

    def scores(n, slot):
        j, iq = pairs[n]
        for u in heads:
            t_refs[slot][u] = lax.dot_general(k_ref[u, j * qb:(j + 1) * qb, :], qq_ref[u * nb + iq], dn,
                                              preferred_element_type=F32)

    def softmax_step(n, slot):
        j, iq = pairs[n]
        for u in heads:
            st = u * nb + iq
            t = t_refs[slot][u] + kbias_ref[u, int(j == iq)]
            off = slopes[u] * jnp.full((1, 2 * qb), float((j - iq) * qb), F32)
            m_blk = jnp.max(t, axis=0, keepdims=True) + off
            if j == 0:
                m_new = m_blk
            else:
                m = m_ref[st, 0:1, :]
                m_new = jnp.maximum(m, m_blk)
                al_ref[slot * HEAD_GROUP + u, 0:1, :] = jnp.exp2(m - m_new)
            p_refs[slot][u] = jnp.exp2(t + (off - m_new)).astype(BF16)
            m_ref[st, 0:1, :] = m_new

    def accumulate(n, slot):
        j, iq = pairs[n]
        for u in heads:
            st = u * nb + iq
            pv = jnp.dot(vt_ref[u, :, j * qb:(j + 1) * qb], p_refs[slot][u], preferred_element_type=F32)
            if j == 0:
                acc_ref[st] = pv
            else:
                acc_ref[st] = al_ref[slot * HEAD_GROUP + u, 0:1, :] * acc_ref[st] + pv

    for u in heads:
        for iq in range(nb):
            qq_ref[u * nb + iq] = _stack_maps(q_ref[u, iq * qb:(iq + 1) * qb, :])
    scores(0, 0)
    for n in range(n_pairs):
        slot = n % 2
        softmax_step(n, slot)
        if n > 0:
            accumulate(n - 1, 1 - slot)
        if n + 1 < n_pairs:
            scores(n + 1, 1 - slot)
    accumulate(n_pairs - 1, (n_pairs - 1) % 2)
    for u in heads:
        for iq in range(nb):
            inv = 1.0 / acc_ref[u * nb + iq, LANES:LANES + 1, :]
            acc = acc_ref[u * nb + iq, :LANES, :]
            ot = acc[:, :qb] * inv[:, :qb] - lam * (acc[:, qb:] * inv[:, qb:])
            ms = jnp.mean(ot * ot, axis=0, keepdims=True)
            ot = ot * lax.rsqrt(ms + SUBLN_EPS) * subln_ref[...] * (1.0 - lam_init)
            o_ref[u, iq * qb:(iq + 1) * qb, :] = ot.T.astype(o_ref.dtype)


def attn_b_prompt(qkvh, vt, lam, subln, lam_init, batch, t, qb):
    nblk = t // qb
    qq = np.arange(qb)
    slope2 = (_alibi(B_HEADS) * LOG2E)[:, None, None]
    kcol = np.broadcast_to(qq[:, None], (qb, qb)).astype(np.float64)
    tri = -2.0 * np.maximum(qq[:, None] - qq[None, :], 0)
    msk = np.where((qq[:, None] // CHUNK) <= (qq[None, :] // CHUNK), 0.0, -np.inf)
    both = lambda a: np.concatenate([a, a], axis=-1)
    kbias = np.stack([both(slope2 * kcol[None]), both(slope2 * (kcol + tri)[None] + msk[None])], axis=1)
    kbias = jnp.asarray(kbias, F32)
    slopes = jnp.asarray(_alibi(B_HEADS) * LOG2E, F32)
    subln_t = jnp.broadcast_to(subln.astype(F32).reshape(LANES, 1), (LANES, qb))
    pairs = jnp.asarray([v for j in range(nblk) for i in range(j, nblk) for v in (j, i)], jnp.int32)
    hg = HEAD_GROUP
    n_groups = B_HEADS // hg
    grid_spec = pltpu.PrefetchScalarGridSpec(
        num_scalar_prefetch=3, grid=(batch, n_groups),
        in_specs=[
            pl.BlockSpec((hg, t, LANES), lambda b, g, *_: (g, b, 0)),
            pl.BlockSpec((hg, t, LANES), lambda b, g, *_: (n_groups + g, b, 0)),
            pl.BlockSpec((hg, LANES + ONES_ROWS, t), lambda b, g, *_: (g, 0, b)),
            pl.BlockSpec((hg,) + kbias.shape[1:], lambda b, g, *_: (g, 0, 0, 0)),
            pl.BlockSpec(subln_t.shape, lambda b, g, *_: (0, 0)),
        ],
        out_specs=pl.BlockSpec((hg, t, LANES), lambda b, g, *_: (g, b, 0)),
        scratch_shapes=[pltpu.VMEM((hg * nblk, LANES + ONES_ROWS, 2 * qb), F32),
                        pltpu.VMEM((hg * nblk, SUBLANES, 2 * qb), F32),
                        pltpu.VMEM((2 * hg, SUBLANES, 2 * qb), F32),
                        pltpu.VMEM((hg * nblk, 2 * qb, LANES), BF16),
                        pltpu.VMEM((hg, qb, 2 * qb), F32), pltpu.VMEM((hg, qb, 2 * qb), F32),
                        pltpu.VMEM((hg, qb, 2 * qb), BF16), pltpu.VMEM((hg, qb, 2 * qb), BF16)],
    )
    return pl.pallas_call(
        functools.partial(_attn_b_prompt_kernel, qb, nblk, lam_init), grid_spec=grid_spec,
        out_shape=jax.ShapeDtypeStruct((B_HEADS, batch * t, LANES), BF16),
        compiler_params=_cparams(("arbitrary", "arbitrary")), name="attn_b_prompt",
    )(pairs, slopes, lam.reshape(1).astype(F32), qkvh, qkvh, vt, kbias, subln_t)


def _attn_b_sample_kernel(dec_t, past, lam_init, slopes, lam_ref, q_ref, kn_ref, vn_ref, ck_ref, cv_ref,
                          tnew_ref, subln_ref, o_ref, tc_ref, tn_ref, pc_ref, pn_ref):
    lam = lam_ref[0]
    subln = subln_ref[...]
    colc = lax.broadcasted_iota(jnp.int32, (1, past), 1).astype(F32) - float(past)
    dn = (((1,), (1,)), ((), ()))
    for h in range(B_HEADS):
        qq = _stack_maps(q_ref[h])
        kc = ck_ref[pl.ds(h, past, stride=B_HEADS), :].astype(BF16)
        tc_ref[h] = lax.dot_general(qq, kc, dn, preferred_element_type=F32)
        tn_ref[h] = lax.dot_general(qq, kn_ref[h], dn, preferred_element_type=F32)
    sums = []
    for h in range(B_HEADS):
        slope = float(slopes[h]) * LOG2E
        s_c = tc_ref[h] + slope * colc
        s_n = tn_ref[h] + slope * tnew_ref[...]
        m = jnp.maximum(jnp.max(s_c, axis=-1, keepdims=True), jnp.max(s_n, axis=-1, keepdims=True))
        p_c = jnp.exp2(s_c - m)
        p_n = jnp.exp2(s_n - m)
        sums.append(jnp.sum(p_c, axis=-1, keepdims=True) + jnp.sum(p_n, axis=-1, keepdims=True))
        pc_ref[h] = p_c.astype(BF16)
        pn_ref[h] = p_n.astype(BF16)
    for h in range(B_HEADS):
        vc = cv_ref[pl.ds(h, past, stride=B_HEADS), :].astype(BF16)
        acc = (jnp.dot(pc_ref[h], vc, preferred_element_type=F32)
               + jnp.dot(pn_ref[h], vn_ref[h], preferred_element_type=F32))
        o_ref[h] = _diff_finish(acc, sums[h], dec_t, lam, lam_init, subln).astype(o_ref.dtype)


def attn_b_sample(qkvh, cache_k, cache_v, lam, subln, lam_init, row0, dec_batch, dec_t):
    past = cache_k.shape[1]
    qi = np.arange(dec_t)
    tnew = (qi[:, None] - np.abs(qi[:, None] - qi[None, :])).astype(np.float32)
    tnew = jnp.asarray(np.concatenate([tnew, tnew], axis=0))
    blk0 = row0 // dec_t
    rows = past * B_HEADS
    cache_k = cache_k.reshape(dec_batch, rows, LANES)
    cache_v = cache_v.reshape(dec_batch, rows, LANES)
    grid_spec = pltpu.PrefetchScalarGridSpec(
        num_scalar_prefetch=1, grid=(dec_batch,),
        in_specs=[
            pl.BlockSpec((B_HEADS, dec_t, LANES), lambda b, *_: (0, blk0 + b, 0)),
            pl.BlockSpec((B_HEADS, dec_t, LANES), lambda b, *_: (1, blk0 + b, 0)),
            pl.BlockSpec((B_HEADS, dec_t, LANES), lambda b, *_: (2, blk0 + b, 0)),
            pl.BlockSpec((None, rows, LANES), lambda b, *_: (b, 0, 0)),
            pl.BlockSpec((None, rows, LANES), lambda b, *_: (b, 0, 0)),
            pl.BlockSpec(tnew.shape, lambda b, *_: (0, 0)),
            pl.BlockSpec((1, LANES), lambda b, *_: (0, 0)),
        ],
        out_specs=pl.BlockSpec((B_HEADS, dec_t, LANES), lambda b, *_: (0, b, 0)),
        scratch_shapes=[pltpu.VMEM((B_HEADS, 2 * dec_t, past), F32), pltpu.VMEM((B_HEADS, 2 * dec_t, dec_t), F32),
                        pltpu.VMEM((B_HEADS, 2 * dec_t, past), BF16), pltpu.VMEM((B_HEADS, 2 * dec_t, dec_t), BF16)],
    )
    return pl.pallas_call(
        functools.partial(_attn_b_sample_kernel, dec_t, past, lam_init, tuple(_alibi(B_HEADS))),
        grid_spec=grid_spec,
        out_shape=jax.ShapeDtypeStruct((B_HEADS, dec_batch * dec_t, LANES), BF16),
        compiler_params=_cparams(("arbitrary",)), name="attn_b_sample",
    )(lam.reshape(1).astype(F32), qkvh, qkvh, qkvh, cache_k, cache_v, tnew, subln.reshape(1, LANES).astype(F32))


ROUTE_ROWS = 40
ROUTE_FIELDS = 8


def _route_t(lt):
    row = lax.broadcasted_iota(jnp.int32, lt.shape, 0).astype(F32)
    big = float(LANES)
    lg = jnp.where(row < N_GROUPS, lt, NEG_INF)
    gmax = jnp.max(lg, axis=0, keepdims=True)
    g_idx = jnp.min(jnp.where(lg == gmax, row, big), axis=0, keepdims=True)
    g_prob = 1.0 / jnp.sum(jnp.exp(lg - gmax), axis=0, keepdims=True)
    lo = N_GROUPS + EXPERTS_PER_GROUP * g_idx
    le = jnp.where((row >= lo) & (row < lo + EXPERTS_PER_GROUP), lt, NEG_INF)
    m1 = jnp.max(le, axis=0, keepdims=True)
    i1 = jnp.min(jnp.where(le == m1, row, big), axis=0, keepdims=True)
    le2 = jnp.where(row == i1, NEG_INF, le)
    m2 = jnp.max(le2, axis=0, keepdims=True)
    i2 = jnp.min(jnp.where(le2 == m2, row, big), axis=0, keepdims=True)
    t = jnp.exp(m2 - m1)
    ga = 1.0 / (1.0 + t)
    gb = t * ga
    return row, i1, i2, g_prob * ga, g_prob * gb


def _unpack_pairs(u):
    lo = pltpu.bitcast(u << 16, F32)
    hi = pltpu.bitcast(u & jnp.uint32(0xFFFF0000), F32)
    return jnp.concatenate([lo, hi], axis=1)


def _pack_pairs(x):
    bits = pltpu.bitcast(x.astype(BF16).astype(F32), jnp.uint32)
    w = x.shape[1] // 2
    return (bits[:, :w] >> 16) | bits[:, w:]


def _out_router_kernel(n_prompt_blocks, heads_in, two_o, two_res, *refs):
    refs = list(refs)
    oa_ref = refs.pop(0)
    ob_ref = refs.pop(0) if two_o else None
    ra_ref = refs.pop(0)
    rb_ref = refs.pop(0) if two_res else None
    (wo_ref, g_ref, wr_ref, wr2_ref, br_ref, utri_ref, h_ref, xn_ref, meta_ref, route_ref, cnt_ref,
     carry_ref) = refs
    is_prompt = pl.program_id(0) < n_prompt_blocks

    def load_o(ref):
        if heads_in:
            return jnp.concatenate([ref[hh] for hh in range(ref.shape[0])], axis=1)
        return ref[...]

    o = load_o(oa_ref)
    if two_o:
        o = jnp.where(is_prompt, o, load_o(ob_ref))
    res = ra_ref[...]
    if two_res:
        res = jnp.where(is_prompt, res, rb_ref[...])
    h = res + jnp.dot(o, wo_ref[...], preferred_element_type=F32)
    h_ref[...] = h
    ms = jnp.mean(h * h, axis=-1, keepdims=True)
    xn = h * lax.rsqrt(ms + NORM_EPS) * g_ref[...]
    x_hi = xn.astype(BF16)
    _store_planes(xn_ref, _pack_pairs(xn))
    x_lo = (xn - x_hi.astype(F32)).astype(BF16)
    r = jnp.dot(x_hi, wr_ref[...], preferred_element_type=F32)
    logits = (r[:, :LANES] + r[:, LANES:]) + jnp.dot(x_lo, wr2_ref[...], preferred_element_type=F32) + br_ref[...]
    row, i1, i2, g1, g2 = _route_t(logits.T[:ROUTE_ROWS])

    @pl.when(pl.program_id(0) == 0)
    def _():
        carry_ref[...] = jnp.zeros_like(carry_ref)

    sel1 = row == i1
    sel2 = row == i2
    onehot = jnp.where(sel1 | sel2, 1.0, 0.0)
    before = jnp.dot(onehot.astype(BF16), utri_ref[...], preferred_element_type=F32) + carry_ref[:, 0:1]
    rank1 = jnp.sum(jnp.where(sel1, before, 0.0), axis=0, keepdims=True)
    rank2 = jnp.sum(jnp.where(sel2, before, 0.0), axis=0, keepdims=True)
    carry_ref[...] = carry_ref[...] + jnp.sum(onehot, axis=1, keepdims=True)
    cnt_ref[...] = carry_ref[...]
    fields = [i1 - N_GROUPS, i2 - N_GROUPS, g1, g2, rank1, rank2]
    fields += [jnp.zeros_like(g1)] * (ROUTE_FIELDS - len(fields))
    route = jnp.concatenate(fields, axis=0)
    route_ref[...] = route
    meta_ref[...] = jnp.concatenate([route, jnp.zeros((LANES - ROUTE_FIELDS, route.shape[1]), F32)], axis=0).T


def out_router(oa, ob, ra, rb, wo, g, w_rg, b_rg, w_re, b_re, heads_in, tr):
    d = ra.shape[1]
    if heads_in:
        na = oa.shape[1]
        nb = 0 if ob is None else ob.shape[1]
    else:
        na = oa.shape[0]
        nb = 0 if ob is None else ob.shape[0]
    n = na + nb
    npb = na // tr
    two_o = ob is not None
    two_res = rb is not None
    first = lambda i: (jnp.minimum(i, npb - 1), 0)
    second = lambda i: (jnp.maximum(i - npb, 0), 0)
    plain = lambda i: (i, 0)
    in_specs, args = [], []

    def add_o(x, imap):
        if heads_in:
            in_specs.append(pl.BlockSpec((x.shape[0], tr, LANES), lambda i: (0, imap(i)[0], 0)))
        else:
            in_specs.append(pl.BlockSpec((tr, x.shape[1]), imap))
        args.append(x)

    add_o(oa, first if two_o else plain)
    if two_o:
        add_o(ob, second)
    in_specs.append(pl.BlockSpec((tr, d), first if two_res else plain))
    args.append(ra)
    if two_res:
        in_specs.append(pl.BlockSpec((tr, d), second))
        args.append(rb)
    wr = jnp.zeros((d, LANES), F32).at[:, :N_GROUPS].set(w_rg.astype(F32))
    wr = wr.at[:, N_GROUPS:N_GROUPS + N_EXPERTS].set(w_re.astype(F32))
    wr_hi = wr.astype(BF16)
    wr_lo = (wr - wr_hi.astype(F32)).astype(BF16)
    br = jnp.zeros((1, LANES), F32).at[0, :N_GROUPS].set(b_rg.astype(F32))
    br = br.at[0, N_GROUPS:N_GROUPS + N_EXPERTS].set(b_re.astype(F32))
    utri = jnp.asarray(np.triu(np.ones((tr, tr), np.float32), 1), BF16)
    consts = [wo, g.reshape(1, d).astype(F32), jnp.concatenate([wr_hi, wr_lo], axis=1), wr_hi, br, utri]
    for cst in consts:
        in_specs.append(pl.BlockSpec(cst.shape, lambda i: (0, 0)))
        args.append(cst)
    parts = d // 2 // SC_ROW_WORDS
    out_shape = [jax.ShapeDtypeStruct((n, d), F32), jax.ShapeDtypeStruct((parts, n, SC_ROW_WORDS), jnp.uint32),
                 jax.ShapeDtypeStruct((n, LANES), F32), jax.ShapeDtypeStruct((ROUTE_FIELDS, n), F32),
                 jax.ShapeDtypeStruct((ROUTE_ROWS, LANES), F32)]
    out_specs = [pl.BlockSpec((tr, d), plain), pl.BlockSpec((parts, tr, SC_ROW_WORDS), lambda i: (0, i, 0)),
                 pl.BlockSpec((tr, LANES), plain), pl.BlockSpec((ROUTE_FIELDS, tr), lambda i: (0, i)),
                 pl.BlockSpec((ROUTE_ROWS, LANES), lambda i: (0, 0))]
    return pl.pallas_call(
        functools.partial(_out_router_kernel, npb, heads_in, two_o, two_res),
        grid=(n // tr,), in_specs=in_specs, out_specs=out_specs, out_shape=out_shape,
        scratch_shapes=[pltpu.VMEM((ROUTE_ROWS, LANES), F32)],
        compiler_params=_cparams(("arbitrary",)), name="out_router")(*args)


def _expert_kernel(tm, tps, te_ref, nt_ref, x_ref, *refs):
    w_refs, o_ref = refs[:3 * tps], refs[3 * tps]
    first = pl.program_id(0) * tps
    n_valid = jnp.clip(nt_ref[0] - first, 0, tps)

    def tile(k):
        wg_ref, wu_ref, wd_ref = w_refs[3 * k:3 * k + 3]
        rows = slice(k * tm, (k + 1) * tm)
        x = _unpack_pairs(jnp.concatenate([x_ref[q, rows, :] for q in range(x_ref.shape[0])], axis=1)).astype(BF16)
        a = jnp.dot(x, wg_ref[...].astype(BF16), preferred_element_type=F32)
        u = jnp.dot(x, wu_ref[...].astype(BF16), preferred_element_type=F32)
        hh = (a * (1.0 / (1.0 + jnp.exp(-a))) * u).astype(BF16)
        packed = _pack_pairs(jnp.dot(hh, wd_ref[...].astype(BF16), preferred_element_type=F32))
        for q in range(o_ref.shape[0]):
            o_ref[q, rows, :] = packed[:, q * o_ref.shape[2]:(q + 1) * o_ref.shape[2]]

    for nv in range(tps + 1):

        @pl.when(n_valid == nv)
        def _(nv=nv):
            for k in range(nv):
                tile(k)
            if nv < tps:
                o_ref[:, nv * tm:, :] = jnp.zeros((o_ref.shape[0], (tps - nv) * tm, o_ref.shape[2]), o_ref.dtype)


def expert_mlp(xs, tile_expert, n_tiles, w_gate, w_up, w_down, layer, tm):
    parts, p, w = xs.shape
    d, f = w_gate.shape[-2:]
    epg = w_gate.shape[2]
    tps = 2 if (p // tm) % 2 == 0 else 1
    w_specs, w_args = [], []
    for k in range(tps):
        wmap = lambda t, te, nt, k=k: (layer, te[tps * t + k] // epg, te[tps * t + k] % epg, 0, 0)
        w_specs += [pl.BlockSpec((None, None, None, d, f), wmap), pl.BlockSpec((None, None, None, d, f), wmap),
                    pl.BlockSpec((None, None, None, f, d), wmap)]
        w_args += [w_gate, w_up, w_down]
    grid_spec = pltpu.PrefetchScalarGridSpec(
        num_scalar_prefetch=2, grid=(p // (tps * tm),),
        in_specs=[pl.BlockSpec((parts, tps * tm, w), lambda t, te, nt: (0, t, 0))] + w_specs,
        out_specs=pl.BlockSpec((parts, tps * tm, w), lambda t, te, nt: (0, t, 0)),
    )
    return pl.pallas_call(
        functools.partial(_expert_kernel, tm, tps), grid_spec=grid_spec,
        out_shape=jax.ShapeDtypeStruct((parts, p, w), jnp.uint32),
        compiler_params=_cparams(("arbitrary",)), name="expert_mlp")(tile_expert, n_tiles, xs, *w_args)


def sc_scatter_rows(x2, idx, n_out):
    mesh = plsc.VectorSubcoreMesh(core_axis_name="c", subcore_axis_name="s")
    n_src_win = x2.shape[0] // SC_WINDOW

    @functools.partial(pl.kernel, out_type=jax.ShapeDtypeStruct((n_out, SC_ROW_WORDS), x2.dtype), mesh=mesh)
    def scatter(x_hbm, i_hbm, o_hbm):
        def body(x_vmem, i_vmem):
            pltpu.sync_copy(x_vmem, o_hbm.at[i_vmem.at[0]])

        pltpu.emit_pipeline(
            body, grid=(n_out // SC_WINDOW,),
            in_specs=[pl.BlockSpec((SC_WINDOW, SC_ROW_WORDS), lambda i: (lax.rem(i, n_src_win), 0)),
                      pl.BlockSpec((1, SC_WINDOW), lambda i: (0, i))],
            out_specs=[], core_axis_name=("c", "s"), dimension_semantics=(pltpu.PARALLEL,),
        )(x_hbm, i_hbm)

    return scatter(x2, idx)


def sc_gather_rows(x2, idx):
    mesh = plsc.VectorSubcoreMesh(core_axis_name="c", subcore_axis_name="s")
    n_col, n = idx.shape

    @functools.partial(pl.kernel, out_type=jax.ShapeDtypeStruct((n, n_col * SC_ROW_WORDS), x2.dtype), mesh=mesh)
    def gather(x_hbm, i_hbm, o_hbm):
        def body(i_vmem, o_vmem):
            pltpu.sync_copy(x_hbm.at[i_vmem.at[0]], o_vmem)

        pltpu.emit_pipeline(
            body, grid=(n // SC_WINDOW, n_col),
            in_specs=[pl.BlockSpec((1, SC_WINDOW), lambda i, c: (c, i))],
            out_specs=[pl.BlockSpec((SC_WINDOW, SC_ROW_WORDS), lambda i, c: (i, c))],
            core_axis_name=("c", "s"), dimension_semantics=(pltpu.PARALLEL, pltpu.ARBITRARY),
        )(i_hbm, o_hbm)

    return gather(x2, idx)


def moe_dispatch(xn_packed, route, counts, tm):
    parts, n, w = xn_packed.shape
    e = route[0:2].astype(jnp.int32)
    rank = route[4:6].astype(jnp.int32)
    counts = counts[N_GROUPS:N_GROUPS + N_EXPERTS, 0].astype(jnp.int32)
    padded = ((counts + tm - 1) // tm) * tm
    pend = jnp.cumsum(padded)
    pstart = pend - padded
    ids = jnp.arange(N_EXPERTS, dtype=jnp.int32)
    pos = jnp.sum(jnp.where(e[..., None] == ids, pstart, 0), axis=-1) + rank
    p = 2 * n + N_EXPERTS * tm
    n_pad = p - 2 * n
    gap_len = jnp.concatenate([padded - counts, (p - pend[-1]).reshape(1)])
    gap_first = jnp.concatenate([pstart + counts, pend[-1:]])
    gap_end = jnp.cumsum(gap_len)
    k = jnp.arange(n_pad, dtype=jnp.int32)
    gap = jnp.sum((k[:, None] >= gap_end[None, :]).astype(jnp.int32), axis=1)
    shift = gap_first - (gap_end - gap_len)
    pad_pos = k + jnp.sum(jnp.where(gap[:, None] == jnp.arange(N_EXPERTS + 1), shift, 0), axis=-1)
    planes = lambda rows: jnp.concatenate([rows + q * p for q in range(parts)])
    scat_idx = jnp.concatenate([planes(pos[0]), planes(pos[1]), planes(pad_pos)]).reshape(1, parts * p)
    gath_idx = jnp.stack([pos[k] + q * p for k in range(2) for q in range(parts)])
    xs = sc_scatter_rows(xn_packed.reshape(parts * n, w), scat_idx, parts * p).reshape(parts, p, w)
    tile_start = jnp.arange(p // tm, dtype=jnp.int32) * tm
    tile_expert = jnp.sum((tile_start[:, None] >= pend[None, :]).astype(jnp.int32), axis=1)
    tile_expert = jnp.minimum(tile_expert, N_EXPERTS - 1)
    n_tiles = (pend[-1] // tm).astype(jnp.int32).reshape(1)
    return xs, tile_expert, n_tiles, gath_idx


def moe_experts(xn_packed, route, counts, w_gate, w_up, w_down, layer, tm):
    parts, n, w = xn_packed.shape
    xs, tile_expert, n_tiles, gath_idx = moe_dispatch(xn_packed, route, counts, tm)
    o = expert_mlp(xs, tile_expert, n_tiles, w_gate, w_up, w_down, layer, tm)
    return sc_gather_rows(o.reshape(parts * o.shape[1], w), gath_idx)


def _final_norm_kernel(n_prompt_blocks, h_ref, o0_ref, o1_ref, meta_ref, g_ref, yp_ref, ys_ref):
    meta = meta_ref[...]
    x = (h_ref[...] + meta[:, 2:3] * _unpack_pairs(o0_ref[...])) + meta[:, 3:4] * _unpack_pairs(o1_ref[...])
    ms = jnp.mean(x * x, axis=-1, keepdims=True)
    y = x * lax.rsqrt(ms + NORM_EPS) * g_ref[...]
    i = pl.program_id(0)

    @pl.when(i < n_prompt_blocks)
    def _():
        yp_ref[...] = y

    @pl.when(i >= n_prompt_blocks)
    def _():
        ys_ref[...] = y


def final_norm(h, og, meta, g, n_prompt, tr):
    n, d = h.shape
    npb = n_prompt // tr
    return pl.pallas_call(
        functools.partial(_final_norm_kernel, npb),
        grid=(n // tr,),
        in_specs=[pl.BlockSpec((tr, d), lambda i: (i, 0)),
                  pl.BlockSpec((tr, og.shape[1] // 2), lambda i: (i, 0)),
                  pl.BlockSpec((tr, og.shape[1] // 2), lambda i: (i, 1)),
                  pl.BlockSpec((tr, LANES), lambda i: (i, 0)), pl.BlockSpec((1, d), lambda i: (0, 0))],
        out_specs=[pl.BlockSpec((tr, d), lambda i: (jnp.minimum(i, npb - 1), 0)),
                   pl.BlockSpec((tr, d), lambda i: (jnp.maximum(i - npb, 0), 0))],
        out_shape=[jax.ShapeDtypeStruct((n_prompt, d), F32), jax.ShapeDtypeStruct((n - n_prompt, d), F32)],
        compiler_params=_cparams(("arbitrary",)), name="final_norm")(h, og, og, meta, g.reshape(1, d).astype(F32))


def _pick_rows_tile(n_prompt, n_sample):
    tr = ROW_TILE
    while tr >= SUBLANES:
        if n_prompt % tr == 0 and n_sample % tr == 0:
            return tr
        tr //= 2
    raise ValueError("row counts must be multiples of the sublane count")


def kernel(x_prompt, x_sample, cache_a_k, cache_a_v, cache_b_k, cache_b_v, norm_mix, norm_ffn, norm_final,
           w_a_qkv, b_a_qkv, a_sinks, w_a_o, w_b_qkv, b_lambda, b_subln, w_b_o,
           w_route_group, b_route_group, w_route_expert, b_route_expert, w_gate, w_up, w_down):
    batch, t, d = x_prompt.shape
    dec_batch, dec_t, _ = x_sample.shape
    n_p, n_s = batch * t, dec_batch * dec_t
    tr = _pick_rows_tile(n_p, n_s)
    tm = min(EXPERT_TILE, 4 * tr)
    xp = x_prompt.reshape(n_p, d)
    xs = x_sample.reshape(n_s, d)
    nq_a = A_HEADS * A_HD
    nkv_a = A_KV_HEADS * A_HD

    a_scale = jnp.where(jnp.arange(nq_a + 2 * nkv_a) < nq_a, A_HD ** -0.5 * LOG2E, 1.0).astype(F32)
    q_a, kv_ap, kv_as = norm_proj(xp, xs, norm_mix[0], (w_a_qkv[0] * a_scale).astype(BF16), b_a_qkv[0] * a_scale,
                                  [(0, nq_a, "flat", BF16), (nq_a, nq_a + 2 * nkv_a, "split", F32)], tr, n_p)
    qb_a = min(CHUNK * WIN_CHUNKS, t)
    o_ap = attn_a_prompt(q_a, kv_ap, a_sinks[0], batch, t, qb_a)
    kv_s = kv_as.reshape(dec_batch, dec_t, 2 * nkv_a)
    past_a = cache_a_k.shape[2]
    kband = jnp.concatenate([cache_a_k[0].reshape(dec_batch, past_a, nkv_a), kv_s[..., :nkv_a]], axis=1)
    vband = jnp.concatenate([cache_a_v[0].reshape(dec_batch, past_a, nkv_a), kv_s[..., nkv_a:]], axis=1)
    o_as = attn_a_sample(q_a, kband, vband, a_sinks[0], n_p, dec_batch, dec_t)
    h, xn, meta, route, counts = out_router(o_ap, o_as, xp, xs, w_a_o[0].astype(BF16), norm_ffn[0], w_route_group[0],
                                            b_route_group[0], w_route_expert[0], b_route_expert[0], False, tr)
    og = moe_experts(xn, route, counts, w_gate, w_up, w_down, 0, tm)

    keep = min(CHUNK * WIN_CHUNKS, t)
    kv_p = kv_ap.reshape(batch, t, 2 * nkv_a)[:, t - keep:]
    a_k_prompt = kv_p[..., :nkv_a].reshape(1, batch, keep, A_KV_HEADS, A_HD)
    a_v_prompt = kv_p[..., nkv_a:].reshape(1, batch, keep, A_KV_HEADS, A_HD)
    a_k_sample = kband[:, dec_t:].reshape(1, dec_batch, past_a, A_KV_HEADS, A_HD)
    a_v_sample = vband[:, dec_t:].reshape(1, dec_batch, past_a, A_KV_HEADS, A_HD)

    lam_init = 0.8 - 0.6 * float(np.exp(-0.3 * 1))
    lp = b_lambda[0].astype(F32)
    lam = jnp.exp(jnp.sum(lp[0] * lp[1])) - jnp.exp(jnp.sum(lp[2] * lp[3])) + lam_init
    wb = B_HEADS * 2 * B_HD
    col_scale = jnp.where(jnp.arange(3 * wb) < wb, B_Q_SCALE, 1.0).astype(F32)
    h, qkvh, k_bp, k_bs, v_bp, v_bs, vt = norm_proj(
        h, None, norm_mix[1], (w_b_qkv[0] * col_scale).astype(BF16), None,
        [(0, 3 * wb, "heads", BF16), (wb, 2 * wb, "split", F32), (2 * wb, 3 * wb, "split", F32),
         (2 * wb, 3 * wb, "heads_t", BF16)], tr, n_p, og=og, meta=meta)
    qb_b = min(ATTN_BLOCK, t)
    o_bp = attn_b_prompt(qkvh, vt, lam, b_subln[0], lam_init, batch, t, qb_b)
    o_bs = attn_b_sample(qkvh, cache_b_k[0], cache_b_v[0], lam, b_subln[0], lam_init, n_p, dec_batch, dec_t)
    hd2 = 2 * B_HD
    o_bs, k_bp, v_bp = lax.optimization_barrier((o_bs, k_bp, v_bp))
    b_k_prompt = k_bp.reshape(1, batch, t, B_HEADS, hd2)
    b_v_prompt = v_bp.reshape(1, batch, t, B_HEADS, hd2)
    o_bp, b_k_prompt, b_v_prompt = lax.optimization_barrier((o_bp, b_k_prompt, b_v_prompt))
    h, xn, meta, route, counts = out_router(o_bp, o_bs, h, None, w_b_o[0].astype(BF16), norm_ffn[1], w_route_group[1],
                                            b_route_group[1], w_route_expert[1], b_route_expert[1], True, tr)
    og = moe_experts(xn, route, counts, w_gate, w_up, w_down, 1, tm)

    y_p, y_s = final_norm(h, og, meta, norm_final, n_p, tr)
    return (y_p.reshape(batch, t, d), y_s.reshape(dec_batch, dec_t, d),
            a_k_prompt, a_v_prompt, a_k_sample, a_v_sample, b_k_prompt, b_v_prompt,
            k_bs.reshape(1, dec_batch, dec_t, B_HEADS, hd2), v_bs.reshape(1, dec_batch, dec_t, B_HEADS, hd2))
```

```python
import functools

import jax
import jax.numpy as jnp
import numpy as np
from jax import lax
from jax.experimental import pallas as pl
from jax.experimental.pallas import tpu as pltpu
from jax.experimental.pallas import tpu_sc as plsc

F32 = jnp.float32
BF16 = jnp.bfloat16
NEG_INF = float("-inf")

CHUNK = 64
WIN_CHUNKS = 2
A_HEADS, A_KV_HEADS, A_GROUP, A_HD = 16, 4, 4, 64
B_HEADS, B_HD = 8, 64
N_GROUPS, EXPERTS_PER_GROUP = 4, 8
N_EXPERTS = N_GROUPS * EXPERTS_PER_GROUP
NORM_EPS = 1e-6
SUBLN_EPS = 1e-5

LANES = 128
SUBLANES = 8
ROW_TILE = 512
EXPERT_TILE = 512
ATTN_BLOCK = 256
COL_CHUNK = 512
HEAD_GROUP = 2
ONES_ROWS = 16
SC_WINDOW = 128
SC_ROW_WORDS = 256
VMEM_LIMIT = 48 * 1024 * 1024


def _cparams(sem, flags=None):
    return pltpu.CompilerParams(dimension_semantics=sem, vmem_limit_bytes=VMEM_LIMIT, flags=flags)


def _alibi(n):
    return 2.0 ** (-8.0 * np.arange(1, n + 1) / n)


def _store_planes(ref, x):
    w = ref.shape[2]
    for p in range(ref.shape[0]):
        ref[p] = x[:, p * w:(p + 1) * w]


def _add_expert_outputs(x, og_ref, meta_ref):
    half = og_ref.shape[1] // 2
    meta = meta_ref[...]
    return (x + meta[:, 2:3] * _unpack_pairs(og_ref[:, :half])) + meta[:, 3:4] * _unpack_pairs(og_ref[:, half:])


def _norm_proj_kernel(n_prompt_blocks, two_src, has_bias, combine, outs, *refs):
    refs = list(refs)
    xa_ref = refs.pop(0)
    xb_ref = refs.pop(0) if two_src else None
    og_ref, meta_ref = (refs.pop(0), refs.pop(0)) if combine else (None, None)
    g_ref = refs.pop(0)
    w_ref = refs.pop(0)
    b_ref = refs.pop(0) if has_bias else None
    out_refs = refs
    x = xa_ref[...]
    if two_src:
        x = jnp.where(pl.program_id(0) < n_prompt_blocks, x, xb_ref[...])
    if combine:
        x = _add_expert_outputs(x, og_ref, meta_ref)
        out_refs.pop(0)[...] = x
    ms = jnp.mean(x * x, axis=-1, keepdims=True)
    xn = (x * lax.rsqrt(ms + NORM_EPS) * g_ref[...]).astype(BF16)
    is_prompt = pl.program_id(0) < n_prompt_blocks
    out_refs = list(out_refs)
    targets = []
    for c0, c1, layout in outs:
        targets.append((c0, c1, layout, out_refs.pop(0), out_refs.pop(0) if layout == "split" else None))
    starts = list(range(0, w_ref.shape[1], COL_CHUNK))
    transposed = [s for s in starts if any(lay == "heads_t" and c0 <= s < c1 for c0, c1, lay in outs)]
    for s in transposed + [s for s in starts if s not in transposed]:
        e = s + COL_CHUNK
        r = jnp.dot(xn, w_ref[:, s:e], preferred_element_type=F32)
        if has_bias:
            r = r + b_ref[:, s:e]
        for c0, c1, layout, o_ref, o2_ref in targets:
            if not (c0 <= s and e <= c1):
                continue
            rr = r.astype(o_ref.dtype)
            if layout == "flat":
                o_ref[:, s - c0:e - c0] = rr
            elif layout == "split":

                @pl.when(is_prompt)
                def _(rr=rr, o_ref=o_ref, s=s, e=e, c0=c0):
                    o_ref[:, s - c0:e - c0] = rr

                @pl.when(jnp.logical_not(is_prompt))
                def _(rr=rr, o2_ref=o2_ref, s=s, e=e, c0=c0):
                    o2_ref[:, s - c0:e - c0] = rr
            elif layout == "heads":
                for hh in range((e - s) // LANES):
                    o_ref[(s - c0) // LANES + hh] = rr[:, hh * LANES:(hh + 1) * LANES]
            else:
                ones = jnp.ones((ONES_ROWS, r.shape[0]), o_ref.dtype)
                for hh in range((e - s) // LANES):
                    head = (s - c0) // LANES + hh
                    o_ref[head, :LANES, :] = r[:, hh * LANES:(hh + 1) * LANES].T.astype(o_ref.dtype)
                    o_ref[head, LANES:, :] = ones


def norm_proj(xa, xb, g, w, b, outs, tr, n_prompt, og=None, meta=None):
    na, d = xa.shape
    nb = 0 if xb is None else xb.shape[0]
    n = na + nb
    npb = n_prompt // tr
    two = xb is not None
    combine = og is not None
    in_specs = [pl.BlockSpec((tr, d), (lambda i: (jnp.minimum(i, npb - 1), 0)) if two else (lambda i: (i, 0)))]
    args = [xa]
    if two:
        in_specs.append(pl.BlockSpec((tr, d), lambda i: (jnp.maximum(i - npb, 0), 0)))
        args.append(xb)
    if combine:
        in_specs += [pl.BlockSpec((tr, og.shape[1]), lambda i: (i, 0)), pl.BlockSpec((tr, LANES), lambda i: (i, 0))]
        args += [og, meta]
    in_specs.append(pl.BlockSpec((1, d), lambda i: (0, 0)))
    args.append(g.reshape(1, d).astype(F32))
    in_specs.append(pl.BlockSpec(w.shape, lambda i: (0, 0)))
    args.append(w)
    if b is not None:
        in_specs.append(pl.BlockSpec((1, w.shape[1]), lambda i: (0, 0)))
        args.append(b.reshape(1, -1).astype(F32))
    out_shapes, out_specs = [], []
    if combine:
        out_shapes.append(jax.ShapeDtypeStruct((n, d), F32))
        out_specs.append(pl.BlockSpec((tr, d), lambda i: (i, 0)))
    for c0, c1, layout, dt in outs:
        assert c0 % COL_CHUNK == 0 and c1 % COL_CHUNK == 0
        if layout == "flat":
            out_shapes.append(jax.ShapeDtypeStruct((n, c1 - c0), dt))
            out_specs.append(pl.BlockSpec((tr, c1 - c0), lambda i: (i, 0)))
        elif layout == "split":
            out_shapes.append(jax.ShapeDtypeStruct((n_prompt, c1 - c0), dt))
            out_specs.append(pl.BlockSpec((tr, c1 - c0), lambda i: (jnp.minimum(i, npb - 1), 0)))
            out_shapes.append(jax.ShapeDtypeStruct((n - n_prompt, c1 - c0), dt))
            out_specs.append(pl.BlockSpec((tr, c1 - c0), lambda i: (jnp.maximum(i - npb, 0), 0)))
        elif layout == "heads":
            nh = (c1 - c0) // LANES
            out_shapes.append(jax.ShapeDtypeStruct((nh, n, LANES), dt))
            out_specs.append(pl.BlockSpec((nh, tr, LANES), lambda i: (0, i, 0)))
        else:
            assert layout == "heads_t"
            nh = (c1 - c0) // LANES
            out_shapes.append(jax.ShapeDtypeStruct((nh, LANES + ONES_ROWS, n), dt))
            out_specs.append(pl.BlockSpec((nh, LANES + ONES_ROWS, tr), lambda i: (0, 0, i)))
    kern = functools.partial(_norm_proj_kernel, npb, two, b is not None, combine,
                             [(c0, c1, lay) for c0, c1, lay, _ in outs])
    return pl.pallas_call(
        kern, grid=(n // tr,), in_specs=in_specs, out_specs=out_specs, out_shape=out_shapes,
        compiler_params=_cparams(("arbitrary",)), name="norm_proj")(*args)


def _attn_a_chunk(q, k, v, bias_ref, sink_ref, valid):
    nq = q.shape[0]
    outs = []
    for kh in range(A_KV_HEADS):
        qs = jnp.concatenate(
            [q[:, (kh * A_GROUP + g) * A_HD:(kh * A_GROUP + g + 1) * A_HD] for g in range(A_GROUP)], axis=0)
        kk = k[:, kh * A_HD:(kh + 1) * A_HD]
        s = lax.dot_general(qs, kk, (((1,), (1,)), ((), ())), preferred_element_type=F32) + bias_ref[kh]
        if valid is not None:
            s = jnp.where(valid, s, NEG_INF)
        sk = sink_ref[kh]
        m = jnp.maximum(jnp.max(s, axis=-1, keepdims=True), sk)
        e = jnp.exp2(s - m)
        den = jnp.sum(e, axis=-1, keepdims=True) + jnp.exp2(sk - m)
        p = (e * (1.0 / den)).astype(BF16)
        o = jnp.dot(p, v[:, kh * A_HD:(kh + 1) * A_HD], preferred_element_type=F32)
        for g in range(A_GROUP):
            outs.append(o[g * nq:(g + 1) * nq])
    return jnp.concatenate(outs, axis=1)


def _attn_a_prompt_kernel(qb, nsub, q_ref, kvc_ref, kvp_ref, bias_ref, sink_ref, o_ref, t_ref, p_ref):
    cb = pl.program_id(1)
    back = CHUNK * WIN_CHUNKS
    nk = back + qb
    nkv = A_KV_HEADS * A_HD
    kvfull = jnp.concatenate([kvp_ref[...], kvc_ref[...]], axis=0)
    k_all = kvfull[:, :nkv].astype(BF16)
    vt_all = kvfull[:, nkv:].T.astype(BF16)
    dn = (((1,), (1,)), ((), ()))
    units = [(s, kh) for s in range(nsub) for kh in range(A_KV_HEADS)]
    for s, kh in units:
        qs = jnp.concatenate([q_ref[s * qb:(s + 1) * qb, (kh * A_GROUP + g) * A_HD:(kh * A_GROUP + g + 1) * A_HD]
                              for g in range(A_GROUP)], axis=0)
        t_ref[s * A_KV_HEADS + kh] = lax.dot_general(k_all[s * qb:s * qb + nk, kh * A_HD:(kh + 1) * A_HD], qs, dn,
                                                     preferred_element_type=F32)
    for s in range(nsub):
        outs = []
        for kh in range(A_KV_HEADS):
            u = s * A_KV_HEADS + kh
            st = t_ref[u] + bias_ref[kh]
            if s == 0:
                key_pos = lax.broadcasted_iota(jnp.int32, (nk, 1), 0) + (cb * nsub * qb - back)
                st = st + jnp.where(key_pos >= 0, 0.0, NEG_INF)
            sk = sink_ref[kh]
            m = jnp.maximum(jnp.max(st, axis=0, keepdims=True), sk)
            e = jnp.exp2(st - m)
            inv_den = 1.0 / (jnp.sum(e, axis=0, keepdims=True) + jnp.exp2(sk - m))
            p_ref[u] = e.astype(BF16)
            ot = jnp.dot(vt_all[kh * A_HD:(kh + 1) * A_HD, s * qb:s * qb + nk], p_ref[u],
                         preferred_element_type=F32)
            o = (ot * inv_den).T
            for g in range(A_GROUP):
                outs.append(o[g * qb:(g + 1) * qb])
        o_ref[s * qb:(s + 1) * qb, :] = jnp.concatenate(outs, axis=1).astype(o_ref.dtype)


def _a_prompt_tables(qb, sinks):
    back = CHUNK * WIN_CHUNKS
    slopes = _alibi(A_HEADS).reshape(A_KV_HEADS, A_GROUP)
    qpos = np.arange(qb)
    kpos = np.arange(back + qb) - back
    dist = np.abs(kpos[:, None] - qpos[None, :]).astype(np.float64)
    band_lo = (qpos // CHUNK) * CHUNK - back
    in_band = (kpos[:, None] >= band_lo[None, :]) & (kpos[:, None] < band_lo[None, :] + back + CHUNK)
    bias = np.where(in_band[None, None], -slopes[:, :, None, None] * dist[None, None], -np.inf)
    bias = np.transpose(bias, (0, 2, 1, 3)).reshape(A_KV_HEADS, back + qb, A_GROUP * qb) * LOG2E
    sk = jnp.repeat(sinks.astype(F32).reshape(A_KV_HEADS, A_GROUP), qb, axis=1)[:, None, :] * LOG2E
    return jnp.asarray(bias, F32), sk


def _a_tables(qpos, kpos, sinks):
    slopes = _alibi(A_HEADS).reshape(A_KV_HEADS, A_GROUP)
    dist = np.abs(qpos[:, None] - kpos[None, :]).astype(np.float64)
    bias = -slopes[:, :, None, None] * dist
    nq = len(qpos)
    bias = jnp.asarray(bias.reshape(A_KV_HEADS, A_GROUP * nq, len(kpos)) * LOG2E, F32)
    sk = jnp.repeat(sinks.astype(F32).reshape(A_KV_HEADS, A_GROUP), nq, axis=1)[..., None] * LOG2E
    return bias, sk


def attn_a_prompt(q_all, kv_all, sinks, batch, t, qb):
    nq = A_HEADS * A_HD
    nkv2 = 2 * A_KV_HEADS * A_HD
    back = CHUNK * WIN_CHUNKS
    bias, sk = _a_prompt_tables(qb, sinks)
    nsub = max(s for s in (1, 2, 4, 8) if t % (s * qb) == 0)
    rows = nsub * qb
    nblk = t // rows
    r = rows // back
    return pl.pallas_call(
        functools.partial(_attn_a_prompt_kernel, qb, nsub),
        grid=(batch, nblk),
        in_specs=[
            pl.BlockSpec((rows, nq), lambda b, i: (b * nblk + i, 0)),
            pl.BlockSpec((rows, nkv2), lambda b, i: (b * nblk + i, 0)),
            pl.BlockSpec((back, nkv2), lambda b, i: (jnp.maximum((b * nblk + i) * r - 1, 0), 0)),
            pl.BlockSpec(bias.shape, lambda b, i: (0, 0, 0)),
            pl.BlockSpec(sk.shape, lambda b, i: (0, 0, 0)),
        ],
        out_specs=pl.BlockSpec((rows, nq), lambda b, i: (b * nblk + i, 0)),
        out_shape=jax.ShapeDtypeStruct((batch * t, nq), BF16),
        scratch_shapes=[pltpu.VMEM((nsub * A_KV_HEADS, back + qb, A_GROUP * qb), F32),
                        pltpu.VMEM((nsub * A_KV_HEADS, back + qb, A_GROUP * qb), BF16)],
        compiler_params=_cparams(("arbitrary", "arbitrary")), name="attn_a_prompt",
    )(q_all, kv_all, kv_all, bias, sk)


def _attn_a_sample_kernel(q_ref, k_ref, v_ref, bias_ref, sink_ref, o_ref):
    o = _attn_a_chunk(q_ref[...], k_ref[...].astype(BF16), v_ref[...].astype(BF16), bias_ref, sink_ref, None)
    o_ref[...] = o.astype(o_ref.dtype)


def attn_a_sample(q_all, kband, vband, sinks, row0, dec_batch, dec_t):
    nq = A_HEADS * A_HD
    s = kband.shape[1]
    past = s - dec_t
    kpos = np.concatenate([np.arange(past) - past, np.arange(dec_t)])
    bias, sk = _a_tables(np.arange(dec_t), kpos, sinks)
    blk0 = row0 // dec_t
    return pl.pallas_call(
        _attn_a_sample_kernel,
        grid=(dec_batch,),
        in_specs=[
            pl.BlockSpec((dec_t, nq), lambda b: (blk0 + b, 0)),
            pl.BlockSpec((None, s, kband.shape[2]), lambda b: (b, 0, 0)),
            pl.BlockSpec((None, s, vband.shape[2]), lambda b: (b, 0, 0)),
            pl.BlockSpec(bias.shape, lambda b: (0, 0, 0)),
            pl.BlockSpec(sk.shape, lambda b: (0, 0, 0)),
        ],
        out_specs=pl.BlockSpec((dec_t, nq), lambda b: (b, 0)),
        out_shape=jax.ShapeDtypeStruct((dec_batch * dec_t, nq), BF16),
        compiler_params=_cparams(("arbitrary",)), name="attn_a_sample",
    )(q_all, kband, vband, bias, sk)


B_Q_SCALE = B_HD ** -0.5 * float(np.log2(np.e))
LOG2E = float(np.log2(np.e))


def _stack_maps(qh):
    lane = lax.broadcasted_iota(jnp.int32, qh.shape, 1)
    zero = jnp.zeros_like(qh)
    return jnp.concatenate([jnp.where(lane < B_HD, qh, zero), jnp.where(lane >= B_HD, qh, zero)], axis=0)


def _diff_finish(acc, l, nq, lam, lam_init, subln):
    inv = 1.0 / l
    o = acc[:nq] * inv[:nq] - lam * (acc[nq:] * inv[nq:])
    ms = jnp.mean(o * o, axis=-1, keepdims=True)
    return o * lax.rsqrt(ms + SUBLN_EPS) * subln * (1.0 - lam_init)


def _attn_b_prompt_kernel(qb, nb, lam_init, slope_ref, lam_ref, q_ref, k_ref, vt_ref, kbias_ref,
                          subln_ref, o_ref, acc_ref, m_ref, al_ref, qq_ref, t0_ref, t1_ref, p0_ref, p1_ref):
    g = pl.program_id(1)
    lam = lam_ref[0]
    dn = (((1,), (1,)), ((), ()))
    t_refs = (t0_ref, t1_ref)
    p_refs = (p0_ref, p1_ref)
    heads = range(HEAD_GROUP)
    n_pairs = nb * (nb + 1) // 2
    slopes = [slope_ref[g * HEAD_GROUP + u] for u in heads]

    pairs = [(j, iq) for j in range(nb) for iq in range(j, nb)]

    def scores(n, slot):
        j, iq = pairs[n]
        for u in heads:
            t_refs[slot][u] = lax.dot_general(k_ref[u, j * qb:(j + 1) * qb, :], qq_ref[u * nb + iq], dn,
                                              preferred_element_type=F32)

    def softmax_step(n, slot):
        j, iq = pairs[n]
        for u in heads:
            st = u * nb + iq
            t = t_refs[slot][u] + kbias_ref[u, int(j == iq)]
            off = slopes[u] * jnp.full((1, 2 * qb), float((j - iq) * qb), F32)
            m_blk = jnp.max(t, axis=0, keepdims=True) + off
            if j == 0:
                m_new = m_blk
            else:
                m = m_ref[st, 0:1, :]
                m_new = jnp.maximum(m, m_blk)
                al_ref[slot * HEAD_GROUP + u, 0:1, :] = jnp.exp2(m - m_new)
            p_refs[slot][u] = jnp.exp2(t + (off - m_new)).astype(BF16)
            m_ref[st, 0:1, :] = m_new

    def accumulate(n, slot):
        j, iq = pairs[n]
        for u in heads:
            st = u * nb + iq
            pv = jnp.dot(vt_ref[u, :, j * qb:(j + 1) * qb], p_refs[slot][u], preferred_element_type=F32)
            if j == 0:
                acc_ref[st] = pv
            else:
                acc_ref[st] = al_ref[slot * HEAD_GROUP + u, 0:1, :] * acc_ref[st] + pv

    for u in heads:
        for iq in range(nb):
            qq_ref[u * nb + iq] = _stack_maps(q_ref[u, iq * qb:(iq + 1) * qb, :])
    scores(0, 0)
    for n in range(n_pairs):
        slot = n % 2
        softmax_step(n, slot)
        if n > 0:
            accumulate(n - 1, 1 - slot)
        if n + 1 < n_pairs:
            scores(n + 1, 1 - slot)
    accumulate(n_pairs - 1, (n_pairs - 1) % 2)
    for u in heads:
        for iq in range(nb):
            inv = 1.0 / acc_ref[u * nb + iq, LANES:LANES + 1, :]
            acc = acc_ref[u * nb + iq, :LANES, :]
            ot = acc[:, :qb] * inv[:, :qb] - lam * (acc[:, qb:] * inv[:, qb:])
            ms = jnp.mean(ot * ot, axis=0, keepdims=True)
            ot = ot * lax.rsqrt(ms + SUBLN_EPS) * subln_ref[...] * (1.0 - lam_init)
            o_ref[u, iq * qb:(iq + 1) * qb, :] = ot.T.astype(o_ref.dtype)


def attn_b_prompt(qkvh, vt, lam, subln, lam_init, batch, t, qb):
    nblk = t // qb
    qq = np.arange(qb)
    slope2 = (_alibi(B_HEADS) * LOG2E)[:, None, None]
    kcol = np.broadcast_to(qq[:, None], (qb, qb)).astype(np.float64)
    tri = -2.0 * np.maximum(qq[:, None] - qq[None, :], 0)
    msk = np.where((qq[:, None] // CHUNK) <= (qq[None, :] // CHUNK), 0.0, -np.inf)
    both = lambda a: np.concatenate([a, a], axis=-1)
    kbias = np.stack([both(slope2 * kcol[None]), both(slope2 * (kcol + tri)[None] + msk[None])], axis=1)
    kbias = jnp.asarray(kbias, F32)
    slopes = jnp.asarray(_alibi(B_HEADS) * LOG2E, F32)
    subln_t = jnp.broadcast_to(subln.astype(F32).reshape(LANES, 1), (LANES, qb))
    hg = HEAD_GROUP
    n_groups = B_HEADS // hg
    grid_spec = pltpu.PrefetchScalarGridSpec(
        num_scalar_prefetch=2, grid=(batch, n_groups),
        in_specs=[
            pl.BlockSpec((hg, t, LANES), lambda b, g, *_: (g, b, 0)),
            pl.BlockSpec((hg, t, LANES), lambda b, g, *_: (n_groups + g, b, 0)),
            pl.BlockSpec((hg, LANES + ONES_ROWS, t), lambda b, g, *_: (g, 0, b)),
            pl.BlockSpec((hg,) + kbias.shape[1:], lambda b, g, *_: (g, 0, 0, 0)),
            pl.BlockSpec(subln_t.shape, lambda b, g, *_: (0, 0)),
        ],
        out_specs=pl.BlockSpec((hg, t, LANES), lambda b, g, *_: (g, b, 0)),
        scratch_shapes=[pltpu.VMEM((hg * nblk, LANES + ONES_ROWS, 2 * qb), F32),
                        pltpu.VMEM((hg * nblk, SUBLANES, 2 * qb), F32),
                        pltpu.VMEM((2 * hg, SUBLANES, 2 * qb), F32),
                        pltpu.VMEM((hg * nblk, 2 * qb, LANES), BF16),
                        pltpu.VMEM((hg, qb, 2 * qb), F32), pltpu.VMEM((hg, qb, 2 * qb), F32),
                        pltpu.VMEM((hg, qb, 2 * qb), BF16), pltpu.VMEM((hg, qb, 2 * qb), BF16)],
    )
    return pl.pallas_call(
        functools.partial(_attn_b_prompt_kernel, qb, nblk, lam_init), grid_spec=grid_spec,
        out_shape=jax.ShapeDtypeStruct((B_HEADS, batch * t, LANES), BF16),
        compiler_params=_cparams(("arbitrary", "arbitrary")), name="attn_b_prompt",
    )(slopes, lam.reshape(1).astype(F32), qkvh, qkvh, vt, kbias, subln_t)


def _attn_b_sample_kernel(dec_t, past, lam_init, slopes, lam_ref, q_ref, kn_ref, vn_ref, ck_ref, cv_ref,
                          tnew_ref, subln_ref, o_ref, tc_ref, tn_ref, pc_ref, pn_ref):
    lam = lam_ref[0]
    subln = subln_ref[...]
    colc = lax.broadcasted_iota(jnp.int32, (1, past), 1).astype(F32) - float(past)
    dn = (((1,), (1,)), ((), ()))
    for h in range(B_HEADS):
        qq = _stack_maps(q_ref[h])
        kc = ck_ref[pl.ds(h, past, stride=B_HEADS), :].astype(BF16)
        tc_ref[h] = lax.dot_general(qq, kc, dn, preferred_element_type=F32)
        tn_ref[h] = lax.dot_general(qq, kn_ref[h], dn, preferred_element_type=F32)
    sums = []
    for h in range(B_HEADS):
        slope = float(slopes[h]) * LOG2E
        s_c = tc_ref[h] + slope * colc
        s_n = tn_ref[h] + slope * tnew_ref[...]
        m = jnp.maximum(jnp.max(s_c, axis=-1, keepdims=True), jnp.max(s_n, axis=-1, keepdims=True))
        p_c = jnp.exp2(s_c - m)
        p_n = jnp.exp2(s_n - m)
        sums.append(jnp.sum(p_c, axis=-1, keepdims=True) + jnp.sum(p_n, axis=-1, keepdims=True))
        pc_ref[h] = p_c.astype(BF16)
        pn_ref[h] = p_n.astype(BF16)
    for h in range(B_HEADS):
        vc = cv_ref[pl.ds(h, past, stride=B_HEADS), :].astype(BF16)
        acc = (jnp.dot(pc_ref[h], vc, preferred_element_type=F32)
               + jnp.dot(pn_ref[h], vn_ref[h], preferred_element_type=F32))
        o_ref[h] = _diff_finish(acc, sums[h], dec_t, lam, lam_init, subln).astype(o_ref.dtype)


def attn_b_sample(qkvh, cache_k, cache_v, lam, subln, lam_init, row0, dec_batch, dec_t):
    past = cache_k.shape[1]
    qi = np.arange(dec_t)
    tnew = (qi[:, None] - np.abs(qi[:, None] - qi[None, :])).astype(np.float32)
    tnew = jnp.asarray(np.concatenate([tnew, tnew], axis=0))
    blk0 = row0 // dec_t
    rows = past * B_HEADS
    cache_k = cache_k.reshape(dec_batch, rows, LANES)
    cache_v = cache_v.reshape(dec_batch, rows, LANES)
    grid_spec = pltpu.PrefetchScalarGridSpec(
        num_scalar_prefetch=1, grid=(dec_batch,),
        in_specs=[
            pl.BlockSpec((B_HEADS, dec_t, LANES), lambda b, *_: (0, blk0 + b, 0)),
            pl.BlockSpec((B_HEADS, dec_t, LANES), lambda b, *_: (1, blk0 + b, 0)),
            pl.BlockSpec((B_HEADS, dec_t, LANES), lambda b, *_: (2, blk0 + b, 0)),
            pl.BlockSpec((None, rows, LANES), lambda b, *_: (b, 0, 0)),
            pl.BlockSpec((None, rows, LANES), lambda b, *_: (b, 0, 0)),
            pl.BlockSpec(tnew.shape, lambda b, *_: (0, 0)),
            pl.BlockSpec((1, LANES), lambda b, *_: (0, 0)),
        ],
        out_specs=pl.BlockSpec((B_HEADS, dec_t, LANES), lambda b, *_: (0, b, 0)),
        scratch_shapes=[pltpu.VMEM((B_HEADS, 2 * dec_t, past), F32), pltpu.VMEM((B_HEADS, 2 * dec_t, dec_t), F32),
                        pltpu.VMEM((B_HEADS, 2 * dec_t, past), BF16), pltpu.VMEM((B_HEADS, 2 * dec_t, dec_t), BF16)],
    )
    return pl.pallas_call(
        functools.partial(_attn_b_sample_kernel, dec_t, past, lam_init, tuple(_alibi(B_HEADS))),
        grid_spec=grid_spec,
        out_shape=jax.ShapeDtypeStruct((B_HEADS, dec_batch * dec_t, LANES), BF16),
        compiler_params=_cparams(("arbitrary",)), name="attn_b_sample",
    )(lam.reshape(1).astype(F32), qkvh, qkvh, qkvh, cache_k, cache_v, tnew, subln.reshape(1, LANES).astype(F32))


ROUTE_ROWS = 40
ROUTE_FIELDS = 8


def _route_t(lt):
    row = lax.broadcasted_iota(jnp.int32, lt.shape, 0).astype(F32)
    big = float(LANES)
    lg = jnp.where(row < N_GROUPS, lt, NEG_INF)
    gmax = jnp.max(lg, axis=0, keepdims=True)
    g_idx = jnp.min(jnp.where(lg == gmax, row, big), axis=0, keepdims=True)
    g_prob = 1.0 / jnp.sum(jnp.exp(lg - gmax), axis=0, keepdims=True)
    lo = N_GROUPS + EXPERTS_PER_GROUP * g_idx
    le = jnp.where((row >= lo) & (row < lo + EXPERTS_PER_GROUP), lt, NEG_INF)
    m1 = jnp.max(le, axis=0, keepdims=True)
    i1 = jnp.min(jnp.where(le == m1, row, big), axis=0, keepdims=True)
    le2 = jnp.where(row == i1, NEG_INF, le)
    m2 = jnp.max(le2, axis=0, keepdims=True)
    i2 = jnp.min(jnp.where(le2 == m2, row, big), axis=0, keepdims=True)
    t = jnp.exp(m2 - m1)
    ga = 1.0 / (1.0 + t)
    gb = t * ga
    return row, i1, i2, g_prob * ga, g_prob * gb


def _unpack_pairs(u):
    lo = pltpu.bitcast(u << 16, F32)
    hi = pltpu.bitcast(u & jnp.uint32(0xFFFF0000), F32)
    return jnp.concatenate([lo, hi], axis=1)


def _pack_pairs(x):
    bits = pltpu.bitcast(x.astype(BF16).astype(F32), jnp.uint32)
    w = x.shape[1] // 2
    return (bits[:, :w] >> 16) | bits[:, w:]


def _out_router_kernel(n_prompt_blocks, heads_in, two_o, two_res, *refs):
    refs = list(refs)
    oa_ref = refs.pop(0)
    ob_ref = refs.pop(0) if two_o else None
    ra_ref = refs.pop(0)
    rb_ref = refs.pop(0) if two_res else None
    (wo_ref, g_ref, wr_ref, wr2_ref, br_ref, utri_ref, h_ref, xn_ref, meta_ref, route_ref, cnt_ref,
     carry_ref) = refs
    is_prompt = pl.program_id(0) < n_prompt_blocks

    def load_o(ref):
        if heads_in:
            return jnp.concatenate([ref[hh] for hh in range(ref.shape[0])], axis=1)
        return ref[...]

    o = load_o(oa_ref)
    if two_o:
        o = jnp.where(is_prompt, o, load_o(ob_ref))
    res = ra_ref[...]
    if two_res:
        res = jnp.where(is_prompt, res, rb_ref[...])
    h = res + jnp.dot(o, wo_ref[...], preferred_element_type=F32)
    h_ref[...] = h
    ms = jnp.mean(h * h, axis=-1, keepdims=True)
    xn = h * lax.rsqrt(ms + NORM_EPS) * g_ref[...]
    x_hi = xn.astype(BF16)
    _store_planes(xn_ref, _pack_pairs(xn))
    x_lo = (xn - x_hi.astype(F32)).astype(BF16)
    r = jnp.dot(x_hi, wr_ref[...], preferred_element_type=F32)
    logits = (r[:, :LANES] + r[:, LANES:]) + jnp.dot(x_lo, wr2_ref[...], preferred_element_type=F32) + br_ref[...]
    row, i1, i2, g1, g2 = _route_t(logits.T[:ROUTE_ROWS])

    @pl.when(pl.program_id(0) == 0)
    def _():
        carry_ref[...] = jnp.zeros_like(carry_ref)

    sel1 = row == i1
    sel2 = row == i2
    onehot = jnp.where(sel1 | sel2, 1.0, 0.0)
    before = jnp.dot(onehot.astype(BF16), utri_ref[...], preferred_element_type=F32) + carry_ref[:, 0:1]
    rank1 = jnp.sum(jnp.where(sel1, before, 0.0), axis=0, keepdims=True)
    rank2 = jnp.sum(jnp.where(sel2, before, 0.0), axis=0, keepdims=True)
    carry_ref[...] = carry_ref[...] + jnp.sum(onehot, axis=1, keepdims=True)
    cnt_ref[...] = carry_ref[...]
    fields = [i1 - N_GROUPS, i2 - N_GROUPS, g1, g2, rank1, rank2]
    fields += [jnp.zeros_like(g1)] * (ROUTE_FIELDS - len(fields))
    route = jnp.concatenate(fields, axis=0)
    route_ref[...] = route
    meta_ref[...] = jnp.concatenate([route, jnp.zeros((LANES - ROUTE_FIELDS, route.shape[1]), F32)], axis=0).T


def out_router(oa, ob, ra, rb, wo, g, w_rg, b_rg, w_re, b_re, heads_in, tr):
    d = ra.shape[1]
    if heads_in:
        na = oa.shape[1]
        nb = 0 if ob is None else ob.shape[1]
    else:
        na = oa.shape[0]
        nb = 0 if ob is None else ob.shape[0]
    n = na + nb
    npb = na // tr
    two_o = ob is not None
    two_res = rb is not None
    first = lambda i: (jnp.minimum(i, npb - 1), 0)
    second = lambda i: (jnp.maximum(i - npb, 0), 0)
    plain = lambda i: (i, 0)
    in_specs, args = [], []

    def add_o(x, imap):
        if heads_in:
            in_specs.append(pl.BlockSpec((x.shape[0], tr, LANES), lambda i: (0, imap(i)[0], 0)))
        else:
            in_specs.append(pl.BlockSpec((tr, x.shape[1]), imap))
        args.append(x)

    add_o(oa, first if two_o else plain)
    if two_o:
        add_o(ob, second)
    in_specs.append(pl.BlockSpec((tr, d), first if two_res else plain))
    args.append(ra)
    if two_res:
        in_specs.append(pl.BlockSpec((tr, d), second))
        args.append(rb)
    wr = jnp.zeros((d, LANES), F32).at[:, :N_GROUPS].set(w_rg.astype(F32))
    wr = wr.at[:, N_GROUPS:N_GROUPS + N_EXPERTS].set(w_re.astype(F32))
    wr_hi = wr.astype(BF16)
    wr_lo = (wr - wr_hi.astype(F32)).astype(BF16)
    br = jnp.zeros((1, LANES), F32).at[0, :N_GROUPS].set(b_rg.astype(F32))
    br = br.at[0, N_GROUPS:N_GROUPS + N_EXPERTS].set(b_re.astype(F32))
    utri = jnp.asarray(np.triu(np.ones((tr, tr), np.float32), 1), BF16)
    consts = [wo, g.reshape(1, d).astype(F32), jnp.concatenate([wr_hi, wr_lo], axis=1), wr_hi, br, utri]
    for cst in consts:
        in_specs.append(pl.BlockSpec(cst.shape, lambda i: (0, 0)))
        args.append(cst)
    parts = d // 2 // SC_ROW_WORDS
    out_shape = [jax.ShapeDtypeStruct((n, d), F32), jax.ShapeDtypeStruct((parts, n, SC_ROW_WORDS), jnp.uint32),
                 jax.ShapeDtypeStruct((n, LANES), F32), jax.ShapeDtypeStruct((ROUTE_FIELDS, n), F32),
                 jax.ShapeDtypeStruct((ROUTE_ROWS, LANES), F32)]
    out_specs = [pl.BlockSpec((tr, d), plain), pl.BlockSpec((parts, tr, SC_ROW_WORDS), lambda i: (0, i, 0)),
                 pl.BlockSpec((tr, LANES), plain), pl.BlockSpec((ROUTE_FIELDS, tr), lambda i: (0, i)),
                 pl.BlockSpec((ROUTE_ROWS, LANES), lambda i: (0, 0))]
    return pl.pallas_call(
        functools.partial(_out_router_kernel, npb, heads_in, two_o, two_res),
        grid=(n // tr,), in_specs=in_specs, out_specs=out_specs, out_shape=out_shape,
        scratch_shapes=[pltpu.VMEM((ROUTE_ROWS, LANES), F32)],
        compiler_params=_cparams(("arbitrary",)), name="out_router")(*args)


def _expert_kernel(tm, tps, te_ref, nt_ref, x_ref, *refs):
    w_refs, o_ref = refs[:3 * tps], refs[3 * tps]
    first = pl.program_id(0) * tps
    n_valid = jnp.clip(nt_ref[0] - first, 0, tps)

    def tile(k):
        wg_ref, wu_ref, wd_ref = w_refs[3 * k:3 * k + 3]
        rows = slice(k * tm, (k + 1) * tm)
        x = _unpack_pairs(jnp.concatenate([x_ref[q, rows, :] for q in range(x_ref.shape[0])], axis=1)).astype(BF16)
        a = jnp.dot(x, wg_ref[...].astype(BF16), preferred_element_type=F32)
        u = jnp.dot(x, wu_ref[...].astype(BF16), preferred_element_type=F32)
        hh = (a * (1.0 / (1.0 + jnp.exp(-a))) * u).astype(BF16)
        packed = _pack_pairs(jnp.dot(hh, wd_ref[...].astype(BF16), preferred_element_type=F32))
        for q in range(o_ref.shape[0]):
            o_ref[q, rows, :] = packed[:, q * o_ref.shape[2]:(q + 1) * o_ref.shape[2]]

    for nv in range(tps + 1):

        @pl.when(n_valid == nv)
        def _(nv=nv):
            for k in range(nv):
                tile(k)
            if nv < tps:
                o_ref[:, nv * tm:, :] = jnp.zeros((o_ref.shape[0], (tps - nv) * tm, o_ref.shape[2]), o_ref.dtype)


def expert_mlp(xs, tile_expert, n_tiles, w_gate, w_up, w_down, layer, tm):
    parts, p, w = xs.shape
    d, f = w_gate.shape[-2:]
    epg = w_gate.shape[2]
    tps = 2 if (p // tm) % 2 == 0 else 1
    w_specs, w_args = [], []
    for k in range(tps):
        wmap = lambda t, te, nt, k=k: (layer, te[tps * t + k] // epg, te[tps * t + k] % epg, 0, 0)
        w_specs += [pl.BlockSpec((None, None, None, d, f), wmap), pl.BlockSpec((None, None, None, d, f), wmap),
                    pl.BlockSpec((None, None, None, f, d), wmap)]
        w_args += [w_gate, w_up, w_down]
    grid_spec = pltpu.PrefetchScalarGridSpec(
        num_scalar_prefetch=2, grid=(p // (tps * tm),),
        in_specs=[pl.BlockSpec((parts, tps * tm, w), lambda t, te, nt: (0, t, 0))] + w_specs,
        out_specs=pl.BlockSpec((parts, tps * tm, w), lambda t, te, nt: (0, t, 0)),
    )
    return pl.pallas_call(
        functools.partial(_expert_kernel, tm, tps), grid_spec=grid_spec,
        out_shape=jax.ShapeDtypeStruct((parts, p, w), jnp.uint32),
        compiler_params=_cparams(("arbitrary",)), name="expert_mlp")(tile_expert, n_tiles, xs, *w_args)


def sc_scatter_rows(x2, idx, n_out):
    mesh = plsc.VectorSubcoreMesh(core_axis_name="c", subcore_axis_name="s")
    n_src_win = x2.shape[0] // SC_WINDOW

    @functools.partial(pl.kernel, out_type=jax.ShapeDtypeStruct((n_out, SC_ROW_WORDS), x2.dtype), mesh=mesh)
    def scatter(x_hbm, i_hbm, o_hbm):
        def body(x_vmem, i_vmem):
            pltpu.sync_copy(x_vmem, o_hbm.at[i_vmem.at[0]])

        pltpu.emit_pipeline(
            body, grid=(n_out // SC_WINDOW,),
            in_specs=[pl.BlockSpec((SC_WINDOW, SC_ROW_WORDS), lambda i: (lax.rem(i, n_src_win), 0)),
                      pl.BlockSpec((1, SC_WINDOW), lambda i: (0, i))],
            out_specs=[], core_axis_name=("c", "s"), dimension_semantics=(pltpu.PARALLEL,),
        )(x_hbm, i_hbm)

    return scatter(x2, idx)


def sc_gather_rows(x2, idx):
    mesh = plsc.VectorSubcoreMesh(core_axis_name="c", subcore_axis_name="s")
    n_col, n = idx.shape

    @functools.partial(pl.kernel, out_type=jax.ShapeDtypeStruct((n, n_col * SC_ROW_WORDS), x2.dtype), mesh=mesh)
    def gather(x_hbm, i_hbm, o_hbm):
        def body(i_vmem, o_vmem):
            pltpu.sync_copy(x_hbm.at[i_vmem.at[0]], o_vmem)

        pltpu.emit_pipeline(
            body, grid=(n // SC_WINDOW, n_col),
            in_specs=[pl.BlockSpec((1, SC_WINDOW), lambda i, c: (c, i))],
            out_specs=[pl.BlockSpec((SC_WINDOW, SC_ROW_WORDS), lambda i, c: (i, c))],
            core_axis_name=("c", "s"), dimension_semantics=(pltpu.PARALLEL, pltpu.ARBITRARY),
        )(i_hbm, o_hbm)

    return gather(x2, idx)


def moe_dispatch(xn_packed, route, counts, tm):
    parts, n, w = xn_packed.shape
    e = route[0:2].astype(jnp.int32)
    rank = route[4:6].astype(jnp.int32)
    counts = counts[N_GROUPS:N_GROUPS + N_EXPERTS, 0].astype(jnp.int32)
    padded = ((counts + tm - 1) // tm) * tm
    pend = jnp.cumsum(padded)
    pstart = pend - padded
    ids = jnp.arange(N_EXPERTS, dtype=jnp.int32)
    pos = jnp.sum(jnp.where(e[..., None] == ids, pstart, 0), axis=-1) + rank
    p = 2 * n + N_EXPERTS * tm
    n_pad = p - 2 * n
    gap_len = jnp.concatenate([padded - counts, (p - pend[-1]).reshape(1)])
    gap_first = jnp.concatenate([pstart + counts, pend[-1:]])
    gap_end = jnp.cumsum(gap_len)
    k = jnp.arange(n_pad, dtype=jnp.int32)
    gap = jnp.sum((k[:, None] >= gap_end[None, :]).astype(jnp.int32), axis=1)
    shift = gap_first - (gap_end - gap_len)
    pad_pos = k + jnp.sum(jnp.where(gap[:, None] == jnp.arange(N_EXPERTS + 1), shift, 0), axis=-1)
    planes = lambda rows: jnp.concatenate([rows + q * p for q in range(parts)])
    scat_idx = jnp.concatenate([planes(pos[0]), planes(pos[1]), planes(pad_pos)]).reshape(1, parts * p)
    gath_idx = jnp.stack([pos[k] + q * p for k in range(2) for q in range(parts)])
    xs = sc_scatter_rows(xn_packed.reshape(parts * n, w), scat_idx, parts * p).reshape(parts, p, w)
    tile_start = jnp.arange(p // tm, dtype=jnp.int32) * tm
    tile_expert = jnp.sum((tile_start[:, None] >= pend[None, :]).astype(jnp.int32), axis=1)
    tile_expert = jnp.minimum(tile_expert, N_EXPERTS - 1)
    n_tiles = (pend[-1] // tm).astype(jnp.int32).reshape(1)
    return xs, tile_expert, n_tiles, gath_idx


def moe_experts(xn_packed, route, counts, w_gate, w_up, w_down, layer, tm):
    parts, n, w = xn_packed.shape
    xs, tile_expert, n_tiles, gath_idx = moe_dispatch(xn_packed, route, counts, tm)
    o = expert_mlp(xs, tile_expert, n_tiles, w_gate, w_up, w_down, layer, tm)
    return sc_gather_rows(o.reshape(parts * o.shape[1], w), gath_idx)


def _final_norm_kernel(n_prompt_blocks, h_ref, o0_ref, o1_ref, meta_ref, g_ref, yp_ref, ys_ref):
    meta = meta_ref[...]
    x = (h_ref[...] + meta[:, 2:3] * _unpack_pairs(o0_ref[...])) + meta[:, 3:4] * _unpack_pairs(o1_ref[...])
    ms = jnp.mean(x * x, axis=-1, keepdims=True)
    y = x * lax.rsqrt(ms + NORM_EPS) * g_ref[...]
    i = pl.program_id(0)

    @pl.when(i < n_prompt_blocks)
    def _():
        yp_ref[...] = y

    @pl.when(i >= n_prompt_blocks)
    def _():
        ys_ref[...] = y


def final_norm(h, og, meta, g, n_prompt, tr):
    n, d = h.shape
    npb = n_prompt // tr
    return pl.pallas_call(
        functools.partial(_final_norm_kernel, npb),
        grid=(n // tr,),
        in_specs=[pl.BlockSpec((tr, d), lambda i: (i, 0)),
                  pl.BlockSpec((tr, og.shape[1] // 2), lambda i: (i, 0)),
                  pl.BlockSpec((tr, og.shape[1] // 2), lambda i: (i, 1)),
                  pl.BlockSpec((tr, LANES), lambda i: (i, 0)), pl.BlockSpec((1, d), lambda i: (0, 0))],
        out_specs=[pl.BlockSpec((tr, d), lambda i: (jnp.minimum(i, npb - 1), 0)),
                   pl.BlockSpec((tr, d), lambda i: (jnp.maximum(i - npb, 0), 0))],
        out_shape=[jax.ShapeDtypeStruct((n_prompt, d), F32), jax.ShapeDtypeStruct((n - n_prompt, d), F32)],
        compiler_params=_cparams(("arbitrary",)), name="final_norm")(h, og, og, meta, g.reshape(1, d).astype(F32))


def _pick_rows_tile(n_prompt, n_sample):
    tr = ROW_TILE
    while tr >= SUBLANES:
        if n_prompt % tr == 0 and n_sample % tr == 0:
            return tr
        tr //= 2
    raise ValueError("row counts must be multiples of the sublane count")


def kernel(x_prompt, x_sample, cache_a_k, cache_a_v, cache_b_k, cache_b_v, norm_mix, norm_ffn, norm_final,
           w_a_qkv, b_a_qkv, a_sinks, w_a_o, w_b_qkv, b_lambda, b_subln, w_b_o,
           w_route_group, b_route_group, w_route_expert, b_route_expert, w_gate, w_up, w_down):
    batch, t, d = x_prompt.shape
    dec_batch, dec_t, _ = x_sample.shape
    n_p, n_s = batch * t, dec_batch * dec_t
    tr = _pick_rows_tile(n_p, n_s)
    tm = min(EXPERT_TILE, 4 * tr)
    xp = x_prompt.reshape(n_p, d)
    xs = x_sample.reshape(n_s, d)
    nq_a = A_HEADS * A_HD
    nkv_a = A_KV_HEADS * A_HD

    a_scale = jnp.where(jnp.arange(nq_a + 2 * nkv_a) < nq_a, A_HD ** -0.5 * LOG2E, 1.0).astype(F32)
    q_a, kv_ap, kv_as = norm_proj(xp, xs, norm_mix[0], (w_a_qkv[0] * a_scale).astype(BF16), b_a_qkv[0] * a_scale,
                                  [(0, nq_a, "flat", BF16), (nq_a, nq_a + 2 * nkv_a, "split", F32)], tr, n_p)
    qb_a = min(CHUNK * WIN_CHUNKS, t)
    o_ap = attn_a_prompt(q_a, kv_ap, a_sinks[0], batch, t, qb_a)
    kv_s = kv_as.reshape(dec_batch, dec_t, 2 * nkv_a)
    past_a = cache_a_k.shape[2]
    kband = jnp.concatenate([cache_a_k[0].reshape(dec_batch, past_a, nkv_a), kv_s[..., :nkv_a]], axis=1)
    vband = jnp.concatenate([cache_a_v[0].reshape(dec_batch, past_a, nkv_a), kv_s[..., nkv_a:]], axis=1)
    o_as = attn_a_sample(q_a, kband, vband, a_sinks[0], n_p, dec_batch, dec_t)
    h, xn, meta, route, counts = out_router(o_ap, o_as, xp, xs, w_a_o[0].astype(BF16), norm_ffn[0], w_route_group[0],
                                            b_route_group[0], w_route_expert[0], b_route_expert[0], False, tr)
    og = moe_experts(xn, route, counts, w_gate, w_up, w_down, 0, tm)

    keep = min(CHUNK * WIN_CHUNKS, t)
    kv_p = kv_ap.reshape(batch, t, 2 * nkv_a)[:, t - keep:]
    a_k_prompt = kv_p[..., :nkv_a].reshape(1, batch, keep, A_KV_HEADS, A_HD)
    a_v_prompt = kv_p[..., nkv_a:].reshape(1, batch, keep, A_KV_HEADS, A_HD)
    a_k_sample = kband[:, dec_t:].reshape(1, dec_batch, past_a, A_KV_HEADS, A_HD)
    a_v_sample = vband[:, dec_t:].reshape(1, dec_batch, past_a, A_KV_HEADS, A_HD)

    lam_init = 0.8 - 0.6 * float(np.exp(-0.3 * 1))
    lp = b_lambda[0].astype(F32)
    lam = jnp.exp(jnp.sum(lp[0] * lp[1])) - jnp.exp(jnp.sum(lp[2] * lp[3])) + lam_init
    wb = B_HEADS * 2 * B_HD
    col_scale = jnp.where(jnp.arange(3 * wb) < wb, B_Q_SCALE, 1.0).astype(F32)
    h, qkvh, k_bp, k_bs, v_bp, v_bs, vt = norm_proj(
        h, None, norm_mix[1], (w_b_qkv[0] * col_scale).astype(BF16), None,
        [(0, 3 * wb, "heads", BF16), (wb, 2 * wb, "split", F32), (2 * wb, 3 * wb, "split", F32),
         (2 * wb, 3 * wb, "heads_t", BF16)], tr, n_p, og=og, meta=meta)
    qb_b = min(ATTN_BLOCK, t)
    o_bp = attn_b_prompt(qkvh, vt, lam, b_subln[0], lam_init, batch, t, qb_b)
    o_bs = attn_b_sample(qkvh, cache_b_k[0], cache_b_v[0], lam, b_subln[0], lam_init, n_p, dec_batch, dec_t)
    hd2 = 2 * B_HD
    o_bs, k_bp, v_bp = lax.optimization_barrier((o_bs, k_bp, v_bp))
    b_k_prompt = k_bp.reshape(1, batch, t, B_HEADS, hd2)
    b_v_prompt = v_bp.reshape(1, batch, t, B_HEADS, hd2)
    o_bp, b_k_prompt, b_v_prompt = lax.optimization_barrier((o_bp, b_k_prompt, b_v_prompt))
    h, xn, meta, route, counts = out_router(o_bp, o_bs, h, None, w_b_o[0].astype(BF16), norm_ffn[1], w_route_group[1],
                                            b_route_group[1], w_route_expert[1], b_route_expert[1], True, tr)
    og = moe_experts(xn, route, counts, w_gate, w_up, w_down, 1, tm)

    y_p, y_s = final_norm(h, og, meta, norm_final, n_p, tr)
    return (y_p.reshape(batch, t, d), y_s.reshape(dec_batch, dec_t, d),
            a_k_prompt, a_v_prompt, a_k_sample, a_v_sample, b_k_prompt, b_v_prompt,
            k_bs.reshape(1, dec_batch, dec_t, B_HEADS, hd2), v_bs.reshape(1, dec_batch, dec_t, B_HEADS, hd2))
```
